```python
import jax, jax.numpy as jnp
from jax import lax
import numpy as np

D_MODEL = 1024
BATCH = 8
SEQ = 4096
DEPTH = 2

CHUNK = 128
N_SGU_GROUPS = 8
SGU_WIDTH = D_MODEL
SGU_GROUP_DIM = SGU_WIDTH // N_SGU_GROUPS
N_HEADS = 8
HEAD_DIM = 128
ATTN_WIDTH = N_HEADS * HEAD_DIM
Q_BLOCK = 128
D_FF = -(-8 * D_MODEL // (3 * 256)) * 256
EPS = 1e-6

_IN_SIZES = (SGU_WIDTH, SGU_WIDTH, ATTN_WIDTH, ATTN_WIDTH, ATTN_WIDTH, D_MODEL, D_MODEL, N_HEADS)
IN_WIDTH = sum(_IN_SIZES)
IN_SPLITS = tuple(int(s) for s in np.cumsum(_IN_SIZES)[:-1])

kernel_name = "hybrid_gmlp_fox_gated_block"


def rmsnorm(x, g):
    xf = x.astype(jnp.float32)
    r = lax.rsqrt(jnp.mean(xf * xf, axis=-1, keepdims=True) + EPS)
    return (xf * r * g.astype(jnp.float32)).astype(x.dtype)


def layernorm(x, g):
    xf = x.astype(jnp.float32)
    mu = jnp.mean(xf, axis=-1, keepdims=True)
    xc = xf - mu
    r = lax.rsqrt(jnp.mean(xc * xc, axis=-1, keepdims=True) + EPS)
    return (xc * r * g.astype(jnp.float32)).astype(x.dtype)


def spatial_gating(u, v, w_s, b_s, g_v):
    bsz, seq = v.shape[0], v.shape[1]
    v = layernorm(v, g_v)
    vc = v.reshape(bsz, seq // CHUNK, CHUNK, N_SGU_GROUPS, SGU_GROUP_DIM)
    causal = jnp.tril(jnp.ones((CHUNK, CHUNK), dtype=bool))
    w = jnp.where(causal[None], w_s, jnp.zeros_like(w_s))
    mixed = jnp.einsum('gts,bcsgd->bctgd', w, vc) + b_s.T[None, None, :, :, None]
    return u * mixed.reshape(bsz, seq, SGU_WIDTH)


def forgetting_attention(q, k, v, f_logit, b_f):
    bsz, seq = q.shape[0], q.shape[1]

    def heads(t):
        return t.reshape(bsz, seq, N_HEADS, HEAD_DIM).transpose(0, 2, 1, 3)

    q = heads(q) * (HEAD_DIM ** -0.5)
    k = heads(k)
    v = heads(v)
    log_f = jax.nn.log_sigmoid((f_logit + b_f).astype(jnp.float32))
    c = jnp.cumsum(log_f, axis=1).transpose(0, 2, 1)
    k_pos = jnp.arange(seq)

    def block(i):
        start = i * Q_BLOCK
        qi = lax.dynamic_slice_in_dim(q, start, Q_BLOCK, axis=2)
        ci = lax.dynamic_slice_in_dim(c, start, Q_BLOCK, axis=2)
        s = jnp.einsum('bhqd,bhkd->bhqk', qi, k).astype(jnp.float32)
        s = s + ci[..., None] - c[:, :, None, :]
        q_pos = start + jnp.arange(Q_BLOCK)
        s = jnp.where(k_pos[None, :] <= q_pos[:, None], s, -jnp.inf)
        p = jax.nn.softmax(s, axis=-1).astype(v.dtype)
        return jnp.einsum('bhqk,bhkd->bhqd', p, v)

    out = lax.map(block, jnp.arange(seq // Q_BLOCK))
    return out.transpose(1, 0, 3, 2, 4).reshape(bsz, seq, ATTN_WIDTH)


def swiglu(h, w_gate, w_up, w_down):
    return (jax.nn.silu(h @ w_gate) * (h @ w_up)) @ w_down


def _fwd_setup_inputs(seed: int = 0) -> dict:
    key = jax.random.key(seed)
    ks = jax.random.split(key, 15)
    f32 = jnp.float32

    def nrm(k, shape, scale):
        return jax.random.normal(k, shape, f32) * scale

    def gain(k, shape):
        return 1.0 + 0.1 * jax.random.normal(k, shape, f32)

    x = jax.random.normal(ks[0], (BATCH, SEQ, D_MODEL), f32)
    mix_pre_g = gain(ks[1], (DEPTH, D_MODEL))
    w_in = nrm(ks[2], (DEPTH, D_MODEL, IN_WIDTH), D_MODEL ** -0.5)
    b_forget = jnp.linspace(1.0, 6.0, N_HEADS, dtype=f32)[None, :] + 0.1 * jax.random.normal(ks[3], (DEPTH, N_HEADS), f32)
    sgu_norm_g = gain(ks[4], (DEPTH, SGU_WIDTH))
    w_spatial = nrm(ks[5], (DEPTH, N_SGU_GROUPS, CHUNK, CHUNK), CHUNK ** -0.5)
    b_spatial = 1.0 + 0.1 * jax.random.normal(ks[6], (DEPTH, N_SGU_GROUPS, CHUNK), f32)
    w_out = nrm(ks[7], (DEPTH, D_MODEL, D_MODEL), D_MODEL ** -0.5)
    mix_post_g = gain(ks[8], (DEPTH, D_MODEL))
    ffn_pre_g = gain(ks[9], (DEPTH, D_MODEL))
    w_gate = nrm(ks[10], (DEPTH, D_MODEL, D_FF), D_MODEL ** -0.5)
    w_up = nrm(ks[11], (DEPTH, D_MODEL, D_FF), D_MODEL ** -0.5)
    w_down = nrm(ks[12], (DEPTH, D_FF, D_MODEL), D_FF ** -0.5)
    ffn_post_g = gain(ks[13], (DEPTH, D_MODEL))
    return {"x": x, "mix_pre_g": mix_pre_g, "w_in": w_in, "b_forget": b_forget,
            "sgu_norm_g": sgu_norm_g, "w_spatial": w_spatial, "b_spatial": b_spatial,
            "w_out": w_out, "mix_post_g": mix_post_g, "ffn_pre_g": ffn_pre_g,
            "w_gate": w_gate, "w_up": w_up, "w_down": w_down, "ffn_post_g": ffn_post_g}


def _fwd_reference(x, mix_pre_g, w_in, b_forget, sgu_norm_g, w_spatial, b_spatial, w_out,
              mix_post_g, ffn_pre_g, w_gate, w_up, w_down, ffn_post_g):
    for l in range(DEPTH):
        h = rmsnorm(x, mix_pre_g[l])
        proj = h @ w_in[l]
        u, v_s, q, k, v_a, g_a, g_b, f_logit = jnp.split(proj, IN_SPLITS, axis=-1)
        y_a = spatial_gating(jax.nn.gelu(u), jax.nn.gelu(v_s), w_spatial[l], b_spatial[l], sgu_norm_g[l])
        y_b = forgetting_attention(q, k, v_a, f_logit, b_forget[l])
        merged = jax.nn.sigmoid(g_a) * y_a + jax.nn.sigmoid(g_b) * y_b
        x = x + rmsnorm(merged @ w_out[l], mix_post_g[l])
        h = rmsnorm(x, ffn_pre_g[l])
        x = x + rmsnorm(swiglu(h, w_gate[l], w_up[l], w_down[l]), ffn_post_g[l])
    return x


import jax as _jax
import jax.numpy as _jnp

TWIN_FORMAT = 'train_step'
FWD_PARAMS = ['x', 'mix_pre_g', 'w_in', 'b_forget', 'sgu_norm_g', 'w_spatial', 'b_spatial', 'w_out', 'mix_post_g', 'ffn_pre_g', 'w_gate', 'w_up', 'w_down', 'ffn_post_g']
TWIN_WEIGHTS = ['mix_pre_g', 'w_in', 'b_forget', 'sgu_norm_g', 'w_spatial', 'b_spatial', 'w_out', 'mix_post_g', 'ffn_pre_g', 'w_gate', 'w_up', 'w_down', 'ffn_post_g']
TWIN_DIFF_INPUT = 'x'
TWIN_INPUTS = ['x', 'mix_pre_g', 'w_in', 'b_forget', 'sgu_norm_g', 'w_spatial', 'b_spatial', 'w_out', 'mix_post_g', 'ffn_pre_g', 'w_gate', 'w_up', 'w_down', 'ffn_post_g', 'loss_target', 'm_mix_pre_g', 'm_w_in', 'm_b_forget', 'm_sgu_norm_g', 'm_w_spatial', 'm_b_spatial', 'm_w_out', 'm_mix_post_g', 'm_ffn_pre_g', 'm_w_gate', 'm_w_up', 'm_w_down', 'm_ffn_post_g', 'v_mix_pre_g', 'v_w_in', 'v_b_forget', 'v_sgu_norm_g', 'v_w_spatial', 'v_b_spatial', 'v_w_out', 'v_mix_post_g', 'v_ffn_pre_g', 'v_w_gate', 'v_w_up', 'v_w_down', 'v_ffn_post_g']
TWIN_OUTPUTS = ['loss', 'grad_x', 'grad_mix_pre_g', 'grad_w_in', 'grad_b_forget', 'grad_sgu_norm_g', 'grad_w_spatial', 'grad_b_spatial', 'grad_w_out', 'grad_mix_post_g', 'grad_ffn_pre_g', 'grad_w_gate', 'grad_w_up', 'grad_w_down', 'grad_ffn_post_g', 'delta_mix_pre_g', 'delta_w_in', 'delta_b_forget', 'delta_sgu_norm_g', 'delta_w_spatial', 'delta_b_spatial', 'delta_w_out', 'delta_mix_post_g', 'delta_ffn_pre_g', 'delta_w_gate', 'delta_w_up', 'delta_w_down', 'delta_ffn_post_g', 'new_m_mix_pre_g', 'new_m_w_in', 'new_m_b_forget', 'new_m_sgu_norm_g', 'new_m_w_spatial', 'new_m_b_spatial', 'new_m_w_out', 'new_m_mix_post_g', 'new_m_ffn_pre_g', 'new_m_w_gate', 'new_m_w_up', 'new_m_w_down', 'new_m_ffn_post_g', 'new_v_mix_pre_g', 'new_v_w_in', 'new_v_b_forget', 'new_v_sgu_norm_g', 'new_v_w_spatial', 'new_v_b_spatial', 'new_v_w_out', 'new_v_mix_post_g', 'new_v_ffn_pre_g', 'new_v_w_gate', 'new_v_w_up', 'new_v_w_down', 'new_v_ffn_post_g']
TWIN_LEAF_KINDS = {'loss': 'loss', 'grad_x': 'grad_x', 'grad_mix_pre_g': 'grad_w', 'grad_w_in': 'grad_w', 'grad_b_forget': 'grad_w', 'grad_sgu_norm_g': 'grad_w', 'grad_w_spatial': 'grad_w', 'grad_b_spatial': 'grad_w', 'grad_w_out': 'grad_w', 'grad_mix_post_g': 'grad_w', 'grad_ffn_pre_g': 'grad_w', 'grad_w_gate': 'grad_w', 'grad_w_up': 'grad_w', 'grad_w_down': 'grad_w', 'grad_ffn_post_g': 'grad_w', 'delta_mix_pre_g': 'delta_w', 'delta_w_in': 'delta_w', 'delta_b_forget': 'delta_w', 'delta_sgu_norm_g': 'delta_w', 'delta_w_spatial': 'delta_w', 'delta_b_spatial': 'delta_w', 'delta_w_out': 'delta_w', 'delta_mix_post_g': 'delta_w', 'delta_ffn_pre_g': 'delta_w', 'delta_w_gate': 'delta_w', 'delta_w_up': 'delta_w', 'delta_w_down': 'delta_w', 'delta_ffn_post_g': 'delta_w', 'new_m_mix_pre_g': 'new_m', 'new_m_w_in': 'new_m', 'new_m_b_forget': 'new_m', 'new_m_sgu_norm_g': 'new_m', 'new_m_w_spatial': 'new_m', 'new_m_b_spatial': 'new_m', 'new_m_w_out': 'new_m', 'new_m_mix_post_g': 'new_m', 'new_m_ffn_pre_g': 'new_m', 'new_m_w_gate': 'new_m', 'new_m_w_up': 'new_m', 'new_m_w_down': 'new_m', 'new_m_ffn_post_g': 'new_m', 'new_v_mix_pre_g': 'new_v', 'new_v_w_in': 'new_v', 'new_v_b_forget': 'new_v', 'new_v_sgu_norm_g': 'new_v', 'new_v_w_spatial': 'new_v', 'new_v_b_spatial': 'new_v', 'new_v_w_out': 'new_v', 'new_v_mix_post_g': 'new_v', 'new_v_ffn_pre_g': 'new_v', 'new_v_w_gate': 'new_v', 'new_v_w_up': 'new_v', 'new_v_w_down': 'new_v', 'new_v_ffn_post_g': 'new_v'}


def _forward(args):
    return _fwd_reference(*[args[k] for k in FWD_PARAMS])


def _output_shape():
    out = _jax.eval_shape(lambda: _forward(_fwd_setup_inputs(0)))
    return out.shape, out.dtype

N_MICROBATCH = 1
ADAM_LR = 0.001
ADAM_B1 = 0.9
ADAM_B2 = 0.999
ADAM_EPS = 1e-08
ADAM_WD = 0.01
ADAM_STEP = 10
PER_EXAMPLE_BATCH_AXIS = {'x': 0, 'loss_target': 0}
SHARED_INPUTS = []
_WEIGHT_DTYPES = {'mix_pre_g': _jnp.float32, 'w_in': _jnp.float32, 'b_forget': _jnp.float32, 'sgu_norm_g': _jnp.float32, 'w_spatial': _jnp.float32, 'b_spatial': _jnp.float32, 'w_out': _jnp.float32, 'mix_post_g': _jnp.float32, 'ffn_pre_g': _jnp.float32, 'w_gate': _jnp.float32, 'w_up': _jnp.float32, 'w_down': _jnp.float32, 'ffn_post_g': _jnp.float32}
MOMENT_SCALE = {'mix_pre_g': 1.598887e+00, 'w_in': 6.130703e-01, 'b_forget': 2.108447e+00, 'sgu_norm_g': 3.895618e-01, 'w_spatial': 3.703607e-01, 'b_spatial': 5.266255e-01, 'w_out': 4.277333e+00, 'mix_post_g': 3.241670e+01, 'ffn_pre_g': 1.732037e+00, 'w_gate': 5.152966e-01, 'w_up': 8.478830e-01, 'w_down': 1.450175e+00, 'ffn_post_g': 3.219390e+01}


def _to_microbatches(a, axis):
    t = _jnp.moveaxis(a, axis, 0)
    t = t.reshape((N_MICROBATCH, t.shape[0] // N_MICROBATCH) + t.shape[1:])
    return _jnp.moveaxis(t, 1, axis + 1)


def setup_inputs(seed: int = 0) -> dict:
    inp = _fwd_setup_inputs(seed)
    key = _jax.random.fold_in(_jax.random.key(seed), 7919)
    shape, _ = _output_shape()
    out = dict(inp)
    out["loss_target"] = _jax.random.normal(_jax.random.fold_in(key, 0), shape, _jnp.float32)
    for i, name in enumerate(TWIN_WEIGHTS):
        w = inp[name].astype(_jnp.float32)
        if MOMENT_SCALE is None:
            s = _jnp.sqrt(_jnp.mean(_jnp.square(w)) + 1e-30)
        else:
            s = MOMENT_SCALE[name]
        km, kv = _jax.random.split(_jax.random.fold_in(key, i + 1))
        out[name] = w
        out["m_" + name] = s * _jax.random.normal(km, w.shape, _jnp.float32)
        out["v_" + name] = (s * s) * _jax.random.uniform(kv, w.shape, _jnp.float32, 0.5, 1.5)
    if N_MICROBATCH > 1:
        for name, axis in PER_EXAMPLE_BATCH_AXIS.items():
            out[name] = _to_microbatches(out[name], axis)
    return {'x': out['x'], 'mix_pre_g': out['mix_pre_g'], 'w_in': out['w_in'], 'b_forget': out['b_forget'], 'sgu_norm_g': out['sgu_norm_g'], 'w_spatial': out['w_spatial'], 'b_spatial': out['b_spatial'], 'w_out': out['w_out'], 'mix_post_g': out['mix_post_g'], 'ffn_pre_g': out['ffn_pre_g'], 'w_gate': out['w_gate'], 'w_up': out['w_up'], 'w_down': out['w_down'], 'ffn_post_g': out['ffn_post_g'], 'loss_target': out['loss_target'], 'm_mix_pre_g': out['m_mix_pre_g'], 'm_w_in': out['m_w_in'], 'm_b_forget': out['m_b_forget'], 'm_sgu_norm_g': out['m_sgu_norm_g'], 'm_w_spatial': out['m_w_spatial'], 'm_b_spatial': out['m_b_spatial'], 'm_w_out': out['m_w_out'], 'm_mix_post_g': out['m_mix_post_g'], 'm_ffn_pre_g': out['m_ffn_pre_g'], 'm_w_gate': out['m_w_gate'], 'm_w_up': out['m_w_up'], 'm_w_down': out['m_w_down'], 'm_ffn_post_g': out['m_ffn_post_g'], 'v_mix_pre_g': out['v_mix_pre_g'], 'v_w_in': out['v_w_in'], 'v_b_forget': out['v_b_forget'], 'v_sgu_norm_g': out['v_sgu_norm_g'], 'v_w_spatial': out['v_w_spatial'], 'v_b_spatial': out['v_b_spatial'], 'v_w_out': out['v_w_out'], 'v_mix_post_g': out['v_mix_post_g'], 'v_ffn_pre_g': out['v_ffn_pre_g'], 'v_w_gate': out['v_w_gate'], 'v_w_up': out['v_w_up'], 'v_w_down': out['v_w_down'], 'v_ffn_post_g': out['v_ffn_post_g']}


def _loss(weights, diff, rest, loss_target):
    with _jax.named_scope("forward"):
        args = {**rest, TWIN_DIFF_INPUT: diff, **{k: w.astype(_WEIGHT_DTYPES[k]) for k, w in weights.items()}}
        y = _forward(args)
    with _jax.named_scope("loss_head"):
        err = _jnp.square(y.astype(_jnp.float32) - loss_target)
        return 0.5 * _jnp.sum(_jnp.mean(err, axis=-1)) if err.ndim else 0.5 * err


def _adamw(w, g, m, v):
    m = ADAM_B1 * m + (1.0 - ADAM_B1) * g
    v = ADAM_B2 * v + (1.0 - ADAM_B2) * _jnp.square(g)
    m_hat = m / (1.0 - ADAM_B1 ** ADAM_STEP)
    v_hat = v / (1.0 - ADAM_B2 ** ADAM_STEP)
    delta = -ADAM_LR * (m_hat / (_jnp.sqrt(v_hat) + ADAM_EPS) + ADAM_WD * w)
    return delta, m, v


def reference(x, mix_pre_g, w_in, b_forget, sgu_norm_g, w_spatial, b_spatial, w_out, mix_post_g, ffn_pre_g, w_gate, w_up, w_down, ffn_post_g, loss_target, m_mix_pre_g, m_w_in, m_b_forget, m_sgu_norm_g, m_w_spatial, m_b_spatial, m_w_out, m_mix_post_g, m_ffn_pre_g, m_w_gate, m_w_up, m_w_down, m_ffn_post_g, v_mix_pre_g, v_w_in, v_b_forget, v_sgu_norm_g, v_w_spatial, v_b_spatial, v_w_out, v_mix_post_g, v_ffn_pre_g, v_w_gate, v_w_up, v_w_down, v_ffn_post_g):
    given = dict(x=x, mix_pre_g=mix_pre_g, w_in=w_in, b_forget=b_forget, sgu_norm_g=sgu_norm_g, w_spatial=w_spatial, b_spatial=b_spatial, w_out=w_out, mix_post_g=mix_post_g, ffn_pre_g=ffn_pre_g, w_gate=w_gate, w_up=w_up, w_down=w_down, ffn_post_g=ffn_post_g, loss_target=loss_target, m_mix_pre_g=m_mix_pre_g, m_w_in=m_w_in, m_b_forget=m_b_forget, m_sgu_norm_g=m_sgu_norm_g, m_w_spatial=m_w_spatial, m_b_spatial=m_b_spatial, m_w_out=m_w_out, m_mix_post_g=m_mix_post_g, m_ffn_pre_g=m_ffn_pre_g, m_w_gate=m_w_gate, m_w_up=m_w_up, m_w_down=m_w_down, m_ffn_post_g=m_ffn_post_g, v_mix_pre_g=v_mix_pre_g, v_w_in=v_w_in, v_b_forget=v_b_forget, v_sgu_norm_g=v_sgu_norm_g, v_w_spatial=v_w_spatial, v_b_spatial=v_b_spatial, v_w_out=v_w_out, v_mix_post_g=v_mix_post_g, v_ffn_pre_g=v_ffn_pre_g, v_w_gate=v_w_gate, v_w_up=v_w_up, v_w_down=v_w_down, v_ffn_post_g=v_ffn_post_g)
    weights = {n: given[n] for n in TWIN_WEIGHTS}
    shared = {n: given[n] for n in SHARED_INPUTS}
    per_example = {n: given[n] for n in ['x']}
    grad_fn = _jax.value_and_grad(_loss, argnums=(0, 1))

    def one_microbatch(ex, loss_target):
        ex = dict(ex)
        diff = ex.pop(TWIN_DIFF_INPUT)
        return grad_fn(weights, diff, {**shared, **ex}, loss_target)

    if N_MICROBATCH == 1:
        loss, (grad_w, grad_x) = one_microbatch(per_example, given["loss_target"])
    else:
        def body(carry, xs):
            loss_sum, grad_sum = carry
            l_k, (gw_k, gx_k) = one_microbatch(xs[0], xs[1])
            with _jax.named_scope("update"):
                return (loss_sum + l_k, _jax.tree.map(_jnp.add, grad_sum, gw_k)), gx_k

        init = (_jnp.zeros((), _jnp.float32), _jax.tree.map(_jnp.zeros_like, weights))
        (loss, grad_w), grad_x = _jax.lax.scan(body, init, (per_example, given["loss_target"]))
    with _jax.named_scope("update"):
        delta_w, new_m, new_v = {}, {}, {}
        for n in TWIN_WEIGHTS:
            delta_w[n], new_m[n], new_v[n] = _adamw(weights[n], grad_w[n], given["m_" + n], given["v_" + n])
    return (loss, grad_x, *[grad_w[n] for n in TWIN_WEIGHTS], *[delta_w[n] for n in TWIN_WEIGHTS],
            *[new_m[n] for n in TWIN_WEIGHTS], *[new_v[n] for n in TWIN_WEIGHTS])
```

```python
import functools
import math

import jax
import jax.numpy as jnp
from jax import lax
from jax.experimental import pallas as pl
from jax.experimental.pallas import tpu as pltpu

F32 = jnp.float32
BF16 = jnp.bfloat16

EPS = 1e-6
LANE = 128
SUBLANE = 8
N_CHIPS = 4
N_DEV = 8
VMEM_LIMIT = 48 * 1024 * 1024
MESH = pl.DeviceIdType.MESH

ADAM_LR = 0.001
ADAM_B1 = 0.9
ADAM_B2 = 0.999
ADAM_EPS = 1e-08
ADAM_WD = 0.01
ADAM_STEP = 10

GELU_K = math.sqrt(2.0 / math.pi)
GELU_A = 0.044715
NEG = -1e30


def _cparams(sem=None):
    return pltpu.CompilerParams(dimension_semantics=sem, vmem_limit_bytes=VMEM_LIMIT)


def _tile(n, cap):
    best = None
    for t in range(LANE, min(n, cap) + 1, LANE):
        if n % t == 0:
            best = t
    return best if best is not None else n


def _rows(n, cap):
    best = None
    for t in range(SUBLANE, min(n, cap) + 1, SUBLANE):
        if n % t == 0:
            best = t
    return best if best is not None else n


def _gelu(x):
    t = jnp.tanh(GELU_K * (x + GELU_A * x * x * x))
    return 0.5 * x * (1.0 + t)


def _gelu_and_grad(x):
    x2 = x * x
    t = jnp.tanh(GELU_K * (x + GELU_A * x2 * x))
    g = 0.5 * x * (1.0 + t)
    dg = 0.5 * (1.0 + t) + 0.5 * x * (1.0 - t * t) * (GELU_K * (1.0 + 3.0 * GELU_A * x2))
    return g, dg


def _sigmoid(x):
    return 1.0 / (1.0 + jnp.exp(-x))


def _sum8(v):
    n, d = v.shape
    return v.reshape(n // SUBLANE, SUBLANE, d).sum(axis=0)


_DIMS = {"nn": ((1,), (0,)), "nt": ((1,), (1,)), "tn": ((0,), (0,))}


def _matmul(a, b, mode, out_dtype, name, tm_cap=512, tn_cap=2432, tk_cap=1408):
    if mode == "nn":
        (m, k), (k2, n) = a.shape, b.shape
    elif mode == "nt":
        (m, k), (n, k2) = a.shape, b.shape
    else:
        (k, m), (k2, n) = a.shape, b.shape
    assert k == k2, (a.shape, b.shape, mode)
    tm, tn, tk = _tile(m, tm_cap), _tile(n, tn_cap), _tile(k, tk_cap)
    nk = k // tk
    if mode == "tn":
        a_spec = pl.BlockSpec((tk, tm), lambda j, i, kk: (kk, i))
    else:
        a_spec = pl.BlockSpec((tm, tk), lambda j, i, kk: (i, kk))
    if mode == "nt":
        b_spec = pl.BlockSpec((tn, tk), lambda j, i, kk: (j, kk))
    else:
        b_spec = pl.BlockSpec((tk, tn), lambda j, i, kk: (kk, j))
    dims = (_DIMS[mode], ((), ()))

    def body(a_ref, b_ref, o_ref, *scratch):
        p = lax.dot_general(a_ref[...], b_ref[...], dims, preferred_element_type=F32)
        if nk == 1:
            o_ref[...] = p.astype(out_dtype)
        else:
            acc = scratch[0]
            kk = pl.program_id(2)

            @pl.when(kk == 0)
            def _():
                acc[...] = p

            @pl.when(kk > 0)
            def _():
                acc[...] += p

            @pl.when(kk == nk - 1)
            def _():
                o_ref[...] = acc[...].astype(out_dtype)

    return pl.pallas_call(
        body,
        grid=(n // tn, m // tm, nk),
        in_specs=[a_spec, b_spec],
        out_specs=pl.BlockSpec((tm, tn), lambda j, i, kk: (i, j)),
        out_shape=jax.ShapeDtypeStruct((m, n), out_dtype),
        scratch_shapes=[pltpu.VMEM((tm, tn), F32)] if nk > 1 else [],
        compiler_params=_cparams(("parallel", "parallel", "arbitrary")),
        name=name,
    )(a, b)


def _norm_fwd(x, z, g_post, g_next, name):
    t, d = x.shape
    tt = _rows(t, 512)
    row = pl.BlockSpec((tt, d), lambda i: (i, 0))
    vec = pl.BlockSpec((1, d), lambda i: (0, 0))

    def body(*refs):
        if z is None:
            x_ref, gn_ref, h_ref = refs
            xn = x_ref[...]
        else:
            x_ref, z_ref, gp_ref, gn_ref, xo_ref, h_ref = refs
            zz = z_ref[...]
            r = lax.rsqrt(jnp.mean(zz * zz, axis=-1, keepdims=True) + EPS)
            xn = x_ref[...] + zz * r * gp_ref[...]
            xo_ref[...] = xn
        r2 = lax.rsqrt(jnp.mean(xn * xn, axis=-1, keepdims=True) + EPS)
        h_ref[...] = (xn * r2 * gn_ref[...]).astype(BF16)

    if z is None:
        return pl.pallas_call(
            body, grid=(t // tt,), in_specs=[row, vec], out_specs=row,
            out_shape=jax.ShapeDtypeStruct((t, d), BF16), compiler_params=_cparams(("parallel",)), name=name,
        )(x, g_next)
    return pl.pallas_call(
        body, grid=(t // tt,), in_specs=[row, row, vec, vec], out_specs=[row, row],
        out_shape=[jax.ShapeDtypeStruct((t, d), F32), jax.ShapeDtypeStruct((t, d), BF16)],
        compiler_params=_cparams(("parallel",)), name=name,
    )(x, z, g_post, g_next)


def _rms_bwd(dy, x, g):
    r = lax.rsqrt(jnp.mean(x * x, axis=-1, keepdims=True) + EPS)
    n = x * r
    dn = dy * g
    dx = r * (dn - n * jnp.mean(dn * n, axis=-1, keepdims=True))
    return dx, dy * n


def _norm_bwd(dres, pre, post, name):
    t, d = dres.shape
    tt = _rows(t, 512)
    nt = t // tt
    row = pl.BlockSpec((tt, d), lambda i: (i, 0))
    vec = pl.BlockSpec((1, d), lambda i: (0, 0))
    has_pre, has_post = pre is not None, post is not None
    n_in = 1 + (3 if has_pre else 0) + (2 if has_post else 0)
    n_out = has_pre + has_post + has_pre + has_post

    def body(*refs):
        ins, outs, scr = refs[:n_in], refs[n_in:n_in + n_out], refs[n_in + n_out:]
        i = pl.program_id(0)
        dx = ins[0][...]
        pos, opos, spos = 1, 0, 0
        row_outs, accs, vec_outs = [], [], []
        if has_pre:
            dh_ref, xa_ref, ga_ref = ins[pos:pos + 3]
            pos += 3
            dxa, dga_t = _rms_bwd(dh_ref[...], xa_ref[...], ga_ref[...])
            dx = dx + dxa
            outs[opos][...] = dx
            opos += 1
            accs.append((scr[spos], dga_t))
            spos += 1
        if has_post:
            zb_ref, gb_ref = ins[pos:pos + 2]
            dz, dgb_t = _rms_bwd(dx, zb_ref[...], gb_ref[...])
            outs[opos][...] = dz.astype(BF16)
            opos += 1
            accs.append((scr[spos], dgb_t))
            spos += 1
        for (acc, val), out in zip(accs, outs[opos:]):
            part = _sum8(val)

            @pl.when(i == 0)
            def _(acc=acc, part=part):
                acc[...] = part

            @pl.when(i > 0)
            def _(acc=acc, part=part):
                acc[...] += part

            @pl.when(i == nt - 1)
            def _(acc=acc, out=out):
                out[...] = jnp.sum(acc[...], axis=0, keepdims=True)

    in_specs, args = [row], [dres]
    out_specs, out_shape = [], []
    if has_pre:
        in_specs += [row, row, vec]
        args += list(pre)
        out_specs.append(row)
        out_shape.append(jax.ShapeDtypeStruct((t, d), F32))
    if has_post:
        in_specs += [row, vec]
        args += list(post)
        out_specs.append(row)
        out_shape.append(jax.ShapeDtypeStruct((t, d), BF16))
    for _ in range(has_pre + has_post):
        out_specs.append(vec)
        out_shape.append(jax.ShapeDtypeStruct((1, d), F32))
    return pl.pallas_call(
        body, grid=(nt,), in_specs=in_specs, out_specs=out_specs, out_shape=out_shape,
        scratch_shapes=[pltpu.VMEM((SUBLANE, d), F32)] * (has_pre + has_post),
        compiler_params=_cparams(("arbitrary",)), name=name,
    )(*args)


def _loss_grad(y, target, name):
    t, d = y.shape
    tt = _rows(t, 512)
    nt = t // tt
    row = pl.BlockSpec((tt, d), lambda i: (i, 0))
    inv_d = 1.0 / d

    def body(y_ref, t_ref, dy_ref, l_ref):
        i = pl.program_id(0)
        diff = y_ref[...] - t_ref[...]
        dy_ref[...] = diff * inv_d
        s8 = _sum8(diff * diff)
        part = s8[:, 0:LANE]
        for k in range(1, d // LANE):
            part = part + s8[:, k * LANE:(k + 1) * LANE]
        part = part * (0.5 * inv_d)

        @pl.when(i == 0)
        def _():
            l_ref[...] = part

        @pl.when(i > 0)
        def _():
            l_ref[...] += part

    return pl.pallas_call(
        body, grid=(nt,), in_specs=[row, row],
        out_specs=[row, pl.BlockSpec((SUBLANE, LANE), lambda i: (0, 0))],
        out_shape=[jax.ShapeDtypeStruct((t, d), F32), jax.ShapeDtypeStruct((SUBLANE, LANE), F32)],
        compiler_params=_cparams(("arbitrary",)), name=name,
    )(y, target)


def _swiglu_fwd(ab, name):
    t, f2 = ab.shape
    f = f2 // 2
    tt = _rows(t, 256)

    def body(a_ref, b_ref, m_ref):
        a = a_ref[...]
        m_ref[...] = (a * _sigmoid(a) * b_ref[...]).astype(BF16)

    return pl.pallas_call(
        body, grid=(t // tt,),
        in_specs=[pl.BlockSpec((tt, f), lambda i: (i, 0)), pl.BlockSpec((tt, f), lambda i: (i, 1))],
        out_specs=pl.BlockSpec((tt, f), lambda i: (i, 0)),
        out_shape=jax.ShapeDtypeStruct((t, f), BF16), compiler_params=_cparams(("parallel",)), name=name,
    )(ab, ab)


def _swiglu_bwd(ab, dm, name):
    t, f2 = ab.shape
    f = f2 // 2
    tt = _rows(t, 256)

    def body(a_ref, b_ref, dm_ref, da_ref, db_ref):
        a = a_ref[...]
        s = _sigmoid(a)
        d = dm_ref[...]
        da_ref[...] = (d * b_ref[...] * s * (1.0 + a * (1.0 - s))).astype(BF16)
        db_ref[...] = (d * a * s).astype(BF16)

    blk0 = pl.BlockSpec((tt, f), lambda i: (i, 0))
    blk1 = pl.BlockSpec((tt, f), lambda i: (i, 1))
    da, db = pl.pallas_call(
        body, grid=(t // tt,), in_specs=[blk0, blk1, blk0], out_specs=[blk0, blk0],
        out_shape=[jax.ShapeDtypeStruct((t, f), BF16)] * 2, compiler_params=_cparams(("parallel",)), name=name,
    )(ab, ab, dm)
    return da, db


def _log_sigmoid(x):
    return jnp.minimum(x, 0.0) - jnp.log1p(jnp.exp(-jnp.abs(x)))


def _fox_prep(f_t, b_f, name):
    h, t = f_t.shape

    def body(f_ref, b_ref, c_ref):
        r = lax.broadcasted_iota(jnp.int32, (LANE, LANE), 0)
        c = lax.broadcasted_iota(jnp.int32, (LANE, LANE), 1)
        upper = (r <= c).astype(F32)
        carry = jnp.zeros((h, 1), F32)
        for j in range(t // LANE):
            sl = slice(j * LANE, (j + 1) * LANE)
            lf = _log_sigmoid(f_ref[:, sl] + b_ref[...])
            cs = jnp.dot(lf, upper, precision=lax.Precision.HIGHEST, preferred_element_type=F32) + carry
            c_ref[:, sl] = cs
            carry = cs[:, LANE - 1:LANE]

    return pl.pallas_call(body, out_shape=jax.ShapeDtypeStruct((h, t), F32), compiler_params=_cparams(), name=name)(f_t, b_f)


def _fox_bwd(dc_q, dc_k, f_t, b_f, name):
    h, t = f_t.shape

    def body(dq_ref, dk_ref, f_ref, b_ref, df_ref, db_ref):
        r = lax.broadcasted_iota(jnp.int32, (LANE, LANE), 0)
        c = lax.broadcasted_iota(jnp.int32, (LANE, LANE), 1)
        lower = (r >= c).astype(F32)
        carry = jnp.zeros((h, 1), F32)
        dbsum = jnp.zeros((h, 1), F32)
        for j in reversed(range(t // LANE)):
            sl = slice(j * LANE, (j + 1) * LANE)
            dc = dq_ref[:, sl] - dk_ref[:, sl]
            dl = jnp.dot(dc, lower, precision=lax.Precision.HIGHEST, preferred_element_type=F32) + carry
            carry = dl[:, 0:1]
            df = dl * _sigmoid(-(f_ref[:, sl] + b_ref[...]))
            df_ref[:, sl] = df
            dbsum = dbsum + jnp.sum(df, axis=-1, keepdims=True)
        db_ref[...] = dbsum

    return pl.pallas_call(
        body, out_shape=[jax.ShapeDtypeStruct((h, t), F32), jax.ShapeDtypeStruct((h, 1), F32)],
        compiler_params=_cparams(), name=name,
    )(dc_q, dc_k, f_t, b_f)


def _nt_dot(a, b):
    return lax.dot_general(a, b, (((1,), (1,)), ((), ())), preferred_element_type=F32)


def _attn_blocks(t):
    return _tile(t, 256)


def _attn_fwd(proj, c_col, c_row, d, name):
    t = proj.shape[0]
    h = d // LANE
    bq = _attn_blocks(t)
    nb = t // bq
    qc, kc, vc = 4 * h, 5 * h, 6 * h
    scale = LANE ** -0.5

    def body(q_ref, k_ref, v_ref, cc_ref, cr_ref, o_ref, lse_ref, kb, vb):
        i = pl.program_id(1)

        @pl.when(i == 0)
        def _():
            kb[...] = k_ref[...].astype(BF16)
            vb[...] = v_ref[...].astype(BF16)

        q = (q_ref[...] * scale).astype(BF16)
        cc = cc_ref[0]

        def scores(j):
            kj = kb[pl.ds(pl.multiple_of(j * bq, bq), bq), :]
            return _nt_dot(q, kj) + (cc - cr_ref[0, j])

        def update(s, j, carry):
            m, l, acc = carry
            m_new = jnp.maximum(m, jnp.max(s, axis=-1, keepdims=True))
            alpha = jnp.exp(m - m_new)
            p = jnp.exp(s - m_new)
            l = alpha * l + jnp.sum(p, axis=-1, keepdims=True)
            vj = vb[pl.ds(pl.multiple_of(j * bq, bq), bq), :]
            acc = alpha * acc + jnp.dot(p.astype(BF16), vj, preferred_element_type=F32)
            return m_new, l, acc

        init = (jnp.full((bq, 1), NEG, F32), jnp.zeros((bq, 1), F32), jnp.zeros((bq, LANE), F32))
        carry = lax.fori_loop(0, i, lambda j, cr: update(scores(j), j, cr), init)
        rr = lax.broadcasted_iota(jnp.int32, (bq, bq), 0)
        cl = lax.broadcasted_iota(jnp.int32, (bq, bq), 1)
        s = jnp.where(rr >= cl, scores(i), NEG)
        m, l, acc = update(s, i, carry)
        o_ref[...] = acc / l
        lse_ref[0] = m + jnp.log(l)

    return pl.pallas_call(
        body, grid=(h, nb),
        in_specs=[
            pl.BlockSpec((bq, LANE), lambda hh, i: (i, qc + hh)),
            pl.BlockSpec((t, LANE), lambda hh, i: (0, kc + hh)),
            pl.BlockSpec((t, LANE), lambda hh, i: (0, vc + hh)),
            pl.BlockSpec((1, bq, 1), lambda hh, i: (hh, i, 0)),
            pl.BlockSpec((1, nb, 1, bq), lambda hh, i: (hh, 0, 0, 0)),
        ],
        out_specs=[
            pl.BlockSpec((bq, LANE), lambda hh, i: (i, hh)),
            pl.BlockSpec((1, bq, 1), lambda hh, i: (hh, i, 0)),
        ],
        out_shape=[jax.ShapeDtypeStruct((t, d), F32), jax.ShapeDtypeStruct((h, t, 1), F32)],
        scratch_shapes=[pltpu.VMEM((t, LANE), BF16), pltpu.VMEM((t, LANE), BF16)],
        compiler_params=_cparams(("arbitrary", "arbitrary")), name=name,
    )(proj, proj, proj, c_col, c_row)


def _attn_bwd_dq(proj, do, o, lse, c_col, c_row, d, name):
    t = proj.shape[0]
    h = d // LANE
    bq = _attn_blocks(t)
    nb = t // bq
    qc, kc, vc = 4 * h, 5 * h, 6 * h
    scale = LANE ** -0.5

    def body(q_ref, k_ref, v_ref, do_ref, o_ref, lse_ref, cc_ref, cr_ref, dq_ref, delta_ref, dcq_ref, kb, vb):
        i = pl.program_id(1)

        @pl.when(i == 0)
        def _():
            kb[...] = k_ref[...].astype(BF16)
            vb[...] = v_ref[...].astype(BF16)

        q = (q_ref[...] * scale).astype(BF16)
        dof = do_ref[...]
        dob = dof.astype(BF16)
        delta = jnp.sum(dof * o_ref[...], axis=-1, keepdims=True)
        delta_ref[0] = delta
        cc = cc_ref[0]
        lse_i = lse_ref[0]

        def block(j, masked):
            kj = kb[pl.ds(pl.multiple_of(j * bq, bq), bq), :]
            vj = vb[pl.ds(pl.multiple_of(j * bq, bq), bq), :]
            s = _nt_dot(q, kj) + (cc - cr_ref[0, j])
            p = jnp.exp(s - lse_i)
            if masked:
                rr = lax.broadcasted_iota(jnp.int32, (bq, bq), 0)
                cl = lax.broadcasted_iota(jnp.int32, (bq, bq), 1)
                p = jnp.where(rr >= cl, p, 0.0)
            dp = _nt_dot(dob, vj)
            ds = p * (dp - delta)
            return jnp.dot(ds.astype(BF16), kj, preferred_element_type=F32), jnp.sum(ds, axis=-1, keepdims=True)

        def step(j, carry):
            dq, dc = carry
            a, b = block(j, False)
            return dq + a, dc + b

        dq, dc = lax.fori_loop(0, i, step, (jnp.zeros((bq, LANE), F32), jnp.zeros((bq, 1), F32)))
        a, b = block(i, True)
        dq_ref[...] = ((dq + a) * scale).astype(BF16)
        dcq_ref[0] = dc + b

    col = pl.BlockSpec((1, bq, 1), lambda hh, i: (hh, i, 0))
    blk = pl.BlockSpec((bq, LANE), lambda hh, i: (i, hh))
    return pl.pallas_call(
        body, grid=(h, nb),
        in_specs=[
            pl.BlockSpec((bq, LANE), lambda hh, i: (i, qc + hh)),
            pl.BlockSpec((t, LANE), lambda hh, i: (0, kc + hh)),
            pl.BlockSpec((t, LANE), lambda hh, i: (0, vc + hh)),
            blk, blk, col, col,
            pl.BlockSpec((1, nb, 1, bq), lambda hh, i: (hh, 0, 0, 0)),
        ],
        out_specs=[blk, col, col],
        out_shape=[jax.ShapeDtypeStruct((t, d), BF16), jax.ShapeDtypeStruct((h, t, 1), F32),
                   jax.ShapeDtypeStruct((h, t, 1), F32)],
        scratch_shapes=[pltpu.VMEM((t, LANE), BF16), pltpu.VMEM((t, LANE), BF16)],
        compiler_params=_cparams(("arbitrary", "arbitrary")), name=name,
    )(proj, proj, proj, do, o, lse, c_col, c_row)


def _attn_bwd_dkv(proj, do, lse_row, delta_row, c_col, c_row, d, name):
    t = proj.shape[0]
    h = d // LANE
    bk = _attn_blocks(t)
    nb = t // bk
    qc, kc, vc = 4 * h, 5 * h, 6 * h
    scale = LANE ** -0.5

    def body(q_ref, k_ref, v_ref, do_ref, lse_ref, delta_ref, cc_ref, cr_ref, dk_ref, dv_ref, dck_ref, qb, dob):
        j = pl.program_id(1)

        @pl.when(j == 0)
        def _():
            qb[...] = (q_ref[...] * scale).astype(BF16)
            dob[...] = do_ref[...].astype(BF16)

        kj = k_ref[...].astype(BF16)
        vj = v_ref[...].astype(BF16)
        ck = cc_ref[0]

        def block(i, masked):
            qi = qb[pl.ds(pl.multiple_of(i * bk, bk), bk), :]
            doi = dob[pl.ds(pl.multiple_of(i * bk, bk), bk), :]
            st = _nt_dot(kj, qi) + (cr_ref[0, i] - ck)
            pt = jnp.exp(st - lse_ref[0, i])
            if masked:
                kk = lax.broadcasted_iota(jnp.int32, (bk, bk), 0)
                qq = lax.broadcasted_iota(jnp.int32, (bk, bk), 1)
                pt = jnp.where(qq >= kk, pt, 0.0)
            dv = jnp.dot(pt.astype(BF16), doi, preferred_element_type=F32)
            dpt = _nt_dot(vj, doi)
            dst = pt * (dpt - delta_ref[0, i])
            dk = jnp.dot(dst.astype(BF16), qi, preferred_element_type=F32)
            return dk, dv, jnp.sum(dst, axis=-1, keepdims=True)

        def step(i, carry):
            a, b, c = block(i, False)
            return carry[0] + a, carry[1] + b, carry[2] + c

        first = block(j, True)
        dk, dv, dc = lax.fori_loop(j + 1, nb, step, first)
        dk_ref[...] = dk.astype(BF16)
        dv_ref[...] = dv.astype(BF16)
        dck_ref[0] = dc

    rowv = pl.BlockSpec((1, nb, 1, bk), lambda hh, j: (hh, 0, 0, 0))
    col = pl.BlockSpec((1, bk, 1), lambda hh, j: (hh, j, 0))
    blk = pl.BlockSpec((bk, LANE), lambda hh, j: (j, hh))
    return pl.pallas_call(
        body, grid=(h, nb),
        in_specs=[
            pl.BlockSpec((t, LANE), lambda hh, j: (0, qc + hh)),
            pl.BlockSpec((bk, LANE), lambda hh, j: (j, kc + hh)),
            pl.BlockSpec((bk, LANE), lambda hh, j: (j, vc + hh)),
            pl.BlockSpec((t, LANE), lambda hh, j: (0, hh)),
            rowv, rowv, col, rowv,
        ],
        out_specs=[blk, blk, col],
        out_shape=[jax.ShapeDtypeStruct((t, d), BF16), jax.ShapeDtypeStruct((t, d), BF16),
                   jax.ShapeDtypeStruct((h, t, 1), F32)],
        scratch_shapes=[pltpu.VMEM((t, LANE), BF16), pltpu.VMEM((t, LANE), BF16)],
        compiler_params=_cparams(("arbitrary", "arbitrary")), name=name,
    )(proj, proj, proj, do, lse_row, delta_row, c_col, c_row)


def _sgu_forward(u_ref, v_ref, gv_ref, wm_ref, bs_ref, mix_sc, groups):
    gu, dgu = _gelu_and_grad(u_ref[...])
    gvv, dgv = _gelu_and_grad(v_ref[...])
    mu = jnp.mean(gvv, axis=-1, keepdims=True)
    xc = gvv - mu
    r = lax.rsqrt(jnp.mean(xc * xc, axis=-1, keepdims=True) + EPS)
    nhat = xc * r
    vn = (nhat * gv_ref[...]).astype(BF16)
    for g in range(groups):
        sl = slice(g * LANE, (g + 1) * LANE)
        mix_sc[:, sl] = jnp.dot(wm_ref[g], vn[:, sl], preferred_element_type=F32) + bs_ref[g]
    return gu, dgu, dgv, nhat, r, vn, mix_sc[...]


def _mix_fwd(proj, o, wm, bs, g_v, d, name):
    t = proj.shape[0]
    groups = d // LANE

    def body(u_ref, v_ref, ga_ref, gb_ref, o_ref, wm_ref, bs_ref, gv_ref, out_ref, mix_sc):
        gu, _, _, _, _, _, mixed = _sgu_forward(u_ref, v_ref, gv_ref, wm_ref, bs_ref, mix_sc, groups)
        out_ref[...] = (_sigmoid(ga_ref[...]) * (gu * mixed) + _sigmoid(gb_ref[...]) * o_ref[...]).astype(BF16)

    def colblk(k):
        return pl.BlockSpec((LANE, d), lambda i, k=k: (i, k))

    full3 = pl.BlockSpec((groups, LANE, LANE), lambda i: (0, 0, 0))
    return pl.pallas_call(
        body, grid=(t // LANE,),
        in_specs=[colblk(0), colblk(1), colblk(2), colblk(3), colblk(0), full3,
                  pl.BlockSpec((groups, LANE, 1), lambda i: (0, 0, 0)), pl.BlockSpec((1, d), lambda i: (0, 0))],
        out_specs=colblk(0),
        out_shape=jax.ShapeDtypeStruct((t, d), BF16),
        scratch_shapes=[pltpu.VMEM((LANE, d), F32)],
        compiler_params=_cparams(("parallel",)), name=name,
    )(proj, proj, proj, proj, o, wm, bs, g_v)


def _mix_bwd(dmerged, proj, o, wm, wm_t, bs, g_v, d, name):
    t = proj.shape[0]
    groups = d // LANE
    nt = t // LANE

    def body(dm_ref, u_ref, v_ref, ga_ref, gb_ref, o_ref, wm_ref, wmt_ref, bs_ref, gv_ref,
             da_ref, do_ref, dws_ref, dbs_ref, dgv_ref, mix_sc, dvn_sc, gv_acc):
        i = pl.program_id(0)

        @pl.when(i == 0)
        def _():
            dws_ref[...] = jnp.zeros_like(dws_ref)
            dbs_ref[...] = jnp.zeros_like(dbs_ref)
            gv_acc[...] = jnp.zeros_like(gv_acc)

        gu, dgu, dgv, nhat, r, vn, mixed = _sgu_forward(u_ref, v_ref, gv_ref, wm_ref, bs_ref, mix_sc, groups)
        dm = dm_ref[...]
        sa = _sigmoid(ga_ref[...])
        sb = _sigmoid(gb_ref[...])
        ov = o_ref[...]
        y_a = gu * mixed
        da_ref[:, 2 * d:3 * d] = (dm * y_a * sa * (1.0 - sa)).astype(BF16)
        da_ref[:, 3 * d:4 * d] = (dm * ov * sb * (1.0 - sb)).astype(BF16)
        do_ref[...] = dm * sb
        dy_a = dm * sa
        da_ref[:, 0:d] = (dy_a * mixed * dgu).astype(BF16)
        dmixed = dy_a * gu
        dmixed_b = dmixed.astype(BF16)
        for g in range(groups):
            sl = slice(g * LANE, (g + 1) * LANE)
            dvn_sc[:, sl] = jnp.dot(wmt_ref[g], dmixed_b[:, sl], preferred_element_type=F32)
            dws_ref[g] += _nt_dot(dmixed_b[:, sl], vn[:, sl])
            dbs_ref[g] += jnp.sum(dmixed[:, sl], axis=-1, keepdims=True)
        dvn = dvn_sc[...]
        gv_acc[...] += _sum8(dvn * nhat)
        dn = dvn * gv_ref[...]
        dgelu = r * (dn - jnp.mean(dn, axis=-1, keepdims=True) - nhat * jnp.mean(dn * nhat, axis=-1, keepdims=True))
        da_ref[:, d:2 * d] = (dgelu * dgv).astype(BF16)

        @pl.when(i == nt - 1)
        def _():
            dgv_ref[...] = jnp.sum(gv_acc[...], axis=0, keepdims=True)
            rr = lax.broadcasted_iota(jnp.int32, (LANE, LANE), 0)
            cl = lax.broadcasted_iota(jnp.int32, (LANE, LANE), 1)
            for g in range(groups):
                dws_ref[g] = jnp.where(rr >= cl, dws_ref[g], 0.0)

    def colblk(k):
        return pl.BlockSpec((LANE, d), lambda i, k=k: (i, k))

    full3 = pl.BlockSpec((groups, LANE, LANE), lambda i: (0, 0, 0))
    col3 = pl.BlockSpec((groups, LANE, 1), lambda i: (0, 0, 0))
    vec = pl.BlockSpec((1, d), lambda i: (0, 0))
    return pl.pallas_call(
        body, grid=(nt,),
        in_specs=[colblk(0), colblk(0), colblk(1), colblk(2), colblk(3), colblk(0), full3, full3, col3, vec],
        out_specs=[pl.BlockSpec((LANE, 4 * d), lambda i: (i, 0)), colblk(0), full3, col3, vec],
        out_shape=[jax.ShapeDtypeStruct((t, 4 * d), BF16), jax.ShapeDtypeStruct((t, d), F32),
                   jax.ShapeDtypeStruct((groups, LANE, LANE), F32), jax.ShapeDtypeStruct((groups, LANE, 1), F32),
                   jax.ShapeDtypeStruct((1, d), F32)],
        scratch_shapes=[pltpu.VMEM((LANE, d), F32), pltpu.VMEM((LANE, d), F32), pltpu.VMEM((SUBLANE, d), F32)],
        compiler_params=_cparams(("arbitrary",)), name=name,
    )(dmerged, proj, proj, proj, proj, o, wm, wm_t, bs, g_v)


def _adamw(w, g, m, v, name):
    r, c = w.shape
    cap = max(SUBLANE, (2 * 1024 * 1024) // (4 * c) // SUBLANE * SUBLANE)
    tr = _rows(r, cap)
    c1 = 1.0 / (1.0 - ADAM_B1 ** ADAM_STEP)
    c2 = 1.0 / (1.0 - ADAM_B2 ** ADAM_STEP)

    def body(w_ref, g_ref, m_ref, v_ref, d_ref, nm_ref, nv_ref):
        gg = g_ref[...]
        nm = ADAM_B1 * m_ref[...] + (1.0 - ADAM_B1) * gg
        nv = ADAM_B2 * v_ref[...] + (1.0 - ADAM_B2) * (gg * gg)
        nm_ref[...] = nm
        nv_ref[...] = nv
        d_ref[...] = -ADAM_LR * ((nm * c1) / (jnp.sqrt(nv * c2) + ADAM_EPS) + ADAM_WD * w_ref[...])

    blk = pl.BlockSpec((tr, c), lambda i: (i, 0))
    return pl.pallas_call(
        body, grid=(r // tr,), in_specs=[blk] * 4, out_specs=[blk] * 3,
        out_shape=[jax.ShapeDtypeStruct((r, c), F32)] * 3, compiler_params=_cparams(("parallel",)), name=name,
    )(w, g, m, v)


def _add_mine(g_all, recv, c_idx, name):
    _, r, c = g_all.shape
    cap = max(16, (2 * 1024 * 1024) // (2 * c) // 16 * 16)
    tr = _rows(r, cap)

    def body(c_ref, a_ref, b_ref, o_ref):
        o_ref[...] = (a_ref[0].astype(F32) + b_ref[...].astype(F32)).astype(BF16)

    return pl.pallas_call(
        body,
        grid_spec=pltpu.PrefetchScalarGridSpec(
            num_scalar_prefetch=1, grid=(r // tr,),
            in_specs=[pl.BlockSpec((1, tr, c), lambda i, cr: (cr[0], i, 0)), pl.BlockSpec((tr, c), lambda i, cr: (i, 0))],
            out_specs=pl.BlockSpec((tr, c), lambda i, cr: (i, 0)),
        ),
        out_shape=jax.ShapeDtypeStruct((r, c), BF16), compiler_params=_cparams(("parallel",)), name=name,
    )(c_idx, g_all, recv)


def _sum_slots(x, name):
    s, r, c = x.shape
    cap = max(16, (1024 * 1024) // (4 * c) // 16 * 16)
    tr = _rows(r, cap) if r % 16 == 0 else r

    def body(x_ref, o_ref):
        acc = x_ref[0].astype(F32)
        for k in range(1, s):
            acc = acc + x_ref[k].astype(F32)
        o_ref[...] = acc

    return pl.pallas_call(
        body, grid=(r // tr,), in_specs=[pl.BlockSpec((s, tr, c), lambda i: (0, i, 0))],
        out_specs=pl.BlockSpec((tr, c), lambda i: (i, 0)),
        out_shape=jax.ShapeDtypeStruct((r, c), F32), compiler_params=_cparams(("parallel",)), name=name,
    )(x)


_HBM = pl.BlockSpec(memory_space=pl.ANY)


def _place():
    x, y, c = lax.axis_index("x"), lax.axis_index("y"), lax.axis_index("c")
    chips = [(1 - x, y), (x, 1 - y), (1 - x, 1 - y)]
    return x, y, c, chips


def _gather_weights(ws, name):
    n = len(ws)

    def body(*refs):
        w_refs, o_refs = refs[:n], refs[n:2 * n]
        ssem, rsem, lsem = refs[2 * n:]
        x, y, c, chips = _place()
        me = 2 * x + y
        sib = (x, y, 1 - c)
        local = []
        for a in range(n):
            for l in range(2):
                cp = pltpu.make_async_copy(w_refs[a].at[l], o_refs[a].at[l, me], lsem.at[a, l])
                cp.start()
                local.append(cp)

        def copy(a, k, src, dst, to):
            return pltpu.make_async_remote_copy(src_ref=src, dst_ref=dst, send_sem=ssem.at[a, k], recv_sem=rsem.at[a, k],
                                                device_id=to, device_id_type=MESH)

        sends = []
        for a in range(n):
            for j, chip in enumerate(chips):
                cp = copy(a, j, w_refs[a].at[c], o_refs[a].at[c, me], (chip[0], chip[1], c))
                cp.start()
                sends.append(cp)
        for a in range(n):
            for j, chip in enumerate(chips):
                slot = o_refs[a].at[c, 2 * chip[0] + chip[1]]
                copy(a, j, slot, slot, sib).wait_recv()
                cp = copy(a, 3 + j, slot, slot, sib)
                cp.start()
                sends.append(cp)
        for a in range(n):
            for j, chip in enumerate(chips):
                slot = o_refs[a].at[1 - c, 2 * chip[0] + chip[1]]
                copy(a, 3 + j, slot, slot, sib).wait_recv()
        for cp in sends:
            cp.wait_send()
        for cp in local:
            cp.wait()

    return pl.pallas_call(
        body, in_specs=[_HBM] * n, out_specs=[_HBM] * n,
        out_shape=[jax.ShapeDtypeStruct((2, N_CHIPS) + w.shape[1:], w.dtype) for w in ws],
        scratch_shapes=[pltpu.SemaphoreType.DMA((n, 6)), pltpu.SemaphoreType.DMA((n, 6)), pltpu.SemaphoreType.DMA((n, 2))],
        name=name,
    )(*ws)


def _swap_other_layer(gs, name):
    n = len(gs)

    def body(*refs):
        g_refs, o_refs = refs[:n], refs[n:2 * n]
        ssem, rsem = refs[2 * n:]
        x, y, c, _ = _place()
        cps = []
        for a in range(n):
            cp = pltpu.make_async_remote_copy(src_ref=g_refs[a].at[1 - c], dst_ref=o_refs[a], send_sem=ssem.at[a],
                                              recv_sem=rsem.at[a], device_id=(x, y, 1 - c), device_id_type=MESH)
            cp.start()
            cps.append(cp)
        for cp in cps:
            cp.wait()

    return pl.pallas_call(
        body, in_specs=[_HBM] * n, out_specs=[_HBM] * n,
        out_shape=[jax.ShapeDtypeStruct(g.shape[1:], g.dtype) for g in gs],
        scratch_shapes=[pltpu.SemaphoreType.DMA((n,)), pltpu.SemaphoreType.DMA((n,))], name=name,
    )(*gs)


def _scatter_to_chips(ps, name):
    n = len(ps)

    def body(*refs):
        p_refs, o_refs = refs[:n], refs[n:2 * n]
        ssem, rsem, lsem = refs[2 * n:]
        x, y, c, chips = _place()
        me = 2 * x + y
        local, sends = [], []
        for a in range(n):
            cp = pltpu.make_async_copy(p_refs[a].at[me], o_refs[a].at[me], lsem.at[a])
            cp.start()
            local.append(cp)
            for j, chip in enumerate(chips):
                cp = pltpu.make_async_remote_copy(
                    src_ref=p_refs[a].at[2 * chip[0] + chip[1]], dst_ref=o_refs[a].at[me], send_sem=ssem.at[a, j],
                    recv_sem=rsem.at[a, j], device_id=(chip[0], chip[1], c), device_id_type=MESH)
                cp.start()
                sends.append(cp)
        for a in range(n):
            for j, chip in enumerate(chips):
                slot = o_refs[a].at[2 * chip[0] + chip[1]]
                pltpu.make_async_remote_copy(src_ref=slot, dst_ref=slot, send_sem=ssem.at[a, j], recv_sem=rsem.at[a, j],
                                             device_id=(x, y, c), device_id_type=MESH).wait_recv()
        for cp in sends:
            cp.wait_send()
        for cp in local:
            cp.wait()

    return pl.pallas_call(
        body, in_specs=[_HBM] * n, out_specs=[_HBM] * n,
        out_shape=[jax.ShapeDtypeStruct(p.shape, p.dtype) for p in ps],
        scratch_shapes=[pltpu.SemaphoreType.DMA((n, 3)), pltpu.SemaphoreType.DMA((n, 3)), pltpu.SemaphoreType.DMA((n,))],
        name=name,
    )(*ps)


def _share_layers(fs, name):
    n = len(fs)

    def body(*refs):
        f_refs, o_refs = refs[:n], refs[n:2 * n]
        ssem, rsem, lsem = refs[2 * n:]
        x, y, c, _ = _place()
        local, sends = [], []
        for a in range(n):
            cp = pltpu.make_async_copy(f_refs[a], o_refs[a].at[c], lsem.at[a])
            cp.start()
            local.append(cp)
            cp = pltpu.make_async_remote_copy(src_ref=f_refs[a], dst_ref=o_refs[a].at[c], send_sem=ssem.at[a],
                                              recv_sem=rsem.at[a], device_id=(x, y, 1 - c), device_id_type=MESH)
            cp.start()
            sends.append(cp)
        for a in range(n):
            slot = o_refs[a].at[1 - c]
            pltpu.make_async_remote_copy(src_ref=slot, dst_ref=slot, send_sem=ssem.at[a], recv_sem=rsem.at[a],
                                         device_id=(x, y, c), device_id_type=MESH).wait_recv()
        for cp in sends:
            cp.wait_send()
        for cp in local:
            cp.wait()

    return pl.pallas_call(
        body, in_specs=[_HBM] * n, out_specs=[_HBM] * n,
        out_shape=[jax.ShapeDtypeStruct((2,) + f.shape, f.dtype) for f in fs],
        scratch_shapes=[pltpu.SemaphoreType.DMA((n,)), pltpu.SemaphoreType.DMA((n,)), pltpu.SemaphoreType.DMA((n,))],
        name=name,
    )(*fs)


def _gather_all(buf, name):
    def body(b_ref, o_ref, ssem, rsem, lsem):
        x, y, c, _ = _place()
        me = 4 * x + 2 * y + c
        local = pltpu.make_async_copy(b_ref, o_ref.at[me], lsem)
        local.start()
        flips = [(fx, fy, fc) for fx in (0, 1) for fy in (0, 1) for fc in (0, 1)][1:]
        peers = [((1 - x) if fx else x, (1 - y) if fy else y, (1 - c) if fc else c) for fx, fy, fc in flips]
        sends = []
        for k, peer in enumerate(peers):
            cp = pltpu.make_async_remote_copy(src_ref=b_ref, dst_ref=o_ref.at[me], send_sem=ssem.at[k], recv_sem=rsem.at[k],
                                              device_id=peer, device_id_type=MESH)
            cp.start()
            sends.append(cp)
        for k, peer in enumerate(peers):
            slot = o_ref.at[4 * peer[0] + 2 * peer[1] + peer[2]]
            pltpu.make_async_remote_copy(src_ref=slot, dst_ref=slot, send_sem=ssem.at[k], recv_sem=rsem.at[k],
                                         device_id=(x, y, c), device_id_type=MESH).wait_recv()
        for cp in sends:
            cp.wait_send()
        local.wait()

    return pl.pallas_call(
        body, in_specs=[_HBM], out_specs=_HBM,
        out_shape=jax.ShapeDtypeStruct((N_DEV,) + buf.shape, buf.dtype),
        scratch_shapes=[pltpu.SemaphoreType.DMA((N_DEV - 1,)), pltpu.SemaphoreType.DMA((N_DEV - 1,)), pltpu.SemaphoreType.DMA],
        name=name,
    )(buf)


def _layer_forward(x, h, wts, sm, d):
    t = x.shape[0]
    heads = d // LANE
    bq = _attn_blocks(t)
    proj = _matmul(h, wts["w_cat"], "nn", F32, "proj_fwd")
    f_t = proj[:, 7 * d:7 * d + heads].T
    c_t = _fox_prep(f_t, sm["b_f"], "fox_prep")
    c_col = c_t.reshape(heads, t, 1)
    c_row = c_t.reshape(heads, t // bq, 1, bq)
    o, lse = _attn_fwd(proj, c_col, c_row, d, "attn_fwd")
    merged = _mix_fwd(proj, o, sm["wm"], sm["bs"], sm["g_v"], d, "mix_fwd")
    z = _matmul(merged, wts["w_out"], "nn", F32, "out_fwd")
    x1, h2 = _norm_fwd(x, z, sm["g_post"], sm["g_fpre"], "norm_mid")
    ab = _matmul(h2, wts["w_gu"], "nn", F32, "gu_fwd")
    mm = _swiglu_fwd(ab, "swiglu_fwd")
    z2 = _matmul(mm, wts["w_d"], "nn", F32, "down_fwd")
    return dict(x=x, h=h, proj=proj, f_t=f_t, c_col=c_col, c_row=c_row, o=o, lse=lse, merged=merged, z=z, x1=x1,
                h2=h2, ab=ab, mm=mm, z2=z2)


def _layer_backward(dz2, dx2, sv, wts, sm, d):
    t = dx2.shape[0]
    heads = d // LANE
    bq = _attn_blocks(t)
    g = {}
    dm = _matmul(dz2, wts["w_d"], "nt", F32, "down_bwd_x", tn_cap=1408, tk_cap=1024)
    g["w_d"] = _matmul(sv["mm"], dz2, "tn", F32, "down_bwd_w", tk_cap=1024)
    da, db = _swiglu_bwd(sv["ab"], dm, "swiglu_bwd")
    dab = jnp.concatenate([da, db], axis=1)
    dh2 = _matmul(dab, wts["w_gu"], "nt", F32, "gu_bwd_x", tn_cap=1024, tk_cap=1408)
    g["w_gu"] = _matmul(sv["h2"], dab, "tn", F32, "gu_bwd_w", tk_cap=1024)
    dx1, dz, g["g_fpre"], g["g_post"] = _norm_bwd(dx2, (dh2, sv["x1"], sm["g_fpre"]), (sv["z"], sm["g_post"]), "norm_bwd_mid")
    dmerged = _matmul(dz, wts["w_out"], "nt", F32, "out_bwd_x", tk_cap=1024)
    g["w_out"] = _matmul(sv["merged"], dz, "tn", F32, "out_bwd_w", tk_cap=1024)
    d_a, do, g["w_s"], g["b_s"], g["g_v"] = _mix_bwd(dmerged, sv["proj"], sv["o"], sm["wm"], sm["wm_t"], sm["bs"], sm["g_v"], d,
                                                      "mix_bwd")
    dq, delta, dc_q = _attn_bwd_dq(sv["proj"], do, sv["o"], sv["lse"], sv["c_col"], sv["c_row"], d, "attn_bwd_dq")
    lse_row = sv["lse"].reshape(heads, t // bq, 1, bq)
    delta_row = delta.reshape(heads, t // bq, 1, bq)
    dk, dv, dc_k = _attn_bwd_dkv(sv["proj"], do, lse_row, delta_row, sv["c_col"], sv["c_row"], d, "attn_bwd_dkv")
    df_t, g["b_f"] = _fox_bwd(dc_q.reshape(heads, t), dc_k.reshape(heads, t), sv["f_t"], sm["b_f"], "fox_bwd")
    df = jnp.pad(df_t.T, ((0, 0), (0, LANE - heads))).astype(BF16)
    dproj = jnp.concatenate([d_a, dq, dk, dv, df], axis=1)
    dh = _matmul(dproj, wts["w_cat"], "nt", F32, "proj_bwd_x", tn_cap=1024, tk_cap=2432)
    g["w_cat"] = _matmul(sv["h"], dproj, "tn", F32, "proj_bwd_w", tk_cap=1024)
    return dh, dx1, g


def _small_pack(parts):
    flat = jnp.concatenate([p.reshape(-1) for p in parts])
    n = flat.shape[0]
    pad = (-n) % (16 * LANE)
    return jnp.pad(flat, (0, pad)).reshape(-1, LANE)


def kernel(x, mix_pre_g, w_in, b_forget, sgu_norm_g, w_spatial, b_spatial, w_out, mix_post_g, ffn_pre_g, w_gate, w_up, w_down, ffn_post_g, loss_target, m_mix_pre_g, m_w_in, m_b_forget, m_sgu_norm_g, m_w_spatial, m_b_spatial, m_w_out, m_mix_post_g, m_ffn_pre_g, m_w_gate, m_w_up, m_w_down, m_ffn_post_g, v_mix_pre_g, v_w_in, v_b_forget, v_sgu_norm_g, v_w_spatial, v_b_spatial, v_w_out, v_mix_post_g, v_ffn_pre_g, v_w_gate, v_w_up, v_w_down, v_ffn_post_g):
    depth, d = mix_pre_g.shape
    assert depth == 2, "core c of a chip owns layer c"
    heads = d // LANE
    t = x.shape[1]
    ff = w_down.shape[1] * N_CHIPS
    in_w = w_in.shape[2] * N_CHIPS
    assert in_w == 7 * d + heads
    xs = x.reshape(t, d)
    target = loss_target.reshape(t, d)
    c_idx = lax.axis_index("c").astype(jnp.int32).reshape(1)

    big = [w_in, w_out, w_gate, w_up, w_down]
    gw_in, gw_out, gw_g, gw_u, gw_d = _gather_weights([w.astype(BF16) for w in big], "gather_weights")

    wts = []
    for l in range(depth):
        full_in = jnp.concatenate([gw_in[l, s] for s in range(N_CHIPS)], axis=1)
        w_cat = jnp.concatenate(
            [full_in[:, 0:2 * d], full_in[:, 5 * d:7 * d], full_in[:, 2 * d:5 * d], full_in[:, 7 * d:],
             jnp.zeros((d, LANE - heads), BF16)], axis=1)
        w_gu = jnp.concatenate([gw_g[l, s] for s in range(N_CHIPS)] + [gw_u[l, s] for s in range(N_CHIPS)], axis=1)
        wts.append(dict(w_cat=w_cat, w_out=gw_out[l].reshape(d, d), w_gu=w_gu, w_d=gw_d[l].reshape(ff, d)))

    tril = jnp.tril(jnp.ones((LANE, LANE), bool))
    smalls = []
    for l in range(depth):
        wm = jnp.where(tril[None], w_spatial[l], 0.0).astype(BF16)
        smalls.append(dict(
            b_f=b_forget[l].reshape(heads, 1), wm=wm, wm_t=jnp.swapaxes(wm, 1, 2), bs=b_spatial[l].reshape(heads, LANE, 1),
            g_v=sgu_norm_g[l].reshape(1, d), g_pre=mix_pre_g[l].reshape(1, d), g_post=mix_post_g[l].reshape(1, d),
            g_fpre=ffn_pre_g[l].reshape(1, d), g_fpost=ffn_post_g[l].reshape(1, d)))

    saved = []
    xin = xs
    h = _norm_fwd(xs, None, None, smalls[0]["g_pre"], "norm_first")
    for l in range(depth):
        sv = _layer_forward(xin, h, wts[l], smalls[l], d)
        saved.append(sv)
        g_next = smalls[l + 1]["g_pre"] if l + 1 < depth else smalls[l]["g_pre"]
        xin, h = _norm_fwd(sv["x1"], sv["z2"], smalls[l]["g_fpost"], g_next, "norm_out")
    dy, loss_part = _loss_grad(xin, target, "loss")
    loss = lax.psum(jnp.sum(loss_part), ("x", "y", "c"))

    grads = [None] * depth
    dx2 = dy
    dz2, g_fpost = _norm_bwd(dx2, None, (saved[depth - 1]["z2"], smalls[depth - 1]["g_fpost"]), "norm_bwd_top")
    for l in reversed(range(depth)):
        dh, dx1, g = _layer_backward(dz2, dx2, saved[l], wts[l], smalls[l], d)
        g["g_fpost"] = g_fpost
        if l > 0:
            dx2, dz2, g["g_pre"], g_fpost = _norm_bwd(dx1, (dh, saved[l]["x"], smalls[l]["g_pre"]),
                                                       (saved[l - 1]["z2"], smalls[l - 1]["g_fpost"]), "norm_bwd_between")
        else:
            grad_x, g["g_pre"] = _norm_bwd(dx1, (dh, saved[l]["x"], smalls[l]["g_pre"]), None, "norm_bwd_bottom")
        grads[l] = g

    def per_chip(l):
        gc = grads[l]["w_cat"]
        g_in = jnp.concatenate([gc[:, 0:2 * d], gc[:, 4 * d:7 * d], gc[:, 2 * d:4 * d], gc[:, 7 * d:7 * d + heads]], axis=1)
        wc = in_w // N_CHIPS
        fc = ff // N_CHIPS
        gu = grads[l]["w_gu"]
        return [
            jnp.stack([g_in[:, s * wc:(s + 1) * wc] for s in range(N_CHIPS)]).astype(BF16),
            grads[l]["w_out"].reshape(N_CHIPS, d // N_CHIPS, d).astype(BF16),
            jnp.stack([gu[:, s * fc:(s + 1) * fc] for s in range(N_CHIPS)]).astype(BF16),
            jnp.stack([gu[:, ff + s * fc:ff + (s + 1) * fc] for s in range(N_CHIPS)]).astype(BF16),
            grads[l]["w_d"].reshape(N_CHIPS, fc, d).astype(BF16),
        ]

    by_layer = [per_chip(l) for l in range(depth)]
    g_all = [jnp.stack([by_layer[0][a], by_layer[1][a]]) for a in range(5)]
    flat2 = [ga.reshape(2, N_CHIPS * ga.shape[2], ga.shape[3]) for ga in g_all]
    from_sib = _swap_other_layer(flat2, "swap_grads")
    chip_part = [_add_mine(a, b, c_idx, "add_sibling") for a, b in zip(flat2, from_sib)]
    chip_part = [p.reshape(ga.shape[1:]) for p, ga in zip(chip_part, g_all)]
    landed = _scatter_to_chips(chip_part, "scatter_grads")
    mine = [_sum_slots(p, "sum_chips") for p in landed]
    g_big = _share_layers(mine, "share_grads")
    grad_w_in, grad_w_out, grad_w_gate, grad_w_up, grad_w_down = g_big

    small_names = ["g_pre", "b_f", "g_v", "w_s", "b_s", "g_post", "g_fpre", "g_fpost"]
    small_shapes = [mix_pre_g.shape, b_forget.shape, sgu_norm_g.shape, w_spatial.shape, b_spatial.shape, mix_post_g.shape,
                    ffn_pre_g.shape, ffn_post_g.shape]
    parts = [jnp.stack([grads[l][nme].reshape(-1) for l in range(depth)]) for nme in small_names]
    packed = _small_pack(parts)
    total = _sum_slots(_gather_all(packed, "gather_small"), "sum_small").reshape(-1)
    small_grads, off = [], 0
    for shp in small_shapes:
        n = math.prod(shp)
        small_grads.append(total[off:off + n].reshape(shp))
        off += n
    (grad_mix_pre_g, grad_b_forget, grad_sgu_norm_g, grad_w_spatial, grad_b_spatial, grad_mix_post_g, grad_ffn_pre_g,
     grad_ffn_post_g) = small_grads

    def adam(w, g, m, v, name):
        shp = w.shape
        if w.ndim >= 3 and shp[-1] >= LANE:
            two = (math.prod(shp[:-1]), shp[-1])
        else:
            two = (1, math.prod(shp)) if math.prod(shp) < LANE else (math.prod(shp) // LANE, LANE)
        outs = _adamw(w.reshape(two), g.reshape(two), m.reshape(two), v.reshape(two), name)
        return [o.reshape(shp) for o in outs]

    ws = [mix_pre_g, w_in, b_forget, sgu_norm_g, w_spatial, b_spatial, w_out, mix_post_g, ffn_pre_g, w_gate, w_up, w_down, ffn_post_g]
    gs = [grad_mix_pre_g, grad_w_in, grad_b_forget, grad_sgu_norm_g, grad_w_spatial, grad_b_spatial, grad_w_out, grad_mix_post_g,
          grad_ffn_pre_g, grad_w_gate, grad_w_up, grad_w_down, grad_ffn_post_g]
    ms = [m_mix_pre_g, m_w_in, m_b_forget, m_sgu_norm_g, m_w_spatial, m_b_spatial, m_w_out, m_mix_post_g, m_ffn_pre_g, m_w_gate,
          m_w_up, m_w_down, m_ffn_post_g]
    vs = [v_mix_pre_g, v_w_in, v_b_forget, v_sgu_norm_g, v_w_spatial, v_b_spatial, v_w_out, v_mix_post_g, v_ffn_pre_g, v_w_gate,
          v_w_up, v_w_down, v_ffn_post_g]
    deltas, new_ms, new_vs = [], [], []
    for w, g, m, v in zip(ws, gs, ms, vs):
        dl, nm, nv = adam(w, g, m, v, "adamw")
        deltas.append(dl)
        new_ms.append(nm)
        new_vs.append(nv)

    return (loss, grad_x.reshape(x.shape), *gs, *deltas, *new_ms, *new_vs)
```

```python
import functools
import math

import jax
import jax.numpy as jnp
from jax import lax
from jax.experimental import pallas as pl
from jax.experimental.pallas import tpu as pltpu

F32 = jnp.float32
BF16 = jnp.bfloat16

EPS = 1e-6
LANE = 128
SUBLANE = 8
N_CHIPS = 4
N_DEV = 8
VMEM_LIMIT = 48 * 1024 * 1024
MESH = pl.DeviceIdType.MESH

ADAM_LR = 0.001
ADAM_B1 = 0.9
ADAM_B2 = 0.999
ADAM_EPS = 1e-08
ADAM_WD = 0.01
ADAM_STEP = 10

GELU_K = math.sqrt(2.0 / math.pi)
GELU_A = 0.044715
NEG = -1e30


def _cparams(sem=None):
    return pltpu.CompilerParams(dimension_semantics=sem, vmem_limit_bytes=VMEM_LIMIT)


def _tile(n, cap):
    best = None
    for t in range(LANE, min(n, cap) + 1, LANE):
        if n % t == 0:
            best = t
    return best if best is not None else n


def _rows(n, cap):
    best = None
    for t in range(SUBLANE, min(n, cap) + 1, SUBLANE):
        if n % t == 0:
            best = t
    return best if best is not None else n


def _gelu(x):
    t = jnp.tanh(GELU_K * (x + GELU_A * x * x * x))
    return 0.5 * x * (1.0 + t)


def _gelu_and_grad(x):
    x2 = x * x
    t = jnp.tanh(GELU_K * (x + GELU_A * x2 * x))
    g = 0.5 * x * (1.0 + t)
    dg = 0.5 * (1.0 + t) + 0.5 * x * (1.0 - t * t) * (GELU_K * (1.0 + 3.0 * GELU_A * x2))
    return g, dg


def _sigmoid(x):
    return 1.0 / (1.0 + jnp.exp(-x))


def _sum8(v):
    n, d = v.shape
    return v.reshape(n // SUBLANE, SUBLANE, d).sum(axis=0)


_DIMS = {"nn": ((1,), (0,)), "nt": ((1,), (1,)), "tn": ((0,), (0,))}


def _matmul(a, b, mode, out_dtype, name, tm_cap=512, tn_cap=2432, tk_cap=1408):
    if mode == "nn":
        (m, k), (k2, n) = a.shape, b.shape
    elif mode == "nt":
        (m, k), (n, k2) = a.shape, b.shape
    else:
        (k, m), (k2, n) = a.shape, b.shape
    assert k == k2, (a.shape, b.shape, mode)
    tm, tn, tk = _tile(m, tm_cap), _tile(n, tn_cap), _tile(k, tk_cap)
    nk = k // tk
    if mode == "tn":
        a_spec = pl.BlockSpec((tk, tm), lambda j, i, kk: (kk, i))
    else:
        a_spec = pl.BlockSpec((tm, tk), lambda j, i, kk: (i, kk))
    if mode == "nt":
        b_spec = pl.BlockSpec((tn, tk), lambda j, i, kk: (j, kk))
    else:
        b_spec = pl.BlockSpec((tk, tn), lambda j, i, kk: (kk, j))
    dims = (_DIMS[mode], ((), ()))

    def body(a_ref, b_ref, o_ref, *scratch):
        p = lax.dot_general(a_ref[...], b_ref[...], dims, preferred_element_type=F32)
        if nk == 1:
            o_ref[...] = p.astype(out_dtype)
        else:
            acc = scratch[0]
            kk = pl.program_id(2)

            @pl.when(kk == 0)
            def _():
                acc[...] = p

            @pl.when(kk > 0)
            def _():
                acc[...] += p

            @pl.when(kk == nk - 1)
            def _():
                o_ref[...] = acc[...].astype(out_dtype)

    return pl.pallas_call(
        body,
        grid=(n // tn, m // tm, nk),
        in_specs=[a_spec, b_spec],
        out_specs=pl.BlockSpec((tm, tn), lambda j, i, kk: (i, j)),
        out_shape=jax.ShapeDtypeStruct((m, n), out_dtype),
        scratch_shapes=[pltpu.VMEM((tm, tn), F32)] if nk > 1 else [],
        compiler_params=_cparams(("parallel", "parallel", "arbitrary")),
        name=name,
    )(a, b)


def _norm_fwd(x, z, g_post, g_next, name):
    t, d = x.shape
    tt = _rows(t, 512)
    row = pl.BlockSpec((tt, d), lambda i: (i, 0))
    vec = pl.BlockSpec((1, d), lambda i: (0, 0))

    def body(*refs):
        if z is None:
            x_ref, gn_ref, h_ref = refs
            xn = x_ref[...]
        else:
            x_ref, z_ref, gp_ref, gn_ref, xo_ref, h_ref = refs
            zz = z_ref[...]
            r = lax.rsqrt(jnp.mean(zz * zz, axis=-1, keepdims=True) + EPS)
            xn = x_ref[...] + zz * r * gp_ref[...]
            xo_ref[...] = xn
        r2 = lax.rsqrt(jnp.mean(xn * xn, axis=-1, keepdims=True) + EPS)
        h_ref[...] = (xn * r2 * gn_ref[...]).astype(BF16)

    if z is None:
        return pl.pallas_call(
            body, grid=(t // tt,), in_specs=[row, vec], out_specs=row,
            out_shape=jax.ShapeDtypeStruct((t, d), BF16), compiler_params=_cparams(("parallel",)), name=name,
        )(x, g_next)
    return pl.pallas_call(
        body, grid=(t // tt,), in_specs=[row, row, vec, vec], out_specs=[row, row],
        out_shape=[jax.ShapeDtypeStruct((t, d), F32), jax.ShapeDtypeStruct((t, d), BF16)],
        compiler_params=_cparams(("parallel",)), name=name,
    )(x, z, g_post, g_next)


def _rms_bwd(dy, x, g):
    r = lax.rsqrt(jnp.mean(x * x, axis=-1, keepdims=True) + EPS)
    n = x * r
    dn = dy * g
    dx = r * (dn - n * jnp.mean(dn * n, axis=-1, keepdims=True))
    return dx, dy * n


def _norm_bwd(dres, pre, post, name):
    t, d = dres.shape
    tt = _rows(t, 512)
    nt = t // tt
    row = pl.BlockSpec((tt, d), lambda i: (i, 0))
    vec = pl.BlockSpec((1, d), lambda i: (0, 0))
    has_pre, has_post = pre is not None, post is not None
    n_in = 1 + (3 if has_pre else 0) + (2 if has_post else 0)
    n_out = has_pre + has_post + has_pre + has_post

    def body(*refs):
        ins, outs, scr = refs[:n_in], refs[n_in:n_in + n_out], refs[n_in + n_out:]
        i = pl.program_id(0)
        dx = ins[0][...]
        pos, opos, spos = 1, 0, 0
        row_outs, accs, vec_outs = [], [], []
        if has_pre:
            dh_ref, xa_ref, ga_ref = ins[pos:pos + 3]
            pos += 3
            dxa, dga_t = _rms_bwd(dh_ref[...], xa_ref[...], ga_ref[...])
            dx = dx + dxa
            outs[opos][...] = dx
            opos += 1
            accs.append((scr[spos], dga_t))
            spos += 1
        if has_post:
            zb_ref, gb_ref = ins[pos:pos + 2]
            dz, dgb_t = _rms_bwd(dx, zb_ref[...], gb_ref[...])
            outs[opos][...] = dz.astype(BF16)
            opos += 1
            accs.append((scr[spos], dgb_t))
            spos += 1
        for (acc, val), out in zip(accs, outs[opos:]):
            part = _sum8(val)

            @pl.when(i == 0)
            def _(acc=acc, part=part):
                acc[...] = part

            @pl.when(i > 0)
            def _(acc=acc, part=part):
                acc[...] += part

            @pl.when(i == nt - 1)
            def _(acc=acc, out=out):
                out[...] = jnp.sum(acc[...], axis=0, keepdims=True)

    in_specs, args = [row], [dres]
    out_specs, out_shape = [], []
    if has_pre:
        in_specs += [row, row, vec]
        args += list(pre)
        out_specs.append(row)
        out_shape.append(jax.ShapeDtypeStruct((t, d), F32))
    if has_post:
        in_specs += [row, vec]
        args += list(post)
        out_specs.append(row)
        out_shape.append(jax.ShapeDtypeStruct((t, d), BF16))
    for _ in range(has_pre + has_post):
        out_specs.append(vec)
        out_shape.append(jax.ShapeDtypeStruct((1, d), F32))
    return pl.pallas_call(
        body, grid=(nt,), in_specs=in_specs, out_specs=out_specs, out_shape=out_shape,
        scratch_shapes=[pltpu.VMEM((SUBLANE, d), F32)] * (has_pre + has_post),
        compiler_params=_cparams(("arbitrary",)), name=name,
    )(*args)


def _loss_grad(y, target, name):
    t, d = y.shape
    tt = _rows(t, 512)
    nt = t // tt
    row = pl.BlockSpec((tt, d), lambda i: (i, 0))
    inv_d = 1.0 / d

    def body(y_ref, t_ref, dy_ref, l_ref):
        i = pl.program_id(0)
        diff = y_ref[...] - t_ref[...]
        dy_ref[...] = diff * inv_d
        s8 = _sum8(diff * diff)
        part = s8[:, 0:LANE]
        for k in range(1, d // LANE):
            part = part + s8[:, k * LANE:(k + 1) * LANE]
        part = part * (0.5 * inv_d)

        @pl.when(i == 0)
        def _():
            l_ref[...] = part

        @pl.when(i > 0)
        def _():
            l_ref[...] += part

    return pl.pallas_call(
        body, grid=(nt,), in_specs=[row, row],
        out_specs=[row, pl.BlockSpec((SUBLANE, LANE), lambda i: (0, 0))],
        out_shape=[jax.ShapeDtypeStruct((t, d), F32), jax.ShapeDtypeStruct((SUBLANE, LANE), F32)],
        compiler_params=_cparams(("arbitrary",)), name=name,
    )(y, target)


def _swiglu_fwd(ab, name):
    t, f2 = ab.shape
    f = f2 // 2
    tt = _rows(t, 256)

    def body(a_ref, b_ref, m_ref):
        a = a_ref[...]
        m_ref[...] = (a * _sigmoid(a) * b_ref[...]).astype(BF16)

    return pl.pallas_call(
        body, grid=(t // tt,),
        in_specs=[pl.BlockSpec((tt, f), lambda i: (i, 0)), pl.BlockSpec((tt, f), lambda i: (i, 1))],
        out_specs=pl.BlockSpec((tt, f), lambda i: (i, 0)),
        out_shape=jax.ShapeDtypeStruct((t, f), BF16), compiler_params=_cparams(("parallel",)), name=name,
    )(ab, ab)


def _swiglu_bwd(ab, dm, name):
    t, f2 = ab.shape
    f = f2 // 2
    tt = _rows(t, 256)

    def body(a_ref, b_ref, dm_ref, dab_ref):
        a = a_ref[...]
        s = _sigmoid(a)
        d = dm_ref[...]
        dab_ref[:, 0:f] = (d * b_ref[...] * s * (1.0 + a * (1.0 - s))).astype(BF16)
        dab_ref[:, f:f2] = (d * a * s).astype(BF16)

    blk0 = pl.BlockSpec((tt, f), lambda i: (i, 0))
    blk1 = pl.BlockSpec((tt, f), lambda i: (i, 1))
    return pl.pallas_call(
        body, grid=(t // tt,), in_specs=[blk0, blk1, blk0], out_specs=pl.BlockSpec((tt, f2), lambda i: (i, 0)),
        out_shape=jax.ShapeDtypeStruct((t, f2), BF16), compiler_params=_cparams(("parallel",)), name=name,
    )(ab, ab, dm)


def _log_sigmoid(x):
    return jnp.minimum(x, 0.0) - jnp.log1p(jnp.exp(-jnp.abs(x)))


def _fox_prep(f_t, b_f, name):
    h, t = f_t.shape

    def body(f_ref, b_ref, c_ref):
        r = lax.broadcasted_iota(jnp.int32, (LANE, LANE), 0)
        c = lax.broadcasted_iota(jnp.int32, (LANE, LANE), 1)
        upper = (r <= c).astype(F32)
        carry = jnp.zeros((h, 1), F32)
        for j in range(t // LANE):
            sl = slice(j * LANE, (j + 1) * LANE)
            lf = _log_sigmoid(f_ref[:, sl] + b_ref[...])
            cs = jnp.dot(lf, upper, precision=lax.Precision.HIGHEST, preferred_element_type=F32) + carry
            c_ref[:, sl] = cs
            carry = cs[:, LANE - 1:LANE]

    return pl.pallas_call(body, out_shape=jax.ShapeDtypeStruct((h, t), F32), compiler_params=_cparams(), name=name)(f_t, b_f)


def _fox_bwd(dc_q, dc_k, f_t, b_f, name):
    h, t = f_t.shape

    def body(dq_ref, dk_ref, f_ref, b_ref, df_ref, db_ref):
        r = lax.broadcasted_iota(jnp.int32, (LANE, LANE), 0)
        c = lax.broadcasted_iota(jnp.int32, (LANE, LANE), 1)
        lower = (r >= c).astype(F32)
        carry = jnp.zeros((h, 1), F32)
        dbsum = jnp.zeros((h, 1), F32)
        for j in reversed(range(t // LANE)):
            sl = slice(j * LANE, (j + 1) * LANE)
            dc = dq_ref[:, sl] - dk_ref[:, sl]
            dl = jnp.dot(dc, lower, precision=lax.Precision.HIGHEST, preferred_element_type=F32) + carry
            carry = dl[:, 0:1]
            df = dl * _sigmoid(-(f_ref[:, sl] + b_ref[...]))
            df_ref[:, sl] = df
            dbsum = dbsum + jnp.sum(df, axis=-1, keepdims=True)
        db_ref[...] = dbsum

    return pl.pallas_call(
        body, out_shape=[jax.ShapeDtypeStruct((h, t), F32), jax.ShapeDtypeStruct((h, 1), F32)],
        compiler_params=_cparams(), name=name,
    )(dc_q, dc_k, f_t, b_f)


def _nt_dot(a, b):
    return lax.dot_general(a, b, (((1,), (1,)), ((), ())), preferred_element_type=F32)


LOG2E = 1.4426950408889634
LN2 = 0.6931471805599453
ATTN_Q = 512
ATTN_K = 256


def _attn_tiles(t):
    return _tile(t, ATTN_Q), _tile(t, ATTN_K)


def _attn_fwd(proj, c_col, c_row, d, name):
    t = proj.shape[0]
    h = d // LANE
    bq, bk = _attn_tiles(t)
    nq, nk, rr = t // bq, t // bk, bq // bk
    qc, kc, vc = 4 * h, 5 * h, 6 * h
    qscale = LANE ** -0.5 * LOG2E

    def body(q_ref, k_ref, v_ref, cc_ref, cr_ref, o_ref, lse_ref, kb, vt, ckb, acc):
        i = pl.program_id(1)

        @pl.when(i == 0)
        def _():
            kb[...] = k_ref[...].astype(BF16)
            ckb[...] = jnp.broadcast_to(cc_ref[0] * LOG2E, (t, bq))
            for jn in range(nk):
                vt[jn] = v_ref[jn * bk:(jn + 1) * bk, :].T.astype(BF16)

        q = (q_ref[...] * qscale).astype(BF16)
        cq = cr_ref[0, 0] * LOG2E
        acc[...] = jnp.zeros((LANE, bq), F32)

        def block(j, diag, m_old, l_old):
            rows = pl.ds(pl.multiple_of(j * bk, bk), bk)
            s = _nt_dot(kb[rows, :], q) - ckb[rows, :]
            if diag is not None:
                kk = lax.broadcasted_iota(jnp.int32, (bk, bq), 0)
                qq = lax.broadcasted_iota(jnp.int32, (bk, bq), 1)
                s = jnp.where(qq >= kk + diag * bk, s, NEG)
            m_new = jnp.maximum(m_old, jnp.max(s, axis=0, keepdims=True) + cq)
            p = jnp.exp2(s + (cq - m_new))
            alpha = jnp.exp2(m_old - m_new)
            l_new = alpha * l_old + jnp.sum(p, axis=0, keepdims=True)
            acc[...] = alpha * acc[...] + jnp.dot(vt[j], p.astype(BF16), preferred_element_type=F32)
            return m_new, l_new

        m, l = lax.fori_loop(0, i * rr, lambda j, c: block(j, None, *c),
                             (jnp.full((1, bq), NEG, F32), jnp.zeros((1, bq), F32)))
        for jj in range(rr):
            m, l = block(i * rr + jj, jj, m, l)
        o_ref[...] = (acc[...] / l).T
        lse_ref[0, 0] = m + jnp.log2(l)

    rowq = pl.BlockSpec((1, 1, 1, bq), lambda hh, i: (hh, i, 0, 0))
    return pl.pallas_call(
        body, grid=(h, nq),
        in_specs=[
            pl.BlockSpec((bq, LANE), lambda hh, i: (i, qc + hh)),
            pl.BlockSpec((t, LANE), lambda hh, i: (0, kc + hh)),
            pl.BlockSpec((t, LANE), lambda hh, i: (0, vc + hh)),
            pl.BlockSpec((1, t, 1), lambda hh, i: (hh, 0, 0)),
            rowq,
        ],
        out_specs=[pl.BlockSpec((bq, LANE), lambda hh, i: (i, hh)), rowq],
        out_shape=[jax.ShapeDtypeStruct((t, d), F32), jax.ShapeDtypeStruct((h, nq, 1, bq), F32)],
        scratch_shapes=[pltpu.VMEM((t, LANE), BF16), pltpu.VMEM((nk, LANE, bk), BF16), pltpu.VMEM((t, bq), F32),
                        pltpu.VMEM((LANE, bq), F32)],
        compiler_params=_cparams(("arbitrary", "arbitrary")), name=name,
    )(proj, proj, proj, c_col, c_row)


def _attn_bwd(proj, do, o, lse, c_col, c_row, d, name):
    t = proj.shape[0]
    h = d // LANE
    bq, bk = _attn_tiles(t)
    nq, nk, rr = t // bq, t // bk, bq // bk
    qc, kc, vc = 4 * h, 5 * h, 6 * h
    scale = LANE ** -0.5

    def body(q_ref, k_ref, v_ref, do_ref, o_ref, lse_ref, cc_ref, cr_ref, dq_ref, dk_ref, dv_ref, dcq_ref, dck_ref,
             kb, kt, vb, ckb, dk_acc, dv_acc, dck_acc, dqt_acc):
        i = pl.program_id(1)

        @pl.when(i == 0)
        def _():
            kb[...] = k_ref[...].astype(BF16)
            vb[...] = v_ref[...].astype(BF16)
            ckb[...] = jnp.broadcast_to(cc_ref[0] * LOG2E, (t, bq))
            for jn in range(nk):
                kt[jn] = k_ref[jn * bk:(jn + 1) * bk, :].T.astype(BF16)
            dk_acc[...] = jnp.zeros((t, LANE), F32)
            dv_acc[...] = jnp.zeros((t, LANE), F32)
            dck_acc[...] = jnp.zeros((t, LANE), F32)

        q = (q_ref[...] * (scale * LOG2E)).astype(BF16)
        dof = do_ref[...]
        dob = dof.astype(BF16)
        delta = jnp.sum((dof * o_ref[...]).T, axis=0, keepdims=True)
        rowb = cr_ref[0, 0] * LOG2E - lse_ref[0, 0]
        dqt_acc[...] = jnp.zeros((LANE, bq), F32)

        def block(j, diag, dcq):
            rows = pl.ds(pl.multiple_of(j * bk, bk), bk)
            p = jnp.exp2(_nt_dot(kb[rows, :], q) - ckb[rows, :] + rowb)
            if diag is not None:
                kk = lax.broadcasted_iota(jnp.int32, (bk, bq), 0)
                qq = lax.broadcasted_iota(jnp.int32, (bk, bq), 1)
                p = jnp.where(qq >= kk + diag * bk, p, 0.0)
            dv_acc[rows, :] += jnp.dot(p.astype(BF16), dob, preferred_element_type=F32)
            ds = p * (_nt_dot(vb[rows, :], dob) - delta)
            dsb = ds.astype(BF16)
            dk_acc[rows, :] += jnp.dot(dsb, q, preferred_element_type=F32)
            dqt_acc[...] += jnp.dot(kt[j], dsb, preferred_element_type=F32)
            part = ds[:, 0:LANE]
            for k in range(1, bq // LANE):
                part = part + ds[:, k * LANE:(k + 1) * LANE]
            dck_acc[rows, :] += part
            return dcq + jnp.sum(ds, axis=0, keepdims=True)

        dcq = lax.fori_loop(0, i * rr, lambda j, c: block(j, None, c), jnp.zeros((1, bq), F32))
        for jj in range(rr):
            dcq = block(i * rr + jj, jj, dcq)
        dq_ref[...] = (dqt_acc[...] * scale).T.astype(BF16)
        dcq_ref[0, 0] = dcq

        @pl.when(i == nq - 1)
        def _():
            dk_ref[...] = (dk_acc[...] * LN2).astype(BF16)
            dv_ref[...] = dv_acc[...].astype(BF16)
            dck_ref[0] = jnp.sum(dck_acc[...], axis=-1, keepdims=True)

    rowq = pl.BlockSpec((1, 1, 1, bq), lambda hh, i: (hh, i, 0, 0))
    blk = pl.BlockSpec((bq, LANE), lambda hh, i: (i, hh))
    whole = pl.BlockSpec((t, LANE), lambda hh, i: (0, hh))
    colk = pl.BlockSpec((1, t, 1), lambda hh, i: (hh, 0, 0))
    return pl.pallas_call(
        body, grid=(h, nq),
        in_specs=[
            pl.BlockSpec((bq, LANE), lambda hh, i: (i, qc + hh)),
            pl.BlockSpec((t, LANE), lambda hh, i: (0, kc + hh)),
            pl.BlockSpec((t, LANE), lambda hh, i: (0, vc + hh)),
            blk, blk, rowq, colk, rowq,
        ],
        out_specs=[blk, whole, whole, rowq, colk],
        out_shape=[jax.ShapeDtypeStruct((t, d), BF16), jax.ShapeDtypeStruct((t, d), BF16), jax.ShapeDtypeStruct((t, d), BF16),
                   jax.ShapeDtypeStruct((h, nq, 1, bq), F32), jax.ShapeDtypeStruct((h, t, 1), F32)],
        scratch_shapes=[pltpu.VMEM((t, LANE), BF16), pltpu.VMEM((nk, LANE, bk), BF16), pltpu.VMEM((t, LANE), BF16),
                        pltpu.VMEM((t, bq), F32), pltpu.VMEM((t, LANE), F32), pltpu.VMEM((t, LANE), F32),
                        pltpu.VMEM((t, LANE), F32), pltpu.VMEM((LANE, bq), F32)],
        compiler_params=_cparams(("arbitrary", "arbitrary")), name=name,
    )(proj, proj, proj, do, o, lse, c_col, c_row)


def _sgu_forward(u_ref, v_ref, gv_ref, wm_ref, bs_ref, mix_sc, groups):
    gu, dgu = _gelu_and_grad(u_ref[...])
    gvv, dgv = _gelu_and_grad(v_ref[...])
    mu = jnp.mean(gvv, axis=-1, keepdims=True)
    xc = gvv - mu
    r = lax.rsqrt(jnp.mean(xc * xc, axis=-1, keepdims=True) + EPS)
    nhat = xc * r
    vn = (nhat * gv_ref[...]).astype(BF16)
    for g in range(groups):
        sl = slice(g * LANE, (g + 1) * LANE)
        mix_sc[:, sl] = jnp.dot(wm_ref[g], vn[:, sl], preferred_element_type=F32) + bs_ref[g]
    return gu, dgu, dgv, nhat, r, vn, mix_sc[...]


def _mix_fwd(proj, o, wm, bs, g_v, d, name):
    t = proj.shape[0]
    groups = d // LANE

    def body(u_ref, v_ref, ga_ref, gb_ref, o_ref, wm_ref, bs_ref, gv_ref, out_ref, mix_sc):
        gu, _, _, _, _, _, mixed = _sgu_forward(u_ref, v_ref, gv_ref, wm_ref, bs_ref, mix_sc, groups)
        out_ref[...] = (_sigmoid(ga_ref[...]) * (gu * mixed) + _sigmoid(gb_ref[...]) * o_ref[...]).astype(BF16)

    def colblk(k):
        return pl.BlockSpec((LANE, d), lambda i, k=k: (i, k))

    full3 = pl.BlockSpec((groups, LANE, LANE), lambda i: (0, 0, 0))
    return pl.pallas_call(
        body, grid=(t // LANE,),
        in_specs=[colblk(0), colblk(1), colblk(2), colblk(3), colblk(0), full3,
                  pl.BlockSpec((groups, LANE, 1), lambda i: (0, 0, 0)), pl.BlockSpec((1, d), lambda i: (0, 0))],
        out_specs=colblk(0),
        out_shape=jax.ShapeDtypeStruct((t, d), BF16),
        scratch_shapes=[pltpu.VMEM((LANE, d), F32)],
        compiler_params=_cparams(("parallel",)), name=name,
    )(proj, proj, proj, proj, o, wm, bs, g_v)


def _mix_bwd(dmerged, proj, o, wm, wm_t, bs, g_v, d, name):
    t = proj.shape[0]
    groups = d // LANE
    nt = t // LANE

    def body(dm_ref, u_ref, v_ref, ga_ref, gb_ref, o_ref, wm_ref, wmt_ref, bs_ref, gv_ref,
             da_ref, do_ref, dws_ref, dbs_ref, dgv_ref, mix_sc, dvn_sc, gv_acc):
        i = pl.program_id(0)

        @pl.when(i == 0)
        def _():
            dws_ref[...] = jnp.zeros_like(dws_ref)
            dbs_ref[...] = jnp.zeros_like(dbs_ref)
            gv_acc[...] = jnp.zeros_like(gv_acc)

        gu, dgu, dgv, nhat, r, vn, mixed = _sgu_forward(u_ref, v_ref, gv_ref, wm_ref, bs_ref, mix_sc, groups)
        dm = dm_ref[...]
        sa = _sigmoid(ga_ref[...])
        sb = _sigmoid(gb_ref[...])
        ov = o_ref[...]
        y_a = gu * mixed
        da_ref[:, 2 * d:3 * d] = (dm * y_a * sa * (1.0 - sa)).astype(BF16)
        da_ref[:, 3 * d:4 * d] = (dm * ov * sb * (1.0 - sb)).astype(BF16)
        do_ref[...] = dm * sb
        dy_a = dm * sa
        da_ref[:, 0:d] = (dy_a * mixed * dgu).astype(BF16)
        dmixed = dy_a * gu
        dmixed_b = dmixed.astype(BF16)
        for g in range(groups):
            sl = slice(g * LANE, (g + 1) * LANE)
            dvn_sc[:, sl] = jnp.dot(wmt_ref[g], dmixed_b[:, sl], preferred_element_type=F32)
            dws_ref[g] += _nt_dot(dmixed_b[:, sl], vn[:, sl])
            dbs_ref[g] += jnp.sum(dmixed[:, sl], axis=-1, keepdims=True)
        dvn = dvn_sc[...]
        gv_acc[...] += _sum8(dvn * nhat)
        dn = dvn * gv_ref[...]
        dgelu = r * (dn - jnp.mean(dn, axis=-1, keepdims=True) - nhat * jnp.mean(dn * nhat, axis=-1, keepdims=True))
        da_ref[:, d:2 * d] = (dgelu * dgv).astype(BF16)

        @pl.when(i == nt - 1)
        def _():
            dgv_ref[...] = jnp.sum(gv_acc[...], axis=0, keepdims=True)
            rr = lax.broadcasted_iota(jnp.int32, (LANE, LANE), 0)
            cl = lax.broadcasted_iota(jnp.int32, (LANE, LANE), 1)
            for g in range(groups):
                dws_ref[g] = jnp.where(rr >= cl, dws_ref[g], 0.0)

    def colblk(k):
        return pl.BlockSpec((LANE, d), lambda i, k=k: (i, k))

    full3 = pl.BlockSpec((groups, LANE, LANE), lambda i: (0, 0, 0))
    col3 = pl.BlockSpec((groups, LANE, 1), lambda i: (0, 0, 0))
    vec = pl.BlockSpec((1, d), lambda i: (0, 0))
    return pl.pallas_call(
        body, grid=(nt,),
        in_specs=[colblk(0), colblk(0), colblk(1), colblk(2), colblk(3), colblk(0), full3, full3, col3, vec],
        out_specs=[pl.BlockSpec((LANE, 4 * d), lambda i: (i, 0)), colblk(0), full3, col3, vec],
        out_shape=[jax.ShapeDtypeStruct((t, 4 * d), BF16), jax.ShapeDtypeStruct((t, d), F32),
                   jax.ShapeDtypeStruct((groups, LANE, LANE), F32), jax.ShapeDtypeStruct((groups, LANE, 1), F32),
                   jax.ShapeDtypeStruct((1, d), F32)],
        scratch_shapes=[pltpu.VMEM((LANE, d), F32), pltpu.VMEM((LANE, d), F32), pltpu.VMEM((SUBLANE, d), F32)],
        compiler_params=_cparams(("arbitrary",)), name=name,
    )(dmerged, proj, proj, proj, proj, o, wm, wm_t, bs, g_v)


def _adamw(w, g, m, v, name):
    r, c = w.shape
    cap = max(SUBLANE, (2 * 1024 * 1024) // (4 * c) // SUBLANE * SUBLANE)
    tr = _rows(r, cap)
    c1 = 1.0 / (1.0 - ADAM_B1 ** ADAM_STEP)
    c2 = 1.0 / (1.0 - ADAM_B2 ** ADAM_STEP)

    def body(w_ref, g_ref, m_ref, v_ref, d_ref, nm_ref, nv_ref):
        gg = g_ref[...]
        nm = ADAM_B1 * m_ref[...] + (1.0 - ADAM_B1) * gg
        nv = ADAM_B2 * v_ref[...] + (1.0 - ADAM_B2) * (gg * gg)
        nm_ref[...] = nm
        nv_ref[...] = nv
        d_ref[...] = -ADAM_LR * ((nm * c1) / (jnp.sqrt(nv * c2) + ADAM_EPS) + ADAM_WD * w_ref[...])

    blk = pl.BlockSpec((tr, c), lambda i: (i, 0))
    return pl.pallas_call(
        body, grid=(r // tr,), in_specs=[blk] * 4, out_specs=[blk] * 3,
        out_shape=[jax.ShapeDtypeStruct((r, c), F32)] * 3, compiler_params=_cparams(("parallel",)), name=name,
    )(w, g, m, v)


def _adamw_layers(w, g_mine, g_sib, m, v, c_idx, name):
    _, r, c = w.shape
    cap = max(SUBLANE, (1024 * 1024) // (4 * c) // SUBLANE * SUBLANE)
    tr = _rows(r, cap)
    c1 = 1.0 / (1.0 - ADAM_B1 ** ADAM_STEP)
    c2 = 1.0 / (1.0 - ADAM_B2 ** ADAM_STEP)

    def body(c_ref, w_ref, gm_ref, gs_ref, m_ref, v_ref, g_ref, d_ref, nm_ref, nv_ref):
        gg = jnp.where(pl.program_id(0) == c_ref[0], gm_ref[...], gs_ref[...])
        g_ref[0] = gg
        nm = ADAM_B1 * m_ref[0] + (1.0 - ADAM_B1) * gg
        nv = ADAM_B2 * v_ref[0] + (1.0 - ADAM_B2) * (gg * gg)
        nm_ref[0] = nm
        nv_ref[0] = nv
        d_ref[0] = -ADAM_LR * ((nm * c1) / (jnp.sqrt(nv * c2) + ADAM_EPS) + ADAM_WD * w_ref[0])

    lay = pl.BlockSpec((1, tr, c), lambda l, i, cr: (l, i, 0))
    flat = pl.BlockSpec((tr, c), lambda l, i, cr: (i, 0))
    return pl.pallas_call(
        body,
        grid_spec=pltpu.PrefetchScalarGridSpec(
            num_scalar_prefetch=1, grid=(2, r // tr), in_specs=[lay, flat, flat, lay, lay], out_specs=[lay] * 4),
        out_shape=[jax.ShapeDtypeStruct((2, r, c), F32)] * 4, compiler_params=_cparams(("parallel", "parallel")), name=name,
    )(c_idx, w, g_mine, g_sib, m, v)


def _add_mine(g_all, recv, c_idx, name):
    _, r, c = g_all.shape
    cap = max(16, (2 * 1024 * 1024) // (2 * c) // 16 * 16)
    tr = _rows(r, cap)

    def body(c_ref, a_ref, b_ref, o_ref):
        o_ref[...] = (a_ref[0].astype(F32) + b_ref[...].astype(F32)).astype(BF16)

    return pl.pallas_call(
        body,
        grid_spec=pltpu.PrefetchScalarGridSpec(
            num_scalar_prefetch=1, grid=(r // tr,),
            in_specs=[pl.BlockSpec((1, tr, c), lambda i, cr: (cr[0], i, 0)), pl.BlockSpec((tr, c), lambda i, cr: (i, 0))],
            out_specs=pl.BlockSpec((tr, c), lambda i, cr: (i, 0)),
        ),
        out_shape=jax.ShapeDtypeStruct((r, c), BF16), compiler_params=_cparams(("parallel",)), name=name,
    )(c_idx, g_all, recv)


def _sum_slots(x, own, sel, name):
    s, r, c = x.shape
    cap = max(16, (1024 * 1024) // (4 * c) // 16 * 16)
    tr = _rows(r, cap) if r % 16 == 0 else r

    def body(sel_ref, x_ref, own_ref, o_ref):
        mine = own_ref[0].astype(F32)
        acc = jnp.zeros((tr, c), F32)
        for k in range(s):
            acc = acc + jnp.where(sel_ref[0] == k, mine, x_ref[k].astype(F32))
        o_ref[...] = acc

    return pl.pallas_call(
        body,
        grid_spec=pltpu.PrefetchScalarGridSpec(
            num_scalar_prefetch=1, grid=(r // tr,),
            in_specs=[pl.BlockSpec((s, tr, c), lambda i, sr: (0, i, 0)), pl.BlockSpec((1, tr, c), lambda i, sr: (sr[1], i, 0))],
            out_specs=pl.BlockSpec((tr, c), lambda i, sr: (i, 0)),
        ),
        out_shape=jax.ShapeDtypeStruct((r, c), F32), compiler_params=_cparams(("parallel",)), name=name,
    )(sel, x, own)


_HBM = pl.BlockSpec(memory_space=pl.ANY)


def _place():
    x, y, c = lax.axis_index("x"), lax.axis_index("y"), lax.axis_index("c")
    chips = [(1 - x, y), (x, 1 - y), (1 - x, 1 - y)]
    return x, y, c, chips


def _gather_weights(ws, name):
    n = len(ws)

    def body(*refs):
        w_refs, o_refs = refs[:n], refs[n:2 * n]
        ssem, rsem = refs[2 * n:]
        x, y, c, chips = _place()
        me = 2 * x + y
        sib = (x, y, 1 - c)

        def copy(a, k, src, dst, to):
            return pltpu.make_async_remote_copy(src_ref=src, dst_ref=dst, send_sem=ssem.at[a, k], recv_sem=rsem.at[a, k],
                                                device_id=to, device_id_type=MESH)

        sends = []
        for a in range(n):
            for j, chip in enumerate(chips):
                cp = copy(a, j, w_refs[a].at[c], o_refs[a].at[c, me], (chip[0], chip[1], c))
                cp.start()
                sends.append(cp)
        for a in range(n):
            for j, chip in enumerate(chips):
                slot = o_refs[a].at[c, 2 * chip[0] + chip[1]]
                copy(a, j, slot, slot, sib).wait_recv()
                cp = copy(a, 3 + j, slot, slot, sib)
                cp.start()
                sends.append(cp)
        for a in range(n):
            for j, chip in enumerate(chips):
                slot = o_refs[a].at[1 - c, 2 * chip[0] + chip[1]]
                copy(a, 3 + j, slot, slot, sib).wait_recv()
        for cp in sends:
            cp.wait_send()

    return pl.pallas_call(
        body, in_specs=[_HBM] * n, out_specs=[_HBM] * n,
        out_shape=[jax.ShapeDtypeStruct((2, N_CHIPS) + w.shape[1:], w.dtype) for w in ws],
        scratch_shapes=[pltpu.SemaphoreType.DMA((n, 6)), pltpu.SemaphoreType.DMA((n, 6))],
        name=name,
    )(*ws)


def _swap_other_layer(gs, name):
    n = len(gs)

    def body(*refs):
        g_refs, o_refs = refs[:n], refs[n:2 * n]
        ssem, rsem = refs[2 * n:]
        x, y, c, _ = _place()
        cps = []
        for a in range(n):
            cp = pltpu.make_async_remote_copy(src_ref=g_refs[a].at[1 - c], dst_ref=o_refs[a], send_sem=ssem.at[a],
                                              recv_sem=rsem.at[a], device_id=(x, y, 1 - c), device_id_type=MESH)
            cp.start()
            cps.append(cp)
        for cp in cps:
            cp.wait()

    return pl.pallas_call(
        body, in_specs=[_HBM] * n, out_specs=[_HBM] * n,
        out_shape=[jax.ShapeDtypeStruct(g.shape[1:], g.dtype) for g in gs],
        scratch_shapes=[pltpu.SemaphoreType.DMA((n,)), pltpu.SemaphoreType.DMA((n,))], name=name,
    )(*gs)


def _scatter_to_chips(ps, name):
    n = len(ps)

    def body(*refs):
        p_refs, o_refs = refs[:n], refs[n:2 * n]
        ssem, rsem = refs[2 * n:]
        x, y, c, chips = _place()
        me = 2 * x + y
        sends = []
        for a in range(n):
            for j, chip in enumerate(chips):
                cp = pltpu.make_async_remote_copy(
                    src_ref=p_refs[a].at[2 * chip[0] + chip[1]], dst_ref=o_refs[a].at[me], send_sem=ssem.at[a, j],
                    recv_sem=rsem.at[a, j], device_id=(chip[0], chip[1], c), device_id_type=MESH)
                cp.start()
                sends.append(cp)
        for a in range(n):
            for j, chip in enumerate(chips):
                slot = o_refs[a].at[2 * chip[0] + chip[1]]
                pltpu.make_async_remote_copy(src_ref=slot, dst_ref=slot, send_sem=ssem.at[a, j], recv_sem=rsem.at[a, j],
                                             device_id=(x, y, c), device_id_type=MESH).wait_recv()
        for cp in sends:
            cp.wait_send()

    return pl.pallas_call(
        body, in_specs=[_HBM] * n, out_specs=[_HBM] * n,
        out_shape=[jax.ShapeDtypeStruct(p.shape, p.dtype) for p in ps],
        scratch_shapes=[pltpu.SemaphoreType.DMA((n, 3)), pltpu.SemaphoreType.DMA((n, 3))],
        name=name,
    )(*ps)


def _swap_with_sibling(fs, name):
    n = len(fs)

    def body(*refs):
        f_refs, o_refs = refs[:n], refs[n:2 * n]
        ssem, rsem = refs[2 * n:]
        x, y, c, _ = _place()
        cps = []
        for a in range(n):
            cp = pltpu.make_async_remote_copy(src_ref=f_refs[a], dst_ref=o_refs[a], send_sem=ssem.at[a],
                                              recv_sem=rsem.at[a], device_id=(x, y, 1 - c), device_id_type=MESH)
            cp.start()
            cps.append(cp)
        for cp in cps:
            cp.wait()

    return pl.pallas_call(
        body, in_specs=[_HBM] * n, out_specs=[_HBM] * n,
        out_shape=[jax.ShapeDtypeStruct(f.shape, f.dtype) for f in fs],
        scratch_shapes=[pltpu.SemaphoreType.DMA((n,)), pltpu.SemaphoreType.DMA((n,))],
        name=name,
    )(*fs)


def _gather_all(buf, name):
    def body(b_ref, o_ref, ssem, rsem):
        x, y, c, _ = _place()
        me = 4 * x + 2 * y + c
        flips = [(fx, fy, fc) for fx in (0, 1) for fy in (0, 1) for fc in (0, 1)][1:]
        peers = [((1 - x) if fx else x, (1 - y) if fy else y, (1 - c) if fc else c) for fx, fy, fc in flips]
        sends = []
        for k, peer in enumerate(peers):
            cp = pltpu.make_async_remote_copy(src_ref=b_ref, dst_ref=o_ref.at[me], send_sem=ssem.at[k], recv_sem=rsem.at[k],
                                              device_id=peer, device_id_type=MESH)
            cp.start()
            sends.append(cp)
        for k, peer in enumerate(peers):
            slot = o_ref.at[4 * peer[0] + 2 * peer[1] + peer[2]]
            pltpu.make_async_remote_copy(src_ref=slot, dst_ref=slot, send_sem=ssem.at[k], recv_sem=rsem.at[k],
                                         device_id=(x, y, c), device_id_type=MESH).wait_recv()
        for cp in sends:
            cp.wait_send()

    return pl.pallas_call(
        body, in_specs=[_HBM], out_specs=_HBM,
        out_shape=jax.ShapeDtypeStruct((N_DEV,) + buf.shape, buf.dtype),
        scratch_shapes=[pltpu.SemaphoreType.DMA((N_DEV - 1,)), pltpu.SemaphoreType.DMA((N_DEV - 1,))],
        name=name,
    )(buf)


def _layer_forward(x, h, wts, sm, d):
    t = x.shape[0]
    heads = d // LANE
    bq, _ = _attn_tiles(t)
    proj = _matmul(h, wts["w_cat"], "nn", F32, "proj_fwd")
    f_t = proj[:, 7 * d:7 * d + heads].T
    c_t = _fox_prep(f_t, sm["b_f"], "fox_prep")
    c_col = c_t.reshape(heads, t, 1)
    c_row = c_t.reshape(heads, t // bq, 1, bq)
    o, lse = _attn_fwd(proj, c_col, c_row, d, "attn_fwd")
    merged = _mix_fwd(proj, o, sm["wm"], sm["bs"], sm["g_v"], d, "mix_fwd")
    z = _matmul(merged, wts["w_out"], "nn", F32, "out_fwd")
    x1, h2 = _norm_fwd(x, z, sm["g_post"], sm["g_fpre"], "norm_mid")
    ab = _matmul(h2, wts["w_gu"], "nn", F32, "gu_fwd")
    mm = _swiglu_fwd(ab, "swiglu_fwd")
    z2 = _matmul(mm, wts["w_d"], "nn", F32, "down_fwd")
    return dict(x=x, h=h, proj=proj, f_t=f_t, c_col=c_col, c_row=c_row, o=o, lse=lse, merged=merged, z=z, x1=x1,
                h2=h2, ab=ab, mm=mm, z2=z2)


def _layer_backward(dz2, dx2, sv, wts, sm, d):
    t = dx2.shape[0]
    heads = d // LANE
    g = {}
    dm = _matmul(dz2, wts["w_d"], "nt", F32, "down_bwd_x", tn_cap=1408, tk_cap=1024)
    g["w_d"] = _matmul(sv["mm"], dz2, "tn", F32, "down_bwd_w", tk_cap=1024)
    dab = _swiglu_bwd(sv["ab"], dm, "swiglu_bwd")
    dh2 = _matmul(dab, wts["w_gu"], "nt", F32, "gu_bwd_x", tn_cap=1024, tk_cap=1408)
    g["w_gu"] = _matmul(sv["h2"], dab, "tn", F32, "gu_bwd_w", tk_cap=1024)
    dx1, dz, g["g_fpre"], g["g_post"] = _norm_bwd(dx2, (dh2, sv["x1"], sm["g_fpre"]), (sv["z"], sm["g_post"]), "norm_bwd_mid")
    dmerged = _matmul(dz, wts["w_out"], "nt", F32, "out_bwd_x", tk_cap=1024)
    g["w_out"] = _matmul(sv["merged"], dz, "tn", F32, "out_bwd_w", tk_cap=1024)
    d_a, do, g["w_s"], g["b_s"], g["g_v"] = _mix_bwd(dmerged, sv["proj"], sv["o"], sm["wm"], sm["wm_t"], sm["bs"], sm["g_v"], d,
                                                      "mix_bwd")
    dq, dk, dv, dc_q, dc_k = _attn_bwd(sv["proj"], do, sv["o"], sv["lse"], sv["c_col"], sv["c_row"], d, "attn_bwd")
    df_t, g["b_f"] = _fox_bwd(dc_q.reshape(heads, t), dc_k.reshape(heads, t), sv["f_t"], sm["b_f"], "fox_bwd")
    df = jnp.pad(df_t.T, ((0, 0), (0, LANE - heads))).astype(BF16)
    dproj = jnp.concatenate([d_a, dq, dk, dv, df], axis=1)
    dh = _matmul(dproj, wts["w_cat"], "nt", F32, "proj_bwd_x", tn_cap=1024, tk_cap=2432)
    g["w_cat"] = _matmul(sv["h"], dproj, "tn", F32, "proj_bwd_w", tk_cap=1024)
    return dh, dx1, g


def _small_pack(parts):
    flat = jnp.concatenate([p.reshape(-1) for p in parts])
    n = flat.shape[0]
    pad = (-n) % (16 * LANE)
    return jnp.pad(flat, (0, pad)).reshape(-1, LANE)


def kernel(x, mix_pre_g, w_in, b_forget, sgu_norm_g, w_spatial, b_spatial, w_out, mix_post_g, ffn_pre_g, w_gate, w_up, w_down, ffn_post_g, loss_target, m_mix_pre_g, m_w_in, m_b_forget, m_sgu_norm_g, m_w_spatial, m_b_spatial, m_w_out, m_mix_post_g, m_ffn_pre_g, m_w_gate, m_w_up, m_w_down, m_ffn_post_g, v_mix_pre_g, v_w_in, v_b_forget, v_sgu_norm_g, v_w_spatial, v_b_spatial, v_w_out, v_mix_post_g, v_ffn_pre_g, v_w_gate, v_w_up, v_w_down, v_ffn_post_g):
    depth, d = mix_pre_g.shape
    assert depth == 2, "core c of a chip owns layer c"
    heads = d // LANE
    t = x.shape[1]
    ff = w_down.shape[1] * N_CHIPS
    in_w = w_in.shape[2] * N_CHIPS
    assert in_w == 7 * d + heads
    xs = x.reshape(t, d)
    target = loss_target.reshape(t, d)
    c_idx = lax.axis_index("c").astype(jnp.int32).reshape(1)

    chip = 2 * lax.axis_index("x") + lax.axis_index("y")
    dev = 2 * chip + lax.axis_index("c")
    own = [w.astype(BF16) for w in (w_in, w_out, w_gate, w_up, w_down)]
    gathered = _gather_weights(own, "gather_weights")

    def shard(a, l, s):
        return jnp.where(chip == s, own[a][l], gathered[a][l, s])

    wts = []
    for l in range(depth):
        full_in = jnp.concatenate([shard(0, l, s) for s in range(N_CHIPS)], axis=1)
        w_cat = jnp.concatenate(
            [full_in[:, 0:2 * d], full_in[:, 5 * d:7 * d], full_in[:, 2 * d:5 * d], full_in[:, 7 * d:],
             jnp.zeros((d, LANE - heads), BF16)], axis=1)
        w_gu = jnp.concatenate([shard(2, l, s) for s in range(N_CHIPS)] + [shard(3, l, s) for s in range(N_CHIPS)], axis=1)
        wts.append(dict(w_cat=w_cat, w_out=jnp.concatenate([shard(1, l, s) for s in range(N_CHIPS)], axis=0), w_gu=w_gu,
                        w_d=jnp.concatenate([shard(4, l, s) for s in range(N_CHIPS)], axis=0)))

    tril = jnp.tril(jnp.ones((LANE, LANE), bool))
    smalls = []
    for l in range(depth):
        wm = jnp.where(tril[None], w_spatial[l], 0.0).astype(BF16)
        smalls.append(dict(
            b_f=b_forget[l].reshape(heads, 1), wm=wm, wm_t=jnp.swapaxes(wm, 1, 2), bs=b_spatial[l].reshape(heads, LANE, 1),
            g_v=sgu_norm_g[l].reshape(1, d), g_pre=mix_pre_g[l].reshape(1, d), g_post=mix_post_g[l].reshape(1, d),
            g_fpre=ffn_pre_g[l].reshape(1, d), g_fpost=ffn_post_g[l].reshape(1, d)))

    saved = []
    xin = xs
    h = _norm_fwd(xs, None, None, smalls[0]["g_pre"], "norm_first")
    for l in range(depth):
        sv = _layer_forward(xin, h, wts[l], smalls[l], d)
        saved.append(sv)
        g_next = smalls[l + 1]["g_pre"] if l + 1 < depth else smalls[l]["g_pre"]
        xin, h = _norm_fwd(sv["x1"], sv["z2"], smalls[l]["g_fpost"], g_next, "norm_out")
    dy, loss_part = _loss_grad(xin, target, "loss")
    loss = lax.psum(jnp.sum(loss_part), ("x", "y", "c"))

    grads = [None] * depth
    dx2 = dy
    dz2, g_fpost = _norm_bwd(dx2, None, (saved[depth - 1]["z2"], smalls[depth - 1]["g_fpost"]), "norm_bwd_top")
    for l in reversed(range(depth)):
        dh, dx1, g = _layer_backward(dz2, dx2, saved[l], wts[l], smalls[l], d)
        g["g_fpost"] = g_fpost
        if l > 0:
            dx2, dz2, g["g_pre"], g_fpost = _norm_bwd(dx1, (dh, saved[l]["x"], smalls[l]["g_pre"]),
                                                       (saved[l - 1]["z2"], smalls[l - 1]["g_fpost"]), "norm_bwd_between")
        else:
            grad_x, g["g_pre"] = _norm_bwd(dx1, (dh, saved[l]["x"], smalls[l]["g_pre"]), None, "norm_bwd_bottom")
        grads[l] = g

    def per_chip(l):
        gc = grads[l]["w_cat"]
        g_in = jnp.concatenate([gc[:, 0:2 * d], gc[:, 4 * d:7 * d], gc[:, 2 * d:4 * d], gc[:, 7 * d:7 * d + heads]], axis=1)
        wc = in_w // N_CHIPS
        fc = ff // N_CHIPS
        gu = grads[l]["w_gu"]
        return [
            jnp.stack([g_in[:, s * wc:(s + 1) * wc] for s in range(N_CHIPS)]).astype(BF16),
            grads[l]["w_out"].reshape(N_CHIPS, d // N_CHIPS, d).astype(BF16),
            jnp.stack([gu[:, s * fc:(s + 1) * fc] for s in range(N_CHIPS)]).astype(BF16),
            jnp.stack([gu[:, ff + s * fc:ff + (s + 1) * fc] for s in range(N_CHIPS)]).astype(BF16),
            grads[l]["w_d"].reshape(N_CHIPS, fc, d).astype(BF16),
        ]

    by_layer = [per_chip(l) for l in range(depth)]
    g_all = [jnp.stack([by_layer[0][a], by_layer[1][a]]) for a in range(5)]
    flat2 = [ga.reshape(2, N_CHIPS * ga.shape[2], ga.shape[3]) for ga in g_all]
    from_sib = _swap_other_layer(flat2, "swap_grads")
    chip_part = [_add_mine(a, b, c_idx, "add_sibling") for a, b in zip(flat2, from_sib)]
    chip_part = [p.reshape(ga.shape[1:]) for p, ga in zip(chip_part, g_all)]
    landed = _scatter_to_chips(chip_part, "scatter_grads")
    chip_sel = jnp.stack([chip, chip]).astype(jnp.int32)
    mine = [_sum_slots(got, sent, chip_sel, "sum_chips") for got, sent in zip(landed, chip_part)]
    theirs = _swap_with_sibling(mine, "share_grads")

    small_names = ["g_pre", "b_f", "g_v", "w_s", "b_s", "g_post", "g_fpre", "g_fpost"]
    small_shapes = [mix_pre_g.shape, b_forget.shape, sgu_norm_g.shape, w_spatial.shape, b_spatial.shape, mix_post_g.shape,
                    ffn_pre_g.shape, ffn_post_g.shape]
    parts = [jnp.stack([grads[l][nme].reshape(-1) for l in range(depth)]) for nme in small_names]
    packed = _small_pack(parts)
    dev_sel = jnp.stack([dev, jnp.zeros_like(dev)]).astype(jnp.int32)
    total = _sum_slots(_gather_all(packed, "gather_small"), packed[None], dev_sel, "sum_small").reshape(-1)
    small_grads, off = [], 0
    for shp in small_shapes:
        n = math.prod(shp)
        small_grads.append(total[off:off + n].reshape(shp))
        off += n
    (grad_mix_pre_g, grad_b_forget, grad_sgu_norm_g, grad_w_spatial, grad_b_spatial, grad_mix_post_g, grad_ffn_pre_g,
     grad_ffn_post_g) = small_grads

    def adam(w, g, m, v, name):
        shp = w.shape
        if w.ndim >= 3 and shp[-1] >= LANE:
            two = (math.prod(shp[:-1]), shp[-1])
        else:
            two = (1, math.prod(shp)) if math.prod(shp) < LANE else (math.prod(shp) // LANE, LANE)
        outs = _adamw(w.reshape(two), g.reshape(two), m.reshape(two), v.reshape(two), name)
        return [o.reshape(shp) for o in outs]

    ws = [mix_pre_g, w_in, b_forget, sgu_norm_g, w_spatial, b_spatial, w_out, mix_post_g, ffn_pre_g, w_gate, w_up, w_down, ffn_post_g]
    small_g = dict(zip([0, 2, 3, 4, 5, 7, 8, 12], [grad_mix_pre_g, grad_b_forget, grad_sgu_norm_g, grad_w_spatial, grad_b_spatial,
                                                  grad_mix_post_g, grad_ffn_pre_g, grad_ffn_post_g]))
    big_at = {1: 0, 6: 1, 9: 2, 10: 3, 11: 4}
    ms = [m_mix_pre_g, m_w_in, m_b_forget, m_sgu_norm_g, m_w_spatial, m_b_spatial, m_w_out, m_mix_post_g, m_ffn_pre_g, m_w_gate,
          m_w_up, m_w_down, m_ffn_post_g]
    vs = [v_mix_pre_g, v_w_in, v_b_forget, v_sgu_norm_g, v_w_spatial, v_b_spatial, v_w_out, v_mix_post_g, v_ffn_pre_g, v_w_gate,
          v_w_up, v_w_down, v_ffn_post_g]
    gs, deltas, new_ms, new_vs = [], [], [], []
    for k, (w, m, v) in enumerate(zip(ws, ms, vs)):
        if k in big_at:
            g, dl, nm, nv = _adamw_layers(w, mine[big_at[k]], theirs[big_at[k]], m, v, c_idx, "adamw_layers")
        else:
            g = small_g[k]
            dl, nm, nv = adam(w, g, m, v, "adamw")
        gs.append(g)
        deltas.append(dl)
        new_ms.append(nm)
        new_vs.append(nv)

    return (loss, grad_x.reshape(x.shape), *gs, *deltas, *new_ms, *new_vs)
```

```python
import functools
import math

import jax
import jax.numpy as jnp
from jax import lax
from jax.experimental import pallas as pl
from jax.experimental.pallas import tpu as pltpu

F32 = jnp.float32
BF16 = jnp.bfloat16

EPS = 1e-6
LANE = 128
SUBLANE = 8
N_CHIPS = 4
N_DEV = 8
VMEM_LIMIT = 48 * 1024 * 1024
MESH = pl.DeviceIdType.MESH

ADAM_LR = 0.001
ADAM_B1 = 0.9
ADAM_B2 = 0.999
ADAM_EPS = 1e-08
ADAM_WD = 0.01
ADAM_STEP = 10
ADAM_C1 = 1.0 / (1.0 - ADAM_B1 ** ADAM_STEP)
ADAM_C2 = 1.0 / (1.0 - ADAM_B2 ** ADAM_STEP)

GELU_K = math.sqrt(2.0 / math.pi)
GELU_A = 0.044715
NEG = -1e30
LOG2E = 1.4426950408889634
LN2 = 0.6931471805599453

COL_U, COL_V, COL_Q, COL_K, COL_VA, COL_GA, COL_GB, COL_F = range(8)


def _cparams(sem=None):
    return pltpu.CompilerParams(dimension_semantics=sem, vmem_limit_bytes=VMEM_LIMIT)


def _tile(n, cap):
    best = None
    for t in range(LANE, min(n, cap) + 1, LANE):
        if n % t == 0:
            best = t
    return best if best is not None else n


def _rows(n, cap):
    best = None
    for t in range(SUBLANE, min(n, cap) + 1, SUBLANE):
        if n % t == 0:
            best = t
    return best if best is not None else n


def _gelu_and_grad(x):
    x2 = x * x
    t = jnp.tanh(GELU_K * (x + GELU_A * x2 * x))
    g = 0.5 * x * (1.0 + t)
    dg = 0.5 * (1.0 + t) + 0.5 * x * (1.0 - t * t) * (GELU_K * (1.0 + 3.0 * GELU_A * x2))
    return g, dg


def _sigmoid(x):
    return 1.0 / (1.0 + jnp.exp(-x))


def _sum8(v):
    n, d = v.shape
    return v.reshape(n // SUBLANE, SUBLANE, d).sum(axis=0)


def _nt_dot(a, b):
    return lax.dot_general(a, b, (((1,), (1,)), ((), ())), preferred_element_type=F32)


_DIMS = {"nn": ((1,), (0,)), "nt": ((1,), (1,)), "tn": ((0,), (0,))}


def _matmul(a, b, mode, out_dtype, name, n=None, slab=None, into=None, tm_cap=512, tn_cap=2048, tk_cap=1408):
    if mode == "nn":
        (m, k), (k2, nn_) = a.shape, b.shape
    elif mode == "nt":
        (m, k), (nn_, k2) = a.shape, b.shape
    else:
        (k, m), (k2, nn_) = a.shape, b.shape
    n = nn_ if n is None else n
    assert k == k2, (a.shape, b.shape, mode)
    tm, tn, tk = _tile(m, tm_cap), _tile(n, tn_cap), _tile(k, tk_cap)
    nk = k // tk
    if mode == "tn":
        a_spec = pl.BlockSpec((tk, tm), lambda j, i, kk, *_: (kk, i))
    else:
        a_spec = pl.BlockSpec((tm, tk), lambda j, i, kk, *_: (i, kk))
    if mode == "nt":
        b_spec = pl.BlockSpec((tn, tk), lambda j, i, kk, *_: (j, kk))
    else:
        b_spec = pl.BlockSpec((tk, tn), lambda j, i, kk, *_: (kk, j))
    dims = (_DIMS[mode], ((), ()))
    aliased = into is not None

    def body(*refs):
        a_ref, b_ref = refs[0], refs[1]
        o_ref = refs[3] if aliased else refs[2]
        p = lax.dot_general(a_ref[...], b_ref[...], dims, preferred_element_type=F32)
        if nk == 1:
            o_ref[...] = p.astype(out_dtype).reshape(o_ref.shape)
        else:
            acc = refs[-1]
            kk = pl.program_id(2)

            @pl.when(kk == 0)
            def _():
                acc[...] = p

            @pl.when(kk > 0)
            def _():
                acc[...] += p

            @pl.when(kk == nk - 1)
            def _():
                o_ref[...] = acc[...].astype(out_dtype).reshape(o_ref.shape)

    if slab is None:
        out_spec = pl.BlockSpec((tm, tn), lambda j, i, kk: (i, j))
        out_shape = jax.ShapeDtypeStruct((m, n), out_dtype)
    else:
        shape3, lead, row0 = slab
        assert row0 % tm == 0 and shape3[2] == n
        out_spec = pl.BlockSpec((1, tm, tn), lambda j, i, kk: (lead, row0 // tm + i, j))
        out_shape = jax.ShapeDtypeStruct(shape3, out_dtype)
    in_specs, args = [a_spec, b_spec], [a, b]
    if aliased:
        in_specs.append(pl.BlockSpec(memory_space=pl.ANY))
        args.append(into)
    return pl.pallas_call(
        body,
        grid=(n // tn, m // tm, nk),
        in_specs=in_specs,
        out_specs=out_spec,
        out_shape=out_shape,
        scratch_shapes=[pltpu.VMEM((tm, tn), F32)] if nk > 1 else [],
        input_output_aliases={2: 0} if aliased else {},
        compiler_params=_cparams(("parallel", "parallel", "arbitrary")),
        name=name,
    )(*args)


def _matmul_pieces(pieces, addend, name, tk, tm_cap=512, tn_cap=1024):
    m = pieces[0][0].shape[0]
    n = pieces[0][1].shape[1]
    tm, tn = _tile(m, tm_cap), _tile(n, tn_cap)
    spans, s0 = [], 0
    for a, b, row0 in pieces:
        assert a.shape[1] % tk == 0 and row0 % tk == 0 and b.shape[1] == n and a.shape[0] == m
        spans.append((s0, a.shape[1] // tk, row0 // tk))
        s0 += a.shape[1] // tk
    steps = s0
    np_ = len(pieces)

    def body(*refs):
        o_ref, acc = refs[-2], refs[-1]
        s = pl.program_id(2)

        @pl.when(s == 0)
        def _():
            acc[...] = refs[2 * np_][...] if addend is not None else jnp.zeros((tm, tn), F32)

        for p, (first, count, _) in enumerate(spans):
            @pl.when((s >= first) & (s < first + count))
            def _(p=p):
                acc[...] += jnp.dot(refs[2 * p][...], refs[2 * p + 1][...], preferred_element_type=F32)

        @pl.when(s == steps - 1)
        def _():
            o_ref[...] = acc[...]

    in_specs, args = [], []
    for (a, b, _), (first, count, brow) in zip(pieces, spans):
        in_specs.append(pl.BlockSpec((tm, tk), lambda j, i, s, f=first, c=count: (i, jnp.clip(s - f, 0, c - 1))))
        in_specs.append(pl.BlockSpec((tk, tn), lambda j, i, s, f=first, c=count, r=brow: (r + jnp.clip(s - f, 0, c - 1), j)))
        args += [a, b]
    if addend is not None:
        in_specs.append(pl.BlockSpec((tm, tn), lambda j, i, s: (i, j)))
        args.append(addend)
    return pl.pallas_call(
        body, grid=(n // tn, m // tm, steps), in_specs=in_specs,
        out_specs=pl.BlockSpec((tm, tn), lambda j, i, s: (i, j)),
        out_shape=jax.ShapeDtypeStruct((m, n), F32),
        scratch_shapes=[pltpu.VMEM((tm, tn), F32)],
        compiler_params=_cparams(("parallel", "parallel", "arbitrary")), name=name,
    )(*args)


def _norm_fwd(x, z, g_post, g_next, name):
    t, d = x.shape
    tt = _rows(t, 512)
    row = pl.BlockSpec((tt, d), lambda i: (i, 0))
    vec = pl.BlockSpec((1, d), lambda i: (0, 0))

    def body(*refs):
        if z is None:
            x_ref, gn_ref, h_ref = refs
            xn = x_ref[...]
        else:
            x_ref, z_ref, gp_ref, gn_ref, xo_ref, h_ref = refs
            zz = z_ref[...]
            r = lax.rsqrt(jnp.mean(zz * zz, axis=-1, keepdims=True) + EPS)
            xn = x_ref[...] + zz * r * gp_ref[...]
            xo_ref[...] = xn
        r2 = lax.rsqrt(jnp.mean(xn * xn, axis=-1, keepdims=True) + EPS)
        h_ref[...] = (xn * r2 * gn_ref[...]).astype(BF16)

    if z is None:
        return pl.pallas_call(
            body, grid=(t // tt,), in_specs=[row, vec], out_specs=row,
            out_shape=jax.ShapeDtypeStruct((t, d), BF16), compiler_params=_cparams(("parallel",)), name=name,
        )(x, g_next)
    return pl.pallas_call(
        body, grid=(t // tt,), in_specs=[row, row, vec, vec], out_specs=[row, row],
        out_shape=[jax.ShapeDtypeStruct((t, d), F32), jax.ShapeDtypeStruct((t, d), BF16)],
        compiler_params=_cparams(("parallel",)), name=name,
    )(x, z, g_post, g_next)


def _rms_bwd(dy, x, g):
    r = lax.rsqrt(jnp.mean(x * x, axis=-1, keepdims=True) + EPS)
    n = x * r
    dn = dy * g
    dx = r * (dn - n * jnp.mean(dn * n, axis=-1, keepdims=True))
    return dx, dy * n


def _norm_bwd(dres, pre, post, name):
    t, d = dres.shape
    tt = _rows(t, 512)
    nt = t // tt
    row = pl.BlockSpec((tt, d), lambda i: (i, 0))
    vec = pl.BlockSpec((1, d), lambda i: (0, 0))
    has_pre, has_post = pre is not None, post is not None
    n_in = 1 + (3 if has_pre else 0) + (2 if has_post else 0)
    n_out = has_pre + has_post + has_pre + has_post

    def body(*refs):
        ins, outs, scr = refs[:n_in], refs[n_in:n_in + n_out], refs[n_in + n_out:]
        i = pl.program_id(0)
        dx = ins[0][...]
        pos, opos, spos = 1, 0, 0
        accs = []
        if has_pre:
            dh_ref, xa_ref, ga_ref = ins[pos:pos + 3]
            pos += 3
            dxa, dga_t = _rms_bwd(dh_ref[...], xa_ref[...], ga_ref[...])
            dx = dx + dxa
            outs[opos][...] = dx
            opos += 1
            accs.append((scr[spos], dga_t))
            spos += 1
        if has_post:
            zb_ref, gb_ref = ins[pos:pos + 2]
            dz, dgb_t = _rms_bwd(dx, zb_ref[...], gb_ref[...])
            outs[opos][...] = dz.astype(BF16)
            opos += 1
            accs.append((scr[spos], dgb_t))
            spos += 1
        for (acc, val), out in zip(accs, outs[opos:]):
            part = _sum8(val)

            @pl.when(i == 0)
            def _(acc=acc, part=part):
                acc[...] = part

            @pl.when(i > 0)
            def _(acc=acc, part=part):
                acc[...] += part

            @pl.when(i == nt - 1)
            def _(acc=acc, out=out):
                out[...] = jnp.sum(acc[...], axis=0, keepdims=True)

    in_specs, args = [row], [dres]
    out_specs, out_shape = [], []
    if has_pre:
        in_specs += [row, row, vec]
        args += list(pre)
        out_specs.append(row)
        out_shape.append(jax.ShapeDtypeStruct((t, d), F32))
    if has_post:
        in_specs += [row, vec]
        args += list(post)
        out_specs.append(row)
        out_shape.append(jax.ShapeDtypeStruct((t, d), BF16))
    for _ in range(has_pre + has_post):
        out_specs.append(vec)
        out_shape.append(jax.ShapeDtypeStruct((1, d), F32))
    return pl.pallas_call(
        body, grid=(nt,), in_specs=in_specs, out_specs=out_specs, out_shape=out_shape,
        scratch_shapes=[pltpu.VMEM((SUBLANE, d), F32)] * (has_pre + has_post),
        compiler_params=_cparams(("arbitrary",)), name=name,
    )(*args)


def _loss_grad(y, target, name):
    t, d = y.shape
    tt = _rows(t, 512)
    nt = t // tt
    row = pl.BlockSpec((tt, d), lambda i: (i, 0))
    inv_d = 1.0 / d

    def body(y_ref, t_ref, dy_ref, l_ref):
        i = pl.program_id(0)
        diff = y_ref[...] - t_ref[...]
        dy_ref[...] = diff * inv_d
        s8 = _sum8(diff * diff)
        part = s8[:, 0:LANE]
        for k in range(1, d // LANE):
            part = part + s8[:, k * LANE:(k + 1) * LANE]
        part = part * (0.5 * inv_d)

        @pl.when(i == 0)
        def _():
            l_ref[...] = part

        @pl.when(i > 0)
        def _():
            l_ref[...] += part

    return pl.pallas_call(
        body, grid=(nt,), in_specs=[row, row],
        out_specs=[row, pl.BlockSpec((SUBLANE, LANE), lambda i: (0, 0))],
        out_shape=[jax.ShapeDtypeStruct((t, d), F32), jax.ShapeDtypeStruct((SUBLANE, LANE), F32)],
        compiler_params=_cparams(("arbitrary",)), name=name,
    )(y, target)


def _swiglu_fwd(a, b, name):
    t, f = a.shape
    tt = _rows(t, 256)
    blk = pl.BlockSpec((tt, f), lambda i: (i, 0))

    def body(a_ref, b_ref, m_ref):
        av = a_ref[...]
        m_ref[...] = (av * _sigmoid(av) * b_ref[...]).astype(BF16)

    return pl.pallas_call(
        body, grid=(t // tt,), in_specs=[blk, blk], out_specs=blk,
        out_shape=jax.ShapeDtypeStruct((t, f), BF16), compiler_params=_cparams(("parallel",)), name=name,
    )(a, b)


def _swiglu_bwd(a, b, dm, name):
    t, f = a.shape
    tt = _rows(t, 256)
    blk = pl.BlockSpec((tt, f), lambda i: (i, 0))

    def body(a_ref, b_ref, dm_ref, da_ref, db_ref):
        av = a_ref[...]
        s = _sigmoid(av)
        dv = dm_ref[...]
        da_ref[...] = (dv * b_ref[...] * s * (1.0 + av * (1.0 - s))).astype(BF16)
        db_ref[...] = (dv * av * s).astype(BF16)

    return pl.pallas_call(
        body, grid=(t // tt,), in_specs=[blk, blk, blk], out_specs=[blk, blk],
        out_shape=[jax.ShapeDtypeStruct((t, f), BF16)] * 2, compiler_params=_cparams(("parallel",)), name=name,
    )(a, b, dm)


def _log_sigmoid(x):
    return jnp.minimum(x, 0.0) - jnp.log1p(jnp.exp(-jnp.abs(x)))


def _fox_prep(f_t, b_f, name):
    h, t = f_t.shape

    def body(f_ref, b_ref, c_ref):
        r = lax.broadcasted_iota(jnp.int32, (LANE, LANE), 0)
        c = lax.broadcasted_iota(jnp.int32, (LANE, LANE), 1)
        upper = (r <= c).astype(F32)
        carry = jnp.zeros((h, 1), F32)
        for j in range(t // LANE):
            sl = slice(j * LANE, (j + 1) * LANE)
            lf = _log_sigmoid(f_ref[:, sl] + b_ref[...])
            cs = jnp.dot(lf, upper, precision=lax.Precision.HIGHEST, preferred_element_type=F32) + carry
            c_ref[:, sl] = cs
            carry = cs[:, LANE - 1:LANE]

    return pl.pallas_call(body, out_shape=jax.ShapeDtypeStruct((h, t), F32), compiler_params=_cparams(), name=name)(f_t, b_f)


def _fox_bwd(dc_q, dc_k, f_t, b_f, name):
    h, t = f_t.shape

    def body(dq_ref, dk_ref, f_ref, b_ref, df_ref, db_ref):
        r = lax.broadcasted_iota(jnp.int32, (LANE, LANE), 0)
        c = lax.broadcasted_iota(jnp.int32, (LANE, LANE), 1)
        lower = (r >= c).astype(F32)
        carry = jnp.zeros((h, 1), F32)
        dbsum = jnp.zeros((h, 1), F32)
        for j in reversed(range(t // LANE)):
            sl = slice(j * LANE, (j + 1) * LANE)
            dc = dq_ref[:, sl] - dk_ref[:, sl]
            dl = jnp.dot(dc, lower, precision=lax.Precision.HIGHEST, preferred_element_type=F32) + carry
            carry = dl[:, 0:1]
            df = dl * _sigmoid(-(f_ref[:, sl] + b_ref[...]))
            df_ref[:, sl] = df
            dbsum = dbsum + jnp.sum(df, axis=-1, keepdims=True)
        db_ref[...] = dbsum

    return pl.pallas_call(
        body, out_shape=[jax.ShapeDtypeStruct((h, t), F32), jax.ShapeDtypeStruct((h, 1), F32)],
        compiler_params=_cparams(), name=name,
    )(dc_q, dc_k, f_t, b_f)


ATTN_Q = 512
ATTN_K = 256


def _attn_tiles(t):
    return _tile(t, ATTN_Q), _tile(t, ATTN_K)


def _attn_fwd(proj, c_col, c_row, d, name):
    t = proj.shape[0]
    h = d // LANE
    bq, bk = _attn_tiles(t)
    nq, nk, rr = t // bq, t // bk, bq // bk
    qc, kc, vc = COL_Q * h, COL_K * h, COL_VA * h
    qscale = LANE ** -0.5 * LOG2E

    def body(q_ref, k_ref, v_ref, cc_ref, cr_ref, o_ref, lse_ref, kb, vt, ckb, acc):
        i = pl.program_id(1)

        @pl.when(i == 0)
        def _():
            kb[...] = k_ref[...].astype(BF16)
            ckb[...] = jnp.broadcast_to(cc_ref[0] * LOG2E, (t, bq))
            for jn in range(nk):
                vt[jn] = v_ref[jn * bk:(jn + 1) * bk, :].T.astype(BF16)

        q = (q_ref[...] * qscale).astype(BF16)
        cq = cr_ref[0, 0] * LOG2E
        acc[...] = jnp.zeros((LANE, bq), F32)

        def block(j, diag, m_old, l_old):
            rows = pl.ds(pl.multiple_of(j * bk, bk), bk)
            s = _nt_dot(kb[rows, :], q) - ckb[rows, :]
            if diag is not None:
                kk = lax.broadcasted_iota(jnp.int32, (bk, bq), 0)
                qq = lax.broadcasted_iota(jnp.int32, (bk, bq), 1)
                s = jnp.where(qq >= kk + diag * bk, s, NEG)
            m_new = jnp.maximum(m_old, jnp.max(s, axis=0, keepdims=True) + cq)
            p = jnp.exp2(s + (cq - m_new))
            alpha = jnp.exp2(m_old - m_new)
            l_new = alpha * l_old + jnp.sum(p, axis=0, keepdims=True)
            acc[...] = alpha * acc[...] + jnp.dot(vt[j], p.astype(BF16), preferred_element_type=F32)
            return m_new, l_new

        m, l = lax.fori_loop(0, i * rr, lambda j, c: block(j, None, *c),
                             (jnp.full((1, bq), NEG, F32), jnp.zeros((1, bq), F32)))
        for jj in range(rr):
            m, l = block(i * rr + jj, jj, m, l)
        o_ref[...] = (acc[...] / l).T
        lse_ref[0, 0] = m + jnp.log2(l)

    rowq = pl.BlockSpec((1, 1, 1, bq), lambda hh, i: (hh, i, 0, 0))
    return pl.pallas_call(
        body, grid=(h, nq),
        in_specs=[
            pl.BlockSpec((bq, LANE), lambda hh, i: (i, qc + hh)),
            pl.BlockSpec((t, LANE), lambda hh, i: (0, kc + hh)),
            pl.BlockSpec((t, LANE), lambda hh, i: (0, vc + hh)),
            pl.BlockSpec((1, t, 1), lambda hh, i: (hh, 0, 0)),
            rowq,
        ],
        out_specs=[pl.BlockSpec((bq, LANE), lambda hh, i: (i, hh)), rowq],
        out_shape=[jax.ShapeDtypeStruct((t, d), F32), jax.ShapeDtypeStruct((h, nq, 1, bq), F32)],
        scratch_shapes=[pltpu.VMEM((t, LANE), BF16), pltpu.VMEM((nk, LANE, bk), BF16), pltpu.VMEM((t, bq), F32),
                        pltpu.VMEM((LANE, bq), F32)],
        compiler_params=_cparams(("arbitrary", "arbitrary")), name=name,
    )(proj, proj, proj, c_col, c_row)


def _attn_bwd(proj, do, o, lse, c_col, c_row, d, name):
    t = proj.shape[0]
    h = d // LANE
    bq, bk = _attn_tiles(t)
    nq, nk, rr = t // bq, t // bk, bq // bk
    qc, kc, vc = COL_Q * h, COL_K * h, COL_VA * h
    scale = LANE ** -0.5

    def body(q_ref, k_ref, v_ref, do_ref, o_ref, lse_ref, cc_ref, cr_ref, dq_ref, dk_ref, dv_ref, dcq_ref, dck_ref,
             kb, kt, vb, ckb, dk_acc, dv_acc, dck_acc, dqt_acc):
        i = pl.program_id(1)

        @pl.when(i == 0)
        def _():
            kb[...] = k_ref[...].astype(BF16)
            vb[...] = v_ref[...].astype(BF16)
            ckb[...] = jnp.broadcast_to(cc_ref[0] * LOG2E, (t, bq))
            for jn in range(nk):
                kt[jn] = k_ref[jn * bk:(jn + 1) * bk, :].T.astype(BF16)
            dk_acc[...] = jnp.zeros((t, LANE), F32)
            dv_acc[...] = jnp.zeros((t, LANE), F32)
            dck_acc[...] = jnp.zeros((t, LANE), F32)

        q = (q_ref[...] * (scale * LOG2E)).astype(BF16)
        dof = do_ref[...]
        dob = dof.astype(BF16)
        delta = jnp.sum((dof * o_ref[...]).T, axis=0, keepdims=True)
        rowb = cr_ref[0, 0] * LOG2E - lse_ref[0, 0]
        dqt_acc[...] = jnp.zeros((LANE, bq), F32)

        def block(j, diag, dcq):
            rows = pl.ds(pl.multiple_of(j * bk, bk), bk)
            p = jnp.exp2(_nt_dot(kb[rows, :], q) - ckb[rows, :] + rowb)
            if diag is not None:
                kk = lax.broadcasted_iota(jnp.int32, (bk, bq), 0)
                qq = lax.broadcasted_iota(jnp.int32, (bk, bq), 1)
                p = jnp.where(qq >= kk + diag * bk, p, 0.0)
            dv_acc[rows, :] += jnp.dot(p.astype(BF16), dob, preferred_element_type=F32)
            ds = p * (_nt_dot(vb[rows, :], dob) - delta)
            dsb = ds.astype(BF16)
            dk_acc[rows, :] += jnp.dot(dsb, q, preferred_element_type=F32)
            dqt_acc[...] += jnp.dot(kt[j], dsb, preferred_element_type=F32)
            part = ds[:, 0:LANE]
            for k in range(1, bq // LANE):
                part = part + ds[:, k * LANE:(k + 1) * LANE]
            dck_acc[rows, :] += part
            return dcq + jnp.sum(ds, axis=0, keepdims=True)

        dcq = lax.fori_loop(0, i * rr, lambda j, c: block(j, None, c), jnp.zeros((1, bq), F32))
        for jj in range(rr):
            dcq = block(i * rr + jj, jj, dcq)
        dq_ref[...] = (dqt_acc[...] * scale).T.astype(BF16)
        dcq_ref[0, 0] = dcq

        @pl.when(i == nq - 1)
        def _():
            dk_ref[...] = (dk_acc[...] * LN2).astype(BF16)
            dv_ref[...] = dv_acc[...].astype(BF16)
            dck_ref[0] = jnp.sum(dck_acc[...], axis=-1, keepdims=True)

    rowq = pl.BlockSpec((1, 1, 1, bq), lambda hh, i: (hh, i, 0, 0))
    blk = pl.BlockSpec((bq, LANE), lambda hh, i: (i, hh))
    whole = pl.BlockSpec((t, LANE), lambda hh, i: (0, hh))
    colk = pl.BlockSpec((1, t, 1), lambda hh, i: (hh, 0, 0))
    return pl.pallas_call(
        body, grid=(h, nq),
        in_specs=[
            pl.BlockSpec((bq, LANE), lambda hh, i: (i, qc + hh)),
            pl.BlockSpec((t, LANE), lambda hh, i: (0, kc + hh)),
            pl.BlockSpec((t, LANE), lambda hh, i: (0, vc + hh)),
            blk, blk, rowq, colk, rowq,
        ],
        out_specs=[blk, whole, whole, rowq, colk],
        out_shape=[jax.ShapeDtypeStruct((t, d), BF16), jax.ShapeDtypeStruct((t, d), BF16), jax.ShapeDtypeStruct((t, d), BF16),
                   jax.ShapeDtypeStruct((h, nq, 1, bq), F32), jax.ShapeDtypeStruct((h, t, 1), F32)],
        scratch_shapes=[pltpu.VMEM((t, LANE), BF16), pltpu.VMEM((nk, LANE, bk), BF16), pltpu.VMEM((t, LANE), BF16),
                        pltpu.VMEM((t, bq), F32), pltpu.VMEM((t, LANE), F32), pltpu.VMEM((t, LANE), F32),
                        pltpu.VMEM((t, LANE), F32), pltpu.VMEM((LANE, bq), F32)],
        compiler_params=_cparams(("arbitrary", "arbitrary")), name=name,
    )(proj, proj, proj, do, o, lse, c_col, c_row)


def _sgu_forward(u_ref, v_ref, gv_ref, wm_ref, bs_ref, mix_sc, groups):
    gu, dgu = _gelu_and_grad(u_ref[...])
    gvv, dgv = _gelu_and_grad(v_ref[...])
    mu = jnp.mean(gvv, axis=-1, keepdims=True)
    xc = gvv - mu
    r = lax.rsqrt(jnp.mean(xc * xc, axis=-1, keepdims=True) + EPS)
    nhat = xc * r
    vn = (nhat * gv_ref[...]).astype(BF16)
    for g in range(groups):
        sl = slice(g * LANE, (g + 1) * LANE)
        mix_sc[:, sl] = jnp.dot(wm_ref[g], vn[:, sl], preferred_element_type=F32) + bs_ref[g]
    return gu, dgu, dgv, nhat, r, vn, mix_sc[...]


def _mix_fwd(proj, o, wm, bs, g_v, d, name):
    t = proj.shape[0]
    groups = d // LANE

    def body(u_ref, v_ref, ga_ref, gb_ref, o_ref, wm_ref, bs_ref, gv_ref, out_ref, mix_sc):
        gu, _, _, _, _, _, mixed = _sgu_forward(u_ref, v_ref, gv_ref, wm_ref, bs_ref, mix_sc, groups)
        out_ref[...] = (_sigmoid(ga_ref[...]) * (gu * mixed) + _sigmoid(gb_ref[...]) * o_ref[...]).astype(BF16)

    def colblk(k):
        return pl.BlockSpec((LANE, d), lambda i, k=k: (i, k))

    full3 = pl.BlockSpec((groups, LANE, LANE), lambda i: (0, 0, 0))
    return pl.pallas_call(
        body, grid=(t // LANE,),
        in_specs=[colblk(COL_U), colblk(COL_V), colblk(COL_GA), colblk(COL_GB), colblk(0), full3,
                  pl.BlockSpec((groups, LANE, 1), lambda i: (0, 0, 0)), pl.BlockSpec((1, d), lambda i: (0, 0))],
        out_specs=colblk(0),
        out_shape=jax.ShapeDtypeStruct((t, d), BF16),
        scratch_shapes=[pltpu.VMEM((LANE, d), F32)],
        compiler_params=_cparams(("parallel",)), name=name,
    )(proj, proj, proj, proj, o, wm, bs, g_v)


def _mix_bwd(dmerged, proj, o, wm, wm_t, bs, g_v, d, name):
    t = proj.shape[0]
    groups = d // LANE
    nt = t // LANE

    def body(dm_ref, u_ref, v_ref, ga_ref, gb_ref, o_ref, wm_ref, wmt_ref, bs_ref, gv_ref,
             duv_ref, dg_ref, do_ref, dws_ref, dbs_ref, dgv_ref, mix_sc, dvn_sc, gv_acc):
        i = pl.program_id(0)

        @pl.when(i == 0)
        def _():
            dws_ref[...] = jnp.zeros_like(dws_ref)
            dbs_ref[...] = jnp.zeros_like(dbs_ref)
            gv_acc[...] = jnp.zeros_like(gv_acc)

        gu, dgu, dgv, nhat, r, vn, mixed = _sgu_forward(u_ref, v_ref, gv_ref, wm_ref, bs_ref, mix_sc, groups)
        dm = dm_ref[...]
        sa = _sigmoid(ga_ref[...])
        sb = _sigmoid(gb_ref[...])
        ov = o_ref[...]
        y_a = gu * mixed
        dg_ref[:, 0:d] = (dm * y_a * sa * (1.0 - sa)).astype(BF16)
        dg_ref[:, d:2 * d] = (dm * ov * sb * (1.0 - sb)).astype(BF16)
        do_ref[...] = dm * sb
        dy_a = dm * sa
        duv_ref[:, 0:d] = (dy_a * mixed * dgu).astype(BF16)
        dmixed = dy_a * gu
        dmixed_b = dmixed.astype(BF16)
        for g in range(groups):
            sl = slice(g * LANE, (g + 1) * LANE)
            dvn_sc[:, sl] = jnp.dot(wmt_ref[g], dmixed_b[:, sl], preferred_element_type=F32)
            dws_ref[g] += _nt_dot(dmixed_b[:, sl], vn[:, sl])
            dbs_ref[g] += jnp.sum(dmixed[:, sl], axis=-1, keepdims=True)
        dvn = dvn_sc[...]
        gv_acc[...] += _sum8(dvn * nhat)
        dn = dvn * gv_ref[...]
        dgelu = r * (dn - jnp.mean(dn, axis=-1, keepdims=True) - nhat * jnp.mean(dn * nhat, axis=-1, keepdims=True))
        duv_ref[:, d:2 * d] = (dgelu * dgv).astype(BF16)

        @pl.when(i == nt - 1)
        def _():
            dgv_ref[...] = jnp.sum(gv_acc[...], axis=0, keepdims=True)
            rr = lax.broadcasted_iota(jnp.int32, (LANE, LANE), 0)
            cl = lax.broadcasted_iota(jnp.int32, (LANE, LANE), 1)
            for g in range(groups):
                dws_ref[g] = jnp.where(rr >= cl, dws_ref[g], 0.0)

    def colblk(k):
        return pl.BlockSpec((LANE, d), lambda i, k=k: (i, k))

    full3 = pl.BlockSpec((groups, LANE, LANE), lambda i: (0, 0, 0))
    col3 = pl.BlockSpec((groups, LANE, 1), lambda i: (0, 0, 0))
    vec = pl.BlockSpec((1, d), lambda i: (0, 0))
    two = pl.BlockSpec((LANE, 2 * d), lambda i: (i, 0))
    return pl.pallas_call(
        body, grid=(nt,),
        in_specs=[colblk(0), colblk(COL_U), colblk(COL_V), colblk(COL_GA), colblk(COL_GB), colblk(0), full3, full3, col3, vec],
        out_specs=[two, two, colblk(0), full3, col3, vec],
        out_shape=[jax.ShapeDtypeStruct((t, 2 * d), BF16), jax.ShapeDtypeStruct((t, 2 * d), BF16), jax.ShapeDtypeStruct((t, d), F32),
                   jax.ShapeDtypeStruct((groups, LANE, LANE), F32), jax.ShapeDtypeStruct((groups, LANE, 1), F32),
                   jax.ShapeDtypeStruct((1, d), F32)],
        scratch_shapes=[pltpu.VMEM((LANE, d), F32), pltpu.VMEM((LANE, d), F32), pltpu.VMEM((SUBLANE, d), F32)],
        compiler_params=_cparams(("arbitrary",)), name=name,
    )(dmerged, proj, proj, proj, proj, o, wm, wm_t, bs, g_v)


def _adam_math(w, g, m, v):
    nm = ADAM_B1 * m + (1.0 - ADAM_B1) * g
    nv = ADAM_B2 * v + (1.0 - ADAM_B2) * (g * g)
    delta = -ADAM_LR * ((nm * ADAM_C1) / (jnp.sqrt(nv * ADAM_C2) + ADAM_EPS) + ADAM_WD * w)
    return delta, nm, nv


def _adamw(w, g, m, v, name):
    r, c = w.shape
    cap = max(SUBLANE, (2 * 1024 * 1024) // (4 * c) // SUBLANE * SUBLANE)
    tr = _rows(r, cap)

    def body(w_ref, g_ref, m_ref, v_ref, d_ref, nm_ref, nv_ref):
        d_ref[...], nm_ref[...], nv_ref[...] = _adam_math(w_ref[...], g_ref[...], m_ref[...], v_ref[...])

    blk = pl.BlockSpec((tr, c), lambda i: (i, 0))
    return pl.pallas_call(
        body, grid=(r // tr,), in_specs=[blk] * 4, out_specs=[blk] * 3,
        out_shape=[jax.ShapeDtypeStruct((r, c), F32)] * 3, compiler_params=_cparams(("parallel",)), name=name,
    )(w, g, m, v)


def _adamw_layers(w, g_mine, g_sib, m, v, c_idx, name):
    _, r, c = w.shape
    cap = max(SUBLANE, (1024 * 1024) // (4 * c) // SUBLANE * SUBLANE)
    tr = _rows(r, cap)

    def body(c_ref, w_ref, gm_ref, gs_ref, m_ref, v_ref, g_ref, d_ref, nm_ref, nv_ref):
        gg = jnp.where(pl.program_id(0) == c_ref[0], gm_ref[...], gs_ref[...])
        g_ref[0] = gg
        d_ref[0], nm_ref[0], nv_ref[0] = _adam_math(w_ref[0], gg, m_ref[0], v_ref[0])

    lay = pl.BlockSpec((1, tr, c), lambda l, i, cr: (l, i, 0))
    flat = pl.BlockSpec((tr, c), lambda l, i, cr: (i, 0))
    return pl.pallas_call(
        body,
        grid_spec=pltpu.PrefetchScalarGridSpec(
            num_scalar_prefetch=1, grid=(2, r // tr), in_specs=[lay, flat, flat, lay, lay], out_specs=[lay] * 4),
        out_shape=[jax.ShapeDtypeStruct((2, r, c), F32)] * 4, compiler_params=_cparams(("parallel", "parallel")), name=name,
    )(c_idx, w, g_mine, g_sib, m, v)


def _adamw_interleaved(w, g_mine, g_sib, m, v, c_idx, name):
    r, _, c = w.shape
    tr = 128

    def body(c_ref, w_ref, gm_ref, gs_ref, m_ref, v_ref, g_ref, d_ref, nm_ref, nv_ref):
        mine_first = c_ref[0] == 0
        for l in range(2):
            gg = jnp.where(mine_first == (l == 0), gm_ref[...], gs_ref[...])
            g_ref[:, l, :] = gg
            d_ref[:, l, :], nm_ref[:, l, :], nv_ref[:, l, :] = _adam_math(w_ref[:, l, :], gg, m_ref[:, l, :], v_ref[:, l, :])

    lay = pl.BlockSpec((tr, 2, c), lambda i, cr: (i, 0, 0))
    flat = pl.BlockSpec((tr, c), lambda i, cr: (i, 0))
    return pl.pallas_call(
        body,
        grid_spec=pltpu.PrefetchScalarGridSpec(
            num_scalar_prefetch=1, grid=(pl.cdiv(r, tr),), in_specs=[lay, flat, flat, lay, lay], out_specs=[lay] * 4),
        out_shape=[jax.ShapeDtypeStruct((r, 2, c), F32)] * 4, compiler_params=_cparams(("parallel",)), name=name,
    )(c_idx, w, g_mine, g_sib, m, v)


def _add_mine(g_all, recv, c_idx, name):
    _, r, c = g_all.shape
    tr = 512

    def body(c_ref, a_ref, b_ref, o_ref):
        o_ref[...] = (a_ref[0].astype(F32) + b_ref[...].astype(F32)).astype(BF16)

    return pl.pallas_call(
        body,
        grid_spec=pltpu.PrefetchScalarGridSpec(
            num_scalar_prefetch=1, grid=(pl.cdiv(r, tr),),
            in_specs=[pl.BlockSpec((1, tr, c), lambda i, cr: (cr[0], i, 0)), pl.BlockSpec((tr, c), lambda i, cr: (i, 0))],
            out_specs=pl.BlockSpec((tr, c), lambda i, cr: (i, 0)),
        ),
        out_shape=jax.ShapeDtypeStruct((r, c), BF16), compiler_params=_cparams(("parallel",)), name=name,
    )(c_idx, g_all, recv)


def _sum_slots(x, own, sel, name):
    s, r, c = x.shape
    tr = 128

    def body(sel_ref, x_ref, own_ref, o_ref):
        mine = own_ref[0].astype(F32)
        acc = jnp.zeros((tr, c), F32)
        for k in range(s):
            acc = acc + jnp.where(sel_ref[0] == k, mine, x_ref[k].astype(F32))
        o_ref[...] = acc

    return pl.pallas_call(
        body,
        grid_spec=pltpu.PrefetchScalarGridSpec(
            num_scalar_prefetch=1, grid=(pl.cdiv(r, tr),),
            in_specs=[pl.BlockSpec((s, tr, c), lambda i, sr: (0, i, 0)), pl.BlockSpec((1, tr, c), lambda i, sr: (sr[1], i, 0))],
            out_specs=pl.BlockSpec((tr, c), lambda i, sr: (i, 0)),
        ),
        out_shape=jax.ShapeDtypeStruct((r, c), F32), compiler_params=_cparams(("parallel",)), name=name,
    )(sel, x, own)


_HBM = pl.BlockSpec(memory_space=pl.ANY)


def _place():
    x, y, c = lax.axis_index("x"), lax.axis_index("y"), lax.axis_index("c")
    chips = [(1 - x, y), (x, 1 - y), (1 - x, 1 - y)]
    return x, y, c, chips


def _gather_weights(bufs, name):
    n = len(bufs)

    def body(*refs):
        o_refs = refs[n:2 * n]
        ssem, rsem = refs[2 * n:]
        x, y, c, chips = _place()
        me = 2 * x + y
        sib = (x, y, 1 - c)

        def copy(a, k, src, dst, to):
            return pltpu.make_async_remote_copy(src_ref=src, dst_ref=dst, send_sem=ssem.at[a, k], recv_sem=rsem.at[a, k],
                                                device_id=to, device_id_type=MESH)

        sends = []
        for a in range(n):
            mine = o_refs[a].at[c, me]
            for j, chip in enumerate(chips):
                cp = copy(a, j, mine, mine, (chip[0], chip[1], c))
                cp.start()
                sends.append(cp)
        for a in range(n):
            for j, chip in enumerate(chips):
                slot = o_refs[a].at[c, 2 * chip[0] + chip[1]]
                copy(a, j, slot, slot, sib).wait_recv()
                cp = copy(a, 3 + j, slot, slot, sib)
                cp.start()
                sends.append(cp)
        for a in range(n):
            for j, chip in enumerate(chips):
                slot = o_refs[a].at[1 - c, 2 * chip[0] + chip[1]]
                copy(a, 3 + j, slot, slot, sib).wait_recv()
        for cp in sends:
            cp.wait_send()

    return pl.pallas_call(
        body, in_specs=[_HBM] * n, out_specs=[_HBM] * n,
        out_shape=[jax.ShapeDtypeStruct(b.shape, b.dtype) for b in bufs],
        scratch_shapes=[pltpu.SemaphoreType.DMA((n, 6)), pltpu.SemaphoreType.DMA((n, 6))],
        input_output_aliases={a: a for a in range(n)},
        name=name,
    )(*bufs)


def _swap_other_layer(gs, name):
    n = len(gs)

    def body(*refs):
        g_refs, o_refs = refs[:n], refs[n:2 * n]
        ssem, rsem = refs[2 * n:]
        x, y, c, _ = _place()
        cps = []
        for a in range(n):
            cp = pltpu.make_async_remote_copy(src_ref=g_refs[a].at[1 - c], dst_ref=o_refs[a], send_sem=ssem.at[a],
                                              recv_sem=rsem.at[a], device_id=(x, y, 1 - c), device_id_type=MESH)
            cp.start()
            cps.append(cp)
        for cp in cps:
            cp.wait()

    return pl.pallas_call(
        body, in_specs=[_HBM] * n, out_specs=[_HBM] * n,
        out_shape=[jax.ShapeDtypeStruct(g.shape[1:], g.dtype) for g in gs],
        scratch_shapes=[pltpu.SemaphoreType.DMA((n,)), pltpu.SemaphoreType.DMA((n,))], name=name,
    )(*gs)


def _scatter_to_chips(ps, name):
    n = len(ps)

    def body(*refs):
        p_refs, o_refs = refs[:n], refs[n:2 * n]
        ssem, rsem = refs[2 * n:]
        x, y, c, chips = _place()
        me = 2 * x + y
        sends = []
        for a in range(n):
            for j, chip in enumerate(chips):
                cp = pltpu.make_async_remote_copy(
                    src_ref=p_refs[a].at[2 * chip[0] + chip[1]], dst_ref=o_refs[a].at[me], send_sem=ssem.at[a, j],
                    recv_sem=rsem.at[a, j], device_id=(chip[0], chip[1], c), device_id_type=MESH)
                cp.start()
                sends.append(cp)
        for a in range(n):
            for j, chip in enumerate(chips):
                slot = o_refs[a].at[2 * chip[0] + chip[1]]
                pltpu.make_async_remote_copy(src_ref=slot, dst_ref=slot, send_sem=ssem.at[a, j], recv_sem=rsem.at[a, j],
                                             device_id=(x, y, c), device_id_type=MESH).wait_recv()
        for cp in sends:
            cp.wait_send()

    return pl.pallas_call(
        body, in_specs=[_HBM] * n, out_specs=[_HBM] * n,
        out_shape=[jax.ShapeDtypeStruct(p.shape, p.dtype) for p in ps],
        scratch_shapes=[pltpu.SemaphoreType.DMA((n, 3)), pltpu.SemaphoreType.DMA((n, 3))],
        name=name,
    )(*ps)


def _swap_with_sibling(fs, name):
    n = len(fs)

    def body(*refs):
        f_refs, o_refs = refs[:n], refs[n:2 * n]
        ssem, rsem = refs[2 * n:]
        x, y, c, _ = _place()
        cps = []
        for a in range(n):
            cp = pltpu.make_async_remote_copy(src_ref=f_refs[a], dst_ref=o_refs[a], send_sem=ssem.at[a],
                                              recv_sem=rsem.at[a], device_id=(x, y, 1 - c), device_id_type=MESH)
            cp.start()
            cps.append(cp)
        for cp in cps:
            cp.wait()

    return pl.pallas_call(
        body, in_specs=[_HBM] * n, out_specs=[_HBM] * n,
        out_shape=[jax.ShapeDtypeStruct(f.shape, f.dtype) for f in fs],
        scratch_shapes=[pltpu.SemaphoreType.DMA((n,)), pltpu.SemaphoreType.DMA((n,))],
        name=name,
    )(*fs)


def _gather_all(buf, name):
    def body(b_ref, o_ref, ssem, rsem):
        x, y, c, _ = _place()
        me = 4 * x + 2 * y + c
        flips = [(fx, fy, fc) for fx in (0, 1) for fy in (0, 1) for fc in (0, 1)][1:]
        peers = [((1 - x) if fx else x, (1 - y) if fy else y, (1 - c) if fc else c) for fx, fy, fc in flips]
        sends = []
        for k, peer in enumerate(peers):
            cp = pltpu.make_async_remote_copy(src_ref=b_ref, dst_ref=o_ref.at[me], send_sem=ssem.at[k], recv_sem=rsem.at[k],
                                              device_id=peer, device_id_type=MESH)
            cp.start()
            sends.append(cp)
        for k, peer in enumerate(peers):
            slot = o_ref.at[4 * peer[0] + 2 * peer[1] + peer[2]]
            pltpu.make_async_remote_copy(src_ref=slot, dst_ref=slot, send_sem=ssem.at[k], recv_sem=rsem.at[k],
                                         device_id=(x, y, c), device_id_type=MESH).wait_recv()
        for cp in sends:
            cp.wait_send()

    return pl.pallas_call(
        body, in_specs=[_HBM], out_specs=_HBM,
        out_shape=jax.ShapeDtypeStruct((N_DEV,) + buf.shape, buf.dtype),
        scratch_shapes=[pltpu.SemaphoreType.DMA((N_DEV - 1,)), pltpu.SemaphoreType.DMA((N_DEV - 1,))],
        name=name,
    )(buf)


def _layer_forward(x, h, wts, sm, d):
    t = x.shape[0]
    heads = d // LANE
    bq, _ = _attn_tiles(t)
    proj = _matmul(h, wts["w_in_t"], "nt", F32, "proj_fwd", n=7 * d, tn_cap=1792)
    f_t = _matmul(wts["w_f_t"], h, "nt", F32, "forget_fwd", tn_cap=1024)
    c_t = _fox_prep(f_t, sm["b_f"], "fox_prep")
    c_col = c_t.reshape(heads, t, 1)
    c_row = c_t.reshape(heads, t // bq, 1, bq)
    o, lse = _attn_fwd(proj, c_col, c_row, d, "attn_fwd")
    merged = _mix_fwd(proj, o, sm["wm"], sm["bs"], sm["g_v"], d, "mix_fwd")
    z = _matmul(merged, wts["w_out"], "nn", F32, "out_fwd")
    x1, h2 = _norm_fwd(x, z, sm["g_post"], sm["g_fpre"], "norm_mid")
    a = _matmul(h2, wts["w_g_t"], "nt", F32, "gate_fwd", tn_cap=1408)
    b = _matmul(h2, wts["w_u_t"], "nt", F32, "up_fwd", tn_cap=1408)
    mm = _swiglu_fwd(a, b, "swiglu_fwd")
    z2 = _matmul(mm, wts["w_d"], "nn", F32, "down_fwd")
    return dict(x=x, h=h, proj=proj, f_t=f_t, c_col=c_col, c_row=c_row, o=o, lse=lse, merged=merged, z=z, x1=x1,
                h2=h2, a=a, b=b, mm=mm, z2=z2)


def _layer_backward(l, dz2, dx2, sv, wts, sm, d, pay):
    t = dx2.shape[0]
    heads = d // LANE
    ff = wts["w_d"].shape[0]
    in_w = 7 * d + heads
    g = {}

    def payload(key, a, b, shape3, row0, name):
        pay[key] = _matmul(a, b, "tn", BF16, name, slab=(shape3, l, row0), into=pay.get(key), tk_cap=1024)

    dm = _matmul(dz2, wts["w_d"], "nt", F32, "down_bwd_x", tn_cap=1408, tk_cap=1024)
    payload("w_d", sv["mm"], dz2, (2, ff, d), 0, "down_bwd_w")
    da, db = _swiglu_bwd(sv["a"], sv["b"], dm, "swiglu_bwd")
    dh2 = _matmul_pieces([(da, wts["w_g_t"], 0), (db, wts["w_u_t"], 0)], None, "gu_bwd_x", tk=_tile(ff, 1408))
    payload("w_g", da, sv["h2"], (2, ff, d), 0, "gate_bwd_w")
    payload("w_u", db, sv["h2"], (2, ff, d), 0, "up_bwd_w")
    dx1, dz, g["g_fpre"], g["g_post"] = _norm_bwd(dx2, (dh2, sv["x1"], sm["g_fpre"]), (sv["z"], sm["g_post"]), "norm_bwd_mid")
    dmerged = _matmul(dz, wts["w_out"], "nt", F32, "out_bwd_x", tk_cap=1024)
    payload("w_out", sv["merged"], dz, (2, d, d), 0, "out_bwd_w")
    d_uv, d_g, do, g["w_s"], g["b_s"], g["g_v"] = _mix_bwd(dmerged, sv["proj"], sv["o"], sm["wm"], sm["wm_t"], sm["bs"],
                                                         sm["g_v"], d, "mix_bwd")
    dq, dk, dv, dc_q, dc_k = _attn_bwd(sv["proj"], do, sv["o"], sv["lse"], sv["c_col"], sv["c_row"], d, "attn_bwd")
    df_t, g["b_f"] = _fox_bwd(dc_q.reshape(heads, t), dc_k.reshape(heads, t), sv["f_t"], sm["b_f"], "fox_bwd")
    df_b = df_t.astype(BF16)
    dh_f = _matmul(df_b, wts["w_f_t"], "tn", F32, "forget_bwd_x")
    pieces = [(d_uv, COL_U), (dq, COL_Q), (dk, COL_K), (dv, COL_VA), (d_g, COL_GA)]
    dh = _matmul_pieces([(p, wts["w_in_t"], col * d) for p, col in pieces], dh_f, "proj_bwd_x", tk=_tile(d, 1024))
    for p, col in pieces:
        payload("w_in", p, sv["h"], (2, in_w, d), col * d, "proj_bwd_w")
    g["w_f_t"] = _matmul(df_b, sv["h"], "nn", F32, "forget_bwd_w", tk_cap=1024)
    return dh, dx1, g


def _small_pack(parts):
    flat = jnp.concatenate([p.reshape(-1) for p in parts])
    n = flat.shape[0]
    pad = (-n) % (LANE * LANE)
    return jnp.pad(flat, (0, pad)).reshape(-1, LANE)


def kernel(x, mix_pre_g, w_in, b_forget, sgu_norm_g, w_spatial, b_spatial, w_out, mix_post_g, ffn_pre_g, w_gate, w_up, w_down, ffn_post_g, loss_target, m_mix_pre_g, m_w_in, m_b_forget, m_sgu_norm_g, m_w_spatial, m_b_spatial, m_w_out, m_mix_post_g, m_ffn_pre_g, m_w_gate, m_w_up, m_w_down, m_ffn_post_g, v_mix_pre_g, v_w_in, v_b_forget, v_sgu_norm_g, v_w_spatial, v_b_spatial, v_w_out, v_mix_post_g, v_ffn_pre_g, v_w_gate, v_w_up, v_w_down, v_ffn_post_g):
    depth, d = mix_pre_g.shape
    assert depth == 2, "core c of a chip owns layer c"
    heads = d // LANE
    t = x.shape[1]
    ff = w_down.shape[1] * N_CHIPS
    in_w = w_in.shape[2] * N_CHIPS
    assert in_w == 7 * d + heads
    xs = x.reshape(t, d)
    target = loss_target.reshape(t, d)
    c_idx = lax.axis_index("c").astype(jnp.int32).reshape(1)
    chip = 2 * lax.axis_index("x") + lax.axis_index("y")
    dev = 2 * chip + lax.axis_index("c")

    def in_view(w):
        return jnp.transpose(w, (2, 0, 1))

    def gu_view(w):
        return jnp.transpose(w, (0, 2, 1))

    own = [jnp.transpose(in_view(w_in).astype(BF16), (1, 0, 2)), w_out.astype(BF16), gu_view(w_gate).astype(BF16),
           gu_view(w_up).astype(BF16), w_down.astype(BF16)]
    bufs = [lax.dynamic_update_slice(jnp.zeros((2, N_CHIPS) + o.shape[1:], BF16), o[:, None], (0, chip, 0, 0)) for o in own]
    g_in, g_out, g_g, g_u, g_d = _gather_weights(bufs, "gather_weights")

    wts = []
    for l in range(depth):
        w_in_t = g_in[l].reshape(in_w, d)
        wts.append(dict(w_in_t=w_in_t, w_f_t=w_in_t[7 * d:], w_out=g_out[l].reshape(d, d), w_g_t=g_g[l].reshape(ff, d),
                        w_u_t=g_u[l].reshape(ff, d), w_d=g_d[l].reshape(ff, d)))

    tril = jnp.tril(jnp.ones((LANE, LANE), bool))
    smalls = []
    for l in range(depth):
        wm = jnp.where(tril[None], w_spatial[l], 0.0).astype(BF16)
        smalls.append(dict(
            b_f=b_forget[l].reshape(heads, 1), wm=wm, wm_t=jnp.swapaxes(wm, 1, 2), bs=b_spatial[l].reshape(heads, LANE, 1),
            g_v=sgu_norm_g[l].reshape(1, d), g_pre=mix_pre_g[l].reshape(1, d), g_post=mix_post_g[l].reshape(1, d),
            g_fpre=ffn_pre_g[l].reshape(1, d), g_fpost=ffn_post_g[l].reshape(1, d)))

    saved = []
    xin = xs
    h = _norm_fwd(xs, None, None, smalls[0]["g_pre"], "norm_first")
    for l in range(depth):
        sv = _layer_forward(xin, h, wts[l], smalls[l], d)
        saved.append(sv)
        g_next = smalls[l + 1]["g_pre"] if l + 1 < depth else smalls[l]["g_pre"]
        xin, h = _norm_fwd(sv["x1"], sv["z2"], smalls[l]["g_fpost"], g_next, "norm_out")
    dy, loss_part = _loss_grad(xin, target, "loss")
    loss = lax.psum(jnp.sum(loss_part), ("x", "y", "c"))

    grads = [None] * depth
    pay = {}
    dx2 = dy
    dz2, g_fpost = _norm_bwd(dx2, None, (saved[depth - 1]["z2"], smalls[depth - 1]["g_fpost"]), "norm_bwd_top")
    for l in reversed(range(depth)):
        dh, dx1, g = _layer_backward(l, dz2, dx2, saved[l], wts[l], smalls[l], d, pay)
        g["g_fpost"] = g_fpost
        if l > 0:
            dx2, dz2, g["g_pre"], g_fpost = _norm_bwd(dx1, (dh, saved[l]["x"], smalls[l]["g_pre"]),
                                                       (saved[l - 1]["z2"], smalls[l - 1]["g_fpost"]), "norm_bwd_between")
        else:
            grad_x, g["g_pre"] = _norm_bwd(dx1, (dh, saved[l]["x"], smalls[l]["g_pre"]), None, "norm_bwd_bottom")
        grads[l] = g

    w_f_rows = jnp.stack([grads[l]["w_f_t"] for l in range(depth)]).astype(BF16)
    pay["w_in"] = lax.dynamic_update_slice(pay["w_in"], w_f_rows, (0, 7 * d, 0))
    flat2 = [pay["w_in"], pay["w_out"], pay["w_g"], pay["w_u"], pay["w_d"]]
    from_sib = _swap_other_layer(flat2, "swap_grads")
    chip_part = [_add_mine(a, b, c_idx, "add_sibling") for a, b in zip(flat2, from_sib)]
    chip_part = [p.reshape(N_CHIPS, p.shape[0] // N_CHIPS, d) for p in chip_part]
    landed = _scatter_to_chips(chip_part, "scatter_grads")
    chip_sel = jnp.stack([chip, chip]).astype(jnp.int32)
    mine = [_sum_slots(got, sent, chip_sel, "sum_chips") for got, sent in zip(landed, chip_part)]
    theirs = _swap_with_sibling(mine, "share_grads")

    small_names = ["g_pre", "b_f", "g_v", "w_s", "b_s", "g_post", "g_fpre", "g_fpost"]
    small_shapes = [mix_pre_g.shape, b_forget.shape, sgu_norm_g.shape, w_spatial.shape, b_spatial.shape, mix_post_g.shape,
                    ffn_pre_g.shape, ffn_post_g.shape]
    parts = [jnp.stack([grads[l][nme].reshape(-1) for l in range(depth)]) for nme in small_names]
    packed = _small_pack(parts)
    dev_sel = jnp.stack([dev, jnp.zeros_like(dev)]).astype(jnp.int32)
    total = _sum_slots(_gather_all(packed, "gather_small"), packed[None], dev_sel, "sum_small").reshape(-1)
    small_grads, off = {}, 0
    for nme, shp in zip(small_names, small_shapes):
        n = math.prod(shp)
        small_grads[nme] = total[off:off + n].reshape(shp)
        off += n

    def adam_small(w, g, m, v):
        shp = w.shape
        if w.ndim >= 3 and shp[-1] >= LANE:
            two = (math.prod(shp[:-1]), shp[-1])
        else:
            two = (1, math.prod(shp)) if math.prod(shp) < LANE else (math.prod(shp) // LANE, LANE)
        outs = _adamw(w.reshape(two), g.reshape(two), m.reshape(two), v.reshape(two), "adamw")
        return [g] + [o.reshape(shp) for o in outs]

    def adam_in(w, m, v):
        outs = _adamw_interleaved(in_view(w), mine[0], theirs[0], in_view(m), in_view(v), c_idx, "adamw_in")
        return [jnp.transpose(o, (1, 2, 0)) for o in outs]

    def adam_gu(k, w, m, v):
        outs = _adamw_layers(gu_view(w), mine[k], theirs[k], gu_view(m), gu_view(v), c_idx, "adamw_layers")
        return [jnp.transpose(o, (0, 2, 1)) for o in outs]

    def adam_rows(k, w, m, v):
        return _adamw_layers(w, mine[k], theirs[k], m, v, c_idx, "adamw_layers")

    results = [
        adam_small(mix_pre_g, small_grads["g_pre"], m_mix_pre_g, v_mix_pre_g),
        adam_in(w_in, m_w_in, v_w_in),
        adam_small(b_forget, small_grads["b_f"], m_b_forget, v_b_forget),
        adam_small(sgu_norm_g, small_grads["g_v"], m_sgu_norm_g, v_sgu_norm_g),
        adam_small(w_spatial, small_grads["w_s"], m_w_spatial, v_w_spatial),
        adam_small(b_spatial, small_grads["b_s"], m_b_spatial, v_b_spatial),
        adam_rows(1, w_out, m_w_out, v_w_out),
        adam_small(mix_post_g, small_grads["g_post"], m_mix_post_g, v_mix_post_g),
        adam_small(ffn_pre_g, small_grads["g_fpre"], m_ffn_pre_g, v_ffn_pre_g),
        adam_gu(2, w_gate, m_w_gate, v_w_gate),
        adam_gu(3, w_up, m_w_up, v_w_up),
        adam_rows(4, w_down, m_w_down, v_w_down),
        adam_small(ffn_post_g, small_grads["g_fpost"], m_ffn_post_g, v_ffn_post_g),
    ]
    gs, deltas, new_ms, new_vs = zip(*results)
    return (loss, grad_x.reshape(x.shape), *gs, *deltas, *new_ms, *new_vs)
```

```python
import functools
import math

import jax
import jax.numpy as jnp
from jax import lax
from jax.experimental import pallas as pl
from jax.experimental.pallas import tpu as pltpu

F32 = jnp.float32
BF16 = jnp.bfloat16

EPS = 1e-6
LANE = 128
SUBLANE = 8
N_CHIPS = 4
N_DEV = 8
VMEM_LIMIT = 48 * 1024 * 1024
MESH = pl.DeviceIdType.MESH

ADAM_LR = 0.001
ADAM_B1 = 0.9
ADAM_B2 = 0.999
ADAM_EPS = 1e-08
ADAM_WD = 0.01
ADAM_STEP = 10
ADAM_C1 = 1.0 / (1.0 - ADAM_B1 ** ADAM_STEP)
ADAM_C2 = 1.0 / (1.0 - ADAM_B2 ** ADAM_STEP)

GELU_K = math.sqrt(2.0 / math.pi)
GELU_A = 0.044715
NEG = -1e30
LOG2E = 1.4426950408889634
LN2 = 0.6931471805599453

COL_U, COL_V, COL_Q, COL_K, COL_VA, COL_GA, COL_GB, COL_F = range(8)


def _cparams(sem=None):
    return pltpu.CompilerParams(dimension_semantics=sem, vmem_limit_bytes=VMEM_LIMIT)


def _tile(n, cap):
    best = None
    for t in range(LANE, min(n, cap) + 1, LANE):
        if n % t == 0:
            best = t
    return best if best is not None else n


def _rows(n, cap):
    best = None
    for t in range(SUBLANE, min(n, cap) + 1, SUBLANE):
        if n % t == 0:
            best = t
    return best if best is not None else n


def _gelu_and_grad(x):
    x2 = x * x
    t = jnp.tanh(GELU_K * (x + GELU_A * x2 * x))
    g = 0.5 * x * (1.0 + t)
    dg = 0.5 * (1.0 + t) + 0.5 * x * (1.0 - t * t) * (GELU_K * (1.0 + 3.0 * GELU_A * x2))
    return g, dg


def _sigmoid(x):
    return 1.0 / (1.0 + jnp.exp(-x))


def _sum8(v):
    n, d = v.shape
    return v.reshape(n // SUBLANE, SUBLANE, d).sum(axis=0)


def _nt_dot(a, b):
    return lax.dot_general(a, b, (((1,), (1,)), ((), ())), preferred_element_type=F32)


_DIMS = {"nn": ((1,), (0,)), "nt": ((1,), (1,)), "tn": ((0,), (0,))}


def _matmul(a, b, mode, out_dtype, name, n=None, slab=None, into=None, tm_cap=512, tn_cap=2048, tk_cap=1408):
    if mode == "nn":
        (m, k), (k2, nn_) = a.shape, b.shape
    elif mode == "nt":
        (m, k), (nn_, k2) = a.shape, b.shape
    else:
        (k, m), (k2, nn_) = a.shape, b.shape
    n = nn_ if n is None else n
    assert k == k2, (a.shape, b.shape, mode)
    tm, tn, tk = _tile(m, tm_cap), _tile(n, tn_cap), _tile(k, tk_cap)
    nk = k // tk
    if mode == "tn":
        a_spec = pl.BlockSpec((tk, tm), lambda j, i, kk, *_: (kk, i))
    else:
        a_spec = pl.BlockSpec((tm, tk), lambda j, i, kk, *_: (i, kk))
    if mode == "nt":
        b_spec = pl.BlockSpec((tn, tk), lambda j, i, kk, *_: (j, kk))
    else:
        b_spec = pl.BlockSpec((tk, tn), lambda j, i, kk, *_: (kk, j))
    dims = (_DIMS[mode], ((), ()))
    aliased = into is not None

    def body(*refs):
        a_ref, b_ref = refs[0], refs[1]
        o_ref = refs[3] if aliased else refs[2]
        p = lax.dot_general(a_ref[...], b_ref[...], dims, preferred_element_type=F32)
        if nk == 1:
            o_ref[...] = p.astype(out_dtype).reshape(o_ref.shape)
        else:
            acc = refs[-1]
            kk = pl.program_id(2)

            @pl.when(kk == 0)
            def _():
                acc[...] = p

            @pl.when(kk > 0)
            def _():
                acc[...] += p

            @pl.when(kk == nk - 1)
            def _():
                o_ref[...] = acc[...].astype(out_dtype).reshape(o_ref.shape)

    if slab is None:
        out_spec = pl.BlockSpec((tm, tn), lambda j, i, kk: (i, j))
        out_shape = jax.ShapeDtypeStruct((m, n), out_dtype)
    else:
        shape3, lead, row0 = slab
        assert row0 % tm == 0 and shape3[2] == n
        out_spec = pl.BlockSpec((1, tm, tn), lambda j, i, kk: (lead, row0 // tm + i, j))
        out_shape = jax.ShapeDtypeStruct(shape3, out_dtype)
    in_specs, args = [a_spec, b_spec], [a, b]
    if aliased:
        in_specs.append(pl.BlockSpec(memory_space=pl.ANY))
        args.append(into)
    return pl.pallas_call(
        body,
        grid=(n // tn, m // tm, nk),
        in_specs=in_specs,
        out_specs=out_spec,
        out_shape=out_shape,
        scratch_shapes=[pltpu.VMEM((tm, tn), F32)] if nk > 1 else [],
        input_output_aliases={2: 0} if aliased else {},
        compiler_params=_cparams(("parallel", "parallel", "arbitrary")),
        name=name,
    )(*args)


def _matmul_pieces(pieces, addend, name, tk, tm_cap=512, tn_cap=1024):
    m = pieces[0][0].shape[0]
    n = pieces[0][1].shape[1]
    tm, tn = _tile(m, tm_cap), _tile(n, tn_cap)
    spans, s0 = [], 0
    for a, b, row0 in pieces:
        assert a.shape[1] % tk == 0 and row0 % tk == 0 and b.shape[1] == n and a.shape[0] == m
        spans.append((s0, a.shape[1] // tk, row0 // tk))
        s0 += a.shape[1] // tk
    steps = s0
    np_ = len(pieces)

    def body(*refs):
        o_ref, acc = refs[-2], refs[-1]
        s = pl.program_id(2)

        @pl.when(s == 0)
        def _():
            acc[...] = refs[2 * np_][...] if addend is not None else jnp.zeros((tm, tn), F32)

        for p, (first, count, _) in enumerate(spans):
            @pl.when((s >= first) & (s < first + count))
            def _(p=p):
                acc[...] += jnp.dot(refs[2 * p][...], refs[2 * p + 1][...], preferred_element_type=F32)

        @pl.when(s == steps - 1)
        def _():
            o_ref[...] = acc[...]

    in_specs, args = [], []
    for (a, b, _), (first, count, brow) in zip(pieces, spans):
        in_specs.append(pl.BlockSpec((tm, tk), lambda j, i, s, f=first, c=count: (i, jnp.clip(s - f, 0, c - 1))))
        in_specs.append(pl.BlockSpec((tk, tn), lambda j, i, s, f=first, c=count, r=brow: (r + jnp.clip(s - f, 0, c - 1), j)))
        args += [a, b]
    if addend is not None:
        in_specs.append(pl.BlockSpec((tm, tn), lambda j, i, s: (i, j)))
        args.append(addend)
    return pl.pallas_call(
        body, grid=(n // tn, m // tm, steps), in_specs=in_specs,
        out_specs=pl.BlockSpec((tm, tn), lambda j, i, s: (i, j)),
        out_shape=jax.ShapeDtypeStruct((m, n), F32),
        scratch_shapes=[pltpu.VMEM((tm, tn), F32)],
        compiler_params=_cparams(("parallel", "parallel", "arbitrary")), name=name,
    )(*args)


def _norm_fwd(x, z, g_post, g_next, name):
    t, d = x.shape
    tt = _rows(t, 512)
    row = pl.BlockSpec((tt, d), lambda i: (i, 0))
    vec = pl.BlockSpec((1, d), lambda i: (0, 0))

    def body(*refs):
        if z is None:
            x_ref, gn_ref, h_ref = refs
            xn = x_ref[...]
        else:
            x_ref, z_ref, gp_ref, gn_ref, xo_ref, h_ref = refs
            zz = z_ref[...]
            r = lax.rsqrt(jnp.mean(zz * zz, axis=-1, keepdims=True) + EPS)
            xn = x_ref[...] + zz * r * gp_ref[...]
            xo_ref[...] = xn
        r2 = lax.rsqrt(jnp.mean(xn * xn, axis=-1, keepdims=True) + EPS)
        h_ref[...] = (xn * r2 * gn_ref[...]).astype(BF16)

    if z is None:
        return pl.pallas_call(
            body, grid=(t // tt,), in_specs=[row, vec], out_specs=row,
            out_shape=jax.ShapeDtypeStruct((t, d), BF16), compiler_params=_cparams(("parallel",)), name=name,
        )(x, g_next)
    return pl.pallas_call(
        body, grid=(t // tt,), in_specs=[row, row, vec, vec], out_specs=[row, row],
        out_shape=[jax.ShapeDtypeStruct((t, d), F32), jax.ShapeDtypeStruct((t, d), BF16)],
        compiler_params=_cparams(("parallel",)), name=name,
    )(x, z, g_post, g_next)


def _rms_bwd(dy, x, g):
    r = lax.rsqrt(jnp.mean(x * x, axis=-1, keepdims=True) + EPS)
    n = x * r
    dn = dy * g
    dx = r * (dn - n * jnp.mean(dn * n, axis=-1, keepdims=True))
    return dx, dy * n


def _norm_bwd(dres, pre, post, name):
    t, d = dres.shape
    tt = _rows(t, 512)
    nt = t // tt
    row = pl.BlockSpec((tt, d), lambda i: (i, 0))
    vec = pl.BlockSpec((1, d), lambda i: (0, 0))
    has_pre, has_post = pre is not None, post is not None
    n_in = 1 + (3 if has_pre else 0) + (2 if has_post else 0)
    n_out = has_pre + has_post + has_pre + has_post

    def body(*refs):
        ins, outs, scr = refs[:n_in], refs[n_in:n_in + n_out], refs[n_in + n_out:]
        i = pl.program_id(0)
        dx = ins[0][...]
        pos, opos, spos = 1, 0, 0
        accs = []
        if has_pre:
            dh_ref, xa_ref, ga_ref = ins[pos:pos + 3]
            pos += 3
            dxa, dga_t = _rms_bwd(dh_ref[...], xa_ref[...], ga_ref[...])
            dx = dx + dxa
            outs[opos][...] = dx
            opos += 1
            accs.append((scr[spos], dga_t))
            spos += 1
        if has_post:
            zb_ref, gb_ref = ins[pos:pos + 2]
            dz, dgb_t = _rms_bwd(dx, zb_ref[...], gb_ref[...])
            outs[opos][...] = dz.astype(BF16)
            opos += 1
            accs.append((scr[spos], dgb_t))
            spos += 1
        for (acc, val), out in zip(accs, outs[opos:]):
            part = _sum8(val)

            @pl.when(i == 0)
            def _(acc=acc, part=part):
                acc[...] = part

            @pl.when(i > 0)
            def _(acc=acc, part=part):
                acc[...] += part

            @pl.when(i == nt - 1)
            def _(acc=acc, out=out):
                out[...] = jnp.sum(acc[...], axis=0, keepdims=True)

    in_specs, args = [row], [dres]
    out_specs, out_shape = [], []
    if has_pre:
        in_specs += [row, row, vec]
        args += list(pre)
        out_specs.append(row)
        out_shape.append(jax.ShapeDtypeStruct((t, d), F32))
    if has_post:
        in_specs += [row, vec]
        args += list(post)
        out_specs.append(row)
        out_shape.append(jax.ShapeDtypeStruct((t, d), BF16))
    for _ in range(has_pre + has_post):
        out_specs.append(vec)
        out_shape.append(jax.ShapeDtypeStruct((1, d), F32))
    return pl.pallas_call(
        body, grid=(nt,), in_specs=in_specs, out_specs=out_specs, out_shape=out_shape,
        scratch_shapes=[pltpu.VMEM((SUBLANE, d), F32)] * (has_pre + has_post),
        compiler_params=_cparams(("arbitrary",)), name=name,
    )(*args)


def _loss_grad(y, target, name):
    t, d = y.shape
    tt = _rows(t, 512)
    nt = t // tt
    row = pl.BlockSpec((tt, d), lambda i: (i, 0))
    inv_d = 1.0 / d

    def body(y_ref, t_ref, dy_ref, l_ref):
        i = pl.program_id(0)
        diff = y_ref[...] - t_ref[...]
        dy_ref[...] = diff * inv_d
        s8 = _sum8(diff * diff)
        part = s8[:, 0:LANE]
        for k in range(1, d // LANE):
            part = part + s8[:, k * LANE:(k + 1) * LANE]
        part = part * (0.5 * inv_d)

        @pl.when(i == 0)
        def _():
            l_ref[...] = part

        @pl.when(i > 0)
        def _():
            l_ref[...] += part

    return pl.pallas_call(
        body, grid=(nt,), in_specs=[row, row],
        out_specs=[row, pl.BlockSpec((SUBLANE, LANE), lambda i: (0, 0))],
        out_shape=[jax.ShapeDtypeStruct((t, d), F32), jax.ShapeDtypeStruct((SUBLANE, LANE), F32)],
        compiler_params=_cparams(("arbitrary",)), name=name,
    )(y, target)


def _swiglu_fwd(a, b, name):
    t, f = a.shape
    tt = _rows(t, 256)
    blk = pl.BlockSpec((tt, f), lambda i: (i, 0))

    def body(a_ref, b_ref, m_ref):
        av = a_ref[...]
        m_ref[...] = (av * _sigmoid(av) * b_ref[...]).astype(BF16)

    return pl.pallas_call(
        body, grid=(t // tt,), in_specs=[blk, blk], out_specs=blk,
        out_shape=jax.ShapeDtypeStruct((t, f), BF16), compiler_params=_cparams(("parallel",)), name=name,
    )(a, b)


def _swiglu_bwd(a, b, dm, name):
    t, f = a.shape
    tt = _rows(t, 256)
    blk = pl.BlockSpec((tt, f), lambda i: (i, 0))

    def body(a_ref, b_ref, dm_ref, da_ref, db_ref):
        av = a_ref[...]
        s = _sigmoid(av)
        dv = dm_ref[...]
        da_ref[...] = (dv * b_ref[...] * s * (1.0 + av * (1.0 - s))).astype(BF16)
        db_ref[...] = (dv * av * s).astype(BF16)

    return pl.pallas_call(
        body, grid=(t // tt,), in_specs=[blk, blk, blk], out_specs=[blk, blk],
        out_shape=[jax.ShapeDtypeStruct((t, f), BF16)] * 2, compiler_params=_cparams(("parallel",)), name=name,
    )(a, b, dm)


def _log_sigmoid(x):
    return jnp.minimum(x, 0.0) - jnp.log1p(jnp.exp(-jnp.abs(x)))


def _fox_prep(f_t, b_f, name):
    h, t = f_t.shape

    def body(f_ref, b_ref, c_ref):
        r = lax.broadcasted_iota(jnp.int32, (LANE, LANE), 0)
        c = lax.broadcasted_iota(jnp.int32, (LANE, LANE), 1)
        upper = (r <= c).astype(F32)
        carry = jnp.zeros((h, 1), F32)
        for j in range(t // LANE):
            sl = slice(j * LANE, (j + 1) * LANE)
            lf = _log_sigmoid(f_ref[:, sl] + b_ref[...])
            cs = jnp.dot(lf, upper, precision=lax.Precision.HIGHEST, preferred_element_type=F32) + carry
            c_ref[:, sl] = cs
            carry = cs[:, LANE - 1:LANE]

    return pl.pallas_call(body, out_shape=jax.ShapeDtypeStruct((h, t), F32), compiler_params=_cparams(), name=name)(f_t, b_f)


def _fox_bwd(dc_q, dc_k, f_t, b_f, name):
    h, t = f_t.shape

    def body(dq_ref, dk_ref, f_ref, b_ref, df_ref, db_ref):
        r = lax.broadcasted_iota(jnp.int32, (LANE, LANE), 0)
        c = lax.broadcasted_iota(jnp.int32, (LANE, LANE), 1)
        lower = (r >= c).astype(F32)
        carry = jnp.zeros((h, 1), F32)
        dbsum = jnp.zeros((h, 1), F32)
        for j in reversed(range(t // LANE)):
            sl = slice(j * LANE, (j + 1) * LANE)
            dc = dq_ref[:, sl] - dk_ref[:, sl]
            dl = jnp.dot(dc, lower, precision=lax.Precision.HIGHEST, preferred_element_type=F32) + carry
            carry = dl[:, 0:1]
            df = dl * _sigmoid(-(f_ref[:, sl] + b_ref[...]))
            df_ref[:, sl] = df
            dbsum = dbsum + jnp.sum(df, axis=-1, keepdims=True)
        db_ref[...] = dbsum

    return pl.pallas_call(
        body, out_shape=[jax.ShapeDtypeStruct((h, t), F32), jax.ShapeDtypeStruct((h, 1), F32)],
        compiler_params=_cparams(), name=name,
    )(dc_q, dc_k, f_t, b_f)


ATTN_Q = 512
ATTN_K = 256


def _attn_tiles(t):
    return _tile(t, ATTN_Q), _tile(t, ATTN_K)


def _attn_fwd(proj, c_col, c_row, d, name):
    t = proj.shape[0]
    h = d // LANE
    bq, bk = _attn_tiles(t)
    nq, nk, rr = t // bq, t // bk, bq // bk
    qc, kc, vc = COL_Q * h, COL_K * h, COL_VA * h
    qscale = LANE ** -0.5 * LOG2E

    def body(q_ref, k_ref, v_ref, cc_ref, cr_ref, o_ref, lse_ref, kb, vt, ckb, acc):
        i = pl.program_id(1)

        @pl.when(i == 0)
        def _():
            kb[...] = k_ref[...].astype(BF16)
            ckb[...] = jnp.broadcast_to(cc_ref[0] * LOG2E, (t, bq))
            for jn in range(nk):
                vt[jn] = v_ref[jn * bk:(jn + 1) * bk, :].T.astype(BF16)

        q = (q_ref[...] * qscale).astype(BF16)
        cq = cr_ref[0, 0] * LOG2E
        acc[...] = jnp.zeros((LANE, bq), F32)

        def block(j, diag, m_old, l_old):
            rows = pl.ds(pl.multiple_of(j * bk, bk), bk)
            s = _nt_dot(kb[rows, :], q) - ckb[rows, :]
            if diag is not None:
                kk = lax.broadcasted_iota(jnp.int32, (bk, bq), 0)
                qq = lax.broadcasted_iota(jnp.int32, (bk, bq), 1)
                s = jnp.where(qq >= kk + diag * bk, s, NEG)
            m_new = jnp.maximum(m_old, jnp.max(s, axis=0, keepdims=True) + cq)
            p = jnp.exp2(s + (cq - m_new))
            alpha = jnp.exp2(m_old - m_new)
            l_new = alpha * l_old + jnp.sum(p, axis=0, keepdims=True)
            acc[...] = alpha * acc[...] + jnp.dot(vt[j], p.astype(BF16), preferred_element_type=F32)
            return m_new, l_new

        m, l = lax.fori_loop(0, i * rr, lambda j, c: block(j, None, *c),
                             (jnp.full((1, bq), NEG, F32), jnp.zeros((1, bq), F32)))
        for jj in range(rr):
            m, l = block(i * rr + jj, jj, m, l)
        o_ref[...] = (acc[...] / l).T
        lse_ref[0, 0] = m + jnp.log2(l)

    rowq = pl.BlockSpec((1, 1, 1, bq), lambda hh, i: (hh, i, 0, 0))
    return pl.pallas_call(
        body, grid=(h, nq),
        in_specs=[
            pl.BlockSpec((bq, LANE), lambda hh, i: (i, qc + hh)),
            pl.BlockSpec((t, LANE), lambda hh, i: (0, kc + hh)),
            pl.BlockSpec((t, LANE), lambda hh, i: (0, vc + hh)),
            pl.BlockSpec((1, t, 1), lambda hh, i: (hh, 0, 0)),
            rowq,
        ],
        out_specs=[pl.BlockSpec((bq, LANE), lambda hh, i: (i, hh)), rowq],
        out_shape=[jax.ShapeDtypeStruct((t, d), F32), jax.ShapeDtypeStruct((h, nq, 1, bq), F32)],
        scratch_shapes=[pltpu.VMEM((t, LANE), BF16), pltpu.VMEM((nk, LANE, bk), BF16), pltpu.VMEM((t, bq), F32),
                        pltpu.VMEM((LANE, bq), F32)],
        compiler_params=_cparams(("arbitrary", "arbitrary")), name=name,
    )(proj, proj, proj, c_col, c_row)


def _attn_bwd(proj, do, o, lse, c_col, c_row, d, name):
    t = proj.shape[0]
    h = d // LANE
    bq, bk = _attn_tiles(t)
    nq, nk, rr = t // bq, t // bk, bq // bk
    qc, kc, vc = COL_Q * h, COL_K * h, COL_VA * h
    scale = LANE ** -0.5

    def body(q_ref, k_ref, v_ref, do_ref, o_ref, lse_ref, cc_ref, cr_ref, dq_ref, dk_ref, dv_ref, dcq_ref, dck_ref,
             kb, kt, vb, ckb, dk_acc, dv_acc, dck_acc, dqt_acc):
        i = pl.program_id(1)

        @pl.when(i == 0)
        def _():
            kb[...] = k_ref[...].astype(BF16)
            vb[...] = v_ref[...].astype(BF16)
            ckb[...] = jnp.broadcast_to(cc_ref[0] * LOG2E, (t, bq))
            for jn in range(nk):
                kt[jn] = k_ref[jn * bk:(jn + 1) * bk, :].T.astype(BF16)
            dk_acc[...] = jnp.zeros((t, LANE), F32)
            dv_acc[...] = jnp.zeros((t, LANE), F32)
            dck_acc[...] = jnp.zeros((t, LANE), F32)

        q = (q_ref[...] * (scale * LOG2E)).astype(BF16)
        dof = do_ref[...]
        dob = dof.astype(BF16)
        delta = jnp.sum((dof * o_ref[...]).T, axis=0, keepdims=True)
        rowb = cr_ref[0, 0] * LOG2E - lse_ref[0, 0]
        dqt_acc[...] = jnp.zeros((LANE, bq), F32)

        def block(j, diag, dcq):
            rows = pl.ds(pl.multiple_of(j * bk, bk), bk)
            p = jnp.exp2(_nt_dot(kb[rows, :], q) - ckb[rows, :] + rowb)
            if diag is not None:
                kk = lax.broadcasted_iota(jnp.int32, (bk, bq), 0)
                qq = lax.broadcasted_iota(jnp.int32, (bk, bq), 1)
                p = jnp.where(qq >= kk + diag * bk, p, 0.0)
            dv_acc[rows, :] += jnp.dot(p.astype(BF16), dob, preferred_element_type=F32)
            ds = p * (_nt_dot(vb[rows, :], dob) - delta)
            dsb = ds.astype(BF16)
            dk_acc[rows, :] += jnp.dot(dsb, q, preferred_element_type=F32)
            dqt_acc[...] += jnp.dot(kt[j], dsb, preferred_element_type=F32)
            part = ds[:, 0:LANE]
            for k in range(1, bq // LANE):
                part = part + ds[:, k * LANE:(k + 1) * LANE]
            dck_acc[rows, :] += part
            return dcq + jnp.sum(ds, axis=0, keepdims=True)

        dcq = lax.fori_loop(0, i * rr, lambda j, c: block(j, None, c), jnp.zeros((1, bq), F32))
        for jj in range(rr):
            dcq = block(i * rr + jj, jj, dcq)
        dq_ref[...] = (dqt_acc[...] * scale).T.astype(BF16)
        dcq_ref[0, 0] = dcq

        @pl.when(i == nq - 1)
        def _():
            dk_ref[...] = (dk_acc[...] * LN2).astype(BF16)
            dv_ref[...] = dv_acc[...].astype(BF16)
            dck_ref[0] = jnp.sum(dck_acc[...], axis=-1, keepdims=True)

    rowq = pl.BlockSpec((1, 1, 1, bq), lambda hh, i: (hh, i, 0, 0))
    blk = pl.BlockSpec((bq, LANE), lambda hh, i: (i, hh))
    whole = pl.BlockSpec((t, LANE), lambda hh, i: (0, hh))
    colk = pl.BlockSpec((1, t, 1), lambda hh, i: (hh, 0, 0))
    return pl.pallas_call(
        body, grid=(h, nq),
        in_specs=[
            pl.BlockSpec((bq, LANE), lambda hh, i: (i, qc + hh)),
            pl.BlockSpec((t, LANE), lambda hh, i: (0, kc + hh)),
            pl.BlockSpec((t, LANE), lambda hh, i: (0, vc + hh)),
            blk, blk, rowq, colk, rowq,
        ],
        out_specs=[blk, whole, whole, rowq, colk],
        out_shape=[jax.ShapeDtypeStruct((t, d), BF16), jax.ShapeDtypeStruct((t, d), BF16), jax.ShapeDtypeStruct((t, d), BF16),
                   jax.ShapeDtypeStruct((h, nq, 1, bq), F32), jax.ShapeDtypeStruct((h, t, 1), F32)],
        scratch_shapes=[pltpu.VMEM((t, LANE), BF16), pltpu.VMEM((nk, LANE, bk), BF16), pltpu.VMEM((t, LANE), BF16),
                        pltpu.VMEM((t, bq), F32), pltpu.VMEM((t, LANE), F32), pltpu.VMEM((t, LANE), F32),
                        pltpu.VMEM((t, LANE), F32), pltpu.VMEM((LANE, bq), F32)],
        compiler_params=_cparams(("arbitrary", "arbitrary")), name=name,
    )(proj, proj, proj, do, o, lse, c_col, c_row)


def _sgu_forward(u_ref, v_ref, gv_ref, wm_ref, bs_ref, mix_sc, groups):
    gu, dgu = _gelu_and_grad(u_ref[...])
    gvv, dgv = _gelu_and_grad(v_ref[...])
    mu = jnp.mean(gvv, axis=-1, keepdims=True)
    xc = gvv - mu
    r = lax.rsqrt(jnp.mean(xc * xc, axis=-1, keepdims=True) + EPS)
    nhat = xc * r
    vn = (nhat * gv_ref[...]).astype(BF16)
    for g in range(groups):
        sl = slice(g * LANE, (g + 1) * LANE)
        mix_sc[:, sl] = jnp.dot(wm_ref[g], vn[:, sl], preferred_element_type=F32) + bs_ref[g]
    return gu, dgu, dgv, nhat, r, vn, mix_sc[...]


def _mix_fwd(proj, o, wm, bs, g_v, d, name):
    t = proj.shape[0]
    groups = d // LANE

    def body(u_ref, v_ref, ga_ref, gb_ref, o_ref, wm_ref, bs_ref, gv_ref, out_ref, mix_sc):
        gu, _, _, _, _, _, mixed = _sgu_forward(u_ref, v_ref, gv_ref, wm_ref, bs_ref, mix_sc, groups)
        out_ref[...] = (_sigmoid(ga_ref[...]) * (gu * mixed) + _sigmoid(gb_ref[...]) * o_ref[...]).astype(BF16)

    def colblk(k):
        return pl.BlockSpec((LANE, d), lambda i, k=k: (i, k))

    full3 = pl.BlockSpec((groups, LANE, LANE), lambda i: (0, 0, 0))
    return pl.pallas_call(
        body, grid=(t // LANE,),
        in_specs=[colblk(COL_U), colblk(COL_V), colblk(COL_GA), colblk(COL_GB), colblk(0), full3,
                  pl.BlockSpec((groups, LANE, 1), lambda i: (0, 0, 0)), pl.BlockSpec((1, d), lambda i: (0, 0))],
        out_specs=colblk(0),
        out_shape=jax.ShapeDtypeStruct((t, d), BF16),
        scratch_shapes=[pltpu.VMEM((LANE, d), F32)],
        compiler_params=_cparams(("parallel",)), name=name,
    )(proj, proj, proj, proj, o, wm, bs, g_v)


def _mix_bwd(dmerged, proj, o, wm, wm_t, bs, g_v, d, name):
    t = proj.shape[0]
    groups = d // LANE
    nt = t // LANE

    def body(dm_ref, u_ref, v_ref, ga_ref, gb_ref, o_ref, wm_ref, wmt_ref, bs_ref, gv_ref,
             duv_ref, dg_ref, do_ref, dws_ref, dbs_ref, dgv_ref, mix_sc, dvn_sc, gv_acc):
        i = pl.program_id(0)

        @pl.when(i == 0)
        def _():
            dws_ref[...] = jnp.zeros_like(dws_ref)
            dbs_ref[...] = jnp.zeros_like(dbs_ref)
            gv_acc[...] = jnp.zeros_like(gv_acc)

        gu, dgu, dgv, nhat, r, vn, mixed = _sgu_forward(u_ref, v_ref, gv_ref, wm_ref, bs_ref, mix_sc, groups)
        dm = dm_ref[...]
        sa = _sigmoid(ga_ref[...])
        sb = _sigmoid(gb_ref[...])
        ov = o_ref[...]
        y_a = gu * mixed
        dg_ref[:, 0:d] = (dm * y_a * sa * (1.0 - sa)).astype(BF16)
        dg_ref[:, d:2 * d] = (dm * ov * sb * (1.0 - sb)).astype(BF16)
        do_ref[...] = dm * sb
        dy_a = dm * sa
        duv_ref[:, 0:d] = (dy_a * mixed * dgu).astype(BF16)
        dmixed = dy_a * gu
        dmixed_b = dmixed.astype(BF16)
        for g in range(groups):
            sl = slice(g * LANE, (g + 1) * LANE)
            dvn_sc[:, sl] = jnp.dot(wmt_ref[g], dmixed_b[:, sl], preferred_element_type=F32)
            dws_ref[g] += _nt_dot(dmixed_b[:, sl], vn[:, sl])
            dbs_ref[g] += jnp.sum(dmixed[:, sl], axis=-1, keepdims=True)
        dvn = dvn_sc[...]
        gv_acc[...] += _sum8(dvn * nhat)
        dn = dvn * gv_ref[...]
        dgelu = r * (dn - jnp.mean(dn, axis=-1, keepdims=True) - nhat * jnp.mean(dn * nhat, axis=-1, keepdims=True))
        duv_ref[:, d:2 * d] = (dgelu * dgv).astype(BF16)

        @pl.when(i == nt - 1)
        def _():
            dgv_ref[...] = jnp.sum(gv_acc[...], axis=0, keepdims=True)
            rr = lax.broadcasted_iota(jnp.int32, (LANE, LANE), 0)
            cl = lax.broadcasted_iota(jnp.int32, (LANE, LANE), 1)
            for g in range(groups):
                dws_ref[g] = jnp.where(rr >= cl, dws_ref[g], 0.0)

    def colblk(k):
        return pl.BlockSpec((LANE, d), lambda i, k=k: (i, k))

    full3 = pl.BlockSpec((groups, LANE, LANE), lambda i: (0, 0, 0))
    col3 = pl.BlockSpec((groups, LANE, 1), lambda i: (0, 0, 0))
    vec = pl.BlockSpec((1, d), lambda i: (0, 0))
    two = pl.BlockSpec((LANE, 2 * d), lambda i: (i, 0))
    return pl.pallas_call(
        body, grid=(nt,),
        in_specs=[colblk(0), colblk(COL_U), colblk(COL_V), colblk(COL_GA), colblk(COL_GB), colblk(0), full3, full3, col3, vec],
        out_specs=[two, two, colblk(0), full3, col3, vec],
        out_shape=[jax.ShapeDtypeStruct((t, 2 * d), BF16), jax.ShapeDtypeStruct((t, 2 * d), BF16), jax.ShapeDtypeStruct((t, d), F32),
                   jax.ShapeDtypeStruct((groups, LANE, LANE), F32), jax.ShapeDtypeStruct((groups, LANE, 1), F32),
                   jax.ShapeDtypeStruct((1, d), F32)],
        scratch_shapes=[pltpu.VMEM((LANE, d), F32), pltpu.VMEM((LANE, d), F32), pltpu.VMEM((SUBLANE, d), F32)],
        compiler_params=_cparams(("arbitrary",)), name=name,
    )(dmerged, proj, proj, proj, proj, o, wm, wm_t, bs, g_v)


def _adam_math(w, g, m, v):
    nm = ADAM_B1 * m + (1.0 - ADAM_B1) * g
    nv = ADAM_B2 * v + (1.0 - ADAM_B2) * (g * g)
    delta = -ADAM_LR * ((nm * ADAM_C1) / (jnp.sqrt(nv * ADAM_C2) + ADAM_EPS) + ADAM_WD * w)
    return delta, nm, nv


def _adamw(w, g, m, v, name):
    r, c = w.shape
    cap = max(SUBLANE, (2 * 1024 * 1024) // (4 * c) // SUBLANE * SUBLANE)
    tr = _rows(r, cap)

    def body(w_ref, g_ref, m_ref, v_ref, d_ref, nm_ref, nv_ref):
        d_ref[...], nm_ref[...], nv_ref[...] = _adam_math(w_ref[...], g_ref[...], m_ref[...], v_ref[...])

    blk = pl.BlockSpec((tr, c), lambda i: (i, 0))
    return pl.pallas_call(
        body, grid=(r // tr,), in_specs=[blk] * 4, out_specs=[blk] * 3,
        out_shape=[jax.ShapeDtypeStruct((r, c), F32)] * 3, compiler_params=_cparams(("parallel",)), name=name,
    )(w, g, m, v)


def _adamw_layers(w, g_mine, g_sib, m, v, c_idx, name):
    _, r, c = w.shape
    cap = max(SUBLANE, (1024 * 1024) // (4 * c) // SUBLANE * SUBLANE)
    tr = _rows(r, cap)

    def body(c_ref, w_ref, gm_ref, gs_ref, m_ref, v_ref, g_ref, d_ref, nm_ref, nv_ref):
        gg = jnp.where(pl.program_id(0) == c_ref[0], gm_ref[...], gs_ref[...])
        g_ref[0] = gg
        d_ref[0], nm_ref[0], nv_ref[0] = _adam_math(w_ref[0], gg, m_ref[0], v_ref[0])

    lay = pl.BlockSpec((1, tr, c), lambda l, i, cr: (l, i, 0))
    flat = pl.BlockSpec((tr, c), lambda l, i, cr: (i, 0))
    return pl.pallas_call(
        body,
        grid_spec=pltpu.PrefetchScalarGridSpec(
            num_scalar_prefetch=1, grid=(2, r // tr), in_specs=[lay, flat, flat, lay, lay], out_specs=[lay] * 4),
        out_shape=[jax.ShapeDtypeStruct((2, r, c), F32)] * 4, compiler_params=_cparams(("parallel", "parallel")), name=name,
    )(c_idx, w, g_mine, g_sib, m, v)


def _adamw_interleaved(w, g_mine, g_sib, m, v, c_idx, name):
    r, _, c = w.shape
    tr = 128

    def body(c_ref, w_ref, gm_ref, gs_ref, m_ref, v_ref, g_ref, d_ref, nm_ref, nv_ref):
        mine_first = c_ref[0] == 0
        for l in range(2):
            gg = jnp.where(mine_first == (l == 0), gm_ref[...], gs_ref[...])
            g_ref[:, l, :] = gg
            d_ref[:, l, :], nm_ref[:, l, :], nv_ref[:, l, :] = _adam_math(w_ref[:, l, :], gg, m_ref[:, l, :], v_ref[:, l, :])

    lay = pl.BlockSpec((tr, 2, c), lambda i, cr: (i, 0, 0))
    flat = pl.BlockSpec((tr, c), lambda i, cr: (i, 0))
    return pl.pallas_call(
        body,
        grid_spec=pltpu.PrefetchScalarGridSpec(
            num_scalar_prefetch=1, grid=(pl.cdiv(r, tr),), in_specs=[lay, flat, flat, lay, lay], out_specs=[lay] * 4),
        out_shape=[jax.ShapeDtypeStruct((r, 2, c), F32)] * 4, compiler_params=_cparams(("parallel",)), name=name,
    )(c_idx, w, g_mine, g_sib, m, v)


def _add_mine(g_all, recv, c_idx, name):
    _, r, c = g_all.shape
    tr = 512

    def body(c_ref, a_ref, b_ref, o_ref):
        o_ref[...] = (a_ref[0].astype(F32) + b_ref[...].astype(F32)).astype(BF16)

    return pl.pallas_call(
        body,
        grid_spec=pltpu.PrefetchScalarGridSpec(
            num_scalar_prefetch=1, grid=(pl.cdiv(r, tr),),
            in_specs=[pl.BlockSpec((1, tr, c), lambda i, cr: (cr[0], i, 0)), pl.BlockSpec((tr, c), lambda i, cr: (i, 0))],
            out_specs=pl.BlockSpec((tr, c), lambda i, cr: (i, 0)),
        ),
        out_shape=jax.ShapeDtypeStruct((r, c), BF16), compiler_params=_cparams(("parallel",)), name=name,
    )(c_idx, g_all, recv)


def _sum_slots(x, own, sel, name):
    s, r, c = x.shape
    tr = 128

    def body(sel_ref, x_ref, own_ref, o_ref):
        mine = own_ref[0].astype(F32)
        acc = jnp.zeros((tr, c), F32)
        for k in range(s):
            acc = acc + jnp.where(sel_ref[0] == k, mine, x_ref[k].astype(F32))
        o_ref[...] = acc

    return pl.pallas_call(
        body,
        grid_spec=pltpu.PrefetchScalarGridSpec(
            num_scalar_prefetch=1, grid=(pl.cdiv(r, tr),),
            in_specs=[pl.BlockSpec((s, tr, c), lambda i, sr: (0, i, 0)), pl.BlockSpec((1, tr, c), lambda i, sr: (sr[1], i, 0))],
            out_specs=pl.BlockSpec((tr, c), lambda i, sr: (i, 0)),
        ),
        out_shape=jax.ShapeDtypeStruct((r, c), F32), compiler_params=_cparams(("parallel",)), name=name,
    )(sel, x, own)


_HBM = pl.BlockSpec(memory_space=pl.ANY)


def _place():
    x, y, c = lax.axis_index("x"), lax.axis_index("y"), lax.axis_index("c")
    chips = [(1 - x, y), (x, 1 - y), (1 - x, 1 - y)]
    return x, y, c, chips


def _gather_weights(bufs, name):
    n = len(bufs)

    def body(*refs):
        o_refs = refs[n:2 * n]
        ssem, rsem = refs[2 * n:]
        x, y, c, chips = _place()
        me = 2 * x + y
        sib = (x, y, 1 - c)

        def copy(a, k, src, dst, to):
            return pltpu.make_async_remote_copy(src_ref=src, dst_ref=dst, send_sem=ssem.at[a, k], recv_sem=rsem.at[a, k],
                                                device_id=to, device_id_type=MESH)

        sends = []
        for a in range(n):
            mine = o_refs[a].at[c, me]
            for j, chip in enumerate(chips):
                cp = copy(a, j, mine, mine, (chip[0], chip[1], c))
                cp.start()
                sends.append(cp)
        for a in range(n):
            for j, chip in enumerate(chips):
                slot = o_refs[a].at[c, 2 * chip[0] + chip[1]]
                copy(a, j, slot, slot, sib).wait_recv()
                cp = copy(a, 3 + j, slot, slot, sib)
                cp.start()
                sends.append(cp)
        for a in range(n):
            for j, chip in enumerate(chips):
                slot = o_refs[a].at[1 - c, 2 * chip[0] + chip[1]]
                copy(a, 3 + j, slot, slot, sib).wait_recv()
        for cp in sends:
            cp.wait_send()

    return pl.pallas_call(
        body, in_specs=[_HBM] * n, out_specs=[_HBM] * n,
        out_shape=[jax.ShapeDtypeStruct(b.shape, b.dtype) for b in bufs],
        scratch_shapes=[pltpu.SemaphoreType.DMA((n, 6)), pltpu.SemaphoreType.DMA((n, 6))],
        input_output_aliases={a: a for a in range(n)},
        name=name,
    )(*bufs)


def _swap_other_layer(gs, name):
    n = len(gs)

    def body(*refs):
        g_refs, o_refs = refs[:n], refs[n:2 * n]
        ssem, rsem = refs[2 * n:]
        x, y, c, _ = _place()
        cps = []
        for a in range(n):
            cp = pltpu.make_async_remote_copy(src_ref=g_refs[a].at[1 - c], dst_ref=o_refs[a], send_sem=ssem.at[a],
                                              recv_sem=rsem.at[a], device_id=(x, y, 1 - c), device_id_type=MESH)
            cp.start()
            cps.append(cp)
        for cp in cps:
            cp.wait()

    return pl.pallas_call(
        body, in_specs=[_HBM] * n, out_specs=[_HBM] * n,
        out_shape=[jax.ShapeDtypeStruct(g.shape[1:], g.dtype) for g in gs],
        scratch_shapes=[pltpu.SemaphoreType.DMA((n,)), pltpu.SemaphoreType.DMA((n,))], name=name,
    )(*gs)


def _scatter_to_chips(ps, name):
    n = len(ps)

    def body(*refs):
        p_refs, o_refs = refs[:n], refs[n:2 * n]
        ssem, rsem = refs[2 * n:]
        x, y, c, chips = _place()
        me = 2 * x + y
        sends = []
        for a in range(n):
            for j, chip in enumerate(chips):
                cp = pltpu.make_async_remote_copy(
                    src_ref=p_refs[a].at[2 * chip[0] + chip[1]], dst_ref=o_refs[a].at[me], send_sem=ssem.at[a, j],
                    recv_sem=rsem.at[a, j], device_id=(chip[0], chip[1], c), device_id_type=MESH)
                cp.start()
                sends.append(cp)
        for a in range(n):
            for j, chip in enumerate(chips):
                slot = o_refs[a].at[2 * chip[0] + chip[1]]
                pltpu.make_async_remote_copy(src_ref=slot, dst_ref=slot, send_sem=ssem.at[a, j], recv_sem=rsem.at[a, j],
                                             device_id=(x, y, c), device_id_type=MESH).wait_recv()
        for cp in sends:
            cp.wait_send()

    return pl.pallas_call(
        body, in_specs=[_HBM] * n, out_specs=[_HBM] * n,
        out_shape=[jax.ShapeDtypeStruct(p.shape, p.dtype) for p in ps],
        scratch_shapes=[pltpu.SemaphoreType.DMA((n, 3)), pltpu.SemaphoreType.DMA((n, 3))],
        name=name,
    )(*ps)


def _swap_with_sibling(fs, name):
    n = len(fs)

    def body(*refs):
        f_refs, o_refs = refs[:n], refs[n:2 * n]
        ssem, rsem = refs[2 * n:]
        x, y, c, _ = _place()
        cps = []
        for a in range(n):
            cp = pltpu.make_async_remote_copy(src_ref=f_refs[a], dst_ref=o_refs[a], send_sem=ssem.at[a],
                                              recv_sem=rsem.at[a], device_id=(x, y, 1 - c), device_id_type=MESH)
            cp.start()
            cps.append(cp)
        for cp in cps:
            cp.wait()

    return pl.pallas_call(
        body, in_specs=[_HBM] * n, out_specs=[_HBM] * n,
        out_shape=[jax.ShapeDtypeStruct(f.shape, f.dtype) for f in fs],
        scratch_shapes=[pltpu.SemaphoreType.DMA((n,)), pltpu.SemaphoreType.DMA((n,))],
        name=name,
    )(*fs)


def _gather_all(buf, name):
    def body(b_ref, o_ref, ssem, rsem):
        x, y, c, _ = _place()
        me = 4 * x + 2 * y + c
        flips = [(fx, fy, fc) for fx in (0, 1) for fy in (0, 1) for fc in (0, 1)][1:]
        peers = [((1 - x) if fx else x, (1 - y) if fy else y, (1 - c) if fc else c) for fx, fy, fc in flips]
        sends = []
        for k, peer in enumerate(peers):
            cp = pltpu.make_async_remote_copy(src_ref=b_ref, dst_ref=o_ref.at[me], send_sem=ssem.at[k], recv_sem=rsem.at[k],
                                              device_id=peer, device_id_type=MESH)
            cp.start()
            sends.append(cp)
        for k, peer in enumerate(peers):
            slot = o_ref.at[4 * peer[0] + 2 * peer[1] + peer[2]]
            pltpu.make_async_remote_copy(src_ref=slot, dst_ref=slot, send_sem=ssem.at[k], recv_sem=rsem.at[k],
                                         device_id=(x, y, c), device_id_type=MESH).wait_recv()
        for cp in sends:
            cp.wait_send()

    return pl.pallas_call(
        body, in_specs=[_HBM], out_specs=_HBM,
        out_shape=jax.ShapeDtypeStruct((N_DEV,) + buf.shape, buf.dtype),
        scratch_shapes=[pltpu.SemaphoreType.DMA((N_DEV - 1,)), pltpu.SemaphoreType.DMA((N_DEV - 1,))],
        name=name,
    )(buf)


def _layer_forward(x, h, wts, sm, d):
    t = x.shape[0]
    heads = d // LANE
    bq, _ = _attn_tiles(t)
    proj = _matmul(h, wts["w_in_t"], "nt", F32, "proj_fwd", n=7 * d, tn_cap=1792)
    f_t = _matmul(wts["w_f_t"], h, "nt", F32, "forget_fwd", tn_cap=1024)
    c_t = _fox_prep(f_t, sm["b_f"], "fox_prep")
    c_col = c_t.reshape(heads, t, 1)
    c_row = c_t.reshape(heads, t // bq, 1, bq)
    o, lse = _attn_fwd(proj, c_col, c_row, d, "attn_fwd")
    merged = _mix_fwd(proj, o, sm["wm"], sm["bs"], sm["g_v"], d, "mix_fwd")
    z = _matmul(merged, wts["w_out"], "nn", F32, "out_fwd")
    x1, h2 = _norm_fwd(x, z, sm["g_post"], sm["g_fpre"], "norm_mid")
    a = _matmul(h2, wts["w_g_t"], "nt", F32, "gate_fwd", tn_cap=1408)
    b = _matmul(h2, wts["w_u_t"], "nt", F32, "up_fwd", tn_cap=1408)
    mm = _swiglu_fwd(a, b, "swiglu_fwd")
    z2 = _matmul(mm, wts["w_d"], "nn", F32, "down_fwd")
    return dict(x=x, h=h, proj=proj, f_t=f_t, c_col=c_col, c_row=c_row, o=o, lse=lse, merged=merged, z=z, x1=x1,
                h2=h2, a=a, b=b, mm=mm, z2=z2)


def _layer_backward(l, dz2, dx2, sv, wts, sm, d, pay):
    t = dx2.shape[0]
    heads = d // LANE
    ff = wts["w_d"].shape[0]
    in_w = 7 * d + heads
    g = {}

    def payload(key, a, b, shape3, row0, name):
        pay[key] = _matmul(a, b, "tn", BF16, name, slab=(shape3, l, row0), into=pay.get(key), tm_cap=1408, tn_cap=1024,
                           tk_cap=1024)

    dm = _matmul(dz2, wts["w_d"], "nt", F32, "down_bwd_x", tn_cap=1408, tk_cap=1024)
    payload("w_d", sv["mm"], dz2, (2, ff, d), 0, "down_bwd_w")
    da, db = _swiglu_bwd(sv["a"], sv["b"], dm, "swiglu_bwd")
    dh2 = _matmul_pieces([(da, wts["w_g_t"], 0), (db, wts["w_u_t"], 0)], None, "gu_bwd_x", tk=_tile(ff, 1408), tm_cap=1024)
    payload("w_g", da, sv["h2"], (2, ff, d), 0, "gate_bwd_w")
    payload("w_u", db, sv["h2"], (2, ff, d), 0, "up_bwd_w")
    dx1, dz, g["g_fpre"], g["g_post"] = _norm_bwd(dx2, (dh2, sv["x1"], sm["g_fpre"]), (sv["z"], sm["g_post"]), "norm_bwd_mid")
    dmerged = _matmul(dz, wts["w_out"], "nt", F32, "out_bwd_x", tk_cap=1024)
    payload("w_out", sv["merged"], dz, (2, d, d), 0, "out_bwd_w")
    d_uv, d_g, do, g["w_s"], g["b_s"], g["g_v"] = _mix_bwd(dmerged, sv["proj"], sv["o"], sm["wm"], sm["wm_t"], sm["bs"],
                                                         sm["g_v"], d, "mix_bwd")
    dq, dk, dv, dc_q, dc_k = _attn_bwd(sv["proj"], do, sv["o"], sv["lse"], sv["c_col"], sv["c_row"], d, "attn_bwd")
    df_t, g["b_f"] = _fox_bwd(dc_q.reshape(heads, t), dc_k.reshape(heads, t), sv["f_t"], sm["b_f"], "fox_bwd")
    df_b = df_t.astype(BF16)
    dh_f = _matmul(df_b, wts["w_f_t"], "tn", F32, "forget_bwd_x")
    pieces = [(d_uv, COL_U), (dq, COL_Q), (dk, COL_K), (dv, COL_VA), (d_g, COL_GA)]
    dh = _matmul_pieces([(p, wts["w_in_t"], col * d) for p, col in pieces], dh_f, "proj_bwd_x", tk=_tile(d, 512), tm_cap=1024)
    for p, col in pieces:
        payload("w_in", p, sv["h"], (2, in_w, d), col * d, "proj_bwd_w")
    g["w_f_t"] = _matmul(df_b, sv["h"], "nn", F32, "forget_bwd_w", tk_cap=1024)
    return dh, dx1, g


def _small_pack(parts):
    flat = jnp.concatenate([p.reshape(-1) for p in parts])
    n = flat.shape[0]
    pad = (-n) % (LANE * LANE)
    return jnp.pad(flat, (0, pad)).reshape(-1, LANE)


def kernel(x, mix_pre_g, w_in, b_forget, sgu_norm_g, w_spatial, b_spatial, w_out, mix_post_g, ffn_pre_g, w_gate, w_up, w_down, ffn_post_g, loss_target, m_mix_pre_g, m_w_in, m_b_forget, m_sgu_norm_g, m_w_spatial, m_b_spatial, m_w_out, m_mix_post_g, m_ffn_pre_g, m_w_gate, m_w_up, m_w_down, m_ffn_post_g, v_mix_pre_g, v_w_in, v_b_forget, v_sgu_norm_g, v_w_spatial, v_b_spatial, v_w_out, v_mix_post_g, v_ffn_pre_g, v_w_gate, v_w_up, v_w_down, v_ffn_post_g):
    depth, d = mix_pre_g.shape
    assert depth == 2, "core c of a chip owns layer c"
    heads = d // LANE
    t = x.shape[1]
    ff = w_down.shape[1] * N_CHIPS
    in_w = w_in.shape[2] * N_CHIPS
    assert in_w == 7 * d + heads
    xs = x.reshape(t, d)
    target = loss_target.reshape(t, d)
    c_idx = lax.axis_index("c").astype(jnp.int32).reshape(1)
    chip = 2 * lax.axis_index("x") + lax.axis_index("y")
    dev = 2 * chip + lax.axis_index("c")

    def in_view(w):
        return jnp.transpose(w, (2, 0, 1))

    def gu_view(w):
        return jnp.transpose(w, (0, 2, 1))

    own = [jnp.transpose(in_view(w_in).astype(BF16), (1, 0, 2)), w_out.astype(BF16), gu_view(w_gate).astype(BF16),
           gu_view(w_up).astype(BF16), w_down.astype(BF16)]
    bufs = [lax.dynamic_update_slice(jnp.zeros((2, N_CHIPS) + o.shape[1:], BF16), o[:, None], (0, chip, 0, 0)) for o in own]
    g_in, g_out, g_g, g_u, g_d = _gather_weights(bufs, "gather_weights")

    wts = []
    for l in range(depth):
        w_in_t = g_in[l].reshape(in_w, d)
        wts.append(dict(w_in_t=w_in_t, w_f_t=w_in_t[7 * d:], w_out=g_out[l].reshape(d, d), w_g_t=g_g[l].reshape(ff, d),
                        w_u_t=g_u[l].reshape(ff, d), w_d=g_d[l].reshape(ff, d)))

    tril = jnp.tril(jnp.ones((LANE, LANE), bool))
    smalls = []
    for l in range(depth):
        wm = jnp.where(tril[None], w_spatial[l], 0.0).astype(BF16)
        smalls.append(dict(
            b_f=b_forget[l].reshape(heads, 1), wm=wm, wm_t=jnp.swapaxes(wm, 1, 2), bs=b_spatial[l].reshape(heads, LANE, 1),
            g_v=sgu_norm_g[l].reshape(1, d), g_pre=mix_pre_g[l].reshape(1, d), g_post=mix_post_g[l].reshape(1, d),
            g_fpre=ffn_pre_g[l].reshape(1, d), g_fpost=ffn_post_g[l].reshape(1, d)))

    saved = []
    xin = xs
    h = _norm_fwd(xs, None, None, smalls[0]["g_pre"], "norm_first")
    for l in range(depth):
        sv = _layer_forward(xin, h, wts[l], smalls[l], d)
        saved.append(sv)
        g_next = smalls[l + 1]["g_pre"] if l + 1 < depth else smalls[l]["g_pre"]
        xin, h = _norm_fwd(sv["x1"], sv["z2"], smalls[l]["g_fpost"], g_next, "norm_out")
    dy, loss_part = _loss_grad(xin, target, "loss")
    loss = lax.psum(jnp.sum(loss_part), ("x", "y", "c"))

    grads = [None] * depth
    pay = {}
    dx2 = dy
    dz2, g_fpost = _norm_bwd(dx2, None, (saved[depth - 1]["z2"], smalls[depth - 1]["g_fpost"]), "norm_bwd_top")
    for l in reversed(range(depth)):
        dh, dx1, g = _layer_backward(l, dz2, dx2, saved[l], wts[l], smalls[l], d, pay)
        g["g_fpost"] = g_fpost
        if l > 0:
            dx2, dz2, g["g_pre"], g_fpost = _norm_bwd(dx1, (dh, saved[l]["x"], smalls[l]["g_pre"]),
                                                       (saved[l - 1]["z2"], smalls[l - 1]["g_fpost"]), "norm_bwd_between")
        else:
            grad_x, g["g_pre"] = _norm_bwd(dx1, (dh, saved[l]["x"], smalls[l]["g_pre"]), None, "norm_bwd_bottom")
        grads[l] = g

    w_f_rows = jnp.stack([grads[l]["w_f_t"] for l in range(depth)]).astype(BF16)
    pay["w_in"] = lax.dynamic_update_slice(pay["w_in"], w_f_rows, (0, 7 * d, 0))
    flat2 = [pay["w_in"], pay["w_out"], pay["w_g"], pay["w_u"], pay["w_d"]]
    from_sib = _swap_other_layer(flat2, "swap_grads")
    chip_part = [_add_mine(a, b, c_idx, "add_sibling") for a, b in zip(flat2, from_sib)]
    chip_part = [p.reshape(N_CHIPS, p.shape[0] // N_CHIPS, d) for p in chip_part]
    landed = _scatter_to_chips(chip_part, "scatter_grads")
    chip_sel = jnp.stack([chip, chip]).astype(jnp.int32)
    mine = [_sum_slots(got, sent, chip_sel, "sum_chips") for got, sent in zip(landed, chip_part)]
    theirs = _swap_with_sibling(mine, "share_grads")

    small_names = ["g_pre", "b_f", "g_v", "w_s", "b_s", "g_post", "g_fpre", "g_fpost"]
    small_shapes = [mix_pre_g.shape, b_forget.shape, sgu_norm_g.shape, w_spatial.shape, b_spatial.shape, mix_post_g.shape,
                    ffn_pre_g.shape, ffn_post_g.shape]
    parts = [jnp.stack([grads[l][nme].reshape(-1) for l in range(depth)]) for nme in small_names]
    packed = _small_pack(parts)
    dev_sel = jnp.stack([dev, jnp.zeros_like(dev)]).astype(jnp.int32)
    total = _sum_slots(_gather_all(packed, "gather_small"), packed[None], dev_sel, "sum_small").reshape(-1)
    small_grads, off = {}, 0
    for nme, shp in zip(small_names, small_shapes):
        n = math.prod(shp)
        small_grads[nme] = total[off:off + n].reshape(shp)
        off += n

    def adam_small(w, g, m, v):
        shp = w.shape
        if w.ndim >= 3 and shp[-1] >= LANE:
            two = (math.prod(shp[:-1]), shp[-1])
        else:
            two = (1, math.prod(shp)) if math.prod(shp) < LANE else (math.prod(shp) // LANE, LANE)
        outs = _adamw(w.reshape(two), g.reshape(two), m.reshape(two), v.reshape(two), "adamw")
        return [g] + [o.reshape(shp) for o in outs]

    def adam_in(w, m, v):
        outs = _adamw_interleaved(in_view(w), mine[0], theirs[0], in_view(m), in_view(v), c_idx, "adamw_in")
        return [jnp.transpose(o, (1, 2, 0)) for o in outs]

    def adam_gu(k, w, m, v):
        outs = _adamw_layers(gu_view(w), mine[k], theirs[k], gu_view(m), gu_view(v), c_idx, "adamw_layers")
        return [jnp.transpose(o, (0, 2, 1)) for o in outs]

    def adam_rows(k, w, m, v):
        return _adamw_layers(w, mine[k], theirs[k], m, v, c_idx, "adamw_layers")

    results = [
        adam_small(mix_pre_g, small_grads["g_pre"], m_mix_pre_g, v_mix_pre_g),
        adam_in(w_in, m_w_in, v_w_in),
        adam_small(b_forget, small_grads["b_f"], m_b_forget, v_b_forget),
        adam_small(sgu_norm_g, small_grads["g_v"], m_sgu_norm_g, v_sgu_norm_g),
        adam_small(w_spatial, small_grads["w_s"], m_w_spatial, v_w_spatial),
        adam_small(b_spatial, small_grads["b_s"], m_b_spatial, v_b_spatial),
        adam_rows(1, w_out, m_w_out, v_w_out),
        adam_small(mix_post_g, small_grads["g_post"], m_mix_post_g, v_mix_post_g),
        adam_small(ffn_pre_g, small_grads["g_fpre"], m_ffn_pre_g, v_ffn_pre_g),
        adam_gu(2, w_gate, m_w_gate, v_w_gate),
        adam_gu(3, w_up, m_w_up, v_w_up),
        adam_rows(4, w_down, m_w_down, v_w_down),
        adam_small(ffn_post_g, small_grads["g_fpost"], m_ffn_post_g, v_ffn_post_g),
    ]
    gs, deltas, new_ms, new_vs = zip(*results)
    return (loss, grad_x.reshape(x.shape), *gs, *deltas, *new_ms, *new_vs)
```

```python
import functools
import math

import jax
import jax.numpy as jnp
from jax import lax
from jax.experimental import pallas as pl
from jax.experimental.pallas import tpu as pltpu

F32 = jnp.float32
BF16 = jnp.bfloat16

EPS = 1e-6
LANE = 128
SUBLANE = 8
N_CHIPS = 4
N_DEV = 8
VMEM_LIMIT = 48 * 1024 * 1024
MESH = pl.DeviceIdType.MESH

ADAM_LR = 0.001
ADAM_B1 = 0.9
ADAM_B2 = 0.999
ADAM_EPS = 1e-08
ADAM_WD = 0.01
ADAM_STEP = 10
ADAM_C1 = 1.0 / (1.0 - ADAM_B1 ** ADAM_STEP)
ADAM_C2 = 1.0 / (1.0 - ADAM_B2 ** ADAM_STEP)

GELU_K = math.sqrt(2.0 / math.pi)
GELU_A = 0.044715
NEG = -1e30
LOG2E = 1.4426950408889634
LN2 = 0.6931471805599453

COL_U, COL_V, COL_Q, COL_K, COL_VA, COL_GA, COL_GB, COL_F = range(8)


def _cparams(sem=None):
    return pltpu.CompilerParams(dimension_semantics=sem, vmem_limit_bytes=VMEM_LIMIT)


def _tile(n, cap):
    best = None
    for t in range(LANE, min(n, cap) + 1, LANE):
        if n % t == 0:
            best = t
    return best if best is not None else n


def _rows(n, cap):
    best = None
    for t in range(SUBLANE, min(n, cap) + 1, SUBLANE):
        if n % t == 0:
            best = t
    return best if best is not None else n


def _gelu_and_grad(x):
    x2 = x * x
    t = jnp.tanh(GELU_K * (x + GELU_A * x2 * x))
    g = 0.5 * x * (1.0 + t)
    dg = 0.5 * (1.0 + t) + 0.5 * x * (1.0 - t * t) * (GELU_K * (1.0 + 3.0 * GELU_A * x2))
    return g, dg


def _sigmoid(x):
    return 1.0 / (1.0 + jnp.exp(-x))


def _sum8(v):
    n, d = v.shape
    return v.reshape(n // SUBLANE, SUBLANE, d).sum(axis=0)


def _nt_dot(a, b):
    return lax.dot_general(a, b, (((1,), (1,)), ((), ())), preferred_element_type=F32)


_HBM = pl.BlockSpec(memory_space=pl.ANY)


def _place():
    x, y, c = lax.axis_index("x"), lax.axis_index("y"), lax.axis_index("c")
    chips = [(1 - x, y), (x, 1 - y), (1 - x, 1 - y)]
    return x, y, c, chips


class _Job:
    def __init__(self, ins, inout, fresh, nsem, first, mid, last):
        self.ins, self.inout, self.fresh, self.nsem = list(ins), list(inout), list(fresh), nsem
        self.first, self.mid, self.last = first, mid, last


def _call(body, *, grid, in_specs, out_specs, out_shape, scratch_shapes, dims, name, args, aliases=None, job=None):
    single = not isinstance(out_shape, (list, tuple))
    out_specs = [out_specs] if single else list(out_specs)
    out_shape = [out_shape] if single else list(out_shape)
    aliases = dict(aliases or {})
    if job is None:
        outs = pl.pallas_call(body, grid=grid, in_specs=in_specs, out_specs=out_specs, out_shape=out_shape,
                              scratch_shapes=scratch_shapes, input_output_aliases=aliases, compiler_params=_cparams(dims),
                              name=name)(*args)
        return (outs[0] if single else outs), []
    n_in, n_out, n_scr = len(args), len(out_shape), len(scratch_shapes)
    n_ji, n_jio, n_jf = len(job.ins), len(job.inout), len(job.fresh)
    total = math.prod(grid)

    def wrapped(*refs):
        host_in = refs[:n_in]
        pos = n_in
        j_ins = refs[pos:pos + n_ji]
        pos += n_ji + n_jio
        host_out = refs[pos:pos + n_out]
        pos += n_out
        j_inout = refs[pos:pos + n_jio]
        pos += n_jio
        j_fresh = refs[pos:pos + n_jf]
        pos += n_jf
        host_scr = refs[pos:pos + n_scr]
        ssem, rsem = refs[pos + n_scr:]
        flat = 0
        for ax, size in enumerate(grid):
            flat = flat * size + pl.program_id(ax)

        def hook(fn, at):
            if fn is not None:
                @pl.when(flat == at)
                def _():
                    fn(j_ins, j_inout, j_fresh, ssem, rsem)

        hook(job.first, 0)
        body(*host_in, *host_out, *host_scr)
        hook(job.mid, total // 2)
        hook(job.last, total - 1)

    for k in range(n_jio):
        aliases[n_in + n_ji + k] = n_out + k
    outs = pl.pallas_call(
        wrapped, grid=grid,
        in_specs=list(in_specs) + [_HBM] * (n_ji + n_jio),
        out_specs=out_specs + [_HBM] * (n_jio + n_jf),
        out_shape=out_shape + [jax.ShapeDtypeStruct(b.shape, b.dtype) for b in job.inout] + list(job.fresh),
        scratch_shapes=list(scratch_shapes) + [pltpu.SemaphoreType.DMA((job.nsem,)), pltpu.SemaphoreType.DMA((job.nsem,))],
        input_output_aliases=aliases, compiler_params=_cparams(tuple("arbitrary" for _ in grid)), name=name,
    )(*args, *job.ins, *job.inout)
    host = outs[:n_out]
    return (host[0] if single else host), outs[n_out:]


def _run_job(job, name):
    n_ji, n_jio, n_jf = len(job.ins), len(job.inout), len(job.fresh)

    def body(*refs):
        j_ins = refs[:n_ji]
        pos = n_ji + n_jio
        j_inout = refs[pos:pos + n_jio]
        j_fresh = refs[pos + n_jio:pos + n_jio + n_jf]
        ssem, rsem = refs[pos + n_jio + n_jf:]
        for fn in (job.first, job.mid, job.last):
            if fn is not None:
                fn(j_ins, j_inout, j_fresh, ssem, rsem)

    return pl.pallas_call(
        body, in_specs=[_HBM] * (n_ji + n_jio), out_specs=[_HBM] * (n_jio + n_jf),
        out_shape=[jax.ShapeDtypeStruct(b.shape, b.dtype) for b in job.inout] + list(job.fresh),
        scratch_shapes=[pltpu.SemaphoreType.DMA((job.nsem,)), pltpu.SemaphoreType.DMA((job.nsem,))],
        input_output_aliases={n_ji + k: k for k in range(n_jio)}, name=name,
    )(*job.ins, *job.inout)


_DIMS ={"nn": ((1,), (0,)), "nt": ((1,), (1,)), "tn": ((0,), (0,))}


def _matmul(a, b, mode, out_dtype, name, n=None, slab=None, into=None, job=None, tm_cap=512, tn_cap=2048, tk_cap=1408):
    if mode == "nn":
        (m, k), (k2, nn_) = a.shape, b.shape
    elif mode == "nt":
        (m, k), (nn_, k2) = a.shape, b.shape
    else:
        (k, m), (k2, nn_) = a.shape, b.shape
    n = nn_ if n is None else n
    assert k == k2, (a.shape, b.shape, mode)
    tm, tn, tk = _tile(m, tm_cap), _tile(n, tn_cap), _tile(k, tk_cap)
    if slab is not None and slab[2]:
        tm = _tile(math.gcd(m, slab[2]), tm_cap)
    nk = k // tk
    if mode == "tn":
        a_spec = pl.BlockSpec((tk, tm), lambda j, i, kk, *_: (kk, i))
    else:
        a_spec = pl.BlockSpec((tm, tk), lambda j, i, kk, *_: (i, kk))
    if mode == "nt":
        b_spec = pl.BlockSpec((tn, tk), lambda j, i, kk, *_: (j, kk))
    else:
        b_spec = pl.BlockSpec((tk, tn), lambda j, i, kk, *_: (kk, j))
    dims = (_DIMS[mode], ((), ()))
    aliased = into is not None

    def body(*refs):
        a_ref, b_ref = refs[0], refs[1]
        o_ref = refs[3] if aliased else refs[2]
        p = lax.dot_general(a_ref[...], b_ref[...], dims, preferred_element_type=F32)
        if nk == 1:
            o_ref[...] = p.astype(out_dtype).reshape(o_ref.shape)
        else:
            acc = refs[-1]
            kk = pl.program_id(2)

            @pl.when(kk == 0)
            def _():
                acc[...] = p

            @pl.when(kk > 0)
            def _():
                acc[...] += p

            @pl.when(kk == nk - 1)
            def _():
                o_ref[...] = acc[...].astype(out_dtype).reshape(o_ref.shape)

    if slab is None:
        out_spec = pl.BlockSpec((tm, tn), lambda j, i, kk: (i, j))
        out_shape = jax.ShapeDtypeStruct((m, n), out_dtype)
    else:
        shape3, lead, row0 = slab
        assert row0 % tm == 0 and shape3[2] == n
        out_spec = pl.BlockSpec((1, tm, tn), lambda j, i, kk: (lead, row0 // tm + i, j))
        out_shape = jax.ShapeDtypeStruct(shape3, out_dtype)
    in_specs, args = [a_spec, b_spec], [a, b]
    if aliased:
        in_specs.append(pl.BlockSpec(memory_space=pl.ANY))
        args.append(into)
    out, moved = _call(
        body, grid=(n // tn, m // tm, nk), in_specs=in_specs, out_specs=out_spec, out_shape=out_shape,
        scratch_shapes=[pltpu.VMEM((tm, tn), F32)] if nk > 1 else [], dims=("parallel", "parallel", "arbitrary"), name=name,
        args=args, aliases={2: 0} if aliased else None, job=job)
    return out if job is None else (out, moved)


def _matmul_pieces(pieces, addend, name, tk, job=None, tm_cap=512, tn_cap=1024):
    m = pieces[0][0].shape[0]
    n = pieces[0][1].shape[1]
    tm, tn = _tile(m, tm_cap), _tile(n, tn_cap)
    spans, s0 = [], 0
    for a, b, row0 in pieces:
        assert a.shape[1] % tk == 0 and row0 % tk == 0 and b.shape[1] == n and a.shape[0] == m
        spans.append((s0, a.shape[1] // tk, row0 // tk))
        s0 += a.shape[1] // tk
    steps = s0
    np_ = len(pieces)
    groups = []
    for (a, b, _), (first, count, brow) in zip(pieces, spans):
        if groups and groups[-1][0] is b and groups[-1][3] + groups[-1][2] == brow:
            groups[-1][2] += count
        else:
            groups.append([b, first, count, brow])
    b_of = []
    for first, count, _ in spans:
        b_of.append(next(k for k, g in enumerate(groups) if g[1] <= first < g[1] + g[2]))
    ng = len(groups)

    def body(*refs):
        o_ref, acc = refs[-2], refs[-1]
        s = pl.program_id(2)

        @pl.when(s == 0)
        def _():
            acc[...] = refs[np_ + ng][...] if addend is not None else jnp.zeros((tm, tn), F32)

        for p, (first, count, _) in enumerate(spans):
            @pl.when((s >= first) & (s < first + count))
            def _(p=p):
                acc[...] += jnp.dot(refs[p][...], refs[np_ + b_of[p]][...], preferred_element_type=F32)

        @pl.when(s == steps - 1)
        def _():
            o_ref[...] = acc[...]

    in_specs, args = [], []
    for (a, _, _), (first, count, _) in zip(pieces, spans):
        in_specs.append(pl.BlockSpec((tm, tk), lambda j, i, s, f=first, c=count: (i, jnp.clip(s - f, 0, c - 1))))
        args.append(a)
    for b, first, count, brow in groups:
        in_specs.append(pl.BlockSpec((tk, tn), lambda j, i, s, f=first, c=count, r=brow: (r + jnp.clip(s - f, 0, c - 1), j)))
        args.append(b)
    if addend is not None:
        in_specs.append(pl.BlockSpec((tm, tn), lambda j, i, s: (i, j)))
        args.append(addend)
    out, moved = _call(
        body, grid=(n // tn, m // tm, steps), in_specs=in_specs, out_specs=pl.BlockSpec((tm, tn), lambda j, i, s: (i, j)),
        out_shape=jax.ShapeDtypeStruct((m, n), F32), scratch_shapes=[pltpu.VMEM((tm, tn), F32)],
        dims=("parallel", "parallel", "arbitrary"), name=name, args=args, job=job)
    return out if job is None else (out, moved)


def _norm_fwd(x, z, g_post, g_next, name):
    t, d = x.shape
    tt = _rows(t, 512)
    row = pl.BlockSpec((tt, d), lambda i: (i, 0))
    vec = pl.BlockSpec((1, d), lambda i: (0, 0))

    def body(*refs):
        if z is None:
            x_ref, gn_ref, h_ref = refs
            xn = x_ref[...]
        else:
            x_ref, z_ref, gp_ref, gn_ref, xo_ref, h_ref = refs
            zz = z_ref[...]
            r = lax.rsqrt(jnp.mean(zz * zz, axis=-1, keepdims=True) + EPS)
            xn = x_ref[...] + zz * r * gp_ref[...]
            xo_ref[...] = xn
        r2 = lax.rsqrt(jnp.mean(xn * xn, axis=-1, keepdims=True) + EPS)
        h_ref[...] = (xn * r2 * gn_ref[...]).astype(BF16)

    if z is None:
        return pl.pallas_call(
            body, grid=(t // tt,), in_specs=[row, vec], out_specs=row,
            out_shape=jax.ShapeDtypeStruct((t, d), BF16), compiler_params=_cparams(("parallel",)), name=name,
        )(x, g_next)
    return pl.pallas_call(
        body, grid=(t // tt,), in_specs=[row, row, vec, vec], out_specs=[row, row],
        out_shape=[jax.ShapeDtypeStruct((t, d), F32), jax.ShapeDtypeStruct((t, d), BF16)],
        compiler_params=_cparams(("parallel",)), name=name,
    )(x, z, g_post, g_next)


def _rms_bwd(dy, x, g):
    r = lax.rsqrt(jnp.mean(x * x, axis=-1, keepdims=True) + EPS)
    n = x * r
    dn = dy * g
    dx = r * (dn - n * jnp.mean(dn * n, axis=-1, keepdims=True))
    return dx, dy * n


def _norm_bwd(dres, pre, post, name):
    t, d = dres.shape
    tt = _rows(t, 512)
    nt = t // tt
    row = pl.BlockSpec((tt, d), lambda i: (i, 0))
    vec = pl.BlockSpec((1, d), lambda i: (0, 0))
    has_pre, has_post = pre is not None, post is not None
    n_in = 1 + (3 if has_pre else 0) + (2 if has_post else 0)
    n_out = has_pre + has_post + has_pre + has_post

    def body(*refs):
        ins, outs, scr = refs[:n_in], refs[n_in:n_in + n_out], refs[n_in + n_out:]
        i = pl.program_id(0)
        dx = ins[0][...]
        pos, opos, spos = 1, 0, 0
        accs = []
        if has_pre:
            dh_ref, xa_ref, ga_ref = ins[pos:pos + 3]
            pos += 3
            dxa, dga_t = _rms_bwd(dh_ref[...], xa_ref[...], ga_ref[...])
            dx = dx + dxa
            outs[opos][...] = dx
            opos += 1
            accs.append((scr[spos], dga_t))
            spos += 1
        if has_post:
            zb_ref, gb_ref = ins[pos:pos + 2]
            dz, dgb_t = _rms_bwd(dx, zb_ref[...], gb_ref[...])
            outs[opos][...] = dz.astype(BF16)
            opos += 1
            accs.append((scr[spos], dgb_t))
            spos += 1
        for (acc, val), out in zip(accs, outs[opos:]):
            part = _sum8(val)

            @pl.when(i == 0)
            def _(acc=acc, part=part):
                acc[...] = part

            @pl.when(i > 0)
            def _(acc=acc, part=part):
                acc[...] += part

            @pl.when(i == nt - 1)
            def _(acc=acc, out=out):
                out[...] = jnp.sum(acc[...], axis=0, keepdims=True)

    in_specs, args = [row], [dres]
    out_specs, out_shape = [], []
    if has_pre:
        in_specs += [row, row, vec]
        args += list(pre)
        out_specs.append(row)
        out_shape.append(jax.ShapeDtypeStruct((t, d), F32))
    if has_post:
        in_specs += [row, vec]
        args += list(post)
        out_specs.append(row)
        out_shape.append(jax.ShapeDtypeStruct((t, d), BF16))
    for _ in range(has_pre + has_post):
        out_specs.append(vec)
        out_shape.append(jax.ShapeDtypeStruct((1, d), F32))
    return pl.pallas_call(
        body, grid=(nt,), in_specs=in_specs, out_specs=out_specs, out_shape=out_shape,
        scratch_shapes=[pltpu.VMEM((SUBLANE, d), F32)] * (has_pre + has_post),
        compiler_params=_cparams(("arbitrary",)), name=name,
    )(*args)


def _loss_grad(y, target, name):
    t, d = y.shape
    tt = _rows(t, 512)
    nt = t // tt
    row = pl.BlockSpec((tt, d), lambda i: (i, 0))
    inv_d = 1.0 / d

    def body(y_ref, t_ref, dy_ref, l_ref):
        i = pl.program_id(0)
        diff = y_ref[...] - t_ref[...]
        dy_ref[...] = diff * inv_d
        s8 = _sum8(diff * diff)
        part = s8[:, 0:LANE]
        for k in range(1, d // LANE):
            part = part + s8[:, k * LANE:(k + 1) * LANE]
        part = part * (0.5 * inv_d)

        @pl.when(i == 0)
        def _():
            l_ref[...] = part

        @pl.when(i > 0)
        def _():
            l_ref[...] += part

    return pl.pallas_call(
        body, grid=(nt,), in_specs=[row, row],
        out_specs=[row, pl.BlockSpec((SUBLANE, LANE), lambda i: (0, 0))],
        out_shape=[jax.ShapeDtypeStruct((t, d), F32), jax.ShapeDtypeStruct((SUBLANE, LANE), F32)],
        compiler_params=_cparams(("arbitrary",)), name=name,
    )(y, target)


def _swiglu_fwd(a, b, name):
    t, f = a.shape
    tt = _rows(t, 256)
    blk = pl.BlockSpec((tt, f), lambda i: (i, 0))

    def body(a_ref, b_ref, m_ref):
        av = a_ref[...]
        m_ref[...] = (av * _sigmoid(av) * b_ref[...]).astype(BF16)

    return pl.pallas_call(
        body, grid=(t // tt,), in_specs=[blk, blk], out_specs=blk,
        out_shape=jax.ShapeDtypeStruct((t, f), BF16), compiler_params=_cparams(("parallel",)), name=name,
    )(a, b)


def _swiglu_bwd(a, b, dm, name):
    t, f = a.shape
    tt = _rows(t, 256)
    blk = pl.BlockSpec((tt, f), lambda i: (i, 0))

    def body(a_ref, b_ref, dm_ref, da_ref, db_ref):
        av = a_ref[...]
        s = _sigmoid(av)
        dv = dm_ref[...]
        da_ref[...] = (dv * b_ref[...] * s * (1.0 + av * (1.0 - s))).astype(BF16)
        db_ref[...] = (dv * av * s).astype(BF16)

    return pl.pallas_call(
        body, grid=(t // tt,), in_specs=[blk, blk, blk], out_specs=[blk, blk],
        out_shape=[jax.ShapeDtypeStruct((t, f), BF16)] * 2, compiler_params=_cparams(("parallel",)), name=name,
    )(a, b, dm)


def _log_sigmoid(x):
    return jnp.minimum(x, 0.0) - jnp.log1p(jnp.exp(-jnp.abs(x)))


def _fox_prep(f_t, b_f, name):
    h, t = f_t.shape

    def body(f_ref, b_ref, c_ref):
        r = lax.broadcasted_iota(jnp.int32, (LANE, LANE), 0)
        c = lax.broadcasted_iota(jnp.int32, (LANE, LANE), 1)
        upper = (r <= c).astype(F32)
        carry = jnp.zeros((h, 1), F32)
        for j in range(t // LANE):
            sl = slice(j * LANE, (j + 1) * LANE)
            lf = _log_sigmoid(f_ref[:, sl] + b_ref[...])
            cs = jnp.dot(lf, upper, precision=lax.Precision.HIGHEST, preferred_element_type=F32) + carry
            c_ref[:, sl] = cs
            carry = cs[:, LANE - 1:LANE]

    return pl.pallas_call(body, out_shape=jax.ShapeDtypeStruct((h, t), F32), compiler_params=_cparams(), name=name)(f_t, b_f)


def _fox_bwd(dc_q, dc_k, f_t, b_f, name):
    h, t = f_t.shape

    def body(dq_ref, dk_ref, f_ref, b_ref, df_ref, db_ref):
        r = lax.broadcasted_iota(jnp.int32, (LANE, LANE), 0)
        c = lax.broadcasted_iota(jnp.int32, (LANE, LANE), 1)
        lower = (r >= c).astype(F32)
        carry = jnp.zeros((h, 1), F32)
        dbsum = jnp.zeros((h, 1), F32)
        for j in reversed(range(t // LANE)):
            sl = slice(j * LANE, (j + 1) * LANE)
            dc = dq_ref[:, sl] - dk_ref[:, sl]
            dl = jnp.dot(dc, lower, precision=lax.Precision.HIGHEST, preferred_element_type=F32) + carry
            carry = dl[:, 0:1]
            df = dl * _sigmoid(-(f_ref[:, sl] + b_ref[...]))
            df_ref[:, sl] = df
            dbsum = dbsum + jnp.sum(df, axis=-1, keepdims=True)
        db_ref[...] = dbsum

    return pl.pallas_call(
        body, out_shape=[jax.ShapeDtypeStruct((h, t), F32), jax.ShapeDtypeStruct((h, 1), F32)],
        compiler_params=_cparams(), name=name,
    )(dc_q, dc_k, f_t, b_f)


ATTN_Q = 512
ATTN_K = 256


def _attn_tiles(t):
    return _tile(t, ATTN_Q), _tile(t, ATTN_K)


def _attn_fwd(proj, c_col, c_row, d, name, job=None):
    t = proj.shape[0]
    h = d // LANE
    bq, bk = _attn_tiles(t)
    nq, nk, rr = t // bq, t // bk, bq // bk
    qc, kc, vc = COL_Q * h, COL_K * h, COL_VA * h
    qscale = LANE ** -0.5 * LOG2E

    def body(q_ref, k_ref, v_ref, cc_ref, cr_ref, o_ref, lse_ref, kb, vt, ckb, acc):
        i = pl.program_id(1)

        @pl.when(i == 0)
        def _():
            kb[...] = k_ref[...].astype(BF16)
            ckb[...] = jnp.broadcast_to(cc_ref[0] * LOG2E, (t, bq))
            for jn in range(nk):
                vt[jn] = v_ref[jn * bk:(jn + 1) * bk, :].T.astype(BF16)

        q = (q_ref[...] * qscale).astype(BF16)
        cq = cr_ref[0, 0] * LOG2E
        acc[...] = jnp.zeros((LANE, bq), F32)

        def block(j, diag, m_old, l_old):
            rows = pl.ds(pl.multiple_of(j * bk, bk), bk)
            s = _nt_dot(kb[rows, :], q) - ckb[rows, :]
            if diag is not None:
                kk = lax.broadcasted_iota(jnp.int32, (bk, bq), 0)
                qq = lax.broadcasted_iota(jnp.int32, (bk, bq), 1)
                s = jnp.where(qq >= kk + diag * bk, s, NEG)
            m_new = jnp.maximum(m_old, jnp.max(s, axis=0, keepdims=True) + cq)
            p = jnp.exp2(s + (cq - m_new))
            alpha = jnp.exp2(m_old - m_new)
            l_new = alpha * l_old + jnp.sum(p, axis=0, keepdims=True)
            acc[...] = alpha * acc[...] + jnp.dot(vt[j], p.astype(BF16), preferred_element_type=F32)
            return m_new, l_new

        m, l = lax.fori_loop(0, i * rr, lambda j, c: block(j, None, *c),
                             (jnp.full((1, bq), NEG, F32), jnp.zeros((1, bq), F32)))
        for jj in range(rr):
            m, l = block(i * rr + jj, jj, m, l)
        o_ref[...] = (acc[...] / l).T
        lse_ref[0, 0] = m + jnp.log2(l)

    rowq = pl.BlockSpec((1, 1, 1, bq), lambda hh, i: (hh, i, 0, 0))
    outs, moved = _call(
        body, grid=(h, nq),
        in_specs=[
            pl.BlockSpec((bq, LANE), lambda hh, i: (i, qc + hh)),
            pl.BlockSpec((t, LANE), lambda hh, i: (0, kc + hh)),
            pl.BlockSpec((t, LANE), lambda hh, i: (0, vc + hh)),
            pl.BlockSpec((1, t, 1), lambda hh, i: (hh, 0, 0)),
            rowq,
        ],
        out_specs=[pl.BlockSpec((bq, LANE), lambda hh, i: (i, hh)), rowq],
        out_shape=[jax.ShapeDtypeStruct((t, d), F32), jax.ShapeDtypeStruct((h, nq, 1, bq), F32)],
        scratch_shapes=[pltpu.VMEM((t, LANE), BF16), pltpu.VMEM((nk, LANE, bk), BF16), pltpu.VMEM((t, bq), F32),
                        pltpu.VMEM((LANE, bq), F32)],
        dims=("arbitrary", "arbitrary"), name=name, args=[proj, proj, proj, c_col, c_row], job=job)
    return outs if job is None else (outs, moved)


def _attn_bwd(proj, do, o, lse, c_col, c_row, d, name, job=None):
    t = proj.shape[0]
    h = d // LANE
    bq, bk = _attn_tiles(t)
    nq, nk, rr = t // bq, t // bk, bq // bk
    qc, kc, vc = COL_Q * h, COL_K * h, COL_VA * h
    scale = LANE ** -0.5

    def body(q_ref, k_ref, v_ref, do_ref, o_ref, lse_ref, cc_ref, cr_ref, dq_ref, dk_ref, dv_ref, dcq_ref, dck_ref,
             kb, kt, vb, ckb, dk_acc, dv_acc, dck_acc, dqt_acc):
        i = pl.program_id(1)

        @pl.when(i == 0)
        def _():
            kb[...] = k_ref[...].astype(BF16)
            vb[...] = v_ref[...].astype(BF16)
            ckb[...] = jnp.broadcast_to(cc_ref[0] * LOG2E, (t, bq))
            for jn in range(nk):
                kt[jn] = k_ref[jn * bk:(jn + 1) * bk, :].T.astype(BF16)
            dk_acc[...] = jnp.zeros((t, LANE), F32)
            dv_acc[...] = jnp.zeros((t, LANE), F32)
            dck_acc[...] = jnp.zeros((t, LANE), F32)

        q = (q_ref[...] * (scale * LOG2E)).astype(BF16)
        dof = do_ref[...]
        dob = dof.astype(BF16)
        delta = jnp.sum((dof * o_ref[...]).T, axis=0, keepdims=True)
        rowb = cr_ref[0, 0] * LOG2E - lse_ref[0, 0]
        dqt_acc[...] = jnp.zeros((LANE, bq), F32)

        def block(j, diag, dcq):
            rows = pl.ds(pl.multiple_of(j * bk, bk), bk)
            p = jnp.exp2(_nt_dot(kb[rows, :], q) - ckb[rows, :] + rowb)
            if diag is not None:
                kk = lax.broadcasted_iota(jnp.int32, (bk, bq), 0)
                qq = lax.broadcasted_iota(jnp.int32, (bk, bq), 1)
                p = jnp.where(qq >= kk + diag * bk, p, 0.0)
            dv_acc[rows, :] += jnp.dot(p.astype(BF16), dob, preferred_element_type=F32)
            ds = p * (_nt_dot(vb[rows, :], dob) - delta)
            dsb = ds.astype(BF16)
            dk_acc[rows, :] += jnp.dot(dsb, q, preferred_element_type=F32)
            dqt_acc[...] += jnp.dot(kt[j], dsb, preferred_element_type=F32)
            part = ds[:, 0:LANE]
            for k in range(1, bq // LANE):
                part = part + ds[:, k * LANE:(k + 1) * LANE]
            dck_acc[rows, :] += part
            return dcq + jnp.sum(ds, axis=0, keepdims=True)

        dcq = lax.fori_loop(0, i * rr, lambda j, c: block(j, None, c), jnp.zeros((1, bq), F32))
        for jj in range(rr):
            dcq = block(i * rr + jj, jj, dcq)
        dq_ref[...] = (dqt_acc[...] * scale).T.astype(BF16)
        dcq_ref[0, 0] = dcq

        @pl.when(i == nq - 1)
        def _():
            dk_ref[...] = (dk_acc[...] * LN2).astype(BF16)
            dv_ref[...] = dv_acc[...].astype(BF16)
            dck_ref[0] = jnp.sum(dck_acc[...], axis=-1, keepdims=True)

    rowq = pl.BlockSpec((1, 1, 1, bq), lambda hh, i: (hh, i, 0, 0))
    blk = pl.BlockSpec((bq, LANE), lambda hh, i: (i, hh))
    whole = pl.BlockSpec((t, LANE), lambda hh, i: (0, hh))
    colk = pl.BlockSpec((1, t, 1), lambda hh, i: (hh, 0, 0))
    outs, moved = _call(
        body, grid=(h, nq),
        in_specs=[
            pl.BlockSpec((bq, LANE), lambda hh, i: (i, qc + hh)),
            pl.BlockSpec((t, LANE), lambda hh, i: (0, kc + hh)),
            pl.BlockSpec((t, LANE), lambda hh, i: (0, vc + hh)),
            blk, blk, rowq, colk, rowq,
        ],
        out_specs=[blk, whole, whole, rowq, colk],
        out_shape=[jax.ShapeDtypeStruct((t, d), BF16), jax.ShapeDtypeStruct((t, d), BF16), jax.ShapeDtypeStruct((t, d), BF16),
                   jax.ShapeDtypeStruct((h, nq, 1, bq), F32), jax.ShapeDtypeStruct((h, t, 1), F32)],
        scratch_shapes=[pltpu.VMEM((t, LANE), BF16), pltpu.VMEM((nk, LANE, bk), BF16), pltpu.VMEM((t, LANE), BF16),
                        pltpu.VMEM((t, bq), F32), pltpu.VMEM((t, LANE), F32), pltpu.VMEM((t, LANE), F32),
                        pltpu.VMEM((t, LANE), F32), pltpu.VMEM((LANE, bq), F32)],
        dims=("arbitrary", "arbitrary"), name=name, args=[proj, proj, proj, do, o, lse, c_col, c_row], job=job)
    return outs if job is None else (outs, moved)


def _sgu_forward(u_ref, v_ref, gv_ref, wm_ref, bs_ref, mix_sc, groups):
    gu, dgu = _gelu_and_grad(u_ref[...])
    gvv, dgv = _gelu_and_grad(v_ref[...])
    mu = jnp.mean(gvv, axis=-1, keepdims=True)
    xc = gvv - mu
    r = lax.rsqrt(jnp.mean(xc * xc, axis=-1, keepdims=True) + EPS)
    nhat = xc * r
    vn = (nhat * gv_ref[...]).astype(BF16)
    for g in range(groups):
        sl = slice(g * LANE, (g + 1) * LANE)
        mix_sc[:, sl] = jnp.dot(wm_ref[g], vn[:, sl], preferred_element_type=F32) + bs_ref[g]
    return gu, dgu, dgv, nhat, r, vn, mix_sc[...]


def _mix_fwd(proj, o, wm, bs, g_v, d, name):
    t = proj.shape[0]
    groups = d // LANE

    def body(u_ref, v_ref, ga_ref, gb_ref, o_ref, wm_ref, bs_ref, gv_ref, out_ref, mix_sc):
        gu, _, _, _, _, _, mixed = _sgu_forward(u_ref, v_ref, gv_ref, wm_ref, bs_ref, mix_sc, groups)
        out_ref[...] = (_sigmoid(ga_ref[...]) * (gu * mixed) + _sigmoid(gb_ref[...]) * o_ref[...]).astype(BF16)

    def colblk(k):
        return pl.BlockSpec((LANE, d), lambda i, k=k: (i, k))

    full3 = pl.BlockSpec((groups, LANE, LANE), lambda i: (0, 0, 0))
    return pl.pallas_call(
        body, grid=(t // LANE,),
        in_specs=[colblk(COL_U), colblk(COL_V), colblk(COL_GA), colblk(COL_GB), colblk(0), full3,
                  pl.BlockSpec((groups, LANE, 1), lambda i: (0, 0, 0)), pl.BlockSpec((1, d), lambda i: (0, 0))],
        out_specs=colblk(0),
        out_shape=jax.ShapeDtypeStruct((t, d), BF16),
        scratch_shapes=[pltpu.VMEM((LANE, d), F32)],
        compiler_params=_cparams(("parallel",)), name=name,
    )(proj, proj, proj, proj, o, wm, bs, g_v)


def _mix_bwd(dmerged, proj, o, wm, wm_t, bs, g_v, d, name):
    t = proj.shape[0]
    groups = d // LANE
    nt = t // LANE

    def body(dm_ref, u_ref, v_ref, ga_ref, gb_ref, o_ref, wm_ref, wmt_ref, bs_ref, gv_ref,
             duv_ref, dg_ref, do_ref, dws_ref, dbs_ref, dgv_ref, mix_sc, dvn_sc, gv_acc):
        i = pl.program_id(0)

        @pl.when(i == 0)
        def _():
            dws_ref[...] = jnp.zeros_like(dws_ref)
            dbs_ref[...] = jnp.zeros_like(dbs_ref)
            gv_acc[...] = jnp.zeros_like(gv_acc)

        gu, dgu, dgv, nhat, r, vn, mixed = _sgu_forward(u_ref, v_ref, gv_ref, wm_ref, bs_ref, mix_sc, groups)
        dm = dm_ref[...]
        sa = _sigmoid(ga_ref[...])
        sb = _sigmoid(gb_ref[...])
        ov = o_ref[...]
        y_a = gu * mixed
        dg_ref[:, 0:d] = (dm * y_a * sa * (1.0 - sa)).astype(BF16)
        dg_ref[:, d:2 * d] = (dm * ov * sb * (1.0 - sb)).astype(BF16)
        do_ref[...] = dm * sb
        dy_a = dm * sa
        duv_ref[:, 0:d] = (dy_a * mixed * dgu).astype(BF16)
        dmixed = dy_a * gu
        dmixed_b = dmixed.astype(BF16)
        for g in range(groups):
            sl = slice(g * LANE, (g + 1) * LANE)
            dvn_sc[:, sl] = jnp.dot(wmt_ref[g], dmixed_b[:, sl], preferred_element_type=F32)
            dws_ref[g] += _nt_dot(dmixed_b[:, sl], vn[:, sl])
            dbs_ref[g] += jnp.sum(dmixed[:, sl], axis=-1, keepdims=True)
        dvn = dvn_sc[...]
        gv_acc[...] += _sum8(dvn * nhat)
        dn = dvn * gv_ref[...]
        dgelu = r * (dn - jnp.mean(dn, axis=-1, keepdims=True) - nhat * jnp.mean(dn * nhat, axis=-1, keepdims=True))
        duv_ref[:, d:2 * d] = (dgelu * dgv).astype(BF16)

        @pl.when(i == nt - 1)
        def _():
            dgv_ref[...] = jnp.sum(gv_acc[...], axis=0, keepdims=True)
            rr = lax.broadcasted_iota(jnp.int32, (LANE, LANE), 0)
            cl = lax.broadcasted_iota(jnp.int32, (LANE, LANE), 1)
            for g in range(groups):
                dws_ref[g] = jnp.where(rr >= cl, dws_ref[g], 0.0)

    def colblk(k):
        return pl.BlockSpec((LANE, d), lambda i, k=k: (i, k))

    full3 = pl.BlockSpec((groups, LANE, LANE), lambda i: (0, 0, 0))
    col3 = pl.BlockSpec((groups, LANE, 1), lambda i: (0, 0, 0))
    vec = pl.BlockSpec((1, d), lambda i: (0, 0))
    two = pl.BlockSpec((LANE, 2 * d), lambda i: (i, 0))
    return pl.pallas_call(
        body, grid=(nt,),
        in_specs=[colblk(0), colblk(COL_U), colblk(COL_V), colblk(COL_GA), colblk(COL_GB), colblk(0), full3, full3, col3, vec],
        out_specs=[two, two, colblk(0), full3, col3, vec],
        out_shape=[jax.ShapeDtypeStruct((t, 2 * d), BF16), jax.ShapeDtypeStruct((t, 2 * d), BF16), jax.ShapeDtypeStruct((t, d), F32),
                   jax.ShapeDtypeStruct((groups, LANE, LANE), F32), jax.ShapeDtypeStruct((groups, LANE, 1), F32),
                   jax.ShapeDtypeStruct((1, d), F32)],
        scratch_shapes=[pltpu.VMEM((LANE, d), F32), pltpu.VMEM((LANE, d), F32), pltpu.VMEM((SUBLANE, d), F32)],
        compiler_params=_cparams(("arbitrary",)), name=name,
    )(dmerged, proj, proj, proj, proj, o, wm, wm_t, bs, g_v)


def _adam_math(w, g, m, v):
    nm = ADAM_B1 * m + (1.0 - ADAM_B1) * g
    nv = ADAM_B2 * v + (1.0 - ADAM_B2) * (g * g)
    delta = -ADAM_LR * ((nm * ADAM_C1) / (jnp.sqrt(nv * ADAM_C2) + ADAM_EPS) + ADAM_WD * w)
    return delta, nm, nv


def _adamw(w, g, m, v, name):
    r, c = w.shape
    cap = max(SUBLANE, (2 * 1024 * 1024) // (4 * c) // SUBLANE * SUBLANE)
    tr = _rows(r, cap)

    def body(w_ref, g_ref, m_ref, v_ref, d_ref, nm_ref, nv_ref):
        d_ref[...], nm_ref[...], nv_ref[...] = _adam_math(w_ref[...], g_ref[...], m_ref[...], v_ref[...])

    blk = pl.BlockSpec((tr, c), lambda i: (i, 0))
    return pl.pallas_call(
        body, grid=(r // tr,), in_specs=[blk] * 4, out_specs=[blk] * 3,
        out_shape=[jax.ShapeDtypeStruct((r, c), F32)] * 3, compiler_params=_cparams(("parallel",)), name=name,
    )(w, g, m, v)


def _adamw_layers(w, g0, g1, m, v, name):
    _, r, c = w.shape
    cap = max(SUBLANE, (1024 * 1024) // (4 * c) // SUBLANE * SUBLANE)
    tr = _rows(r, cap)

    def body(w_ref, g0_ref, g1_ref, m_ref, v_ref, g_ref, d_ref, nm_ref, nv_ref):
        gg = jnp.where(pl.program_id(0) == 0, g0_ref[...], g1_ref[...])
        g_ref[0] = gg
        d_ref[0], nm_ref[0], nv_ref[0] = _adam_math(w_ref[0], gg, m_ref[0], v_ref[0])

    lay = pl.BlockSpec((1, tr, c), lambda l, i: (l, i, 0))

    def gspec(l0):
        return pl.BlockSpec((tr, c), lambda l, i: (jnp.where(l == l0, i, 0), 0))

    return pl.pallas_call(
        body, grid=(2, r // tr), in_specs=[lay, gspec(0), gspec(1), lay, lay], out_specs=[lay] * 4,
        out_shape=[jax.ShapeDtypeStruct((2, r, c), F32)] * 4, compiler_params=_cparams(("arbitrary", "arbitrary")), name=name,
    )(w, g0, g1, m, v)


def _adamw_interleaved(w, g0, g1, m, v, name):
    r, _, c = w.shape
    tr = 128

    def body(w_ref, g0_ref, g1_ref, m_ref, v_ref, g_ref, d_ref, nm_ref, nv_ref):
        for l, gl in enumerate((g0_ref, g1_ref)):
            gg = gl[...]
            g_ref[:, l, :] = gg
            d_ref[:, l, :], nm_ref[:, l, :], nv_ref[:, l, :] = _adam_math(w_ref[:, l, :], gg, m_ref[:, l, :], v_ref[:, l, :])

    lay = pl.BlockSpec((tr, 2, c), lambda i: (i, 0, 0))
    flat = pl.BlockSpec((tr, c), lambda i: (i, 0))
    return pl.pallas_call(
        body, grid=(pl.cdiv(r, tr),), in_specs=[lay, flat, flat, lay, lay], out_specs=[lay] * 4,
        out_shape=[jax.ShapeDtypeStruct((r, 2, c), F32)] * 4, compiler_params=_cparams(("parallel",)), name=name,
    )(w, g0, g1, m, v)


def _add_half(p4, recv, c_idx, name):
    _, r, c = p4.shape
    hw = c // 2
    tr = 256 if r % 256 == 0 else r

    def body(c_ref, a_ref, b_ref, o_ref):
        o_ref[...] = (a_ref[...].astype(F32) + b_ref[...].astype(F32)).astype(BF16)

    return pl.pallas_call(
        body,
        grid_spec=pltpu.PrefetchScalarGridSpec(
            num_scalar_prefetch=1, grid=(N_CHIPS, pl.cdiv(r, tr)),
            in_specs=[pl.BlockSpec((1, tr, hw), lambda s, i, cr: (s, i, cr[0])), pl.BlockSpec((1, tr, hw), lambda s, i, cr: (s, i, 0))],
            out_specs=pl.BlockSpec((1, tr, hw), lambda s, i, cr: (s, i, 0)),
        ),
        out_shape=jax.ShapeDtypeStruct((N_CHIPS, r, hw), BF16), compiler_params=_cparams(("parallel", "parallel")), name=name,
    )(c_idx, p4, recv)


def _sum_slots(x, own, sel, name, out_cols=None):
    s, r, c = x.shape
    tr = 128 if r % 128 == 0 else r

    def body(sel_ref, x_ref, own_ref, o_ref):
        mine = own_ref[0].astype(F32)
        acc = jnp.zeros((tr, c), F32)
        for k in range(s):
            acc = acc + jnp.where(sel_ref[0] == k, mine, x_ref[k].astype(F32))
        o_ref[...] = acc

    return pl.pallas_call(
        body,
        grid_spec=pltpu.PrefetchScalarGridSpec(
            num_scalar_prefetch=1, grid=(pl.cdiv(r, tr),),
            in_specs=[pl.BlockSpec((s, tr, c), lambda i, sr: (0, i, 0)), pl.BlockSpec((1, tr, c), lambda i, sr: (sr[1], i, 0))],
            out_specs=pl.BlockSpec((tr, c), lambda i, sr: (i, sr[2])),
        ),
        out_shape=jax.ShapeDtypeStruct((r, out_cols or c), F32), compiler_params=_cparams(("parallel",)), name=name,
    )(sel, x, own)


def _half_cols(width, hc):
    hw = width // 2
    assert hw % LANE == 0
    return pl.ds(pl.multiple_of(hc * hw, LANE), hw)


def _remote(src, dst, ssem, rsem, k, to):
    return pltpu.make_async_remote_copy(src_ref=src, dst_ref=dst, send_sem=ssem.at[k], recv_sem=rsem.at[k], device_id=to,
                                        device_id_type=MESH)


def _gather_job(bufs, items):
    def rows(o, a, l, slot, hc):
        return o[a].at[l, slot, :, _half_cols(bufs[a].shape[3], hc)]

    def first(ins, o, fresh, ssem, rsem):
        x, y, c, chips = _place()
        for k, (a, l) in enumerate(items):
            mine = rows(o, a, l, 2 * x + y, c)
            for j, chip in enumerate(chips):
                _remote(mine, mine, ssem, rsem, 6 * k + j, (chip[0], chip[1], c)).start()

    def mid(ins, o, fresh, ssem, rsem):
        x, y, c, chips = _place()
        for k, (a, l) in enumerate(items):
            for j, chip in enumerate(chips):
                got = rows(o, a, l, 2 * chip[0] + chip[1], c)
                _remote(got, got, ssem, rsem, 6 * k + j, (x, y, c)).wait_recv()
                _remote(got, got, ssem, rsem, 6 * k + 3 + j, (x, y, 1 - c)).start()

    def last(ins, o, fresh, ssem, rsem):
        x, y, c, chips = _place()
        for k, (a, l) in enumerate(items):
            for j, chip in enumerate(chips):
                got = rows(o, a, l, 2 * chip[0] + chip[1], 1 - c)
                _remote(got, got, ssem, rsem, 6 * k + 3 + j, (x, y, c)).wait_recv()
        for k, (a, l) in enumerate(items):
            mine = rows(o, a, l, 2 * x + y, c)
            for j, chip in enumerate(chips):
                _remote(mine, mine, ssem, rsem, 6 * k + j, (x, y, c)).wait_send()
                passed = rows(o, a, l, 2 * chip[0] + chip[1], c)
                _remote(passed, passed, ssem, rsem, 6 * k + 3 + j, (x, y, c)).wait_send()

    return _Job([], bufs, [], 6 * len(items), first, mid, last)


def _swap_job(p4s):
    def pairs(ins, fresh, c):
        return [(a, s, ins[a].at[s, :, _half_cols(p4s[a].shape[2], 1 - c)], fresh[a].at[s])
                for a in range(len(p4s)) for s in range(N_CHIPS)]

    def first(ins, inout, fresh, ssem, rsem):
        x, y, c, _ = _place()
        for a, s, src, dst in pairs(ins, fresh, c):
            _remote(src, dst, ssem, rsem, N_CHIPS * a + s, (x, y, 1 - c)).start()

    def last(ins, inout, fresh, ssem, rsem):
        x, y, c, _ = _place()
        for a, s, src, dst in pairs(ins, fresh, c):
            _remote(src, dst, ssem, rsem, N_CHIPS * a + s, (x, y, 1 - c)).wait()

    fresh = [jax.ShapeDtypeStruct(p.shape[:2] + (p.shape[2] // 2,), p.dtype) for p in p4s]
    return _Job(p4s, [], fresh, N_CHIPS * len(p4s), first, None, last)


def _scatter_job(parts):
    def first(ins, inout, fresh, ssem, rsem):
        x, y, c, chips = _place()
        for a in range(len(parts)):
            for j, chip in enumerate(chips):
                _remote(ins[a].at[2 * chip[0] + chip[1]], fresh[a].at[2 * x + y], ssem, rsem, 3 * a + j, (chip[0], chip[1], c)).start()

    def last(ins, inout, fresh, ssem, rsem):
        x, y, c, chips = _place()
        for a in range(len(parts)):
            for j, chip in enumerate(chips):
                slot = 2 * chip[0] + chip[1]
                _remote(ins[a].at[slot], fresh[a].at[slot], ssem, rsem, 3 * a + j, (x, y, c)).wait()

    return _Job(parts, [], [jax.ShapeDtypeStruct(p.shape, p.dtype) for p in parts], 3 * len(parts), first, None, last)


def _share_job(gs):
    def halves(o, a, c):
        width = gs[a].shape[1]
        return o[a].at[:, _half_cols(width, c)], o[a].at[:, _half_cols(width, 1 - c)]

    def first(ins, o, fresh, ssem, rsem):
        x, y, c, _ = _place()
        for a in range(len(gs)):
            mine, _ = halves(o, a, c)
            _remote(mine, mine, ssem, rsem, a, (x, y, 1 - c)).start()

    def last(ins, o, fresh, ssem, rsem):
        x, y, c, _ = _place()
        for a in range(len(gs)):
            mine, theirs = halves(o, a, c)
            _remote(mine, theirs, ssem, rsem, a, (x, y, 1 - c)).wait()

    return _Job([], gs, [], len(gs), first, None, last)


def _gather_all(buf, name):
    def body(b_ref, o_ref, ssem, rsem):
        x, y, c, _ = _place()
        me = 4 * x + 2 * y + c
        flips = [(fx, fy, fc) for fx in (0, 1) for fy in (0, 1) for fc in (0, 1)][1:]
        peers = [((1 - x) if fx else x, (1 - y) if fy else y, (1 - c) if fc else c) for fx, fy, fc in flips]
        sends = []
        for k, peer in enumerate(peers):
            cp = pltpu.make_async_remote_copy(src_ref=b_ref, dst_ref=o_ref.at[me], send_sem=ssem.at[k], recv_sem=rsem.at[k],
                                              device_id=peer, device_id_type=MESH)
            cp.start()
            sends.append(cp)
        for k, peer in enumerate(peers):
            slot = o_ref.at[4 * peer[0] + 2 * peer[1] + peer[2]]
            pltpu.make_async_remote_copy(src_ref=slot, dst_ref=slot, send_sem=ssem.at[k], recv_sem=rsem.at[k],
                                         device_id=(x, y, c), device_id_type=MESH).wait_recv()
        for cp in sends:
            cp.wait_send()

    return pl.pallas_call(
        body, in_specs=[_HBM], out_specs=_HBM,
        out_shape=jax.ShapeDtypeStruct((N_DEV,) + buf.shape, buf.dtype),
        scratch_shapes=[pltpu.SemaphoreType.DMA((N_DEV - 1,)), pltpu.SemaphoreType.DMA((N_DEV - 1,))],
        name=name,
    )(buf)


def _layer_forward(x, h, w_in_t, rest, sm, d, proj_job=None, attn_job=None):
    t = x.shape[0]
    heads = d // LANE
    bq, _ = _attn_tiles(t)
    moved = []
    if proj_job is None:
        proj = _matmul(h, w_in_t, "nt", F32, "proj_fwd", n=7 * d, tn_cap=1792)
    else:
        proj, moved = _matmul(h, w_in_t, "nt", F32, "proj_fwd_gather", n=7 * d, tn_cap=1792, job=proj_job)
    f_t = _matmul(w_in_t[7 * d:], h, "nt", F32, "forget_fwd", tn_cap=1024)
    c_t = _fox_prep(f_t, sm["b_f"], "fox_prep")
    c_col = c_t.reshape(heads, t, 1)
    c_row = c_t.reshape(heads, t // bq, 1, bq)
    if attn_job is None:
        o, lse = _attn_fwd(proj, c_col, c_row, d, "attn_fwd")
    else:
        (o, lse), moved = _attn_fwd(proj, c_col, c_row, d, "attn_fwd_gather", job=attn_job(moved))
    wts = rest(moved)
    merged = _mix_fwd(proj, o, sm["wm"], sm["bs"], sm["g_v"], d, "mix_fwd")
    z = _matmul(merged, wts["w_out"], "nn", F32, "out_fwd")
    x1, h2 = _norm_fwd(x, z, sm["g_post"], sm["g_fpre"], "norm_mid")
    a = _matmul(h2, wts["w_g_t"], "nt", F32, "gate_fwd", tn_cap=1408)
    b = _matmul(h2, wts["w_u_t"], "nt", F32, "up_fwd", tn_cap=1408)
    mm = _swiglu_fwd(a, b, "swiglu_fwd")
    z2 = _matmul(mm, wts["w_d"], "nn", F32, "down_fwd")
    return dict(x=x, h=h, proj=proj, f_t=f_t, c_col=c_col, c_row=c_row, o=o, lse=lse, merged=merged, z=z, x1=x1,
                h2=h2, a=a, b=b, mm=mm, z2=z2)


def _layer_backward(dz2, dx2, sv, wts, sm, d, carry=None):
    t = dx2.shape[0]
    heads = d // LANE
    ff = wts["w_d"].shape[0]
    in_w = 7 * d + heads
    g, pay = {}, {}

    def payload(key, a, b, rows, row0, name):
        pay[key] = _matmul(a, b, "tn", BF16, name, slab=((1, rows, d), 0, row0), into=pay.get(key), tm_cap=1408, tn_cap=1024,
                           tk_cap=1024)

    if carry is None:
        dm = _matmul(dz2, wts["w_d"], "nt", F32, "down_bwd_x", tn_cap=1408, tk_cap=1024)
    else:
        dm = carry.swap_in(lambda job: _matmul(dz2, wts["w_d"], "nt", F32, "down_bwd_x_swap", tn_cap=1408, tk_cap=1024, job=job))
    payload("w_d", sv["mm"], dz2, ff, 0, "down_bwd_w")
    da, db = _swiglu_bwd(sv["a"], sv["b"], dm, "swiglu_bwd")
    dh2 = _matmul_pieces([(da, wts["w_g_t"], 0), (db, wts["w_u_t"], 0)], None, "gu_bwd_x", tk=_tile(ff, 1408), tm_cap=1024)
    payload("w_g", da, sv["h2"], ff, 0, "gate_bwd_w")
    payload("w_u", db, sv["h2"], ff, 0, "up_bwd_w")
    dx1, dz, g["g_fpre"], g["g_post"] = _norm_bwd(dx2, (dh2, sv["x1"], sm["g_fpre"]), (sv["z"], sm["g_post"]), "norm_bwd_mid")
    dmerged = _matmul(dz, wts["w_out"], "nt", F32, "out_bwd_x", tk_cap=1024)
    payload("w_out", sv["merged"], dz, d, 0, "out_bwd_w")
    d_uv, d_g, do, g["w_s"], g["b_s"], g["g_v"] = _mix_bwd(dmerged, sv["proj"], sv["o"], sm["wm"], sm["wm_t"], sm["bs"],
                                                         sm["g_v"], d, "mix_bwd")
    attn_args = (sv["proj"], do, sv["o"], sv["lse"], sv["c_col"], sv["c_row"], d)
    if carry is None:
        dq, dk, dv, dc_q, dc_k = _attn_bwd(*attn_args, "attn_bwd")
    else:
        dq, dk, dv, dc_q, dc_k = carry.scatter_in(lambda job: _attn_bwd(*attn_args, "attn_bwd_scatter", job=job))
    df_t, g["b_f"] = _fox_bwd(dc_q.reshape(heads, t), dc_k.reshape(heads, t), sv["f_t"], sm["b_f"], "fox_bwd")
    df_b = df_t.astype(BF16)
    dh_f = _matmul(df_b, wts["w_in_t"][7 * d:], "tn", F32, "forget_bwd_x")
    pieces = [(d_uv, COL_U), (dq, COL_Q), (dk, COL_K), (dv, COL_VA), (d_g, COL_GA)]
    ops = [(p, wts["w_in_t"], col * d) for p, col in pieces]
    kw = dict(tk=_tile(d, 1024), tm_cap=1024, tn_cap=512)
    if carry is None:
        dh = _matmul_pieces(ops, dh_f, "proj_bwd_x", **kw)
    else:
        dh = carry.share_in(lambda job: _matmul_pieces(ops, dh_f, "proj_bwd_x_share", job=job, **kw))
    for p, col in pieces:
        payload("w_in", p, sv["h"], in_w, col * d, "proj_bwd_w")
    w_f_rows = _matmul(df_b, sv["h"], "nn", BF16, "forget_bwd_w", tk_cap=1024)
    pay["w_in"] = lax.dynamic_update_slice(pay["w_in"], w_f_rows[None], (0, 7 * d, 0))
    return dh, dx1, g, pay


class _GradExchange:
    def __init__(self, pay, c_idx, chip):
        self.keys = ["w_in", "w_out", "w_g", "w_u", "w_d"]
        self.p4 = []
        for k in self.keys:
            _, rows, dd = pay[k].shape
            self.p4.append(pay[k].reshape(N_CHIPS, rows // N_CHIPS, dd))
        self.c_idx = c_idx
        self.sel = jnp.stack([chip, chip, c_idx[0]]).astype(jnp.int32)

    def _after_swap(self, landed):
        self.parts = [_add_half(p, r, self.c_idx, "add_sibling") for p, r in zip(self.p4, landed)]

    def _after_scatter(self, landed):
        self.g = [_sum_slots(got, sent, self.sel, "sum_chips", out_cols=p.shape[2])
                  for got, sent, p in zip(landed, self.parts, self.p4)]

    def swap_in(self, host):
        out, landed = host(_swap_job(self.p4))
        self._after_swap(landed)
        return out

    def scatter_in(self, host):
        out, landed = host(_scatter_job(self.parts))
        self._after_scatter(landed)
        return out

    def share_in(self, host):
        out, self.g = host(_share_job(self.g))
        return out

    def run(self):
        self._after_swap(_run_job(_swap_job(self.p4), "swap_grads"))
        self._after_scatter(_run_job(_scatter_job(self.parts), "scatter_grads"))
        self.g = _run_job(_share_job(self.g), "share_grads")

    def grads(self):
        return self.g


def _small_pack(parts):
    flat = jnp.concatenate([p.reshape(-1) for p in parts])
    n = flat.shape[0]
    pad = (-n) % (LANE * LANE)
    return jnp.pad(flat, (0, pad)).reshape(-1, LANE)


def kernel(x, mix_pre_g, w_in, b_forget, sgu_norm_g, w_spatial, b_spatial, w_out, mix_post_g, ffn_pre_g, w_gate, w_up, w_down, ffn_post_g, loss_target, m_mix_pre_g, m_w_in, m_b_forget, m_sgu_norm_g, m_w_spatial, m_b_spatial, m_w_out, m_mix_post_g, m_ffn_pre_g, m_w_gate, m_w_up, m_w_down, m_ffn_post_g, v_mix_pre_g, v_w_in, v_b_forget, v_sgu_norm_g, v_w_spatial, v_b_spatial, v_w_out, v_mix_post_g, v_ffn_pre_g, v_w_gate, v_w_up, v_w_down, v_ffn_post_g):
    depth, d = mix_pre_g.shape
    assert depth == 2, "core c of a chip owns layer c"
    heads = d // LANE
    t = x.shape[1]
    ff = w_down.shape[1] * N_CHIPS
    in_w = w_in.shape[2] * N_CHIPS
    assert in_w == 7 * d + heads
    xs = x.reshape(t, d)
    target = loss_target.reshape(t, d)
    c_idx = lax.axis_index("c").astype(jnp.int32).reshape(1)
    chip = 2 * lax.axis_index("x") + lax.axis_index("y")
    dev = 2 * chip + lax.axis_index("c")

    def in_view(w):
        return jnp.transpose(w, (2, 0, 1))

    def gu_view(w):
        return jnp.transpose(w, (0, 2, 1))

    own = [jnp.transpose(in_view(w_in).astype(BF16), (1, 0, 2)), w_out.astype(BF16), gu_view(w_gate).astype(BF16),
           gu_view(w_up).astype(BF16), w_down.astype(BF16)]
    bufs = [lax.dynamic_update_slice(jnp.zeros((2, N_CHIPS) + o.shape[1:], BF16), o[:, None], (0, chip, 0, 0)) for o in own]
    bufs[0] = _run_job(_gather_job([bufs[0]], [(0, 0)]), "gather_first")[0]

    def weights(all_bufs, l):
        g_in, g_out, g_g, g_u, g_d = all_bufs
        return dict(w_in_t=g_in[l].reshape(in_w, d), w_out=g_out[l].reshape(d, d), w_g_t=g_g[l].reshape(ff, d),
                    w_u_t=g_u[l].reshape(ff, d), w_d=g_d[l].reshape(ff, d))

    tril = jnp.tril(jnp.ones((LANE, LANE), bool))
    smalls = []
    for l in range(depth):
        wm = jnp.where(tril[None], w_spatial[l], 0.0).astype(BF16)
        smalls.append(dict(
            b_f=b_forget[l].reshape(heads, 1), wm=wm, wm_t=jnp.swapaxes(wm, 1, 2), bs=b_spatial[l].reshape(heads, LANE, 1),
            g_v=sgu_norm_g[l].reshape(1, d), g_pre=mix_pre_g[l].reshape(1, d), g_post=mix_post_g[l].reshape(1, d),
            g_fpre=ffn_pre_g[l].reshape(1, d), g_fpost=ffn_post_g[l].reshape(1, d)))

    final = {}

    def rest_first(moved):
        final["bufs"] = list(moved)
        return weights(final["bufs"], 0)

    h = _norm_fwd(xs, None, None, smalls[0]["g_pre"], "norm_first")
    sv = _layer_forward(
        xs, h, bufs[0][0].reshape(in_w, d), rest_first, smalls[0], d,
        proj_job=_gather_job(bufs[1:], [(k, 0) for k in range(4)]),
        attn_job=lambda moved: _gather_job([bufs[0]] + list(moved), [(k, 1) for k in range(5)]))
    saved = [sv]
    wts = [weights(final["bufs"], l) for l in range(depth)]
    for l in range(1, depth):
        xin, h = _norm_fwd(saved[l - 1]["x1"], saved[l - 1]["z2"], smalls[l - 1]["g_fpost"], smalls[l]["g_pre"], "norm_out")
        saved.append(_layer_forward(xin, h, wts[l]["w_in_t"], lambda moved, l=l: wts[l], smalls[l], d))
    y, _ = _norm_fwd(saved[-1]["x1"], saved[-1]["z2"], smalls[-1]["g_fpost"], smalls[-1]["g_pre"], "norm_out")
    dy, loss_part = _loss_grad(y, target, "loss")
    loss = lax.psum(jnp.sum(loss_part), ("x", "y", "c"))

    grads = [None] * depth
    exchanges = [None] * depth
    carry = None
    dx2 = dy
    dz2, g_fpost = _norm_bwd(dx2, None, (saved[depth - 1]["z2"], smalls[depth - 1]["g_fpost"]), "norm_bwd_top")
    for l in reversed(range(depth)):
        dh, dx1, g, pay = _layer_backward(dz2, dx2, saved[l], wts[l], smalls[l], d, carry)
        exchanges[l] = carry = _GradExchange(pay, c_idx, chip)
        g["g_fpost"] = g_fpost
        if l > 0:
            dx2, dz2, g["g_pre"], g_fpost = _norm_bwd(dx1, (dh, saved[l]["x"], smalls[l]["g_pre"]),
                                                       (saved[l - 1]["z2"], smalls[l - 1]["g_fpost"]), "norm_bwd_between")
        else:
            grad_x, g["g_pre"] = _norm_bwd(dx1, (dh, saved[l]["x"], smalls[l]["g_pre"]), None, "norm_bwd_bottom")
        grads[l] = g
    exchanges[0].run()
    big = list(zip(*[ex.grads() for ex in exchanges]))

    small_names = ["g_pre", "b_f", "g_v", "w_s", "b_s", "g_post", "g_fpre", "g_fpost"]
    small_shapes = [mix_pre_g.shape, b_forget.shape, sgu_norm_g.shape, w_spatial.shape, b_spatial.shape, mix_post_g.shape,
                    ffn_pre_g.shape, ffn_post_g.shape]
    parts = [jnp.stack([grads[l][nme].reshape(-1) for l in range(depth)]) for nme in small_names]
    packed = _small_pack(parts)
    dev_sel = jnp.stack([dev, jnp.zeros_like(dev), jnp.zeros_like(dev)]).astype(jnp.int32)
    total = _sum_slots(_gather_all(packed, "gather_small"), packed[None], dev_sel, "sum_small").reshape(-1)
    small_grads, off = {}, 0
    for nme, shp in zip(small_names, small_shapes):
        n = math.prod(shp)
        small_grads[nme] = total[off:off + n].reshape(shp)
        off += n

    def adam_small(w, g, m, v):
        shp = w.shape
        if w.ndim >= 3 and shp[-1] >= LANE:
            two = (math.prod(shp[:-1]), shp[-1])
        else:
            two = (1, math.prod(shp)) if math.prod(shp) < LANE else (math.prod(shp) // LANE, LANE)
        outs = _adamw(w.reshape(two), g.reshape(two), m.reshape(two), v.reshape(two), "adamw")
        return [g] + [o.reshape(shp) for o in outs]

    def adam_in(w, m, v):
        outs = _adamw_interleaved(in_view(w), big[0][0], big[0][1], in_view(m), in_view(v), "adamw_in")
        return [jnp.transpose(o, (1, 2, 0)) for o in outs]

    def adam_gu(k, w, m, v):
        outs = _adamw_layers(gu_view(w), big[k][0], big[k][1], gu_view(m), gu_view(v), "adamw_layers")
        return [jnp.transpose(o, (0, 2, 1)) for o in outs]

    def adam_rows(k, w, m, v):
        return _adamw_layers(w, big[k][0], big[k][1], m, v, "adamw_layers")

    results = [
        adam_small(mix_pre_g, small_grads["g_pre"], m_mix_pre_g, v_mix_pre_g),
        adam_in(w_in, m_w_in, v_w_in),
        adam_small(b_forget, small_grads["b_f"], m_b_forget, v_b_forget),
        adam_small(sgu_norm_g, small_grads["g_v"], m_sgu_norm_g, v_sgu_norm_g),
        adam_small(w_spatial, small_grads["w_s"], m_w_spatial, v_w_spatial),
        adam_small(b_spatial, small_grads["b_s"], m_b_spatial, v_b_spatial),
        adam_rows(1, w_out, m_w_out, v_w_out),
        adam_small(mix_post_g, small_grads["g_post"], m_mix_post_g, v_mix_post_g),
        adam_small(ffn_pre_g, small_grads["g_fpre"], m_ffn_pre_g, v_ffn_pre_g),
        adam_gu(2, w_gate, m_w_gate, v_w_gate),
        adam_gu(3, w_up, m_w_up, v_w_up),
        adam_rows(4, w_down, m_w_down, v_w_down),
        adam_small(ffn_post_g, small_grads["g_fpost"], m_ffn_post_g, v_ffn_post_g),
    ]
    gs, deltas, new_ms, new_vs = zip(*results)
    return (loss, grad_x.reshape(x.shape), *gs, *deltas, *new_ms, *new_vs)
```

```python
import functools
import math

import jax
import jax.numpy as jnp
from jax import lax
from jax.experimental import pallas as pl
from jax.experimental.pallas import tpu as pltpu

F32 = jnp.float32
BF16 = jnp.bfloat16

EPS = 1e-6
LANE = 128
SUBLANE = 8
N_CHIPS = 4
N_DEV = 8
VMEM_LIMIT = 48 * 1024 * 1024
MESH = pl.DeviceIdType.MESH

ADAM_LR = 0.001
ADAM_B1 = 0.9
ADAM_B2 = 0.999
ADAM_EPS = 1e-08
ADAM_WD = 0.01
ADAM_STEP = 10
ADAM_C1 = 1.0 / (1.0 - ADAM_B1 ** ADAM_STEP)
ADAM_C2 = 1.0 / (1.0 - ADAM_B2 ** ADAM_STEP)

GELU_K = math.sqrt(2.0 / math.pi)
GELU_A = 0.044715
NEG = -1e30
LOG2E = 1.4426950408889634
LN2 = 0.6931471805599453

COL_U, COL_V, COL_Q, COL_K, COL_VA, COL_GA, COL_GB, COL_F = range(8)


def _cparams(sem=None):
    return pltpu.CompilerParams(dimension_semantics=sem, vmem_limit_bytes=VMEM_LIMIT)


def _tile(n, cap):
    best = None
    for t in range(LANE, min(n, cap) + 1, LANE):
        if n % t == 0:
            best = t
    return best if best is not None else n


def _rows(n, cap):
    best = None
    for t in range(SUBLANE, min(n, cap) + 1, SUBLANE):
        if n % t == 0:
            best = t
    return best if best is not None else n


def _gelu_and_grad(x):
    x2 = x * x
    t = jnp.tanh(GELU_K * (x + GELU_A * x2 * x))
    g = 0.5 * x * (1.0 + t)
    dg = 0.5 * (1.0 + t) + 0.5 * x * (1.0 - t * t) * (GELU_K * (1.0 + 3.0 * GELU_A * x2))
    return g, dg


def _sigmoid(x):
    return 1.0 / (1.0 + jnp.exp(-x))


def _sum8(v):
    n, d = v.shape
    return v.reshape(n // SUBLANE, SUBLANE, d).sum(axis=0)


def _nt_dot(a, b):
    return lax.dot_general(a, b, (((1,), (1,)), ((), ())), preferred_element_type=F32)


_HBM = pl.BlockSpec(memory_space=pl.ANY)


def _place():
    x, y, c = lax.axis_index("x"), lax.axis_index("y"), lax.axis_index("c")
    chips = [(1 - x, y), (x, 1 - y), (1 - x, 1 - y)]
    return x, y, c, chips


class _Job:
    def __init__(self, ins, inout, fresh, nsem, first, mid, last, mid_at=0.5):
        self.ins, self.inout, self.fresh, self.nsem = list(ins), list(inout), list(fresh), nsem
        self.first, self.mid, self.last, self.mid_at = first, mid, last, mid_at


def _call(body, *, grid, in_specs, out_specs, out_shape, scratch_shapes, dims, name, args, aliases=None, job=None):
    single = not isinstance(out_shape, (list, tuple))
    out_specs = [out_specs] if single else list(out_specs)
    out_shape = [out_shape] if single else list(out_shape)
    aliases = dict(aliases or {})
    if job is None:
        outs = pl.pallas_call(body, grid=grid, in_specs=in_specs, out_specs=out_specs, out_shape=out_shape,
                              scratch_shapes=scratch_shapes, input_output_aliases=aliases, compiler_params=_cparams(dims),
                              name=name)(*args)
        return (outs[0] if single else outs), []
    n_in, n_out, n_scr = len(args), len(out_shape), len(scratch_shapes)
    n_ji, n_jio, n_jf = len(job.ins), len(job.inout), len(job.fresh)
    total = math.prod(grid)

    def wrapped(*refs):
        host_in = refs[:n_in]
        pos = n_in
        j_ins = refs[pos:pos + n_ji]
        pos += n_ji + n_jio
        host_out = refs[pos:pos + n_out]
        pos += n_out
        j_inout = refs[pos:pos + n_jio]
        pos += n_jio
        j_fresh = refs[pos:pos + n_jf]
        pos += n_jf
        host_scr = refs[pos:pos + n_scr]
        ssem, rsem = refs[pos + n_scr:]
        flat = 0
        for ax, size in enumerate(grid):
            flat = flat * size + pl.program_id(ax)

        def hook(fn, at):
            if fn is not None:
                @pl.when(flat == at)
                def _():
                    fn(j_ins, j_inout, j_fresh, ssem, rsem)

        hook(job.first, 0)
        body(*host_in, *host_out, *host_scr)
        hook(job.mid, min(int(total * job.mid_at), total - 1))
        hook(job.last, total - 1)

    for k in range(n_jio):
        aliases[n_in + n_ji + k] = n_out + k
    outs = pl.pallas_call(
        wrapped, grid=grid,
        in_specs=list(in_specs) + [_HBM] * (n_ji + n_jio),
        out_specs=out_specs + [_HBM] * (n_jio + n_jf),
        out_shape=out_shape + [jax.ShapeDtypeStruct(b.shape, b.dtype) for b in job.inout] + list(job.fresh),
        scratch_shapes=list(scratch_shapes) + [pltpu.SemaphoreType.DMA((job.nsem,)), pltpu.SemaphoreType.DMA((job.nsem,))],
        input_output_aliases=aliases, compiler_params=_cparams(tuple("arbitrary" for _ in grid)), name=name,
    )(*args, *job.ins, *job.inout)
    host = outs[:n_out]
    return (host[0] if single else host), outs[n_out:]


def _run_job(job, name):
    n_ji, n_jio, n_jf = len(job.ins), len(job.inout), len(job.fresh)

    def body(*refs):
        j_ins = refs[:n_ji]
        pos = n_ji + n_jio
        j_inout = refs[pos:pos + n_jio]
        j_fresh = refs[pos + n_jio:pos + n_jio + n_jf]
        ssem, rsem = refs[pos + n_jio + n_jf:]
        for fn in (job.first, job.mid, job.last):
            if fn is not None:
                fn(j_ins, j_inout, j_fresh, ssem, rsem)

    return pl.pallas_call(
        body, in_specs=[_HBM] * (n_ji + n_jio), out_specs=[_HBM] * (n_jio + n_jf),
        out_shape=[jax.ShapeDtypeStruct(b.shape, b.dtype) for b in job.inout] + list(job.fresh),
        scratch_shapes=[pltpu.SemaphoreType.DMA((job.nsem,)), pltpu.SemaphoreType.DMA((job.nsem,))],
        input_output_aliases={n_ji + k: k for k in range(n_jio)}, name=name,
    )(*job.ins, *job.inout)


_DIMS ={"nn": ((1,), (0,)), "nt": ((1,), (1,)), "tn": ((0,), (0,))}


def _matmul(a, b, mode, out_dtype, name, n=None, slab=None, into=None, job=None, tm_cap=512, tn_cap=2048, tk_cap=1408):
    if mode == "nn":
        (m, k), (k2, nn_) = a.shape, b.shape
    elif mode == "nt":
        (m, k), (nn_, k2) = a.shape, b.shape
    else:
        (k, m), (k2, nn_) = a.shape, b.shape
    n = nn_ if n is None else n
    assert k == k2, (a.shape, b.shape, mode)
    tm, tn, tk = _tile(m, tm_cap), _tile(n, tn_cap), _tile(k, tk_cap)
    if slab is not None and slab[2]:
        tm = _tile(math.gcd(m, slab[2]), tm_cap)
    nk = k // tk
    if mode == "tn":
        a_spec = pl.BlockSpec((tk, tm), lambda j, i, kk, *_: (kk, i))
    else:
        a_spec = pl.BlockSpec((tm, tk), lambda j, i, kk, *_: (i, kk))
    if mode == "nt":
        b_spec = pl.BlockSpec((tn, tk), lambda j, i, kk, *_: (j, kk))
    else:
        b_spec = pl.BlockSpec((tk, tn), lambda j, i, kk, *_: (kk, j))
    dims = (_DIMS[mode], ((), ()))
    aliased = into is not None

    def body(*refs):
        a_ref, b_ref = refs[0], refs[1]
        o_ref = refs[3] if aliased else refs[2]
        p = lax.dot_general(a_ref[...], b_ref[...], dims, preferred_element_type=F32)
        if nk == 1:
            o_ref[...] = p.astype(out_dtype).reshape(o_ref.shape)
        else:
            acc = refs[-1]
            kk = pl.program_id(2)

            @pl.when(kk == 0)
            def _():
                acc[...] = p

            @pl.when(kk > 0)
            def _():
                acc[...] += p

            @pl.when(kk == nk - 1)
            def _():
                o_ref[...] = acc[...].astype(out_dtype).reshape(o_ref.shape)

    if slab is None:
        out_spec = pl.BlockSpec((tm, tn), lambda j, i, kk: (i, j))
        out_shape = jax.ShapeDtypeStruct((m, n), out_dtype)
    else:
        shape3, lead, row0 = slab
        assert row0 % tm == 0 and shape3[2] == n
        out_spec = pl.BlockSpec((1, tm, tn), lambda j, i, kk: (lead, row0 // tm + i, j))
        out_shape = jax.ShapeDtypeStruct(shape3, out_dtype)
    in_specs, args = [a_spec, b_spec], [a, b]
    if aliased:
        in_specs.append(pl.BlockSpec(memory_space=pl.ANY))
        args.append(into)
    out, moved = _call(
        body, grid=(n // tn, m // tm, nk), in_specs=in_specs, out_specs=out_spec, out_shape=out_shape,
        scratch_shapes=[pltpu.VMEM((tm, tn), F32)] if nk > 1 else [], dims=("parallel", "parallel", "arbitrary"), name=name,
        args=args, aliases={2: 0} if aliased else None, job=job)
    return out if job is None else (out, moved)


def _matmul_pieces(pieces, addend, name, tk, job=None, tm_cap=512, tn_cap=1024):
    m = pieces[0][0].shape[0]
    n = pieces[0][1].shape[1]
    tm, tn = _tile(m, tm_cap), _tile(n, tn_cap)
    spans, s0 = [], 0
    for a, b, row0 in pieces:
        assert a.shape[1] % tk == 0 and row0 % tk == 0 and b.shape[1] == n and a.shape[0] == m
        spans.append((s0, a.shape[1] // tk, row0 // tk))
        s0 += a.shape[1] // tk
    steps = s0
    np_ = len(pieces)
    groups = []
    for (a, b, _), (first, count, brow) in zip(pieces, spans):
        if groups and groups[-1][0] is b and groups[-1][3] + groups[-1][2] == brow:
            groups[-1][2] += count
        else:
            groups.append([b, first, count, brow])
    b_of = []
    for first, count, _ in spans:
        b_of.append(next(k for k, g in enumerate(groups) if g[1] <= first < g[1] + g[2]))
    ng = len(groups)

    def body(*refs):
        o_ref, acc = refs[-2], refs[-1]
        s = pl.program_id(2)

        @pl.when(s == 0)
        def _():
            acc[...] = refs[np_ + ng][...] if addend is not None else jnp.zeros((tm, tn), F32)

        for p, (first, count, _) in enumerate(spans):
            @pl.when((s >= first) & (s < first + count))
            def _(p=p):
                acc[...] += jnp.dot(refs[p][...], refs[np_ + b_of[p]][...], preferred_element_type=F32)

        @pl.when(s == steps - 1)
        def _():
            o_ref[...] = acc[...]

    in_specs, args = [], []
    for (a, _, _), (first, count, _) in zip(pieces, spans):
        in_specs.append(pl.BlockSpec((tm, tk), lambda j, i, s, f=first, c=count: (i, jnp.clip(s - f, 0, c - 1))))
        args.append(a)
    for b, first, count, brow in groups:
        in_specs.append(pl.BlockSpec((tk, tn), lambda j, i, s, f=first, c=count, r=brow: (r + jnp.clip(s - f, 0, c - 1), j)))
        args.append(b)
    if addend is not None:
        in_specs.append(pl.BlockSpec((tm, tn), lambda j, i, s: (i, j)))
        args.append(addend)
    out, moved = _call(
        body, grid=(n // tn, m // tm, steps), in_specs=in_specs, out_specs=pl.BlockSpec((tm, tn), lambda j, i, s: (i, j)),
        out_shape=jax.ShapeDtypeStruct((m, n), F32), scratch_shapes=[pltpu.VMEM((tm, tn), F32)],
        dims=("parallel", "parallel", "arbitrary"), name=name, args=args, job=job)
    return out if job is None else (out, moved)


def _norm_fwd(x, z, g_post, g_next, name):
    t, d = x.shape
    tt = _rows(t, 512)
    row = pl.BlockSpec((tt, d), lambda i: (i, 0))
    vec = pl.BlockSpec((1, d), lambda i: (0, 0))

    def body(*refs):
        if z is None:
            x_ref, gn_ref, h_ref = refs
            xn = x_ref[...]
        else:
            x_ref, z_ref, gp_ref, gn_ref, xo_ref, h_ref = refs
            zz = z_ref[...]
            r = lax.rsqrt(jnp.mean(zz * zz, axis=-1, keepdims=True) + EPS)
            xn = x_ref[...] + zz * r * gp_ref[...]
            xo_ref[...] = xn
        r2 = lax.rsqrt(jnp.mean(xn * xn, axis=-1, keepdims=True) + EPS)
        h_ref[...] = (xn * r2 * gn_ref[...]).astype(BF16)

    if z is None:
        return pl.pallas_call(
            body, grid=(t // tt,), in_specs=[row, vec], out_specs=row,
            out_shape=jax.ShapeDtypeStruct((t, d), BF16), compiler_params=_cparams(("parallel",)), name=name,
        )(x, g_next)
    return pl.pallas_call(
        body, grid=(t // tt,), in_specs=[row, row, vec, vec], out_specs=[row, row],
        out_shape=[jax.ShapeDtypeStruct((t, d), F32), jax.ShapeDtypeStruct((t, d), BF16)],
        compiler_params=_cparams(("parallel",)), name=name,
    )(x, z, g_post, g_next)


def _rms_bwd(dy, x, g):
    r = lax.rsqrt(jnp.mean(x * x, axis=-1, keepdims=True) + EPS)
    n = x * r
    dn = dy * g
    dx = r * (dn - n * jnp.mean(dn * n, axis=-1, keepdims=True))
    return dx, dy * n


def _norm_bwd(dres, pre, post, name):
    t, d = dres.shape
    tt = _rows(t, 512)
    nt = t // tt
    row = pl.BlockSpec((tt, d), lambda i: (i, 0))
    vec = pl.BlockSpec((1, d), lambda i: (0, 0))
    has_pre, has_post = pre is not None, post is not None
    n_in = 1 + (3 if has_pre else 0) + (2 if has_post else 0)
    n_out = has_pre + has_post + has_pre + has_post

    def body(*refs):
        ins, outs, scr = refs[:n_in], refs[n_in:n_in + n_out], refs[n_in + n_out:]
        i = pl.program_id(0)
        dx = ins[0][...]
        pos, opos, spos = 1, 0, 0
        accs = []
        if has_pre:
            dh_ref, xa_ref, ga_ref = ins[pos:pos + 3]
            pos += 3
            dxa, dga_t = _rms_bwd(dh_ref[...], xa_ref[...], ga_ref[...])
            dx = dx + dxa
            outs[opos][...] = dx
            opos += 1
            accs.append((scr[spos], dga_t))
            spos += 1
        if has_post:
            zb_ref, gb_ref = ins[pos:pos + 2]
            dz, dgb_t = _rms_bwd(dx, zb_ref[...], gb_ref[...])
            outs[opos][...] = dz.astype(BF16)
            opos += 1
            accs.append((scr[spos], dgb_t))
            spos += 1
        for (acc, val), out in zip(accs, outs[opos:]):
            part = _sum8(val)

            @pl.when(i == 0)
            def _(acc=acc, part=part):
                acc[...] = part

            @pl.when(i > 0)
            def _(acc=acc, part=part):
                acc[...] += part

            @pl.when(i == nt - 1)
            def _(acc=acc, out=out):
                out[...] = jnp.sum(acc[...], axis=0, keepdims=True)

    in_specs, args = [row], [dres]
    out_specs, out_shape = [], []
    if has_pre:
        in_specs += [row, row, vec]
        args += list(pre)
        out_specs.append(row)
        out_shape.append(jax.ShapeDtypeStruct((t, d), F32))
    if has_post:
        in_specs += [row, vec]
        args += list(post)
        out_specs.append(row)
        out_shape.append(jax.ShapeDtypeStruct((t, d), BF16))
    for _ in range(has_pre + has_post):
        out_specs.append(vec)
        out_shape.append(jax.ShapeDtypeStruct((1, d), F32))
    return pl.pallas_call(
        body, grid=(nt,), in_specs=in_specs, out_specs=out_specs, out_shape=out_shape,
        scratch_shapes=[pltpu.VMEM((SUBLANE, d), F32)] * (has_pre + has_post),
        compiler_params=_cparams(("arbitrary",)), name=name,
    )(*args)


def _loss_grad(y, target, name):
    t, d = y.shape
    tt = _rows(t, 512)
    nt = t // tt
    row = pl.BlockSpec((tt, d), lambda i: (i, 0))
    inv_d = 1.0 / d

    def body(y_ref, t_ref, dy_ref, l_ref):
        i = pl.program_id(0)
        diff = y_ref[...] - t_ref[...]
        dy_ref[...] = diff * inv_d
        s8 = _sum8(diff * diff)
        part = s8[:, 0:LANE]
        for k in range(1, d // LANE):
            part = part + s8[:, k * LANE:(k + 1) * LANE]
        part = part * (0.5 * inv_d)

        @pl.when(i == 0)
        def _():
            l_ref[...] = part

        @pl.when(i > 0)
        def _():
            l_ref[...] += part

    return pl.pallas_call(
        body, grid=(nt,), in_specs=[row, row],
        out_specs=[row, pl.BlockSpec((SUBLANE, LANE), lambda i: (0, 0))],
        out_shape=[jax.ShapeDtypeStruct((t, d), F32), jax.ShapeDtypeStruct((SUBLANE, LANE), F32)],
        compiler_params=_cparams(("arbitrary",)), name=name,
    )(y, target)


def _swiglu_fwd(a, b, name):
    t, f = a.shape
    tt = _rows(t, 256)
    blk = pl.BlockSpec((tt, f), lambda i: (i, 0))

    def body(a_ref, b_ref, m_ref):
        av = a_ref[...]
        m_ref[...] = (av * _sigmoid(av) * b_ref[...]).astype(BF16)

    return pl.pallas_call(
        body, grid=(t // tt,), in_specs=[blk, blk], out_specs=blk,
        out_shape=jax.ShapeDtypeStruct((t, f), BF16), compiler_params=_cparams(("parallel",)), name=name,
    )(a, b)


def _swiglu_bwd(a, b, dm, name):
    t, f = a.shape
    tt = _rows(t, 256)
    blk = pl.BlockSpec((tt, f), lambda i: (i, 0))

    def body(a_ref, b_ref, dm_ref, da_ref, db_ref):
        av = a_ref[...]
        s = _sigmoid(av)
        dv = dm_ref[...]
        da_ref[...] = (dv * b_ref[...] * s * (1.0 + av * (1.0 - s))).astype(BF16)
        db_ref[...] = (dv * av * s).astype(BF16)

    return pl.pallas_call(
        body, grid=(t // tt,), in_specs=[blk, blk, blk], out_specs=[blk, blk],
        out_shape=[jax.ShapeDtypeStruct((t, f), BF16)] * 2, compiler_params=_cparams(("parallel",)), name=name,
    )(a, b, dm)


def _log_sigmoid(x):
    return jnp.minimum(x, 0.0) - jnp.log1p(jnp.exp(-jnp.abs(x)))


def _fox_prep(f_t, b_f, name):
    h, t = f_t.shape

    def body(f_ref, b_ref, c_ref):
        r = lax.broadcasted_iota(jnp.int32, (LANE, LANE), 0)
        c = lax.broadcasted_iota(jnp.int32, (LANE, LANE), 1)
        upper = (r <= c).astype(F32)
        carry = jnp.zeros((h, 1), F32)
        for j in range(t // LANE):
            sl = slice(j * LANE, (j + 1) * LANE)
            lf = _log_sigmoid(f_ref[:, sl] + b_ref[...])
            cs = jnp.dot(lf, upper, precision=lax.Precision.HIGHEST, preferred_element_type=F32) + carry
            c_ref[:, sl] = cs
            carry = cs[:, LANE - 1:LANE]

    return pl.pallas_call(body, out_shape=jax.ShapeDtypeStruct((h, t), F32), compiler_params=_cparams(), name=name)(f_t, b_f)


def _fox_bwd(dc_q, dc_k, f_t, b_f, name):
    h, t = f_t.shape

    def body(dq_ref, dk_ref, f_ref, b_ref, df_ref, db_ref):
        r = lax.broadcasted_iota(jnp.int32, (LANE, LANE), 0)
        c = lax.broadcasted_iota(jnp.int32, (LANE, LANE), 1)
        lower = (r >= c).astype(F32)
        carry = jnp.zeros((h, 1), F32)
        dbsum = jnp.zeros((h, 1), F32)
        for j in reversed(range(t // LANE)):
            sl = slice(j * LANE, (j + 1) * LANE)
            dc = dq_ref[:, sl] - dk_ref[:, sl]
            dl = jnp.dot(dc, lower, precision=lax.Precision.HIGHEST, preferred_element_type=F32) + carry
            carry = dl[:, 0:1]
            df = dl * _sigmoid(-(f_ref[:, sl] + b_ref[...]))
            df_ref[:, sl] = df
            dbsum = dbsum + jnp.sum(df, axis=-1, keepdims=True)
        db_ref[...] = dbsum

    return pl.pallas_call(
        body, out_shape=[jax.ShapeDtypeStruct((h, t), F32), jax.ShapeDtypeStruct((h, 1), F32)],
        compiler_params=_cparams(), name=name,
    )(dc_q, dc_k, f_t, b_f)


ATTN_Q = 512
ATTN_K = 256


def _attn_tiles(t):
    return _tile(t, ATTN_Q), _tile(t, ATTN_K)


def _attn_fwd(proj, c_col, c_row, d, name, job=None):
    t = proj.shape[0]
    h = d // LANE
    bq, bk = _attn_tiles(t)
    nq, nk, rr = t // bq, t // bk, bq // bk
    qc, kc, vc = COL_Q * h, COL_K * h, COL_VA * h
    qscale = LANE ** -0.5 * LOG2E

    def body(q_ref, k_ref, v_ref, cc_ref, cr_ref, o_ref, lse_ref, kb, vt, ckb, acc):
        i = pl.program_id(1)

        @pl.when(i == 0)
        def _():
            kb[...] = k_ref[...].astype(BF16)
            ckb[...] = jnp.broadcast_to(cc_ref[0] * LOG2E, (t, bq))
            for jn in range(nk):
                vt[jn] = v_ref[jn * bk:(jn + 1) * bk, :].T.astype(BF16)

        q = (q_ref[...] * qscale).astype(BF16)
        cq = cr_ref[0, 0] * LOG2E
        acc[...] = jnp.zeros((LANE, bq), F32)

        def block(j, diag, m_old, l_old):
            rows = pl.ds(pl.multiple_of(j * bk, bk), bk)
            s = _nt_dot(kb[rows, :], q) - ckb[rows, :]
            if diag is not None:
                kk = lax.broadcasted_iota(jnp.int32, (bk, bq), 0)
                qq = lax.broadcasted_iota(jnp.int32, (bk, bq), 1)
                s = jnp.where(qq >= kk + diag * bk, s, NEG)
            m_new = jnp.maximum(m_old, jnp.max(s, axis=0, keepdims=True) + cq)
            p = jnp.exp2(s + (cq - m_new))
            alpha = jnp.exp2(m_old - m_new)
            l_new = alpha * l_old + jnp.sum(p, axis=0, keepdims=True)
            acc[...] = alpha * acc[...] + jnp.dot(vt[j], p.astype(BF16), preferred_element_type=F32)
            return m_new, l_new

        m, l = lax.fori_loop(0, i * rr, lambda j, c: block(j, None, *c),
                             (jnp.full((1, bq), NEG, F32), jnp.zeros((1, bq), F32)))
        for jj in range(rr):
            m, l = block(i * rr + jj, jj, m, l)
        o_ref[...] = (acc[...] / l).T
        lse_ref[0, 0] = m + jnp.log2(l)

    rowq = pl.BlockSpec((1, 1, 1, bq), lambda hh, i: (hh, i, 0, 0))
    outs, moved = _call(
        body, grid=(h, nq),
        in_specs=[
            pl.BlockSpec((bq, LANE), lambda hh, i: (i, qc + hh)),
            pl.BlockSpec((t, LANE), lambda hh, i: (0, kc + hh)),
            pl.BlockSpec((t, LANE), lambda hh, i: (0, vc + hh)),
            pl.BlockSpec((1, t, 1), lambda hh, i: (hh, 0, 0)),
            rowq,
        ],
        out_specs=[pl.BlockSpec((bq, LANE), lambda hh, i: (i, hh)), rowq],
        out_shape=[jax.ShapeDtypeStruct((t, d), F32), jax.ShapeDtypeStruct((h, nq, 1, bq), F32)],
        scratch_shapes=[pltpu.VMEM((t, LANE), BF16), pltpu.VMEM((nk, LANE, bk), BF16), pltpu.VMEM((t, bq), F32),
                        pltpu.VMEM((LANE, bq), F32)],
        dims=("arbitrary", "arbitrary"), name=name, args=[proj, proj, proj, c_col, c_row], job=job)
    return outs if job is None else (outs, moved)


def _attn_bwd(proj, do, o, lse, c_col, c_row, d, name, job=None):
    t = proj.shape[0]
    h = d // LANE
    bq, bk = _attn_tiles(t)
    nq, nk, rr = t // bq, t // bk, bq // bk
    qc, kc, vc = COL_Q * h, COL_K * h, COL_VA * h
    scale = LANE ** -0.5

    def body(q_ref, k_ref, v_ref, do_ref, o_ref, lse_ref, cc_ref, cr_ref, dq_ref, dk_ref, dv_ref, dcq_ref, dck_ref,
             kb, kt, vb, ckb, dk_acc, dv_acc, dck_acc, dqt_acc):
        i = pl.program_id(1)

        @pl.when(i == 0)
        def _():
            kb[...] = k_ref[...].astype(BF16)
            vb[...] = v_ref[...].astype(BF16)
            ckb[...] = jnp.broadcast_to(cc_ref[0] * LOG2E, (t, bq))
            for jn in range(nk):
                kt[jn] = k_ref[jn * bk:(jn + 1) * bk, :].T.astype(BF16)
            dk_acc[...] = jnp.zeros((t, LANE), F32)
            dv_acc[...] = jnp.zeros((t, LANE), F32)
            dck_acc[...] = jnp.zeros((t, LANE), F32)

        q = (q_ref[...] * (scale * LOG2E)).astype(BF16)
        dof = do_ref[...]
        dob = dof.astype(BF16)
        delta = jnp.sum((dof * o_ref[...]).T, axis=0, keepdims=True)
        rowb = cr_ref[0, 0] * LOG2E - lse_ref[0, 0]
        dqt_acc[...] = jnp.zeros((LANE, bq), F32)

        def block(j, diag, dcq):
            rows = pl.ds(pl.multiple_of(j * bk, bk), bk)
            p = jnp.exp2(_nt_dot(kb[rows, :], q) - ckb[rows, :] + rowb)
            if diag is not None:
                kk = lax.broadcasted_iota(jnp.int32, (bk, bq), 0)
                qq = lax.broadcasted_iota(jnp.int32, (bk, bq), 1)
                p = jnp.where(qq >= kk + diag * bk, p, 0.0)
            dv_acc[rows, :] += jnp.dot(p.astype(BF16), dob, preferred_element_type=F32)
            ds = p * (_nt_dot(vb[rows, :], dob) - delta)
            dsb = ds.astype(BF16)
            dk_acc[rows, :] += jnp.dot(dsb, q, preferred_element_type=F32)
            dqt_acc[...] += jnp.dot(kt[j], dsb, preferred_element_type=F32)
            part = ds[:, 0:LANE]
            for k in range(1, bq // LANE):
                part = part + ds[:, k * LANE:(k + 1) * LANE]
            dck_acc[rows, :] += part
            return dcq + jnp.sum(ds, axis=0, keepdims=True)

        dcq = lax.fori_loop(0, i * rr, lambda j, c: block(j, None, c), jnp.zeros((1, bq), F32))
        for jj in range(rr):
            dcq = block(i * rr + jj, jj, dcq)
        dq_ref[...] = (dqt_acc[...] * scale).T.astype(BF16)
        dcq_ref[0, 0] = dcq

        @pl.when(i == nq - 1)
        def _():
            dk_ref[...] = (dk_acc[...] * LN2).astype(BF16)
            dv_ref[...] = dv_acc[...].astype(BF16)
            dck_ref[0] = jnp.sum(dck_acc[...], axis=-1, keepdims=True)

    rowq = pl.BlockSpec((1, 1, 1, bq), lambda hh, i: (hh, i, 0, 0))
    blk = pl.BlockSpec((bq, LANE), lambda hh, i: (i, hh))
    whole = pl.BlockSpec((t, LANE), lambda hh, i: (0, hh))
    colk = pl.BlockSpec((1, t, 1), lambda hh, i: (hh, 0, 0))
    outs, moved = _call(
        body, grid=(h, nq),
        in_specs=[
            pl.BlockSpec((bq, LANE), lambda hh, i: (i, qc + hh)),
            pl.BlockSpec((t, LANE), lambda hh, i: (0, kc + hh)),
            pl.BlockSpec((t, LANE), lambda hh, i: (0, vc + hh)),
            blk, blk, rowq, colk, rowq,
        ],
        out_specs=[blk, whole, whole, rowq, colk],
        out_shape=[jax.ShapeDtypeStruct((t, d), BF16), jax.ShapeDtypeStruct((t, d), BF16), jax.ShapeDtypeStruct((t, d), BF16),
                   jax.ShapeDtypeStruct((h, nq, 1, bq), F32), jax.ShapeDtypeStruct((h, t, 1), F32)],
        scratch_shapes=[pltpu.VMEM((t, LANE), BF16), pltpu.VMEM((nk, LANE, bk), BF16), pltpu.VMEM((t, LANE), BF16),
                        pltpu.VMEM((t, bq), F32), pltpu.VMEM((t, LANE), F32), pltpu.VMEM((t, LANE), F32),
                        pltpu.VMEM((t, LANE), F32), pltpu.VMEM((LANE, bq), F32)],
        dims=("arbitrary", "arbitrary"), name=name, args=[proj, proj, proj, do, o, lse, c_col, c_row], job=job)
    return outs if job is None else (outs, moved)


def _sgu_forward(u_ref, v_ref, gv_ref, wm_ref, bs_ref, mix_sc, groups):
    gu, dgu = _gelu_and_grad(u_ref[...])
    gvv, dgv = _gelu_and_grad(v_ref[...])
    mu = jnp.mean(gvv, axis=-1, keepdims=True)
    xc = gvv - mu
    r = lax.rsqrt(jnp.mean(xc * xc, axis=-1, keepdims=True) + EPS)
    nhat = xc * r
    vn = (nhat * gv_ref[...]).astype(BF16)
    for g in range(groups):
        sl = slice(g * LANE, (g + 1) * LANE)
        mix_sc[:, sl] = jnp.dot(wm_ref[g], vn[:, sl], preferred_element_type=F32) + bs_ref[g]
    return gu, dgu, dgv, nhat, r, vn, mix_sc[...]


def _mix_fwd(proj, o, wm, bs, g_v, d, name):
    t = proj.shape[0]
    groups = d // LANE

    def body(u_ref, v_ref, ga_ref, gb_ref, o_ref, wm_ref, bs_ref, gv_ref, out_ref, mix_sc):
        gu, _, _, _, _, _, mixed = _sgu_forward(u_ref, v_ref, gv_ref, wm_ref, bs_ref, mix_sc, groups)
        out_ref[...] = (_sigmoid(ga_ref[...]) * (gu * mixed) + _sigmoid(gb_ref[...]) * o_ref[...]).astype(BF16)

    def colblk(k):
        return pl.BlockSpec((LANE, d), lambda i, k=k: (i, k))

    full3 = pl.BlockSpec((groups, LANE, LANE), lambda i: (0, 0, 0))
    return pl.pallas_call(
        body, grid=(t // LANE,),
        in_specs=[colblk(COL_U), colblk(COL_V), colblk(COL_GA), colblk(COL_GB), colblk(0), full3,
                  pl.BlockSpec((groups, LANE, 1), lambda i: (0, 0, 0)), pl.BlockSpec((1, d), lambda i: (0, 0))],
        out_specs=colblk(0),
        out_shape=jax.ShapeDtypeStruct((t, d), BF16),
        scratch_shapes=[pltpu.VMEM((LANE, d), F32)],
        compiler_params=_cparams(("parallel",)), name=name,
    )(proj, proj, proj, proj, o, wm, bs, g_v)


def _mix_bwd(dmerged, proj, o, wm, wm_t, bs, g_v, d, name):
    t = proj.shape[0]
    groups = d // LANE
    nt = t // LANE

    def body(dm_ref, u_ref, v_ref, ga_ref, gb_ref, o_ref, wm_ref, wmt_ref, bs_ref, gv_ref,
             duv_ref, dg_ref, do_ref, dws_ref, dbs_ref, dgv_ref, mix_sc, dvn_sc, gv_acc):
        i = pl.program_id(0)

        @pl.when(i == 0)
        def _():
            dws_ref[...] = jnp.zeros_like(dws_ref)
            dbs_ref[...] = jnp.zeros_like(dbs_ref)
            gv_acc[...] = jnp.zeros_like(gv_acc)

        gu, dgu, dgv, nhat, r, vn, mixed = _sgu_forward(u_ref, v_ref, gv_ref, wm_ref, bs_ref, mix_sc, groups)
        dm = dm_ref[...]
        sa = _sigmoid(ga_ref[...])
        sb = _sigmoid(gb_ref[...])
        ov = o_ref[...]
        y_a = gu * mixed
        dg_ref[:, 0:d] = (dm * y_a * sa * (1.0 - sa)).astype(BF16)
        dg_ref[:, d:2 * d] = (dm * ov * sb * (1.0 - sb)).astype(BF16)
        do_ref[...] = dm * sb
        dy_a = dm * sa
        duv_ref[:, 0:d] = (dy_a * mixed * dgu).astype(BF16)
        dmixed = dy_a * gu
        dmixed_b = dmixed.astype(BF16)
        for g in range(groups):
            sl = slice(g * LANE, (g + 1) * LANE)
            dvn_sc[:, sl] = jnp.dot(wmt_ref[g], dmixed_b[:, sl], preferred_element_type=F32)
            dws_ref[g] += _nt_dot(dmixed_b[:, sl], vn[:, sl])
            dbs_ref[g] += jnp.sum(dmixed[:, sl], axis=-1, keepdims=True)
        dvn = dvn_sc[...]
        gv_acc[...] += _sum8(dvn * nhat)
        dn = dvn * gv_ref[...]
        dgelu = r * (dn - jnp.mean(dn, axis=-1, keepdims=True) - nhat * jnp.mean(dn * nhat, axis=-1, keepdims=True))
        duv_ref[:, d:2 * d] = (dgelu * dgv).astype(BF16)

        @pl.when(i == nt - 1)
        def _():
            dgv_ref[...] = jnp.sum(gv_acc[...], axis=0, keepdims=True)
            rr = lax.broadcasted_iota(jnp.int32, (LANE, LANE), 0)
            cl = lax.broadcasted_iota(jnp.int32, (LANE, LANE), 1)
            for g in range(groups):
                dws_ref[g] = jnp.where(rr >= cl, dws_ref[g], 0.0)

    def colblk(k):
        return pl.BlockSpec((LANE, d), lambda i, k=k: (i, k))

    full3 = pl.BlockSpec((groups, LANE, LANE), lambda i: (0, 0, 0))
    col3 = pl.BlockSpec((groups, LANE, 1), lambda i: (0, 0, 0))
    vec = pl.BlockSpec((1, d), lambda i: (0, 0))
    two = pl.BlockSpec((LANE, 2 * d), lambda i: (i, 0))
    return pl.pallas_call(
        body, grid=(nt,),
        in_specs=[colblk(0), colblk(COL_U), colblk(COL_V), colblk(COL_GA), colblk(COL_GB), colblk(0), full3, full3, col3, vec],
        out_specs=[two, two, colblk(0), full3, col3, vec],
        out_shape=[jax.ShapeDtypeStruct((t, 2 * d), BF16), jax.ShapeDtypeStruct((t, 2 * d), BF16), jax.ShapeDtypeStruct((t, d), F32),
                   jax.ShapeDtypeStruct((groups, LANE, LANE), F32), jax.ShapeDtypeStruct((groups, LANE, 1), F32),
                   jax.ShapeDtypeStruct((1, d), F32)],
        scratch_shapes=[pltpu.VMEM((LANE, d), F32), pltpu.VMEM((LANE, d), F32), pltpu.VMEM((SUBLANE, d), F32)],
        compiler_params=_cparams(("arbitrary",)), name=name,
    )(dmerged, proj, proj, proj, proj, o, wm, wm_t, bs, g_v)


def _adam_math(w, g, m, v):
    nm = ADAM_B1 * m + (1.0 - ADAM_B1) * g
    nv = ADAM_B2 * v + (1.0 - ADAM_B2) * (g * g)
    delta = -ADAM_LR * ((nm * ADAM_C1) / (jnp.sqrt(nv * ADAM_C2) + ADAM_EPS) + ADAM_WD * w)
    return delta, nm, nv


def _adamw(w, g, m, v, name):
    r, c = w.shape
    cap = max(SUBLANE, (2 * 1024 * 1024) // (4 * c) // SUBLANE * SUBLANE)
    tr = _rows(r, cap)

    def body(w_ref, g_ref, m_ref, v_ref, d_ref, nm_ref, nv_ref):
        d_ref[...], nm_ref[...], nv_ref[...] = _adam_math(w_ref[...], g_ref[...], m_ref[...], v_ref[...])

    blk = pl.BlockSpec((tr, c), lambda i: (i, 0))
    return pl.pallas_call(
        body, grid=(r // tr,), in_specs=[blk] * 4, out_specs=[blk] * 3,
        out_shape=[jax.ShapeDtypeStruct((r, c), F32)] * 3, compiler_params=_cparams(("parallel",)), name=name,
    )(w, g, m, v)


def _adamw_layers(w, g0, g1, m, v, name):
    _, r, c = w.shape
    cap = max(SUBLANE, (1024 * 1024) // (4 * c) // SUBLANE * SUBLANE)
    tr = _rows(r, cap)

    def body(w_ref, g0_ref, g1_ref, m_ref, v_ref, g_ref, d_ref, nm_ref, nv_ref):
        gg = jnp.where(pl.program_id(0) == 0, g0_ref[...], g1_ref[...])
        g_ref[0] = gg
        d_ref[0], nm_ref[0], nv_ref[0] = _adam_math(w_ref[0], gg, m_ref[0], v_ref[0])

    lay = pl.BlockSpec((1, tr, c), lambda l, i: (l, i, 0))

    def gspec(l0):
        return pl.BlockSpec((tr, c), lambda l, i: (jnp.where(l == l0, i, 0), 0))

    return pl.pallas_call(
        body, grid=(2, r // tr), in_specs=[lay, gspec(0), gspec(1), lay, lay], out_specs=[lay] * 4,
        out_shape=[jax.ShapeDtypeStruct((2, r, c), F32)] * 4, compiler_params=_cparams(("arbitrary", "arbitrary")), name=name,
    )(w, g0, g1, m, v)


def _adamw_interleaved(w, g0, g1, m, v, name):
    r, _, c = w.shape
    tr = 128

    def body(w_ref, g0_ref, g1_ref, m_ref, v_ref, g_ref, d_ref, nm_ref, nv_ref):
        for l, gl in enumerate((g0_ref, g1_ref)):
            gg = gl[...]
            g_ref[:, l, :] = gg
            d_ref[:, l, :], nm_ref[:, l, :], nv_ref[:, l, :] = _adam_math(w_ref[:, l, :], gg, m_ref[:, l, :], v_ref[:, l, :])

    lay = pl.BlockSpec((tr, 2, c), lambda i: (i, 0, 0))
    flat = pl.BlockSpec((tr, c), lambda i: (i, 0))
    return pl.pallas_call(
        body, grid=(pl.cdiv(r, tr),), in_specs=[lay, flat, flat, lay, lay], out_specs=[lay] * 4,
        out_shape=[jax.ShapeDtypeStruct((r, 2, c), F32)] * 4, compiler_params=_cparams(("parallel",)), name=name,
    )(w, g0, g1, m, v)


def _add_half(p4, recv, c_idx, name):
    _, r, c = p4.shape
    hw = c // 2
    tr = 256 if r % 256 == 0 else r

    def body(c_ref, a_ref, b_ref, o_ref):
        o_ref[...] = (a_ref[...].astype(F32) + b_ref[...].astype(F32)).astype(BF16)

    return pl.pallas_call(
        body,
        grid_spec=pltpu.PrefetchScalarGridSpec(
            num_scalar_prefetch=1, grid=(N_CHIPS, pl.cdiv(r, tr)),
            in_specs=[pl.BlockSpec((1, tr, hw), lambda s, i, cr: (s, i, cr[0])), pl.BlockSpec((1, tr, hw), lambda s, i, cr: (s, i, 0))],
            out_specs=pl.BlockSpec((1, tr, hw), lambda s, i, cr: (s, i, 0)),
        ),
        out_shape=jax.ShapeDtypeStruct((N_CHIPS, r, hw), BF16), compiler_params=_cparams(("parallel", "parallel")), name=name,
    )(c_idx, p4, recv)


def _sum_slots(x, own, sel, name, out_cols=None):
    s, r, c = x.shape
    tr = 128 if r % 128 == 0 else r

    def body(sel_ref, x_ref, own_ref, o_ref):
        mine = own_ref[0].astype(F32)
        acc = jnp.zeros((tr, c), F32)
        for k in range(s):
            acc = acc + jnp.where(sel_ref[0] == k, mine, x_ref[k].astype(F32))
        o_ref[...] = acc

    return pl.pallas_call(
        body,
        grid_spec=pltpu.PrefetchScalarGridSpec(
            num_scalar_prefetch=1, grid=(pl.cdiv(r, tr),),
            in_specs=[pl.BlockSpec((s, tr, c), lambda i, sr: (0, i, 0)), pl.BlockSpec((1, tr, c), lambda i, sr: (sr[1], i, 0))],
            out_specs=pl.BlockSpec((tr, c), lambda i, sr: (i, sr[2])),
        ),
        out_shape=jax.ShapeDtypeStruct((r, out_cols or c), F32), compiler_params=_cparams(("parallel",)), name=name,
    )(sel, x, own)


def _half_cols(width, hc):
    hw = width // 2
    assert hw % LANE == 0
    return pl.ds(pl.multiple_of(hc * hw, LANE), hw)


def _remote(src, dst, ssem, rsem, k, to):
    return pltpu.make_async_remote_copy(src_ref=src, dst_ref=dst, send_sem=ssem.at[k], recv_sem=rsem.at[k], device_id=to,
                                        device_id_type=MESH)


def _gather_job(bufs, mid_at=0.5):
    def part(o, a, slot, hc):
        return o[a].at[slot, :, _half_cols(bufs[a].shape[2], hc)]

    def first(ins, o, fresh, ssem, rsem):
        x, y, c, chips = _place()
        for a in range(len(bufs)):
            mine = part(o, a, 2 * x + y, c)
            for j, chip in enumerate(chips):
                _remote(mine, mine, ssem, rsem, 6 * a + j, (chip[0], chip[1], c)).start()

    def mid(ins, o, fresh, ssem, rsem):
        x, y, c, chips = _place()
        for a in range(len(bufs)):
            for j, chip in enumerate(chips):
                got = part(o, a, 2 * chip[0] + chip[1], c)
                _remote(got, got, ssem, rsem, 6 * a + j, (x, y, c)).wait_recv()
                _remote(got, got, ssem, rsem, 6 * a + 3 + j, (x, y, 1 - c)).start()

    def last(ins, o, fresh, ssem, rsem):
        x, y, c, chips = _place()
        for a in range(len(bufs)):
            for j, chip in enumerate(chips):
                got = part(o, a, 2 * chip[0] + chip[1], 1 - c)
                _remote(got, got, ssem, rsem, 6 * a + 3 + j, (x, y, c)).wait_recv()
        for a in range(len(bufs)):
            mine = part(o, a, 2 * x + y, c)
            for j, chip in enumerate(chips):
                _remote(mine, mine, ssem, rsem, 6 * a + j, (x, y, c)).wait_send()
                passed = part(o, a, 2 * chip[0] + chip[1], c)
                _remote(passed, passed, ssem, rsem, 6 * a + 3 + j, (x, y, c)).wait_send()

    return _Job([], bufs, [], 6 * len(bufs), first, mid, last, mid_at)


def _swap_job(p4s):
    def pairs(ins, fresh, c):
        return [(a, s, ins[a].at[s, :, _half_cols(p4s[a].shape[2], 1 - c)], fresh[a].at[s])
                for a in range(len(p4s)) for s in range(N_CHIPS)]

    def first(ins, inout, fresh, ssem, rsem):
        x, y, c, _ = _place()
        for a, s, src, dst in pairs(ins, fresh, c):
            _remote(src, dst, ssem, rsem, N_CHIPS * a + s, (x, y, 1 - c)).start()

    def last(ins, inout, fresh, ssem, rsem):
        x, y, c, _ = _place()
        for a, s, src, dst in pairs(ins, fresh, c):
            _remote(src, dst, ssem, rsem, N_CHIPS * a + s, (x, y, 1 - c)).wait()

    fresh = [jax.ShapeDtypeStruct(p.shape[:2] + (p.shape[2] // 2,), p.dtype) for p in p4s]
    return _Job(p4s, [], fresh, N_CHIPS * len(p4s), first, None, last)


def _scatter_job(parts):
    def first(ins, inout, fresh, ssem, rsem):
        x, y, c, chips = _place()
        for a in range(len(parts)):
            for j, chip in enumerate(chips):
                _remote(ins[a].at[2 * chip[0] + chip[1]], fresh[a].at[2 * x + y], ssem, rsem, 3 * a + j, (chip[0], chip[1], c)).start()

    def last(ins, inout, fresh, ssem, rsem):
        x, y, c, chips = _place()
        for a in range(len(parts)):
            for j, chip in enumerate(chips):
                slot = 2 * chip[0] + chip[1]
                _remote(ins[a].at[slot], fresh[a].at[slot], ssem, rsem, 3 * a + j, (x, y, c)).wait()

    return _Job(parts, [], [jax.ShapeDtypeStruct(p.shape, p.dtype) for p in parts], 3 * len(parts), first, None, last)


def _share_job(gs):
    def halves(o, a, c):
        width = gs[a].shape[1]
        return o[a].at[:, _half_cols(width, c)], o[a].at[:, _half_cols(width, 1 - c)]

    def first(ins, o, fresh, ssem, rsem):
        x, y, c, _ = _place()
        for a in range(len(gs)):
            mine, _ = halves(o, a, c)
            _remote(mine, mine, ssem, rsem, a, (x, y, 1 - c)).start()

    def last(ins, o, fresh, ssem, rsem):
        x, y, c, _ = _place()
        for a in range(len(gs)):
            mine, theirs = halves(o, a, c)
            _remote(mine, theirs, ssem, rsem, a, (x, y, 1 - c)).wait()

    return _Job([], gs, [], len(gs), first, None, last)


def _gather_all(buf, name):
    def body(b_ref, o_ref, ssem, rsem):
        x, y, c, _ = _place()
        me = 4 * x + 2 * y + c
        flips = [(fx, fy, fc) for fx in (0, 1) for fy in (0, 1) for fc in (0, 1)][1:]
        peers = [((1 - x) if fx else x, (1 - y) if fy else y, (1 - c) if fc else c) for fx, fy, fc in flips]
        sends = []
        for k, peer in enumerate(peers):
            cp = pltpu.make_async_remote_copy(src_ref=b_ref, dst_ref=o_ref.at[me], send_sem=ssem.at[k], recv_sem=rsem.at[k],
                                              device_id=peer, device_id_type=MESH)
            cp.start()
            sends.append(cp)
        for k, peer in enumerate(peers):
            slot = o_ref.at[4 * peer[0] + 2 * peer[1] + peer[2]]
            pltpu.make_async_remote_copy(src_ref=slot, dst_ref=slot, send_sem=ssem.at[k], recv_sem=rsem.at[k],
                                         device_id=(x, y, c), device_id_type=MESH).wait_recv()
        for cp in sends:
            cp.wait_send()

    return pl.pallas_call(
        body, in_specs=[_HBM], out_specs=_HBM,
        out_shape=jax.ShapeDtypeStruct((N_DEV,) + buf.shape, buf.dtype),
        scratch_shapes=[pltpu.SemaphoreType.DMA((N_DEV - 1,)), pltpu.SemaphoreType.DMA((N_DEV - 1,))],
        name=name,
    )(buf)


def _layer_forward(x, h, w_in_t, rest, sm, d, proj_job=None, attn_job=None):
    t = x.shape[0]
    heads = d // LANE
    bq, _ = _attn_tiles(t)
    moved = []
    if proj_job is None:
        proj = _matmul(h, w_in_t, "nt", F32, "proj_fwd", n=7 * d, tn_cap=1792)
    else:
        proj, moved = _matmul(h, w_in_t, "nt", F32, "proj_fwd_gather", n=7 * d, tn_cap=1792, job=proj_job)
    f_t = _matmul(w_in_t[7 * d:], h, "nt", F32, "forget_fwd", tn_cap=1024)
    c_t = _fox_prep(f_t, sm["b_f"], "fox_prep")
    c_col = c_t.reshape(heads, t, 1)
    c_row = c_t.reshape(heads, t // bq, 1, bq)
    if attn_job is None:
        o, lse = _attn_fwd(proj, c_col, c_row, d, "attn_fwd")
    else:
        (o, lse), moved_attn = _attn_fwd(proj, c_col, c_row, d, "attn_fwd_gather", job=attn_job)
        moved = (moved, moved_attn)
    wts = rest(moved)
    merged = _mix_fwd(proj, o, sm["wm"], sm["bs"], sm["g_v"], d, "mix_fwd")
    z = _matmul(merged, wts["w_out"], "nn", F32, "out_fwd")
    x1, h2 = _norm_fwd(x, z, sm["g_post"], sm["g_fpre"], "norm_mid")
    a = _matmul(h2, wts["w_g_t"], "nt", F32, "gate_fwd", tn_cap=1408)
    b = _matmul(h2, wts["w_u_t"], "nt", F32, "up_fwd", tn_cap=1408)
    mm = _swiglu_fwd(a, b, "swiglu_fwd")
    z2 = _matmul(mm, wts["w_d"], "nn", F32, "down_fwd")
    return dict(x=x, h=h, proj=proj, f_t=f_t, c_col=c_col, c_row=c_row, o=o, lse=lse, merged=merged, z=z, x1=x1,
                h2=h2, a=a, b=b, mm=mm, z2=z2)


def _layer_backward(dz2, dx2, sv, wts, sm, d, carry=None):
    t = dx2.shape[0]
    heads = d // LANE
    ff = wts["w_d"].shape[0]
    in_w = 7 * d + heads
    g, pay = {}, {}

    def payload(key, a, b, rows, row0, name):
        pay[key] = _matmul(a, b, "tn", BF16, name, slab=((1, rows, d), 0, row0), into=pay.get(key), tm_cap=1408, tn_cap=1024,
                           tk_cap=1024)

    if carry is None:
        dm = _matmul(dz2, wts["w_d"], "nt", F32, "down_bwd_x", tn_cap=1408, tk_cap=1024)
    else:
        dm = carry.swap_in(lambda job: _matmul(dz2, wts["w_d"], "nt", F32, "down_bwd_x_swap", tn_cap=1408, tk_cap=1024, job=job))
    payload("w_d", sv["mm"], dz2, ff, 0, "down_bwd_w")
    da, db = _swiglu_bwd(sv["a"], sv["b"], dm, "swiglu_bwd")
    dh2 = _matmul_pieces([(da, wts["w_g_t"], 0), (db, wts["w_u_t"], 0)], None, "gu_bwd_x", tk=_tile(ff, 1408), tm_cap=1024)
    payload("w_g", da, sv["h2"], ff, 0, "gate_bwd_w")
    payload("w_u", db, sv["h2"], ff, 0, "up_bwd_w")
    dx1, dz, g["g_fpre"], g["g_post"] = _norm_bwd(dx2, (dh2, sv["x1"], sm["g_fpre"]), (sv["z"], sm["g_post"]), "norm_bwd_mid")
    dmerged = _matmul(dz, wts["w_out"], "nt", F32, "out_bwd_x", tk_cap=1024)
    payload("w_out", sv["merged"], dz, d, 0, "out_bwd_w")
    d_uv, d_g, do, g["w_s"], g["b_s"], g["g_v"] = _mix_bwd(dmerged, sv["proj"], sv["o"], sm["wm"], sm["wm_t"], sm["bs"],
                                                         sm["g_v"], d, "mix_bwd")
    attn_args = (sv["proj"], do, sv["o"], sv["lse"], sv["c_col"], sv["c_row"], d)
    if carry is None:
        dq, dk, dv, dc_q, dc_k = _attn_bwd(*attn_args, "attn_bwd")
    else:
        dq, dk, dv, dc_q, dc_k = carry.scatter_in(lambda job: _attn_bwd(*attn_args, "attn_bwd_scatter", job=job))
    df_t, g["b_f"] = _fox_bwd(dc_q.reshape(heads, t), dc_k.reshape(heads, t), sv["f_t"], sm["b_f"], "fox_bwd")
    df_b = df_t.astype(BF16)
    dh_f = _matmul(df_b, wts["w_in_t"][7 * d:], "tn", F32, "forget_bwd_x")
    pieces = [(d_uv, COL_U), (dq, COL_Q), (dk, COL_K), (dv, COL_VA), (d_g, COL_GA)]
    ops = [(p, wts["w_in_t"], col * d) for p, col in pieces]
    kw = dict(tk=_tile(d, 1024), tm_cap=1024, tn_cap=512)
    if carry is None:
        dh = _matmul_pieces(ops, dh_f, "proj_bwd_x", **kw)
    else:
        dh = carry.share_in(lambda job: _matmul_pieces(ops, dh_f, "proj_bwd_x_share", job=job, **kw))
    for p, col in pieces:
        payload("w_in", p, sv["h"], in_w, col * d, "proj_bwd_w")
    w_f_rows = _matmul(df_b, sv["h"], "nn", BF16, "forget_bwd_w", tk_cap=1024)
    pay["w_in"] = lax.dynamic_update_slice(pay["w_in"], w_f_rows[None], (0, 7 * d, 0))
    return dh, dx1, g, pay


class _GradExchange:
    def __init__(self, pay, c_idx, chip):
        self.keys = ["w_in", "w_out", "w_g", "w_u", "w_d"]
        self.p4 = []
        for k in self.keys:
            _, rows, dd = pay[k].shape
            self.p4.append(pay[k].reshape(N_CHIPS, rows // N_CHIPS, dd))
        self.c_idx = c_idx
        self.sel = jnp.stack([chip, chip, c_idx[0]]).astype(jnp.int32)

    def _after_swap(self, landed):
        self.parts = [_add_half(p, r, self.c_idx, "add_sibling") for p, r in zip(self.p4, landed)]

    def _after_scatter(self, landed):
        self.g = [_sum_slots(got, sent, self.sel, "sum_chips", out_cols=p.shape[2])
                  for got, sent, p in zip(landed, self.parts, self.p4)]

    def swap_in(self, host):
        out, landed = host(_swap_job(self.p4))
        self._after_swap(landed)
        return out

    def scatter_in(self, host):
        out, landed = host(_scatter_job(self.parts))
        self._after_scatter(landed)
        return out

    def share_in(self, host):
        out, self.g = host(_share_job(self.g))
        return out

    def run(self):
        self._after_swap(_run_job(_swap_job(self.p4), "swap_grads"))
        self._after_scatter(_run_job(_scatter_job(self.parts), "scatter_grads"))
        self.g = _run_job(_share_job(self.g), "share_grads")

    def grads(self):
        return self.g


def _small_pack(parts):
    flat = jnp.concatenate([p.reshape(-1) for p in parts])
    n = flat.shape[0]
    pad = (-n) % (LANE * LANE)
    return jnp.pad(flat, (0, pad)).reshape(-1, LANE)


def kernel(x, mix_pre_g, w_in, b_forget, sgu_norm_g, w_spatial, b_spatial, w_out, mix_post_g, ffn_pre_g, w_gate, w_up, w_down, ffn_post_g, loss_target, m_mix_pre_g, m_w_in, m_b_forget, m_sgu_norm_g, m_w_spatial, m_b_spatial, m_w_out, m_mix_post_g, m_ffn_pre_g, m_w_gate, m_w_up, m_w_down, m_ffn_post_g, v_mix_pre_g, v_w_in, v_b_forget, v_sgu_norm_g, v_w_spatial, v_b_spatial, v_w_out, v_mix_post_g, v_ffn_pre_g, v_w_gate, v_w_up, v_w_down, v_ffn_post_g):
    depth, d = mix_pre_g.shape
    assert depth == 2, "core c of a chip owns layer c"
    heads = d // LANE
    t = x.shape[1]
    ff = w_down.shape[1] * N_CHIPS
    in_w = w_in.shape[2] * N_CHIPS
    assert in_w == 7 * d + heads
    xs = x.reshape(t, d)
    target = loss_target.reshape(t, d)
    c_idx = lax.axis_index("c").astype(jnp.int32).reshape(1)
    chip = 2 * lax.axis_index("x") + lax.axis_index("y")
    dev = 2 * chip + lax.axis_index("c")

    def in_view(w):
        return jnp.transpose(w, (2, 0, 1))

    def gu_view(w):
        return jnp.transpose(w, (0, 2, 1))

    own = [jnp.transpose(in_view(w_in).astype(BF16), (1, 0, 2)), w_out.astype(BF16), gu_view(w_gate).astype(BF16),
           gu_view(w_up).astype(BF16), w_down.astype(BF16)]
    bufs = [[lax.dynamic_update_slice(jnp.zeros((N_CHIPS,) + o.shape[1:], BF16), o[l][None], (chip, 0, 0)) for o in own]
            for l in range(depth)]
    first_in = _run_job(_gather_job([bufs[0][0]]), "gather_first")[0]

    def weights(g_in, g_out, g_g, g_u, g_d):
        return dict(w_in_t=g_in.reshape(in_w, d), w_out=g_out.reshape(d, d), w_g_t=g_g.reshape(ff, d),
                    w_u_t=g_u.reshape(ff, d), w_d=g_d.reshape(ff, d))

    tril = jnp.tril(jnp.ones((LANE, LANE), bool))
    smalls = []
    for l in range(depth):
        wm = jnp.where(tril[None], w_spatial[l], 0.0).astype(BF16)
        smalls.append(dict(
            b_f=b_forget[l].reshape(heads, 1), wm=wm, wm_t=jnp.swapaxes(wm, 1, 2), bs=b_spatial[l].reshape(heads, LANE, 1),
            g_v=sgu_norm_g[l].reshape(1, d), g_pre=mix_pre_g[l].reshape(1, d), g_post=mix_post_g[l].reshape(1, d),
            g_fpre=ffn_pre_g[l].reshape(1, d), g_fpost=ffn_post_g[l].reshape(1, d)))

    wts = []

    def rest_first(moved):
        wts.append(weights(first_in, *moved[0]))
        wts.append(weights(*moved[1]))
        return wts[0]

    h = _norm_fwd(xs, None, None, smalls[0]["g_pre"], "norm_first")
    sv = _layer_forward(xs, h, first_in.reshape(in_w, d), rest_first, smalls[0], d,
                        proj_job=_gather_job(bufs[0][1:], mid_at=1.0), attn_job=_gather_job(bufs[1]))
    saved = [sv]
    for l in range(1, depth):
        xin, h = _norm_fwd(saved[l - 1]["x1"], saved[l - 1]["z2"], smalls[l - 1]["g_fpost"], smalls[l]["g_pre"], "norm_out")
        saved.append(_layer_forward(xin, h, wts[l]["w_in_t"], lambda moved, l=l: wts[l], smalls[l], d))
    y, _ = _norm_fwd(saved[-1]["x1"], saved[-1]["z2"], smalls[-1]["g_fpost"], smalls[-1]["g_pre"], "norm_out")
    dy, loss_part = _loss_grad(y, target, "loss")
    loss = lax.psum(jnp.sum(loss_part), ("x", "y", "c"))

    grads = [None] * depth
    exchanges = [None] * depth
    carry = None
    dx2 = dy
    dz2, g_fpost = _norm_bwd(dx2, None, (saved[depth - 1]["z2"], smalls[depth - 1]["g_fpost"]), "norm_bwd_top")
    for l in reversed(range(depth)):
        dh, dx1, g, pay = _layer_backward(dz2, dx2, saved[l], wts[l], smalls[l], d, carry)
        exchanges[l] = carry = _GradExchange(pay, c_idx, chip)
        g["g_fpost"] = g_fpost
        if l > 0:
            dx2, dz2, g["g_pre"], g_fpost = _norm_bwd(dx1, (dh, saved[l]["x"], smalls[l]["g_pre"]),
                                                       (saved[l - 1]["z2"], smalls[l - 1]["g_fpost"]), "norm_bwd_between")
        else:
            grad_x, g["g_pre"] = _norm_bwd(dx1, (dh, saved[l]["x"], smalls[l]["g_pre"]), None, "norm_bwd_bottom")
        grads[l] = g
    exchanges[0].run()
    big = list(zip(*[ex.grads() for ex in exchanges]))

    small_names = ["g_pre", "b_f", "g_v", "w_s", "b_s", "g_post", "g_fpre", "g_fpost"]
    small_shapes = [mix_pre_g.shape, b_forget.shape, sgu_norm_g.shape, w_spatial.shape, b_spatial.shape, mix_post_g.shape,
                    ffn_pre_g.shape, ffn_post_g.shape]
    parts = [jnp.stack([grads[l][nme].reshape(-1) for l in range(depth)]) for nme in small_names]
    packed = _small_pack(parts)
    dev_sel = jnp.stack([dev, jnp.zeros_like(dev), jnp.zeros_like(dev)]).astype(jnp.int32)
    total = _sum_slots(_gather_all(packed, "gather_small"), packed[None], dev_sel, "sum_small").reshape(-1)
    small_grads, off = {}, 0
    for nme, shp in zip(small_names, small_shapes):
        n = math.prod(shp)
        small_grads[nme] = total[off:off + n].reshape(shp)
        off += n

    def adam_small(w, g, m, v):
        shp = w.shape
        if w.ndim >= 3 and shp[-1] >= LANE:
            two = (math.prod(shp[:-1]), shp[-1])
        else:
            two = (1, math.prod(shp)) if math.prod(shp) < LANE else (math.prod(shp) // LANE, LANE)
        outs = _adamw(w.reshape(two), g.reshape(two), m.reshape(two), v.reshape(two), "adamw")
        return [g] + [o.reshape(shp) for o in outs]

    def adam_in(w, m, v):
        outs = _adamw_interleaved(in_view(w), big[0][0], big[0][1], in_view(m), in_view(v), "adamw_in")
        return [jnp.transpose(o, (1, 2, 0)) for o in outs]

    def adam_gu(k, w, m, v):
        outs = _adamw_layers(gu_view(w), big[k][0], big[k][1], gu_view(m), gu_view(v), "adamw_layers")
        return [jnp.transpose(o, (0, 2, 1)) for o in outs]

    def adam_rows(k, w, m, v):
        return _adamw_layers(w, big[k][0], big[k][1], m, v, "adamw_layers")

    results = [
        adam_small(mix_pre_g, small_grads["g_pre"], m_mix_pre_g, v_mix_pre_g),
        adam_in(w_in, m_w_in, v_w_in),
        adam_small(b_forget, small_grads["b_f"], m_b_forget, v_b_forget),
        adam_small(sgu_norm_g, small_grads["g_v"], m_sgu_norm_g, v_sgu_norm_g),
        adam_small(w_spatial, small_grads["w_s"], m_w_spatial, v_w_spatial),
        adam_small(b_spatial, small_grads["b_s"], m_b_spatial, v_b_spatial),
        adam_rows(1, w_out, m_w_out, v_w_out),
        adam_small(mix_post_g, small_grads["g_post"], m_mix_post_g, v_mix_post_g),
        adam_small(ffn_pre_g, small_grads["g_fpre"], m_ffn_pre_g, v_ffn_pre_g),
        adam_gu(2, w_gate, m_w_gate, v_w_gate),
        adam_gu(3, w_up, m_w_up, v_w_up),
        adam_rows(4, w_down, m_w_down, v_w_down),
        adam_small(ffn_post_g, small_grads["g_fpost"], m_ffn_post_g, v_ffn_post_g),
    ]
    gs, deltas, new_ms, new_vs = zip(*results)
    return (loss, grad_x.reshape(x.shape), *gs, *deltas, *new_ms, *new_vs)
```

```python
import functools
import math

import jax
import jax.numpy as jnp
from jax import lax
from jax.experimental import pallas as pl
from jax.experimental.pallas import tpu as pltpu

F32 = jnp.float32
BF16 = jnp.bfloat16

EPS = 1e-6
LANE = 128
SUBLANE = 8
N_CHIPS = 4
N_DEV = 8
VMEM_LIMIT = 48 * 1024 * 1024
MESH = pl.DeviceIdType.MESH

ADAM_LR = 0.001
ADAM_B1 = 0.9
ADAM_B2 = 0.999
ADAM_EPS = 1e-08
ADAM_WD = 0.01
ADAM_STEP = 10
ADAM_C1 = 1.0 / (1.0 - ADAM_B1 ** ADAM_STEP)
ADAM_C2 = 1.0 / (1.0 - ADAM_B2 ** ADAM_STEP)

GELU_K = math.sqrt(2.0 / math.pi)
GELU_A = 0.044715
NEG = -1e30
LOG2E = 1.4426950408889634
LN2 = 0.6931471805599453

COL_U, COL_V, COL_Q, COL_K, COL_VA, COL_GA, COL_GB, COL_F = range(8)


def _cparams(sem=None):
    return pltpu.CompilerParams(dimension_semantics=sem, vmem_limit_bytes=VMEM_LIMIT)


def _tile(n, cap):
    best = None
    for t in range(LANE, min(n, cap) + 1, LANE):
        if n % t == 0:
            best = t
    return best if best is not None else n


def _rows(n, cap):
    best = None
    for t in range(SUBLANE, min(n, cap) + 1, SUBLANE):
        if n % t == 0:
            best = t
    return best if best is not None else n


def _gelu_and_grad(x):
    x2 = x * x
    t = jnp.tanh(GELU_K * (x + GELU_A * x2 * x))
    g = 0.5 * x * (1.0 + t)
    dg = 0.5 * (1.0 + t) + 0.5 * x * (1.0 - t * t) * (GELU_K * (1.0 + 3.0 * GELU_A * x2))
    return g, dg


def _sigmoid(x):
    return 1.0 / (1.0 + jnp.exp(-x))


def _sum8(v):
    n, d = v.shape
    return v.reshape(n // SUBLANE, SUBLANE, d).sum(axis=0)


def _nt_dot(a, b):
    return lax.dot_general(a, b, (((1,), (1,)), ((), ())), preferred_element_type=F32)


_HBM = pl.BlockSpec(memory_space=pl.ANY)


def _place():
    x, y, c = lax.axis_index("x"), lax.axis_index("y"), lax.axis_index("c")
    chips = [(1 - x, y), (x, 1 - y), (1 - x, 1 - y)]
    return x, y, c, chips


class _Job:
    def __init__(self, ins, inout, fresh, nsem, first, mid, last, mid_at=0.5):
        self.ins, self.inout, self.fresh, self.nsem = list(ins), list(inout), list(fresh), nsem
        self.first, self.mid, self.last, self.mid_at = first, mid, last, mid_at


def _call(body, *, grid, in_specs, out_specs, out_shape, scratch_shapes, dims, name, args, aliases=None, job=None):
    single = not isinstance(out_shape, (list, tuple))
    out_specs = [out_specs] if single else list(out_specs)
    out_shape = [out_shape] if single else list(out_shape)
    aliases = dict(aliases or {})
    if job is None:
        outs = pl.pallas_call(body, grid=grid, in_specs=in_specs, out_specs=out_specs, out_shape=out_shape,
                              scratch_shapes=scratch_shapes, input_output_aliases=aliases, compiler_params=_cparams(dims),
                              name=name)(*args)
        return (outs[0] if single else outs), []
    n_in, n_out, n_scr = len(args), len(out_shape), len(scratch_shapes)
    n_ji, n_jio, n_jf = len(job.ins), len(job.inout), len(job.fresh)
    total = math.prod(grid)

    def wrapped(*refs):
        host_in = refs[:n_in]
        pos = n_in
        j_ins = refs[pos:pos + n_ji]
        pos += n_ji + n_jio
        host_out = refs[pos:pos + n_out]
        pos += n_out
        j_inout = refs[pos:pos + n_jio]
        pos += n_jio
        j_fresh = refs[pos:pos + n_jf]
        pos += n_jf
        host_scr = refs[pos:pos + n_scr]
        ssem, rsem = refs[pos + n_scr:]
        flat = 0
        for ax, size in enumerate(grid):
            flat = flat * size + pl.program_id(ax)

        def hook(fn, at):
            if fn is not None:
                @pl.when(flat == at)
                def _():
                    fn(j_ins, j_inout, j_fresh, ssem, rsem)

        hook(job.first, 0)
        body(*host_in, *host_out, *host_scr)
        hook(job.mid, min(int(total * job.mid_at), total - 1))
        hook(job.last, total - 1)

    for k in range(n_jio):
        aliases[n_in + n_ji + k] = n_out + k
    outs = pl.pallas_call(
        wrapped, grid=grid,
        in_specs=list(in_specs) + [_HBM] * (n_ji + n_jio),
        out_specs=out_specs + [_HBM] * (n_jio + n_jf),
        out_shape=out_shape + [jax.ShapeDtypeStruct(b.shape, b.dtype) for b in job.inout] + list(job.fresh),
        scratch_shapes=list(scratch_shapes) + [pltpu.SemaphoreType.DMA((job.nsem,)), pltpu.SemaphoreType.DMA((job.nsem,))],
        input_output_aliases=aliases, compiler_params=_cparams(tuple("arbitrary" for _ in grid)), name=name,
    )(*args, *job.ins, *job.inout)
    host = outs[:n_out]
    return (host[0] if single else host), outs[n_out:]


def _run_job(job, name):
    n_ji, n_jio, n_jf = len(job.ins), len(job.inout), len(job.fresh)

    def body(*refs):
        j_ins = refs[:n_ji]
        pos = n_ji + n_jio
        j_inout = refs[pos:pos + n_jio]
        j_fresh = refs[pos + n_jio:pos + n_jio + n_jf]
        ssem, rsem = refs[pos + n_jio + n_jf:]
        for fn in (job.first, job.mid, job.last):
            if fn is not None:
                fn(j_ins, j_inout, j_fresh, ssem, rsem)

    return pl.pallas_call(
        body, in_specs=[_HBM] * (n_ji + n_jio), out_specs=[_HBM] * (n_jio + n_jf),
        out_shape=[jax.ShapeDtypeStruct(b.shape, b.dtype) for b in job.inout] + list(job.fresh),
        scratch_shapes=[pltpu.SemaphoreType.DMA((job.nsem,)), pltpu.SemaphoreType.DMA((job.nsem,))],
        input_output_aliases={n_ji + k: k for k in range(n_jio)}, name=name,
    )(*job.ins, *job.inout)


_DIMS ={"nn": ((1,), (0,)), "nt": ((1,), (1,)), "tn": ((0,), (0,))}


def _matmul(a, b, mode, out_dtype, name, n=None, slab=None, into=None, job=None, tm_cap=512, tn_cap=2048, tk_cap=1408):
    if mode == "nn":
        (m, k), (k2, nn_) = a.shape, b.shape
    elif mode == "nt":
        (m, k), (nn_, k2) = a.shape, b.shape
    else:
        (k, m), (k2, nn_) = a.shape, b.shape
    n = nn_ if n is None else n
    assert k == k2, (a.shape, b.shape, mode)
    tm, tn, tk = _tile(m, tm_cap), _tile(n, tn_cap), _tile(k, tk_cap)
    if slab is not None and slab[2]:
        tm = _tile(math.gcd(m, slab[2]), tm_cap)
    nk = k // tk
    if mode == "tn":
        a_spec = pl.BlockSpec((tk, tm), lambda j, i, kk, *_: (kk, i))
    else:
        a_spec = pl.BlockSpec((tm, tk), lambda j, i, kk, *_: (i, kk))
    if mode == "nt":
        b_spec = pl.BlockSpec((tn, tk), lambda j, i, kk, *_: (j, kk))
    else:
        b_spec = pl.BlockSpec((tk, tn), lambda j, i, kk, *_: (kk, j))
    dims = (_DIMS[mode], ((), ()))
    aliased = into is not None

    def body(*refs):
        a_ref, b_ref = refs[0], refs[1]
        o_ref = refs[3] if aliased else refs[2]
        p = lax.dot_general(a_ref[...], b_ref[...], dims, preferred_element_type=F32)
        if nk == 1:
            o_ref[...] = p.astype(out_dtype).reshape(o_ref.shape)
        else:
            acc = refs[-1]
            kk = pl.program_id(2)

            @pl.when(kk == 0)
            def _():
                acc[...] = p

            @pl.when(kk > 0)
            def _():
                acc[...] += p

            @pl.when(kk == nk - 1)
            def _():
                o_ref[...] = acc[...].astype(out_dtype).reshape(o_ref.shape)

    if slab is None:
        out_spec = pl.BlockSpec((tm, tn), lambda j, i, kk: (i, j))
        out_shape = jax.ShapeDtypeStruct((m, n), out_dtype)
    else:
        shape3, lead, row0 = slab
        assert row0 % tm == 0 and shape3[2] == n
        out_spec = pl.BlockSpec((1, tm, tn), lambda j, i, kk: (lead, row0 // tm + i, j))
        out_shape = jax.ShapeDtypeStruct(shape3, out_dtype)
    in_specs, args = [a_spec, b_spec], [a, b]
    if aliased:
        in_specs.append(pl.BlockSpec(memory_space=pl.ANY))
        args.append(into)
    out, moved = _call(
        body, grid=(n // tn, m // tm, nk), in_specs=in_specs, out_specs=out_spec, out_shape=out_shape,
        scratch_shapes=[pltpu.VMEM((tm, tn), F32)] if nk > 1 else [], dims=("parallel", "parallel", "arbitrary"), name=name,
        args=args, aliases={2: 0} if aliased else None, job=job)
    return out if job is None else (out, moved)


def _matmul_pieces(pieces, addend, name, tk, job=None, tm_cap=512, tn_cap=1024):
    m = pieces[0][0].shape[0]
    n = pieces[0][1].shape[1]
    tm, tn = _tile(m, tm_cap), _tile(n, tn_cap)
    spans, s0 = [], 0
    for a, b, row0 in pieces:
        assert a.shape[1] % tk == 0 and row0 % tk == 0 and b.shape[1] == n and a.shape[0] == m
        spans.append((s0, a.shape[1] // tk, row0 // tk))
        s0 += a.shape[1] // tk
    steps = s0
    np_ = len(pieces)
    groups = []
    for (a, b, _), (first, count, brow) in zip(pieces, spans):
        if groups and groups[-1][0] is b and groups[-1][3] + groups[-1][2] == brow:
            groups[-1][2] += count
        else:
            groups.append([b, first, count, brow])
    b_of = []
    for first, count, _ in spans:
        b_of.append(next(k for k, g in enumerate(groups) if g[1] <= first < g[1] + g[2]))
    ng = len(groups)

    def body(*refs):
        o_ref, acc = refs[-2], refs[-1]
        s = pl.program_id(2)

        @pl.when(s == 0)
        def _():
            acc[...] = refs[np_ + ng][...] if addend is not None else jnp.zeros((tm, tn), F32)

        for p, (first, count, _) in enumerate(spans):
            @pl.when((s >= first) & (s < first + count))
            def _(p=p):
                acc[...] += jnp.dot(refs[p][...], refs[np_ + b_of[p]][...], preferred_element_type=F32)

        @pl.when(s == steps - 1)
        def _():
            o_ref[...] = acc[...]

    in_specs, args = [], []
    for (a, _, _), (first, count, _) in zip(pieces, spans):
        in_specs.append(pl.BlockSpec((tm, tk), lambda j, i, s, f=first, c=count: (i, jnp.clip(s - f, 0, c - 1))))
        args.append(a)
    for b, first, count, brow in groups:
        in_specs.append(pl.BlockSpec((tk, tn), lambda j, i, s, f=first, c=count, r=brow: (r + jnp.clip(s - f, 0, c - 1), j)))
        args.append(b)
    if addend is not None:
        in_specs.append(pl.BlockSpec((tm, tn), lambda j, i, s: (i, j)))
        args.append(addend)
    out, moved = _call(
        body, grid=(n // tn, m // tm, steps), in_specs=in_specs, out_specs=pl.BlockSpec((tm, tn), lambda j, i, s: (i, j)),
        out_shape=jax.ShapeDtypeStruct((m, n), F32), scratch_shapes=[pltpu.VMEM((tm, tn), F32)],
        dims=("parallel", "parallel", "arbitrary"), name=name, args=args, job=job)
    return out if job is None else (out, moved)


def _norm_fwd(x, z, g_post, g_next, name):
    t, d = x.shape
    tt = _rows(t, 512)
    row = pl.BlockSpec((tt, d), lambda i: (i, 0))
    vec = pl.BlockSpec((1, d), lambda i: (0, 0))

    def body(*refs):
        if z is None:
            x_ref, gn_ref, h_ref = refs
            xn = x_ref[...]
        else:
            x_ref, z_ref, gp_ref, gn_ref, xo_ref, h_ref = refs
            zz = z_ref[...]
            r = lax.rsqrt(jnp.mean(zz * zz, axis=-1, keepdims=True) + EPS)
            xn = x_ref[...] + zz * r * gp_ref[...]
            xo_ref[...] = xn
        r2 = lax.rsqrt(jnp.mean(xn * xn, axis=-1, keepdims=True) + EPS)
        h_ref[...] = (xn * r2 * gn_ref[...]).astype(BF16)

    if z is None:
        return pl.pallas_call(
            body, grid=(t // tt,), in_specs=[row, vec], out_specs=row,
            out_shape=jax.ShapeDtypeStruct((t, d), BF16), compiler_params=_cparams(("parallel",)), name=name,
        )(x, g_next)
    return pl.pallas_call(
        body, grid=(t // tt,), in_specs=[row, row, vec, vec], out_specs=[row, row],
        out_shape=[jax.ShapeDtypeStruct((t, d), F32), jax.ShapeDtypeStruct((t, d), BF16)],
        compiler_params=_cparams(("parallel",)), name=name,
    )(x, z, g_post, g_next)


def _rms_bwd(dy, x, g):
    r = lax.rsqrt(jnp.mean(x * x, axis=-1, keepdims=True) + EPS)
    n = x * r
    dn = dy * g
    dx = r * (dn - n * jnp.mean(dn * n, axis=-1, keepdims=True))
    return dx, dy * n


def _norm_bwd(dres, pre, post, name):
    t, d = dres.shape
    tt = _rows(t, 512)
    nt = t // tt
    row = pl.BlockSpec((tt, d), lambda i: (i, 0))
    vec = pl.BlockSpec((1, d), lambda i: (0, 0))
    has_pre, has_post = pre is not None, post is not None
    n_in = 1 + (3 if has_pre else 0) + (2 if has_post else 0)
    n_out = has_pre + has_post + has_pre + has_post

    def body(*refs):
        ins, outs, scr = refs[:n_in], refs[n_in:n_in + n_out], refs[n_in + n_out:]
        i = pl.program_id(0)
        dx = ins[0][...]
        pos, opos, spos = 1, 0, 0
        accs = []
        if has_pre:
            dh_ref, xa_ref, ga_ref = ins[pos:pos + 3]
            pos += 3
            dxa, dga_t = _rms_bwd(dh_ref[...], xa_ref[...], ga_ref[...])
            dx = dx + dxa
            outs[opos][...] = dx
            opos += 1
            accs.append((scr[spos], dga_t))
            spos += 1
        if has_post:
            zb_ref, gb_ref = ins[pos:pos + 2]
            dz, dgb_t = _rms_bwd(dx, zb_ref[...], gb_ref[...])
            outs[opos][...] = dz.astype(BF16)
            opos += 1
            accs.append((scr[spos], dgb_t))
            spos += 1
        for (acc, val), out in zip(accs, outs[opos:]):
            part = _sum8(val)

            @pl.when(i == 0)
            def _(acc=acc, part=part):
                acc[...] = part

            @pl.when(i > 0)
            def _(acc=acc, part=part):
                acc[...] += part

            @pl.when(i == nt - 1)
            def _(acc=acc, out=out):
                out[...] = jnp.sum(acc[...], axis=0, keepdims=True)

    in_specs, args = [row], [dres]
    out_specs, out_shape = [], []
    if has_pre:
        in_specs += [row, row, vec]
        args += list(pre)
        out_specs.append(row)
        out_shape.append(jax.ShapeDtypeStruct((t, d), F32))
    if has_post:
        in_specs += [row, vec]
        args += list(post)
        out_specs.append(row)
        out_shape.append(jax.ShapeDtypeStruct((t, d), BF16))
    for _ in range(has_pre + has_post):
        out_specs.append(vec)
        out_shape.append(jax.ShapeDtypeStruct((1, d), F32))
    return pl.pallas_call(
        body, grid=(nt,), in_specs=in_specs, out_specs=out_specs, out_shape=out_shape,
        scratch_shapes=[pltpu.VMEM((SUBLANE, d), F32)] * (has_pre + has_post),
        compiler_params=_cparams(("arbitrary",)), name=name,
    )(*args)


def _loss_grad(y, target, name):
    t, d = y.shape
    tt = _rows(t, 512)
    nt = t // tt
    row = pl.BlockSpec((tt, d), lambda i: (i, 0))
    inv_d = 1.0 / d

    def body(y_ref, t_ref, dy_ref, l_ref):
        i = pl.program_id(0)
        diff = y_ref[...] - t_ref[...]
        dy_ref[...] = diff * inv_d
        s8 = _sum8(diff * diff)
        part = s8[:, 0:LANE]
        for k in range(1, d // LANE):
            part = part + s8[:, k * LANE:(k + 1) * LANE]
        part = part * (0.5 * inv_d)

        @pl.when(i == 0)
        def _():
            l_ref[...] = part

        @pl.when(i > 0)
        def _():
            l_ref[...] += part

    return pl.pallas_call(
        body, grid=(nt,), in_specs=[row, row],
        out_specs=[row, pl.BlockSpec((SUBLANE, LANE), lambda i: (0, 0))],
        out_shape=[jax.ShapeDtypeStruct((t, d), F32), jax.ShapeDtypeStruct((SUBLANE, LANE), F32)],
        compiler_params=_cparams(("arbitrary",)), name=name,
    )(y, target)


def _swiglu_fwd(a, b, name):
    t, f = a.shape
    tt = _rows(t, 256)
    blk = pl.BlockSpec((tt, f), lambda i: (i, 0))

    def body(a_ref, b_ref, m_ref):
        av = a_ref[...]
        m_ref[...] = (av * _sigmoid(av) * b_ref[...]).astype(BF16)

    return pl.pallas_call(
        body, grid=(t // tt,), in_specs=[blk, blk], out_specs=blk,
        out_shape=jax.ShapeDtypeStruct((t, f), BF16), compiler_params=_cparams(("parallel",)), name=name,
    )(a, b)


def _swiglu_bwd(a, b, dm, name):
    t, f = a.shape
    tt = _rows(t, 256)
    blk = pl.BlockSpec((tt, f), lambda i: (i, 0))

    def body(a_ref, b_ref, dm_ref, da_ref, db_ref):
        av = a_ref[...]
        s = _sigmoid(av)
        dv = dm_ref[...]
        da_ref[...] = (dv * b_ref[...] * s * (1.0 + av * (1.0 - s))).astype(BF16)
        db_ref[...] = (dv * av * s).astype(BF16)

    return pl.pallas_call(
        body, grid=(t // tt,), in_specs=[blk, blk, blk], out_specs=[blk, blk],
        out_shape=[jax.ShapeDtypeStruct((t, f), BF16)] * 2, compiler_params=_cparams(("parallel",)), name=name,
    )(a, b, dm)


def _log_sigmoid(x):
    return jnp.minimum(x, 0.0) - jnp.log1p(jnp.exp(-jnp.abs(x)))


def _fox_prep(f_t, b_f, name):
    h, t = f_t.shape

    def body(f_ref, b_ref, c_ref):
        r = lax.broadcasted_iota(jnp.int32, (LANE, LANE), 0)
        c = lax.broadcasted_iota(jnp.int32, (LANE, LANE), 1)
        upper = (r <= c).astype(F32)
        carry = jnp.zeros((h, 1), F32)
        for j in range(t // LANE):
            sl = slice(j * LANE, (j + 1) * LANE)
            lf = _log_sigmoid(f_ref[:, sl] + b_ref[...])
            cs = jnp.dot(lf, upper, precision=lax.Precision.HIGHEST, preferred_element_type=F32) + carry
            c_ref[:, sl] = cs
            carry = cs[:, LANE - 1:LANE]

    return pl.pallas_call(body, out_shape=jax.ShapeDtypeStruct((h, t), F32), compiler_params=_cparams(), name=name)(f_t, b_f)


def _fox_bwd(dc_q, dc_k, f_t, b_f, name):
    h, t = f_t.shape

    def body(dq_ref, dk_ref, f_ref, b_ref, df_ref, db_ref):
        r = lax.broadcasted_iota(jnp.int32, (LANE, LANE), 0)
        c = lax.broadcasted_iota(jnp.int32, (LANE, LANE), 1)
        lower = (r >= c).astype(F32)
        carry = jnp.zeros((h, 1), F32)
        dbsum = jnp.zeros((h, 1), F32)
        for j in reversed(range(t // LANE)):
            sl = slice(j * LANE, (j + 1) * LANE)
            dc = dq_ref[:, sl] - dk_ref[:, sl]
            dl = jnp.dot(dc, lower, precision=lax.Precision.HIGHEST, preferred_element_type=F32) + carry
            carry = dl[:, 0:1]
            df = dl * _sigmoid(-(f_ref[:, sl] + b_ref[...]))
            df_ref[:, sl] = df
            dbsum = dbsum + jnp.sum(df, axis=-1, keepdims=True)
        db_ref[...] = dbsum

    return pl.pallas_call(
        body, out_shape=[jax.ShapeDtypeStruct((h, t), F32), jax.ShapeDtypeStruct((h, 1), F32)],
        compiler_params=_cparams(), name=name,
    )(dc_q, dc_k, f_t, b_f)


ATTN_FWD = (1024, 512)
ATTN_BWD = (512, 512)


def _attn_tiles(t, tiles):
    return _tile(t, tiles[0]), _tile(t, tiles[1])


def _attn_fwd(proj, c_t, d, name, job=None):
    t = proj.shape[0]
    h = d // LANE
    bq, bk = _attn_tiles(t, ATTN_FWD)
    nq, nk, rr = t // bq, t // bk, bq // bk
    qc, kc, vc = COL_Q * h, COL_K * h, COL_VA * h
    qscale = LANE ** -0.5 * LOG2E

    def body(q_ref, k_ref, v_ref, cc_ref, cr_ref, o_ref, lse_ref, kb, vt, ckb, acc):
        i = pl.program_id(1)

        @pl.when(i == 0)
        def _():
            kb[...] = k_ref[...].astype(BF16)
            ckb[...] = jnp.broadcast_to(cc_ref[0] * LOG2E, (t, bq))
            for jn in range(nk):
                vt[jn] = v_ref[jn * bk:(jn + 1) * bk, :].T.astype(BF16)

        q = (q_ref[...] * qscale).astype(BF16)
        cq = cr_ref[0, 0] * LOG2E
        acc[...] = jnp.zeros((LANE, bq), F32)

        def block(j, diag, m_old, l_old):
            rows = pl.ds(pl.multiple_of(j * bk, bk), bk)
            s = _nt_dot(kb[rows, :], q) - ckb[rows, :]
            if diag is not None:
                kk = lax.broadcasted_iota(jnp.int32, (bk, bq), 0)
                qq = lax.broadcasted_iota(jnp.int32, (bk, bq), 1)
                s = jnp.where(qq >= kk + diag * bk, s, NEG)
            m_new = jnp.maximum(m_old, jnp.max(s, axis=0, keepdims=True) + cq)
            p = jnp.exp2(s + (cq - m_new))
            alpha = jnp.exp2(m_old - m_new)
            l_new = alpha * l_old + jnp.sum(p, axis=0, keepdims=True)
            acc[...] = alpha * acc[...] + jnp.dot(vt[j], p.astype(BF16), preferred_element_type=F32)
            return m_new, l_new

        m, l = lax.fori_loop(0, i * rr, lambda j, c: block(j, None, *c),
                             (jnp.full((1, bq), NEG, F32), jnp.zeros((1, bq), F32)))
        for jj in range(rr):
            m, l = block(i * rr + jj, jj, m, l)
        o_ref[...] = (acc[...] / l).T
        lse_ref[0, 0] = m + jnp.log2(l)

    rowq = pl.BlockSpec((1, 1, 1, bq), lambda hh, i: (hh, i, 0, 0))
    outs, moved = _call(
        body, grid=(h, nq),
        in_specs=[
            pl.BlockSpec((bq, LANE), lambda hh, i: (i, qc + hh)),
            pl.BlockSpec((t, LANE), lambda hh, i: (0, kc + hh)),
            pl.BlockSpec((t, LANE), lambda hh, i: (0, vc + hh)),
            pl.BlockSpec((1, t, 1), lambda hh, i: (hh, 0, 0)),
            rowq,
        ],
        out_specs=[pl.BlockSpec((bq, LANE), lambda hh, i: (i, hh)), rowq],
        out_shape=[jax.ShapeDtypeStruct((t, d), F32), jax.ShapeDtypeStruct((h, nq, 1, bq), F32)],
        scratch_shapes=[pltpu.VMEM((t, LANE), BF16), pltpu.VMEM((nk, LANE, bk), BF16), pltpu.VMEM((t, bq), F32),
                        pltpu.VMEM((LANE, bq), F32)],
        dims=("arbitrary", "arbitrary"), name=name,
        args=[proj, proj, proj, c_t.reshape(h, t, 1), c_t.reshape(h, nq, 1, bq)], job=job)
    outs = [outs[0], outs[1].reshape(h, t)]
    return outs if job is None else (outs, moved)


def _attn_bwd(proj, do, o, lse, c_t, d, name, job=None):
    t = proj.shape[0]
    h = d // LANE
    bq, bk = _attn_tiles(t, ATTN_BWD)
    nq, nk, rr = t // bq, t // bk, bq // bk
    qc, kc, vc = COL_Q * h, COL_K * h, COL_VA * h
    scale = LANE ** -0.5

    def body(q_ref, k_ref, v_ref, do_ref, o_ref, lse_ref, cc_ref, cr_ref, dq_ref, dk_ref, dv_ref, dcq_ref, dck_ref,
             kb, kt, vb, ckb, dk_acc, dv_acc, dck_acc, dqt_acc):
        i = pl.program_id(1)

        @pl.when(i == 0)
        def _():
            kb[...] = k_ref[...].astype(BF16)
            vb[...] = v_ref[...].astype(BF16)
            ckb[...] = jnp.broadcast_to(cc_ref[0] * LOG2E, (t, bq))
            for jn in range(nk):
                kt[jn] = k_ref[jn * bk:(jn + 1) * bk, :].T.astype(BF16)
            dk_acc[...] = jnp.zeros((t, LANE), F32)
            dv_acc[...] = jnp.zeros((t, LANE), F32)
            dck_acc[...] = jnp.zeros((t, LANE), F32)

        q = (q_ref[...] * (scale * LOG2E)).astype(BF16)
        dof = do_ref[...]
        dob = dof.astype(BF16)
        delta = jnp.sum((dof * o_ref[...]).T, axis=0, keepdims=True)
        rowb = cr_ref[0, 0] * LOG2E - lse_ref[0, 0]
        dqt_acc[...] = jnp.zeros((LANE, bq), F32)

        def block(j, diag, dcq):
            rows = pl.ds(pl.multiple_of(j * bk, bk), bk)
            p = jnp.exp2(_nt_dot(kb[rows, :], q) - ckb[rows, :] + rowb)
            if diag is not None:
                kk = lax.broadcasted_iota(jnp.int32, (bk, bq), 0)
                qq = lax.broadcasted_iota(jnp.int32, (bk, bq), 1)
                p = jnp.where(qq >= kk + diag * bk, p, 0.0)
            dv_acc[rows, :] += jnp.dot(p.astype(BF16), dob, preferred_element_type=F32)
            ds = p * (_nt_dot(vb[rows, :], dob) - delta)
            dsb = ds.astype(BF16)
            dk_acc[rows, :] += jnp.dot(dsb, q, preferred_element_type=F32)
            dqt_acc[...] += jnp.dot(kt[j], dsb, preferred_element_type=F32)
            part = ds[:, 0:LANE]
            for k in range(1, bq // LANE):
                part = part + ds[:, k * LANE:(k + 1) * LANE]
            dck_acc[rows, :] += part
            return dcq + jnp.sum(ds, axis=0, keepdims=True)

        dcq = lax.fori_loop(0, i * rr, lambda j, c: block(j, None, c), jnp.zeros((1, bq), F32))
        for jj in range(rr):
            dcq = block(i * rr + jj, jj, dcq)
        dq_ref[...] = (dqt_acc[...] * scale).T.astype(BF16)
        dcq_ref[0, 0] = dcq

        @pl.when(i == nq - 1)
        def _():
            dk_ref[...] = (dk_acc[...] * LN2).astype(BF16)
            dv_ref[...] = dv_acc[...].astype(BF16)
            dck_ref[0] = jnp.sum(dck_acc[...], axis=-1, keepdims=True)

    rowq = pl.BlockSpec((1, 1, 1, bq), lambda hh, i: (hh, i, 0, 0))
    blk = pl.BlockSpec((bq, LANE), lambda hh, i: (i, hh))
    whole = pl.BlockSpec((t, LANE), lambda hh, i: (0, hh))
    colk = pl.BlockSpec((1, t, 1), lambda hh, i: (hh, 0, 0))
    outs, moved = _call(
        body, grid=(h, nq),
        in_specs=[
            pl.BlockSpec((bq, LANE), lambda hh, i: (i, qc + hh)),
            pl.BlockSpec((t, LANE), lambda hh, i: (0, kc + hh)),
            pl.BlockSpec((t, LANE), lambda hh, i: (0, vc + hh)),
            blk, blk, rowq, colk, rowq,
        ],
        out_specs=[blk, whole, whole, rowq, colk],
        out_shape=[jax.ShapeDtypeStruct((t, d), BF16), jax.ShapeDtypeStruct((t, d), BF16), jax.ShapeDtypeStruct((t, d), BF16),
                   jax.ShapeDtypeStruct((h, nq, 1, bq), F32), jax.ShapeDtypeStruct((h, t, 1), F32)],
        scratch_shapes=[pltpu.VMEM((t, LANE), BF16), pltpu.VMEM((nk, LANE, bk), BF16), pltpu.VMEM((t, LANE), BF16),
                        pltpu.VMEM((t, bq), F32), pltpu.VMEM((t, LANE), F32), pltpu.VMEM((t, LANE), F32),
                        pltpu.VMEM((t, LANE), F32), pltpu.VMEM((LANE, bq), F32)],
        dims=("arbitrary", "arbitrary"), name=name,
        args=[proj, proj, proj, do, o, lse.reshape(h, nq, 1, bq), c_t.reshape(h, t, 1), c_t.reshape(h, nq, 1, bq)], job=job)
    outs = list(outs[:3]) + [outs[3].reshape(h, t), outs[4].reshape(h, t)]
    return outs if job is None else (outs, moved)


def _sgu_forward(u_ref, v_ref, gv_ref, wm_ref, bs_ref, mix_sc, groups):
    gu, dgu = _gelu_and_grad(u_ref[...])
    gvv, dgv = _gelu_and_grad(v_ref[...])
    mu = jnp.mean(gvv, axis=-1, keepdims=True)
    xc = gvv - mu
    r = lax.rsqrt(jnp.mean(xc * xc, axis=-1, keepdims=True) + EPS)
    nhat = xc * r
    vn = (nhat * gv_ref[...]).astype(BF16)
    for g in range(groups):
        sl = slice(g * LANE, (g + 1) * LANE)
        mix_sc[:, sl] = jnp.dot(wm_ref[g], vn[:, sl], preferred_element_type=F32) + bs_ref[g]
    return gu, dgu, dgv, nhat, r, vn, mix_sc[...]


def _mix_fwd(proj, o, wm, bs, g_v, d, name):
    t = proj.shape[0]
    groups = d // LANE

    def body(u_ref, v_ref, ga_ref, gb_ref, o_ref, wm_ref, bs_ref, gv_ref, out_ref, mix_sc):
        gu, _, _, _, _, _, mixed = _sgu_forward(u_ref, v_ref, gv_ref, wm_ref, bs_ref, mix_sc, groups)
        out_ref[...] = (_sigmoid(ga_ref[...]) * (gu * mixed) + _sigmoid(gb_ref[...]) * o_ref[...]).astype(BF16)

    def colblk(k):
        return pl.BlockSpec((LANE, d), lambda i, k=k: (i, k))

    full3 = pl.BlockSpec((groups, LANE, LANE), lambda i: (0, 0, 0))
    return pl.pallas_call(
        body, grid=(t // LANE,),
        in_specs=[colblk(COL_U), colblk(COL_V), colblk(COL_GA), colblk(COL_GB), colblk(0), full3,
                  pl.BlockSpec((groups, LANE, 1), lambda i: (0, 0, 0)), pl.BlockSpec((1, d), lambda i: (0, 0))],
        out_specs=colblk(0),
        out_shape=jax.ShapeDtypeStruct((t, d), BF16),
        scratch_shapes=[pltpu.VMEM((LANE, d), F32)],
        compiler_params=_cparams(("parallel",)), name=name,
    )(proj, proj, proj, proj, o, wm, bs, g_v)


def _mix_bwd(dmerged, proj, o, wm, wm_t, bs, g_v, d, name):
    t = proj.shape[0]
    groups = d // LANE
    nt = t // LANE

    def body(dm_ref, u_ref, v_ref, ga_ref, gb_ref, o_ref, wm_ref, wmt_ref, bs_ref, gv_ref,
             duv_ref, dg_ref, do_ref, dws_ref, dbs_ref, dgv_ref, mix_sc, dvn_sc, gv_acc):
        i = pl.program_id(0)

        @pl.when(i == 0)
        def _():
            dws_ref[...] = jnp.zeros_like(dws_ref)
            dbs_ref[...] = jnp.zeros_like(dbs_ref)
            gv_acc[...] = jnp.zeros_like(gv_acc)

        gu, dgu, dgv, nhat, r, vn, mixed = _sgu_forward(u_ref, v_ref, gv_ref, wm_ref, bs_ref, mix_sc, groups)
        dm = dm_ref[...]
        sa = _sigmoid(ga_ref[...])
        sb = _sigmoid(gb_ref[...])
        ov = o_ref[...]
        y_a = gu * mixed
        dg_ref[:, 0:d] = (dm * y_a * sa * (1.0 - sa)).astype(BF16)
        dg_ref[:, d:2 * d] = (dm * ov * sb * (1.0 - sb)).astype(BF16)
        do_ref[...] = dm * sb
        dy_a = dm * sa
        duv_ref[:, 0:d] = (dy_a * mixed * dgu).astype(BF16)
        dmixed = dy_a * gu
        dmixed_b = dmixed.astype(BF16)
        for g in range(groups):
            sl = slice(g * LANE, (g + 1) * LANE)
            dvn_sc[:, sl] = jnp.dot(wmt_ref[g], dmixed_b[:, sl], preferred_element_type=F32)
            dws_ref[g] += _nt_dot(dmixed_b[:, sl], vn[:, sl])
            dbs_ref[g] += jnp.sum(dmixed[:, sl], axis=-1, keepdims=True)
        dvn = dvn_sc[...]
        gv_acc[...] += _sum8(dvn * nhat)
        dn = dvn * gv_ref[...]
        dgelu = r * (dn - jnp.mean(dn, axis=-1, keepdims=True) - nhat * jnp.mean(dn * nhat, axis=-1, keepdims=True))
        duv_ref[:, d:2 * d] = (dgelu * dgv).astype(BF16)

        @pl.when(i == nt - 1)
        def _():
            dgv_ref[...] = jnp.sum(gv_acc[...], axis=0, keepdims=True)
            rr = lax.broadcasted_iota(jnp.int32, (LANE, LANE), 0)
            cl = lax.broadcasted_iota(jnp.int32, (LANE, LANE), 1)
            for g in range(groups):
                dws_ref[g] = jnp.where(rr >= cl, dws_ref[g], 0.0)

    def colblk(k):
        return pl.BlockSpec((LANE, d), lambda i, k=k: (i, k))

    full3 = pl.BlockSpec((groups, LANE, LANE), lambda i: (0, 0, 0))
    col3 = pl.BlockSpec((groups, LANE, 1), lambda i: (0, 0, 0))
    vec = pl.BlockSpec((1, d), lambda i: (0, 0))
    two = pl.BlockSpec((LANE, 2 * d), lambda i: (i, 0))
    return pl.pallas_call(
        body, grid=(nt,),
        in_specs=[colblk(0), colblk(COL_U), colblk(COL_V), colblk(COL_GA), colblk(COL_GB), colblk(0), full3, full3, col3, vec],
        out_specs=[two, two, colblk(0), full3, col3, vec],
        out_shape=[jax.ShapeDtypeStruct((t, 2 * d), BF16), jax.ShapeDtypeStruct((t, 2 * d), BF16), jax.ShapeDtypeStruct((t, d), F32),
                   jax.ShapeDtypeStruct((groups, LANE, LANE), F32), jax.ShapeDtypeStruct((groups, LANE, 1), F32),
                   jax.ShapeDtypeStruct((1, d), F32)],
        scratch_shapes=[pltpu.VMEM((LANE, d), F32), pltpu.VMEM((LANE, d), F32), pltpu.VMEM((SUBLANE, d), F32)],
        compiler_params=_cparams(("arbitrary",)), name=name,
    )(dmerged, proj, proj, proj, proj, o, wm, wm_t, bs, g_v)


def _adam_math(w, g, m, v):
    nm = ADAM_B1 * m + (1.0 - ADAM_B1) * g
    nv = ADAM_B2 * v + (1.0 - ADAM_B2) * (g * g)
    delta = -ADAM_LR * ((nm * ADAM_C1) / (jnp.sqrt(nv * ADAM_C2) + ADAM_EPS) + ADAM_WD * w)
    return delta, nm, nv


def _adamw(w, g, m, v, name):
    r, c = w.shape
    cap = max(SUBLANE, (2 * 1024 * 1024) // (4 * c) // SUBLANE * SUBLANE)
    tr = _rows(r, cap)

    def body(w_ref, g_ref, m_ref, v_ref, d_ref, nm_ref, nv_ref):
        d_ref[...], nm_ref[...], nv_ref[...] = _adam_math(w_ref[...], g_ref[...], m_ref[...], v_ref[...])

    blk = pl.BlockSpec((tr, c), lambda i: (i, 0))
    return pl.pallas_call(
        body, grid=(r // tr,), in_specs=[blk] * 4, out_specs=[blk] * 3,
        out_shape=[jax.ShapeDtypeStruct((r, c), F32)] * 3, compiler_params=_cparams(("parallel",)), name=name,
    )(w, g, m, v)


def _adamw_layers(w, g0, g1, m, v, name):
    _, r, c = w.shape
    cap = max(SUBLANE, (1024 * 1024) // (4 * c) // SUBLANE * SUBLANE)
    tr = _rows(r, cap)

    def body(w_ref, g0_ref, g1_ref, m_ref, v_ref, g_ref, d_ref, nm_ref, nv_ref):
        gg = jnp.where(pl.program_id(0) == 0, g0_ref[...], g1_ref[...])
        g_ref[0] = gg
        d_ref[0], nm_ref[0], nv_ref[0] = _adam_math(w_ref[0], gg, m_ref[0], v_ref[0])

    lay = pl.BlockSpec((1, tr, c), lambda l, i: (l, i, 0))

    def gspec(l0):
        return pl.BlockSpec((tr, c), lambda l, i: (jnp.where(l == l0, i, 0), 0))

    return pl.pallas_call(
        body, grid=(2, r // tr), in_specs=[lay, gspec(0), gspec(1), lay, lay], out_specs=[lay] * 4,
        out_shape=[jax.ShapeDtypeStruct((2, r, c), F32)] * 4, compiler_params=_cparams(("arbitrary", "arbitrary")), name=name,
    )(w, g0, g1, m, v)


def _adamw_interleaved(w, g0, g1, m, v, name):
    r, _, c = w.shape
    tr = 128

    def body(w_ref, g0_ref, g1_ref, m_ref, v_ref, g_ref, d_ref, nm_ref, nv_ref):
        for l, gl in enumerate((g0_ref, g1_ref)):
            gg = gl[...]
            g_ref[:, l, :] = gg
            d_ref[:, l, :], nm_ref[:, l, :], nv_ref[:, l, :] = _adam_math(w_ref[:, l, :], gg, m_ref[:, l, :], v_ref[:, l, :])

    lay = pl.BlockSpec((tr, 2, c), lambda i: (i, 0, 0))
    flat = pl.BlockSpec((tr, c), lambda i: (i, 0))
    return pl.pallas_call(
        body, grid=(pl.cdiv(r, tr),), in_specs=[lay, flat, flat, lay, lay], out_specs=[lay] * 4,
        out_shape=[jax.ShapeDtypeStruct((r, 2, c), F32)] * 4, compiler_params=_cparams(("parallel",)), name=name,
    )(w, g0, g1, m, v)


def _add_half(p4, recv, c_idx, name):
    _, r, c = p4.shape
    hw = c // 2
    tr = 256 if r % 256 == 0 else r

    def body(c_ref, a_ref, b_ref, o_ref):
        o_ref[...] = (a_ref[...].astype(F32) + b_ref[...].astype(F32)).astype(BF16)

    return pl.pallas_call(
        body,
        grid_spec=pltpu.PrefetchScalarGridSpec(
            num_scalar_prefetch=1, grid=(N_CHIPS, pl.cdiv(r, tr)),
            in_specs=[pl.BlockSpec((1, tr, hw), lambda s, i, cr: (s, i, cr[0])), pl.BlockSpec((1, tr, hw), lambda s, i, cr: (s, i, 0))],
            out_specs=pl.BlockSpec((1, tr, hw), lambda s, i, cr: (s, i, 0)),
        ),
        out_shape=jax.ShapeDtypeStruct((N_CHIPS, r, hw), BF16), compiler_params=_cparams(("parallel", "parallel")), name=name,
    )(c_idx, p4, recv)


def _sum_slots(x, own, sel, name, out_cols=None):
    s, r, c = x.shape
    tr = 128 if r % 128 == 0 else r

    def body(sel_ref, x_ref, own_ref, o_ref):
        mine = own_ref[0].astype(F32)
        acc = jnp.zeros((tr, c), F32)
        for k in range(s):
            acc = acc + jnp.where(sel_ref[0] == k, mine, x_ref[k].astype(F32))
        o_ref[...] = acc

    return pl.pallas_call(
        body,
        grid_spec=pltpu.PrefetchScalarGridSpec(
            num_scalar_prefetch=1, grid=(pl.cdiv(r, tr),),
            in_specs=[pl.BlockSpec((s, tr, c), lambda i, sr: (0, i, 0)), pl.BlockSpec((1, tr, c), lambda i, sr: (sr[1], i, 0))],
            out_specs=pl.BlockSpec((tr, c), lambda i, sr: (i, sr[2])),
        ),
        out_shape=jax.ShapeDtypeStruct((r, out_cols or c), F32), compiler_params=_cparams(("parallel",)), name=name,
    )(sel, x, own)


def _half_cols(width, hc):
    hw = width // 2
    assert hw % LANE == 0
    return pl.ds(pl.multiple_of(hc * hw, LANE), hw)


def _remote(src, dst, ssem, rsem, k, to):
    return pltpu.make_async_remote_copy(src_ref=src, dst_ref=dst, send_sem=ssem.at[k], recv_sem=rsem.at[k], device_id=to,
                                        device_id_type=MESH)


def _gather_job(bufs, mid_at=0.5):
    def part(o, a, slot, hc):
        return o[a].at[slot, :, _half_cols(bufs[a].shape[2], hc)]

    def first(ins, o, fresh, ssem, rsem):
        x, y, c, chips = _place()
        for a in range(len(bufs)):
            mine = part(o, a, 2 * x + y, c)
            for j, chip in enumerate(chips):
                _remote(mine, mine, ssem, rsem, 6 * a + j, (chip[0], chip[1], c)).start()

    def mid(ins, o, fresh, ssem, rsem):
        x, y, c, chips = _place()
        for a in range(len(bufs)):
            for j, chip in enumerate(chips):
                got = part(o, a, 2 * chip[0] + chip[1], c)
                _remote(got, got, ssem, rsem, 6 * a + j, (x, y, c)).wait_recv()
                _remote(got, got, ssem, rsem, 6 * a + 3 + j, (x, y, 1 - c)).start()

    def last(ins, o, fresh, ssem, rsem):
        x, y, c, chips = _place()
        for a in range(len(bufs)):
            for j, chip in enumerate(chips):
                got = part(o, a, 2 * chip[0] + chip[1], 1 - c)
                _remote(got, got, ssem, rsem, 6 * a + 3 + j, (x, y, c)).wait_recv()
        for a in range(len(bufs)):
            mine = part(o, a, 2 * x + y, c)
            for j, chip in enumerate(chips):
                _remote(mine, mine, ssem, rsem, 6 * a + j, (x, y, c)).wait_send()
                passed = part(o, a, 2 * chip[0] + chip[1], c)
                _remote(passed, passed, ssem, rsem, 6 * a + 3 + j, (x, y, c)).wait_send()

    return _Job([], bufs, [], 6 * len(bufs), first, mid, last, mid_at)


def _swap_job(p4s):
    def pairs(ins, fresh, c):
        return [(a, s, ins[a].at[s, :, _half_cols(p4s[a].shape[2], 1 - c)], fresh[a].at[s])
                for a in range(len(p4s)) for s in range(N_CHIPS)]

    def first(ins, inout, fresh, ssem, rsem):
        x, y, c, _ = _place()
        for a, s, src, dst in pairs(ins, fresh, c):
            _remote(src, dst, ssem, rsem, N_CHIPS * a + s, (x, y, 1 - c)).start()

    def last(ins, inout, fresh, ssem, rsem):
        x, y, c, _ = _place()
        for a, s, src, dst in pairs(ins, fresh, c):
            _remote(src, dst, ssem, rsem, N_CHIPS * a + s, (x, y, 1 - c)).wait()

    fresh = [jax.ShapeDtypeStruct(p.shape[:2] + (p.shape[2] // 2,), p.dtype) for p in p4s]
    return _Job(p4s, [], fresh, N_CHIPS * len(p4s), first, None, last)


def _scatter_job(parts):
    def first(ins, inout, fresh, ssem, rsem):
        x, y, c, chips = _place()
        for a in range(len(parts)):
            for j, chip in enumerate(chips):
                _remote(ins[a].at[2 * chip[0] + chip[1]], fresh[a].at[2 * x + y], ssem, rsem, 3 * a + j, (chip[0], chip[1], c)).start()

    def last(ins, inout, fresh, ssem, rsem):
        x, y, c, chips = _place()
        for a in range(len(parts)):
            for j, chip in enumerate(chips):
                slot = 2 * chip[0] + chip[1]
                _remote(ins[a].at[slot], fresh[a].at[slot], ssem, rsem, 3 * a + j, (x, y, c)).wait()

    return _Job(parts, [], [jax.ShapeDtypeStruct(p.shape, p.dtype) for p in parts], 3 * len(parts), first, None, last)


def _share_job(gs):
    def halves(o, a, c):
        width = gs[a].shape[1]
        return o[a].at[:, _half_cols(width, c)], o[a].at[:, _half_cols(width, 1 - c)]

    def first(ins, o, fresh, ssem, rsem):
        x, y, c, _ = _place()
        for a in range(len(gs)):
            mine, _ = halves(o, a, c)
            _remote(mine, mine, ssem, rsem, a, (x, y, 1 - c)).start()

    def last(ins, o, fresh, ssem, rsem):
        x, y, c, _ = _place()
        for a in range(len(gs)):
            mine, theirs = halves(o, a, c)
            _remote(mine, theirs, ssem, rsem, a, (x, y, 1 - c)).wait()

    return _Job([], gs, [], len(gs), first, None, last)


def _gather_all(buf, name):
    def body(b_ref, o_ref, ssem, rsem):
        x, y, c, _ = _place()
        me = 4 * x + 2 * y + c
        flips = [(fx, fy, fc) for fx in (0, 1) for fy in (0, 1) for fc in (0, 1)][1:]
        peers = [((1 - x) if fx else x, (1 - y) if fy else y, (1 - c) if fc else c) for fx, fy, fc in flips]
        sends = []
        for k, peer in enumerate(peers):
            cp = pltpu.make_async_remote_copy(src_ref=b_ref, dst_ref=o_ref.at[me], send_sem=ssem.at[k], recv_sem=rsem.at[k],
                                              device_id=peer, device_id_type=MESH)
            cp.start()
            sends.append(cp)
        for k, peer in enumerate(peers):
            slot = o_ref.at[4 * peer[0] + 2 * peer[1] + peer[2]]
            pltpu.make_async_remote_copy(src_ref=slot, dst_ref=slot, send_sem=ssem.at[k], recv_sem=rsem.at[k],
                                         device_id=(x, y, c), device_id_type=MESH).wait_recv()
        for cp in sends:
            cp.wait_send()

    return pl.pallas_call(
        body, in_specs=[_HBM], out_specs=_HBM,
        out_shape=jax.ShapeDtypeStruct((N_DEV,) + buf.shape, buf.dtype),
        scratch_shapes=[pltpu.SemaphoreType.DMA((N_DEV - 1,)), pltpu.SemaphoreType.DMA((N_DEV - 1,))],
        name=name,
    )(buf)


def _layer_forward(x, h, w_in_t, rest, sm, d, proj_job=None, attn_job=None):
    moved = []
    if proj_job is None:
        proj = _matmul(h, w_in_t, "nt", F32, "proj_fwd", n=7 * d, tn_cap=1792)
    else:
        proj, moved = _matmul(h, w_in_t, "nt", F32, "proj_fwd_gather", n=7 * d, tn_cap=1792, job=proj_job)
    f_t = _matmul(w_in_t[7 * d:], h, "nt", F32, "forget_fwd", tn_cap=1024)
    c_t = _fox_prep(f_t, sm["b_f"], "fox_prep")
    if attn_job is None:
        o, lse = _attn_fwd(proj, c_t, d, "attn_fwd")
    else:
        (o, lse), moved_attn = _attn_fwd(proj, c_t, d, "attn_fwd_gather", job=attn_job)
        moved = (moved, moved_attn)
    wts = rest(moved)
    merged = _mix_fwd(proj, o, sm["wm"], sm["bs"], sm["g_v"], d, "mix_fwd")
    z = _matmul(merged, wts["w_out"], "nn", F32, "out_fwd")
    x1, h2 = _norm_fwd(x, z, sm["g_post"], sm["g_fpre"], "norm_mid")
    a = _matmul(h2, wts["w_g_t"], "nt", F32, "gate_fwd", tn_cap=1408)
    b = _matmul(h2, wts["w_u_t"], "nt", F32, "up_fwd", tn_cap=1408)
    mm = _swiglu_fwd(a, b, "swiglu_fwd")
    z2 = _matmul(mm, wts["w_d"], "nn", F32, "down_fwd")
    return dict(x=x, h=h, proj=proj, f_t=f_t, c_t=c_t, o=o, lse=lse, merged=merged, z=z, x1=x1,
                h2=h2, a=a, b=b, mm=mm, z2=z2)


def _layer_backward(dz2, dx2, sv, wts, sm, d, carry=None):
    t = dx2.shape[0]
    heads = d // LANE
    ff = wts["w_d"].shape[0]
    in_w = 7 * d + heads
    g, pay = {}, {}

    def payload(key, a, b, rows, row0, name):
        pay[key] = _matmul(a, b, "tn", BF16, name, slab=((1, rows, d), 0, row0), into=pay.get(key), tm_cap=1408, tn_cap=1024,
                           tk_cap=1024)

    if carry is None:
        dm = _matmul(dz2, wts["w_d"], "nt", F32, "down_bwd_x", tn_cap=1408, tk_cap=1024)
    else:
        dm = carry.swap_in(lambda job: _matmul(dz2, wts["w_d"], "nt", F32, "down_bwd_x_swap", tn_cap=1408, tk_cap=1024, job=job))
    payload("w_d", sv["mm"], dz2, ff, 0, "down_bwd_w")
    da, db = _swiglu_bwd(sv["a"], sv["b"], dm, "swiglu_bwd")
    dh2 = _matmul_pieces([(da, wts["w_g_t"], 0), (db, wts["w_u_t"], 0)], None, "gu_bwd_x", tk=_tile(ff, 1408), tm_cap=1024)
    payload("w_g", da, sv["h2"], ff, 0, "gate_bwd_w")
    payload("w_u", db, sv["h2"], ff, 0, "up_bwd_w")
    dx1, dz, g["g_fpre"], g["g_post"] = _norm_bwd(dx2, (dh2, sv["x1"], sm["g_fpre"]), (sv["z"], sm["g_post"]), "norm_bwd_mid")
    dmerged = _matmul(dz, wts["w_out"], "nt", F32, "out_bwd_x", tk_cap=1024)
    payload("w_out", sv["merged"], dz, d, 0, "out_bwd_w")
    d_uv, d_g, do, g["w_s"], g["b_s"], g["g_v"] = _mix_bwd(dmerged, sv["proj"], sv["o"], sm["wm"], sm["wm_t"], sm["bs"],
                                                         sm["g_v"], d, "mix_bwd")
    attn_args = (sv["proj"], do, sv["o"], sv["lse"], sv["c_t"], d)
    if carry is None:
        dq, dk, dv, dc_q, dc_k = _attn_bwd(*attn_args, "attn_bwd")
    else:
        dq, dk, dv, dc_q, dc_k = carry.scatter_in(lambda job: _attn_bwd(*attn_args, "attn_bwd_scatter", job=job))
    df_t, g["b_f"] = _fox_bwd(dc_q, dc_k, sv["f_t"], sm["b_f"], "fox_bwd")
    df_b = df_t.astype(BF16)
    dh_f = _matmul(df_b, wts["w_in_t"][7 * d:], "tn", F32, "forget_bwd_x")
    pieces = [(d_uv, COL_U), (dq, COL_Q), (dk, COL_K), (dv, COL_VA), (d_g, COL_GA)]
    ops = [(p, wts["w_in_t"], col * d) for p, col in pieces]
    kw = dict(tk=_tile(d, 1024), tm_cap=1024, tn_cap=512)
    if carry is None:
        dh = _matmul_pieces(ops, dh_f, "proj_bwd_x", **kw)
    else:
        dh = carry.share_in(lambda job: _matmul_pieces(ops, dh_f, "proj_bwd_x_share", job=job, **kw))
    for p, col in pieces:
        payload("w_in", p, sv["h"], in_w, col * d, "proj_bwd_w")
    w_f_rows = _matmul(df_b, sv["h"], "nn", BF16, "forget_bwd_w", tk_cap=1024)
    pay["w_in"] = lax.dynamic_update_slice(pay["w_in"], w_f_rows[None], (0, 7 * d, 0))
    return dh, dx1, g, pay


class _GradExchange:
    def __init__(self, pay, c_idx, chip):
        self.keys = ["w_in", "w_out", "w_g", "w_u", "w_d"]
        self.p4 = []
        for k in self.keys:
            _, rows, dd = pay[k].shape
            self.p4.append(pay[k].reshape(N_CHIPS, rows // N_CHIPS, dd))
        self.c_idx = c_idx
        self.sel = jnp.stack([chip, chip, c_idx[0]]).astype(jnp.int32)

    def _after_swap(self, landed):
        self.parts = [_add_half(p, r, self.c_idx, "add_sibling") for p, r in zip(self.p4, landed)]

    def _after_scatter(self, landed):
        self.g = [_sum_slots(got, sent, self.sel, "sum_chips", out_cols=p.shape[2])
                  for got, sent, p in zip(landed, self.parts, self.p4)]

    def swap_in(self, host):
        out, landed = host(_swap_job(self.p4))
        self._after_swap(landed)
        return out

    def scatter_in(self, host):
        out, landed = host(_scatter_job(self.parts))
        self._after_scatter(landed)
        return out

    def share_in(self, host):
        out, self.g = host(_share_job(self.g))
        return out

    def run(self):
        self._after_swap(_run_job(_swap_job(self.p4), "swap_grads"))
        self._after_scatter(_run_job(_scatter_job(self.parts), "scatter_grads"))
        self.g = _run_job(_share_job(self.g), "share_grads")

    def grads(self):
        return self.g


def _small_pack(parts):
    flat = jnp.concatenate([p.reshape(-1) for p in parts])
    n = flat.shape[0]
    pad = (-n) % (LANE * LANE)
    return jnp.pad(flat, (0, pad)).reshape(-1, LANE)


def kernel(x, mix_pre_g, w_in, b_forget, sgu_norm_g, w_spatial, b_spatial, w_out, mix_post_g, ffn_pre_g, w_gate, w_up, w_down, ffn_post_g, loss_target, m_mix_pre_g, m_w_in, m_b_forget, m_sgu_norm_g, m_w_spatial, m_b_spatial, m_w_out, m_mix_post_g, m_ffn_pre_g, m_w_gate, m_w_up, m_w_down, m_ffn_post_g, v_mix_pre_g, v_w_in, v_b_forget, v_sgu_norm_g, v_w_spatial, v_b_spatial, v_w_out, v_mix_post_g, v_ffn_pre_g, v_w_gate, v_w_up, v_w_down, v_ffn_post_g):
    depth, d = mix_pre_g.shape
    assert depth == 2, "core c of a chip owns layer c"
    heads = d // LANE
    t = x.shape[1]
    ff = w_down.shape[1] * N_CHIPS
    in_w = w_in.shape[2] * N_CHIPS
    assert in_w == 7 * d + heads
    xs = x.reshape(t, d)
    target = loss_target.reshape(t, d)
    c_idx = lax.axis_index("c").astype(jnp.int32).reshape(1)
    chip = 2 * lax.axis_index("x") + lax.axis_index("y")
    dev = 2 * chip + lax.axis_index("c")

    def in_view(w):
        return jnp.transpose(w, (2, 0, 1))

    def gu_view(w):
        return jnp.transpose(w, (0, 2, 1))

    own = [jnp.transpose(in_view(w_in).astype(BF16), (1, 0, 2)), w_out.astype(BF16), gu_view(w_gate).astype(BF16),
           gu_view(w_up).astype(BF16), w_down.astype(BF16)]
    bufs = [[lax.dynamic_update_slice(jnp.zeros((N_CHIPS,) + o.shape[1:], BF16), o[l][None], (chip, 0, 0)) for o in own]
            for l in range(depth)]
    first_in = _run_job(_gather_job([bufs[0][0]]), "gather_first")[0]

    def weights(g_in, g_out, g_g, g_u, g_d):
        return dict(w_in_t=g_in.reshape(in_w, d), w_out=g_out.reshape(d, d), w_g_t=g_g.reshape(ff, d),
                    w_u_t=g_u.reshape(ff, d), w_d=g_d.reshape(ff, d))

    tril = jnp.tril(jnp.ones((LANE, LANE), bool))
    smalls = []
    for l in range(depth):
        wm = jnp.where(tril[None], w_spatial[l], 0.0).astype(BF16)
        smalls.append(dict(
            b_f=b_forget[l].reshape(heads, 1), wm=wm, wm_t=jnp.swapaxes(wm, 1, 2), bs=b_spatial[l].reshape(heads, LANE, 1),
            g_v=sgu_norm_g[l].reshape(1, d), g_pre=mix_pre_g[l].reshape(1, d), g_post=mix_post_g[l].reshape(1, d),
            g_fpre=ffn_pre_g[l].reshape(1, d), g_fpost=ffn_post_g[l].reshape(1, d)))

    wts = []

    def rest_first(moved):
        wts.append(weights(first_in, *moved[0]))
        wts.append(weights(*moved[1]))
        return wts[0]

    h = _norm_fwd(xs, None, None, smalls[0]["g_pre"], "norm_first")
    sv = _layer_forward(xs, h, first_in.reshape(in_w, d), rest_first, smalls[0], d,
                        proj_job=_gather_job(bufs[0][1:], mid_at=1.0), attn_job=_gather_job(bufs[1]))
    saved = [sv]
    for l in range(1, depth):
        xin, h = _norm_fwd(saved[l - 1]["x1"], saved[l - 1]["z2"], smalls[l - 1]["g_fpost"], smalls[l]["g_pre"], "norm_out")
        saved.append(_layer_forward(xin, h, wts[l]["w_in_t"], lambda moved, l=l: wts[l], smalls[l], d))
    y, _ = _norm_fwd(saved[-1]["x1"], saved[-1]["z2"], smalls[-1]["g_fpost"], smalls[-1]["g_pre"], "norm_out")
    dy, loss_part = _loss_grad(y, target, "loss")
    loss = lax.psum(jnp.sum(loss_part), ("x", "y", "c"))

    grads = [None] * depth
    exchanges = [None] * depth
    carry = None
    dx2 = dy
    dz2, g_fpost = _norm_bwd(dx2, None, (saved[depth - 1]["z2"], smalls[depth - 1]["g_fpost"]), "norm_bwd_top")
    for l in reversed(range(depth)):
        dh, dx1, g, pay = _layer_backward(dz2, dx2, saved[l], wts[l], smalls[l], d, carry)
        exchanges[l] = carry = _GradExchange(pay, c_idx, chip)
        g["g_fpost"] = g_fpost
        if l > 0:
            dx2, dz2, g["g_pre"], g_fpost = _norm_bwd(dx1, (dh, saved[l]["x"], smalls[l]["g_pre"]),
                                                       (saved[l - 1]["z2"], smalls[l - 1]["g_fpost"]), "norm_bwd_between")
        else:
            grad_x, g["g_pre"] = _norm_bwd(dx1, (dh, saved[l]["x"], smalls[l]["g_pre"]), None, "norm_bwd_bottom")
        grads[l] = g
    exchanges[0].run()
    big = list(zip(*[ex.grads() for ex in exchanges]))

    small_names = ["g_pre", "b_f", "g_v", "w_s", "b_s", "g_post", "g_fpre", "g_fpost"]
    small_shapes = [mix_pre_g.shape, b_forget.shape, sgu_norm_g.shape, w_spatial.shape, b_spatial.shape, mix_post_g.shape,
                    ffn_pre_g.shape, ffn_post_g.shape]
    parts = [jnp.stack([grads[l][nme].reshape(-1) for l in range(depth)]) for nme in small_names]
    packed = _small_pack(parts)
    dev_sel = jnp.stack([dev, jnp.zeros_like(dev), jnp.zeros_like(dev)]).astype(jnp.int32)
    total = _sum_slots(_gather_all(packed, "gather_small"), packed[None], dev_sel, "sum_small").reshape(-1)
    small_grads, off = {}, 0
    for nme, shp in zip(small_names, small_shapes):
        n = math.prod(shp)
        small_grads[nme] = total[off:off + n].reshape(shp)
        off += n

    def adam_small(w, g, m, v):
        shp = w.shape
        if w.ndim >= 3 and shp[-1] >= LANE:
            two = (math.prod(shp[:-1]), shp[-1])
        else:
            two = (1, math.prod(shp)) if math.prod(shp) < LANE else (math.prod(shp) // LANE, LANE)
        outs = _adamw(w.reshape(two), g.reshape(two), m.reshape(two), v.reshape(two), "adamw")
        return [g] + [o.reshape(shp) for o in outs]

    def adam_in(w, m, v):
        outs = _adamw_interleaved(in_view(w), big[0][0], big[0][1], in_view(m), in_view(v), "adamw_in")
        return [jnp.transpose(o, (1, 2, 0)) for o in outs]

    def adam_gu(k, w, m, v):
        outs = _adamw_layers(gu_view(w), big[k][0], big[k][1], gu_view(m), gu_view(v), "adamw_layers")
        return [jnp.transpose(o, (0, 2, 1)) for o in outs]

    def adam_rows(k, w, m, v):
        return _adamw_layers(w, big[k][0], big[k][1], m, v, "adamw_layers")

    results = [
        adam_small(mix_pre_g, small_grads["g_pre"], m_mix_pre_g, v_mix_pre_g),
        adam_in(w_in, m_w_in, v_w_in),
        adam_small(b_forget, small_grads["b_f"], m_b_forget, v_b_forget),
        adam_small(sgu_norm_g, small_grads["g_v"], m_sgu_norm_g, v_sgu_norm_g),
        adam_small(w_spatial, small_grads["w_s"], m_w_spatial, v_w_spatial),
        adam_small(b_spatial, small_grads["b_s"], m_b_spatial, v_b_spatial),
        adam_rows(1, w_out, m_w_out, v_w_out),
        adam_small(mix_post_g, small_grads["g_post"], m_mix_post_g, v_mix_post_g),
        adam_small(ffn_pre_g, small_grads["g_fpre"], m_ffn_pre_g, v_ffn_pre_g),
        adam_gu(2, w_gate, m_w_gate, v_w_gate),
        adam_gu(3, w_up, m_w_up, v_w_up),
        adam_rows(4, w_down, m_w_down, v_w_down),
        adam_small(ffn_post_g, small_grads["g_fpost"], m_ffn_post_g, v_ffn_post_g),
    ]
    gs, deltas, new_ms, new_vs = zip(*results)
    return (loss, grad_x.reshape(x.shape), *gs, *deltas, *new_ms, *new_vs)
```

```python
import functools
import math

import jax
import jax.numpy as jnp
from jax import lax
from jax.experimental import pallas as pl
from jax.experimental.pallas import tpu as pltpu

F32 = jnp.float32
BF16 = jnp.bfloat16

EPS = 1e-6
LANE = 128
SUBLANE = 8
N_CHIPS = 4
N_DEV = 8
VMEM_LIMIT = 48 * 1024 * 1024
MESH = pl.DeviceIdType.MESH

ADAM_LR = 0.001
ADAM_B1 = 0.9
ADAM_B2 = 0.999
ADAM_EPS = 1e-08
ADAM_WD = 0.01
ADAM_STEP = 10
ADAM_C1 = 1.0 / (1.0 - ADAM_B1 ** ADAM_STEP)
ADAM_C2 = 1.0 / (1.0 - ADAM_B2 ** ADAM_STEP)

GELU_K = math.sqrt(2.0 / math.pi)
GELU_A = 0.044715
NEG = -1e30
LOG2E = 1.4426950408889634
LN2 = 0.6931471805599453

COL_U, COL_V, COL_Q, COL_K, COL_VA, COL_GA, COL_GB, COL_F = range(8)


def _cparams(sem=None):
    return pltpu.CompilerParams(dimension_semantics=sem, vmem_limit_bytes=VMEM_LIMIT)


def _tile(n, cap):
    best = None
    for t in range(LANE, min(n, cap) + 1, LANE):
        if n % t == 0:
            best = t
    return best if best is not None else n


def _rows(n, cap):
    best = None
    for t in range(SUBLANE, min(n, cap) + 1, SUBLANE):
        if n % t == 0:
            best = t
    return best if best is not None else n


def _gelu_and_grad(x):
    x2 = x * x
    t = jnp.tanh(GELU_K * (x + GELU_A * x2 * x))
    g = 0.5 * x * (1.0 + t)
    dg = 0.5 * (1.0 + t) + 0.5 * x * (1.0 - t * t) * (GELU_K * (1.0 + 3.0 * GELU_A * x2))
    return g, dg


def _sigmoid(x):
    return 1.0 / (1.0 + jnp.exp(-x))


def _sum8(v):
    n, d = v.shape
    return v.reshape(n // SUBLANE, SUBLANE, d).sum(axis=0)


def _nt_dot(a, b):
    return lax.dot_general(a, b, (((1,), (1,)), ((), ())), preferred_element_type=F32)


_HBM = pl.BlockSpec(memory_space=pl.ANY)


def _place():
    x, y, c = lax.axis_index("x"), lax.axis_index("y"), lax.axis_index("c")
    chips = [(1 - x, y), (x, 1 - y), (1 - x, 1 - y)]
    return x, y, c, chips


class _Job:
    def __init__(self, ins, inout, fresh, nsem, first, mid, last, mid_at=0.5):
        self.ins, self.inout, self.fresh, self.nsem = list(ins), list(inout), list(fresh), nsem
        self.first, self.mid, self.last, self.mid_at = first, mid, last, mid_at


def _call(body, *, grid, in_specs, out_specs, out_shape, scratch_shapes, dims, name, args, aliases=None, job=None):
    single = not isinstance(out_shape, (list, tuple))
    out_specs = [out_specs] if single else list(out_specs)
    out_shape = [out_shape] if single else list(out_shape)
    aliases = dict(aliases or {})
    if job is None:
        outs = pl.pallas_call(body, grid=grid, in_specs=in_specs, out_specs=out_specs, out_shape=out_shape,
                              scratch_shapes=scratch_shapes, input_output_aliases=aliases, compiler_params=_cparams(dims),
                              name=name)(*args)
        return (outs[0] if single else outs), []
    n_in, n_out, n_scr = len(args), len(out_shape), len(scratch_shapes)
    n_ji, n_jio, n_jf = len(job.ins), len(job.inout), len(job.fresh)
    total = math.prod(grid)

    def wrapped(*refs):
        host_in = refs[:n_in]
        pos = n_in
        j_ins = refs[pos:pos + n_ji]
        pos += n_ji + n_jio
        host_out = refs[pos:pos + n_out]
        pos += n_out
        j_inout = refs[pos:pos + n_jio]
        pos += n_jio
        j_fresh = refs[pos:pos + n_jf]
        pos += n_jf
        host_scr = refs[pos:pos + n_scr]
        ssem, rsem = refs[pos + n_scr:]
        flat = 0
        for ax, size in enumerate(grid):
            flat = flat * size + pl.program_id(ax)

        def hook(fn, at):
            if fn is not None:
                @pl.when(flat == at)
                def _():
                    fn(j_ins, j_inout, j_fresh, ssem, rsem)

        hook(job.first, 0)
        body(*host_in, *host_out, *host_scr)
        hook(job.mid, min(int(total * job.mid_at), total - 1))
        hook(job.last, total - 1)

    for k in range(n_jio):
        aliases[n_in + n_ji + k] = n_out + k
    outs = pl.pallas_call(
        wrapped, grid=grid,
        in_specs=list(in_specs) + [_HBM] * (n_ji + n_jio),
        out_specs=out_specs + [_HBM] * (n_jio + n_jf),
        out_shape=out_shape + [jax.ShapeDtypeStruct(b.shape, b.dtype) for b in job.inout] + list(job.fresh),
        scratch_shapes=list(scratch_shapes) + [pltpu.SemaphoreType.DMA((job.nsem,)), pltpu.SemaphoreType.DMA((job.nsem,))],
        input_output_aliases=aliases, compiler_params=_cparams(tuple("arbitrary" for _ in grid)), name=name,
    )(*args, *job.ins, *job.inout)
    host = outs[:n_out]
    return (host[0] if single else host), outs[n_out:]


def _run_job(job, name):
    n_ji, n_jio, n_jf = len(job.ins), len(job.inout), len(job.fresh)

    def body(*refs):
        j_ins = refs[:n_ji]
        pos = n_ji + n_jio
        j_inout = refs[pos:pos + n_jio]
        j_fresh = refs[pos + n_jio:pos + n_jio + n_jf]
        ssem, rsem = refs[pos + n_jio + n_jf:]
        for fn in (job.first, job.mid, job.last):
            if fn is not None:
                fn(j_ins, j_inout, j_fresh, ssem, rsem)

    return pl.pallas_call(
        body, in_specs=[_HBM] * (n_ji + n_jio), out_specs=[_HBM] * (n_jio + n_jf),
        out_shape=[jax.ShapeDtypeStruct(b.shape, b.dtype) for b in job.inout] + list(job.fresh),
        scratch_shapes=[pltpu.SemaphoreType.DMA((job.nsem,)), pltpu.SemaphoreType.DMA((job.nsem,))],
        input_output_aliases={n_ji + k: k for k in range(n_jio)}, name=name,
    )(*job.ins, *job.inout)


_DIMS ={"nn": ((1,), (0,)), "nt": ((1,), (1,)), "tn": ((0,), (0,))}


def _matmul(a, b, mode, out_dtype, name, n=None, slab=None, into=None, job=None, tm_cap=512, tn_cap=2048, tk_cap=1408):
    if mode == "nn":
        (m, k), (k2, nn_) = a.shape, b.shape
    elif mode == "nt":
        (m, k), (nn_, k2) = a.shape, b.shape
    else:
        (k, m), (k2, nn_) = a.shape, b.shape
    n = nn_ if n is None else n
    assert k == k2, (a.shape, b.shape, mode)
    tm, tn, tk = _tile(m, tm_cap), _tile(n, tn_cap), _tile(k, tk_cap)
    if slab is not None and slab[2]:
        tm = _tile(math.gcd(m, slab[2]), tm_cap)
    nk = k // tk
    if mode == "tn":
        a_spec = pl.BlockSpec((tk, tm), lambda j, i, kk, *_: (kk, i))
    else:
        a_spec = pl.BlockSpec((tm, tk), lambda j, i, kk, *_: (i, kk))
    if mode == "nt":
        b_spec = pl.BlockSpec((tn, tk), lambda j, i, kk, *_: (j, kk))
    else:
        b_spec = pl.BlockSpec((tk, tn), lambda j, i, kk, *_: (kk, j))
    dims = (_DIMS[mode], ((), ()))
    aliased = into is not None

    def body(*refs):
        a_ref, b_ref = refs[0], refs[1]
        o_ref = refs[3] if aliased else refs[2]
        p = lax.dot_general(a_ref[...], b_ref[...], dims, preferred_element_type=F32)
        if nk == 1:
            o_ref[...] = p.astype(out_dtype).reshape(o_ref.shape)
        else:
            acc = refs[-1]
            kk = pl.program_id(2)

            @pl.when(kk == 0)
            def _():
                acc[...] = p

            @pl.when(kk > 0)
            def _():
                acc[...] += p

            @pl.when(kk == nk - 1)
            def _():
                o_ref[...] = acc[...].astype(out_dtype).reshape(o_ref.shape)

    if slab is None:
        out_spec = pl.BlockSpec((tm, tn), lambda j, i, kk: (i, j))
        out_shape = jax.ShapeDtypeStruct((m, n), out_dtype)
    else:
        shape3, lead, row0 = slab
        assert row0 % tm == 0 and shape3[2] == n
        out_spec = pl.BlockSpec((1, tm, tn), lambda j, i, kk: (lead, row0 // tm + i, j))
        out_shape = jax.ShapeDtypeStruct(shape3, out_dtype)
    in_specs, args = [a_spec, b_spec], [a, b]
    if aliased:
        in_specs.append(pl.BlockSpec(memory_space=pl.ANY))
        args.append(into)
    out, moved = _call(
        body, grid=(n // tn, m // tm, nk), in_specs=in_specs, out_specs=out_spec, out_shape=out_shape,
        scratch_shapes=[pltpu.VMEM((tm, tn), F32)] if nk > 1 else [], dims=("parallel", "parallel", "arbitrary"), name=name,
        args=args, aliases={2: 0} if aliased else None, job=job)
    return out if job is None else (out, moved)


def _matmul_pieces(pieces, addend, name, tk, job=None, tm_cap=512, tn_cap=1024):
    m = pieces[0][0].shape[0]
    n = pieces[0][1].shape[1]
    tm, tn = _tile(m, tm_cap), _tile(n, tn_cap)
    spans, s0 = [], 0
    for a, b, row0 in pieces:
        assert a.shape[1] % tk == 0 and row0 % tk == 0 and b.shape[1] == n and a.shape[0] == m
        spans.append((s0, a.shape[1] // tk, row0 // tk))
        s0 += a.shape[1] // tk
    steps = s0
    np_ = len(pieces)
    groups = []
    for (a, b, _), (first, count, brow) in zip(pieces, spans):
        if groups and groups[-1][0] is b and groups[-1][3] + groups[-1][2] == brow:
            groups[-1][2] += count
        else:
            groups.append([b, first, count, brow])
    b_of = []
    for first, count, _ in spans:
        b_of.append(next(k for k, g in enumerate(groups) if g[1] <= first < g[1] + g[2]))
    ng = len(groups)

    def body(*refs):
        o_ref, acc = refs[-2], refs[-1]
        s = pl.program_id(2)

        @pl.when(s == 0)
        def _():
            acc[...] = refs[np_ + ng][...] if addend is not None else jnp.zeros((tm, tn), F32)

        for p, (first, count, _) in enumerate(spans):
            @pl.when((s >= first) & (s < first + count))
            def _(p=p):
                acc[...] += jnp.dot(refs[p][...], refs[np_ + b_of[p]][...], preferred_element_type=F32)

        @pl.when(s == steps - 1)
        def _():
            o_ref[...] = acc[...]

    in_specs, args = [], []
    for (a, _, _), (first, count, _) in zip(pieces, spans):
        in_specs.append(pl.BlockSpec((tm, tk), lambda j, i, s, f=first, c=count: (i, jnp.clip(s - f, 0, c - 1))))
        args.append(a)
    for b, first, count, brow in groups:
        in_specs.append(pl.BlockSpec((tk, tn), lambda j, i, s, f=first, c=count, r=brow: (r + jnp.clip(s - f, 0, c - 1), j)))
        args.append(b)
    if addend is not None:
        in_specs.append(pl.BlockSpec((tm, tn), lambda j, i, s: (i, j)))
        args.append(addend)
    out, moved = _call(
        body, grid=(n // tn, m // tm, steps), in_specs=in_specs, out_specs=pl.BlockSpec((tm, tn), lambda j, i, s: (i, j)),
        out_shape=jax.ShapeDtypeStruct((m, n), F32), scratch_shapes=[pltpu.VMEM((tm, tn), F32)],
        dims=("parallel", "parallel", "arbitrary"), name=name, args=args, job=job)
    return out if job is None else (out, moved)


def _norm_fwd(x, z, g_post, g_next, name):
    t, d = x.shape
    tt = _rows(t, 512)
    row = pl.BlockSpec((tt, d), lambda i: (i, 0))
    vec = pl.BlockSpec((1, d), lambda i: (0, 0))

    def body(*refs):
        if z is None:
            x_ref, gn_ref, h_ref = refs
            xn = x_ref[...]
        else:
            x_ref, z_ref, gp_ref, gn_ref, xo_ref, h_ref = refs
            zz = z_ref[...]
            r = lax.rsqrt(jnp.mean(zz * zz, axis=-1, keepdims=True) + EPS)
            xn = x_ref[...] + zz * r * gp_ref[...]
            xo_ref[...] = xn
        r2 = lax.rsqrt(jnp.mean(xn * xn, axis=-1, keepdims=True) + EPS)
        h_ref[...] = (xn * r2 * gn_ref[...]).astype(BF16)

    if z is None:
        return pl.pallas_call(
            body, grid=(t // tt,), in_specs=[row, vec], out_specs=row,
            out_shape=jax.ShapeDtypeStruct((t, d), BF16), compiler_params=_cparams(("parallel",)), name=name,
        )(x, g_next)
    return pl.pallas_call(
        body, grid=(t // tt,), in_specs=[row, row, vec, vec], out_specs=[row, row],
        out_shape=[jax.ShapeDtypeStruct((t, d), F32), jax.ShapeDtypeStruct((t, d), BF16)],
        compiler_params=_cparams(("parallel",)), name=name,
    )(x, z, g_post, g_next)


def _rms_bwd(dy, x, g):
    r = lax.rsqrt(jnp.mean(x * x, axis=-1, keepdims=True) + EPS)
    n = x * r
    dn = dy * g
    dx = r * (dn - n * jnp.mean(dn * n, axis=-1, keepdims=True))
    return dx, dy * n


def _norm_bwd(dres, pre, post, name):
    t, d = dres.shape
    tt = _rows(t, 512)
    nt = t // tt
    row = pl.BlockSpec((tt, d), lambda i: (i, 0))
    vec = pl.BlockSpec((1, d), lambda i: (0, 0))
    has_pre, has_post = pre is not None, post is not None
    n_in = 1 + (3 if has_pre else 0) + (2 if has_post else 0)
    n_out = has_pre + has_post + has_pre + has_post

    def body(*refs):
        ins, outs, scr = refs[:n_in], refs[n_in:n_in + n_out], refs[n_in + n_out:]
        i = pl.program_id(0)
        dx = ins[0][...]
        pos, opos, spos = 1, 0, 0
        accs = []
        if has_pre:
            dh_ref, xa_ref, ga_ref = ins[pos:pos + 3]
            pos += 3
            dxa, dga_t = _rms_bwd(dh_ref[...], xa_ref[...], ga_ref[...])
            dx = dx + dxa
            outs[opos][...] = dx
            opos += 1
            accs.append((scr[spos], dga_t))
            spos += 1
        if has_post:
            zb_ref, gb_ref = ins[pos:pos + 2]
            dz, dgb_t = _rms_bwd(dx, zb_ref[...], gb_ref[...])
            outs[opos][...] = dz.astype(BF16)
            opos += 1
            accs.append((scr[spos], dgb_t))
            spos += 1
        for (acc, val), out in zip(accs, outs[opos:]):
            part = _sum8(val)

            @pl.when(i == 0)
            def _(acc=acc, part=part):
                acc[...] = part

            @pl.when(i > 0)
            def _(acc=acc, part=part):
                acc[...] += part

            @pl.when(i == nt - 1)
            def _(acc=acc, out=out):
                out[...] = jnp.sum(acc[...], axis=0, keepdims=True)

    in_specs, args = [row], [dres]
    out_specs, out_shape = [], []
    if has_pre:
        in_specs += [row, row, vec]
        args += list(pre)
        out_specs.append(row)
        out_shape.append(jax.ShapeDtypeStruct((t, d), F32))
    if has_post:
        in_specs += [row, vec]
        args += list(post)
        out_specs.append(row)
        out_shape.append(jax.ShapeDtypeStruct((t, d), BF16))
    for _ in range(has_pre + has_post):
        out_specs.append(vec)
        out_shape.append(jax.ShapeDtypeStruct((1, d), F32))
    return pl.pallas_call(
        body, grid=(nt,), in_specs=in_specs, out_specs=out_specs, out_shape=out_shape,
        scratch_shapes=[pltpu.VMEM((SUBLANE, d), F32)] * (has_pre + has_post),
        compiler_params=_cparams(("arbitrary",)), name=name,
    )(*args)


def _loss_grad(y, target, name):
    t, d = y.shape
    tt = _rows(t, 512)
    nt = t // tt
    row = pl.BlockSpec((tt, d), lambda i: (i, 0))
    inv_d = 1.0 / d

    def body(y_ref, t_ref, dy_ref, l_ref):
        i = pl.program_id(0)
        diff = y_ref[...] - t_ref[...]
        dy_ref[...] = diff * inv_d
        s8 = _sum8(diff * diff)
        part = s8[:, 0:LANE]
        for k in range(1, d // LANE):
            part = part + s8[:, k * LANE:(k + 1) * LANE]
        part = part * (0.5 * inv_d)

        @pl.when(i == 0)
        def _():
            l_ref[...] = part

        @pl.when(i > 0)
        def _():
            l_ref[...] += part

    return pl.pallas_call(
        body, grid=(nt,), in_specs=[row, row],
        out_specs=[row, pl.BlockSpec((SUBLANE, LANE), lambda i: (0, 0))],
        out_shape=[jax.ShapeDtypeStruct((t, d), F32), jax.ShapeDtypeStruct((SUBLANE, LANE), F32)],
        compiler_params=_cparams(("arbitrary",)), name=name,
    )(y, target)


def _swiglu_fwd(a, b, name):
    t, f = a.shape
    tt = _rows(t, 256)
    blk = pl.BlockSpec((tt, f), lambda i: (i, 0))

    def body(a_ref, b_ref, m_ref):
        av = a_ref[...]
        m_ref[...] = (av * _sigmoid(av) * b_ref[...]).astype(BF16)

    return pl.pallas_call(
        body, grid=(t // tt,), in_specs=[blk, blk], out_specs=blk,
        out_shape=jax.ShapeDtypeStruct((t, f), BF16), compiler_params=_cparams(("parallel",)), name=name,
    )(a, b)


def _swiglu_bwd(a, b, dm, name):
    t, f = a.shape
    tt = _rows(t, 256)
    blk = pl.BlockSpec((tt, f), lambda i: (i, 0))

    def body(a_ref, b_ref, dm_ref, da_ref, db_ref):
        av = a_ref[...]
        s = _sigmoid(av)
        dv = dm_ref[...]
        da_ref[...] = (dv * b_ref[...] * s * (1.0 + av * (1.0 - s))).astype(BF16)
        db_ref[...] = (dv * av * s).astype(BF16)

    return pl.pallas_call(
        body, grid=(t // tt,), in_specs=[blk, blk, blk], out_specs=[blk, blk],
        out_shape=[jax.ShapeDtypeStruct((t, f), BF16)] * 2, compiler_params=_cparams(("parallel",)), name=name,
    )(a, b, dm)


def _log_sigmoid(x):
    return jnp.minimum(x, 0.0) - jnp.log1p(jnp.exp(-jnp.abs(x)))


def _fox_prep(f_t, b_f, name):
    h, t = f_t.shape

    def body(f_ref, b_ref, c_ref):
        r = lax.broadcasted_iota(jnp.int32, (LANE, LANE), 0)
        c = lax.broadcasted_iota(jnp.int32, (LANE, LANE), 1)
        upper = (r <= c).astype(F32)
        carry = jnp.zeros((h, 1), F32)
        for j in range(t // LANE):
            sl = slice(j * LANE, (j + 1) * LANE)
            lf = _log_sigmoid(f_ref[:, sl] + b_ref[...])
            cs = jnp.dot(lf, upper, precision=lax.Precision.HIGHEST, preferred_element_type=F32) + carry
            c_ref[:, sl] = cs
            carry = cs[:, LANE - 1:LANE]

    return pl.pallas_call(body, out_shape=jax.ShapeDtypeStruct((h, t), F32), compiler_params=_cparams(), name=name)(f_t, b_f)


def _fox_bwd(dc_q, dc_k, f_t, b_f, name):
    h, t = f_t.shape

    def body(dq_ref, dk_ref, f_ref, b_ref, df_ref, db_ref):
        r = lax.broadcasted_iota(jnp.int32, (LANE, LANE), 0)
        c = lax.broadcasted_iota(jnp.int32, (LANE, LANE), 1)
        lower = (r >= c).astype(F32)
        carry = jnp.zeros((h, 1), F32)
        dbsum = jnp.zeros((h, 1), F32)
        for j in reversed(range(t // LANE)):
            sl = slice(j * LANE, (j + 1) * LANE)
            dc = dq_ref[:, sl] - dk_ref[:, sl]
            dl = jnp.dot(dc, lower, precision=lax.Precision.HIGHEST, preferred_element_type=F32) + carry
            carry = dl[:, 0:1]
            df = dl * _sigmoid(-(f_ref[:, sl] + b_ref[...]))
            df_ref[:, sl] = df
            dbsum = dbsum + jnp.sum(df, axis=-1, keepdims=True)
        db_ref[...] = dbsum

    return pl.pallas_call(
        body, out_shape=[jax.ShapeDtypeStruct((h, t), F32), jax.ShapeDtypeStruct((h, 1), F32)],
        compiler_params=_cparams(), name=name,
    )(dc_q, dc_k, f_t, b_f)


ATTN_FWD = (1024, 512)
ATTN_BWD = (512, 512)


def _attn_tiles(t, tiles):
    return _tile(t, tiles[0]), _tile(t, tiles[1])


def _attn_fwd(proj, c_t, d, name, job=None):
    t = proj.shape[0]
    h = d // LANE
    bq, bk = _attn_tiles(t, ATTN_FWD)
    nq, nk, rr = t // bq, t // bk, bq // bk
    qc, kc, vc = COL_Q * h, COL_K * h, COL_VA * h
    qscale = LANE ** -0.5 * LOG2E

    def body(q_ref, k_ref, v_ref, cc_ref, cr_ref, o_ref, lse_ref, kb, vt, ckb, acc):
        i = pl.program_id(1)

        @pl.when(i == 0)
        def _():
            kb[...] = k_ref[...].astype(BF16)
            ckb[...] = jnp.broadcast_to(cc_ref[0] * LOG2E, (t, bq))
            for jn in range(nk):
                vt[jn] = v_ref[jn * bk:(jn + 1) * bk, :].T.astype(BF16)

        q = (q_ref[...] * qscale).astype(BF16)
        cq = cr_ref[0, 0] * LOG2E
        acc[...] = jnp.zeros((LANE, bq), F32)

        def block(j, diag, m_old, l_old):
            rows = pl.ds(pl.multiple_of(j * bk, bk), bk)
            s = _nt_dot(kb[rows, :], q) - ckb[rows, :]
            if diag is not None:
                kk = lax.broadcasted_iota(jnp.int32, (bk, bq), 0)
                qq = lax.broadcasted_iota(jnp.int32, (bk, bq), 1)
                s = jnp.where(qq >= kk + diag * bk, s, NEG)
            m_new = jnp.maximum(m_old, jnp.max(s, axis=0, keepdims=True) + cq)
            p = jnp.exp2(s + (cq - m_new))
            alpha = jnp.exp2(m_old - m_new)
            l_new = alpha * l_old + jnp.sum(p, axis=0, keepdims=True)
            acc[...] = alpha * acc[...] + jnp.dot(vt[j], p.astype(BF16), preferred_element_type=F32)
            return m_new, l_new

        m, l = lax.fori_loop(0, i * rr, lambda j, c: block(j, None, *c),
                             (jnp.full((1, bq), NEG, F32), jnp.zeros((1, bq), F32)))
        for jj in range(rr):
            m, l = block(i * rr + jj, jj, m, l)
        o_ref[...] = (acc[...] / l).T
        lse_ref[0, 0] = m + jnp.log2(l)

    rowq = pl.BlockSpec((1, 1, 1, bq), lambda hh, i: (hh, i, 0, 0))
    outs, moved = _call(
        body, grid=(h, nq),
        in_specs=[
            pl.BlockSpec((bq, LANE), lambda hh, i: (i, qc + hh)),
            pl.BlockSpec((t, LANE), lambda hh, i: (0, kc + hh)),
            pl.BlockSpec((t, LANE), lambda hh, i: (0, vc + hh)),
            pl.BlockSpec((1, t, 1), lambda hh, i: (hh, 0, 0)),
            rowq,
        ],
        out_specs=[pl.BlockSpec((bq, LANE), lambda hh, i: (i, hh)), rowq],
        out_shape=[jax.ShapeDtypeStruct((t, d), F32), jax.ShapeDtypeStruct((h, nq, 1, bq), F32)],
        scratch_shapes=[pltpu.VMEM((t, LANE), BF16), pltpu.VMEM((nk, LANE, bk), BF16), pltpu.VMEM((t, bq), F32),
                        pltpu.VMEM((LANE, bq), F32)],
        dims=("arbitrary", "arbitrary"), name=name,
        args=[proj, proj, proj, c_t.reshape(h, t, 1), c_t.reshape(h, nq, 1, bq)], job=job)
    outs = [outs[0], outs[1].reshape(h, t)]
    return outs if job is None else (outs, moved)


def _attn_bwd(proj, do, o, lse, c_t, d, name, job=None):
    t = proj.shape[0]
    h = d // LANE
    bq, bk = _attn_tiles(t, ATTN_BWD)
    nq, nk, rr = t // bq, t // bk, bq // bk
    qc, kc, vc = COL_Q * h, COL_K * h, COL_VA * h
    scale = LANE ** -0.5

    def body(q_ref, k_ref, v_ref, do_ref, o_ref, lse_ref, cc_ref, cr_ref, dq_ref, dk_ref, dv_ref, dcq_ref, dck_ref,
             kb, kt, vb, ckb, dk_acc, dv_acc, dck_acc, dqt_acc):
        i = pl.program_id(1)

        @pl.when(i == 0)
        def _():
            kb[...] = k_ref[...].astype(BF16)
            vb[...] = v_ref[...].astype(BF16)
            ckb[...] = jnp.broadcast_to(cc_ref[0] * LOG2E, (t, bq))
            for jn in range(nk):
                kt[jn] = k_ref[jn * bk:(jn + 1) * bk, :].T.astype(BF16)
            dk_acc[...] = jnp.zeros((t, LANE), F32)
            dv_acc[...] = jnp.zeros((t, LANE), F32)
            dck_acc[...] = jnp.zeros((t, LANE), F32)

        q = (q_ref[...] * (scale * LOG2E)).astype(BF16)
        dof = do_ref[...]
        dob = dof.astype(BF16)
        delta = jnp.sum((dof * o_ref[...]).T, axis=0, keepdims=True)
        rowb = cr_ref[0, 0] * LOG2E - lse_ref[0, 0]
        dqt_acc[...] = jnp.zeros((LANE, bq), F32)

        def block(j, diag, dcq):
            rows = pl.ds(pl.multiple_of(j * bk, bk), bk)
            p = jnp.exp2(_nt_dot(kb[rows, :], q) - ckb[rows, :] + rowb)
            if diag is not None:
                kk = lax.broadcasted_iota(jnp.int32, (bk, bq), 0)
                qq = lax.broadcasted_iota(jnp.int32, (bk, bq), 1)
                p = jnp.where(qq >= kk + diag * bk, p, 0.0)
            dv_acc[rows, :] += jnp.dot(p.astype(BF16), dob, preferred_element_type=F32)
            ds = p * (_nt_dot(vb[rows, :], dob) - delta)
            dsb = ds.astype(BF16)
            dk_acc[rows, :] += jnp.dot(dsb, q, preferred_element_type=F32)
            dqt_acc[...] += jnp.dot(kt[j], dsb, preferred_element_type=F32)
            part = ds[:, 0:LANE]
            for k in range(1, bq // LANE):
                part = part + ds[:, k * LANE:(k + 1) * LANE]
            dck_acc[rows, :] += part
            return dcq + jnp.sum(ds, axis=0, keepdims=True)

        dcq = lax.fori_loop(0, i * rr, lambda j, c: block(j, None, c), jnp.zeros((1, bq), F32))
        for jj in range(rr):
            dcq = block(i * rr + jj, jj, dcq)
        dq_ref[...] = (dqt_acc[...] * scale).T.astype(BF16)
        dcq_ref[0, 0] = dcq

        @pl.when(i == nq - 1)
        def _():
            dk_ref[...] = (dk_acc[...] * LN2).astype(BF16)
            dv_ref[...] = dv_acc[...].astype(BF16)
            dck_ref[0] = jnp.sum(dck_acc[...], axis=-1, keepdims=True)

    rowq = pl.BlockSpec((1, 1, 1, bq), lambda hh, i: (hh, i, 0, 0))
    blk = pl.BlockSpec((bq, LANE), lambda hh, i: (i, hh))
    whole = pl.BlockSpec((t, LANE), lambda hh, i: (0, hh))
    colk = pl.BlockSpec((1, t, 1), lambda hh, i: (hh, 0, 0))
    outs, moved = _call(
        body, grid=(h, nq),
        in_specs=[
            pl.BlockSpec((bq, LANE), lambda hh, i: (i, qc + hh)),
            pl.BlockSpec((t, LANE), lambda hh, i: (0, kc + hh)),
            pl.BlockSpec((t, LANE), lambda hh, i: (0, vc + hh)),
            blk, blk, rowq, colk, rowq,
        ],
        out_specs=[blk, whole, whole, rowq, colk],
        out_shape=[jax.ShapeDtypeStruct((t, d), BF16), jax.ShapeDtypeStruct((t, d), BF16), jax.ShapeDtypeStruct((t, d), BF16),
                   jax.ShapeDtypeStruct((h, nq, 1, bq), F32), jax.ShapeDtypeStruct((h, t, 1), F32)],
        scratch_shapes=[pltpu.VMEM((t, LANE), BF16), pltpu.VMEM((nk, LANE, bk), BF16), pltpu.VMEM((t, LANE), BF16),
                        pltpu.VMEM((t, bq), F32), pltpu.VMEM((t, LANE), F32), pltpu.VMEM((t, LANE), F32),
                        pltpu.VMEM((t, LANE), F32), pltpu.VMEM((LANE, bq), F32)],
        dims=("arbitrary", "arbitrary"), name=name,
        args=[proj, proj, proj, do, o, lse.reshape(h, nq, 1, bq), c_t.reshape(h, t, 1), c_t.reshape(h, nq, 1, bq)], job=job)
    outs = list(outs[:3]) + [outs[3].reshape(h, t), outs[4].reshape(h, t)]
    return outs if job is None else (outs, moved)


def _sgu_forward(u_ref, v_ref, gv_ref, wm_ref, bs_ref, mix_sc, groups):
    gu, dgu = _gelu_and_grad(u_ref[...])
    gvv, dgv = _gelu_and_grad(v_ref[...])
    mu = jnp.mean(gvv, axis=-1, keepdims=True)
    xc = gvv - mu
    r = lax.rsqrt(jnp.mean(xc * xc, axis=-1, keepdims=True) + EPS)
    nhat = xc * r
    vn = (nhat * gv_ref[...]).astype(BF16)
    for g in range(groups):
        sl = slice(g * LANE, (g + 1) * LANE)
        mix_sc[:, sl] = jnp.dot(wm_ref[g], vn[:, sl], preferred_element_type=F32) + bs_ref[g]
    return gu, dgu, dgv, nhat, r, vn, mix_sc[...]


def _mix_fwd(proj, o, wm, bs, g_v, d, name):
    t = proj.shape[0]
    groups = d // LANE

    def body(u_ref, v_ref, ga_ref, gb_ref, o_ref, wm_ref, bs_ref, gv_ref, out_ref, mix_sc):
        gu, _, _, _, _, _, mixed = _sgu_forward(u_ref, v_ref, gv_ref, wm_ref, bs_ref, mix_sc, groups)
        out_ref[...] = (_sigmoid(ga_ref[...]) * (gu * mixed) + _sigmoid(gb_ref[...]) * o_ref[...]).astype(BF16)

    def colblk(k):
        return pl.BlockSpec((LANE, d), lambda i, k=k: (i, k))

    full3 = pl.BlockSpec((groups, LANE, LANE), lambda i: (0, 0, 0))
    return pl.pallas_call(
        body, grid=(t // LANE,),
        in_specs=[colblk(COL_U), colblk(COL_V), colblk(COL_GA), colblk(COL_GB), colblk(0), full3,
                  pl.BlockSpec((groups, LANE, 1), lambda i: (0, 0, 0)), pl.BlockSpec((1, d), lambda i: (0, 0))],
        out_specs=colblk(0),
        out_shape=jax.ShapeDtypeStruct((t, d), BF16),
        scratch_shapes=[pltpu.VMEM((LANE, d), F32)],
        compiler_params=_cparams(("parallel",)), name=name,
    )(proj, proj, proj, proj, o, wm, bs, g_v)


def _mix_bwd(dmerged, proj, o, wm, wm_t, bs, g_v, d, name):
    t = proj.shape[0]
    groups = d // LANE
    nt = t // LANE

    def body(dm_ref, u_ref, v_ref, ga_ref, gb_ref, o_ref, wm_ref, wmt_ref, bs_ref, gv_ref,
             duv_ref, dg_ref, do_ref, dws_ref, dbs_ref, dgv_ref, mix_sc, dvn_sc, gv_acc):
        i = pl.program_id(0)

        @pl.when(i == 0)
        def _():
            dws_ref[...] = jnp.zeros_like(dws_ref)
            dbs_ref[...] = jnp.zeros_like(dbs_ref)
            gv_acc[...] = jnp.zeros_like(gv_acc)

        gu, dgu, dgv, nhat, r, vn, mixed = _sgu_forward(u_ref, v_ref, gv_ref, wm_ref, bs_ref, mix_sc, groups)
        dm = dm_ref[...]
        sa = _sigmoid(ga_ref[...])
        sb = _sigmoid(gb_ref[...])
        ov = o_ref[...]
        y_a = gu * mixed
        dg_ref[:, 0:d] = (dm * y_a * sa * (1.0 - sa)).astype(BF16)
        dg_ref[:, d:2 * d] = (dm * ov * sb * (1.0 - sb)).astype(BF16)
        do_ref[...] = dm * sb
        dy_a = dm * sa
        duv_ref[:, 0:d] = (dy_a * mixed * dgu).astype(BF16)
        dmixed = dy_a * gu
        dmixed_b = dmixed.astype(BF16)
        for g in range(groups):
            sl = slice(g * LANE, (g + 1) * LANE)
            dvn_sc[:, sl] = jnp.dot(wmt_ref[g], dmixed_b[:, sl], preferred_element_type=F32)
            dws_ref[g] += _nt_dot(dmixed_b[:, sl], vn[:, sl])
            dbs_ref[g] += jnp.sum(dmixed[:, sl], axis=-1, keepdims=True)
        dvn = dvn_sc[...]
        gv_acc[...] += _sum8(dvn * nhat)
        dn = dvn * gv_ref[...]
        dgelu = r * (dn - jnp.mean(dn, axis=-1, keepdims=True) - nhat * jnp.mean(dn * nhat, axis=-1, keepdims=True))
        duv_ref[:, d:2 * d] = (dgelu * dgv).astype(BF16)

        @pl.when(i == nt - 1)
        def _():
            dgv_ref[...] = jnp.sum(gv_acc[...], axis=0, keepdims=True)
            rr = lax.broadcasted_iota(jnp.int32, (LANE, LANE), 0)
            cl = lax.broadcasted_iota(jnp.int32, (LANE, LANE), 1)
            for g in range(groups):
                dws_ref[g] = jnp.where(rr >= cl, dws_ref[g], 0.0)

    def colblk(k):
        return pl.BlockSpec((LANE, d), lambda i, k=k: (i, k))

    full3 = pl.BlockSpec((groups, LANE, LANE), lambda i: (0, 0, 0))
    col3 = pl.BlockSpec((groups, LANE, 1), lambda i: (0, 0, 0))
    vec = pl.BlockSpec((1, d), lambda i: (0, 0))
    two = pl.BlockSpec((LANE, 2 * d), lambda i: (i, 0))
    return pl.pallas_call(
        body, grid=(nt,),
        in_specs=[colblk(0), colblk(COL_U), colblk(COL_V), colblk(COL_GA), colblk(COL_GB), colblk(0), full3, full3, col3, vec],
        out_specs=[two, two, colblk(0), full3, col3, vec],
        out_shape=[jax.ShapeDtypeStruct((t, 2 * d), BF16), jax.ShapeDtypeStruct((t, 2 * d), BF16), jax.ShapeDtypeStruct((t, d), F32),
                   jax.ShapeDtypeStruct((groups, LANE, LANE), F32), jax.ShapeDtypeStruct((groups, LANE, 1), F32),
                   jax.ShapeDtypeStruct((1, d), F32)],
        scratch_shapes=[pltpu.VMEM((LANE, d), F32), pltpu.VMEM((LANE, d), F32), pltpu.VMEM((SUBLANE, d), F32)],
        compiler_params=_cparams(("arbitrary",)), name=name,
    )(dmerged, proj, proj, proj, proj, o, wm, wm_t, bs, g_v)


def _adam_math(w, g, m, v):
    nm = ADAM_B1 * m + (1.0 - ADAM_B1) * g
    nv = ADAM_B2 * v + (1.0 - ADAM_B2) * (g * g)
    delta = -ADAM_LR * ((nm * ADAM_C1) / (jnp.sqrt(nv * ADAM_C2) + ADAM_EPS) + ADAM_WD * w)
    return delta, nm, nv


def _adamw(w, g, m, v, name):
    r, c = w.shape
    cap = max(SUBLANE, (2 * 1024 * 1024) // (4 * c) // SUBLANE * SUBLANE)
    tr = _rows(r, cap)

    def body(w_ref, g_ref, m_ref, v_ref, d_ref, nm_ref, nv_ref):
        d_ref[...], nm_ref[...], nv_ref[...] = _adam_math(w_ref[...], g_ref[...], m_ref[...], v_ref[...])

    blk = pl.BlockSpec((tr, c), lambda i: (i, 0))
    return pl.pallas_call(
        body, grid=(r // tr,), in_specs=[blk] * 4, out_specs=[blk] * 3,
        out_shape=[jax.ShapeDtypeStruct((r, c), F32)] * 3, compiler_params=_cparams(("parallel",)), name=name,
    )(w, g, m, v)


def _adamw_layers(w, g0, g1, m, v, name):
    _, r, c = w.shape
    cap = max(SUBLANE, (1024 * 1024) // (4 * c) // SUBLANE * SUBLANE)
    tr = _rows(r, cap)

    def body(w_ref, g0_ref, g1_ref, m_ref, v_ref, g_ref, d_ref, nm_ref, nv_ref):
        gg = jnp.where(pl.program_id(0) == 0, g0_ref[...], g1_ref[...])
        g_ref[0] = gg
        d_ref[0], nm_ref[0], nv_ref[0] = _adam_math(w_ref[0], gg, m_ref[0], v_ref[0])

    lay = pl.BlockSpec((1, tr, c), lambda l, i: (l, i, 0))

    def gspec(l0):
        return pl.BlockSpec((tr, c), lambda l, i: (jnp.where(l == l0, i, 0), 0))

    return pl.pallas_call(
        body, grid=(2, r // tr), in_specs=[lay, gspec(0), gspec(1), lay, lay], out_specs=[lay] * 4,
        out_shape=[jax.ShapeDtypeStruct((2, r, c), F32)] * 4, compiler_params=_cparams(("arbitrary", "arbitrary")), name=name,
    )(w, g0, g1, m, v)


def _adamw_interleaved(w, g0, g1, m, v, name):
    r, _, c = w.shape
    tr = 128

    def body(w_ref, g0_ref, g1_ref, m_ref, v_ref, g_ref, d_ref, nm_ref, nv_ref):
        for l, gl in enumerate((g0_ref, g1_ref)):
            gg = gl[...]
            g_ref[:, l, :] = gg
            d_ref[:, l, :], nm_ref[:, l, :], nv_ref[:, l, :] = _adam_math(w_ref[:, l, :], gg, m_ref[:, l, :], v_ref[:, l, :])

    lay = pl.BlockSpec((tr, 2, c), lambda i: (i, 0, 0))
    flat = pl.BlockSpec((tr, c), lambda i: (i, 0))
    return pl.pallas_call(
        body, grid=(pl.cdiv(r, tr),), in_specs=[lay, flat, flat, lay, lay], out_specs=[lay] * 4,
        out_shape=[jax.ShapeDtypeStruct((r, 2, c), F32)] * 4, compiler_params=_cparams(("parallel",)), name=name,
    )(w, g0, g1, m, v)


def _add_half(p4, recv, c_idx, name):
    _, r, c = p4.shape
    hw = c // 2
    tr = 256 if r % 256 == 0 else r

    def body(c_ref, a_ref, b_ref, o_ref):
        o_ref[...] = (a_ref[...].astype(F32) + b_ref[...].astype(F32)).astype(BF16)

    return pl.pallas_call(
        body,
        grid_spec=pltpu.PrefetchScalarGridSpec(
            num_scalar_prefetch=1, grid=(N_CHIPS, pl.cdiv(r, tr)),
            in_specs=[pl.BlockSpec((1, tr, hw), lambda s, i, cr: (s, i, cr[0])), pl.BlockSpec((1, tr, hw), lambda s, i, cr: (s, i, 0))],
            out_specs=pl.BlockSpec((1, tr, hw), lambda s, i, cr: (s, i, 0)),
        ),
        out_shape=jax.ShapeDtypeStruct((N_CHIPS, r, hw), BF16), compiler_params=_cparams(("parallel", "parallel")), name=name,
    )(c_idx, p4, recv)


def _sum_slots(x, own, sel, name, out_cols=None):
    s, r, c = x.shape
    tr = 128 if r % 128 == 0 else r

    def body(sel_ref, x_ref, own_ref, o_ref):
        mine = own_ref[0].astype(F32)
        acc = jnp.zeros((tr, c), F32)
        for k in range(s):
            acc = acc + jnp.where(sel_ref[0] == k, mine, x_ref[k].astype(F32))
        o_ref[...] = acc

    return pl.pallas_call(
        body,
        grid_spec=pltpu.PrefetchScalarGridSpec(
            num_scalar_prefetch=1, grid=(pl.cdiv(r, tr),),
            in_specs=[pl.BlockSpec((s, tr, c), lambda i, sr: (0, i, 0)), pl.BlockSpec((1, tr, c), lambda i, sr: (sr[1], i, 0))],
            out_specs=pl.BlockSpec((tr, c), lambda i, sr: (i, sr[2])),
        ),
        out_shape=jax.ShapeDtypeStruct((r, out_cols or c), F32), compiler_params=_cparams(("parallel",)), name=name,
    )(sel, x, own)


def _half_cols(width, hc):
    hw = width // 2
    assert hw % LANE == 0
    return pl.ds(pl.multiple_of(hc * hw, LANE), hw)


def _remote(src, dst, ssem, rsem, k, to):
    return pltpu.make_async_remote_copy(src_ref=src, dst_ref=dst, send_sem=ssem.at[k], recv_sem=rsem.at[k], device_id=to,
                                        device_id_type=MESH)


def _gather_job(bufs, mid_at=0.5):
    def part(o, a, slot, hc):
        return o[a].at[slot, :, _half_cols(bufs[a].shape[2], hc)]

    def first(ins, o, fresh, ssem, rsem):
        x, y, c, chips = _place()
        for a in range(len(bufs)):
            mine = part(o, a, 2 * x + y, c)
            for j, chip in enumerate(chips):
                _remote(mine, mine, ssem, rsem, 6 * a + j, (chip[0], chip[1], c)).start()

    def mid(ins, o, fresh, ssem, rsem):
        x, y, c, chips = _place()
        for a in range(len(bufs)):
            for j, chip in enumerate(chips):
                got = part(o, a, 2 * chip[0] + chip[1], c)
                _remote(got, got, ssem, rsem, 6 * a + j, (x, y, c)).wait_recv()
                _remote(got, got, ssem, rsem, 6 * a + 3 + j, (x, y, 1 - c)).start()

    def last(ins, o, fresh, ssem, rsem):
        x, y, c, chips = _place()
        for a in range(len(bufs)):
            for j, chip in enumerate(chips):
                got = part(o, a, 2 * chip[0] + chip[1], 1 - c)
                _remote(got, got, ssem, rsem, 6 * a + 3 + j, (x, y, c)).wait_recv()
        for a in range(len(bufs)):
            mine = part(o, a, 2 * x + y, c)
            for j, chip in enumerate(chips):
                _remote(mine, mine, ssem, rsem, 6 * a + j, (x, y, c)).wait_send()
                passed = part(o, a, 2 * chip[0] + chip[1], c)
                _remote(passed, passed, ssem, rsem, 6 * a + 3 + j, (x, y, c)).wait_send()

    return _Job([], bufs, [], 6 * len(bufs), first, mid, last, mid_at)


def _swap_job(p4s):
    def pairs(ins, fresh, c):
        return [(a, s, ins[a].at[s, :, _half_cols(p4s[a].shape[2], 1 - c)], fresh[a].at[s])
                for a in range(len(p4s)) for s in range(N_CHIPS)]

    def first(ins, inout, fresh, ssem, rsem):
        x, y, c, _ = _place()
        for a, s, src, dst in pairs(ins, fresh, c):
            _remote(src, dst, ssem, rsem, N_CHIPS * a + s, (x, y, 1 - c)).start()

    def last(ins, inout, fresh, ssem, rsem):
        x, y, c, _ = _place()
        for a, s, src, dst in pairs(ins, fresh, c):
            _remote(src, dst, ssem, rsem, N_CHIPS * a + s, (x, y, 1 - c)).wait()

    fresh = [jax.ShapeDtypeStruct(p.shape[:2] + (p.shape[2] // 2,), p.dtype) for p in p4s]
    return _Job(p4s, [], fresh, N_CHIPS * len(p4s), first, None, last)


def _scatter_job(parts):
    def first(ins, inout, fresh, ssem, rsem):
        x, y, c, chips = _place()
        for a in range(len(parts)):
            for j, chip in enumerate(chips):
                _remote(ins[a].at[2 * chip[0] + chip[1]], fresh[a].at[2 * x + y], ssem, rsem, 3 * a + j, (chip[0], chip[1], c)).start()

    def last(ins, inout, fresh, ssem, rsem):
        x, y, c, chips = _place()
        for a in range(len(parts)):
            for j, chip in enumerate(chips):
                slot = 2 * chip[0] + chip[1]
                _remote(ins[a].at[slot], fresh[a].at[slot], ssem, rsem, 3 * a + j, (x, y, c)).wait()

    return _Job(parts, [], [jax.ShapeDtypeStruct(p.shape, p.dtype) for p in parts], 3 * len(parts), first, None, last)


def _share_job(gs):
    def halves(o, a, c):
        width = gs[a].shape[1]
        return o[a].at[:, _half_cols(width, c)], o[a].at[:, _half_cols(width, 1 - c)]

    def first(ins, o, fresh, ssem, rsem):
        x, y, c, _ = _place()
        for a in range(len(gs)):
            mine, _ = halves(o, a, c)
            _remote(mine, mine, ssem, rsem, a, (x, y, 1 - c)).start()

    def last(ins, o, fresh, ssem, rsem):
        x, y, c, _ = _place()
        for a in range(len(gs)):
            mine, theirs = halves(o, a, c)
            _remote(mine, theirs, ssem, rsem, a, (x, y, 1 - c)).wait()

    return _Job([], gs, [], len(gs), first, None, last)


def _gather_all_job(buf):
    def peers():
        x, y, c, _ = _place()
        flips = [(fx, fy, fc) for fx in (0, 1) for fy in (0, 1) for fc in (0, 1)][1:]
        return (x, y, c), [((1 - x) if fx else x, (1 - y) if fy else y, (1 - c) if fc else c) for fx, fy, fc in flips]

    def first(ins, inout, fresh, ssem, rsem):
        (x, y, c), others = peers()
        for k, peer in enumerate(others):
            _remote(ins[0], fresh[0].at[4 * x + 2 * y + c], ssem, rsem, k, peer).start()

    def last(ins, inout, fresh, ssem, rsem):
        me, others = peers()
        for k, peer in enumerate(others):
            _remote(ins[0], fresh[0].at[4 * peer[0] + 2 * peer[1] + peer[2]], ssem, rsem, k, me).wait()

    return _Job([buf], [], [jax.ShapeDtypeStruct((N_DEV,) + buf.shape, buf.dtype)], N_DEV - 1, first, None, last)


class _SemView:
    def __init__(self, sems, off):
        self.sems, self.off = sems, off

    @property
    def at(self):
        return self

    def __getitem__(self, k):
        return self.sems.at[k + self.off]


def _join(jobs):
    spans, pos = [], [0, 0, 0, 0]
    for j in jobs:
        nxt = [pos[0] + len(j.ins), pos[1] + len(j.inout), pos[2] + len(j.fresh), pos[3] + j.nsem]
        spans.append((pos, nxt))
        pos = nxt

    def hook(which):
        fns = [getattr(j, which) for j in jobs]
        if all(f is None for f in fns):
            return None

        def run(ins, inout, fresh, ssem, rsem):
            for fn, (lo, hi) in zip(fns, spans):
                if fn is not None:
                    fn(ins[lo[0]:hi[0]], inout[lo[1]:hi[1]], fresh[lo[2]:hi[2]], _SemView(ssem, lo[3]), _SemView(rsem, lo[3]))

        return run

    mids = [j.mid_at for j in jobs if j.mid is not None]
    joined = _Job([a for j in jobs for a in j.ins], [a for j in jobs for a in j.inout], [a for j in jobs for a in j.fresh],
                  pos[3], hook("first"), hook("mid"), hook("last"), max(mids) if mids else 0.5)
    n_io = pos[1]

    def split(moved):
        return [list(moved[lo[1]:hi[1]]) + list(moved[n_io + lo[2]:n_io + hi[2]]) for lo, hi in spans]

    return joined, split


def _carrying(stages, call):
    stages = [s for s in stages if s is not None]
    if not stages:
        return call(None)
    job, split = _join([s[0] for s in stages])
    out, moved = call(job)
    for (_, done), part in zip(stages, split(moved)):
        done(part)
    return out


def _layer_forward(x, h, w_in_t, rest, sm, d, stages=None):
    stages = stages or {}
    proj = _carrying([stages.get("proj")], lambda job: _matmul(h, w_in_t, "nt", F32, "proj_fwd", n=7 * d, tn_cap=1792, job=job))
    f_t = _matmul(w_in_t[7 * d:], h, "nt", F32, "forget_fwd", tn_cap=1024)
    c_t = _fox_prep(f_t, sm["b_f"], "fox_prep")
    o, lse = _carrying([stages.get("attn")], lambda job: _attn_fwd(proj, c_t, d, "attn_fwd", job=job))
    wts = rest()
    merged = _mix_fwd(proj, o, sm["wm"], sm["bs"], sm["g_v"], d, "mix_fwd")
    z = _matmul(merged, wts["w_out"], "nn", F32, "out_fwd")
    x1, h2 = _norm_fwd(x, z, sm["g_post"], sm["g_fpre"], "norm_mid")
    a = _carrying([stages.get("gate")], lambda job: _matmul(h2, wts["w_g_t"], "nt", F32, "gate_fwd", tn_cap=1408, job=job))
    b = _carrying([stages.get("up")], lambda job: _matmul(h2, wts["w_u_t"], "nt", F32, "up_fwd", tn_cap=1408, job=job))
    mm = _swiglu_fwd(a, b, "swiglu_fwd")
    z2 = _carrying([stages.get("down")], lambda job: _matmul(mm, wts["w_d"], "nn", F32, "down_fwd", job=job))
    return dict(x=x, h=h, proj=proj, f_t=f_t, c_t=c_t, o=o, lse=lse, merged=merged, z=z, x1=x1,
                h2=h2, a=a, b=b, mm=mm, z2=z2)


class _GradExchange:
    def __init__(self, pay, keys, c_idx, chip):
        self.keys = list(keys)
        self.p4 = [pay[k].reshape(N_CHIPS, pay[k].shape[1] // N_CHIPS, pay[k].shape[2]) for k in self.keys]
        self.c_idx = c_idx
        self.sel = jnp.stack([chip, chip, c_idx[0]]).astype(jnp.int32)
        self.done = 0

    def _after_swap(self, landed):
        self.parts = [_add_half(p, r, self.c_idx, "add_sibling") for p, r in zip(self.p4, landed)]
        self.done = 1

    def _after_scatter(self, landed):
        self.g = [_sum_slots(got, sent, self.sel, "sum_chips", out_cols=p.shape[2])
                  for got, sent, p in zip(landed, self.parts, self.p4)]
        self.done = 2

    def _after_share(self, moved):
        self.g = list(moved)
        self.done = 3

    def stage(self):
        if self.done == 0:
            return _swap_job(self.p4), self._after_swap
        if self.done == 1:
            return _scatter_job(self.parts), self._after_scatter
        if self.done == 2:
            return _share_job(self.g), self._after_share
        return None

    def run(self):
        for name in ("swap_grads", "scatter_grads", "share_grads")[self.done:]:
            job, done = self.stage()
            done(_run_job(job, name))

    def grads(self):
        return dict(zip(self.keys, self.g))


EARLY_KEYS = ("w_d", "w_g", "w_u", "w_out")


def _layer_backward(dz2, dx2, sv, wts, sm, d, c_idx, chip, carried=(), split_own=False, small_stage=None):
    t = dx2.shape[0]
    heads = d // LANE
    ff = wts["w_d"].shape[0]
    in_w = 7 * d + heads
    g, pay = {}, {}
    carried = list(carried)

    def payload(key, a, b, rows, row0, name, extra=()):
        def call(job):
            return _matmul(a, b, "tn", BF16, name, slab=((1, rows, d), 0, row0), into=pay.get(key), job=job, tm_cap=1408,
                           tn_cap=1024, tk_cap=1024)
        pay[key] = _carrying(list(extra), call)

    def stages():
        return [ex.stage() for ex in carried]

    dm = _carrying(stages(), lambda job: _matmul(dz2, wts["w_d"], "nt", F32, "down_bwd_x", tn_cap=1408, tk_cap=1024, job=job))
    payload("w_d", sv["mm"], dz2, ff, 0, "down_bwd_w")
    da, db = _swiglu_bwd(sv["a"], sv["b"], dm, "swiglu_bwd")
    dh2 = _matmul_pieces([(da, wts["w_g_t"], 0), (db, wts["w_u_t"], 0)], None, "gu_bwd_x", tk=_tile(ff, 1408), tm_cap=1024)
    payload("w_g", da, sv["h2"], ff, 0, "gate_bwd_w")
    payload("w_u", db, sv["h2"], ff, 0, "up_bwd_w")
    dx1, dz, g["g_fpre"], g["g_post"] = _norm_bwd(dx2, (dh2, sv["x1"], sm["g_fpre"]), (sv["z"], sm["g_post"]), "norm_bwd_mid")
    dmerged = _matmul(dz, wts["w_out"], "nt", F32, "out_bwd_x", tk_cap=1024)
    payload("w_out", sv["merged"], dz, d, 0, "out_bwd_w")
    d_uv, d_g, do, g["w_s"], g["b_s"], g["g_v"] = _mix_bwd(dmerged, sv["proj"], sv["o"], sm["wm"], sm["wm_t"], sm["bs"],
                                                         sm["g_v"], d, "mix_bwd")
    own = []
    if split_own:
        own.append(_GradExchange(pay, EARLY_KEYS, c_idx, chip))
        carried.append(own[0])
    extra = [small_stage(g)] if small_stage is not None else []
    attn_args = (sv["proj"], do, sv["o"], sv["lse"], sv["c_t"], d)
    dq, dk, dv, dc_q, dc_k = _carrying(stages() + extra, lambda job: _attn_bwd(*attn_args, "attn_bwd", job=job))
    df_t, g["b_f"] = _fox_bwd(dc_q, dc_k, sv["f_t"], sm["b_f"], "fox_bwd")
    df_b = df_t.astype(BF16)
    dh_f = _matmul(df_b, wts["w_in_t"][7 * d:], "tn", F32, "forget_bwd_x")
    pieces = [(d_uv, COL_U), (dq, COL_Q), (dk, COL_K), (dv, COL_VA), (d_g, COL_GA)]
    ops = [(p, wts["w_in_t"], col * d) for p, col in pieces]
    dh = _carrying(stages(), lambda job: _matmul_pieces(ops, dh_f, "proj_bwd_x", job=job, tk=_tile(d, 1024), tm_cap=1024,
                                                        tn_cap=512))
    for p, col in pieces:
        payload("w_in", p, sv["h"], in_w, col * d, "proj_bwd_w", extra=stages())
    w_f_rows = _matmul(df_b, sv["h"], "nn", BF16, "forget_bwd_w", tk_cap=1024)
    pay["w_in"] = lax.dynamic_update_slice(pay["w_in"], w_f_rows[None], (0, 7 * d, 0))
    own.append(_GradExchange(pay, [k for k in ("w_in",) + EARLY_KEYS if not (split_own and k in EARLY_KEYS)], c_idx, chip))
    return dh, dx1, g, own


def _small_pack(parts):
    flat = jnp.concatenate([p.reshape(-1) for p in parts])
    n = flat.shape[0]
    pad = (-n) % (LANE * LANE)
    return jnp.pad(flat, (0, pad)).reshape(-1, LANE)


def kernel(x, mix_pre_g, w_in, b_forget, sgu_norm_g, w_spatial, b_spatial, w_out, mix_post_g, ffn_pre_g, w_gate, w_up, w_down, ffn_post_g, loss_target, m_mix_pre_g, m_w_in, m_b_forget, m_sgu_norm_g, m_w_spatial, m_b_spatial, m_w_out, m_mix_post_g, m_ffn_pre_g, m_w_gate, m_w_up, m_w_down, m_ffn_post_g, v_mix_pre_g, v_w_in, v_b_forget, v_sgu_norm_g, v_w_spatial, v_b_spatial, v_w_out, v_mix_post_g, v_ffn_pre_g, v_w_gate, v_w_up, v_w_down, v_ffn_post_g):
    depth, d = mix_pre_g.shape
    assert depth == 2, "core c of a chip owns layer c"
    heads = d // LANE
    t = x.shape[1]
    ff = w_down.shape[1] * N_CHIPS
    in_w = w_in.shape[2] * N_CHIPS
    assert in_w == 7 * d + heads
    xs = x.reshape(t, d)
    target = loss_target.reshape(t, d)
    c_idx = lax.axis_index("c").astype(jnp.int32).reshape(1)
    chip = 2 * lax.axis_index("x") + lax.axis_index("y")
    dev = 2 * chip + lax.axis_index("c")

    def in_view(w):
        return jnp.transpose(w, (2, 0, 1))

    def gu_view(w):
        return jnp.transpose(w, (0, 2, 1))

    own = [jnp.transpose(in_view(w_in).astype(BF16), (1, 0, 2)), w_out.astype(BF16), gu_view(w_gate).astype(BF16),
           gu_view(w_up).astype(BF16), w_down.astype(BF16)]
    bufs = [[lax.dynamic_update_slice(jnp.zeros((N_CHIPS,) + o.shape[1:], BF16), o[l][None], (chip, 0, 0)) for o in own]
            for l in range(depth)]
    first_in = _run_job(_gather_job([bufs[0][0]]), "gather_first")[0]

    def weights(g_in, g_out, g_g, g_u, g_d):
        return dict(w_in_t=g_in.reshape(in_w, d), w_out=g_out.reshape(d, d), w_g_t=g_g.reshape(ff, d),
                    w_u_t=g_u.reshape(ff, d), w_d=g_d.reshape(ff, d))

    tril = jnp.tril(jnp.ones((LANE, LANE), bool))
    smalls = []
    for l in range(depth):
        wm = jnp.where(tril[None], w_spatial[l], 0.0).astype(BF16)
        smalls.append(dict(
            b_f=b_forget[l].reshape(heads, 1), wm=wm, wm_t=jnp.swapaxes(wm, 1, 2), bs=b_spatial[l].reshape(heads, LANE, 1),
            g_v=sgu_norm_g[l].reshape(1, d), g_pre=mix_pre_g[l].reshape(1, d), g_post=mix_post_g[l].reshape(1, d),
            g_fpre=ffn_pre_g[l].reshape(1, d), g_fpost=ffn_post_g[l].reshape(1, d)))

    wts, later = [], {}

    def keep(key):
        def done(moved):
            later[key] = list(moved)
        return done

    def rest_first():
        wts.append(weights(first_in, *later["rest0"]))
        return wts[0]

    stages = dict(proj=(_gather_job(bufs[0][1:], mid_at=1.0), keep("rest0")),
                  attn=(_gather_job(bufs[1][0:2], mid_at=0.7), keep("in_out1")),
                  gate=(_gather_job(bufs[1][2:3], mid_at=1.0), keep("g1")), up=(_gather_job(bufs[1][3:4], mid_at=1.0), keep("u1")),
                  down=(_gather_job(bufs[1][4:5], mid_at=1.0), keep("d1")))
    h = _norm_fwd(xs, None, None, smalls[0]["g_pre"], "norm_first")
    saved = [_layer_forward(xs, h, first_in.reshape(in_w, d), rest_first, smalls[0], d, stages)]
    wts.append(weights(*later["in_out1"], later["g1"][0], later["u1"][0], later["d1"][0]))
    for l in range(1, depth):
        xin, h = _norm_fwd(saved[l - 1]["x1"], saved[l - 1]["z2"], smalls[l - 1]["g_fpost"], smalls[l]["g_pre"], "norm_out")
        saved.append(_layer_forward(xin, h, wts[l]["w_in_t"], lambda l=l: wts[l], smalls[l], d))
    y, _ = _norm_fwd(saved[-1]["x1"], saved[-1]["z2"], smalls[-1]["g_fpost"], smalls[-1]["g_pre"], "norm_out")
    dy, loss_part = _loss_grad(y, target, "loss")
    loss = lax.psum(jnp.sum(loss_part), ("x", "y", "c"))

    small_shapes = dict(g_pre=(d,), b_f=(heads,), g_v=(d,), w_s=w_spatial.shape[1:], b_s=b_spatial.shape[1:], g_post=(d,),
                        g_fpre=(d,), g_fpost=(d,))
    late_entries = [(0, "g_pre"), (0, "b_f")]
    early_entries = [(l, n) for l in reversed(range(depth)) for n in small_shapes if (l, n) not in late_entries]
    dev_sel = jnp.stack([dev, jnp.zeros_like(dev), jnp.zeros_like(dev)]).astype(jnp.int32)
    small_sum = {}

    def small_exchange(entries, values):
        packed = _small_pack([values[e].reshape(-1) for e in entries])

        def done(moved):
            total = _sum_slots(moved[0], packed[None], dev_sel, "sum_small").reshape(-1)
            off = 0
            for e in entries:
                n = math.prod(small_shapes[e[1]])
                small_sum[e] = total[off:off + n].reshape(small_shapes[e[1]])
                off += n

        return _gather_all_job(packed), done

    grads = [None] * depth
    exchanges = [None] * depth
    dx2 = dy
    dz2, g_fpost = _norm_bwd(dx2, None, (saved[depth - 1]["z2"], smalls[depth - 1]["g_fpost"]), "norm_bwd_top")
    for l in reversed(range(depth)):
        last = l == 0

        def small_stage(g, l=l, g_fpost=g_fpost):
            known = {(k, n): grads[k][n] for k in range(l + 1, depth) for n in small_shapes}
            known.update({(l, n): g[n] for n in g})
            known[(l, "g_fpost")] = g_fpost
            return small_exchange(early_entries, known)

        carried = [ex for k in range(l + 1, depth) for ex in exchanges[k]]
        dh, dx1, g, exchanges[l] = _layer_backward(dz2, dx2, saved[l], wts[l], smalls[l], d, c_idx, chip, carried=carried,
                                                    split_own=last, small_stage=small_stage if last else None)
        g["g_fpost"] = g_fpost
        if l > 0:
            dx2, dz2, g["g_pre"], g_fpost = _norm_bwd(dx1, (dh, saved[l]["x"], smalls[l]["g_pre"]),
                                                       (saved[l - 1]["z2"], smalls[l - 1]["g_fpost"]), "norm_bwd_between")
        else:
            grad_x, g["g_pre"] = _norm_bwd(dx1, (dh, saved[l]["x"], smalls[l]["g_pre"]), None, "norm_bwd_bottom")
        grads[l] = g
    job, done = small_exchange(late_entries, {(0, n): grads[0][n] for n in ("g_pre", "b_f")})
    done(_run_job(job, "gather_small"))
    big = [{} for _ in range(depth)]
    for l in range(depth):
        for ex in exchanges[l]:
            ex.run()
            big[l].update(ex.grads())
    small_grads = {n: jnp.stack([small_sum[(l, n)] for l in range(depth)]) for n in small_shapes}

    def adam_small(w, g, m, v):
        shp = w.shape
        if w.ndim >= 3 and shp[-1] >= LANE:
            two = (math.prod(shp[:-1]), shp[-1])
        else:
            two = (1, math.prod(shp)) if math.prod(shp) < LANE else (math.prod(shp) // LANE, LANE)
        outs = _adamw(w.reshape(two), g.reshape(two), m.reshape(two), v.reshape(two), "adamw")
        return [g] + [o.reshape(shp) for o in outs]

    def adam_in(w, m, v):
        outs = _adamw_interleaved(in_view(w), big[0]["w_in"], big[1]["w_in"], in_view(m), in_view(v), "adamw_in")
        return [jnp.transpose(o, (1, 2, 0)) for o in outs]

    def adam_gu(k, w, m, v):
        outs = _adamw_layers(gu_view(w), big[0][k], big[1][k], gu_view(m), gu_view(v), "adamw_layers")
        return [jnp.transpose(o, (0, 2, 1)) for o in outs]

    def adam_rows(k, w, m, v):
        return _adamw_layers(w, big[0][k], big[1][k], m, v, "adamw_layers")

    results = [
        adam_small(mix_pre_g, small_grads["g_pre"], m_mix_pre_g, v_mix_pre_g),
        adam_in(w_in, m_w_in, v_w_in),
        adam_small(b_forget, small_grads["b_f"], m_b_forget, v_b_forget),
        adam_small(sgu_norm_g, small_grads["g_v"], m_sgu_norm_g, v_sgu_norm_g),
        adam_small(w_spatial, small_grads["w_s"], m_w_spatial, v_w_spatial),
        adam_small(b_spatial, small_grads["b_s"], m_b_spatial, v_b_spatial),
        adam_rows("w_out", w_out, m_w_out, v_w_out),
        adam_small(mix_post_g, small_grads["g_post"], m_mix_post_g, v_mix_post_g),
        adam_small(ffn_pre_g, small_grads["g_fpre"], m_ffn_pre_g, v_ffn_pre_g),
        adam_gu("w_g", w_gate, m_w_gate, v_w_gate),
        adam_gu("w_u", w_up, m_w_up, v_w_up),
        adam_rows("w_d", w_down, m_w_down, v_w_down),
        adam_small(ffn_post_g, small_grads["g_fpost"], m_ffn_post_g, v_ffn_post_g),
    ]
    gs, deltas, new_ms, new_vs = zip(*results)
    return (loss, grad_x.reshape(x.shape), *gs, *deltas, *new_ms, *new_vs)
```

```python
import functools
import math

import jax
import jax.numpy as jnp
from jax import lax
from jax.experimental import pallas as pl
from jax.experimental.pallas import tpu as pltpu

F32 = jnp.float32
BF16 = jnp.bfloat16

EPS = 1e-6
LANE = 128
SUBLANE = 8
N_CHIPS = 4
N_DEV = 8
VMEM_LIMIT = 48 * 1024 * 1024
MESH = pl.DeviceIdType.MESH

ADAM_LR = 0.001
ADAM_B1 = 0.9
ADAM_B2 = 0.999
ADAM_EPS = 1e-08
ADAM_WD = 0.01
ADAM_STEP = 10
ADAM_C1 = 1.0 / (1.0 - ADAM_B1 ** ADAM_STEP)
ADAM_C2 = 1.0 / (1.0 - ADAM_B2 ** ADAM_STEP)

GELU_K = math.sqrt(2.0 / math.pi)
GELU_A = 0.044715
NEG = -1e30
LOG2E = 1.4426950408889634
LN2 = 0.6931471805599453

COL_U, COL_V, COL_Q, COL_K, COL_VA, COL_GA, COL_GB, COL_F = range(8)


def _cparams(sem=None):
    return pltpu.CompilerParams(dimension_semantics=sem, vmem_limit_bytes=VMEM_LIMIT)


def _tile(n, cap):
    best = None
    for t in range(LANE, min(n, cap) + 1, LANE):
        if n % t == 0:
            best = t
    return best if best is not None else n


def _rows(n, cap):
    best = None
    for t in range(SUBLANE, min(n, cap) + 1, SUBLANE):
        if n % t == 0:
            best = t
    return best if best is not None else n


def _gelu_and_grad(x):
    x2 = x * x
    t = jnp.tanh(GELU_K * (x + GELU_A * x2 * x))
    g = 0.5 * x * (1.0 + t)
    dg = 0.5 * (1.0 + t) + 0.5 * x * (1.0 - t * t) * (GELU_K * (1.0 + 3.0 * GELU_A * x2))
    return g, dg


def _sigmoid(x):
    return 1.0 / (1.0 + jnp.exp(-x))


def _sum8(v):
    n, d = v.shape
    return v.reshape(n // SUBLANE, SUBLANE, d).sum(axis=0)


def _nt_dot(a, b):
    return lax.dot_general(a, b, (((1,), (1,)), ((), ())), preferred_element_type=F32)


_HBM = pl.BlockSpec(memory_space=pl.ANY)


def _place():
    x, y, c = lax.axis_index("x"), lax.axis_index("y"), lax.axis_index("c")
    chips = [(1 - x, y), (x, 1 - y), (1 - x, 1 - y)]
    return x, y, c, chips


class _Job:
    def __init__(self, ins, inout, fresh, nsem, first, mid, last, mid_at=0.5):
        self.ins, self.inout, self.fresh, self.nsem = list(ins), list(inout), list(fresh), nsem
        self.first, self.mid, self.last, self.mid_at = first, mid, last, mid_at


def _call(body, *, grid, in_specs, out_specs, out_shape, scratch_shapes, dims, name, args, aliases=None, job=None):
    single = not isinstance(out_shape, (list, tuple))
    out_specs = [out_specs] if single else list(out_specs)
    out_shape = [out_shape] if single else list(out_shape)
    aliases = dict(aliases or {})
    if job is None:
        outs = pl.pallas_call(body, grid=grid, in_specs=in_specs, out_specs=out_specs, out_shape=out_shape,
                              scratch_shapes=scratch_shapes, input_output_aliases=aliases, compiler_params=_cparams(dims),
                              name=name)(*args)
        return (outs[0] if single else outs), []
    n_in, n_out, n_scr = len(args), len(out_shape), len(scratch_shapes)
    n_ji, n_jio, n_jf = len(job.ins), len(job.inout), len(job.fresh)
    total = math.prod(grid)

    def wrapped(*refs):
        host_in = refs[:n_in]
        pos = n_in
        j_ins = refs[pos:pos + n_ji]
        pos += n_ji + n_jio
        host_out = refs[pos:pos + n_out]
        pos += n_out
        j_inout = refs[pos:pos + n_jio]
        pos += n_jio
        j_fresh = refs[pos:pos + n_jf]
        pos += n_jf
        host_scr = refs[pos:pos + n_scr]
        ssem, rsem = refs[pos + n_scr:]
        flat = 0
        for ax, size in enumerate(grid):
            flat = flat * size + pl.program_id(ax)

        def hook(fn, at):
            if fn is not None:
                @pl.when(flat == at)
                def _():
                    fn(j_ins, j_inout, j_fresh, ssem, rsem)

        hook(job.first, 0)
        body(*host_in, *host_out, *host_scr)
        hook(job.mid, min(int(total * job.mid_at), total - 1))
        hook(job.last, total - 1)

    for k in range(n_jio):
        aliases[n_in + n_ji + k] = n_out + k
    outs = pl.pallas_call(
        wrapped, grid=grid,
        in_specs=list(in_specs) + [_HBM] * (n_ji + n_jio),
        out_specs=out_specs + [_HBM] * (n_jio + n_jf),
        out_shape=out_shape + [jax.ShapeDtypeStruct(b.shape, b.dtype) for b in job.inout] + list(job.fresh),
        scratch_shapes=list(scratch_shapes) + [pltpu.SemaphoreType.DMA((job.nsem,)), pltpu.SemaphoreType.DMA((job.nsem,))],
        input_output_aliases=aliases, compiler_params=_cparams(tuple("arbitrary" for _ in grid)), name=name,
    )(*args, *job.ins, *job.inout)
    host = outs[:n_out]
    return (host[0] if single else host), outs[n_out:]


def _run_job(job, name):
    n_ji, n_jio, n_jf = len(job.ins), len(job.inout), len(job.fresh)

    def body(*refs):
        j_ins = refs[:n_ji]
        pos = n_ji + n_jio
        j_inout = refs[pos:pos + n_jio]
        j_fresh = refs[pos + n_jio:pos + n_jio + n_jf]
        ssem, rsem = refs[pos + n_jio + n_jf:]
        for fn in (job.first, job.mid, job.last):
            if fn is not None:
                fn(j_ins, j_inout, j_fresh, ssem, rsem)

    return pl.pallas_call(
        body, in_specs=[_HBM] * (n_ji + n_jio), out_specs=[_HBM] * (n_jio + n_jf),
        out_shape=[jax.ShapeDtypeStruct(b.shape, b.dtype) for b in job.inout] + list(job.fresh),
        scratch_shapes=[pltpu.SemaphoreType.DMA((job.nsem,)), pltpu.SemaphoreType.DMA((job.nsem,))],
        input_output_aliases={n_ji + k: k for k in range(n_jio)}, name=name,
    )(*job.ins, *job.inout)


_DIMS ={"nn": ((1,), (0,)), "nt": ((1,), (1,)), "tn": ((0,), (0,))}


def _matmul(a, b, mode, out_dtype, name, n=None, slab=None, into=None, job=None, tm_cap=512, tn_cap=2048, tk_cap=1408):
    if mode == "nn":
        (m, k), (k2, nn_) = a.shape, b.shape
    elif mode == "nt":
        (m, k), (nn_, k2) = a.shape, b.shape
    else:
        (k, m), (k2, nn_) = a.shape, b.shape
    n = nn_ if n is None else n
    assert k == k2, (a.shape, b.shape, mode)
    tm, tn, tk = _tile(m, tm_cap), _tile(n, tn_cap), _tile(k, tk_cap)
    if slab is not None and slab[2]:
        tm = _tile(math.gcd(m, slab[2]), tm_cap)
    nk = k // tk
    if mode == "tn":
        a_spec = pl.BlockSpec((tk, tm), lambda j, i, kk, *_: (kk, i))
    else:
        a_spec = pl.BlockSpec((tm, tk), lambda j, i, kk, *_: (i, kk))
    if mode == "nt":
        b_spec = pl.BlockSpec((tn, tk), lambda j, i, kk, *_: (j, kk))
    else:
        b_spec = pl.BlockSpec((tk, tn), lambda j, i, kk, *_: (kk, j))
    dims = (_DIMS[mode], ((), ()))
    aliased = into is not None

    def body(*refs):
        a_ref, b_ref = refs[0], refs[1]
        o_ref = refs[3] if aliased else refs[2]
        p = lax.dot_general(a_ref[...], b_ref[...], dims, preferred_element_type=F32)
        if nk == 1:
            o_ref[...] = p.astype(out_dtype).reshape(o_ref.shape)
        else:
            acc = refs[-1]
            kk = pl.program_id(2)

            @pl.when(kk == 0)
            def _():
                acc[...] = p

            @pl.when(kk > 0)
            def _():
                acc[...] += p

            @pl.when(kk == nk - 1)
            def _():
                o_ref[...] = acc[...].astype(out_dtype).reshape(o_ref.shape)

    if slab is None:
        out_spec = pl.BlockSpec((tm, tn), lambda j, i, kk: (i, j))
        out_shape = jax.ShapeDtypeStruct((m, n), out_dtype)
    else:
        shape3, lead, row0 = slab
        assert row0 % tm == 0 and shape3[2] == n
        out_spec = pl.BlockSpec((1, tm, tn), lambda j, i, kk: (lead, row0 // tm + i, j))
        out_shape = jax.ShapeDtypeStruct(shape3, out_dtype)
    in_specs, args = [a_spec, b_spec], [a, b]
    if aliased:
        in_specs.append(pl.BlockSpec(memory_space=pl.ANY))
        args.append(into)
    out, moved = _call(
        body, grid=(n // tn, m // tm, nk), in_specs=in_specs, out_specs=out_spec, out_shape=out_shape,
        scratch_shapes=[pltpu.VMEM((tm, tn), F32)] if nk > 1 else [], dims=("parallel", "parallel", "arbitrary"), name=name,
        args=args, aliases={2: 0} if aliased else None, job=job)
    return out if job is None else (out, moved)


def _matmul_pieces(pieces, addend, name, tk, job=None, tm_cap=512):
    m = pieces[0][0].shape[0]
    n = pieces[0][1].shape[1]
    tm = _tile(m, tm_cap)
    spans, s0 = [], 0
    for a, b, row0 in pieces:
        assert a.shape[1] % tk == 0 and row0 % tk == 0 and b.shape[1] == n and a.shape[0] == m
        spans.append((s0, a.shape[1] // tk, row0 // tk))
        s0 += a.shape[1] // tk
    steps = s0
    np_ = len(pieces)
    groups = []
    for (a, b, _), (first, count, brow) in zip(pieces, spans):
        if groups and groups[-1][0] is b and groups[-1][3] + groups[-1][2] == brow:
            groups[-1][2] += count
        else:
            groups.append([b, first, count, brow])
    b_of = []
    for first, count, _ in spans:
        b_of.append(next(k for k, g in enumerate(groups) if g[1] <= first < g[1] + g[2]))
    ng = len(groups)

    nm = m // tm

    def body(*refs):
        o_ref, acc = refs[-2], refs[-1]
        s, i = pl.program_id(0), pl.program_id(1)
        rows = pl.ds(pl.multiple_of(i * tm, tm), tm)

        @pl.when(s == 0)
        def _():
            acc[rows, :] = refs[np_ + ng][...] if addend is not None else jnp.zeros((tm, n), F32)

        for p, (first, count, _) in enumerate(spans):
            @pl.when((s >= first) & (s < first + count))
            def _(p=p):
                acc[rows, :] += jnp.dot(refs[p][...], refs[np_ + b_of[p]][...], preferred_element_type=F32)

        @pl.when(s == steps - 1)
        def _():
            o_ref[...] = acc[rows, :]

    in_specs, args = [], []
    for (a, _, _), (first, count, _) in zip(pieces, spans):
        in_specs.append(pl.BlockSpec((tm, tk), lambda s, i, f=first, c=count: (
            jnp.where(s < f, 0, jnp.where(s >= f + c, nm - 1, i)), jnp.clip(s - f, 0, c - 1))))
        args.append(a)
    for b, first, count, brow in groups:
        in_specs.append(pl.BlockSpec((tk, n), lambda s, i, f=first, c=count, r=brow: (r + jnp.clip(s - f, 0, c - 1), 0)))
        args.append(b)
    if addend is not None:
        in_specs.append(pl.BlockSpec((tm, n), lambda s, i: (jnp.where(s == 0, i, nm - 1), 0)))
        args.append(addend)
    out, moved = _call(
        body, grid=(steps, nm), in_specs=in_specs,
        out_specs=pl.BlockSpec((tm, n), lambda s, i: (jnp.where(s == steps - 1, i, 0), 0)),
        out_shape=jax.ShapeDtypeStruct((m, n), F32), scratch_shapes=[pltpu.VMEM((m, n), F32)],
        dims=("arbitrary", "arbitrary"), name=name, args=args, job=job)
    return out if job is None else (out, moved)


def _norm_fwd(x, z, g_post, g_next, name):
    t, d = x.shape
    tt = _rows(t, 512)
    row = pl.BlockSpec((tt, d), lambda i: (i, 0))
    vec = pl.BlockSpec((1, d), lambda i: (0, 0))

    def body(*refs):
        if z is None:
            x_ref, gn_ref, h_ref = refs
            xn = x_ref[...]
        else:
            x_ref, z_ref, gp_ref, gn_ref, xo_ref, h_ref = refs
            zz = z_ref[...]
            r = lax.rsqrt(jnp.mean(zz * zz, axis=-1, keepdims=True) + EPS)
            xn = x_ref[...] + zz * r * gp_ref[...]
            xo_ref[...] = xn
        r2 = lax.rsqrt(jnp.mean(xn * xn, axis=-1, keepdims=True) + EPS)
        h_ref[...] = (xn * r2 * gn_ref[...]).astype(BF16)

    if z is None:
        return pl.pallas_call(
            body, grid=(t // tt,), in_specs=[row, vec], out_specs=row,
            out_shape=jax.ShapeDtypeStruct((t, d), BF16), compiler_params=_cparams(("parallel",)), name=name,
        )(x, g_next)
    return pl.pallas_call(
        body, grid=(t // tt,), in_specs=[row, row, vec, vec], out_specs=[row, row],
        out_shape=[jax.ShapeDtypeStruct((t, d), F32), jax.ShapeDtypeStruct((t, d), BF16)],
        compiler_params=_cparams(("parallel",)), name=name,
    )(x, z, g_post, g_next)


def _rms_bwd(dy, x, g):
    r = lax.rsqrt(jnp.mean(x * x, axis=-1, keepdims=True) + EPS)
    n = x * r
    dn = dy * g
    dx = r * (dn - n * jnp.mean(dn * n, axis=-1, keepdims=True))
    return dx, dy * n


def _norm_bwd(dres, pre, post, name):
    t, d = dres.shape
    tt = _rows(t, 512)
    nt = t // tt
    row = pl.BlockSpec((tt, d), lambda i: (i, 0))
    vec = pl.BlockSpec((1, d), lambda i: (0, 0))
    has_pre, has_post = pre is not None, post is not None
    n_in = 1 + (3 if has_pre else 0) + (2 if has_post else 0)
    n_out = has_pre + has_post + has_pre + has_post

    def body(*refs):
        ins, outs, scr = refs[:n_in], refs[n_in:n_in + n_out], refs[n_in + n_out:]
        i = pl.program_id(0)
        dx = ins[0][...]
        pos, opos, spos = 1, 0, 0
        accs = []
        if has_pre:
            dh_ref, xa_ref, ga_ref = ins[pos:pos + 3]
            pos += 3
            dxa, dga_t = _rms_bwd(dh_ref[...], xa_ref[...], ga_ref[...])
            dx = dx + dxa
            outs[opos][...] = dx
            opos += 1
            accs.append((scr[spos], dga_t))
            spos += 1
        if has_post:
            zb_ref, gb_ref = ins[pos:pos + 2]
            dz, dgb_t = _rms_bwd(dx, zb_ref[...], gb_ref[...])
            outs[opos][...] = dz.astype(BF16)
            opos += 1
            accs.append((scr[spos], dgb_t))
            spos += 1
        for (acc, val), out in zip(accs, outs[opos:]):
            part = _sum8(val)

            @pl.when(i == 0)
            def _(acc=acc, part=part):
                acc[...] = part

            @pl.when(i > 0)
            def _(acc=acc, part=part):
                acc[...] += part

            @pl.when(i == nt - 1)
            def _(acc=acc, out=out):
                out[...] = jnp.sum(acc[...], axis=0, keepdims=True)

    in_specs, args = [row], [dres]
    out_specs, out_shape = [], []
    if has_pre:
        in_specs += [row, row, vec]
        args += list(pre)
        out_specs.append(row)
        out_shape.append(jax.ShapeDtypeStruct((t, d), F32))
    if has_post:
        in_specs += [row, vec]
        args += list(post)
        out_specs.append(row)
        out_shape.append(jax.ShapeDtypeStruct((t, d), BF16))
    for _ in range(has_pre + has_post):
        out_specs.append(vec)
        out_shape.append(jax.ShapeDtypeStruct((1, d), F32))
    return pl.pallas_call(
        body, grid=(nt,), in_specs=in_specs, out_specs=out_specs, out_shape=out_shape,
        scratch_shapes=[pltpu.VMEM((SUBLANE, d), F32)] * (has_pre + has_post),
        compiler_params=_cparams(("arbitrary",)), name=name,
    )(*args)


def _loss_grad(y, target, name):
    t, d = y.shape
    tt = _rows(t, 512)
    nt = t // tt
    row = pl.BlockSpec((tt, d), lambda i: (i, 0))
    inv_d = 1.0 / d

    def body(y_ref, t_ref, dy_ref, l_ref):
        i = pl.program_id(0)
        diff = y_ref[...] - t_ref[...]
        dy_ref[...] = diff * inv_d
        s8 = _sum8(diff * diff)
        part = s8[:, 0:LANE]
        for k in range(1, d // LANE):
            part = part + s8[:, k * LANE:(k + 1) * LANE]
        part = part * (0.5 * inv_d)

        @pl.when(i == 0)
        def _():
            l_ref[...] = part

        @pl.when(i > 0)
        def _():
            l_ref[...] += part

    return pl.pallas_call(
        body, grid=(nt,), in_specs=[row, row],
        out_specs=[row, pl.BlockSpec((SUBLANE, LANE), lambda i: (0, 0))],
        out_shape=[jax.ShapeDtypeStruct((t, d), F32), jax.ShapeDtypeStruct((SUBLANE, LANE), F32)],
        compiler_params=_cparams(("arbitrary",)), name=name,
    )(y, target)


def _swiglu_fwd(a, b, name):
    t, f = a.shape
    tt = _rows(t, 256)
    blk = pl.BlockSpec((tt, f), lambda i: (i, 0))

    def body(a_ref, b_ref, m_ref):
        av = a_ref[...].astype(F32)
        m_ref[...] = (av * _sigmoid(av) * b_ref[...].astype(F32)).astype(BF16)

    return pl.pallas_call(
        body, grid=(t // tt,), in_specs=[blk, blk], out_specs=blk,
        out_shape=jax.ShapeDtypeStruct((t, f), BF16), compiler_params=_cparams(("parallel",)), name=name,
    )(a, b)


def _swiglu_bwd(a, b, dm, name):
    t, f = a.shape
    tt = _rows(t, 256)
    blk = pl.BlockSpec((tt, f), lambda i: (i, 0))

    def body(a_ref, b_ref, dm_ref, da_ref, db_ref):
        av = a_ref[...].astype(F32)
        s = _sigmoid(av)
        dv = dm_ref[...].astype(F32)
        da_ref[...] = (dv * b_ref[...].astype(F32) * s * (1.0 + av * (1.0 - s))).astype(BF16)
        db_ref[...] = (dv * av * s).astype(BF16)

    return pl.pallas_call(
        body, grid=(t // tt,), in_specs=[blk, blk, blk], out_specs=[blk, blk],
        out_shape=[jax.ShapeDtypeStruct((t, f), BF16)] * 2, compiler_params=_cparams(("parallel",)), name=name,
    )(a, b, dm)


def _log_sigmoid(x):
    return jnp.minimum(x, 0.0) - jnp.log1p(jnp.exp(-jnp.abs(x)))


def _fox_prep(f_t, b_f, name):
    h, t = f_t.shape

    def body(f_ref, b_ref, c_ref):
        r = lax.broadcasted_iota(jnp.int32, (LANE, LANE), 0)
        c = lax.broadcasted_iota(jnp.int32, (LANE, LANE), 1)
        upper = (r <= c).astype(F32)
        carry = jnp.zeros((h, 1), F32)
        for j in range(t // LANE):
            sl = slice(j * LANE, (j + 1) * LANE)
            lf = _log_sigmoid(f_ref[:, sl] + b_ref[...])
            cs = jnp.dot(lf, upper, precision=lax.Precision.HIGHEST, preferred_element_type=F32) + carry
            c_ref[:, sl] = cs
            carry = cs[:, LANE - 1:LANE]

    return pl.pallas_call(body, out_shape=jax.ShapeDtypeStruct((h, t), F32), compiler_params=_cparams(), name=name)(f_t, b_f)


def _fox_bwd(dc_q, dc_k, f_t, b_f, name):
    h, t = f_t.shape

    def body(dq_ref, dk_ref, f_ref, b_ref, df_ref, db_ref):
        r = lax.broadcasted_iota(jnp.int32, (LANE, LANE), 0)
        c = lax.broadcasted_iota(jnp.int32, (LANE, LANE), 1)
        lower = (r >= c).astype(F32)
        carry = jnp.zeros((h, 1), F32)
        dbsum = jnp.zeros((h, 1), F32)
        for j in reversed(range(t // LANE)):
            sl = slice(j * LANE, (j + 1) * LANE)
            dc = dq_ref[:, sl] - dk_ref[:, sl]
            dl = jnp.dot(dc, lower, precision=lax.Precision.HIGHEST, preferred_element_type=F32) + carry
            carry = dl[:, 0:1]
            df = dl * _sigmoid(-(f_ref[:, sl] + b_ref[...]))
            df_ref[:, sl] = df
            dbsum = dbsum + jnp.sum(df, axis=-1, keepdims=True)
        db_ref[...] = dbsum

    return pl.pallas_call(
        body, out_shape=[jax.ShapeDtypeStruct((h, t), F32), jax.ShapeDtypeStruct((h, 1), F32)],
        compiler_params=_cparams(), name=name,
    )(dc_q, dc_k, f_t, b_f)


ATTN_FWD = (1024, 512)
ATTN_BWD = (512, 512)


def _attn_tiles(t, tiles):
    return _tile(t, tiles[0]), _tile(t, tiles[1])


def _attn_fwd(proj, c_t, d, name, job=None):
    t = proj.shape[0]
    h = d // LANE
    bq, bk = _attn_tiles(t, ATTN_FWD)
    nq, nk, rr = t // bq, t // bk, bq // bk
    qc, kc, vc = COL_Q * h, COL_K * h, COL_VA * h
    qscale = LANE ** -0.5 * LOG2E

    def body(q_ref, k_ref, v_ref, cc_ref, cr_ref, o_ref, lse_ref, kb, vt, ckb, acc):
        i = pl.program_id(1)

        @pl.when(i == 0)
        def _():
            kb[...] = k_ref[...].astype(BF16)
            ckb[...] = jnp.broadcast_to(cc_ref[0] * LOG2E, (t, bq))
            for jn in range(nk):
                vt[jn] = v_ref[jn * bk:(jn + 1) * bk, :].T.astype(BF16)

        q = (q_ref[...] * qscale).astype(BF16)
        cq = cr_ref[0, 0] * LOG2E
        acc[...] = jnp.zeros((LANE, bq), F32)

        def block(j, diag, m_old, l_old):
            rows = pl.ds(pl.multiple_of(j * bk, bk), bk)
            s = _nt_dot(kb[rows, :], q) - ckb[rows, :]
            if diag is not None:
                kk = lax.broadcasted_iota(jnp.int32, (bk, bq), 0)
                qq = lax.broadcasted_iota(jnp.int32, (bk, bq), 1)
                s = jnp.where(qq >= kk + diag * bk, s, NEG)
            m_new = jnp.maximum(m_old, jnp.max(s, axis=0, keepdims=True) + cq)
            p = jnp.exp2(s + (cq - m_new))
            alpha = jnp.exp2(m_old - m_new)
            l_new = alpha * l_old + jnp.sum(p, axis=0, keepdims=True)
            acc[...] = alpha * acc[...] + jnp.dot(vt[j], p.astype(BF16), preferred_element_type=F32)
            return m_new, l_new

        m, l = lax.fori_loop(0, i * rr, lambda j, c: block(j, None, *c),
                             (jnp.full((1, bq), NEG, F32), jnp.zeros((1, bq), F32)))
        for jj in range(rr):
            m, l = block(i * rr + jj, jj, m, l)
        o_ref[...] = (acc[...] / l).T
        lse_ref[0, 0] = m + jnp.log2(l)

    rowq = pl.BlockSpec((1, 1, 1, bq), lambda hh, i: (hh, i, 0, 0))
    outs, moved = _call(
        body, grid=(h, nq),
        in_specs=[
            pl.BlockSpec((bq, LANE), lambda hh, i: (i, qc + hh)),
            pl.BlockSpec((t, LANE), lambda hh, i: (0, kc + hh)),
            pl.BlockSpec((t, LANE), lambda hh, i: (0, vc + hh)),
            pl.BlockSpec((1, t, 1), lambda hh, i: (hh, 0, 0)),
            rowq,
        ],
        out_specs=[pl.BlockSpec((bq, LANE), lambda hh, i: (i, hh)), rowq],
        out_shape=[jax.ShapeDtypeStruct((t, d), F32), jax.ShapeDtypeStruct((h, nq, 1, bq), F32)],
        scratch_shapes=[pltpu.VMEM((t, LANE), BF16), pltpu.VMEM((nk, LANE, bk), BF16), pltpu.VMEM((t, bq), F32),
                        pltpu.VMEM((LANE, bq), F32)],
        dims=("arbitrary", "arbitrary"), name=name,
        args=[proj, proj, proj, c_t.reshape(h, t, 1), c_t.reshape(h, nq, 1, bq)], job=job)
    outs = [outs[0], outs[1].reshape(h, t)]
    return outs if job is None else (outs, moved)


def _attn_bwd(proj, do, o, lse, c_t, d, name, job=None):
    t = proj.shape[0]
    h = d // LANE
    bq, bk = _attn_tiles(t, ATTN_BWD)
    nq, nk, rr = t // bq, t // bk, bq // bk
    qc, kc, vc = COL_Q * h, COL_K * h, COL_VA * h
    scale = LANE ** -0.5

    def body(q_ref, k_ref, v_ref, do_ref, o_ref, lse_ref, cc_ref, cr_ref, dq_ref, dk_ref, dv_ref, dcq_ref, dck_ref,
             kb, kt, vb, ckb, dk_acc, dv_acc, dck_acc, dqt_acc):
        i = pl.program_id(1)

        @pl.when(i == 0)
        def _():
            kb[...] = k_ref[...].astype(BF16)
            vb[...] = v_ref[...].astype(BF16)
            ckb[...] = jnp.broadcast_to(cc_ref[0] * LOG2E, (t, bq))
            for jn in range(nk):
                kt[jn] = k_ref[jn * bk:(jn + 1) * bk, :].T.astype(BF16)
            dk_acc[...] = jnp.zeros((t, LANE), F32)
            dv_acc[...] = jnp.zeros((t, LANE), F32)
            dck_acc[...] = jnp.zeros((t, LANE), F32)

        q = (q_ref[...] * (scale * LOG2E)).astype(BF16)
        dof = do_ref[...]
        dob = dof.astype(BF16)
        delta = jnp.sum((dof * o_ref[...]).T, axis=0, keepdims=True)
        rowb = cr_ref[0, 0] * LOG2E - lse_ref[0, 0]
        dqt_acc[...] = jnp.zeros((LANE, bq), F32)

        def block(j, diag, dcq):
            rows = pl.ds(pl.multiple_of(j * bk, bk), bk)
            p = jnp.exp2(_nt_dot(kb[rows, :], q) - ckb[rows, :] + rowb)
            if diag is not None:
                kk = lax.broadcasted_iota(jnp.int32, (bk, bq), 0)
                qq = lax.broadcasted_iota(jnp.int32, (bk, bq), 1)
                p = jnp.where(qq >= kk + diag * bk, p, 0.0)
            dv_acc[rows, :] += jnp.dot(p.astype(BF16), dob, preferred_element_type=F32)
            ds = p * (_nt_dot(vb[rows, :], dob) - delta)
            dsb = ds.astype(BF16)
            dk_acc[rows, :] += jnp.dot(dsb, q, preferred_element_type=F32)
            dqt_acc[...] += jnp.dot(kt[j], dsb, preferred_element_type=F32)
            part = ds[:, 0:LANE]
            for k in range(1, bq // LANE):
                part = part + ds[:, k * LANE:(k + 1) * LANE]
            dck_acc[rows, :] += part
            return dcq + jnp.sum(ds, axis=0, keepdims=True)

        dcq = lax.fori_loop(0, i * rr, lambda j, c: block(j, None, c), jnp.zeros((1, bq), F32))
        for jj in range(rr):
            dcq = block(i * rr + jj, jj, dcq)
        dq_ref[...] = (dqt_acc[...] * scale).T.astype(BF16)
        dcq_ref[0, 0] = dcq

        @pl.when(i == nq - 1)
        def _():
            dk_ref[...] = (dk_acc[...] * LN2).astype(BF16)
            dv_ref[...] = dv_acc[...].astype(BF16)
            dck_ref[0] = jnp.sum(dck_acc[...], axis=-1, keepdims=True)

    rowq = pl.BlockSpec((1, 1, 1, bq), lambda hh, i: (hh, i, 0, 0))
    blk = pl.BlockSpec((bq, LANE), lambda hh, i: (i, hh))
    whole = pl.BlockSpec((t, LANE), lambda hh, i: (0, hh))
    colk = pl.BlockSpec((1, t, 1), lambda hh, i: (hh, 0, 0))
    outs, moved = _call(
        body, grid=(h, nq),
        in_specs=[
            pl.BlockSpec((bq, LANE), lambda hh, i: (i, qc + hh)),
            pl.BlockSpec((t, LANE), lambda hh, i: (0, kc + hh)),
            pl.BlockSpec((t, LANE), lambda hh, i: (0, vc + hh)),
            blk, blk, rowq, colk, rowq,
        ],
        out_specs=[blk, whole, whole, rowq, colk],
        out_shape=[jax.ShapeDtypeStruct((t, d), BF16), jax.ShapeDtypeStruct((t, d), BF16), jax.ShapeDtypeStruct((t, d), BF16),
                   jax.ShapeDtypeStruct((h, nq, 1, bq), F32), jax.ShapeDtypeStruct((h, t, 1), F32)],
        scratch_shapes=[pltpu.VMEM((t, LANE), BF16), pltpu.VMEM((nk, LANE, bk), BF16), pltpu.VMEM((t, LANE), BF16),
                        pltpu.VMEM((t, bq), F32), pltpu.VMEM((t, LANE), F32), pltpu.VMEM((t, LANE), F32),
                        pltpu.VMEM((t, LANE), F32), pltpu.VMEM((LANE, bq), F32)],
        dims=("arbitrary", "arbitrary"), name=name,
        args=[proj, proj, proj, do, o, lse.reshape(h, nq, 1, bq), c_t.reshape(h, t, 1), c_t.reshape(h, nq, 1, bq)], job=job)
    outs = list(outs[:3]) + [outs[3].reshape(h, t), outs[4].reshape(h, t)]
    return outs if job is None else (outs, moved)


def _sgu_forward(u_ref, v_ref, gv_ref, wm_ref, bs_ref, mix_sc, groups):
    gu, dgu = _gelu_and_grad(u_ref[...])
    gvv, dgv = _gelu_and_grad(v_ref[...])
    mu = jnp.mean(gvv, axis=-1, keepdims=True)
    xc = gvv - mu
    r = lax.rsqrt(jnp.mean(xc * xc, axis=-1, keepdims=True) + EPS)
    nhat = xc * r
    vn = (nhat * gv_ref[...]).astype(BF16)
    for g in range(groups):
        sl = slice(g * LANE, (g + 1) * LANE)
        mix_sc[:, sl] = jnp.dot(wm_ref[g], vn[:, sl], preferred_element_type=F32) + bs_ref[g]
    return gu, dgu, dgv, nhat, r, vn, mix_sc[...]


def _mix_fwd(proj, o, wm, bs, g_v, d, name):
    t = proj.shape[0]
    groups = d // LANE

    def body(u_ref, v_ref, ga_ref, gb_ref, o_ref, wm_ref, bs_ref, gv_ref, out_ref, mix_sc):
        gu, _, _, _, _, _, mixed = _sgu_forward(u_ref, v_ref, gv_ref, wm_ref, bs_ref, mix_sc, groups)
        out_ref[...] = (_sigmoid(ga_ref[...]) * (gu * mixed) + _sigmoid(gb_ref[...]) * o_ref[...]).astype(BF16)

    def colblk(k):
        return pl.BlockSpec((LANE, d), lambda i, k=k: (i, k))

    full3 = pl.BlockSpec((groups, LANE, LANE), lambda i: (0, 0, 0))
    return pl.pallas_call(
        body, grid=(t // LANE,),
        in_specs=[colblk(COL_U), colblk(COL_V), colblk(COL_GA), colblk(COL_GB), colblk(0), full3,
                  pl.BlockSpec((groups, LANE, 1), lambda i: (0, 0, 0)), pl.BlockSpec((1, d), lambda i: (0, 0))],
        out_specs=colblk(0),
        out_shape=jax.ShapeDtypeStruct((t, d), BF16),
        scratch_shapes=[pltpu.VMEM((LANE, d), F32)],
        compiler_params=_cparams(("parallel",)), name=name,
    )(proj, proj, proj, proj, o, wm, bs, g_v)


def _mix_bwd(dmerged, proj, o, wm, wm_t, bs, g_v, d, name):
    t = proj.shape[0]
    groups = d // LANE
    nt = t // LANE

    def body(dm_ref, u_ref, v_ref, ga_ref, gb_ref, o_ref, wm_ref, wmt_ref, bs_ref, gv_ref,
             duv_ref, dg_ref, do_ref, dws_ref, dbs_ref, dgv_ref, mix_sc, dvn_sc, gv_acc):
        i = pl.program_id(0)

        @pl.when(i == 0)
        def _():
            dws_ref[...] = jnp.zeros_like(dws_ref)
            dbs_ref[...] = jnp.zeros_like(dbs_ref)
            gv_acc[...] = jnp.zeros_like(gv_acc)

        gu, dgu, dgv, nhat, r, vn, mixed = _sgu_forward(u_ref, v_ref, gv_ref, wm_ref, bs_ref, mix_sc, groups)
        dm = dm_ref[...]
        sa = _sigmoid(ga_ref[...])
        sb = _sigmoid(gb_ref[...])
        ov = o_ref[...]
        y_a = gu * mixed
        dg_ref[:, 0:d] = (dm * y_a * sa * (1.0 - sa)).astype(BF16)
        dg_ref[:, d:2 * d] = (dm * ov * sb * (1.0 - sb)).astype(BF16)
        do_ref[...] = dm * sb
        dy_a = dm * sa
        duv_ref[:, 0:d] = (dy_a * mixed * dgu).astype(BF16)
        dmixed = dy_a * gu
        dmixed_b = dmixed.astype(BF16)
        for g in range(groups):
            sl = slice(g * LANE, (g + 1) * LANE)
            dvn_sc[:, sl] = jnp.dot(wmt_ref[g], dmixed_b[:, sl], preferred_element_type=F32)
            dws_ref[g] += _nt_dot(dmixed_b[:, sl], vn[:, sl])
            dbs_ref[g] += jnp.sum(dmixed[:, sl], axis=-1, keepdims=True)
        dvn = dvn_sc[...]
        gv_acc[...] += _sum8(dvn * nhat)
        dn = dvn * gv_ref[...]
        dgelu = r * (dn - jnp.mean(dn, axis=-1, keepdims=True) - nhat * jnp.mean(dn * nhat, axis=-1, keepdims=True))
        duv_ref[:, d:2 * d] = (dgelu * dgv).astype(BF16)

        @pl.when(i == nt - 1)
        def _():
            dgv_ref[...] = jnp.sum(gv_acc[...], axis=0, keepdims=True)
            rr = lax.broadcasted_iota(jnp.int32, (LANE, LANE), 0)
            cl = lax.broadcasted_iota(jnp.int32, (LANE, LANE), 1)
            for g in range(groups):
                dws_ref[g] = jnp.where(rr >= cl, dws_ref[g], 0.0)

    def colblk(k):
        return pl.BlockSpec((LANE, d), lambda i, k=k: (i, k))

    full3 = pl.BlockSpec((groups, LANE, LANE), lambda i: (0, 0, 0))
    col3 = pl.BlockSpec((groups, LANE, 1), lambda i: (0, 0, 0))
    vec = pl.BlockSpec((1, d), lambda i: (0, 0))
    two = pl.BlockSpec((LANE, 2 * d), lambda i: (i, 0))
    return pl.pallas_call(
        body, grid=(nt,),
        in_specs=[colblk(0), colblk(COL_U), colblk(COL_V), colblk(COL_GA), colblk(COL_GB), colblk(0), full3, full3, col3, vec],
        out_specs=[two, two, colblk(0), full3, col3, vec],
        out_shape=[jax.ShapeDtypeStruct((t, 2 * d), BF16), jax.ShapeDtypeStruct((t, 2 * d), BF16), jax.ShapeDtypeStruct((t, d), F32),
                   jax.ShapeDtypeStruct((groups, LANE, LANE), F32), jax.ShapeDtypeStruct((groups, LANE, 1), F32),
                   jax.ShapeDtypeStruct((1, d), F32)],
        scratch_shapes=[pltpu.VMEM((LANE, d), F32), pltpu.VMEM((LANE, d), F32), pltpu.VMEM((SUBLANE, d), F32)],
        compiler_params=_cparams(("arbitrary",)), name=name,
    )(dmerged, proj, proj, proj, proj, o, wm, wm_t, bs, g_v)


def _adam_math(w, g, m, v):
    nm = ADAM_B1 * m + (1.0 - ADAM_B1) * g
    nv = ADAM_B2 * v + (1.0 - ADAM_B2) * (g * g)
    delta = -ADAM_LR * ((nm * ADAM_C1) / (jnp.sqrt(nv * ADAM_C2) + ADAM_EPS) + ADAM_WD * w)
    return delta, nm, nv


def _adamw(w, g, m, v, name):
    r, c = w.shape
    cap = max(SUBLANE, (2 * 1024 * 1024) // (4 * c) // SUBLANE * SUBLANE)
    tr = _rows(r, cap)

    def body(w_ref, g_ref, m_ref, v_ref, d_ref, nm_ref, nv_ref):
        d_ref[...], nm_ref[...], nv_ref[...] = _adam_math(w_ref[...], g_ref[...], m_ref[...], v_ref[...])

    blk = pl.BlockSpec((tr, c), lambda i: (i, 0))
    return pl.pallas_call(
        body, grid=(r // tr,), in_specs=[blk] * 4, out_specs=[blk] * 3,
        out_shape=[jax.ShapeDtypeStruct((r, c), F32)] * 3, compiler_params=_cparams(("parallel",)), name=name,
    )(w, g, m, v)


def _adamw_layers(w, g0, g1, m, v, name):
    _, r, c = w.shape
    cap = max(SUBLANE, (1024 * 1024) // (4 * c) // SUBLANE * SUBLANE)
    tr = _rows(r, cap)

    def body(w_ref, g0_ref, g1_ref, m_ref, v_ref, g_ref, d_ref, nm_ref, nv_ref):
        gg = jnp.where(pl.program_id(0) == 0, g0_ref[...], g1_ref[...])
        g_ref[0] = gg
        d_ref[0], nm_ref[0], nv_ref[0] = _adam_math(w_ref[0], gg, m_ref[0], v_ref[0])

    lay = pl.BlockSpec((1, tr, c), lambda l, i: (l, i, 0))

    def gspec(l0):
        return pl.BlockSpec((tr, c), lambda l, i: (jnp.where(l == l0, i, 0), 0))

    return pl.pallas_call(
        body, grid=(2, r // tr), in_specs=[lay, gspec(0), gspec(1), lay, lay], out_specs=[lay] * 4,
        out_shape=[jax.ShapeDtypeStruct((2, r, c), F32)] * 4, compiler_params=_cparams(("arbitrary", "arbitrary")), name=name,
    )(w, g0, g1, m, v)


def _adamw_interleaved(w, g0, g1, m, v, name):
    r, _, c = w.shape
    tr = 128

    def body(w_ref, g0_ref, g1_ref, m_ref, v_ref, g_ref, d_ref, nm_ref, nv_ref):
        for l, gl in enumerate((g0_ref, g1_ref)):
            gg = gl[...]
            g_ref[:, l, :] = gg
            d_ref[:, l, :], nm_ref[:, l, :], nv_ref[:, l, :] = _adam_math(w_ref[:, l, :], gg, m_ref[:, l, :], v_ref[:, l, :])

    lay = pl.BlockSpec((tr, 2, c), lambda i: (i, 0, 0))
    flat = pl.BlockSpec((tr, c), lambda i: (i, 0))
    return pl.pallas_call(
        body, grid=(pl.cdiv(r, tr),), in_specs=[lay, flat, flat, lay, lay], out_specs=[lay] * 4,
        out_shape=[jax.ShapeDtypeStruct((r, 2, c), F32)] * 4, compiler_params=_cparams(("parallel",)), name=name,
    )(w, g0, g1, m, v)


def _add_half(p4, recv, c_idx, name):
    _, r, c = p4.shape
    hw = c // 2
    tr = 256 if r % 256 == 0 else r

    def body(c_ref, a_ref, b_ref, o_ref):
        o_ref[...] = (a_ref[...].astype(F32) + b_ref[...].astype(F32)).astype(BF16)

    return pl.pallas_call(
        body,
        grid_spec=pltpu.PrefetchScalarGridSpec(
            num_scalar_prefetch=1, grid=(N_CHIPS, pl.cdiv(r, tr)),
            in_specs=[pl.BlockSpec((1, tr, hw), lambda s, i, cr: (s, i, cr[0])), pl.BlockSpec((1, tr, hw), lambda s, i, cr: (s, i, 0))],
            out_specs=pl.BlockSpec((1, tr, hw), lambda s, i, cr: (s, i, 0)),
        ),
        out_shape=jax.ShapeDtypeStruct((N_CHIPS, r, hw), BF16), compiler_params=_cparams(("parallel", "parallel")), name=name,
    )(c_idx, p4, recv)


def _sum_slots(x, own, sel, name, out_cols=None):
    s, r, c = x.shape
    tr = 128 if r % 128 == 0 else r

    def body(sel_ref, x_ref, own_ref, o_ref):
        mine = own_ref[0].astype(F32)
        acc = jnp.zeros((tr, c), F32)
        for k in range(s):
            acc = acc + jnp.where(sel_ref[0] == k, mine, x_ref[k].astype(F32))
        o_ref[...] = acc

    return pl.pallas_call(
        body,
        grid_spec=pltpu.PrefetchScalarGridSpec(
            num_scalar_prefetch=1, grid=(pl.cdiv(r, tr),),
            in_specs=[pl.BlockSpec((s, tr, c), lambda i, sr: (0, i, 0)), pl.BlockSpec((1, tr, c), lambda i, sr: (sr[1], i, 0))],
            out_specs=pl.BlockSpec((tr, c), lambda i, sr: (i, sr[2])),
        ),
        out_shape=jax.ShapeDtypeStruct((r, out_cols or c), F32), compiler_params=_cparams(("parallel",)), name=name,
    )(sel, x, own)


def _half_cols(width, hc):
    hw = width // 2
    assert hw % LANE == 0
    return pl.ds(pl.multiple_of(hc * hw, LANE), hw)


def _remote(src, dst, ssem, rsem, k, to):
    return pltpu.make_async_remote_copy(src_ref=src, dst_ref=dst, send_sem=ssem.at[k], recv_sem=rsem.at[k], device_id=to,
                                        device_id_type=MESH)


def _gather_job(bufs, mid_at=0.5):
    def part(o, a, slot, hc):
        return o[a].at[slot, :, _half_cols(bufs[a].shape[2], hc)]

    def first(ins, o, fresh, ssem, rsem):
        x, y, c, chips = _place()
        for a in range(len(bufs)):
            mine = part(o, a, 2 * x + y, c)
            for j, chip in enumerate(chips):
                _remote(mine, mine, ssem, rsem, 6 * a + j, (chip[0], chip[1], c)).start()

    def mid(ins, o, fresh, ssem, rsem):
        x, y, c, chips = _place()
        for a in range(len(bufs)):
            for j, chip in enumerate(chips):
                got = part(o, a, 2 * chip[0] + chip[1], c)
                _remote(got, got, ssem, rsem, 6 * a + j, (x, y, c)).wait_recv()
                _remote(got, got, ssem, rsem, 6 * a + 3 + j, (x, y, 1 - c)).start()

    def last(ins, o, fresh, ssem, rsem):
        x, y, c, chips = _place()
        for a in range(len(bufs)):
            for j, chip in enumerate(chips):
                got = part(o, a, 2 * chip[0] + chip[1], 1 - c)
                _remote(got, got, ssem, rsem, 6 * a + 3 + j, (x, y, c)).wait_recv()
        for a in range(len(bufs)):
            mine = part(o, a, 2 * x + y, c)
            for j, chip in enumerate(chips):
                _remote(mine, mine, ssem, rsem, 6 * a + j, (x, y, c)).wait_send()
                passed = part(o, a, 2 * chip[0] + chip[1], c)
                _remote(passed, passed, ssem, rsem, 6 * a + 3 + j, (x, y, c)).wait_send()

    return _Job([], bufs, [], 6 * len(bufs), first, mid, last, mid_at)


def _swap_job(p4s):
    def pairs(ins, fresh, c):
        return [(a, s, ins[a].at[s, :, _half_cols(p4s[a].shape[2], 1 - c)], fresh[a].at[s])
                for a in range(len(p4s)) for s in range(N_CHIPS)]

    def first(ins, inout, fresh, ssem, rsem):
        x, y, c, _ = _place()
        for a, s, src, dst in pairs(ins, fresh, c):
            _remote(src, dst, ssem, rsem, N_CHIPS * a + s, (x, y, 1 - c)).start()

    def last(ins, inout, fresh, ssem, rsem):
        x, y, c, _ = _place()
        for a, s, src, dst in pairs(ins, fresh, c):
            _remote(src, dst, ssem, rsem, N_CHIPS * a + s, (x, y, 1 - c)).wait()

    fresh = [jax.ShapeDtypeStruct(p.shape[:2] + (p.shape[2] // 2,), p.dtype) for p in p4s]
    return _Job(p4s, [], fresh, N_CHIPS * len(p4s), first, None, last)


def _scatter_job(parts):
    def first(ins, inout, fresh, ssem, rsem):
        x, y, c, chips = _place()
        for a in range(len(parts)):
            for j, chip in enumerate(chips):
                _remote(ins[a].at[2 * chip[0] + chip[1]], fresh[a].at[2 * x + y], ssem, rsem, 3 * a + j, (chip[0], chip[1], c)).start()

    def last(ins, inout, fresh, ssem, rsem):
        x, y, c, chips = _place()
        for a in range(len(parts)):
            for j, chip in enumerate(chips):
                slot = 2 * chip[0] + chip[1]
                _remote(ins[a].at[slot], fresh[a].at[slot], ssem, rsem, 3 * a + j, (x, y, c)).wait()

    return _Job(parts, [], [jax.ShapeDtypeStruct(p.shape, p.dtype) for p in parts], 3 * len(parts), first, None, last)


def _share_job(gs):
    def halves(o, a, c):
        width = gs[a].shape[1]
        return o[a].at[:, _half_cols(width, c)], o[a].at[:, _half_cols(width, 1 - c)]

    def first(ins, o, fresh, ssem, rsem):
        x, y, c, _ = _place()
        for a in range(len(gs)):
            mine, _ = halves(o, a, c)
            _remote(mine, mine, ssem, rsem, a, (x, y, 1 - c)).start()

    def last(ins, o, fresh, ssem, rsem):
        x, y, c, _ = _place()
        for a in range(len(gs)):
            mine, theirs = halves(o, a, c)
            _remote(mine, theirs, ssem, rsem, a, (x, y, 1 - c)).wait()

    return _Job([], gs, [], len(gs), first, None, last)


def _gather_all_job(buf):
    def peers():
        x, y, c, _ = _place()
        flips = [(fx, fy, fc) for fx in (0, 1) for fy in (0, 1) for fc in (0, 1)][1:]
        return (x, y, c), [((1 - x) if fx else x, (1 - y) if fy else y, (1 - c) if fc else c) for fx, fy, fc in flips]

    def first(ins, inout, fresh, ssem, rsem):
        (x, y, c), others = peers()
        for k, peer in enumerate(others):
            _remote(ins[0], fresh[0].at[4 * x + 2 * y + c], ssem, rsem, k, peer).start()

    def last(ins, inout, fresh, ssem, rsem):
        me, others = peers()
        for k, peer in enumerate(others):
            _remote(ins[0], fresh[0].at[4 * peer[0] + 2 * peer[1] + peer[2]], ssem, rsem, k, me).wait()

    return _Job([buf], [], [jax.ShapeDtypeStruct((N_DEV,) + buf.shape, buf.dtype)], N_DEV - 1, first, None, last)


class _SemView:
    def __init__(self, sems, off):
        self.sems, self.off = sems, off

    @property
    def at(self):
        return self

    def __getitem__(self, k):
        return self.sems.at[k + self.off]


def _join(jobs):
    spans, pos = [], [0, 0, 0, 0]
    for j in jobs:
        nxt = [pos[0] + len(j.ins), pos[1] + len(j.inout), pos[2] + len(j.fresh), pos[3] + j.nsem]
        spans.append((pos, nxt))
        pos = nxt

    def hook(which):
        fns = [getattr(j, which) for j in jobs]
        if all(f is None for f in fns):
            return None

        def run(ins, inout, fresh, ssem, rsem):
            for fn, (lo, hi) in zip(fns, spans):
                if fn is not None:
                    fn(ins[lo[0]:hi[0]], inout[lo[1]:hi[1]], fresh[lo[2]:hi[2]], _SemView(ssem, lo[3]), _SemView(rsem, lo[3]))

        return run

    mids = [j.mid_at for j in jobs if j.mid is not None]
    joined = _Job([a for j in jobs for a in j.ins], [a for j in jobs for a in j.inout], [a for j in jobs for a in j.fresh],
                  pos[3], hook("first"), hook("mid"), hook("last"), max(mids) if mids else 0.5)
    n_io = pos[1]

    def split(moved):
        return [list(moved[lo[1]:hi[1]]) + list(moved[n_io + lo[2]:n_io + hi[2]]) for lo, hi in spans]

    return joined, split


def _carrying(stages, call):
    stages = [s for s in stages if s is not None]
    if not stages:
        return call(None)
    job, split = _join([s[0] for s in stages])
    out, moved = call(job)
    for (_, done), part in zip(stages, split(moved)):
        done(part)
    return out


def _layer_forward(x, h, w_in_t, rest, sm, d, stages=None):
    stages = stages or {}
    proj = _carrying([stages.get("proj")], lambda job: _matmul(h, w_in_t, "nt", F32, "proj_fwd", n=7 * d, tn_cap=1792, job=job))
    f_t = _matmul(w_in_t[7 * d:], h, "nt", F32, "forget_fwd", tn_cap=1024)
    c_t = _fox_prep(f_t, sm["b_f"], "fox_prep")
    o, lse = _carrying([stages.get("attn")], lambda job: _attn_fwd(proj, c_t, d, "attn_fwd", job=job))
    wts = rest()
    merged = _mix_fwd(proj, o, sm["wm"], sm["bs"], sm["g_v"], d, "mix_fwd")
    z = _matmul(merged, wts["w_out"], "nn", F32, "out_fwd")
    x1, h2 = _norm_fwd(x, z, sm["g_post"], sm["g_fpre"], "norm_mid")
    a = _carrying([stages.get("gate")], lambda job: _matmul(h2, wts["w_g_t"], "nt", BF16, "gate_fwd", tn_cap=1408, job=job))
    b = _carrying([stages.get("up")], lambda job: _matmul(h2, wts["w_u_t"], "nt", BF16, "up_fwd", tn_cap=1408, job=job))
    mm = _swiglu_fwd(a, b, "swiglu_fwd")
    z2 = _carrying([stages.get("down")], lambda job: _matmul(mm, wts["w_d"], "nn", F32, "down_fwd", job=job))
    return dict(x=x, h=h, proj=proj, f_t=f_t, c_t=c_t, o=o, lse=lse, merged=merged, z=z, x1=x1,
                h2=h2, a=a, b=b, mm=mm, z2=z2)


class _GradExchange:
    def __init__(self, pay, keys, c_idx, chip):
        self.keys = list(keys)
        self.p4 = [pay[k].reshape(N_CHIPS, pay[k].shape[1] // N_CHIPS, pay[k].shape[2]) for k in self.keys]
        self.c_idx = c_idx
        self.sel = jnp.stack([chip, chip, c_idx[0]]).astype(jnp.int32)
        self.done = 0

    def _after_swap(self, landed):
        self.parts = [_add_half(p, r, self.c_idx, "add_sibling") for p, r in zip(self.p4, landed)]
        self.done = 1

    def _after_scatter(self, landed):
        self.g = [_sum_slots(got, sent, self.sel, "sum_chips", out_cols=p.shape[2])
                  for got, sent, p in zip(landed, self.parts, self.p4)]
        self.done = 2

    def _after_share(self, moved):
        self.g = list(moved)
        self.done = 3

    def stage(self):
        if self.done == 0:
            return _swap_job(self.p4), self._after_swap
        if self.done == 1:
            return _scatter_job(self.parts), self._after_scatter
        if self.done == 2:
            return _share_job(self.g), self._after_share
        return None

    def run(self):
        for name in ("swap_grads", "scatter_grads", "share_grads")[self.done:]:
            job, done = self.stage()
            done(_run_job(job, name))

    def grads(self):
        return dict(zip(self.keys, self.g))


EARLY_KEYS = ("w_d", "w_g", "w_u", "w_out")


def _layer_backward(dz2, dx2, sv, wts, sm, d, c_idx, chip, carried=(), split_own=False, small_stage=None):
    t = dx2.shape[0]
    heads = d // LANE
    ff = wts["w_d"].shape[0]
    in_w = 7 * d + heads
    g, pay = {}, {}
    carried = list(carried)

    def payload(key, a, b, rows, row0, name, extra=()):
        def call(job):
            return _matmul(a, b, "tn", BF16, name, slab=((1, rows, d), 0, row0), into=pay.get(key), job=job, tm_cap=1408,
                           tn_cap=1024, tk_cap=1024)
        pay[key] = _carrying(list(extra), call)

    def stages():
        return [ex.stage() for ex in carried]

    dm = _carrying(stages(), lambda job: _matmul(dz2, wts["w_d"], "nt", BF16, "down_bwd_x", tn_cap=1408, tk_cap=1024, job=job))
    payload("w_d", sv["mm"], dz2, ff, 0, "down_bwd_w")
    da, db = _swiglu_bwd(sv["a"], sv["b"], dm, "swiglu_bwd")
    dh2 = _matmul_pieces([(da, wts["w_g_t"], 0), (db, wts["w_u_t"], 0)], None, "gu_bwd_x", tk=_tile(ff, 1408))
    payload("w_g", da, sv["h2"], ff, 0, "gate_bwd_w")
    payload("w_u", db, sv["h2"], ff, 0, "up_bwd_w")
    dx1, dz, g["g_fpre"], g["g_post"] = _norm_bwd(dx2, (dh2, sv["x1"], sm["g_fpre"]), (sv["z"], sm["g_post"]), "norm_bwd_mid")
    dmerged = _matmul(dz, wts["w_out"], "nt", F32, "out_bwd_x", tk_cap=1024)
    payload("w_out", sv["merged"], dz, d, 0, "out_bwd_w")
    d_uv, d_g, do, g["w_s"], g["b_s"], g["g_v"] = _mix_bwd(dmerged, sv["proj"], sv["o"], sm["wm"], sm["wm_t"], sm["bs"],
                                                         sm["g_v"], d, "mix_bwd")
    own = []
    if split_own:
        own.append(_GradExchange(pay, EARLY_KEYS, c_idx, chip))
        carried.append(own[0])
    extra = [small_stage(g)] if small_stage is not None else []
    attn_args = (sv["proj"], do, sv["o"], sv["lse"], sv["c_t"], d)
    dq, dk, dv, dc_q, dc_k = _carrying(stages() + extra, lambda job: _attn_bwd(*attn_args, "attn_bwd", job=job))
    df_t, g["b_f"] = _fox_bwd(dc_q, dc_k, sv["f_t"], sm["b_f"], "fox_bwd")
    df_b = df_t.astype(BF16)
    dh_f = _matmul(df_b, wts["w_in_t"][7 * d:], "tn", F32, "forget_bwd_x")
    pieces = [(d_uv, COL_U), (dq, COL_Q), (dk, COL_K), (dv, COL_VA), (d_g, COL_GA)]
    ops = [(p, wts["w_in_t"], col * d) for p, col in pieces]
    dh = _carrying(stages(), lambda job: _matmul_pieces(ops, dh_f, "proj_bwd_x", job=job, tk=_tile(d, 1024)))
    for p, col in pieces:
        payload("w_in", p, sv["h"], in_w, col * d, "proj_bwd_w", extra=stages())
    w_f_rows = _matmul(df_b, sv["h"], "nn", BF16, "forget_bwd_w", tk_cap=1024)
    pay["w_in"] = lax.dynamic_update_slice(pay["w_in"], w_f_rows[None], (0, 7 * d, 0))
    own.append(_GradExchange(pay, [k for k in ("w_in",) + EARLY_KEYS if not (split_own and k in EARLY_KEYS)], c_idx, chip))
    return dh, dx1, g, own


def _small_pack(parts):
    flat = jnp.concatenate([p.reshape(-1) for p in parts])
    n = flat.shape[0]
    pad = (-n) % (LANE * LANE)
    return jnp.pad(flat, (0, pad)).reshape(-1, LANE)


def kernel(x, mix_pre_g, w_in, b_forget, sgu_norm_g, w_spatial, b_spatial, w_out, mix_post_g, ffn_pre_g, w_gate, w_up, w_down, ffn_post_g, loss_target, m_mix_pre_g, m_w_in, m_b_forget, m_sgu_norm_g, m_w_spatial, m_b_spatial, m_w_out, m_mix_post_g, m_ffn_pre_g, m_w_gate, m_w_up, m_w_down, m_ffn_post_g, v_mix_pre_g, v_w_in, v_b_forget, v_sgu_norm_g, v_w_spatial, v_b_spatial, v_w_out, v_mix_post_g, v_ffn_pre_g, v_w_gate, v_w_up, v_w_down, v_ffn_post_g):
    depth, d = mix_pre_g.shape
    assert depth == 2, "core c of a chip owns layer c"
    heads = d // LANE
    t = x.shape[1]
    ff = w_down.shape[1] * N_CHIPS
    in_w = w_in.shape[2] * N_CHIPS
    assert in_w == 7 * d + heads
    xs = x.reshape(t, d)
    target = loss_target.reshape(t, d)
    c_idx = lax.axis_index("c").astype(jnp.int32).reshape(1)
    chip = 2 * lax.axis_index("x") + lax.axis_index("y")
    dev = 2 * chip + lax.axis_index("c")

    def in_view(w):
        return jnp.transpose(w, (2, 0, 1))

    def gu_view(w):
        return jnp.transpose(w, (0, 2, 1))

    own = [jnp.transpose(in_view(w_in).astype(BF16), (1, 0, 2)), w_out.astype(BF16), gu_view(w_gate).astype(BF16),
           gu_view(w_up).astype(BF16), w_down.astype(BF16)]
    bufs = [[lax.dynamic_update_slice(jnp.zeros((N_CHIPS,) + o.shape[1:], BF16), o[l][None], (chip, 0, 0)) for o in own]
            for l in range(depth)]
    first_in = _run_job(_gather_job([bufs[0][0]]), "gather_first")[0]

    def weights(g_in, g_out, g_g, g_u, g_d):
        return dict(w_in_t=g_in.reshape(in_w, d), w_out=g_out.reshape(d, d), w_g_t=g_g.reshape(ff, d),
                    w_u_t=g_u.reshape(ff, d), w_d=g_d.reshape(ff, d))

    tril = jnp.tril(jnp.ones((LANE, LANE), bool))
    smalls = []
    for l in range(depth):
        wm = jnp.where(tril[None], w_spatial[l], 0.0).astype(BF16)
        smalls.append(dict(
            b_f=b_forget[l].reshape(heads, 1), wm=wm, wm_t=jnp.swapaxes(wm, 1, 2), bs=b_spatial[l].reshape(heads, LANE, 1),
            g_v=sgu_norm_g[l].reshape(1, d), g_pre=mix_pre_g[l].reshape(1, d), g_post=mix_post_g[l].reshape(1, d),
            g_fpre=ffn_pre_g[l].reshape(1, d), g_fpost=ffn_post_g[l].reshape(1, d)))

    wts, later = [], {}

    def keep(key):
        def done(moved):
            later[key] = list(moved)
        return done

    def rest_first():
        wts.append(weights(first_in, *later["rest0"]))
        return wts[0]

    stages = dict(proj=(_gather_job(bufs[0][1:], mid_at=1.0), keep("rest0")),
                  attn=(_gather_job(bufs[1][0:2], mid_at=0.7), keep("in_out1")),
                  gate=(_gather_job(bufs[1][2:3], mid_at=1.0), keep("g1")), up=(_gather_job(bufs[1][3:4], mid_at=1.0), keep("u1")),
                  down=(_gather_job(bufs[1][4:5], mid_at=1.0), keep("d1")))
    h = _norm_fwd(xs, None, None, smalls[0]["g_pre"], "norm_first")
    saved = [_layer_forward(xs, h, first_in.reshape(in_w, d), rest_first, smalls[0], d, stages)]
    wts.append(weights(*later["in_out1"], later["g1"][0], later["u1"][0], later["d1"][0]))
    for l in range(1, depth):
        xin, h = _norm_fwd(saved[l - 1]["x1"], saved[l - 1]["z2"], smalls[l - 1]["g_fpost"], smalls[l]["g_pre"], "norm_out")
        saved.append(_layer_forward(xin, h, wts[l]["w_in_t"], lambda l=l: wts[l], smalls[l], d))
    y, _ = _norm_fwd(saved[-1]["x1"], saved[-1]["z2"], smalls[-1]["g_fpost"], smalls[-1]["g_pre"], "norm_out")
    dy, loss_part = _loss_grad(y, target, "loss")
    loss = lax.psum(jnp.sum(loss_part), ("x", "y", "c"))

    small_shapes = dict(g_pre=(d,), b_f=(heads,), g_v=(d,), w_s=w_spatial.shape[1:], b_s=b_spatial.shape[1:], g_post=(d,),
                        g_fpre=(d,), g_fpost=(d,))
    late_entries = [(0, "g_pre"), (0, "b_f")]
    early_entries = [(l, n) for l in reversed(range(depth)) for n in small_shapes if (l, n) not in late_entries]
    dev_sel = jnp.stack([dev, jnp.zeros_like(dev), jnp.zeros_like(dev)]).astype(jnp.int32)
    small_sum = {}

    def small_exchange(entries, values):
        packed = _small_pack([values[e].reshape(-1) for e in entries])

        def done(moved):
            total = _sum_slots(moved[0], packed[None], dev_sel, "sum_small").reshape(-1)
            off = 0
            for e in entries:
                n = math.prod(small_shapes[e[1]])
                small_sum[e] = total[off:off + n].reshape(small_shapes[e[1]])
                off += n

        return _gather_all_job(packed), done

    grads = [None] * depth
    exchanges = [None] * depth
    dx2 = dy
    dz2, g_fpost = _norm_bwd(dx2, None, (saved[depth - 1]["z2"], smalls[depth - 1]["g_fpost"]), "norm_bwd_top")
    for l in reversed(range(depth)):
        last = l == 0

        def small_stage(g, l=l, g_fpost=g_fpost):
            known = {(k, n): grads[k][n] for k in range(l + 1, depth) for n in small_shapes}
            known.update({(l, n): g[n] for n in g})
            known[(l, "g_fpost")] = g_fpost
            return small_exchange(early_entries, known)

        carried = [ex for k in range(l + 1, depth) for ex in exchanges[k]]
        dh, dx1, g, exchanges[l] = _layer_backward(dz2, dx2, saved[l], wts[l], smalls[l], d, c_idx, chip, carried=carried,
                                                    split_own=last, small_stage=small_stage if last else None)
        g["g_fpost"] = g_fpost
        if l > 0:
            dx2, dz2, g["g_pre"], g_fpost = _norm_bwd(dx1, (dh, saved[l]["x"], smalls[l]["g_pre"]),
                                                       (saved[l - 1]["z2"], smalls[l - 1]["g_fpost"]), "norm_bwd_between")
        else:
            grad_x, g["g_pre"] = _norm_bwd(dx1, (dh, saved[l]["x"], smalls[l]["g_pre"]), None, "norm_bwd_bottom")
        grads[l] = g
    job, done = small_exchange(late_entries, {(0, n): grads[0][n] for n in ("g_pre", "b_f")})
    done(_run_job(job, "gather_small"))
    big = [{} for _ in range(depth)]
    for l in range(depth):
        for ex in exchanges[l]:
            ex.run()
            big[l].update(ex.grads())
    small_grads = {n: jnp.stack([small_sum[(l, n)] for l in range(depth)]) for n in small_shapes}

    def adam_small(w, g, m, v):
        shp = w.shape
        if w.ndim >= 3 and shp[-1] >= LANE:
            two = (math.prod(shp[:-1]), shp[-1])
        else:
            two = (1, math.prod(shp)) if math.prod(shp) < LANE else (math.prod(shp) // LANE, LANE)
        outs = _adamw(w.reshape(two), g.reshape(two), m.reshape(two), v.reshape(two), "adamw")
        return [g] + [o.reshape(shp) for o in outs]

    def adam_in(w, m, v):
        outs = _adamw_interleaved(in_view(w), big[0]["w_in"], big[1]["w_in"], in_view(m), in_view(v), "adamw_in")
        return [jnp.transpose(o, (1, 2, 0)) for o in outs]

    def adam_gu(k, w, m, v):
        outs = _adamw_layers(gu_view(w), big[0][k], big[1][k], gu_view(m), gu_view(v), "adamw_layers")
        return [jnp.transpose(o, (0, 2, 1)) for o in outs]

    def adam_rows(k, w, m, v):
        return _adamw_layers(w, big[0][k], big[1][k], m, v, "adamw_layers")

    results = [
        adam_small(mix_pre_g, small_grads["g_pre"], m_mix_pre_g, v_mix_pre_g),
        adam_in(w_in, m_w_in, v_w_in),
        adam_small(b_forget, small_grads["b_f"], m_b_forget, v_b_forget),
        adam_small(sgu_norm_g, small_grads["g_v"], m_sgu_norm_g, v_sgu_norm_g),
        adam_small(w_spatial, small_grads["w_s"], m_w_spatial, v_w_spatial),
        adam_small(b_spatial, small_grads["b_s"], m_b_spatial, v_b_spatial),
        adam_rows("w_out", w_out, m_w_out, v_w_out),
        adam_small(mix_post_g, small_grads["g_post"], m_mix_post_g, v_mix_post_g),
        adam_small(ffn_pre_g, small_grads["g_fpre"], m_ffn_pre_g, v_ffn_pre_g),
        adam_gu("w_g", w_gate, m_w_gate, v_w_gate),
        adam_gu("w_u", w_up, m_w_up, v_w_up),
        adam_rows("w_d", w_down, m_w_down, v_w_down),
        adam_small(ffn_post_g, small_grads["g_fpost"], m_ffn_post_g, v_ffn_post_g),
    ]
    gs, deltas, new_ms, new_vs = zip(*results)
    return (loss, grad_x.reshape(x.shape), *gs, *deltas, *new_ms, *new_vs)
```

```python
import functools
import math

import jax
import jax.numpy as jnp
from jax import lax
from jax.experimental import pallas as pl
from jax.experimental.pallas import tpu as pltpu

F32 = jnp.float32
BF16 = jnp.bfloat16

EPS = 1e-6
LANE = 128
SUBLANE = 8
N_CHIPS = 4
N_DEV = 8
VMEM_LIMIT = 48 * 1024 * 1024
MESH = pl.DeviceIdType.MESH

ADAM_LR = 0.001
ADAM_B1 = 0.9
ADAM_B2 = 0.999
ADAM_EPS = 1e-08
ADAM_WD = 0.01
ADAM_STEP = 10
ADAM_C1 = 1.0 / (1.0 - ADAM_B1 ** ADAM_STEP)
ADAM_C2 = 1.0 / (1.0 - ADAM_B2 ** ADAM_STEP)

GELU_K = math.sqrt(2.0 / math.pi)
GELU_A = 0.044715
NEG = -1e30
LOG2E = 1.4426950408889634
LN2 = 0.6931471805599453

COL_U, COL_V, COL_Q, COL_K, COL_VA, COL_GA, COL_GB, COL_F = range(8)


def _cparams(sem=None):
    return pltpu.CompilerParams(dimension_semantics=sem, vmem_limit_bytes=VMEM_LIMIT)


def _tile(n, cap):
    best = None
    for t in range(LANE, min(n, cap) + 1, LANE):
        if n % t == 0:
            best = t
    return best if best is not None else n


def _rows(n, cap):
    best = None
    for t in range(SUBLANE, min(n, cap) + 1, SUBLANE):
        if n % t == 0:
            best = t
    return best if best is not None else n


def _gelu_and_grad(x):
    x2 = x * x
    t = jnp.tanh(GELU_K * (x + GELU_A * x2 * x))
    g = 0.5 * x * (1.0 + t)
    dg = 0.5 * (1.0 + t) + 0.5 * x * (1.0 - t * t) * (GELU_K * (1.0 + 3.0 * GELU_A * x2))
    return g, dg


def _sigmoid(x):
    return 1.0 / (1.0 + jnp.exp(-x))


def _sum8(v):
    n, d = v.shape
    return v.reshape(n // SUBLANE, SUBLANE, d).sum(axis=0)


def _nt_dot(a, b):
    return lax.dot_general(a, b, (((1,), (1,)), ((), ())), preferred_element_type=F32)


_HBM = pl.BlockSpec(memory_space=pl.ANY)


def _place():
    x, y, c = lax.axis_index("x"), lax.axis_index("y"), lax.axis_index("c")
    chips = [(1 - x, y), (x, 1 - y), (1 - x, 1 - y)]
    return x, y, c, chips


class _Job:
    def __init__(self, ins, inout, fresh, nsem, first, mid, last, mid_at=0.5):
        self.ins, self.inout, self.fresh, self.nsem = list(ins), list(inout), list(fresh), nsem
        self.first, self.mid, self.last, self.mid_at = first, mid, last, mid_at


def _call(body, *, grid, in_specs, out_specs, out_shape, scratch_shapes, dims, name, args, aliases=None, job=None):
    single = not isinstance(out_shape, (list, tuple))
    out_specs = [out_specs] if single else list(out_specs)
    out_shape = [out_shape] if single else list(out_shape)
    aliases = dict(aliases or {})
    if job is None:
        outs = pl.pallas_call(body, grid=grid, in_specs=in_specs, out_specs=out_specs, out_shape=out_shape,
                              scratch_shapes=scratch_shapes, input_output_aliases=aliases, compiler_params=_cparams(dims),
                              name=name)(*args)
        return (outs[0] if single else outs), []
    n_in, n_out, n_scr = len(args), len(out_shape), len(scratch_shapes)
    n_ji, n_jio, n_jf = len(job.ins), len(job.inout), len(job.fresh)
    total = math.prod(grid)

    def wrapped(*refs):
        host_in = refs[:n_in]
        pos = n_in
        j_ins = refs[pos:pos + n_ji]
        pos += n_ji + n_jio
        host_out = refs[pos:pos + n_out]
        pos += n_out
        j_inout = refs[pos:pos + n_jio]
        pos += n_jio
        j_fresh = refs[pos:pos + n_jf]
        pos += n_jf
        host_scr = refs[pos:pos + n_scr]
        ssem, rsem = refs[pos + n_scr:]
        flat = 0
        for ax, size in enumerate(grid):
            flat = flat * size + pl.program_id(ax)

        def hook(fn, at):
            if fn is not None:
                @pl.when(flat == at)
                def _():
                    fn(j_ins, j_inout, j_fresh, ssem, rsem)

        hook(job.first, 0)
        body(*host_in, *host_out, *host_scr)
        hook(job.mid, min(int(total * job.mid_at), total - 1))
        hook(job.last, total - 1)

    for k in range(n_jio):
        aliases[n_in + n_ji + k] = n_out + k
    outs = pl.pallas_call(
        wrapped, grid=grid,
        in_specs=list(in_specs) + [_HBM] * (n_ji + n_jio),
        out_specs=out_specs + [_HBM] * (n_jio + n_jf),
        out_shape=out_shape + [jax.ShapeDtypeStruct(b.shape, b.dtype) for b in job.inout] + list(job.fresh),
        scratch_shapes=list(scratch_shapes) + [pltpu.SemaphoreType.DMA((job.nsem,)), pltpu.SemaphoreType.DMA((job.nsem,))],
        input_output_aliases=aliases, compiler_params=_cparams(tuple("arbitrary" for _ in grid)), name=name,
    )(*args, *job.ins, *job.inout)
    host = outs[:n_out]
    return (host[0] if single else host), outs[n_out:]


def _run_job(job, name):
    n_ji, n_jio, n_jf = len(job.ins), len(job.inout), len(job.fresh)

    def body(*refs):
        j_ins = refs[:n_ji]
        pos = n_ji + n_jio
        j_inout = refs[pos:pos + n_jio]
        j_fresh = refs[pos + n_jio:pos + n_jio + n_jf]
        ssem, rsem = refs[pos + n_jio + n_jf:]
        for fn in (job.first, job.mid, job.last):
            if fn is not None:
                fn(j_ins, j_inout, j_fresh, ssem, rsem)

    return pl.pallas_call(
        body, in_specs=[_HBM] * (n_ji + n_jio), out_specs=[_HBM] * (n_jio + n_jf),
        out_shape=[jax.ShapeDtypeStruct(b.shape, b.dtype) for b in job.inout] + list(job.fresh),
        scratch_shapes=[pltpu.SemaphoreType.DMA((job.nsem,)), pltpu.SemaphoreType.DMA((job.nsem,))],
        input_output_aliases={n_ji + k: k for k in range(n_jio)}, name=name,
    )(*job.ins, *job.inout)


_DIMS ={"nn": ((1,), (0,)), "nt": ((1,), (1,)), "tn": ((0,), (0,))}


def _matmul(a, b, mode, out_dtype, name, n=None, slab=None, into=None, job=None, tm_cap=512, tn_cap=2048, tk_cap=1408):
    if mode == "nn":
        (m, k), (k2, nn_) = a.shape, b.shape
    elif mode == "nt":
        (m, k), (nn_, k2) = a.shape, b.shape
    else:
        (k, m), (k2, nn_) = a.shape, b.shape
    n = nn_ if n is None else n
    assert k == k2, (a.shape, b.shape, mode)
    tm, tn, tk = _tile(m, tm_cap), _tile(n, tn_cap), _tile(k, tk_cap)
    if slab is not None and slab[2]:
        tm = _tile(math.gcd(m, slab[2]), tm_cap)
    nk = k // tk
    if mode == "tn":
        a_spec = pl.BlockSpec((tk, tm), lambda j, i, kk, *_: (kk, i))
    else:
        a_spec = pl.BlockSpec((tm, tk), lambda j, i, kk, *_: (i, kk))
    if mode == "nt":
        b_spec = pl.BlockSpec((tn, tk), lambda j, i, kk, *_: (j, kk))
    else:
        b_spec = pl.BlockSpec((tk, tn), lambda j, i, kk, *_: (kk, j))
    dims = (_DIMS[mode], ((), ()))
    aliased = into is not None

    def body(*refs):
        a_ref, b_ref = refs[0], refs[1]
        o_ref = refs[3] if aliased else refs[2]
        p = lax.dot_general(a_ref[...], b_ref[...], dims, preferred_element_type=F32)
        if nk == 1:
            o_ref[...] = p.astype(out_dtype).reshape(o_ref.shape)
        else:
            acc = refs[-1]
            kk = pl.program_id(2)

            @pl.when(kk == 0)
            def _():
                acc[...] = p

            @pl.when(kk > 0)
            def _():
                acc[...] += p

            @pl.when(kk == nk - 1)
            def _():
                o_ref[...] = acc[...].astype(out_dtype).reshape(o_ref.shape)

    if slab is None:
        out_spec = pl.BlockSpec((tm, tn), lambda j, i, kk: (i, j))
        out_shape = jax.ShapeDtypeStruct((m, n), out_dtype)
    else:
        shape3, lead, row0 = slab
        assert row0 % tm == 0 and shape3[2] == n
        out_spec = pl.BlockSpec((1, tm, tn), lambda j, i, kk: (lead, row0 // tm + i, j))
        out_shape = jax.ShapeDtypeStruct(shape3, out_dtype)
    in_specs, args = [a_spec, b_spec], [a, b]
    if aliased:
        in_specs.append(pl.BlockSpec(memory_space=pl.ANY))
        args.append(into)
    out, moved = _call(
        body, grid=(n // tn, m // tm, nk), in_specs=in_specs, out_specs=out_spec, out_shape=out_shape,
        scratch_shapes=[pltpu.VMEM((tm, tn), F32)] if nk > 1 else [], dims=("parallel", "parallel", "arbitrary"), name=name,
        args=args, aliases={2: 0} if aliased else None, job=job)
    return out if job is None else (out, moved)


def _matmul_pieces(pieces, addend, name, tk, job=None, tm_cap=512):
    m = pieces[0][0].shape[0]
    n = pieces[0][1].shape[1]
    tm = _tile(m, tm_cap)
    spans, s0 = [], 0
    for a, b, row0 in pieces:
        assert a.shape[1] % tk == 0 and row0 % tk == 0 and b.shape[1] == n and a.shape[0] == m
        spans.append((s0, a.shape[1] // tk, row0 // tk))
        s0 += a.shape[1] // tk
    steps = s0
    np_ = len(pieces)
    groups = []
    for (a, b, _), (first, count, brow) in zip(pieces, spans):
        if groups and groups[-1][0] is b and groups[-1][3] + groups[-1][2] == brow:
            groups[-1][2] += count
        else:
            groups.append([b, first, count, brow])
    b_of = []
    for first, count, _ in spans:
        b_of.append(next(k for k, g in enumerate(groups) if g[1] <= first < g[1] + g[2]))
    ng = len(groups)

    nm = m // tm

    def body(*refs):
        o_ref, acc = refs[-2], refs[-1]
        s, i = pl.program_id(0), pl.program_id(1)
        rows = pl.ds(pl.multiple_of(i * tm, tm), tm)

        @pl.when(s == 0)
        def _():
            acc[rows, :] = refs[np_ + ng][...] if addend is not None else jnp.zeros((tm, n), F32)

        for p, (first, count, _) in enumerate(spans):
            @pl.when((s >= first) & (s < first + count))
            def _(p=p):
                acc[rows, :] += jnp.dot(refs[p][...], refs[np_ + b_of[p]][...], preferred_element_type=F32)

        @pl.when(s == steps - 1)
        def _():
            o_ref[...] = acc[rows, :]

    in_specs, args = [], []
    for (a, _, _), (first, count, _) in zip(pieces, spans):
        in_specs.append(pl.BlockSpec((tm, tk), lambda s, i, f=first, c=count: (
            jnp.where(s < f, 0, jnp.where(s >= f + c, nm - 1, i)), jnp.clip(s - f, 0, c - 1))))
        args.append(a)
    for b, first, count, brow in groups:
        in_specs.append(pl.BlockSpec((tk, n), lambda s, i, f=first, c=count, r=brow: (r + jnp.clip(s - f, 0, c - 1), 0)))
        args.append(b)
    if addend is not None:
        in_specs.append(pl.BlockSpec((tm, n), lambda s, i: (jnp.where(s == 0, i, nm - 1), 0)))
        args.append(addend)
    out, moved = _call(
        body, grid=(steps, nm), in_specs=in_specs,
        out_specs=pl.BlockSpec((tm, n), lambda s, i: (jnp.where(s == steps - 1, i, 0), 0)),
        out_shape=jax.ShapeDtypeStruct((m, n), F32), scratch_shapes=[pltpu.VMEM((m, n), F32)],
        dims=("arbitrary", "arbitrary"), name=name, args=args, job=job)
    return out if job is None else (out, moved)


def _norm_fwd(x, z, g_post, g_next, name):
    t, d = x.shape
    tt = _rows(t, 512)
    row = pl.BlockSpec((tt, d), lambda i: (i, 0))
    vec = pl.BlockSpec((1, d), lambda i: (0, 0))

    def body(*refs):
        if z is None:
            x_ref, gn_ref, h_ref = refs
            xn = x_ref[...]
        else:
            x_ref, z_ref, gp_ref, gn_ref, xo_ref, h_ref = refs
            zz = z_ref[...]
            r = lax.rsqrt(jnp.mean(zz * zz, axis=-1, keepdims=True) + EPS)
            xn = x_ref[...] + zz * r * gp_ref[...]
            xo_ref[...] = xn
        r2 = lax.rsqrt(jnp.mean(xn * xn, axis=-1, keepdims=True) + EPS)
        h_ref[...] = (xn * r2 * gn_ref[...]).astype(BF16)

    if z is None:
        return pl.pallas_call(
            body, grid=(t // tt,), in_specs=[row, vec], out_specs=row,
            out_shape=jax.ShapeDtypeStruct((t, d), BF16), compiler_params=_cparams(("parallel",)), name=name,
        )(x, g_next)
    return pl.pallas_call(
        body, grid=(t // tt,), in_specs=[row, row, vec, vec], out_specs=[row, row],
        out_shape=[jax.ShapeDtypeStruct((t, d), F32), jax.ShapeDtypeStruct((t, d), BF16)],
        compiler_params=_cparams(("parallel",)), name=name,
    )(x, z, g_post, g_next)


def _rms_bwd(dy, x, g):
    r = lax.rsqrt(jnp.mean(x * x, axis=-1, keepdims=True) + EPS)
    n = x * r
    dn = dy * g
    dx = r * (dn - n * jnp.mean(dn * n, axis=-1, keepdims=True))
    return dx, dy * n


def _norm_bwd(dres, pre, post, name):
    t, d = dres.shape
    tt = _rows(t, 512)
    nt = t // tt
    row = pl.BlockSpec((tt, d), lambda i: (i, 0))
    vec = pl.BlockSpec((1, d), lambda i: (0, 0))
    has_pre, has_post = pre is not None, post is not None
    n_in = 1 + (3 if has_pre else 0) + (2 if has_post else 0)
    n_out = has_pre + has_post + has_pre + has_post

    def body(*refs):
        ins, outs, scr = refs[:n_in], refs[n_in:n_in + n_out], refs[n_in + n_out:]
        i = pl.program_id(0)
        dx = ins[0][...]
        pos, opos, spos = 1, 0, 0
        accs = []
        if has_pre:
            dh_ref, xa_ref, ga_ref = ins[pos:pos + 3]
            pos += 3
            dxa, dga_t = _rms_bwd(dh_ref[...], xa_ref[...], ga_ref[...])
            dx = dx + dxa
            outs[opos][...] = dx
            opos += 1
            accs.append((scr[spos], dga_t))
            spos += 1
        if has_post:
            zb_ref, gb_ref = ins[pos:pos + 2]
            dz, dgb_t = _rms_bwd(dx, zb_ref[...], gb_ref[...])
            outs[opos][...] = dz.astype(BF16)
            opos += 1
            accs.append((scr[spos], dgb_t))
            spos += 1
        for (acc, val), out in zip(accs, outs[opos:]):
            part = _sum8(val)

            @pl.when(i == 0)
            def _(acc=acc, part=part):
                acc[...] = part

            @pl.when(i > 0)
            def _(acc=acc, part=part):
                acc[...] += part

            @pl.when(i == nt - 1)
            def _(acc=acc, out=out):
                out[...] = jnp.sum(acc[...], axis=0, keepdims=True)

    in_specs, args = [row], [dres]
    out_specs, out_shape = [], []
    if has_pre:
        in_specs += [row, row, vec]
        args += list(pre)
        out_specs.append(row)
        out_shape.append(jax.ShapeDtypeStruct((t, d), F32))
    if has_post:
        in_specs += [row, vec]
        args += list(post)
        out_specs.append(row)
        out_shape.append(jax.ShapeDtypeStruct((t, d), BF16))
    for _ in range(has_pre + has_post):
        out_specs.append(vec)
        out_shape.append(jax.ShapeDtypeStruct((1, d), F32))
    return pl.pallas_call(
        body, grid=(nt,), in_specs=in_specs, out_specs=out_specs, out_shape=out_shape,
        scratch_shapes=[pltpu.VMEM((SUBLANE, d), F32)] * (has_pre + has_post),
        compiler_params=_cparams(("arbitrary",)), name=name,
    )(*args)


def _loss_grad(y, target, name):
    t, d = y.shape
    tt = _rows(t, 512)
    nt = t // tt
    row = pl.BlockSpec((tt, d), lambda i: (i, 0))
    inv_d = 1.0 / d

    def body(y_ref, t_ref, dy_ref, l_ref):
        i = pl.program_id(0)
        diff = y_ref[...] - t_ref[...]
        dy_ref[...] = diff * inv_d
        s8 = _sum8(diff * diff)
        part = s8[:, 0:LANE]
        for k in range(1, d // LANE):
            part = part + s8[:, k * LANE:(k + 1) * LANE]
        part = part * (0.5 * inv_d)

        @pl.when(i == 0)
        def _():
            l_ref[...] = part

        @pl.when(i > 0)
        def _():
            l_ref[...] += part

    return pl.pallas_call(
        body, grid=(nt,), in_specs=[row, row],
        out_specs=[row, pl.BlockSpec((SUBLANE, LANE), lambda i: (0, 0))],
        out_shape=[jax.ShapeDtypeStruct((t, d), F32), jax.ShapeDtypeStruct((SUBLANE, LANE), F32)],
        compiler_params=_cparams(("arbitrary",)), name=name,
    )(y, target)


def _swiglu_fwd(a, b, name):
    t, f = a.shape
    tt = _rows(t, 256)
    blk = pl.BlockSpec((tt, f), lambda i: (i, 0))

    def body(a_ref, b_ref, m_ref):
        av = a_ref[...].astype(F32)
        m_ref[...] = (av * _sigmoid(av) * b_ref[...].astype(F32)).astype(BF16)

    return pl.pallas_call(
        body, grid=(t // tt,), in_specs=[blk, blk], out_specs=blk,
        out_shape=jax.ShapeDtypeStruct((t, f), BF16), compiler_params=_cparams(("parallel",)), name=name,
    )(a, b)


def _swiglu_bwd(a, b, dm, name):
    t, f = a.shape
    tt = _rows(t, 256)
    blk = pl.BlockSpec((tt, f), lambda i: (i, 0))

    def body(a_ref, b_ref, dm_ref, da_ref, db_ref):
        av = a_ref[...].astype(F32)
        s = _sigmoid(av)
        dv = dm_ref[...].astype(F32)
        da_ref[...] = (dv * b_ref[...].astype(F32) * s * (1.0 + av * (1.0 - s))).astype(BF16)
        db_ref[...] = (dv * av * s).astype(BF16)

    return pl.pallas_call(
        body, grid=(t // tt,), in_specs=[blk, blk, blk], out_specs=[blk, blk],
        out_shape=[jax.ShapeDtypeStruct((t, f), BF16)] * 2, compiler_params=_cparams(("parallel",)), name=name,
    )(a, b, dm)


def _log_sigmoid(x):
    return jnp.minimum(x, 0.0) - jnp.log1p(jnp.exp(-jnp.abs(x)))


def _fox_prep(f_t, b_f, name):
    h, t = f_t.shape

    def body(f_ref, b_ref, c_ref):
        r = lax.broadcasted_iota(jnp.int32, (LANE, LANE), 0)
        c = lax.broadcasted_iota(jnp.int32, (LANE, LANE), 1)
        upper = (r <= c).astype(F32)
        carry = jnp.zeros((h, 1), F32)
        for j in range(t // LANE):
            sl = slice(j * LANE, (j + 1) * LANE)
            lf = _log_sigmoid(f_ref[:, sl] + b_ref[...])
            cs = jnp.dot(lf, upper, precision=lax.Precision.HIGHEST, preferred_element_type=F32) + carry
            c_ref[:, sl] = cs
            carry = cs[:, LANE - 1:LANE]

    return pl.pallas_call(body, out_shape=jax.ShapeDtypeStruct((h, t), F32), compiler_params=_cparams(), name=name)(f_t, b_f)


def _fox_bwd(dc_q, dc_k, f_t, b_f, name):
    h, t = f_t.shape

    def body(dq_ref, dk_ref, f_ref, b_ref, df_ref, db_ref):
        r = lax.broadcasted_iota(jnp.int32, (LANE, LANE), 0)
        c = lax.broadcasted_iota(jnp.int32, (LANE, LANE), 1)
        lower = (r >= c).astype(F32)
        carry = jnp.zeros((h, 1), F32)
        dbsum = jnp.zeros((h, 1), F32)
        for j in reversed(range(t // LANE)):
            sl = slice(j * LANE, (j + 1) * LANE)
            dc = dq_ref[:, sl] - dk_ref[:, sl]
            dl = jnp.dot(dc, lower, precision=lax.Precision.HIGHEST, preferred_element_type=F32) + carry
            carry = dl[:, 0:1]
            df = dl * _sigmoid(-(f_ref[:, sl] + b_ref[...]))
            df_ref[:, sl] = df
            dbsum = dbsum + jnp.sum(df, axis=-1, keepdims=True)
        db_ref[...] = dbsum

    return pl.pallas_call(
        body, out_shape=[jax.ShapeDtypeStruct((h, t), F32), jax.ShapeDtypeStruct((h, 1), F32)],
        compiler_params=_cparams(), name=name,
    )(dc_q, dc_k, f_t, b_f)


ATTN_FWD = (1024, 512)
ATTN_BWD = (512, 512)


def _attn_tiles(t, tiles):
    return _tile(t, tiles[0]), _tile(t, tiles[1])


def _attn_fwd(proj, c_t, d, name, job=None):
    t = proj.shape[0]
    h = d // LANE
    bq, bk = _attn_tiles(t, ATTN_FWD)
    nq, nk, rr = t // bq, t // bk, bq // bk
    qc, kc, vc = COL_Q * h, COL_K * h, COL_VA * h
    qscale = LANE ** -0.5 * LOG2E

    def body(q_ref, k_ref, v_ref, cc_ref, cr_ref, o_ref, lse_ref, kb, vt, ckb, acc):
        i = pl.program_id(1)

        @pl.when(i == 0)
        def _():
            kb[...] = k_ref[...].astype(BF16)
            ckb[...] = jnp.broadcast_to(cc_ref[0] * LOG2E, (t, bq))
            for jn in range(nk):
                vt[jn] = v_ref[jn * bk:(jn + 1) * bk, :].astype(F32).T.astype(BF16)

        q = (q_ref[...].astype(F32) * qscale).astype(BF16)
        cq = cr_ref[0, 0] * LOG2E
        acc[...] = jnp.zeros((LANE, bq), F32)

        def block(j, diag, m_old, l_old):
            rows = pl.ds(pl.multiple_of(j * bk, bk), bk)
            s = _nt_dot(kb[rows, :], q) - ckb[rows, :]
            if diag is not None:
                kk = lax.broadcasted_iota(jnp.int32, (bk, bq), 0)
                qq = lax.broadcasted_iota(jnp.int32, (bk, bq), 1)
                s = jnp.where(qq >= kk + diag * bk, s, NEG)
            m_new = jnp.maximum(m_old, jnp.max(s, axis=0, keepdims=True) + cq)
            p = jnp.exp2(s + (cq - m_new))
            alpha = jnp.exp2(m_old - m_new)
            l_new = alpha * l_old + jnp.sum(p, axis=0, keepdims=True)
            acc[...] = alpha * acc[...] + jnp.dot(vt[j], p.astype(BF16), preferred_element_type=F32)
            return m_new, l_new

        m, l = lax.fori_loop(0, i * rr, lambda j, c: block(j, None, *c),
                             (jnp.full((1, bq), NEG, F32), jnp.zeros((1, bq), F32)))
        for jj in range(rr):
            m, l = block(i * rr + jj, jj, m, l)
        o_ref[...] = (acc[...] / l).T
        lse_ref[0, 0] = m + jnp.log2(l)

    rowq = pl.BlockSpec((1, 1, 1, bq), lambda hh, i: (hh, i, 0, 0))
    outs, moved = _call(
        body, grid=(h, nq),
        in_specs=[
            pl.BlockSpec((bq, LANE), lambda hh, i: (i, qc + hh)),
            pl.BlockSpec((t, LANE), lambda hh, i: (0, kc + hh)),
            pl.BlockSpec((t, LANE), lambda hh, i: (0, vc + hh)),
            pl.BlockSpec((1, t, 1), lambda hh, i: (hh, 0, 0)),
            rowq,
        ],
        out_specs=[pl.BlockSpec((bq, LANE), lambda hh, i: (i, hh)), rowq],
        out_shape=[jax.ShapeDtypeStruct((t, d), F32), jax.ShapeDtypeStruct((h, nq, 1, bq), F32)],
        scratch_shapes=[pltpu.VMEM((t, LANE), BF16), pltpu.VMEM((nk, LANE, bk), BF16), pltpu.VMEM((t, bq), F32),
                        pltpu.VMEM((LANE, bq), F32)],
        dims=("arbitrary", "arbitrary"), name=name,
        args=[proj, proj, proj, c_t.reshape(h, t, 1), c_t.reshape(h, nq, 1, bq)], job=job)
    outs = [outs[0], outs[1].reshape(h, t)]
    return outs if job is None else (outs, moved)


def _attn_bwd(proj, do, o, lse, c_t, d, name, job=None):
    t = proj.shape[0]
    h = d // LANE
    bq, bk = _attn_tiles(t, ATTN_BWD)
    nq, nk, rr = t // bq, t // bk, bq // bk
    qc, kc, vc = COL_Q * h, COL_K * h, COL_VA * h
    scale = LANE ** -0.5

    def body(q_ref, k_ref, v_ref, do_ref, o_ref, lse_ref, cc_ref, cr_ref, dq_ref, dk_ref, dv_ref, dcq_ref, dck_ref,
             kb, kt, vb, ckb, dk_acc, dv_acc, dck_acc, dqt_acc):
        i = pl.program_id(1)

        @pl.when(i == 0)
        def _():
            kb[...] = k_ref[...].astype(BF16)
            vb[...] = v_ref[...].astype(BF16)
            ckb[...] = jnp.broadcast_to(cc_ref[0] * LOG2E, (t, bq))
            for jn in range(nk):
                kt[jn] = k_ref[jn * bk:(jn + 1) * bk, :].astype(F32).T.astype(BF16)
            dk_acc[...] = jnp.zeros((t, LANE), F32)
            dv_acc[...] = jnp.zeros((t, LANE), F32)
            dck_acc[...] = jnp.zeros((t, LANE), F32)

        q = (q_ref[...].astype(F32) * (scale * LOG2E)).astype(BF16)
        dof = do_ref[...]
        dob = dof.astype(BF16)
        delta = jnp.sum((dof * o_ref[...]).T, axis=0, keepdims=True)
        rowb = cr_ref[0, 0] * LOG2E - lse_ref[0, 0]
        dqt_acc[...] = jnp.zeros((LANE, bq), F32)

        def block(j, diag, dcq):
            rows = pl.ds(pl.multiple_of(j * bk, bk), bk)
            p = jnp.exp2(_nt_dot(kb[rows, :], q) - ckb[rows, :] + rowb)
            if diag is not None:
                kk = lax.broadcasted_iota(jnp.int32, (bk, bq), 0)
                qq = lax.broadcasted_iota(jnp.int32, (bk, bq), 1)
                p = jnp.where(qq >= kk + diag * bk, p, 0.0)
            dv_acc[rows, :] += jnp.dot(p.astype(BF16), dob, preferred_element_type=F32)
            ds = p * (_nt_dot(vb[rows, :], dob) - delta)
            dsb = ds.astype(BF16)
            dk_acc[rows, :] += jnp.dot(dsb, q, preferred_element_type=F32)
            dqt_acc[...] += jnp.dot(kt[j], dsb, preferred_element_type=F32)
            part = ds[:, 0:LANE]
            for k in range(1, bq // LANE):
                part = part + ds[:, k * LANE:(k + 1) * LANE]
            dck_acc[rows, :] += part
            return dcq + jnp.sum(ds, axis=0, keepdims=True)

        dcq = lax.fori_loop(0, i * rr, lambda j, c: block(j, None, c), jnp.zeros((1, bq), F32))
        for jj in range(rr):
            dcq = block(i * rr + jj, jj, dcq)
        dq_ref[...] = (dqt_acc[...] * scale).T.astype(BF16)
        dcq_ref[0, 0] = dcq

        @pl.when(i == nq - 1)
        def _():
            dk_ref[...] = (dk_acc[...] * LN2).astype(BF16)
            dv_ref[...] = dv_acc[...].astype(BF16)
            dck_ref[0] = jnp.sum(dck_acc[...], axis=-1, keepdims=True)

    rowq = pl.BlockSpec((1, 1, 1, bq), lambda hh, i: (hh, i, 0, 0))
    blk = pl.BlockSpec((bq, LANE), lambda hh, i: (i, hh))
    whole = pl.BlockSpec((t, LANE), lambda hh, i: (0, hh))
    colk = pl.BlockSpec((1, t, 1), lambda hh, i: (hh, 0, 0))
    outs, moved = _call(
        body, grid=(h, nq),
        in_specs=[
            pl.BlockSpec((bq, LANE), lambda hh, i: (i, qc + hh)),
            pl.BlockSpec((t, LANE), lambda hh, i: (0, kc + hh)),
            pl.BlockSpec((t, LANE), lambda hh, i: (0, vc + hh)),
            blk, blk, rowq, colk, rowq,
        ],
        out_specs=[blk, whole, whole, rowq, colk],
        out_shape=[jax.ShapeDtypeStruct((t, d), BF16), jax.ShapeDtypeStruct((t, d), BF16), jax.ShapeDtypeStruct((t, d), BF16),
                   jax.ShapeDtypeStruct((h, nq, 1, bq), F32), jax.ShapeDtypeStruct((h, t, 1), F32)],
        scratch_shapes=[pltpu.VMEM((t, LANE), BF16), pltpu.VMEM((nk, LANE, bk), BF16), pltpu.VMEM((t, LANE), BF16),
                        pltpu.VMEM((t, bq), F32), pltpu.VMEM((t, LANE), F32), pltpu.VMEM((t, LANE), F32),
                        pltpu.VMEM((t, LANE), F32), pltpu.VMEM((LANE, bq), F32)],
        dims=("arbitrary", "arbitrary"), name=name,
        args=[proj, proj, proj, do, o, lse.reshape(h, nq, 1, bq), c_t.reshape(h, t, 1), c_t.reshape(h, nq, 1, bq)], job=job)
    outs = list(outs[:3]) + [outs[3].reshape(h, t), outs[4].reshape(h, t)]
    return outs if job is None else (outs, moved)


def _sgu_forward(u_ref, v_ref, gv_ref, wm_ref, bs_ref, mix_sc, groups):
    gu, dgu = _gelu_and_grad(u_ref[...].astype(F32))
    gvv, dgv = _gelu_and_grad(v_ref[...].astype(F32))
    mu = jnp.mean(gvv, axis=-1, keepdims=True)
    xc = gvv - mu
    r = lax.rsqrt(jnp.mean(xc * xc, axis=-1, keepdims=True) + EPS)
    nhat = xc * r
    vn = (nhat * gv_ref[...]).astype(BF16)
    for g in range(groups):
        sl = slice(g * LANE, (g + 1) * LANE)
        mix_sc[:, sl] = jnp.dot(wm_ref[g], vn[:, sl], preferred_element_type=F32) + bs_ref[g]
    return gu, dgu, dgv, nhat, r, vn, mix_sc[...]


def _mix_fwd(proj, o, wm, bs, g_v, d, name):
    t = proj.shape[0]
    groups = d // LANE

    def body(u_ref, v_ref, ga_ref, gb_ref, o_ref, wm_ref, bs_ref, gv_ref, out_ref, mix_sc):
        gu, _, _, _, _, _, mixed = _sgu_forward(u_ref, v_ref, gv_ref, wm_ref, bs_ref, mix_sc, groups)
        out_ref[...] = (_sigmoid(ga_ref[...].astype(F32)) * (gu * mixed) + _sigmoid(gb_ref[...].astype(F32)) * o_ref[...]).astype(BF16)

    def colblk(k):
        return pl.BlockSpec((LANE, d), lambda i, k=k: (i, k))

    full3 = pl.BlockSpec((groups, LANE, LANE), lambda i: (0, 0, 0))
    return pl.pallas_call(
        body, grid=(t // LANE,),
        in_specs=[colblk(COL_U), colblk(COL_V), colblk(COL_GA), colblk(COL_GB), colblk(0), full3,
                  pl.BlockSpec((groups, LANE, 1), lambda i: (0, 0, 0)), pl.BlockSpec((1, d), lambda i: (0, 0))],
        out_specs=colblk(0),
        out_shape=jax.ShapeDtypeStruct((t, d), BF16),
        scratch_shapes=[pltpu.VMEM((LANE, d), F32)],
        compiler_params=_cparams(("parallel",)), name=name,
    )(proj, proj, proj, proj, o, wm, bs, g_v)


def _mix_bwd(dmerged, proj, o, wm, wm_t, bs, g_v, d, name):
    t = proj.shape[0]
    groups = d // LANE
    nt = t // LANE

    def body(dm_ref, u_ref, v_ref, ga_ref, gb_ref, o_ref, wm_ref, wmt_ref, bs_ref, gv_ref,
             duv_ref, dg_ref, do_ref, dws_ref, dbs_ref, dgv_ref, mix_sc, dvn_sc, gv_acc):
        i = pl.program_id(0)

        @pl.when(i == 0)
        def _():
            dws_ref[...] = jnp.zeros_like(dws_ref)
            dbs_ref[...] = jnp.zeros_like(dbs_ref)
            gv_acc[...] = jnp.zeros_like(gv_acc)

        gu, dgu, dgv, nhat, r, vn, mixed = _sgu_forward(u_ref, v_ref, gv_ref, wm_ref, bs_ref, mix_sc, groups)
        dm = dm_ref[...]
        sa = _sigmoid(ga_ref[...].astype(F32))
        sb = _sigmoid(gb_ref[...].astype(F32))
        ov = o_ref[...]
        y_a = gu * mixed
        dg_ref[:, 0:d] = (dm * y_a * sa * (1.0 - sa)).astype(BF16)
        dg_ref[:, d:2 * d] = (dm * ov * sb * (1.0 - sb)).astype(BF16)
        do_ref[...] = dm * sb
        dy_a = dm * sa
        duv_ref[:, 0:d] = (dy_a * mixed * dgu).astype(BF16)
        dmixed = dy_a * gu
        dmixed_b = dmixed.astype(BF16)
        for g in range(groups):
            sl = slice(g * LANE, (g + 1) * LANE)
            dvn_sc[:, sl] = jnp.dot(wmt_ref[g], dmixed_b[:, sl], preferred_element_type=F32)
            dws_ref[g] += _nt_dot(dmixed_b[:, sl], vn[:, sl])
            dbs_ref[g] += jnp.sum(dmixed[:, sl], axis=-1, keepdims=True)
        dvn = dvn_sc[...]
        gv_acc[...] += _sum8(dvn * nhat)
        dn = dvn * gv_ref[...]
        dgelu = r * (dn - jnp.mean(dn, axis=-1, keepdims=True) - nhat * jnp.mean(dn * nhat, axis=-1, keepdims=True))
        duv_ref[:, d:2 * d] = (dgelu * dgv).astype(BF16)

        @pl.when(i == nt - 1)
        def _():
            dgv_ref[...] = jnp.sum(gv_acc[...], axis=0, keepdims=True)
            rr = lax.broadcasted_iota(jnp.int32, (LANE, LANE), 0)
            cl = lax.broadcasted_iota(jnp.int32, (LANE, LANE), 1)
            for g in range(groups):
                dws_ref[g] = jnp.where(rr >= cl, dws_ref[g], 0.0)

    def colblk(k):
        return pl.BlockSpec((LANE, d), lambda i, k=k: (i, k))

    full3 = pl.BlockSpec((groups, LANE, LANE), lambda i: (0, 0, 0))
    col3 = pl.BlockSpec((groups, LANE, 1), lambda i: (0, 0, 0))
    vec = pl.BlockSpec((1, d), lambda i: (0, 0))
    two = pl.BlockSpec((LANE, 2 * d), lambda i: (i, 0))
    return pl.pallas_call(
        body, grid=(nt,),
        in_specs=[colblk(0), colblk(COL_U), colblk(COL_V), colblk(COL_GA), colblk(COL_GB), colblk(0), full3, full3, col3, vec],
        out_specs=[two, two, colblk(0), full3, col3, vec],
        out_shape=[jax.ShapeDtypeStruct((t, 2 * d), BF16), jax.ShapeDtypeStruct((t, 2 * d), BF16), jax.ShapeDtypeStruct((t, d), F32),
                   jax.ShapeDtypeStruct((groups, LANE, LANE), F32), jax.ShapeDtypeStruct((groups, LANE, 1), F32),
                   jax.ShapeDtypeStruct((1, d), F32)],
        scratch_shapes=[pltpu.VMEM((LANE, d), F32), pltpu.VMEM((LANE, d), F32), pltpu.VMEM((SUBLANE, d), F32)],
        compiler_params=_cparams(("arbitrary",)), name=name,
    )(dmerged, proj, proj, proj, proj, o, wm, wm_t, bs, g_v)


def _adam_math(w, g, m, v):
    nm = ADAM_B1 * m + (1.0 - ADAM_B1) * g
    nv = ADAM_B2 * v + (1.0 - ADAM_B2) * (g * g)
    delta = -ADAM_LR * ((nm * ADAM_C1) / (jnp.sqrt(nv * ADAM_C2) + ADAM_EPS) + ADAM_WD * w)
    return delta, nm, nv


def _adamw(w, g, m, v, name):
    r, c = w.shape
    cap = max(SUBLANE, (2 * 1024 * 1024) // (4 * c) // SUBLANE * SUBLANE)
    tr = _rows(r, cap)

    def body(w_ref, g_ref, m_ref, v_ref, d_ref, nm_ref, nv_ref):
        d_ref[...], nm_ref[...], nv_ref[...] = _adam_math(w_ref[...], g_ref[...], m_ref[...], v_ref[...])

    blk = pl.BlockSpec((tr, c), lambda i: (i, 0))
    return pl.pallas_call(
        body, grid=(r // tr,), in_specs=[blk] * 4, out_specs=[blk] * 3,
        out_shape=[jax.ShapeDtypeStruct((r, c), F32)] * 3, compiler_params=_cparams(("parallel",)), name=name,
    )(w, g, m, v)


def _adamw_layers(w, g0, g1, m, v, name):
    _, r, c = w.shape
    cap = max(SUBLANE, (1024 * 1024) // (4 * c) // SUBLANE * SUBLANE)
    tr = _rows(r, cap)

    def body(w_ref, g0_ref, g1_ref, m_ref, v_ref, g_ref, d_ref, nm_ref, nv_ref):
        gg = jnp.where(pl.program_id(0) == 0, g0_ref[...], g1_ref[...])
        g_ref[0] = gg
        d_ref[0], nm_ref[0], nv_ref[0] = _adam_math(w_ref[0], gg, m_ref[0], v_ref[0])

    lay = pl.BlockSpec((1, tr, c), lambda l, i: (l, i, 0))

    def gspec(l0):
        return pl.BlockSpec((tr, c), lambda l, i: (jnp.where(l == l0, i, 0), 0))

    return pl.pallas_call(
        body, grid=(2, r // tr), in_specs=[lay, gspec(0), gspec(1), lay, lay], out_specs=[lay] * 4,
        out_shape=[jax.ShapeDtypeStruct((2, r, c), F32)] * 4, compiler_params=_cparams(("arbitrary", "arbitrary")), name=name,
    )(w, g0, g1, m, v)


def _adamw_interleaved(w, g0, g1, m, v, name):
    r, _, c = w.shape
    tr = 128

    def body(w_ref, g0_ref, g1_ref, m_ref, v_ref, g_ref, d_ref, nm_ref, nv_ref):
        for l, gl in enumerate((g0_ref, g1_ref)):
            gg = gl[...]
            g_ref[:, l, :] = gg
            d_ref[:, l, :], nm_ref[:, l, :], nv_ref[:, l, :] = _adam_math(w_ref[:, l, :], gg, m_ref[:, l, :], v_ref[:, l, :])

    lay = pl.BlockSpec((tr, 2, c), lambda i: (i, 0, 0))
    flat = pl.BlockSpec((tr, c), lambda i: (i, 0))
    return pl.pallas_call(
        body, grid=(pl.cdiv(r, tr),), in_specs=[lay, flat, flat, lay, lay], out_specs=[lay] * 4,
        out_shape=[jax.ShapeDtypeStruct((r, 2, c), F32)] * 4, compiler_params=_cparams(("parallel",)), name=name,
    )(w, g0, g1, m, v)


def _add_half(p4, recv, c_idx, name):
    _, r, c = p4.shape
    hw = c // 2
    tr = 256 if r % 256 == 0 else r

    def body(c_ref, a_ref, b_ref, o_ref):
        o_ref[...] = (a_ref[...].astype(F32) + b_ref[...].astype(F32)).astype(BF16)

    return pl.pallas_call(
        body,
        grid_spec=pltpu.PrefetchScalarGridSpec(
            num_scalar_prefetch=1, grid=(N_CHIPS, pl.cdiv(r, tr)),
            in_specs=[pl.BlockSpec((1, tr, hw), lambda s, i, cr: (s, i, cr[0])), pl.BlockSpec((1, tr, hw), lambda s, i, cr: (s, i, 0))],
            out_specs=pl.BlockSpec((1, tr, hw), lambda s, i, cr: (s, i, 0)),
        ),
        out_shape=jax.ShapeDtypeStruct((N_CHIPS, r, hw), BF16), compiler_params=_cparams(("parallel", "parallel")), name=name,
    )(c_idx, p4, recv)


def _sum_slots(x, own, sel, name, out_cols=None):
    s, r, c = x.shape
    tr = 128 if r % 128 == 0 else r

    def body(sel_ref, x_ref, own_ref, o_ref):
        mine = own_ref[0].astype(F32)
        acc = jnp.zeros((tr, c), F32)
        for k in range(s):
            acc = acc + jnp.where(sel_ref[0] == k, mine, x_ref[k].astype(F32))
        o_ref[...] = acc

    return pl.pallas_call(
        body,
        grid_spec=pltpu.PrefetchScalarGridSpec(
            num_scalar_prefetch=1, grid=(pl.cdiv(r, tr),),
            in_specs=[pl.BlockSpec((s, tr, c), lambda i, sr: (0, i, 0)), pl.BlockSpec((1, tr, c), lambda i, sr: (sr[1], i, 0))],
            out_specs=pl.BlockSpec((tr, c), lambda i, sr: (i, sr[2])),
        ),
        out_shape=jax.ShapeDtypeStruct((r, out_cols or c), F32), compiler_params=_cparams(("parallel",)), name=name,
    )(sel, x, own)


def _half_cols(width, hc):
    hw = width // 2
    assert hw % LANE == 0
    return pl.ds(pl.multiple_of(hc * hw, LANE), hw)


def _remote(src, dst, ssem, rsem, k, to):
    return pltpu.make_async_remote_copy(src_ref=src, dst_ref=dst, send_sem=ssem.at[k], recv_sem=rsem.at[k], device_id=to,
                                        device_id_type=MESH)


def _gather_job(bufs, mid_at=0.5):
    def part(o, a, slot, hc):
        return o[a].at[slot, :, _half_cols(bufs[a].shape[2], hc)]

    def first(ins, o, fresh, ssem, rsem):
        x, y, c, chips = _place()
        for a in range(len(bufs)):
            mine = part(o, a, 2 * x + y, c)
            for j, chip in enumerate(chips):
                _remote(mine, mine, ssem, rsem, 6 * a + j, (chip[0], chip[1], c)).start()

    def mid(ins, o, fresh, ssem, rsem):
        x, y, c, chips = _place()
        for a in range(len(bufs)):
            for j, chip in enumerate(chips):
                got = part(o, a, 2 * chip[0] + chip[1], c)
                _remote(got, got, ssem, rsem, 6 * a + j, (x, y, c)).wait_recv()
                _remote(got, got, ssem, rsem, 6 * a + 3 + j, (x, y, 1 - c)).start()

    def last(ins, o, fresh, ssem, rsem):
        x, y, c, chips = _place()
        for a in range(len(bufs)):
            for j, chip in enumerate(chips):
                got = part(o, a, 2 * chip[0] + chip[1], 1 - c)
                _remote(got, got, ssem, rsem, 6 * a + 3 + j, (x, y, c)).wait_recv()
        for a in range(len(bufs)):
            mine = part(o, a, 2 * x + y, c)
            for j, chip in enumerate(chips):
                _remote(mine, mine, ssem, rsem, 6 * a + j, (x, y, c)).wait_send()
                passed = part(o, a, 2 * chip[0] + chip[1], c)
                _remote(passed, passed, ssem, rsem, 6 * a + 3 + j, (x, y, c)).wait_send()

    return _Job([], bufs, [], 6 * len(bufs), first, mid, last, mid_at)


def _swap_job(p4s):
    def pairs(ins, fresh, c):
        return [(a, s, ins[a].at[s, :, _half_cols(p4s[a].shape[2], 1 - c)], fresh[a].at[s])
                for a in range(len(p4s)) for s in range(N_CHIPS)]

    def first(ins, inout, fresh, ssem, rsem):
        x, y, c, _ = _place()
        for a, s, src, dst in pairs(ins, fresh, c):
            _remote(src, dst, ssem, rsem, N_CHIPS * a + s, (x, y, 1 - c)).start()

    def last(ins, inout, fresh, ssem, rsem):
        x, y, c, _ = _place()
        for a, s, src, dst in pairs(ins, fresh, c):
            _remote(src, dst, ssem, rsem, N_CHIPS * a + s, (x, y, 1 - c)).wait()

    fresh = [jax.ShapeDtypeStruct(p.shape[:2] + (p.shape[2] // 2,), p.dtype) for p in p4s]
    return _Job(p4s, [], fresh, N_CHIPS * len(p4s), first, None, last)


def _scatter_job(parts):
    def first(ins, inout, fresh, ssem, rsem):
        x, y, c, chips = _place()
        for a in range(len(parts)):
            for j, chip in enumerate(chips):
                _remote(ins[a].at[2 * chip[0] + chip[1]], fresh[a].at[2 * x + y], ssem, rsem, 3 * a + j, (chip[0], chip[1], c)).start()

    def last(ins, inout, fresh, ssem, rsem):
        x, y, c, chips = _place()
        for a in range(len(parts)):
            for j, chip in enumerate(chips):
                slot = 2 * chip[0] + chip[1]
                _remote(ins[a].at[slot], fresh[a].at[slot], ssem, rsem, 3 * a + j, (x, y, c)).wait()

    return _Job(parts, [], [jax.ShapeDtypeStruct(p.shape, p.dtype) for p in parts], 3 * len(parts), first, None, last)


def _share_job(gs):
    def halves(o, a, c):
        width = gs[a].shape[1]
        return o[a].at[:, _half_cols(width, c)], o[a].at[:, _half_cols(width, 1 - c)]

    def first(ins, o, fresh, ssem, rsem):
        x, y, c, _ = _place()
        for a in range(len(gs)):
            mine, _ = halves(o, a, c)
            _remote(mine, mine, ssem, rsem, a, (x, y, 1 - c)).start()

    def last(ins, o, fresh, ssem, rsem):
        x, y, c, _ = _place()
        for a in range(len(gs)):
            mine, theirs = halves(o, a, c)
            _remote(mine, theirs, ssem, rsem, a, (x, y, 1 - c)).wait()

    return _Job([], gs, [], len(gs), first, None, last)


def _gather_all_job(buf):
    def peers():
        x, y, c, _ = _place()
        flips = [(fx, fy, fc) for fx in (0, 1) for fy in (0, 1) for fc in (0, 1)][1:]
        return (x, y, c), [((1 - x) if fx else x, (1 - y) if fy else y, (1 - c) if fc else c) for fx, fy, fc in flips]

    def first(ins, inout, fresh, ssem, rsem):
        (x, y, c), others = peers()
        for k, peer in enumerate(others):
            _remote(ins[0], fresh[0].at[4 * x + 2 * y + c], ssem, rsem, k, peer).start()

    def last(ins, inout, fresh, ssem, rsem):
        me, others = peers()
        for k, peer in enumerate(others):
            _remote(ins[0], fresh[0].at[4 * peer[0] + 2 * peer[1] + peer[2]], ssem, rsem, k, me).wait()

    return _Job([buf], [], [jax.ShapeDtypeStruct((N_DEV,) + buf.shape, buf.dtype)], N_DEV - 1, first, None, last)


class _SemView:
    def __init__(self, sems, off):
        self.sems, self.off = sems, off

    @property
    def at(self):
        return self

    def __getitem__(self, k):
        return self.sems.at[k + self.off]


def _join(jobs):
    spans, pos = [], [0, 0, 0, 0]
    for j in jobs:
        nxt = [pos[0] + len(j.ins), pos[1] + len(j.inout), pos[2] + len(j.fresh), pos[3] + j.nsem]
        spans.append((pos, nxt))
        pos = nxt

    def hook(which):
        fns = [getattr(j, which) for j in jobs]
        if all(f is None for f in fns):
            return None

        def run(ins, inout, fresh, ssem, rsem):
            for fn, (lo, hi) in zip(fns, spans):
                if fn is not None:
                    fn(ins[lo[0]:hi[0]], inout[lo[1]:hi[1]], fresh[lo[2]:hi[2]], _SemView(ssem, lo[3]), _SemView(rsem, lo[3]))

        return run

    mids = [j.mid_at for j in jobs if j.mid is not None]
    joined = _Job([a for j in jobs for a in j.ins], [a for j in jobs for a in j.inout], [a for j in jobs for a in j.fresh],
                  pos[3], hook("first"), hook("mid"), hook("last"), max(mids) if mids else 0.5)
    n_io = pos[1]

    def split(moved):
        return [list(moved[lo[1]:hi[1]]) + list(moved[n_io + lo[2]:n_io + hi[2]]) for lo, hi in spans]

    return joined, split


def _carrying(stages, call):
    stages = [s for s in stages if s is not None]
    if not stages:
        return call(None)
    job, split = _join([s[0] for s in stages])
    out, moved = call(job)
    for (_, done), part in zip(stages, split(moved)):
        done(part)
    return out


def _layer_forward(x, h, w_in_t, rest, sm, d, stages=None):
    stages = stages or {}
    proj = _carrying([stages.get("proj")], lambda job: _matmul(h, w_in_t, "nt", BF16, "proj_fwd", n=7 * d, tn_cap=1792, job=job))
    f_t = _matmul(w_in_t[7 * d:], h, "nt", F32, "forget_fwd", tn_cap=1024)
    c_t = _fox_prep(f_t, sm["b_f"], "fox_prep")
    o, lse = _carrying([stages.get("attn")], lambda job: _attn_fwd(proj, c_t, d, "attn_fwd", job=job))
    wts = rest()
    merged = _mix_fwd(proj, o, sm["wm"], sm["bs"], sm["g_v"], d, "mix_fwd")
    z = _matmul(merged, wts["w_out"], "nn", F32, "out_fwd")
    x1, h2 = _norm_fwd(x, z, sm["g_post"], sm["g_fpre"], "norm_mid")
    a = _carrying([stages.get("gate")], lambda job: _matmul(h2, wts["w_g_t"], "nt", BF16, "gate_fwd", tn_cap=1408, job=job))
    b = _carrying([stages.get("up")], lambda job: _matmul(h2, wts["w_u_t"], "nt", BF16, "up_fwd", tn_cap=1408, job=job))
    mm = _swiglu_fwd(a, b, "swiglu_fwd")
    z2 = _carrying([stages.get("down")], lambda job: _matmul(mm, wts["w_d"], "nn", F32, "down_fwd", job=job))
    return dict(x=x, h=h, proj=proj, f_t=f_t, c_t=c_t, o=o, lse=lse, merged=merged, z=z, x1=x1,
                h2=h2, a=a, b=b, mm=mm, z2=z2)


class _GradExchange:
    def __init__(self, pay, keys, c_idx, chip):
        self.keys = list(keys)
        self.p4 = [pay[k].reshape(N_CHIPS, pay[k].shape[1] // N_CHIPS, pay[k].shape[2]) for k in self.keys]
        self.c_idx = c_idx
        self.sel = jnp.stack([chip, chip, c_idx[0]]).astype(jnp.int32)
        self.done = 0

    def _after_swap(self, landed):
        self.parts = [_add_half(p, r, self.c_idx, "add_sibling") for p, r in zip(self.p4, landed)]
        self.done = 1

    def _after_scatter(self, landed):
        self.g = [_sum_slots(got, sent, self.sel, "sum_chips", out_cols=p.shape[2])
                  for got, sent, p in zip(landed, self.parts, self.p4)]
        self.done = 2

    def _after_share(self, moved):
        self.g = list(moved)
        self.done = 3

    def stage(self):
        if self.done == 0:
            return _swap_job(self.p4), self._after_swap
        if self.done == 1:
            return _scatter_job(self.parts), self._after_scatter
        if self.done == 2:
            return _share_job(self.g), self._after_share
        return None

    def run(self):
        for name in ("swap_grads", "scatter_grads", "share_grads")[self.done:]:
            job, done = self.stage()
            done(_run_job(job, name))

    def grads(self):
        return dict(zip(self.keys, self.g))


EARLY_KEYS = ("w_d", "w_g", "w_u", "w_out")


def _layer_backward(dz2, dx2, sv, wts, sm, d, c_idx, chip, carried=(), split_own=False, small_stage=None):
    t = dx2.shape[0]
    heads = d // LANE
    ff = wts["w_d"].shape[0]
    in_w = 7 * d + heads
    g, pay = {}, {}
    carried = list(carried)

    def payload(key, a, b, rows, row0, name, extra=()):
        def call(job):
            return _matmul(a, b, "tn", BF16, name, slab=((1, rows, d), 0, row0), into=pay.get(key), job=job, tm_cap=1408,
                           tn_cap=1024, tk_cap=1024)
        pay[key] = _carrying(list(extra), call)

    def stages():
        return [ex.stage() for ex in carried]

    dm = _carrying(stages(), lambda job: _matmul(dz2, wts["w_d"], "nt", BF16, "down_bwd_x", tn_cap=1408, tk_cap=1024, job=job))
    payload("w_d", sv["mm"], dz2, ff, 0, "down_bwd_w")
    da, db = _swiglu_bwd(sv["a"], sv["b"], dm, "swiglu_bwd")
    dh2 = _matmul_pieces([(da, wts["w_g_t"], 0), (db, wts["w_u_t"], 0)], None, "gu_bwd_x", tk=_tile(ff, 1408))
    payload("w_g", da, sv["h2"], ff, 0, "gate_bwd_w")
    payload("w_u", db, sv["h2"], ff, 0, "up_bwd_w")
    dx1, dz, g["g_fpre"], g["g_post"] = _norm_bwd(dx2, (dh2, sv["x1"], sm["g_fpre"]), (sv["z"], sm["g_post"]), "norm_bwd_mid")
    dmerged = _matmul(dz, wts["w_out"], "nt", F32, "out_bwd_x", tk_cap=1024)
    payload("w_out", sv["merged"], dz, d, 0, "out_bwd_w")
    d_uv, d_g, do, g["w_s"], g["b_s"], g["g_v"] = _mix_bwd(dmerged, sv["proj"], sv["o"], sm["wm"], sm["wm_t"], sm["bs"],
                                                         sm["g_v"], d, "mix_bwd")
    own = []
    if split_own:
        own.append(_GradExchange(pay, EARLY_KEYS, c_idx, chip))
        carried.append(own[0])
    extra = [small_stage(g)] if small_stage is not None else []
    attn_args = (sv["proj"], do, sv["o"], sv["lse"], sv["c_t"], d)
    dq, dk, dv, dc_q, dc_k = _carrying(stages() + extra, lambda job: _attn_bwd(*attn_args, "attn_bwd", job=job))
    df_t, g["b_f"] = _fox_bwd(dc_q, dc_k, sv["f_t"], sm["b_f"], "fox_bwd")
    df_b = df_t.astype(BF16)
    dh_f = _matmul(df_b, wts["w_in_t"][7 * d:], "tn", F32, "forget_bwd_x")
    pieces = [(d_uv, COL_U), (dq, COL_Q), (dk, COL_K), (dv, COL_VA), (d_g, COL_GA)]
    ops = [(p, wts["w_in_t"], col * d) for p, col in pieces]
    dh = _carrying(stages(), lambda job: _matmul_pieces(ops, dh_f, "proj_bwd_x", job=job, tk=_tile(d, 1024)))
    for p, col in pieces:
        payload("w_in", p, sv["h"], in_w, col * d, "proj_bwd_w", extra=stages())
    w_f_rows = _matmul(df_b, sv["h"], "nn", BF16, "forget_bwd_w", tk_cap=1024)
    pay["w_in"] = lax.dynamic_update_slice(pay["w_in"], w_f_rows[None], (0, 7 * d, 0))
    own.append(_GradExchange(pay, [k for k in ("w_in",) + EARLY_KEYS if not (split_own and k in EARLY_KEYS)], c_idx, chip))
    return dh, dx1, g, own


def _small_pack(parts):
    flat = jnp.concatenate([p.reshape(-1) for p in parts])
    n = flat.shape[0]
    pad = (-n) % (LANE * LANE)
    return jnp.pad(flat, (0, pad)).reshape(-1, LANE)


def kernel(x, mix_pre_g, w_in, b_forget, sgu_norm_g, w_spatial, b_spatial, w_out, mix_post_g, ffn_pre_g, w_gate, w_up, w_down, ffn_post_g, loss_target, m_mix_pre_g, m_w_in, m_b_forget, m_sgu_norm_g, m_w_spatial, m_b_spatial, m_w_out, m_mix_post_g, m_ffn_pre_g, m_w_gate, m_w_up, m_w_down, m_ffn_post_g, v_mix_pre_g, v_w_in, v_b_forget, v_sgu_norm_g, v_w_spatial, v_b_spatial, v_w_out, v_mix_post_g, v_ffn_pre_g, v_w_gate, v_w_up, v_w_down, v_ffn_post_g):
    depth, d = mix_pre_g.shape
    assert depth == 2, "core c of a chip owns layer c"
    heads = d // LANE
    t = x.shape[1]
    ff = w_down.shape[1] * N_CHIPS
    in_w = w_in.shape[2] * N_CHIPS
    assert in_w == 7 * d + heads
    xs = x.reshape(t, d)
    target = loss_target.reshape(t, d)
    c_idx = lax.axis_index("c").astype(jnp.int32).reshape(1)
    chip = 2 * lax.axis_index("x") + lax.axis_index("y")
    dev = 2 * chip + lax.axis_index("c")

    def in_view(w):
        return jnp.transpose(w, (2, 0, 1))

    def gu_view(w):
        return jnp.transpose(w, (0, 2, 1))

    own = [jnp.transpose(in_view(w_in).astype(BF16), (1, 0, 2)), w_out.astype(BF16), gu_view(w_gate).astype(BF16),
           gu_view(w_up).astype(BF16), w_down.astype(BF16)]
    bufs = [[lax.dynamic_update_slice(jnp.zeros((N_CHIPS,) + o.shape[1:], BF16), o[l][None], (chip, 0, 0)) for o in own]
            for l in range(depth)]
    first_in = _run_job(_gather_job([bufs[0][0]]), "gather_first")[0]

    def weights(g_in, g_out, g_g, g_u, g_d):
        return dict(w_in_t=g_in.reshape(in_w, d), w_out=g_out.reshape(d, d), w_g_t=g_g.reshape(ff, d),
                    w_u_t=g_u.reshape(ff, d), w_d=g_d.reshape(ff, d))

    tril = jnp.tril(jnp.ones((LANE, LANE), bool))
    smalls = []
    for l in range(depth):
        wm = jnp.where(tril[None], w_spatial[l], 0.0).astype(BF16)
        smalls.append(dict(
            b_f=b_forget[l].reshape(heads, 1), wm=wm, wm_t=jnp.swapaxes(wm, 1, 2), bs=b_spatial[l].reshape(heads, LANE, 1),
            g_v=sgu_norm_g[l].reshape(1, d), g_pre=mix_pre_g[l].reshape(1, d), g_post=mix_post_g[l].reshape(1, d),
            g_fpre=ffn_pre_g[l].reshape(1, d), g_fpost=ffn_post_g[l].reshape(1, d)))

    wts, later = [], {}

    def keep(key):
        def done(moved):
            later[key] = list(moved)
        return done

    def rest_first():
        wts.append(weights(first_in, *later["rest0"]))
        return wts[0]

    stages = dict(proj=(_gather_job(bufs[0][1:], mid_at=1.0), keep("rest0")),
                  attn=(_gather_job(bufs[1][0:2], mid_at=0.7), keep("in_out1")),
                  gate=(_gather_job(bufs[1][2:3], mid_at=1.0), keep("g1")), up=(_gather_job(bufs[1][3:4], mid_at=1.0), keep("u1")),
                  down=(_gather_job(bufs[1][4:5], mid_at=1.0), keep("d1")))
    h = _norm_fwd(xs, None, None, smalls[0]["g_pre"], "norm_first")
    saved = [_layer_forward(xs, h, first_in.reshape(in_w, d), rest_first, smalls[0], d, stages)]
    wts.append(weights(*later["in_out1"], later["g1"][0], later["u1"][0], later["d1"][0]))
    for l in range(1, depth):
        xin, h = _norm_fwd(saved[l - 1]["x1"], saved[l - 1]["z2"], smalls[l - 1]["g_fpost"], smalls[l]["g_pre"], "norm_out")
        saved.append(_layer_forward(xin, h, wts[l]["w_in_t"], lambda l=l: wts[l], smalls[l], d))
    y, _ = _norm_fwd(saved[-1]["x1"], saved[-1]["z2"], smalls[-1]["g_fpost"], smalls[-1]["g_pre"], "norm_out")
    dy, loss_part = _loss_grad(y, target, "loss")
    loss = lax.psum(jnp.sum(loss_part), ("x", "y", "c"))

    small_shapes = dict(g_pre=(d,), b_f=(heads,), g_v=(d,), w_s=w_spatial.shape[1:], b_s=b_spatial.shape[1:], g_post=(d,),
                        g_fpre=(d,), g_fpost=(d,))
    late_entries = [(0, "g_pre"), (0, "b_f")]
    early_entries = [(l, n) for l in reversed(range(depth)) for n in small_shapes if (l, n) not in late_entries]
    dev_sel = jnp.stack([dev, jnp.zeros_like(dev), jnp.zeros_like(dev)]).astype(jnp.int32)
    small_sum = {}

    def small_exchange(entries, values):
        packed = _small_pack([values[e].reshape(-1) for e in entries])

        def done(moved):
            total = _sum_slots(moved[0], packed[None], dev_sel, "sum_small").reshape(-1)
            off = 0
            for e in entries:
                n = math.prod(small_shapes[e[1]])
                small_sum[e] = total[off:off + n].reshape(small_shapes[e[1]])
                off += n

        return _gather_all_job(packed), done

    grads = [None] * depth
    exchanges = [None] * depth
    dx2 = dy
    dz2, g_fpost = _norm_bwd(dx2, None, (saved[depth - 1]["z2"], smalls[depth - 1]["g_fpost"]), "norm_bwd_top")
    for l in reversed(range(depth)):
        last = l == 0

        def small_stage(g, l=l, g_fpost=g_fpost):
            known = {(k, n): grads[k][n] for k in range(l + 1, depth) for n in small_shapes}
            known.update({(l, n): g[n] for n in g})
            known[(l, "g_fpost")] = g_fpost
            return small_exchange(early_entries, known)

        carried = [ex for k in range(l + 1, depth) for ex in exchanges[k]]
        dh, dx1, g, exchanges[l] = _layer_backward(dz2, dx2, saved[l], wts[l], smalls[l], d, c_idx, chip, carried=carried,
                                                    split_own=last, small_stage=small_stage if last else None)
        g["g_fpost"] = g_fpost
        if l > 0:
            dx2, dz2, g["g_pre"], g_fpost = _norm_bwd(dx1, (dh, saved[l]["x"], smalls[l]["g_pre"]),
                                                       (saved[l - 1]["z2"], smalls[l - 1]["g_fpost"]), "norm_bwd_between")
        else:
            grad_x, g["g_pre"] = _norm_bwd(dx1, (dh, saved[l]["x"], smalls[l]["g_pre"]), None, "norm_bwd_bottom")
        grads[l] = g
    job, done = small_exchange(late_entries, {(0, n): grads[0][n] for n in ("g_pre", "b_f")})
    done(_run_job(job, "gather_small"))
    big = [{} for _ in range(depth)]
    for l in range(depth):
        for ex in exchanges[l]:
            ex.run()
            big[l].update(ex.grads())
    small_grads = {n: jnp.stack([small_sum[(l, n)] for l in range(depth)]) for n in small_shapes}

    def adam_small(w, g, m, v):
        shp = w.shape
        if w.ndim >= 3 and shp[-1] >= LANE:
            two = (math.prod(shp[:-1]), shp[-1])
        else:
            two = (1, math.prod(shp)) if math.prod(shp) < LANE else (math.prod(shp) // LANE, LANE)
        outs = _adamw(w.reshape(two), g.reshape(two), m.reshape(two), v.reshape(two), "adamw")
        return [g] + [o.reshape(shp) for o in outs]

    def adam_in(w, m, v):
        outs = _adamw_interleaved(in_view(w), big[0]["w_in"], big[1]["w_in"], in_view(m), in_view(v), "adamw_in")
        return [jnp.transpose(o, (1, 2, 0)) for o in outs]

    def adam_gu(k, w, m, v):
        outs = _adamw_layers(gu_view(w), big[0][k], big[1][k], gu_view(m), gu_view(v), "adamw_layers")
        return [jnp.transpose(o, (0, 2, 1)) for o in outs]

    def adam_rows(k, w, m, v):
        return _adamw_layers(w, big[0][k], big[1][k], m, v, "adamw_layers")

    results = [
        adam_small(mix_pre_g, small_grads["g_pre"], m_mix_pre_g, v_mix_pre_g),
        adam_in(w_in, m_w_in, v_w_in),
        adam_small(b_forget, small_grads["b_f"], m_b_forget, v_b_forget),
        adam_small(sgu_norm_g, small_grads["g_v"], m_sgu_norm_g, v_sgu_norm_g),
        adam_small(w_spatial, small_grads["w_s"], m_w_spatial, v_w_spatial),
        adam_small(b_spatial, small_grads["b_s"], m_b_spatial, v_b_spatial),
        adam_rows("w_out", w_out, m_w_out, v_w_out),
        adam_small(mix_post_g, small_grads["g_post"], m_mix_post_g, v_mix_post_g),
        adam_small(ffn_pre_g, small_grads["g_fpre"], m_ffn_pre_g, v_ffn_pre_g),
        adam_gu("w_g", w_gate, m_w_gate, v_w_gate),
        adam_gu("w_u", w_up, m_w_up, v_w_up),
        adam_rows("w_d", w_down, m_w_down, v_w_down),
        adam_small(ffn_post_g, small_grads["g_fpost"], m_ffn_post_g, v_ffn_post_g),
    ]
    gs, deltas, new_ms, new_vs = zip(*results)
    return (loss, grad_x.reshape(x.shape), *gs, *deltas, *new_ms, *new_vs)
```

```python
import functools
import math

import jax
import jax.numpy as jnp
from jax import lax
from jax.experimental import pallas as pl
from jax.experimental.pallas import tpu as pltpu

F32 = jnp.float32
BF16 = jnp.bfloat16

EPS = 1e-6
LANE = 128
SUBLANE = 8
N_CHIPS = 4
N_DEV = 8
VMEM_LIMIT = 48 * 1024 * 1024
MESH = pl.DeviceIdType.MESH

ADAM_LR = 0.001
ADAM_B1 = 0.9
ADAM_B2 = 0.999
ADAM_EPS = 1e-08
ADAM_WD = 0.01
ADAM_STEP = 10
ADAM_C1 = 1.0 / (1.0 - ADAM_B1 ** ADAM_STEP)
ADAM_C2 = 1.0 / (1.0 - ADAM_B2 ** ADAM_STEP)

GELU_K = math.sqrt(2.0 / math.pi)
GELU_A = 0.044715
NEG = -1e30
LOG2E = 1.4426950408889634
LN2 = 0.6931471805599453

COL_U, COL_V, COL_Q, COL_K, COL_VA, COL_GA, COL_GB, COL_F = range(8)


def _cparams(sem=None):
    return pltpu.CompilerParams(dimension_semantics=sem, vmem_limit_bytes=VMEM_LIMIT)


def _tile(n, cap):
    best = None
    for t in range(LANE, min(n, cap) + 1, LANE):
        if n % t == 0:
            best = t
    return best if best is not None else n


def _rows(n, cap):
    best = None
    for t in range(SUBLANE, min(n, cap) + 1, SUBLANE):
        if n % t == 0:
            best = t
    return best if best is not None else n


def _gelu_and_grad(x):
    x2 = x * x
    t = jnp.tanh(GELU_K * (x + GELU_A * x2 * x))
    g = 0.5 * x * (1.0 + t)
    dg = 0.5 * (1.0 + t) + 0.5 * x * (1.0 - t * t) * (GELU_K * (1.0 + 3.0 * GELU_A * x2))
    return g, dg


def _sigmoid(x):
    return 1.0 / (1.0 + jnp.exp(-x))


def _sum8(v):
    n, d = v.shape
    return v.reshape(n // SUBLANE, SUBLANE, d).sum(axis=0)


def _nt_dot(a, b):
    return lax.dot_general(a, b, (((1,), (1,)), ((), ())), preferred_element_type=F32)


_HBM = pl.BlockSpec(memory_space=pl.ANY)


def _place():
    x, y, c = lax.axis_index("x"), lax.axis_index("y"), lax.axis_index("c")
    chips = [(1 - x, y), (x, 1 - y), (1 - x, 1 - y)]
    return x, y, c, chips


class _Job:
    def __init__(self, ins, inout, fresh, nsem, first, mid, last, mid_at=0.5):
        self.ins, self.inout, self.fresh, self.nsem = list(ins), list(inout), list(fresh), nsem
        self.first, self.mid, self.last, self.mid_at = first, mid, last, mid_at


def _call(body, *, grid, in_specs, out_specs, out_shape, scratch_shapes, dims, name, args, aliases=None, job=None):
    single = not isinstance(out_shape, (list, tuple))
    out_specs = [out_specs] if single else list(out_specs)
    out_shape = [out_shape] if single else list(out_shape)
    aliases = dict(aliases or {})
    if job is None:
        outs = pl.pallas_call(body, grid=grid, in_specs=in_specs, out_specs=out_specs, out_shape=out_shape,
                              scratch_shapes=scratch_shapes, input_output_aliases=aliases, compiler_params=_cparams(dims),
                              name=name)(*args)
        return (outs[0] if single else outs), []
    n_in, n_out, n_scr = len(args), len(out_shape), len(scratch_shapes)
    n_ji, n_jio, n_jf = len(job.ins), len(job.inout), len(job.fresh)
    total = math.prod(grid)

    def wrapped(*refs):
        host_in = refs[:n_in]
        pos = n_in
        j_ins = refs[pos:pos + n_ji]
        pos += n_ji + n_jio
        host_out = refs[pos:pos + n_out]
        pos += n_out
        j_inout = refs[pos:pos + n_jio]
        pos += n_jio
        j_fresh = refs[pos:pos + n_jf]
        pos += n_jf
        host_scr = refs[pos:pos + n_scr]
        ssem, rsem = refs[pos + n_scr:]
        flat = 0
        for ax, size in enumerate(grid):
            flat = flat * size + pl.program_id(ax)

        def hook(fn, at):
            if fn is not None:
                @pl.when(flat == at)
                def _():
                    fn(j_ins, j_inout, j_fresh, ssem, rsem)

        hook(job.first, 0)
        body(*host_in, *host_out, *host_scr)
        hook(job.mid, min(int(total * job.mid_at), total - 1))
        hook(job.last, total - 1)

    for k in range(n_jio):
        aliases[n_in + n_ji + k] = n_out + k
    outs = pl.pallas_call(
        wrapped, grid=grid,
        in_specs=list(in_specs) + [_HBM] * (n_ji + n_jio),
        out_specs=out_specs + [_HBM] * (n_jio + n_jf),
        out_shape=out_shape + [jax.ShapeDtypeStruct(b.shape, b.dtype) for b in job.inout] + list(job.fresh),
        scratch_shapes=list(scratch_shapes) + [pltpu.SemaphoreType.DMA((job.nsem,)), pltpu.SemaphoreType.DMA((job.nsem,))],
        input_output_aliases=aliases, compiler_params=_cparams(tuple("arbitrary" for _ in grid)), name=name,
    )(*args, *job.ins, *job.inout)
    host = outs[:n_out]
    return (host[0] if single else host), outs[n_out:]


def _run_job(job, name):
    n_ji, n_jio, n_jf = len(job.ins), len(job.inout), len(job.fresh)

    def body(*refs):
        j_ins = refs[:n_ji]
        pos = n_ji + n_jio
        j_inout = refs[pos:pos + n_jio]
        j_fresh = refs[pos + n_jio:pos + n_jio + n_jf]
        ssem, rsem = refs[pos + n_jio + n_jf:]
        for fn in (job.first, job.mid, job.last):
            if fn is not None:
                fn(j_ins, j_inout, j_fresh, ssem, rsem)

    return pl.pallas_call(
        body, in_specs=[_HBM] * (n_ji + n_jio), out_specs=[_HBM] * (n_jio + n_jf),
        out_shape=[jax.ShapeDtypeStruct(b.shape, b.dtype) for b in job.inout] + list(job.fresh),
        scratch_shapes=[pltpu.SemaphoreType.DMA((job.nsem,)), pltpu.SemaphoreType.DMA((job.nsem,))],
        input_output_aliases={n_ji + k: k for k in range(n_jio)}, name=name,
    )(*job.ins, *job.inout)


_DIMS ={"nn": ((1,), (0,)), "nt": ((1,), (1,)), "tn": ((0,), (0,))}


def _matmul(a, b, mode, out_dtype, name, n=None, slab=None, into=None, job=None, norms=None, tm_cap=512, tn_cap=2048,
            tk_cap=1408):
    if mode == "nn":
        (m, k), (k2, nn_) = a.shape, b.shape
    elif mode == "nt":
        (m, k), (nn_, k2) = a.shape, b.shape
    else:
        (k, m), (k2, nn_) = a.shape, b.shape
    n = nn_ if n is None else n
    assert k == k2, (a.shape, b.shape, mode)
    tm, tn, tk = _tile(m, tm_cap), _tile(n, tn_cap), _tile(k, tk_cap)
    if slab is not None and slab[2]:
        tm = _tile(math.gcd(m, slab[2]), tm_cap)
    nk = k // tk
    if mode == "tn":
        a_spec = pl.BlockSpec((tk, tm), lambda j, i, kk, *_: (kk, i))
    else:
        a_spec = pl.BlockSpec((tm, tk), lambda j, i, kk, *_: (i, kk))
    if mode == "nt":
        b_spec = pl.BlockSpec((tn, tk), lambda j, i, kk, *_: (j, kk))
    else:
        b_spec = pl.BlockSpec((tk, tn), lambda j, i, kk, *_: (kk, j))
    dims = (_DIMS[mode], ((), ()))
    aliased = into is not None

    n_in = 2 + aliased + (3 if norms is not None else 0)

    def finish(refs, z):
        refs[n_in][...] = z.astype(out_dtype).reshape(refs[n_in].shape)
        if norms is not None:
            x_ref, gp_ref, gn_ref = refs[n_in - 3:n_in]
            r = lax.rsqrt(jnp.mean(z * z, axis=-1, keepdims=True) + EPS)
            xn = x_ref[...] + z * r * gp_ref[...]
            refs[n_in + 1][...] = xn
            r2 = lax.rsqrt(jnp.mean(xn * xn, axis=-1, keepdims=True) + EPS)
            refs[n_in + 2][...] = (xn * r2 * gn_ref[...]).astype(BF16)

    def body(*refs):
        p = lax.dot_general(refs[0][...], refs[1][...], dims, preferred_element_type=F32)
        if nk == 1:
            finish(refs, p)
        else:
            acc = refs[-1]
            kk = pl.program_id(2)

            @pl.when(kk == 0)
            def _():
                acc[...] = p

            @pl.when(kk > 0)
            def _():
                acc[...] += p

            @pl.when(kk == nk - 1)
            def _():
                finish(refs, acc[...])

    if slab is None:
        out_spec = pl.BlockSpec((tm, tn), lambda j, i, kk: (i, j))
        out_shape = jax.ShapeDtypeStruct((m, n), out_dtype)
    else:
        shape3, lead, row0 = slab
        assert row0 % tm == 0 and shape3[2] == n
        out_spec = pl.BlockSpec((1, tm, tn), lambda j, i, kk: (lead, row0 // tm + i, j))
        out_shape = jax.ShapeDtypeStruct(shape3, out_dtype)
    in_specs, args = [a_spec, b_spec], [a, b]
    if aliased:
        in_specs.append(pl.BlockSpec(memory_space=pl.ANY))
        args.append(into)
    if norms is not None:
        assert tn == n and slab is None, "the fused norms need whole rows"
        row = pl.BlockSpec((tm, n), lambda j, i, kk: (i, 0))
        vec = pl.BlockSpec((1, n), lambda j, i, kk: (0, 0))
        in_specs += [row, vec, vec]
        args += list(norms)
        out_spec = [out_spec, row, row]
        out_shape = [out_shape, jax.ShapeDtypeStruct((m, n), F32), jax.ShapeDtypeStruct((m, n), BF16)]
    out, moved = _call(
        body, grid=(n // tn, m // tm, nk), in_specs=in_specs, out_specs=out_spec, out_shape=out_shape,
        scratch_shapes=[pltpu.VMEM((tm, tn), F32)] if nk > 1 else [], dims=("parallel", "parallel", "arbitrary"), name=name,
        args=args, aliases={2: 0} if aliased else None, job=job)
    return out if job is None else (out, moved)


def _matmul_pieces(pieces, addend, name, tk, job=None, tm_cap=512):
    m = pieces[0][0].shape[0]
    n = pieces[0][1].shape[1]
    tm = _tile(m, tm_cap)
    spans, s0 = [], 0
    for a, b, row0 in pieces:
        assert a.shape[1] % tk == 0 and row0 % tk == 0 and b.shape[1] == n and a.shape[0] == m
        spans.append((s0, a.shape[1] // tk, row0 // tk))
        s0 += a.shape[1] // tk
    steps = s0
    np_ = len(pieces)
    groups = []
    for (a, b, _), (first, count, brow) in zip(pieces, spans):
        if groups and groups[-1][0] is b and groups[-1][3] + groups[-1][2] == brow:
            groups[-1][2] += count
        else:
            groups.append([b, first, count, brow])
    b_of = []
    for first, count, _ in spans:
        b_of.append(next(k for k, g in enumerate(groups) if g[1] <= first < g[1] + g[2]))
    ng = len(groups)

    nm = m // tm

    def body(*refs):
        o_ref, acc = refs[-2], refs[-1]
        s, i = pl.program_id(0), pl.program_id(1)
        rows = pl.ds(pl.multiple_of(i * tm, tm), tm)

        @pl.when(s == 0)
        def _():
            acc[rows, :] = refs[np_ + ng][...] if addend is not None else jnp.zeros((tm, n), F32)

        for p, (first, count, _) in enumerate(spans):
            @pl.when((s >= first) & (s < first + count))
            def _(p=p):
                acc[rows, :] += jnp.dot(refs[p][...], refs[np_ + b_of[p]][...], preferred_element_type=F32)

        @pl.when(s == steps - 1)
        def _():
            o_ref[...] = acc[rows, :]

    in_specs, args = [], []
    for (a, _, _), (first, count, _) in zip(pieces, spans):
        in_specs.append(pl.BlockSpec((tm, tk), lambda s, i, f=first, c=count: (
            jnp.where(s < f, 0, jnp.where(s >= f + c, nm - 1, i)), jnp.clip(s - f, 0, c - 1))))
        args.append(a)
    for b, first, count, brow in groups:
        in_specs.append(pl.BlockSpec((tk, n), lambda s, i, f=first, c=count, r=brow: (r + jnp.clip(s - f, 0, c - 1), 0)))
        args.append(b)
    if addend is not None:
        in_specs.append(pl.BlockSpec((tm, n), lambda s, i: (jnp.where(s == 0, i, nm - 1), 0)))
        args.append(addend)
    out, moved = _call(
        body, grid=(steps, nm), in_specs=in_specs,
        out_specs=pl.BlockSpec((tm, n), lambda s, i: (jnp.where(s == steps - 1, i, 0), 0)),
        out_shape=jax.ShapeDtypeStruct((m, n), F32), scratch_shapes=[pltpu.VMEM((m, n), F32)],
        dims=("arbitrary", "arbitrary"), name=name, args=args, job=job)
    return out if job is None else (out, moved)


def _norm_fwd(x, z, g_post, g_next, name):
    t, d = x.shape
    tt = _rows(t, 512)
    row = pl.BlockSpec((tt, d), lambda i: (i, 0))
    vec = pl.BlockSpec((1, d), lambda i: (0, 0))

    def body(*refs):
        if z is None:
            x_ref, gn_ref, h_ref = refs
            xn = x_ref[...]
        else:
            x_ref, z_ref, gp_ref, gn_ref, xo_ref, h_ref = refs
            zz = z_ref[...]
            r = lax.rsqrt(jnp.mean(zz * zz, axis=-1, keepdims=True) + EPS)
            xn = x_ref[...] + zz * r * gp_ref[...]
            xo_ref[...] = xn
        r2 = lax.rsqrt(jnp.mean(xn * xn, axis=-1, keepdims=True) + EPS)
        h_ref[...] = (xn * r2 * gn_ref[...]).astype(BF16)

    if z is None:
        return pl.pallas_call(
            body, grid=(t // tt,), in_specs=[row, vec], out_specs=row,
            out_shape=jax.ShapeDtypeStruct((t, d), BF16), compiler_params=_cparams(("parallel",)), name=name,
        )(x, g_next)
    return pl.pallas_call(
        body, grid=(t // tt,), in_specs=[row, row, vec, vec], out_specs=[row, row],
        out_shape=[jax.ShapeDtypeStruct((t, d), F32), jax.ShapeDtypeStruct((t, d), BF16)],
        compiler_params=_cparams(("parallel",)), name=name,
    )(x, z, g_post, g_next)


def _rms_bwd(dy, x, g):
    r = lax.rsqrt(jnp.mean(x * x, axis=-1, keepdims=True) + EPS)
    n = x * r
    dn = dy * g
    dx = r * (dn - n * jnp.mean(dn * n, axis=-1, keepdims=True))
    return dx, dy * n


def _norm_bwd(dres, pre, post, name):
    t, d = dres.shape
    tt = _rows(t, 512)
    nt = t // tt
    row = pl.BlockSpec((tt, d), lambda i: (i, 0))
    vec = pl.BlockSpec((1, d), lambda i: (0, 0))
    has_pre, has_post = pre is not None, post is not None
    n_in = 1 + (3 if has_pre else 0) + (2 if has_post else 0)
    n_out = has_pre + has_post + has_pre + has_post

    def body(*refs):
        ins, outs, scr = refs[:n_in], refs[n_in:n_in + n_out], refs[n_in + n_out:]
        i = pl.program_id(0)
        dx = ins[0][...]
        pos, opos, spos = 1, 0, 0
        accs = []
        if has_pre:
            dh_ref, xa_ref, ga_ref = ins[pos:pos + 3]
            pos += 3
            dxa, dga_t = _rms_bwd(dh_ref[...], xa_ref[...], ga_ref[...])
            dx = dx + dxa
            outs[opos][...] = dx
            opos += 1
            accs.append((scr[spos], dga_t))
            spos += 1
        if has_post:
            zb_ref, gb_ref = ins[pos:pos + 2]
            dz, dgb_t = _rms_bwd(dx, zb_ref[...], gb_ref[...])
            outs[opos][...] = dz.astype(BF16)
            opos += 1
            accs.append((scr[spos], dgb_t))
            spos += 1
        for (acc, val), out in zip(accs, outs[opos:]):
            part = _sum8(val)

            @pl.when(i == 0)
            def _(acc=acc, part=part):
                acc[...] = part

            @pl.when(i > 0)
            def _(acc=acc, part=part):
                acc[...] += part

            @pl.when(i == nt - 1)
            def _(acc=acc, out=out):
                out[...] = jnp.sum(acc[...], axis=0, keepdims=True)

    in_specs, args = [row], [dres]
    out_specs, out_shape = [], []
    if has_pre:
        in_specs += [row, row, vec]
        args += list(pre)
        out_specs.append(row)
        out_shape.append(jax.ShapeDtypeStruct((t, d), F32))
    if has_post:
        in_specs += [row, vec]
        args += list(post)
        out_specs.append(row)
        out_shape.append(jax.ShapeDtypeStruct((t, d), BF16))
    for _ in range(has_pre + has_post):
        out_specs.append(vec)
        out_shape.append(jax.ShapeDtypeStruct((1, d), F32))
    return pl.pallas_call(
        body, grid=(nt,), in_specs=in_specs, out_specs=out_specs, out_shape=out_shape,
        scratch_shapes=[pltpu.VMEM((SUBLANE, d), F32)] * (has_pre + has_post),
        compiler_params=_cparams(("arbitrary",)), name=name,
    )(*args)


def _loss_grad(y, target, name):
    t, d = y.shape
    tt = _rows(t, 512)
    nt = t // tt
    row = pl.BlockSpec((tt, d), lambda i: (i, 0))
    inv_d = 1.0 / d

    def body(y_ref, t_ref, dy_ref, l_ref):
        i = pl.program_id(0)
        diff = y_ref[...] - t_ref[...]
        dy_ref[...] = diff * inv_d
        s8 = _sum8(diff * diff)
        part = s8[:, 0:LANE]
        for k in range(1, d // LANE):
            part = part + s8[:, k * LANE:(k + 1) * LANE]
        part = part * (0.5 * inv_d)

        @pl.when(i == 0)
        def _():
            l_ref[...] = part

        @pl.when(i > 0)
        def _():
            l_ref[...] += part

    return pl.pallas_call(
        body, grid=(nt,), in_specs=[row, row],
        out_specs=[row, pl.BlockSpec((SUBLANE, LANE), lambda i: (0, 0))],
        out_shape=[jax.ShapeDtypeStruct((t, d), F32), jax.ShapeDtypeStruct((SUBLANE, LANE), F32)],
        compiler_params=_cparams(("arbitrary",)), name=name,
    )(y, target)


def _swiglu_fwd(a, b, name):
    t, f = a.shape
    tt = _rows(t, 256)
    blk = pl.BlockSpec((tt, f), lambda i: (i, 0))

    def body(a_ref, b_ref, m_ref):
        av = a_ref[...].astype(F32)
        m_ref[...] = (av * _sigmoid(av) * b_ref[...].astype(F32)).astype(BF16)

    return pl.pallas_call(
        body, grid=(t // tt,), in_specs=[blk, blk], out_specs=blk,
        out_shape=jax.ShapeDtypeStruct((t, f), BF16), compiler_params=_cparams(("parallel",)), name=name,
    )(a, b)


def _swiglu_bwd(a, b, dm, name):
    t, f = a.shape
    tt = _rows(t, 256)
    blk = pl.BlockSpec((tt, f), lambda i: (i, 0))

    def body(a_ref, b_ref, dm_ref, da_ref, db_ref):
        av = a_ref[...].astype(F32)
        s = _sigmoid(av)
        dv = dm_ref[...].astype(F32)
        da_ref[...] = (dv * b_ref[...].astype(F32) * s * (1.0 + av * (1.0 - s))).astype(BF16)
        db_ref[...] = (dv * av * s).astype(BF16)

    return pl.pallas_call(
        body, grid=(t // tt,), in_specs=[blk, blk, blk], out_specs=[blk, blk],
        out_shape=[jax.ShapeDtypeStruct((t, f), BF16)] * 2, compiler_params=_cparams(("parallel",)), name=name,
    )(a, b, dm)


def _log_sigmoid(x):
    return jnp.minimum(x, 0.0) - jnp.log1p(jnp.exp(-jnp.abs(x)))


def _fox_prep(f_t, b_f, name):
    h, t = f_t.shape

    def body(f_ref, b_ref, c_ref):
        r = lax.broadcasted_iota(jnp.int32, (LANE, LANE), 0)
        c = lax.broadcasted_iota(jnp.int32, (LANE, LANE), 1)
        upper = (r <= c).astype(F32)
        carry = jnp.zeros((h, 1), F32)
        for j in range(t // LANE):
            sl = slice(j * LANE, (j + 1) * LANE)
            lf = _log_sigmoid(f_ref[:, sl] + b_ref[...])
            cs = jnp.dot(lf, upper, precision=lax.Precision.HIGHEST, preferred_element_type=F32) + carry
            c_ref[:, sl] = cs
            carry = cs[:, LANE - 1:LANE]

    return pl.pallas_call(body, out_shape=jax.ShapeDtypeStruct((h, t), F32), compiler_params=_cparams(), name=name)(f_t, b_f)


def _fox_bwd(dc_q, dc_k, f_t, b_f, name):
    h, t = f_t.shape

    def body(dq_ref, dk_ref, f_ref, b_ref, df_ref, db_ref):
        r = lax.broadcasted_iota(jnp.int32, (LANE, LANE), 0)
        c = lax.broadcasted_iota(jnp.int32, (LANE, LANE), 1)
        lower = (r >= c).astype(F32)
        carry = jnp.zeros((h, 1), F32)
        dbsum = jnp.zeros((h, 1), F32)
        for j in reversed(range(t // LANE)):
            sl = slice(j * LANE, (j + 1) * LANE)
            dc = dq_ref[:, sl] - dk_ref[:, sl]
            dl = jnp.dot(dc, lower, precision=lax.Precision.HIGHEST, preferred_element_type=F32) + carry
            carry = dl[:, 0:1]
            df = dl * _sigmoid(-(f_ref[:, sl] + b_ref[...]))
            df_ref[:, sl] = df
            dbsum = dbsum + jnp.sum(df, axis=-1, keepdims=True)
        db_ref[...] = dbsum

    return pl.pallas_call(
        body, out_shape=[jax.ShapeDtypeStruct((h, t), F32), jax.ShapeDtypeStruct((h, 1), F32)],
        compiler_params=_cparams(), name=name,
    )(dc_q, dc_k, f_t, b_f)


ATTN_FWD = (1024, 512)
ATTN_BWD = (512, 512)


def _attn_tiles(t, tiles):
    return _tile(t, tiles[0]), _tile(t, tiles[1])


def _attn_fwd(proj, c_t, d, name, job=None):
    t = proj.shape[0]
    h = d // LANE
    bq, bk = _attn_tiles(t, ATTN_FWD)
    nq, nk, rr = t // bq, t // bk, bq // bk
    qc, kc, vc = COL_Q * h, COL_K * h, COL_VA * h
    qscale = LANE ** -0.5 * LOG2E

    def body(q_ref, k_ref, v_ref, cc_ref, cr_ref, o_ref, lse_ref, kb, vt, ckb, acc):
        i = pl.program_id(1)

        @pl.when(i == 0)
        def _():
            kb[...] = k_ref[...].astype(BF16)
            ckb[...] = jnp.broadcast_to(cc_ref[0] * LOG2E, (t, bq))
            for jn in range(nk):
                vt[jn] = v_ref[jn * bk:(jn + 1) * bk, :].astype(F32).T.astype(BF16)

        q = (q_ref[...].astype(F32) * qscale).astype(BF16)
        cq = cr_ref[0, 0] * LOG2E
        acc[...] = jnp.zeros((LANE, bq), F32)

        def block(j, diag, m_old, l_old):
            off = 0 if diag is None else diag * bk
            w = bq - off
            rows = pl.ds(pl.multiple_of(j * bk, bk), bk)
            s = _nt_dot(kb[rows, :], q[off:, :]) - ckb[rows, off:]
            if diag is not None:
                kk = lax.broadcasted_iota(jnp.int32, (bk, w), 0)
                qq = lax.broadcasted_iota(jnp.int32, (bk, w), 1)
                s = jnp.where(qq >= kk, s, NEG)
            cqs, m_part, l_part = cq[:, off:], m_old[:, off:], l_old[:, off:]
            m_new = jnp.maximum(m_part, jnp.max(s, axis=0, keepdims=True) + cqs)
            p = jnp.exp2(s + (cqs - m_new))
            alpha = jnp.exp2(m_part - m_new)
            l_new = alpha * l_part + jnp.sum(p, axis=0, keepdims=True)
            acc[:, off:] = alpha * acc[:, off:] + jnp.dot(vt[j], p.astype(BF16), preferred_element_type=F32)
            if off:
                m_new = jnp.concatenate([m_old[:, :off], m_new], axis=1)
                l_new = jnp.concatenate([l_old[:, :off], l_new], axis=1)
            return m_new, l_new

        m, l = lax.fori_loop(0, i * rr, lambda j, c: block(j, None, *c),
                             (jnp.full((1, bq), NEG, F32), jnp.zeros((1, bq), F32)))
        for jj in range(rr):
            m, l = block(i * rr + jj, jj, m, l)
        o_ref[...] = (acc[...] / l).T
        lse_ref[0, 0] = m + jnp.log2(l)

    rowq = pl.BlockSpec((1, 1, 1, bq), lambda hh, i: (hh, i, 0, 0))
    outs, moved = _call(
        body, grid=(h, nq),
        in_specs=[
            pl.BlockSpec((bq, LANE), lambda hh, i: (i, qc + hh)),
            pl.BlockSpec((t, LANE), lambda hh, i: (0, kc + hh)),
            pl.BlockSpec((t, LANE), lambda hh, i: (0, vc + hh)),
            pl.BlockSpec((1, t, 1), lambda hh, i: (hh, 0, 0)),
            rowq,
        ],
        out_specs=[pl.BlockSpec((bq, LANE), lambda hh, i: (i, hh)), rowq],
        out_shape=[jax.ShapeDtypeStruct((t, d), F32), jax.ShapeDtypeStruct((h, nq, 1, bq), F32)],
        scratch_shapes=[pltpu.VMEM((t, LANE), BF16), pltpu.VMEM((nk, LANE, bk), BF16), pltpu.VMEM((t, bq), F32),
                        pltpu.VMEM((LANE, bq), F32)],
        dims=("arbitrary", "arbitrary"), name=name,
        args=[proj, proj, proj, c_t.reshape(h, t, 1), c_t.reshape(h, nq, 1, bq)], job=job)
    outs = [outs[0], outs[1].reshape(h, t)]
    return outs if job is None else (outs, moved)


def _attn_bwd(proj, do, o, lse, c_t, d, name, job=None):
    t = proj.shape[0]
    h = d // LANE
    bq, bk = _attn_tiles(t, ATTN_BWD)
    nq, nk, rr = t // bq, t // bk, bq // bk
    qc, kc, vc = COL_Q * h, COL_K * h, COL_VA * h
    scale = LANE ** -0.5

    def body(q_ref, k_ref, v_ref, do_ref, o_ref, lse_ref, cc_ref, cr_ref, dq_ref, dk_ref, dv_ref, dcq_ref, dck_ref,
             kb, kt, vb, ckb, dk_acc, dv_acc, dck_acc, dqt_acc):
        i = pl.program_id(1)

        @pl.when(i == 0)
        def _():
            kb[...] = k_ref[...].astype(BF16)
            vb[...] = v_ref[...].astype(BF16)
            ckb[...] = jnp.broadcast_to(cc_ref[0] * LOG2E, (t, bq))
            for jn in range(nk):
                kt[jn] = k_ref[jn * bk:(jn + 1) * bk, :].astype(F32).T.astype(BF16)
            dk_acc[...] = jnp.zeros((t, LANE), F32)
            dv_acc[...] = jnp.zeros((t, LANE), F32)
            dck_acc[...] = jnp.zeros((t, LANE), F32)

        q = (q_ref[...].astype(F32) * (scale * LOG2E)).astype(BF16)
        dof = do_ref[...]
        dob = dof.astype(BF16)
        delta = jnp.sum((dof * o_ref[...]).T, axis=0, keepdims=True)
        rowb = cr_ref[0, 0] * LOG2E - lse_ref[0, 0]
        dqt_acc[...] = jnp.zeros((LANE, bq), F32)

        def block(j, diag, dcq):
            rows = pl.ds(pl.multiple_of(j * bk, bk), bk)
            p = jnp.exp2(_nt_dot(kb[rows, :], q) - ckb[rows, :] + rowb)
            if diag is not None:
                kk = lax.broadcasted_iota(jnp.int32, (bk, bq), 0)
                qq = lax.broadcasted_iota(jnp.int32, (bk, bq), 1)
                p = jnp.where(qq >= kk + diag * bk, p, 0.0)
            dv_acc[rows, :] += jnp.dot(p.astype(BF16), dob, preferred_element_type=F32)
            ds = p * (_nt_dot(vb[rows, :], dob) - delta)
            dsb = ds.astype(BF16)
            dk_acc[rows, :] += jnp.dot(dsb, q, preferred_element_type=F32)
            dqt_acc[...] += jnp.dot(kt[j], dsb, preferred_element_type=F32)
            part = ds[:, 0:LANE]
            for k in range(1, bq // LANE):
                part = part + ds[:, k * LANE:(k + 1) * LANE]
            dck_acc[rows, :] += part
            return dcq + jnp.sum(ds, axis=0, keepdims=True)

        dcq = lax.fori_loop(0, i * rr, lambda j, c: block(j, None, c), jnp.zeros((1, bq), F32))
        for jj in range(rr):
            dcq = block(i * rr + jj, jj, dcq)
        dq_ref[...] = (dqt_acc[...] * scale).T.astype(BF16)
        dcq_ref[0, 0] = dcq

        @pl.when(i == nq - 1)
        def _():
            dk_ref[...] = (dk_acc[...] * LN2).astype(BF16)
            dv_ref[...] = dv_acc[...].astype(BF16)
            dck_ref[0] = jnp.sum(dck_acc[...], axis=-1, keepdims=True)

    rowq = pl.BlockSpec((1, 1, 1, bq), lambda hh, i: (hh, i, 0, 0))
    blk = pl.BlockSpec((bq, LANE), lambda hh, i: (i, hh))
    whole = pl.BlockSpec((t, LANE), lambda hh, i: (0, hh))
    colk = pl.BlockSpec((1, t, 1), lambda hh, i: (hh, 0, 0))
    outs, moved = _call(
        body, grid=(h, nq),
        in_specs=[
            pl.BlockSpec((bq, LANE), lambda hh, i: (i, qc + hh)),
            pl.BlockSpec((t, LANE), lambda hh, i: (0, kc + hh)),
            pl.BlockSpec((t, LANE), lambda hh, i: (0, vc + hh)),
            blk, blk, rowq, colk, rowq,
        ],
        out_specs=[blk, whole, whole, rowq, colk],
        out_shape=[jax.ShapeDtypeStruct((t, d), BF16), jax.ShapeDtypeStruct((t, d), BF16), jax.ShapeDtypeStruct((t, d), BF16),
                   jax.ShapeDtypeStruct((h, nq, 1, bq), F32), jax.ShapeDtypeStruct((h, t, 1), F32)],
        scratch_shapes=[pltpu.VMEM((t, LANE), BF16), pltpu.VMEM((nk, LANE, bk), BF16), pltpu.VMEM((t, LANE), BF16),
                        pltpu.VMEM((t, bq), F32), pltpu.VMEM((t, LANE), F32), pltpu.VMEM((t, LANE), F32),
                        pltpu.VMEM((t, LANE), F32), pltpu.VMEM((LANE, bq), F32)],
        dims=("arbitrary", "arbitrary"), name=name,
        args=[proj, proj, proj, do, o, lse.reshape(h, nq, 1, bq), c_t.reshape(h, t, 1), c_t.reshape(h, nq, 1, bq)], job=job)
    outs = list(outs[:3]) + [outs[3].reshape(h, t), outs[4].reshape(h, t)]
    return outs if job is None else (outs, moved)


def _sgu_forward(u_ref, v_ref, gv_ref, wm_ref, bs_ref, mix_sc, groups):
    gu, dgu = _gelu_and_grad(u_ref[...].astype(F32))
    gvv, dgv = _gelu_and_grad(v_ref[...].astype(F32))
    mu = jnp.mean(gvv, axis=-1, keepdims=True)
    xc = gvv - mu
    r = lax.rsqrt(jnp.mean(xc * xc, axis=-1, keepdims=True) + EPS)
    nhat = xc * r
    vn = (nhat * gv_ref[...]).astype(BF16)
    for g in range(groups):
        sl = slice(g * LANE, (g + 1) * LANE)
        mix_sc[:, sl] = jnp.dot(wm_ref[g], vn[:, sl], preferred_element_type=F32) + bs_ref[g]
    return gu, dgu, dgv, nhat, r, vn, mix_sc[...]


def _mix_fwd(proj, o, wm, bs, g_v, d, name):
    t = proj.shape[0]
    groups = d // LANE

    def body(u_ref, v_ref, ga_ref, gb_ref, o_ref, wm_ref, bs_ref, gv_ref, out_ref, mix_sc):
        gu, _, _, _, _, _, mixed = _sgu_forward(u_ref, v_ref, gv_ref, wm_ref, bs_ref, mix_sc, groups)
        out_ref[...] = (_sigmoid(ga_ref[...].astype(F32)) * (gu * mixed) + _sigmoid(gb_ref[...].astype(F32)) * o_ref[...]).astype(BF16)

    def colblk(k):
        return pl.BlockSpec((LANE, d), lambda i, k=k: (i, k))

    full3 = pl.BlockSpec((groups, LANE, LANE), lambda i: (0, 0, 0))
    return pl.pallas_call(
        body, grid=(t // LANE,),
        in_specs=[colblk(COL_U), colblk(COL_V), colblk(COL_GA), colblk(COL_GB), colblk(0), full3,
                  pl.BlockSpec((groups, LANE, 1), lambda i: (0, 0, 0)), pl.BlockSpec((1, d), lambda i: (0, 0))],
        out_specs=colblk(0),
        out_shape=jax.ShapeDtypeStruct((t, d), BF16),
        scratch_shapes=[pltpu.VMEM((LANE, d), F32)],
        compiler_params=_cparams(("parallel",)), name=name,
    )(proj, proj, proj, proj, o, wm, bs, g_v)


def _mix_bwd(dmerged, proj, o, wm, wm_t, bs, g_v, d, name):
    t = proj.shape[0]
    groups = d // LANE
    nt = t // LANE

    def body(dm_ref, u_ref, v_ref, ga_ref, gb_ref, o_ref, wm_ref, wmt_ref, bs_ref, gv_ref,
             duv_ref, dg_ref, do_ref, dws_ref, dbs_ref, dgv_ref, mix_sc, dvn_sc, gv_acc):
        i = pl.program_id(0)

        @pl.when(i == 0)
        def _():
            dws_ref[...] = jnp.zeros_like(dws_ref)
            dbs_ref[...] = jnp.zeros_like(dbs_ref)
            gv_acc[...] = jnp.zeros_like(gv_acc)

        gu, dgu, dgv, nhat, r, vn, mixed = _sgu_forward(u_ref, v_ref, gv_ref, wm_ref, bs_ref, mix_sc, groups)
        dm = dm_ref[...]
        sa = _sigmoid(ga_ref[...].astype(F32))
        sb = _sigmoid(gb_ref[...].astype(F32))
        ov = o_ref[...]
        y_a = gu * mixed
        dg_ref[:, 0:d] = (dm * y_a * sa * (1.0 - sa)).astype(BF16)
        dg_ref[:, d:2 * d] = (dm * ov * sb * (1.0 - sb)).astype(BF16)
        do_ref[...] = dm * sb
        dy_a = dm * sa
        duv_ref[:, 0:d] = (dy_a * mixed * dgu).astype(BF16)
        dmixed = dy_a * gu
        dmixed_b = dmixed.astype(BF16)
        for g in range(groups):
            sl = slice(g * LANE, (g + 1) * LANE)
            dvn_sc[:, sl] = jnp.dot(wmt_ref[g], dmixed_b[:, sl], preferred_element_type=F32)
            dws_ref[g] += _nt_dot(dmixed_b[:, sl], vn[:, sl])
            dbs_ref[g] += jnp.sum(dmixed[:, sl], axis=-1, keepdims=True)
        dvn = dvn_sc[...]
        gv_acc[...] += _sum8(dvn * nhat)
        dn = dvn * gv_ref[...]
        dgelu = r * (dn - jnp.mean(dn, axis=-1, keepdims=True) - nhat * jnp.mean(dn * nhat, axis=-1, keepdims=True))
        duv_ref[:, d:2 * d] = (dgelu * dgv).astype(BF16)

        @pl.when(i == nt - 1)
        def _():
            dgv_ref[...] = jnp.sum(gv_acc[...], axis=0, keepdims=True)
            rr = lax.broadcasted_iota(jnp.int32, (LANE, LANE), 0)
            cl = lax.broadcasted_iota(jnp.int32, (LANE, LANE), 1)
            for g in range(groups):
                dws_ref[g] = jnp.where(rr >= cl, dws_ref[g], 0.0)

    def colblk(k):
        return pl.BlockSpec((LANE, d), lambda i, k=k: (i, k))

    full3 = pl.BlockSpec((groups, LANE, LANE), lambda i: (0, 0, 0))
    col3 = pl.BlockSpec((groups, LANE, 1), lambda i: (0, 0, 0))
    vec = pl.BlockSpec((1, d), lambda i: (0, 0))
    two = pl.BlockSpec((LANE, 2 * d), lambda i: (i, 0))
    return pl.pallas_call(
        body, grid=(nt,),
        in_specs=[colblk(0), colblk(COL_U), colblk(COL_V), colblk(COL_GA), colblk(COL_GB), colblk(0), full3, full3, col3, vec],
        out_specs=[two, two, colblk(0), full3, col3, vec],
        out_shape=[jax.ShapeDtypeStruct((t, 2 * d), BF16), jax.ShapeDtypeStruct((t, 2 * d), BF16), jax.ShapeDtypeStruct((t, d), F32),
                   jax.ShapeDtypeStruct((groups, LANE, LANE), F32), jax.ShapeDtypeStruct((groups, LANE, 1), F32),
                   jax.ShapeDtypeStruct((1, d), F32)],
        scratch_shapes=[pltpu.VMEM((LANE, d), F32), pltpu.VMEM((LANE, d), F32), pltpu.VMEM((SUBLANE, d), F32)],
        compiler_params=_cparams(("arbitrary",)), name=name,
    )(dmerged, proj, proj, proj, proj, o, wm, wm_t, bs, g_v)


def _adam_math(w, g, m, v):
    nm = ADAM_B1 * m + (1.0 - ADAM_B1) * g
    nv = ADAM_B2 * v + (1.0 - ADAM_B2) * (g * g)
    delta = -ADAM_LR * ((nm * ADAM_C1) / (jnp.sqrt(nv * ADAM_C2) + ADAM_EPS) + ADAM_WD * w)
    return delta, nm, nv


def _adamw(w, g, m, v, name):
    r, c = w.shape
    cap = max(SUBLANE, (2 * 1024 * 1024) // (4 * c) // SUBLANE * SUBLANE)
    tr = _rows(r, cap)

    def body(w_ref, g_ref, m_ref, v_ref, d_ref, nm_ref, nv_ref):
        d_ref[...], nm_ref[...], nv_ref[...] = _adam_math(w_ref[...], g_ref[...], m_ref[...], v_ref[...])

    blk = pl.BlockSpec((tr, c), lambda i: (i, 0))
    return pl.pallas_call(
        body, grid=(r // tr,), in_specs=[blk] * 4, out_specs=[blk] * 3,
        out_shape=[jax.ShapeDtypeStruct((r, c), F32)] * 3, compiler_params=_cparams(("parallel",)), name=name,
    )(w, g, m, v)


def _adamw_layers(w, g0, g1, m, v, name):
    _, r, c = w.shape
    cap = max(SUBLANE, (1024 * 1024) // (4 * c) // SUBLANE * SUBLANE)
    tr = _rows(r, cap)

    def body(w_ref, g0_ref, g1_ref, m_ref, v_ref, g_ref, d_ref, nm_ref, nv_ref):
        gg = jnp.where(pl.program_id(0) == 0, g0_ref[...], g1_ref[...])
        g_ref[0] = gg
        d_ref[0], nm_ref[0], nv_ref[0] = _adam_math(w_ref[0], gg, m_ref[0], v_ref[0])

    lay = pl.BlockSpec((1, tr, c), lambda l, i: (l, i, 0))

    def gspec(l0):
        return pl.BlockSpec((tr, c), lambda l, i: (jnp.where(l == l0, i, 0), 0))

    return pl.pallas_call(
        body, grid=(2, r // tr), in_specs=[lay, gspec(0), gspec(1), lay, lay], out_specs=[lay] * 4,
        out_shape=[jax.ShapeDtypeStruct((2, r, c), F32)] * 4, compiler_params=_cparams(("arbitrary", "arbitrary")), name=name,
    )(w, g0, g1, m, v)


def _adamw_interleaved(w, g0, g1, m, v, name):
    r, _, c = w.shape
    tr = 128

    def body(w_ref, g0_ref, g1_ref, m_ref, v_ref, g_ref, d_ref, nm_ref, nv_ref):
        for l, gl in enumerate((g0_ref, g1_ref)):
            gg = gl[...]
            g_ref[:, l, :] = gg
            d_ref[:, l, :], nm_ref[:, l, :], nv_ref[:, l, :] = _adam_math(w_ref[:, l, :], gg, m_ref[:, l, :], v_ref[:, l, :])

    lay = pl.BlockSpec((tr, 2, c), lambda i: (i, 0, 0))
    flat = pl.BlockSpec((tr, c), lambda i: (i, 0))
    return pl.pallas_call(
        body, grid=(pl.cdiv(r, tr),), in_specs=[lay, flat, flat, lay, lay], out_specs=[lay] * 4,
        out_shape=[jax.ShapeDtypeStruct((r, 2, c), F32)] * 4, compiler_params=_cparams(("parallel",)), name=name,
    )(w, g0, g1, m, v)


def _add_half(p4, recv, c_idx, name):
    _, r, c = p4.shape
    hw = c // 2
    tr = 256 if r % 256 == 0 else r

    def body(c_ref, a_ref, b_ref, o_ref):
        o_ref[...] = (a_ref[...].astype(F32) + b_ref[...].astype(F32)).astype(BF16)

    return pl.pallas_call(
        body,
        grid_spec=pltpu.PrefetchScalarGridSpec(
            num_scalar_prefetch=1, grid=(N_CHIPS, pl.cdiv(r, tr)),
            in_specs=[pl.BlockSpec((1, tr, hw), lambda s, i, cr: (s, i, cr[0])), pl.BlockSpec((1, tr, hw), lambda s, i, cr: (s, i, 0))],
            out_specs=pl.BlockSpec((1, tr, hw), lambda s, i, cr: (s, i, 0)),
        ),
        out_shape=jax.ShapeDtypeStruct((N_CHIPS, r, hw), BF16), compiler_params=_cparams(("parallel", "parallel")), name=name,
    )(c_idx, p4, recv)


def _sum_slots(x, own, sel, name, out_cols=None):
    s, r, c = x.shape
    tr = 128 if r % 128 == 0 else r

    def body(sel_ref, x_ref, own_ref, o_ref):
        mine = own_ref[0].astype(F32)
        acc = jnp.zeros((tr, c), F32)
        for k in range(s):
            acc = acc + jnp.where(sel_ref[0] == k, mine, x_ref[k].astype(F32))
        o_ref[...] = acc

    return pl.pallas_call(
        body,
        grid_spec=pltpu.PrefetchScalarGridSpec(
            num_scalar_prefetch=1, grid=(pl.cdiv(r, tr),),
            in_specs=[pl.BlockSpec((s, tr, c), lambda i, sr: (0, i, 0)), pl.BlockSpec((1, tr, c), lambda i, sr: (sr[1], i, 0))],
            out_specs=pl.BlockSpec((tr, c), lambda i, sr: (i, sr[2])),
        ),
        out_shape=jax.ShapeDtypeStruct((r, out_cols or c), F32), compiler_params=_cparams(("parallel",)), name=name,
    )(sel, x, own)


def _half_cols(width, hc):
    hw = width // 2
    assert hw % LANE == 0
    return pl.ds(pl.multiple_of(hc * hw, LANE), hw)


def _remote(src, dst, ssem, rsem, k, to):
    return pltpu.make_async_remote_copy(src_ref=src, dst_ref=dst, send_sem=ssem.at[k], recv_sem=rsem.at[k], device_id=to,
                                        device_id_type=MESH)


def _gather_job(bufs, mid_at=0.5):
    def part(o, a, slot, hc):
        return o[a].at[slot, :, _half_cols(bufs[a].shape[2], hc)]

    def first(ins, o, fresh, ssem, rsem):
        x, y, c, chips = _place()
        for a in range(len(bufs)):
            mine = part(o, a, 2 * x + y, c)
            for j, chip in enumerate(chips):
                _remote(mine, mine, ssem, rsem, 6 * a + j, (chip[0], chip[1], c)).start()

    def mid(ins, o, fresh, ssem, rsem):
        x, y, c, chips = _place()
        for a in range(len(bufs)):
            for j, chip in enumerate(chips):
                got = part(o, a, 2 * chip[0] + chip[1], c)
                _remote(got, got, ssem, rsem, 6 * a + j, (x, y, c)).wait_recv()
                _remote(got, got, ssem, rsem, 6 * a + 3 + j, (x, y, 1 - c)).start()

    def last(ins, o, fresh, ssem, rsem):
        x, y, c, chips = _place()
        for a in range(len(bufs)):
            for j, chip in enumerate(chips):
                got = part(o, a, 2 * chip[0] + chip[1], 1 - c)
                _remote(got, got, ssem, rsem, 6 * a + 3 + j, (x, y, c)).wait_recv()
        for a in range(len(bufs)):
            mine = part(o, a, 2 * x + y, c)
            for j, chip in enumerate(chips):
                _remote(mine, mine, ssem, rsem, 6 * a + j, (x, y, c)).wait_send()
                passed = part(o, a, 2 * chip[0] + chip[1], c)
                _remote(passed, passed, ssem, rsem, 6 * a + 3 + j, (x, y, c)).wait_send()

    return _Job([], bufs, [], 6 * len(bufs), first, mid, last, mid_at)


def _swap_job(p4s):
    def pairs(ins, fresh, c):
        return [(a, s, ins[a].at[s, :, _half_cols(p4s[a].shape[2], 1 - c)], fresh[a].at[s])
                for a in range(len(p4s)) for s in range(N_CHIPS)]

    def first(ins, inout, fresh, ssem, rsem):
        x, y, c, _ = _place()
        for a, s, src, dst in pairs(ins, fresh, c):
            _remote(src, dst, ssem, rsem, N_CHIPS * a + s, (x, y, 1 - c)).start()

    def last(ins, inout, fresh, ssem, rsem):
        x, y, c, _ = _place()
        for a, s, src, dst in pairs(ins, fresh, c):
            _remote(src, dst, ssem, rsem, N_CHIPS * a + s, (x, y, 1 - c)).wait()

    fresh = [jax.ShapeDtypeStruct(p.shape[:2] + (p.shape[2] // 2,), p.dtype) for p in p4s]
    return _Job(p4s, [], fresh, N_CHIPS * len(p4s), first, None, last)


def _scatter_job(parts):
    def first(ins, inout, fresh, ssem, rsem):
        x, y, c, chips = _place()
        for a in range(len(parts)):
            for j, chip in enumerate(chips):
                _remote(ins[a].at[2 * chip[0] + chip[1]], fresh[a].at[2 * x + y], ssem, rsem, 3 * a + j, (chip[0], chip[1], c)).start()

    def last(ins, inout, fresh, ssem, rsem):
        x, y, c, chips = _place()
        for a in range(len(parts)):
            for j, chip in enumerate(chips):
                slot = 2 * chip[0] + chip[1]
                _remote(ins[a].at[slot], fresh[a].at[slot], ssem, rsem, 3 * a + j, (x, y, c)).wait()

    return _Job(parts, [], [jax.ShapeDtypeStruct(p.shape, p.dtype) for p in parts], 3 * len(parts), first, None, last)


def _share_job(gs):
    def halves(o, a, c):
        width = gs[a].shape[1]
        return o[a].at[:, _half_cols(width, c)], o[a].at[:, _half_cols(width, 1 - c)]

    def first(ins, o, fresh, ssem, rsem):
        x, y, c, _ = _place()
        for a in range(len(gs)):
            mine, _ = halves(o, a, c)
            _remote(mine, mine, ssem, rsem, a, (x, y, 1 - c)).start()

    def last(ins, o, fresh, ssem, rsem):
        x, y, c, _ = _place()
        for a in range(len(gs)):
            mine, theirs = halves(o, a, c)
            _remote(mine, theirs, ssem, rsem, a, (x, y, 1 - c)).wait()

    return _Job([], gs, [], len(gs), first, None, last)


def _gather_all_job(buf):
    def peers():
        x, y, c, _ = _place()
        flips = [(fx, fy, fc) for fx in (0, 1) for fy in (0, 1) for fc in (0, 1)][1:]
        return (x, y, c), [((1 - x) if fx else x, (1 - y) if fy else y, (1 - c) if fc else c) for fx, fy, fc in flips]

    def first(ins, inout, fresh, ssem, rsem):
        (x, y, c), others = peers()
        for k, peer in enumerate(others):
            _remote(ins[0], fresh[0].at[4 * x + 2 * y + c], ssem, rsem, k, peer).start()

    def last(ins, inout, fresh, ssem, rsem):
        me, others = peers()
        for k, peer in enumerate(others):
            _remote(ins[0], fresh[0].at[4 * peer[0] + 2 * peer[1] + peer[2]], ssem, rsem, k, me).wait()

    return _Job([buf], [], [jax.ShapeDtypeStruct((N_DEV,) + buf.shape, buf.dtype)], N_DEV - 1, first, None, last)


class _SemView:
    def __init__(self, sems, off):
        self.sems, self.off = sems, off

    @property
    def at(self):
        return self

    def __getitem__(self, k):
        return self.sems.at[k + self.off]


def _join(jobs):
    spans, pos = [], [0, 0, 0, 0]
    for j in jobs:
        nxt = [pos[0] + len(j.ins), pos[1] + len(j.inout), pos[2] + len(j.fresh), pos[3] + j.nsem]
        spans.append((pos, nxt))
        pos = nxt

    def hook(which):
        fns = [getattr(j, which) for j in jobs]
        if all(f is None for f in fns):
            return None

        def run(ins, inout, fresh, ssem, rsem):
            for fn, (lo, hi) in zip(fns, spans):
                if fn is not None:
                    fn(ins[lo[0]:hi[0]], inout[lo[1]:hi[1]], fresh[lo[2]:hi[2]], _SemView(ssem, lo[3]), _SemView(rsem, lo[3]))

        return run

    mids = [j.mid_at for j in jobs if j.mid is not None]
    joined = _Job([a for j in jobs for a in j.ins], [a for j in jobs for a in j.inout], [a for j in jobs for a in j.fresh],
                  pos[3], hook("first"), hook("mid"), hook("last"), max(mids) if mids else 0.5)
    n_io = pos[1]

    def split(moved):
        return [list(moved[lo[1]:hi[1]]) + list(moved[n_io + lo[2]:n_io + hi[2]]) for lo, hi in spans]

    return joined, split


def _carrying(stages, call):
    stages = [s for s in stages if s is not None]
    if not stages:
        return call(None)
    job, split = _join([s[0] for s in stages])
    out, moved = call(job)
    for (_, done), part in zip(stages, split(moved)):
        done(part)
    return out


def _layer_forward(x, h, w_in_t, rest, sm, g_next, d, stages=None):
    stages = stages or {}
    proj = _carrying([stages.get("proj")], lambda job: _matmul(h, w_in_t, "nt", BF16, "proj_fwd", n=7 * d, tn_cap=1792, job=job))
    f_t = _matmul(w_in_t[7 * d:], h, "nt", F32, "forget_fwd", tn_cap=1024)
    c_t = _fox_prep(f_t, sm["b_f"], "fox_prep")
    o, lse = _carrying([stages.get("attn")], lambda job: _attn_fwd(proj, c_t, d, "attn_fwd", job=job))
    wts = rest()
    merged = _mix_fwd(proj, o, sm["wm"], sm["bs"], sm["g_v"], d, "mix_fwd")
    z, x1, h2 = _matmul(merged, wts["w_out"], "nn", F32, "out_fwd", norms=(x, sm["g_post"], sm["g_fpre"]))
    a = _carrying([stages.get("gate")], lambda job: _matmul(h2, wts["w_g_t"], "nt", BF16, "gate_fwd", tn_cap=1408, job=job))
    b = _carrying([stages.get("up")], lambda job: _matmul(h2, wts["w_u_t"], "nt", BF16, "up_fwd", tn_cap=1408, job=job))
    mm = _swiglu_fwd(a, b, "swiglu_fwd")
    z2, x_out, h_out = _carrying([stages.get("down")], lambda job: _matmul(
        mm, wts["w_d"], "nn", F32, "down_fwd", norms=(x1, sm["g_fpost"], g_next), job=job))
    return dict(x=x, h=h, proj=proj, f_t=f_t, c_t=c_t, o=o, lse=lse, merged=merged, z=z, x1=x1,
                h2=h2, a=a, b=b, mm=mm, z2=z2, x_out=x_out, h_out=h_out)


class _GradExchange:
    def __init__(self, pay, keys, c_idx, chip):
        self.keys = list(keys)
        self.p4 = [pay[k].reshape(N_CHIPS, pay[k].shape[1] // N_CHIPS, pay[k].shape[2]) for k in self.keys]
        self.c_idx = c_idx
        self.sel = jnp.stack([chip, chip, c_idx[0]]).astype(jnp.int32)
        self.done = 0

    def _after_swap(self, landed):
        self.parts = [_add_half(p, r, self.c_idx, "add_sibling") for p, r in zip(self.p4, landed)]
        self.done = 1

    def _after_scatter(self, landed):
        self.g = [_sum_slots(got, sent, self.sel, "sum_chips", out_cols=p.shape[2])
                  for got, sent, p in zip(landed, self.parts, self.p4)]
        self.done = 2

    def _after_share(self, moved):
        self.g = list(moved)
        self.done = 3

    def stage(self):
        if self.done == 0:
            return _swap_job(self.p4), self._after_swap
        if self.done == 1:
            return _scatter_job(self.parts), self._after_scatter
        if self.done == 2:
            return _share_job(self.g), self._after_share
        return None

    def run(self):
        for name in ("swap_grads", "scatter_grads", "share_grads")[self.done:]:
            job, done = self.stage()
            done(_run_job(job, name))

    def grads(self):
        return dict(zip(self.keys, self.g))


EARLY_KEYS = ("w_d", "w_g", "w_u", "w_out")


def _layer_backward(dz2, dx2, sv, wts, sm, d, c_idx, chip, carried=(), split_own=False, small_stage=None):
    t = dx2.shape[0]
    heads = d // LANE
    ff = wts["w_d"].shape[0]
    in_w = 7 * d + heads
    g, pay = {}, {}
    carried = list(carried)

    def payload(key, a, b, rows, row0, name, extra=()):
        def call(job):
            return _matmul(a, b, "tn", BF16, name, slab=((1, rows, d), 0, row0), into=pay.get(key), job=job, tm_cap=1408,
                           tn_cap=1024, tk_cap=1024)
        pay[key] = _carrying(list(extra), call)

    def stages():
        return [ex.stage() for ex in carried]

    dm = _carrying(stages(), lambda job: _matmul(dz2, wts["w_d"], "nt", BF16, "down_bwd_x", tn_cap=1408, tk_cap=1024, job=job))
    payload("w_d", sv["mm"], dz2, ff, 0, "down_bwd_w")
    da, db = _swiglu_bwd(sv["a"], sv["b"], dm, "swiglu_bwd")
    dh2 = _matmul_pieces([(da, wts["w_g_t"], 0), (db, wts["w_u_t"], 0)], None, "gu_bwd_x", tk=_tile(ff, 1408))
    payload("w_g", da, sv["h2"], ff, 0, "gate_bwd_w")
    payload("w_u", db, sv["h2"], ff, 0, "up_bwd_w")
    dx1, dz, g["g_fpre"], g["g_post"] = _norm_bwd(dx2, (dh2, sv["x1"], sm["g_fpre"]), (sv["z"], sm["g_post"]), "norm_bwd_mid")
    dmerged = _matmul(dz, wts["w_out"], "nt", F32, "out_bwd_x", tk_cap=1024)
    payload("w_out", sv["merged"], dz, d, 0, "out_bwd_w")
    d_uv, d_g, do, g["w_s"], g["b_s"], g["g_v"] = _mix_bwd(dmerged, sv["proj"], sv["o"], sm["wm"], sm["wm_t"], sm["bs"],
                                                         sm["g_v"], d, "mix_bwd")
    own = []
    if split_own:
        own.append(_GradExchange(pay, EARLY_KEYS, c_idx, chip))
        carried.append(own[0])
    extra = [small_stage(g)] if small_stage is not None else []
    attn_args = (sv["proj"], do, sv["o"], sv["lse"], sv["c_t"], d)
    dq, dk, dv, dc_q, dc_k = _carrying(stages() + extra, lambda job: _attn_bwd(*attn_args, "attn_bwd", job=job))
    df_t, g["b_f"] = _fox_bwd(dc_q, dc_k, sv["f_t"], sm["b_f"], "fox_bwd")
    df_b = df_t.astype(BF16)
    dh_f = _matmul(df_b, wts["w_in_t"][7 * d:], "tn", F32, "forget_bwd_x")
    pieces = [(d_uv, COL_U), (dq, COL_Q), (dk, COL_K), (dv, COL_VA), (d_g, COL_GA)]
    ops = [(p, wts["w_in_t"], col * d) for p, col in pieces]
    dh = _carrying(stages(), lambda job: _matmul_pieces(ops, dh_f, "proj_bwd_x", job=job, tk=_tile(d, 1024)))
    for p, col in pieces:
        payload("w_in", p, sv["h"], in_w, col * d, "proj_bwd_w", extra=stages())
    w_f_rows = _matmul(df_b, sv["h"], "nn", BF16, "forget_bwd_w", tk_cap=1024)
    pay["w_in"] = lax.dynamic_update_slice(pay["w_in"], w_f_rows[None], (0, 7 * d, 0))
    own.append(_GradExchange(pay, [k for k in ("w_in",) + EARLY_KEYS if not (split_own and k in EARLY_KEYS)], c_idx, chip))
    return dh, dx1, g, own


def _small_pack(parts):
    flat = jnp.concatenate([p.reshape(-1) for p in parts])
    n = flat.shape[0]
    pad = (-n) % (LANE * LANE)
    return jnp.pad(flat, (0, pad)).reshape(-1, LANE)


def kernel(x, mix_pre_g, w_in, b_forget, sgu_norm_g, w_spatial, b_spatial, w_out, mix_post_g, ffn_pre_g, w_gate, w_up, w_down, ffn_post_g, loss_target, m_mix_pre_g, m_w_in, m_b_forget, m_sgu_norm_g, m_w_spatial, m_b_spatial, m_w_out, m_mix_post_g, m_ffn_pre_g, m_w_gate, m_w_up, m_w_down, m_ffn_post_g, v_mix_pre_g, v_w_in, v_b_forget, v_sgu_norm_g, v_w_spatial, v_b_spatial, v_w_out, v_mix_post_g, v_ffn_pre_g, v_w_gate, v_w_up, v_w_down, v_ffn_post_g):
    depth, d = mix_pre_g.shape
    assert depth == 2, "core c of a chip owns layer c"
    heads = d // LANE
    t = x.shape[1]
    ff = w_down.shape[1] * N_CHIPS
    in_w = w_in.shape[2] * N_CHIPS
    assert in_w == 7 * d + heads
    xs = x.reshape(t, d)
    target = loss_target.reshape(t, d)
    c_idx = lax.axis_index("c").astype(jnp.int32).reshape(1)
    chip = 2 * lax.axis_index("x") + lax.axis_index("y")
    dev = 2 * chip + lax.axis_index("c")

    def in_view(w):
        return jnp.transpose(w, (2, 0, 1))

    def gu_view(w):
        return jnp.transpose(w, (0, 2, 1))

    own = [jnp.transpose(in_view(w_in).astype(BF16), (1, 0, 2)), w_out.astype(BF16), gu_view(w_gate).astype(BF16),
           gu_view(w_up).astype(BF16), w_down.astype(BF16)]
    bufs = [[lax.dynamic_update_slice(lax.empty((N_CHIPS,) + o.shape[1:], BF16), o[l][None], (chip, 0, 0)) for o in own]
            for l in range(depth)]
    first_in = _run_job(_gather_job([bufs[0][0]]), "gather_first")[0]

    def weights(g_in, g_out, g_g, g_u, g_d):
        return dict(w_in_t=g_in.reshape(in_w, d), w_out=g_out.reshape(d, d), w_g_t=g_g.reshape(ff, d),
                    w_u_t=g_u.reshape(ff, d), w_d=g_d.reshape(ff, d))

    tril = jnp.tril(jnp.ones((LANE, LANE), bool))
    smalls = []
    for l in range(depth):
        wm = jnp.where(tril[None], w_spatial[l], 0.0).astype(BF16)
        smalls.append(dict(
            b_f=b_forget[l].reshape(heads, 1), wm=wm, wm_t=jnp.swapaxes(wm, 1, 2), bs=b_spatial[l].reshape(heads, LANE, 1),
            g_v=sgu_norm_g[l].reshape(1, d), g_pre=mix_pre_g[l].reshape(1, d), g_post=mix_post_g[l].reshape(1, d),
            g_fpre=ffn_pre_g[l].reshape(1, d), g_fpost=ffn_post_g[l].reshape(1, d)))

    wts, later = [], {}

    def keep(key):
        def done(moved):
            later[key] = list(moved)
        return done

    def rest_first():
        wts.append(weights(first_in, *later["rest0"]))
        return wts[0]

    stages = dict(proj=(_gather_job(bufs[0][1:], mid_at=1.0), keep("rest0")),
                  attn=(_gather_job(bufs[1][0:2], mid_at=0.7), keep("in_out1")),
                  gate=(_gather_job(bufs[1][2:3], mid_at=1.0), keep("g1")), up=(_gather_job(bufs[1][3:4], mid_at=1.0), keep("u1")),
                  down=(_gather_job(bufs[1][4:5], mid_at=1.0), keep("d1")))
    h = _norm_fwd(xs, None, None, smalls[0]["g_pre"], "norm_first")
    g_after = [smalls[min(l + 1, depth - 1)]["g_pre"] for l in range(depth)]
    saved = [_layer_forward(xs, h, first_in.reshape(in_w, d), rest_first, smalls[0], g_after[0], d, stages)]
    wts.append(weights(*later["in_out1"], later["g1"][0], later["u1"][0], later["d1"][0]))
    for l in range(1, depth):
        saved.append(_layer_forward(saved[l - 1]["x_out"], saved[l - 1]["h_out"], wts[l]["w_in_t"], lambda l=l: wts[l], smalls[l],
                                    g_after[l], d))
    dy, loss_part = _loss_grad(saved[-1]["x_out"], target, "loss")
    loss = lax.psum(jnp.sum(loss_part), ("x", "y", "c"))

    small_shapes = dict(g_pre=(d,), b_f=(heads,), g_v=(d,), w_s=w_spatial.shape[1:], b_s=b_spatial.shape[1:], g_post=(d,),
                        g_fpre=(d,), g_fpost=(d,))
    late_entries = [(0, "g_pre"), (0, "b_f")]
    early_entries = [(l, n) for l in reversed(range(depth)) for n in small_shapes if (l, n) not in late_entries]
    dev_sel = jnp.stack([dev, jnp.zeros_like(dev), jnp.zeros_like(dev)]).astype(jnp.int32)
    small_sum = {}

    def small_exchange(entries, values):
        packed = _small_pack([values[e].reshape(-1) for e in entries])

        def done(moved):
            total = _sum_slots(moved[0], packed[None], dev_sel, "sum_small").reshape(-1)
            off = 0
            for e in entries:
                n = math.prod(small_shapes[e[1]])
                small_sum[e] = total[off:off + n].reshape(small_shapes[e[1]])
                off += n

        return _gather_all_job(packed), done

    grads = [None] * depth
    exchanges = [None] * depth
    dx2 = dy
    dz2, g_fpost = _norm_bwd(dx2, None, (saved[depth - 1]["z2"], smalls[depth - 1]["g_fpost"]), "norm_bwd_top")
    for l in reversed(range(depth)):
        last = l == 0

        def small_stage(g, l=l, g_fpost=g_fpost):
            known = {(k, n): grads[k][n] for k in range(l + 1, depth) for n in small_shapes}
            known.update({(l, n): g[n] for n in g})
            known[(l, "g_fpost")] = g_fpost
            return small_exchange(early_entries, known)

        carried = [ex for k in range(l + 1, depth) for ex in exchanges[k]]
        dh, dx1, g, exchanges[l] = _layer_backward(dz2, dx2, saved[l], wts[l], smalls[l], d, c_idx, chip, carried=carried,
                                                    split_own=last, small_stage=small_stage if last else None)
        g["g_fpost"] = g_fpost
        if l > 0:
            dx2, dz2, g["g_pre"], g_fpost = _norm_bwd(dx1, (dh, saved[l]["x"], smalls[l]["g_pre"]),
                                                       (saved[l - 1]["z2"], smalls[l - 1]["g_fpost"]), "norm_bwd_between")
        else:
            grad_x, g["g_pre"] = _norm_bwd(dx1, (dh, saved[l]["x"], smalls[l]["g_pre"]), None, "norm_bwd_bottom")
        grads[l] = g
    job, done = small_exchange(late_entries, {(0, n): grads[0][n] for n in ("g_pre", "b_f")})
    done(_run_job(job, "gather_small"))
    big = [{} for _ in range(depth)]
    for l in range(depth):
        for ex in exchanges[l]:
            ex.run()
            big[l].update(ex.grads())
    small_grads = {n: jnp.stack([small_sum[(l, n)] for l in range(depth)]) for n in small_shapes}

    def adam_small(w, g, m, v):
        shp = w.shape
        if w.ndim >= 3 and shp[-1] >= LANE:
            two = (math.prod(shp[:-1]), shp[-1])
        else:
            two = (1, math.prod(shp)) if math.prod(shp) < LANE else (math.prod(shp) // LANE, LANE)
        outs = _adamw(w.reshape(two), g.reshape(two), m.reshape(two), v.reshape(two), "adamw")
        return [g] + [o.reshape(shp) for o in outs]

    def adam_in(w, m, v):
        outs = _adamw_interleaved(in_view(w), big[0]["w_in"], big[1]["w_in"], in_view(m), in_view(v), "adamw_in")
        return [jnp.transpose(o, (1, 2, 0)) for o in outs]

    def adam_gu(k, w, m, v):
        outs = _adamw_layers(gu_view(w), big[0][k], big[1][k], gu_view(m), gu_view(v), "adamw_layers")
        return [jnp.transpose(o, (0, 2, 1)) for o in outs]

    def adam_rows(k, w, m, v):
        return _adamw_layers(w, big[0][k], big[1][k], m, v, "adamw_layers")

    results = [
        adam_small(mix_pre_g, small_grads["g_pre"], m_mix_pre_g, v_mix_pre_g),
        adam_in(w_in, m_w_in, v_w_in),
        adam_small(b_forget, small_grads["b_f"], m_b_forget, v_b_forget),
        adam_small(sgu_norm_g, small_grads["g_v"], m_sgu_norm_g, v_sgu_norm_g),
        adam_small(w_spatial, small_grads["w_s"], m_w_spatial, v_w_spatial),
        adam_small(b_spatial, small_grads["b_s"], m_b_spatial, v_b_spatial),
        adam_rows("w_out", w_out, m_w_out, v_w_out),
        adam_small(mix_post_g, small_grads["g_post"], m_mix_post_g, v_mix_post_g),
        adam_small(ffn_pre_g, small_grads["g_fpre"], m_ffn_pre_g, v_ffn_pre_g),
        adam_gu("w_g", w_gate, m_w_gate, v_w_gate),
        adam_gu("w_u", w_up, m_w_up, v_w_up),
        adam_rows("w_d", w_down, m_w_down, v_w_down),
        adam_small(ffn_post_g, small_grads["g_fpost"], m_ffn_post_g, v_ffn_post_g),
    ]
    gs, deltas, new_ms, new_vs = zip(*results)
    return (loss, grad_x.reshape(x.shape), *gs, *deltas, *new_ms, *new_vs)
```

```python
import functools
import math

import jax
import jax.numpy as jnp
from jax import lax
from jax.experimental import pallas as pl
from jax.experimental.pallas import tpu as pltpu

F32 = jnp.float32
BF16 = jnp.bfloat16

EPS = 1e-6
LANE = 128
SUBLANE = 8
N_CHIPS = 4
N_DEV = 8
VMEM_LIMIT = 48 * 1024 * 1024
MESH = pl.DeviceIdType.MESH

ADAM_LR = 0.001
ADAM_B1 = 0.9
ADAM_B2 = 0.999
ADAM_EPS = 1e-08
ADAM_WD = 0.01
ADAM_STEP = 10
ADAM_C1 = 1.0 / (1.0 - ADAM_B1 ** ADAM_STEP)
ADAM_C2 = 1.0 / (1.0 - ADAM_B2 ** ADAM_STEP)

GELU_K = math.sqrt(2.0 / math.pi)
GELU_A = 0.044715
NEG = -1e30
LOG2E = 1.4426950408889634
LN2 = 0.6931471805599453

COL_U, COL_V, COL_Q, COL_K, COL_VA, COL_GA, COL_GB, COL_F = range(8)


def _cparams(sem=None):
    return pltpu.CompilerParams(dimension_semantics=sem, vmem_limit_bytes=VMEM_LIMIT)


def _tile(n, cap):
    best = None
    for t in range(LANE, min(n, cap) + 1, LANE):
        if n % t == 0:
            best = t
    return best if best is not None else n


def _rows(n, cap):
    best = None
    for t in range(SUBLANE, min(n, cap) + 1, SUBLANE):
        if n % t == 0:
            best = t
    return best if best is not None else n


def _gelu_and_grad(x):
    x2 = x * x
    t = jnp.tanh(GELU_K * (x + GELU_A * x2 * x))
    g = 0.5 * x * (1.0 + t)
    dg = 0.5 * (1.0 + t) + 0.5 * x * (1.0 - t * t) * (GELU_K * (1.0 + 3.0 * GELU_A * x2))
    return g, dg


def _sigmoid(x):
    return 1.0 / (1.0 + jnp.exp(-x))


def _sum8(v):
    n, d = v.shape
    return v.reshape(n // SUBLANE, SUBLANE, d).sum(axis=0)


def _nt_dot(a, b):
    return lax.dot_general(a, b, (((1,), (1,)), ((), ())), preferred_element_type=F32)


_HBM = pl.BlockSpec(memory_space=pl.ANY)


def _place():
    x, y, c = lax.axis_index("x"), lax.axis_index("y"), lax.axis_index("c")
    chips = [(1 - x, y), (x, 1 - y), (1 - x, 1 - y)]
    return x, y, c, chips


class _Job:
    def __init__(self, ins, inout, fresh, nsem, first, mid, last, mid_at=0.5):
        self.ins, self.inout, self.fresh, self.nsem = list(ins), list(inout), list(fresh), nsem
        self.first, self.mid, self.last, self.mid_at = first, mid, last, mid_at


def _call(body, *, grid, in_specs, out_specs, out_shape, scratch_shapes, dims, name, args, aliases=None, job=None):
    single = not isinstance(out_shape, (list, tuple))
    out_specs = [out_specs] if single else list(out_specs)
    out_shape = [out_shape] if single else list(out_shape)
    aliases = dict(aliases or {})
    if job is None:
        outs = pl.pallas_call(body, grid=grid, in_specs=in_specs, out_specs=out_specs, out_shape=out_shape,
                              scratch_shapes=scratch_shapes, input_output_aliases=aliases, compiler_params=_cparams(dims),
                              name=name)(*args)
        return (outs[0] if single else outs), []
    n_in, n_out, n_scr = len(args), len(out_shape), len(scratch_shapes)
    n_ji, n_jio, n_jf = len(job.ins), len(job.inout), len(job.fresh)
    total = math.prod(grid)

    def wrapped(*refs):
        host_in = refs[:n_in]
        pos = n_in
        j_ins = refs[pos:pos + n_ji]
        pos += n_ji + n_jio
        host_out = refs[pos:pos + n_out]
        pos += n_out
        j_inout = refs[pos:pos + n_jio]
        pos += n_jio
        j_fresh = refs[pos:pos + n_jf]
        pos += n_jf
        host_scr = refs[pos:pos + n_scr]
        ssem, rsem = refs[pos + n_scr:]
        flat = 0
        for ax, size in enumerate(grid):
            flat = flat * size + pl.program_id(ax)

        def hook(fn, at):
            if fn is not None:
                @pl.when(flat == at)
                def _():
                    fn(j_ins, j_inout, j_fresh, ssem, rsem)

        hook(job.first, 0)
        body(*host_in, *host_out, *host_scr)
        hook(job.mid, min(int(total * job.mid_at), total - 1))
        hook(job.last, total - 1)

    for k in range(n_jio):
        aliases[n_in + n_ji + k] = n_out + k
    outs = pl.pallas_call(
        wrapped, grid=grid,
        in_specs=list(in_specs) + [_HBM] * (n_ji + n_jio),
        out_specs=out_specs + [_HBM] * (n_jio + n_jf),
        out_shape=out_shape + [jax.ShapeDtypeStruct(b.shape, b.dtype) for b in job.inout] + list(job.fresh),
        scratch_shapes=list(scratch_shapes) + [pltpu.SemaphoreType.DMA((job.nsem,)), pltpu.SemaphoreType.DMA((job.nsem,))],
        input_output_aliases=aliases, compiler_params=_cparams(tuple("arbitrary" for _ in grid)), name=name,
    )(*args, *job.ins, *job.inout)
    host = outs[:n_out]
    return (host[0] if single else host), outs[n_out:]


def _run_job(job, name):
    n_ji, n_jio, n_jf = len(job.ins), len(job.inout), len(job.fresh)

    def body(*refs):
        j_ins = refs[:n_ji]
        pos = n_ji + n_jio
        j_inout = refs[pos:pos + n_jio]
        j_fresh = refs[pos + n_jio:pos + n_jio + n_jf]
        ssem, rsem = refs[pos + n_jio + n_jf:]
        for fn in (job.first, job.mid, job.last):
            if fn is not None:
                fn(j_ins, j_inout, j_fresh, ssem, rsem)

    return pl.pallas_call(
        body, in_specs=[_HBM] * (n_ji + n_jio), out_specs=[_HBM] * (n_jio + n_jf),
        out_shape=[jax.ShapeDtypeStruct(b.shape, b.dtype) for b in job.inout] + list(job.fresh),
        scratch_shapes=[pltpu.SemaphoreType.DMA((job.nsem,)), pltpu.SemaphoreType.DMA((job.nsem,))],
        input_output_aliases={n_ji + k: k for k in range(n_jio)}, name=name,
    )(*job.ins, *job.inout)


_DIMS ={"nn": ((1,), (0,)), "nt": ((1,), (1,)), "tn": ((0,), (0,))}


def _matmul(a, b, mode, out_dtype, name, n=None, slab=None, into=None, job=None, norms=None, tm_cap=512, tn_cap=2048,
            tk_cap=1408):
    if mode == "nn":
        (m, k), (k2, nn_) = a.shape, b.shape
    elif mode == "nt":
        (m, k), (nn_, k2) = a.shape, b.shape
    else:
        (k, m), (k2, nn_) = a.shape, b.shape
    n = nn_ if n is None else n
    assert k == k2, (a.shape, b.shape, mode)
    tm, tn, tk = _tile(m, tm_cap), _tile(n, tn_cap), _tile(k, tk_cap)
    if slab is not None and slab[2]:
        tm = _tile(math.gcd(m, slab[2]), tm_cap)
    nk = k // tk
    if mode == "tn":
        a_spec = pl.BlockSpec((tk, tm), lambda j, i, kk, *_: (kk, i))
    else:
        a_spec = pl.BlockSpec((tm, tk), lambda j, i, kk, *_: (i, kk))
    if mode == "nt":
        b_spec = pl.BlockSpec((tn, tk), lambda j, i, kk, *_: (j, kk))
    else:
        b_spec = pl.BlockSpec((tk, tn), lambda j, i, kk, *_: (kk, j))
    dims = (_DIMS[mode], ((), ()))
    aliased = into is not None

    n_in = 2 + aliased + (3 if norms is not None else 0)

    def finish(refs, z):
        refs[n_in][...] = z.astype(out_dtype).reshape(refs[n_in].shape)
        if norms is not None:
            x_ref, gp_ref, gn_ref = refs[n_in - 3:n_in]
            r = lax.rsqrt(jnp.mean(z * z, axis=-1, keepdims=True) + EPS)
            xn = x_ref[...] + z * r * gp_ref[...]
            refs[n_in + 1][...] = xn
            r2 = lax.rsqrt(jnp.mean(xn * xn, axis=-1, keepdims=True) + EPS)
            refs[n_in + 2][...] = (xn * r2 * gn_ref[...]).astype(BF16)

    def body(*refs):
        p = lax.dot_general(refs[0][...], refs[1][...], dims, preferred_element_type=F32)
        if nk == 1:
            finish(refs, p)
        else:
            acc = refs[-1]
            kk = pl.program_id(2)

            @pl.when(kk == 0)
            def _():
                acc[...] = p

            @pl.when(kk > 0)
            def _():
                acc[...] += p

            @pl.when(kk == nk - 1)
            def _():
                finish(refs, acc[...])

    if slab is None:
        out_spec = pl.BlockSpec((tm, tn), lambda j, i, kk: (i, j))
        out_shape = jax.ShapeDtypeStruct((m, n), out_dtype)
    else:
        shape3, lead, row0 = slab
        assert row0 % tm == 0 and shape3[2] == n
        out_spec = pl.BlockSpec((1, tm, tn), lambda j, i, kk: (lead, row0 // tm + i, j))
        out_shape = jax.ShapeDtypeStruct(shape3, out_dtype)
    in_specs, args = [a_spec, b_spec], [a, b]
    if aliased:
        in_specs.append(pl.BlockSpec(memory_space=pl.ANY))
        args.append(into)
    if norms is not None:
        assert tn == n and slab is None, "the fused norms need whole rows"
        row = pl.BlockSpec((tm, n), lambda j, i, kk: (i, 0))
        vec = pl.BlockSpec((1, n), lambda j, i, kk: (0, 0))
        in_specs += [row, vec, vec]
        args += list(norms)
        out_spec = [out_spec, row, row]
        out_shape = [out_shape, jax.ShapeDtypeStruct((m, n), F32), jax.ShapeDtypeStruct((m, n), BF16)]
    out, moved = _call(
        body, grid=(n // tn, m // tm, nk), in_specs=in_specs, out_specs=out_spec, out_shape=out_shape,
        scratch_shapes=[pltpu.VMEM((tm, tn), F32)] if nk > 1 else [], dims=("parallel", "parallel", "arbitrary"), name=name,
        args=args, aliases={2: 0} if aliased else None, job=job)
    return out if job is None else (out, moved)


def _matmul_pieces(pieces, addend, name, tk, job=None, tm_cap=512):
    m = pieces[0][0].shape[0]
    n = pieces[0][1].shape[1]
    tm = _tile(m, tm_cap)
    spans, s0 = [], 0
    for a, b, row0 in pieces:
        assert a.shape[1] % tk == 0 and row0 % tk == 0 and b.shape[1] == n and a.shape[0] == m
        spans.append((s0, a.shape[1] // tk, row0 // tk))
        s0 += a.shape[1] // tk
    steps = s0
    np_ = len(pieces)
    groups = []
    for (a, b, _), (first, count, brow) in zip(pieces, spans):
        if groups and groups[-1][0] is b and groups[-1][3] + groups[-1][2] == brow:
            groups[-1][2] += count
        else:
            groups.append([b, first, count, brow])
    b_of = []
    for first, count, _ in spans:
        b_of.append(next(k for k, g in enumerate(groups) if g[1] <= first < g[1] + g[2]))
    ng = len(groups)

    nm = m // tm

    def body(*refs):
        o_ref, acc = refs[-2], refs[-1]
        s, i = pl.program_id(0), pl.program_id(1)
        rows = pl.ds(pl.multiple_of(i * tm, tm), tm)

        @pl.when(s == 0)
        def _():
            acc[rows, :] = refs[np_ + ng][...] if addend is not None else jnp.zeros((tm, n), F32)

        for p, (first, count, _) in enumerate(spans):
            @pl.when((s >= first) & (s < first + count))
            def _(p=p):
                acc[rows, :] += jnp.dot(refs[p][...], refs[np_ + b_of[p]][...], preferred_element_type=F32)

        @pl.when(s == steps - 1)
        def _():
            o_ref[...] = acc[rows, :]

    in_specs, args = [], []
    for (a, _, _), (first, count, _) in zip(pieces, spans):
        in_specs.append(pl.BlockSpec((tm, tk), lambda s, i, f=first, c=count: (
            jnp.where(s < f, 0, jnp.where(s >= f + c, nm - 1, i)), jnp.clip(s - f, 0, c - 1))))
        args.append(a)
    for b, first, count, brow in groups:
        in_specs.append(pl.BlockSpec((tk, n), lambda s, i, f=first, c=count, r=brow: (r + jnp.clip(s - f, 0, c - 1), 0)))
        args.append(b)
    if addend is not None:
        in_specs.append(pl.BlockSpec((tm, n), lambda s, i: (jnp.where(s == 0, i, nm - 1), 0)))
        args.append(addend)
    out, moved = _call(
        body, grid=(steps, nm), in_specs=in_specs,
        out_specs=pl.BlockSpec((tm, n), lambda s, i: (jnp.where(s == steps - 1, i, 0), 0)),
        out_shape=jax.ShapeDtypeStruct((m, n), F32), scratch_shapes=[pltpu.VMEM((m, n), F32)],
        dims=("arbitrary", "arbitrary"), name=name, args=args, job=job)
    return out if job is None else (out, moved)


def _norm_fwd(x, z, g_post, g_next, name):
    t, d = x.shape
    tt = _rows(t, 512)
    row = pl.BlockSpec((tt, d), lambda i: (i, 0))
    vec = pl.BlockSpec((1, d), lambda i: (0, 0))

    def body(*refs):
        if z is None:
            x_ref, gn_ref, h_ref = refs
            xn = x_ref[...]
        else:
            x_ref, z_ref, gp_ref, gn_ref, xo_ref, h_ref = refs
            zz = z_ref[...]
            r = lax.rsqrt(jnp.mean(zz * zz, axis=-1, keepdims=True) + EPS)
            xn = x_ref[...] + zz * r * gp_ref[...]
            xo_ref[...] = xn
        r2 = lax.rsqrt(jnp.mean(xn * xn, axis=-1, keepdims=True) + EPS)
        h_ref[...] = (xn * r2 * gn_ref[...]).astype(BF16)

    if z is None:
        return pl.pallas_call(
            body, grid=(t // tt,), in_specs=[row, vec], out_specs=row,
            out_shape=jax.ShapeDtypeStruct((t, d), BF16), compiler_params=_cparams(("parallel",)), name=name,
        )(x, g_next)
    return pl.pallas_call(
        body, grid=(t // tt,), in_specs=[row, row, vec, vec], out_specs=[row, row],
        out_shape=[jax.ShapeDtypeStruct((t, d), F32), jax.ShapeDtypeStruct((t, d), BF16)],
        compiler_params=_cparams(("parallel",)), name=name,
    )(x, z, g_post, g_next)


def _rms_bwd(dy, x, g):
    r = lax.rsqrt(jnp.mean(x * x, axis=-1, keepdims=True) + EPS)
    n = x * r
    dn = dy * g
    dx = r * (dn - n * jnp.mean(dn * n, axis=-1, keepdims=True))
    return dx, dy * n


def _norm_bwd(dres, pre, post, name):
    t, d = dres.shape
    tt = _rows(t, 512)
    nt = t // tt
    row = pl.BlockSpec((tt, d), lambda i: (i, 0))
    vec = pl.BlockSpec((1, d), lambda i: (0, 0))
    has_pre, has_post = pre is not None, post is not None
    n_in = 1 + (3 if has_pre else 0) + (2 if has_post else 0)
    n_out = has_pre + has_post + has_pre + has_post

    def body(*refs):
        ins, outs, scr = refs[:n_in], refs[n_in:n_in + n_out], refs[n_in + n_out:]
        i = pl.program_id(0)
        dx = ins[0][...]
        pos, opos, spos = 1, 0, 0
        accs = []
        if has_pre:
            dh_ref, xa_ref, ga_ref = ins[pos:pos + 3]
            pos += 3
            dxa, dga_t = _rms_bwd(dh_ref[...], xa_ref[...], ga_ref[...])
            dx = dx + dxa
            outs[opos][...] = dx
            opos += 1
            accs.append((scr[spos], dga_t))
            spos += 1
        if has_post:
            zb_ref, gb_ref = ins[pos:pos + 2]
            dz, dgb_t = _rms_bwd(dx, zb_ref[...], gb_ref[...])
            outs[opos][...] = dz.astype(BF16)
            opos += 1
            accs.append((scr[spos], dgb_t))
            spos += 1
        for (acc, val), out in zip(accs, outs[opos:]):
            part = _sum8(val)

            @pl.when(i == 0)
            def _(acc=acc, part=part):
                acc[...] = part

            @pl.when(i > 0)
            def _(acc=acc, part=part):
                acc[...] += part

            @pl.when(i == nt - 1)
            def _(acc=acc, out=out):
                out[...] = jnp.sum(acc[...], axis=0, keepdims=True)

    in_specs, args = [row], [dres]
    out_specs, out_shape = [], []
    if has_pre:
        in_specs += [row, row, vec]
        args += list(pre)
        out_specs.append(row)
        out_shape.append(jax.ShapeDtypeStruct((t, d), F32))
    if has_post:
        in_specs += [row, vec]
        args += list(post)
        out_specs.append(row)
        out_shape.append(jax.ShapeDtypeStruct((t, d), BF16))
    for _ in range(has_pre + has_post):
        out_specs.append(vec)
        out_shape.append(jax.ShapeDtypeStruct((1, d), F32))
    return pl.pallas_call(
        body, grid=(nt,), in_specs=in_specs, out_specs=out_specs, out_shape=out_shape,
        scratch_shapes=[pltpu.VMEM((SUBLANE, d), F32)] * (has_pre + has_post),
        compiler_params=_cparams(("arbitrary",)), name=name,
    )(*args)


def _loss_grad(y, target, name):
    t, d = y.shape
    tt = _rows(t, 512)
    nt = t // tt
    row = pl.BlockSpec((tt, d), lambda i: (i, 0))
    inv_d = 1.0 / d

    def body(y_ref, t_ref, dy_ref, l_ref):
        i = pl.program_id(0)
        diff = y_ref[...] - t_ref[...]
        dy_ref[...] = diff * inv_d
        s8 = _sum8(diff * diff)
        part = s8[:, 0:LANE]
        for k in range(1, d // LANE):
            part = part + s8[:, k * LANE:(k + 1) * LANE]
        part = part * (0.5 * inv_d)

        @pl.when(i == 0)
        def _():
            l_ref[...] = part

        @pl.when(i > 0)
        def _():
            l_ref[...] += part

    return pl.pallas_call(
        body, grid=(nt,), in_specs=[row, row],
        out_specs=[row, pl.BlockSpec((SUBLANE, LANE), lambda i: (0, 0))],
        out_shape=[jax.ShapeDtypeStruct((t, d), F32), jax.ShapeDtypeStruct((SUBLANE, LANE), F32)],
        compiler_params=_cparams(("arbitrary",)), name=name,
    )(y, target)


def _swiglu_fwd(a, b, name):
    t, f = a.shape
    tt = _rows(t, 256)
    blk = pl.BlockSpec((tt, f), lambda i: (i, 0))

    def body(a_ref, b_ref, m_ref):
        av = a_ref[...].astype(F32)
        m_ref[...] = (av * _sigmoid(av) * b_ref[...].astype(F32)).astype(BF16)

    return pl.pallas_call(
        body, grid=(t // tt,), in_specs=[blk, blk], out_specs=blk,
        out_shape=jax.ShapeDtypeStruct((t, f), BF16), compiler_params=_cparams(("parallel",)), name=name,
    )(a, b)


def _swiglu_bwd(a, b, dm, name):
    t, f = a.shape
    tt = _rows(t, 256)
    blk = pl.BlockSpec((tt, f), lambda i: (i, 0))

    def body(a_ref, b_ref, dm_ref, da_ref, db_ref):
        av = a_ref[...].astype(F32)
        s = _sigmoid(av)
        dv = dm_ref[...].astype(F32)
        da_ref[...] = (dv * b_ref[...].astype(F32) * s * (1.0 + av * (1.0 - s))).astype(BF16)
        db_ref[...] = (dv * av * s).astype(BF16)

    return pl.pallas_call(
        body, grid=(t // tt,), in_specs=[blk, blk, blk], out_specs=[blk, blk],
        out_shape=[jax.ShapeDtypeStruct((t, f), BF16)] * 2, compiler_params=_cparams(("parallel",)), name=name,
    )(a, b, dm)


def _log_sigmoid(x):
    return jnp.minimum(x, 0.0) - jnp.log1p(jnp.exp(-jnp.abs(x)))


def _fox_prep(f_t, b_f, name):
    h, t = f_t.shape

    def body(f_ref, b_ref, c_ref):
        r = lax.broadcasted_iota(jnp.int32, (LANE, LANE), 0)
        c = lax.broadcasted_iota(jnp.int32, (LANE, LANE), 1)
        upper = (r <= c).astype(F32)
        carry = jnp.zeros((h, 1), F32)
        for j in range(t // LANE):
            sl = slice(j * LANE, (j + 1) * LANE)
            lf = _log_sigmoid(f_ref[:, sl] + b_ref[...])
            cs = jnp.dot(lf, upper, precision=lax.Precision.HIGHEST, preferred_element_type=F32) + carry
            c_ref[:, sl] = cs
            carry = cs[:, LANE - 1:LANE]

    return pl.pallas_call(body, out_shape=jax.ShapeDtypeStruct((h, t), F32), compiler_params=_cparams(), name=name)(f_t, b_f)


def _fox_bwd(dc_q, dc_k, f_t, b_f, name):
    h, t = f_t.shape

    def body(dq_ref, dk_ref, f_ref, b_ref, df_ref, db_ref):
        r = lax.broadcasted_iota(jnp.int32, (LANE, LANE), 0)
        c = lax.broadcasted_iota(jnp.int32, (LANE, LANE), 1)
        lower = (r >= c).astype(F32)
        carry = jnp.zeros((h, 1), F32)
        dbsum = jnp.zeros((h, 1), F32)
        for j in reversed(range(t // LANE)):
            sl = slice(j * LANE, (j + 1) * LANE)
            dc = dq_ref[:, sl] - dk_ref[:, sl]
            dl = jnp.dot(dc, lower, precision=lax.Precision.HIGHEST, preferred_element_type=F32) + carry
            carry = dl[:, 0:1]
            df = dl * _sigmoid(-(f_ref[:, sl] + b_ref[...]))
            df_ref[:, sl] = df
            dbsum = dbsum + jnp.sum(df, axis=-1, keepdims=True)
        db_ref[...] = dbsum

    return pl.pallas_call(
        body, out_shape=[jax.ShapeDtypeStruct((h, t), F32), jax.ShapeDtypeStruct((h, 1), F32)],
        compiler_params=_cparams(), name=name,
    )(dc_q, dc_k, f_t, b_f)


ATTN_FWD = (1024, 512)
ATTN_BWD = (512, 512)


def _attn_tiles(t, tiles):
    return _tile(t, tiles[0]), _tile(t, tiles[1])


def _attn_fwd(proj, c_t, d, name, job=None):
    t = proj.shape[0]
    h = d // LANE
    bq, bk = _attn_tiles(t, ATTN_FWD)
    nq, nk, rr = t // bq, t // bk, bq // bk
    qc, kc, vc = COL_Q * h, COL_K * h, COL_VA * h
    qscale = LANE ** -0.5 * LOG2E

    def body(q_ref, k_ref, v_ref, cc_ref, cr_ref, o_ref, lse_ref, kb, vt, ckb, acc):
        i = pl.program_id(1)

        @pl.when(i == 0)
        def _():
            kb[...] = k_ref[...].astype(BF16)
            ckb[...] = jnp.broadcast_to(cc_ref[0] * LOG2E, (t, bq))
            for jn in range(nk):
                vt[jn] = v_ref[jn * bk:(jn + 1) * bk, :].astype(F32).T.astype(BF16)

        q = (q_ref[...].astype(F32) * qscale).astype(BF16)
        cq = cr_ref[0, 0] * LOG2E
        acc[...] = jnp.zeros((LANE, bq), F32)

        def block(j, diag, m_old, l_old):
            off = 0 if diag is None else diag * bk
            w = bq - off
            rows = pl.ds(pl.multiple_of(j * bk, bk), bk)
            s = _nt_dot(kb[rows, :], q[off:, :]) - ckb[rows, off:]
            if diag is not None:
                kk = lax.broadcasted_iota(jnp.int32, (bk, w), 0)
                qq = lax.broadcasted_iota(jnp.int32, (bk, w), 1)
                s = jnp.where(qq >= kk, s, NEG)
            cqs, m_part, l_part = cq[:, off:], m_old[:, off:], l_old[:, off:]
            m_new = jnp.maximum(m_part, jnp.max(s, axis=0, keepdims=True) + cqs)
            p = jnp.exp2(s + (cqs - m_new))
            alpha = jnp.exp2(m_part - m_new)
            l_new = alpha * l_part + jnp.sum(p, axis=0, keepdims=True)
            acc[:, off:] = alpha * acc[:, off:] + jnp.dot(vt[j], p.astype(BF16), preferred_element_type=F32)
            if off:
                m_new = jnp.concatenate([m_old[:, :off], m_new], axis=1)
                l_new = jnp.concatenate([l_old[:, :off], l_new], axis=1)
            return m_new, l_new

        m, l = lax.fori_loop(0, i * rr, lambda j, c: block(j, None, *c),
                             (jnp.full((1, bq), NEG, F32), jnp.zeros((1, bq), F32)))
        for jj in range(rr):
            m, l = block(i * rr + jj, jj, m, l)
        o_ref[...] = (acc[...] / l).T
        lse_ref[0, 0] = m + jnp.log2(l)

    rowq = pl.BlockSpec((1, 1, 1, bq), lambda hh, i: (hh, i, 0, 0))
    outs, moved = _call(
        body, grid=(h, nq),
        in_specs=[
            pl.BlockSpec((bq, LANE), lambda hh, i: (i, qc + hh)),
            pl.BlockSpec((t, LANE), lambda hh, i: (0, kc + hh)),
            pl.BlockSpec((t, LANE), lambda hh, i: (0, vc + hh)),
            pl.BlockSpec((1, t, 1), lambda hh, i: (hh, 0, 0)),
            rowq,
        ],
        out_specs=[pl.BlockSpec((bq, LANE), lambda hh, i: (i, hh)), rowq],
        out_shape=[jax.ShapeDtypeStruct((t, d), F32), jax.ShapeDtypeStruct((h, nq, 1, bq), F32)],
        scratch_shapes=[pltpu.VMEM((t, LANE), BF16), pltpu.VMEM((nk, LANE, bk), BF16), pltpu.VMEM((t, bq), F32),
                        pltpu.VMEM((LANE, bq), F32)],
        dims=("arbitrary", "arbitrary"), name=name,
        args=[proj, proj, proj, c_t.reshape(h, t, 1), c_t.reshape(h, nq, 1, bq)], job=job)
    outs = [outs[0], outs[1].reshape(h, t)]
    return outs if job is None else (outs, moved)


def _attn_bwd(proj, do, o, lse, c_t, d, name, job=None):
    t = proj.shape[0]
    h = d // LANE
    bq, bk = _attn_tiles(t, ATTN_BWD)
    nq, nk, rr = t // bq, t // bk, bq // bk
    qc, kc, vc = COL_Q * h, COL_K * h, COL_VA * h
    scale = LANE ** -0.5

    def body(q_ref, k_ref, v_ref, do_ref, o_ref, lse_ref, cc_ref, cr_ref, dq_ref, dk_ref, dv_ref, dcq_ref, dck_ref,
             kb, kt, vb, ckb, dk_acc, dv_acc, dck_acc, dqt_acc):
        i = pl.program_id(1)

        @pl.when(i == 0)
        def _():
            kb[...] = k_ref[...].astype(BF16)
            vb[...] = v_ref[...].astype(BF16)
            ckb[...] = jnp.broadcast_to(cc_ref[0] * LOG2E, (t, bq))
            for jn in range(nk):
                kt[jn] = k_ref[jn * bk:(jn + 1) * bk, :].astype(F32).T.astype(BF16)
            dk_acc[...] = jnp.zeros((t, LANE), F32)
            dv_acc[...] = jnp.zeros((t, LANE), F32)
            dck_acc[...] = jnp.zeros((t, LANE), F32)

        q = (q_ref[...].astype(F32) * (scale * LOG2E)).astype(BF16)
        dof = do_ref[...]
        dob = dof.astype(BF16)
        delta = jnp.sum((dof * o_ref[...]).T, axis=0, keepdims=True)
        rowb = cr_ref[0, 0] * LOG2E - lse_ref[0, 0]
        dqt_acc[...] = jnp.zeros((LANE, bq), F32)

        def block(j, diag, dcq):
            rows = pl.ds(pl.multiple_of(j * bk, bk), bk)
            p = jnp.exp2(_nt_dot(kb[rows, :], q) - ckb[rows, :] + rowb)
            if diag is not None:
                kk = lax.broadcasted_iota(jnp.int32, (bk, bq), 0)
                qq = lax.broadcasted_iota(jnp.int32, (bk, bq), 1)
                p = jnp.where(qq >= kk + diag * bk, p, 0.0)
            dv_acc[rows, :] += jnp.dot(p.astype(BF16), dob, preferred_element_type=F32)
            ds = p * (_nt_dot(vb[rows, :], dob) - delta)
            dsb = ds.astype(BF16)
            dk_acc[rows, :] += jnp.dot(dsb, q, preferred_element_type=F32)
            dqt_acc[...] += jnp.dot(kt[j], dsb, preferred_element_type=F32)
            part = ds[:, 0:LANE]
            for k in range(1, bq // LANE):
                part = part + ds[:, k * LANE:(k + 1) * LANE]
            dck_acc[rows, :] += part
            return dcq + jnp.sum(ds, axis=0, keepdims=True)

        dcq = lax.fori_loop(0, i * rr, lambda j, c: block(j, None, c), jnp.zeros((1, bq), F32))
        for jj in range(rr):
            dcq = block(i * rr + jj, jj, dcq)
        dq_ref[...] = (dqt_acc[...] * scale).T.astype(BF16)
        dcq_ref[0, 0] = dcq

        @pl.when(i == nq - 1)
        def _():
            dk_ref[...] = (dk_acc[...] * LN2).astype(BF16)
            dv_ref[...] = dv_acc[...].astype(BF16)
            dck_ref[0] = jnp.sum(dck_acc[...], axis=-1, keepdims=True)

    rowq = pl.BlockSpec((1, 1, 1, bq), lambda hh, i: (hh, i, 0, 0))
    blk = pl.BlockSpec((bq, LANE), lambda hh, i: (i, hh))
    whole = pl.BlockSpec((t, LANE), lambda hh, i: (0, hh))
    colk = pl.BlockSpec((1, t, 1), lambda hh, i: (hh, 0, 0))
    outs, moved = _call(
        body, grid=(h, nq),
        in_specs=[
            pl.BlockSpec((bq, LANE), lambda hh, i: (i, qc + hh)),
            pl.BlockSpec((t, LANE), lambda hh, i: (0, kc + hh)),
            pl.BlockSpec((t, LANE), lambda hh, i: (0, vc + hh)),
            blk, blk, rowq, colk, rowq,
        ],
        out_specs=[blk, whole, whole, rowq, colk],
        out_shape=[jax.ShapeDtypeStruct((t, d), BF16), jax.ShapeDtypeStruct((t, d), BF16), jax.ShapeDtypeStruct((t, d), BF16),
                   jax.ShapeDtypeStruct((h, nq, 1, bq), F32), jax.ShapeDtypeStruct((h, t, 1), F32)],
        scratch_shapes=[pltpu.VMEM((t, LANE), BF16), pltpu.VMEM((nk, LANE, bk), BF16), pltpu.VMEM((t, LANE), BF16),
                        pltpu.VMEM((t, bq), F32), pltpu.VMEM((t, LANE), F32), pltpu.VMEM((t, LANE), F32),
                        pltpu.VMEM((t, LANE), F32), pltpu.VMEM((LANE, bq), F32)],
        dims=("arbitrary", "arbitrary"), name=name,
        args=[proj, proj, proj, do, o, lse.reshape(h, nq, 1, bq), c_t.reshape(h, t, 1), c_t.reshape(h, nq, 1, bq)], job=job)
    outs = list(outs[:3]) + [outs[3].reshape(h, t), outs[4].reshape(h, t)]
    return outs if job is None else (outs, moved)


def _sgu_forward(u_ref, v_ref, gv_ref, wm_ref, bs_ref, mix_sc, groups):
    gu, dgu = _gelu_and_grad(u_ref[...].astype(F32))
    gvv, dgv = _gelu_and_grad(v_ref[...].astype(F32))
    mu = jnp.mean(gvv, axis=-1, keepdims=True)
    xc = gvv - mu
    r = lax.rsqrt(jnp.mean(xc * xc, axis=-1, keepdims=True) + EPS)
    nhat = xc * r
    vn = (nhat * gv_ref[...]).astype(BF16)
    for g in range(groups):
        sl = slice(g * LANE, (g + 1) * LANE)
        mix_sc[:, sl] = jnp.dot(wm_ref[g], vn[:, sl], preferred_element_type=F32) + bs_ref[g]
    return gu, dgu, dgv, nhat, r, vn, mix_sc[...]


def _mix_fwd(proj, o, wm, bs, g_v, d, name):
    t = proj.shape[0]
    groups = d // LANE

    def body(u_ref, v_ref, ga_ref, gb_ref, o_ref, wm_ref, bs_ref, gv_ref, out_ref, mix_sc):
        gu, _, _, _, _, _, mixed = _sgu_forward(u_ref, v_ref, gv_ref, wm_ref, bs_ref, mix_sc, groups)
        out_ref[...] = (_sigmoid(ga_ref[...].astype(F32)) * (gu * mixed) + _sigmoid(gb_ref[...].astype(F32)) * o_ref[...]).astype(BF16)

    def colblk(k):
        return pl.BlockSpec((LANE, d), lambda i, k=k: (i, k))

    full3 = pl.BlockSpec((groups, LANE, LANE), lambda i: (0, 0, 0))
    return pl.pallas_call(
        body, grid=(t // LANE,),
        in_specs=[colblk(COL_U), colblk(COL_V), colblk(COL_GA), colblk(COL_GB), colblk(0), full3,
                  pl.BlockSpec((groups, LANE, 1), lambda i: (0, 0, 0)), pl.BlockSpec((1, d), lambda i: (0, 0))],
        out_specs=colblk(0),
        out_shape=jax.ShapeDtypeStruct((t, d), BF16),
        scratch_shapes=[pltpu.VMEM((LANE, d), F32)],
        compiler_params=_cparams(("parallel",)), name=name,
    )(proj, proj, proj, proj, o, wm, bs, g_v)


def _mix_bwd(dmerged, proj, o, wm, wm_t, bs, g_v, d, name, job=None):
    t = proj.shape[0]
    groups = d // LANE
    nt = t // LANE

    def body(dm_ref, u_ref, v_ref, ga_ref, gb_ref, o_ref, wm_ref, wmt_ref, bs_ref, gv_ref,
             duv_ref, dg_ref, do_ref, dws_ref, dbs_ref, dgv_ref, mix_sc, dvn_sc, gv_acc):
        i = pl.program_id(0)

        @pl.when(i == 0)
        def _():
            dws_ref[...] = jnp.zeros_like(dws_ref)
            dbs_ref[...] = jnp.zeros_like(dbs_ref)
            gv_acc[...] = jnp.zeros_like(gv_acc)

        gu, dgu, dgv, nhat, r, vn, mixed = _sgu_forward(u_ref, v_ref, gv_ref, wm_ref, bs_ref, mix_sc, groups)
        dm = dm_ref[...]
        sa = _sigmoid(ga_ref[...].astype(F32))
        sb = _sigmoid(gb_ref[...].astype(F32))
        ov = o_ref[...]
        y_a = gu * mixed
        dg_ref[:, 0:d] = (dm * y_a * sa * (1.0 - sa)).astype(BF16)
        dg_ref[:, d:2 * d] = (dm * ov * sb * (1.0 - sb)).astype(BF16)
        do_ref[...] = dm * sb
        dy_a = dm * sa
        duv_ref[:, 0:d] = (dy_a * mixed * dgu).astype(BF16)
        dmixed = dy_a * gu
        dmixed_b = dmixed.astype(BF16)
        for g in range(groups):
            sl = slice(g * LANE, (g + 1) * LANE)
            dvn_sc[:, sl] = jnp.dot(wmt_ref[g], dmixed_b[:, sl], preferred_element_type=F32)
            dws_ref[g] += _nt_dot(dmixed_b[:, sl], vn[:, sl])
            dbs_ref[g] += jnp.sum(dmixed[:, sl], axis=-1, keepdims=True)
        dvn = dvn_sc[...]
        gv_acc[...] += _sum8(dvn * nhat)
        dn = dvn * gv_ref[...]
        dgelu = r * (dn - jnp.mean(dn, axis=-1, keepdims=True) - nhat * jnp.mean(dn * nhat, axis=-1, keepdims=True))
        duv_ref[:, d:2 * d] = (dgelu * dgv).astype(BF16)

        @pl.when(i == nt - 1)
        def _():
            dgv_ref[...] = jnp.sum(gv_acc[...], axis=0, keepdims=True)
            rr = lax.broadcasted_iota(jnp.int32, (LANE, LANE), 0)
            cl = lax.broadcasted_iota(jnp.int32, (LANE, LANE), 1)
            for g in range(groups):
                dws_ref[g] = jnp.where(rr >= cl, dws_ref[g], 0.0)

    def colblk(k):
        return pl.BlockSpec((LANE, d), lambda i, k=k: (i, k))

    full3 = pl.BlockSpec((groups, LANE, LANE), lambda i: (0, 0, 0))
    col3 = pl.BlockSpec((groups, LANE, 1), lambda i: (0, 0, 0))
    vec = pl.BlockSpec((1, d), lambda i: (0, 0))
    two = pl.BlockSpec((LANE, 2 * d), lambda i: (i, 0))
    outs, moved = _call(
        body, grid=(nt,),
        in_specs=[colblk(0), colblk(COL_U), colblk(COL_V), colblk(COL_GA), colblk(COL_GB), colblk(0), full3, full3, col3, vec],
        out_specs=[two, two, colblk(0), full3, col3, vec],
        out_shape=[jax.ShapeDtypeStruct((t, 2 * d), BF16), jax.ShapeDtypeStruct((t, 2 * d), BF16), jax.ShapeDtypeStruct((t, d), F32),
                   jax.ShapeDtypeStruct((groups, LANE, LANE), F32), jax.ShapeDtypeStruct((groups, LANE, 1), F32),
                   jax.ShapeDtypeStruct((1, d), F32)],
        scratch_shapes=[pltpu.VMEM((LANE, d), F32), pltpu.VMEM((LANE, d), F32), pltpu.VMEM((SUBLANE, d), F32)],
        dims=("arbitrary",), name=name, args=[dmerged, proj, proj, proj, proj, o, wm, wm_t, bs, g_v], job=job)
    return outs if job is None else (outs, moved)


def _adam_math(w, g, m, v):
    nm = ADAM_B1 * m + (1.0 - ADAM_B1) * g
    nv = ADAM_B2 * v + (1.0 - ADAM_B2) * (g * g)
    delta = -ADAM_LR * ((nm * ADAM_C1) / (jnp.sqrt(nv * ADAM_C2) + ADAM_EPS) + ADAM_WD * w)
    return delta, nm, nv


def _adamw(w, g, m, v, name):
    r, c = w.shape
    cap = max(SUBLANE, (2 * 1024 * 1024) // (4 * c) // SUBLANE * SUBLANE)
    tr = _rows(r, cap)

    def body(w_ref, g_ref, m_ref, v_ref, d_ref, nm_ref, nv_ref):
        d_ref[...], nm_ref[...], nv_ref[...] = _adam_math(w_ref[...], g_ref[...], m_ref[...], v_ref[...])

    blk = pl.BlockSpec((tr, c), lambda i: (i, 0))
    return pl.pallas_call(
        body, grid=(r // tr,), in_specs=[blk] * 4, out_specs=[blk] * 3,
        out_shape=[jax.ShapeDtypeStruct((r, c), F32)] * 3, compiler_params=_cparams(("parallel",)), name=name,
    )(w, g, m, v)


def _adamw_layers(w, g0, g1, m, v, name):
    _, r, c = w.shape
    cap = max(SUBLANE, (1024 * 1024) // (4 * c) // SUBLANE * SUBLANE)
    tr = _rows(r, cap)

    def body(w_ref, g0_ref, g1_ref, m_ref, v_ref, g_ref, d_ref, nm_ref, nv_ref):
        gg = jnp.where(pl.program_id(0) == 0, g0_ref[...], g1_ref[...])
        g_ref[0] = gg
        d_ref[0], nm_ref[0], nv_ref[0] = _adam_math(w_ref[0], gg, m_ref[0], v_ref[0])

    lay = pl.BlockSpec((1, tr, c), lambda l, i: (l, i, 0))

    def gspec(l0):
        return pl.BlockSpec((tr, c), lambda l, i: (jnp.where(l == l0, i, 0), 0))

    return pl.pallas_call(
        body, grid=(2, r // tr), in_specs=[lay, gspec(0), gspec(1), lay, lay], out_specs=[lay] * 4,
        out_shape=[jax.ShapeDtypeStruct((2, r, c), F32)] * 4, compiler_params=_cparams(("arbitrary", "arbitrary")), name=name,
    )(w, g0, g1, m, v)


def _adamw_interleaved(w, g0, g1, m, v, name):
    r, _, c = w.shape
    tr = 128

    def body(w_ref, g0_ref, g1_ref, m_ref, v_ref, g_ref, d_ref, nm_ref, nv_ref):
        for l, gl in enumerate((g0_ref, g1_ref)):
            gg = gl[...]
            g_ref[:, l, :] = gg
            d_ref[:, l, :], nm_ref[:, l, :], nv_ref[:, l, :] = _adam_math(w_ref[:, l, :], gg, m_ref[:, l, :], v_ref[:, l, :])

    lay = pl.BlockSpec((tr, 2, c), lambda i: (i, 0, 0))
    flat = pl.BlockSpec((tr, c), lambda i: (i, 0))
    return pl.pallas_call(
        body, grid=(pl.cdiv(r, tr),), in_specs=[lay, flat, flat, lay, lay], out_specs=[lay] * 4,
        out_shape=[jax.ShapeDtypeStruct((r, 2, c), F32)] * 4, compiler_params=_cparams(("parallel",)), name=name,
    )(w, g0, g1, m, v)


def _add_half(p4, recv, c_idx, name):
    _, r, c = p4.shape
    hw = c // 2
    tr = 256 if r % 256 == 0 else r

    def body(c_ref, a_ref, b_ref, o_ref):
        o_ref[...] = (a_ref[...].astype(F32) + b_ref[...].astype(F32)).astype(BF16)

    return pl.pallas_call(
        body,
        grid_spec=pltpu.PrefetchScalarGridSpec(
            num_scalar_prefetch=1, grid=(N_CHIPS, pl.cdiv(r, tr)),
            in_specs=[pl.BlockSpec((1, tr, hw), lambda s, i, cr: (s, i, cr[0])), pl.BlockSpec((1, tr, hw), lambda s, i, cr: (s, i, 0))],
            out_specs=pl.BlockSpec((1, tr, hw), lambda s, i, cr: (s, i, 0)),
        ),
        out_shape=jax.ShapeDtypeStruct((N_CHIPS, r, hw), BF16), compiler_params=_cparams(("parallel", "parallel")), name=name,
    )(c_idx, p4, recv)


def _sum_slots(x, own, sel, name, out_cols=None):
    s, r, c = x.shape
    tr = 128 if r % 128 == 0 else r

    def body(sel_ref, x_ref, own_ref, o_ref):
        mine = own_ref[0].astype(F32)
        acc = jnp.zeros((tr, c), F32)
        for k in range(s):
            acc = acc + jnp.where(sel_ref[0] == k, mine, x_ref[k].astype(F32))
        o_ref[...] = acc

    return pl.pallas_call(
        body,
        grid_spec=pltpu.PrefetchScalarGridSpec(
            num_scalar_prefetch=1, grid=(pl.cdiv(r, tr),),
            in_specs=[pl.BlockSpec((s, tr, c), lambda i, sr: (0, i, 0)), pl.BlockSpec((1, tr, c), lambda i, sr: (sr[1], i, 0))],
            out_specs=pl.BlockSpec((tr, c), lambda i, sr: (i, sr[2])),
        ),
        out_shape=jax.ShapeDtypeStruct((r, out_cols or c), F32), compiler_params=_cparams(("parallel",)), name=name,
    )(sel, x, own)


def _half_cols(width, hc):
    hw = width // 2
    assert hw % LANE == 0
    return pl.ds(pl.multiple_of(hc * hw, LANE), hw)


def _remote(src, dst, ssem, rsem, k, to):
    return pltpu.make_async_remote_copy(src_ref=src, dst_ref=dst, send_sem=ssem.at[k], recv_sem=rsem.at[k], device_id=to,
                                        device_id_type=MESH)


def _gather_job(bufs, mid_at=0.5):
    def part(o, a, slot, hc):
        return o[a].at[slot, :, _half_cols(bufs[a].shape[2], hc)]

    def first(ins, o, fresh, ssem, rsem):
        x, y, c, chips = _place()
        for a in range(len(bufs)):
            mine = part(o, a, 2 * x + y, c)
            for j, chip in enumerate(chips):
                _remote(mine, mine, ssem, rsem, 6 * a + j, (chip[0], chip[1], c)).start()

    def mid(ins, o, fresh, ssem, rsem):
        x, y, c, chips = _place()
        for a in range(len(bufs)):
            for j, chip in enumerate(chips):
                got = part(o, a, 2 * chip[0] + chip[1], c)
                _remote(got, got, ssem, rsem, 6 * a + j, (x, y, c)).wait_recv()
                _remote(got, got, ssem, rsem, 6 * a + 3 + j, (x, y, 1 - c)).start()

    def last(ins, o, fresh, ssem, rsem):
        x, y, c, chips = _place()
        for a in range(len(bufs)):
            for j, chip in enumerate(chips):
                got = part(o, a, 2 * chip[0] + chip[1], 1 - c)
                _remote(got, got, ssem, rsem, 6 * a + 3 + j, (x, y, c)).wait_recv()
        for a in range(len(bufs)):
            mine = part(o, a, 2 * x + y, c)
            for j, chip in enumerate(chips):
                _remote(mine, mine, ssem, rsem, 6 * a + j, (x, y, c)).wait_send()
                passed = part(o, a, 2 * chip[0] + chip[1], c)
                _remote(passed, passed, ssem, rsem, 6 * a + 3 + j, (x, y, c)).wait_send()

    return _Job([], bufs, [], 6 * len(bufs), first, mid, last, mid_at)


def _swap_job(p4s):
    def pairs(ins, fresh, c):
        return [(a, s, ins[a].at[s, :, _half_cols(p4s[a].shape[2], 1 - c)], fresh[a].at[s])
                for a in range(len(p4s)) for s in range(N_CHIPS)]

    def first(ins, inout, fresh, ssem, rsem):
        x, y, c, _ = _place()
        for a, s, src, dst in pairs(ins, fresh, c):
            _remote(src, dst, ssem, rsem, N_CHIPS * a + s, (x, y, 1 - c)).start()

    def last(ins, inout, fresh, ssem, rsem):
        x, y, c, _ = _place()
        for a, s, src, dst in pairs(ins, fresh, c):
            _remote(src, dst, ssem, rsem, N_CHIPS * a + s, (x, y, 1 - c)).wait()

    fresh = [jax.ShapeDtypeStruct(p.shape[:2] + (p.shape[2] // 2,), p.dtype) for p in p4s]
    return _Job(p4s, [], fresh, N_CHIPS * len(p4s), first, None, last)


def _scatter_job(parts):
    def first(ins, inout, fresh, ssem, rsem):
        x, y, c, chips = _place()
        for a in range(len(parts)):
            for j, chip in enumerate(chips):
                _remote(ins[a].at[2 * chip[0] + chip[1]], fresh[a].at[2 * x + y], ssem, rsem, 3 * a + j, (chip[0], chip[1], c)).start()

    def last(ins, inout, fresh, ssem, rsem):
        x, y, c, chips = _place()
        for a in range(len(parts)):
            for j, chip in enumerate(chips):
                slot = 2 * chip[0] + chip[1]
                _remote(ins[a].at[slot], fresh[a].at[slot], ssem, rsem, 3 * a + j, (x, y, c)).wait()

    return _Job(parts, [], [jax.ShapeDtypeStruct(p.shape, p.dtype) for p in parts], 3 * len(parts), first, None, last)


def _share_job(gs):
    def halves(o, a, c):
        width = gs[a].shape[1]
        return o[a].at[:, _half_cols(width, c)], o[a].at[:, _half_cols(width, 1 - c)]

    def first(ins, o, fresh, ssem, rsem):
        x, y, c, _ = _place()
        for a in range(len(gs)):
            mine, _ = halves(o, a, c)
            _remote(mine, mine, ssem, rsem, a, (x, y, 1 - c)).start()

    def last(ins, o, fresh, ssem, rsem):
        x, y, c, _ = _place()
        for a in range(len(gs)):
            mine, theirs = halves(o, a, c)
            _remote(mine, theirs, ssem, rsem, a, (x, y, 1 - c)).wait()

    return _Job([], gs, [], len(gs), first, None, last)


def _gather_all_job(buf):
    def peers():
        x, y, c, _ = _place()
        flips = [(fx, fy, fc) for fx in (0, 1) for fy in (0, 1) for fc in (0, 1)][1:]
        return (x, y, c), [((1 - x) if fx else x, (1 - y) if fy else y, (1 - c) if fc else c) for fx, fy, fc in flips]

    def first(ins, inout, fresh, ssem, rsem):
        (x, y, c), others = peers()
        for k, peer in enumerate(others):
            _remote(ins[0], fresh[0].at[4 * x + 2 * y + c], ssem, rsem, k, peer).start()

    def last(ins, inout, fresh, ssem, rsem):
        me, others = peers()
        for k, peer in enumerate(others):
            _remote(ins[0], fresh[0].at[4 * peer[0] + 2 * peer[1] + peer[2]], ssem, rsem, k, me).wait()

    return _Job([buf], [], [jax.ShapeDtypeStruct((N_DEV,) + buf.shape, buf.dtype)], N_DEV - 1, first, None, last)


class _SemView:
    def __init__(self, sems, off):
        self.sems, self.off = sems, off

    @property
    def at(self):
        return self

    def __getitem__(self, k):
        return self.sems.at[k + self.off]


def _join(jobs):
    spans, pos = [], [0, 0, 0, 0]
    for j in jobs:
        nxt = [pos[0] + len(j.ins), pos[1] + len(j.inout), pos[2] + len(j.fresh), pos[3] + j.nsem]
        spans.append((pos, nxt))
        pos = nxt

    def hook(which):
        fns = [getattr(j, which) for j in jobs]
        if all(f is None for f in fns):
            return None

        def run(ins, inout, fresh, ssem, rsem):
            for fn, (lo, hi) in zip(fns, spans):
                if fn is not None:
                    fn(ins[lo[0]:hi[0]], inout[lo[1]:hi[1]], fresh[lo[2]:hi[2]], _SemView(ssem, lo[3]), _SemView(rsem, lo[3]))

        return run

    mids = [j.mid_at for j in jobs if j.mid is not None]
    joined = _Job([a for j in jobs for a in j.ins], [a for j in jobs for a in j.inout], [a for j in jobs for a in j.fresh],
                  pos[3], hook("first"), hook("mid"), hook("last"), max(mids) if mids else 0.5)
    n_io = pos[1]

    def split(moved):
        return [list(moved[lo[1]:hi[1]]) + list(moved[n_io + lo[2]:n_io + hi[2]]) for lo, hi in spans]

    return joined, split


def _carrying(stages, call):
    stages = [s for s in stages if s is not None]
    if not stages:
        return call(None)
    job, split = _join([s[0] for s in stages])
    out, moved = call(job)
    for (_, done), part in zip(stages, split(moved)):
        done(part)
    return out


def _layer_forward(x, h, w_in_t, rest, sm, g_next, d, stages=None):
    stages = stages or {}
    proj = _carrying([stages.get("proj")], lambda job: _matmul(h, w_in_t, "nt", BF16, "proj_fwd", n=7 * d, tn_cap=1792, job=job))
    f_t = _matmul(w_in_t[7 * d:], h, "nt", F32, "forget_fwd", tn_cap=1024)
    c_t = _fox_prep(f_t, sm["b_f"], "fox_prep")
    o, lse = _carrying([stages.get("attn")], lambda job: _attn_fwd(proj, c_t, d, "attn_fwd", job=job))
    wts = rest()
    merged = _mix_fwd(proj, o, sm["wm"], sm["bs"], sm["g_v"], d, "mix_fwd")
    z, x1, h2 = _matmul(merged, wts["w_out"], "nn", F32, "out_fwd", norms=(x, sm["g_post"], sm["g_fpre"]))
    a = _carrying([stages.get("gate")], lambda job: _matmul(h2, wts["w_g_t"], "nt", BF16, "gate_fwd", tn_cap=1408, job=job))
    b = _carrying([stages.get("up")], lambda job: _matmul(h2, wts["w_u_t"], "nt", BF16, "up_fwd", tn_cap=1408, job=job))
    mm = _swiglu_fwd(a, b, "swiglu_fwd")
    z2, x_out, h_out = _carrying([stages.get("down")], lambda job: _matmul(
        mm, wts["w_d"], "nn", F32, "down_fwd", norms=(x1, sm["g_fpost"], g_next), job=job))
    return dict(x=x, h=h, proj=proj, f_t=f_t, c_t=c_t, o=o, lse=lse, merged=merged, z=z, x1=x1,
                h2=h2, a=a, b=b, mm=mm, z2=z2, x_out=x_out, h_out=h_out)


class _GradExchange:
    def __init__(self, pay, keys, c_idx, chip):
        self.keys = list(keys)
        self.p4 = [pay[k].reshape(N_CHIPS, pay[k].shape[1] // N_CHIPS, pay[k].shape[2]) for k in self.keys]
        self.c_idx = c_idx
        self.sel = jnp.stack([chip, chip, c_idx[0]]).astype(jnp.int32)
        self.done = 0

    def _after_swap(self, landed):
        self.parts = [_add_half(p, r, self.c_idx, "add_sibling") for p, r in zip(self.p4, landed)]
        self.done = 1

    def _after_scatter(self, landed):
        self.g = [_sum_slots(got, sent, self.sel, "sum_chips", out_cols=p.shape[2])
                  for got, sent, p in zip(landed, self.parts, self.p4)]
        self.done = 2

    def _after_share(self, moved):
        self.g = list(moved)
        self.done = 3

    def stage(self):
        if self.done == 0:
            return _swap_job(self.p4), self._after_swap
        if self.done == 1:
            return _scatter_job(self.parts), self._after_scatter
        if self.done == 2:
            return _share_job(self.g), self._after_share
        return None

    def run(self):
        for name in ("swap_grads", "scatter_grads", "share_grads")[self.done:]:
            job, done = self.stage()
            done(_run_job(job, name))

    def grads(self):
        return dict(zip(self.keys, self.g))


EARLY_KEYS = ("w_d", "w_g", "w_u", "w_out")


def _layer_backward(dz2, dx2, sv, wts, sm, d, c_idx, chip, carried=(), split_own=False, small_stage=None):
    t = dx2.shape[0]
    heads = d // LANE
    ff = wts["w_d"].shape[0]
    in_w = 7 * d + heads
    g, pay = {}, {}
    carried = list(carried)

    def payload(key, a, b, rows, row0, name, extra=()):
        def call(job):
            return _matmul(a, b, "tn", BF16, name, slab=((1, rows, d), 0, row0), into=pay.get(key), job=job, tm_cap=1408,
                           tn_cap=1024, tk_cap=1024)
        pay[key] = _carrying(list(extra), call)

    def nxt(*exchanges):
        return [ex.stage() for ex in exchanges]

    dm = _carrying(nxt(*carried), lambda job: _matmul(dz2, wts["w_d"], "nt", BF16, "down_bwd_x", tn_cap=1408, tk_cap=1024, job=job))
    payload("w_d", sv["mm"], dz2, ff, 0, "down_bwd_w")
    da, db = _swiglu_bwd(sv["a"], sv["b"], dm, "swiglu_bwd")
    dh2 = _matmul_pieces([(da, wts["w_g_t"], 0), (db, wts["w_u_t"], 0)], None, "gu_bwd_x", tk=_tile(ff, 1408))
    payload("w_g", da, sv["h2"], ff, 0, "gate_bwd_w")
    payload("w_u", db, sv["h2"], ff, 0, "up_bwd_w")
    dx1, dz, g["g_fpre"], g["g_post"] = _norm_bwd(dx2, (dh2, sv["x1"], sm["g_fpre"]), (sv["z"], sm["g_post"]), "norm_bwd_mid")
    dmerged = _matmul(dz, wts["w_out"], "nt", F32, "out_bwd_x", tk_cap=1024)
    payload("w_out", sv["merged"], dz, d, 0, "out_bwd_w")
    early = [_GradExchange(pay, EARLY_KEYS, c_idx, chip)] if split_own else []
    d_uv, d_g, do, g["w_s"], g["b_s"], g["g_v"] = _carrying(nxt(*early), lambda job: _mix_bwd(
        dmerged, sv["proj"], sv["o"], sm["wm"], sm["wm_t"], sm["bs"], sm["g_v"], d, "mix_bwd", job=job))
    extra = [small_stage(g)] if small_stage is not None else []
    attn_args = (sv["proj"], do, sv["o"], sv["lse"], sv["c_t"], d)
    dq, dk, dv, dc_q, dc_k = _carrying(nxt(*carried, *early) + extra, lambda job: _attn_bwd(*attn_args, "attn_bwd", job=job))
    df_t, g["b_f"] = _fox_bwd(dc_q, dc_k, sv["f_t"], sm["b_f"], "fox_bwd")
    df_b = df_t.astype(BF16)
    pieces = [(d_uv, COL_U), (dq, COL_Q), (dk, COL_K), (dv, COL_VA), (d_g, COL_GA)]
    for k, (p, col) in enumerate(pieces):
        payload("w_in", p, sv["h"], in_w, col * d, "proj_bwd_w", extra=nxt(*carried, *early) if k == 0 else ())
    w_f_rows = _matmul(df_b, sv["h"], "nn", BF16, "forget_bwd_w", tk_cap=1024)
    pay["w_in"] = lax.dynamic_update_slice(pay["w_in"], w_f_rows[None], (0, 7 * d, 0))
    late = _GradExchange(pay, [k for k in ("w_in",) + EARLY_KEYS if not (split_own and k in EARLY_KEYS)], c_idx, chip)
    mine = [late] if split_own else []
    dh_f = _carrying(nxt(*mine), lambda job: _matmul(df_b, wts["w_in_t"][7 * d:], "tn", F32, "forget_bwd_x", job=job))
    ops = [(p, wts["w_in_t"], col * d) for p, col in pieces]
    dh = _carrying(nxt(*mine), lambda job: _matmul_pieces(ops, dh_f, "proj_bwd_x", job=job, tk=_tile(d, 1024)))
    return dh, dx1, g, early + [late]


def _small_pack(parts):
    flat = jnp.concatenate([p.reshape(-1) for p in parts])
    n = flat.shape[0]
    pad = (-n) % (LANE * LANE)
    return jnp.pad(flat, (0, pad)).reshape(-1, LANE)


def kernel(x, mix_pre_g, w_in, b_forget, sgu_norm_g, w_spatial, b_spatial, w_out, mix_post_g, ffn_pre_g, w_gate, w_up, w_down, ffn_post_g, loss_target, m_mix_pre_g, m_w_in, m_b_forget, m_sgu_norm_g, m_w_spatial, m_b_spatial, m_w_out, m_mix_post_g, m_ffn_pre_g, m_w_gate, m_w_up, m_w_down, m_ffn_post_g, v_mix_pre_g, v_w_in, v_b_forget, v_sgu_norm_g, v_w_spatial, v_b_spatial, v_w_out, v_mix_post_g, v_ffn_pre_g, v_w_gate, v_w_up, v_w_down, v_ffn_post_g):
    depth, d = mix_pre_g.shape
    assert depth == 2, "core c of a chip owns layer c"
    heads = d // LANE
    t = x.shape[1]
    ff = w_down.shape[1] * N_CHIPS
    in_w = w_in.shape[2] * N_CHIPS
    assert in_w == 7 * d + heads
    xs = x.reshape(t, d)
    target = loss_target.reshape(t, d)
    c_idx = lax.axis_index("c").astype(jnp.int32).reshape(1)
    chip = 2 * lax.axis_index("x") + lax.axis_index("y")
    dev = 2 * chip + lax.axis_index("c")

    def in_view(w):
        return jnp.transpose(w, (2, 0, 1))

    def gu_view(w):
        return jnp.transpose(w, (0, 2, 1))

    own = [jnp.transpose(in_view(w_in).astype(BF16), (1, 0, 2)), w_out.astype(BF16), gu_view(w_gate).astype(BF16),
           gu_view(w_up).astype(BF16), w_down.astype(BF16)]
    bufs = [[lax.dynamic_update_slice(lax.empty((N_CHIPS,) + o.shape[1:], BF16), o[l][None], (chip, 0, 0)) for o in own]
            for l in range(depth)]
    first_in = _run_job(_gather_job([bufs[0][0]]), "gather_first")[0]

    def weights(g_in, g_out, g_g, g_u, g_d):
        return dict(w_in_t=g_in.reshape(in_w, d), w_out=g_out.reshape(d, d), w_g_t=g_g.reshape(ff, d),
                    w_u_t=g_u.reshape(ff, d), w_d=g_d.reshape(ff, d))

    tril = jnp.tril(jnp.ones((LANE, LANE), bool))
    smalls = []
    for l in range(depth):
        wm = jnp.where(tril[None], w_spatial[l], 0.0).astype(BF16)
        smalls.append(dict(
            b_f=b_forget[l].reshape(heads, 1), wm=wm, wm_t=jnp.swapaxes(wm, 1, 2), bs=b_spatial[l].reshape(heads, LANE, 1),
            g_v=sgu_norm_g[l].reshape(1, d), g_pre=mix_pre_g[l].reshape(1, d), g_post=mix_post_g[l].reshape(1, d),
            g_fpre=ffn_pre_g[l].reshape(1, d), g_fpost=ffn_post_g[l].reshape(1, d)))

    wts, later = [], {}

    def keep(key):
        def done(moved):
            later[key] = list(moved)
        return done

    def rest_first():
        wts.append(weights(first_in, *later["rest0"]))
        return wts[0]

    stages = dict(proj=(_gather_job(bufs[0][1:], mid_at=1.0), keep("rest0")),
                  attn=(_gather_job(bufs[1][0:2], mid_at=0.7), keep("in_out1")),
                  gate=(_gather_job(bufs[1][2:3], mid_at=1.0), keep("g1")), up=(_gather_job(bufs[1][3:4], mid_at=1.0), keep("u1")),
                  down=(_gather_job(bufs[1][4:5], mid_at=1.0), keep("d1")))
    h = _norm_fwd(xs, None, None, smalls[0]["g_pre"], "norm_first")
    g_after = [smalls[min(l + 1, depth - 1)]["g_pre"] for l in range(depth)]
    saved = [_layer_forward(xs, h, first_in.reshape(in_w, d), rest_first, smalls[0], g_after[0], d, stages)]
    wts.append(weights(*later["in_out1"], later["g1"][0], later["u1"][0], later["d1"][0]))
    for l in range(1, depth):
        saved.append(_layer_forward(saved[l - 1]["x_out"], saved[l - 1]["h_out"], wts[l]["w_in_t"], lambda l=l: wts[l], smalls[l],
                                    g_after[l], d))
    dy, loss_part = _loss_grad(saved[-1]["x_out"], target, "loss")
    loss = lax.psum(jnp.sum(loss_part), ("x", "y", "c"))

    small_shapes = dict(g_pre=(d,), b_f=(heads,), g_v=(d,), w_s=w_spatial.shape[1:], b_s=b_spatial.shape[1:], g_post=(d,),
                        g_fpre=(d,), g_fpost=(d,))
    late_entries = [(0, "g_pre"), (0, "b_f")]
    early_entries = [(l, n) for l in reversed(range(depth)) for n in small_shapes if (l, n) not in late_entries]
    dev_sel = jnp.stack([dev, jnp.zeros_like(dev), jnp.zeros_like(dev)]).astype(jnp.int32)
    small_sum = {}

    def small_exchange(entries, values):
        packed = _small_pack([values[e].reshape(-1) for e in entries])

        def done(moved):
            total = _sum_slots(moved[0], packed[None], dev_sel, "sum_small").reshape(-1)
            off = 0
            for e in entries:
                n = math.prod(small_shapes[e[1]])
                small_sum[e] = total[off:off + n].reshape(small_shapes[e[1]])
                off += n

        return _gather_all_job(packed), done

    grads = [None] * depth
    exchanges = [None] * depth
    dx2 = dy
    dz2, g_fpost = _norm_bwd(dx2, None, (saved[depth - 1]["z2"], smalls[depth - 1]["g_fpost"]), "norm_bwd_top")
    for l in reversed(range(depth)):
        last = l == 0

        def small_stage(g, l=l, g_fpost=g_fpost):
            known = {(k, n): grads[k][n] for k in range(l + 1, depth) for n in small_shapes}
            known.update({(l, n): g[n] for n in g})
            known[(l, "g_fpost")] = g_fpost
            return small_exchange(early_entries, known)

        carried = [ex for k in range(l + 1, depth) for ex in exchanges[k]]
        dh, dx1, g, exchanges[l] = _layer_backward(dz2, dx2, saved[l], wts[l], smalls[l], d, c_idx, chip, carried=carried,
                                                    split_own=last, small_stage=small_stage if last else None)
        g["g_fpost"] = g_fpost
        if l > 0:
            dx2, dz2, g["g_pre"], g_fpost = _norm_bwd(dx1, (dh, saved[l]["x"], smalls[l]["g_pre"]),
                                                       (saved[l - 1]["z2"], smalls[l - 1]["g_fpost"]), "norm_bwd_between")
        else:
            grad_x, g["g_pre"] = _norm_bwd(dx1, (dh, saved[l]["x"], smalls[l]["g_pre"]), None, "norm_bwd_bottom")
        grads[l] = g
    job, done = small_exchange(late_entries, {(0, n): grads[0][n] for n in ("g_pre", "b_f")})
    done(_run_job(job, "gather_small"))
    big = [{} for _ in range(depth)]
    for l in range(depth):
        for ex in exchanges[l]:
            ex.run()
            big[l].update(ex.grads())
    small_grads = {n: jnp.stack([small_sum[(l, n)] for l in range(depth)]) for n in small_shapes}

    def adam_small(w, g, m, v):
        shp = w.shape
        if w.ndim >= 3 and shp[-1] >= LANE:
            two = (math.prod(shp[:-1]), shp[-1])
        else:
            two = (1, math.prod(shp)) if math.prod(shp) < LANE else (math.prod(shp) // LANE, LANE)
        outs = _adamw(w.reshape(two), g.reshape(two), m.reshape(two), v.reshape(two), "adamw")
        return [g] + [o.reshape(shp) for o in outs]

    def adam_in(w, m, v):
        outs = _adamw_interleaved(in_view(w), big[0]["w_in"], big[1]["w_in"], in_view(m), in_view(v), "adamw_in")
        return [jnp.transpose(o, (1, 2, 0)) for o in outs]

    def adam_gu(k, w, m, v):
        outs = _adamw_layers(gu_view(w), big[0][k], big[1][k], gu_view(m), gu_view(v), "adamw_layers")
        return [jnp.transpose(o, (0, 2, 1)) for o in outs]

    def adam_rows(k, w, m, v):
        return _adamw_layers(w, big[0][k], big[1][k], m, v, "adamw_layers")

    results = [
        adam_small(mix_pre_g, small_grads["g_pre"], m_mix_pre_g, v_mix_pre_g),
        adam_in(w_in, m_w_in, v_w_in),
        adam_small(b_forget, small_grads["b_f"], m_b_forget, v_b_forget),
        adam_small(sgu_norm_g, small_grads["g_v"], m_sgu_norm_g, v_sgu_norm_g),
        adam_small(w_spatial, small_grads["w_s"], m_w_spatial, v_w_spatial),
        adam_small(b_spatial, small_grads["b_s"], m_b_spatial, v_b_spatial),
        adam_rows("w_out", w_out, m_w_out, v_w_out),
        adam_small(mix_post_g, small_grads["g_post"], m_mix_post_g, v_mix_post_g),
        adam_small(ffn_pre_g, small_grads["g_fpre"], m_ffn_pre_g, v_ffn_pre_g),
        adam_gu("w_g", w_gate, m_w_gate, v_w_gate),
        adam_gu("w_u", w_up, m_w_up, v_w_up),
        adam_rows("w_d", w_down, m_w_down, v_w_down),
        adam_small(ffn_post_g, small_grads["g_fpost"], m_ffn_post_g, v_ffn_post_g),
    ]
    gs, deltas, new_ms, new_vs = zip(*results)
    return (loss, grad_x.reshape(x.shape), *gs, *deltas, *new_ms, *new_vs)
```

```python
import functools
import math

import jax
import jax.numpy as jnp
from jax import lax
from jax.experimental import pallas as pl
from jax.experimental.pallas import tpu as pltpu

F32 = jnp.float32
BF16 = jnp.bfloat16

EPS = 1e-6
LANE = 128
SUBLANE = 8
N_CHIPS = 4
N_DEV = 8
VMEM_LIMIT = 48 * 1024 * 1024
MESH = pl.DeviceIdType.MESH

ADAM_LR = 0.001
ADAM_B1 = 0.9
ADAM_B2 = 0.999
ADAM_EPS = 1e-08
ADAM_WD = 0.01
ADAM_STEP = 10
ADAM_C1 = 1.0 / (1.0 - ADAM_B1 ** ADAM_STEP)
ADAM_C2 = 1.0 / (1.0 - ADAM_B2 ** ADAM_STEP)

GELU_K = math.sqrt(2.0 / math.pi)
GELU_A = 0.044715
NEG = -1e30
LOG2E = 1.4426950408889634
LN2 = 0.6931471805599453

COL_U, COL_V, COL_Q, COL_K, COL_VA, COL_GA, COL_GB, COL_F = range(8)


def _cparams(sem=None):
    return pltpu.CompilerParams(dimension_semantics=sem, vmem_limit_bytes=VMEM_LIMIT)


def _tile(n, cap):
    best = None
    for t in range(LANE, min(n, cap) + 1, LANE):
        if n % t == 0:
            best = t
    return best if best is not None else n


def _rows(n, cap):
    best = None
    for t in range(SUBLANE, min(n, cap) + 1, SUBLANE):
        if n % t == 0:
            best = t
    return best if best is not None else n


def _gelu_and_grad(x):
    x2 = x * x
    t = jnp.tanh(GELU_K * (x + GELU_A * x2 * x))
    g = 0.5 * x * (1.0 + t)
    dg = 0.5 * (1.0 + t) + 0.5 * x * (1.0 - t * t) * (GELU_K * (1.0 + 3.0 * GELU_A * x2))
    return g, dg


def _sigmoid(x):
    return 1.0 / (1.0 + jnp.exp(-x))


def _sum8(v):
    n, d = v.shape
    return v.reshape(n // SUBLANE, SUBLANE, d).sum(axis=0)


def _nt_dot(a, b):
    return lax.dot_general(a, b, (((1,), (1,)), ((), ())), preferred_element_type=F32)


_HBM = pl.BlockSpec(memory_space=pl.ANY)


def _place():
    x, y, c = lax.axis_index("x"), lax.axis_index("y"), lax.axis_index("c")
    chips = [(1 - x, y), (x, 1 - y), (1 - x, 1 - y)]
    return x, y, c, chips


class _Job:
    def __init__(self, ins, inout, fresh, nsem, first, mid, last, mid_at=0.5):
        self.ins, self.inout, self.fresh, self.nsem = list(ins), list(inout), list(fresh), nsem
        self.first, self.mid, self.last, self.mid_at = first, mid, last, mid_at


def _call(body, *, grid, in_specs, out_specs, out_shape, scratch_shapes, dims, name, args, aliases=None, job=None):
    single = not isinstance(out_shape, (list, tuple))
    out_specs = [out_specs] if single else list(out_specs)
    out_shape = [out_shape] if single else list(out_shape)
    aliases = dict(aliases or {})
    if job is None:
        outs = pl.pallas_call(body, grid=grid, in_specs=in_specs, out_specs=out_specs, out_shape=out_shape,
                              scratch_shapes=scratch_shapes, input_output_aliases=aliases, compiler_params=_cparams(dims),
                              name=name)(*args)
        return (outs[0] if single else outs), []
    n_in, n_out, n_scr = len(args), len(out_shape), len(scratch_shapes)
    n_ji, n_jio, n_jf = len(job.ins), len(job.inout), len(job.fresh)
    total = math.prod(grid)

    def wrapped(*refs):
        host_in = refs[:n_in]
        pos = n_in
        j_ins = refs[pos:pos + n_ji]
        pos += n_ji + n_jio
        host_out = refs[pos:pos + n_out]
        pos += n_out
        j_inout = refs[pos:pos + n_jio]
        pos += n_jio
        j_fresh = refs[pos:pos + n_jf]
        pos += n_jf
        host_scr = refs[pos:pos + n_scr]
        ssem, rsem = refs[pos + n_scr:]
        flat = 0
        for ax, size in enumerate(grid):
            flat = flat * size + pl.program_id(ax)

        def hook(fn, at):
            if fn is not None:
                @pl.when(flat == at)
                def _():
                    fn(j_ins, j_inout, j_fresh, ssem, rsem)

        hook(job.first, 0)
        body(*host_in, *host_out, *host_scr)
        hook(job.mid, min(int(total * job.mid_at), total - 1))
        hook(job.last, total - 1)

    for k in range(n_jio):
        aliases[n_in + n_ji + k] = n_out + k
    outs = pl.pallas_call(
        wrapped, grid=grid,
        in_specs=list(in_specs) + [_HBM] * (n_ji + n_jio),
        out_specs=out_specs + [_HBM] * (n_jio + n_jf),
        out_shape=out_shape + [jax.ShapeDtypeStruct(b.shape, b.dtype) for b in job.inout] + list(job.fresh),
        scratch_shapes=list(scratch_shapes) + [pltpu.SemaphoreType.DMA((job.nsem,)), pltpu.SemaphoreType.DMA((job.nsem,))],
        input_output_aliases=aliases, compiler_params=_cparams(tuple("arbitrary" for _ in grid)), name=name,
    )(*args, *job.ins, *job.inout)
    host = outs[:n_out]
    return (host[0] if single else host), outs[n_out:]


def _run_job(job, name):
    n_ji, n_jio, n_jf = len(job.ins), len(job.inout), len(job.fresh)

    def body(*refs):
        j_ins = refs[:n_ji]
        pos = n_ji + n_jio
        j_inout = refs[pos:pos + n_jio]
        j_fresh = refs[pos + n_jio:pos + n_jio + n_jf]
        ssem, rsem = refs[pos + n_jio + n_jf:]
        for fn in (job.first, job.mid, job.last):
            if fn is not None:
                fn(j_ins, j_inout, j_fresh, ssem, rsem)

    return pl.pallas_call(
        body, in_specs=[_HBM] * (n_ji + n_jio), out_specs=[_HBM] * (n_jio + n_jf),
        out_shape=[jax.ShapeDtypeStruct(b.shape, b.dtype) for b in job.inout] + list(job.fresh),
        scratch_shapes=[pltpu.SemaphoreType.DMA((job.nsem,)), pltpu.SemaphoreType.DMA((job.nsem,))],
        input_output_aliases={n_ji + k: k for k in range(n_jio)}, name=name,
    )(*job.ins, *job.inout)


_DIMS ={"nn": ((1,), (0,)), "nt": ((1,), (1,)), "tn": ((0,), (0,))}


def _matmul(a, b, mode, out_dtype, name, n=None, slab=None, into=None, job=None, norms=None, tm_cap=512, tn_cap=2048,
            tk_cap=1408):
    if mode == "nn":
        (m, k), (k2, nn_) = a.shape, b.shape
    elif mode == "nt":
        (m, k), (nn_, k2) = a.shape, b.shape
    else:
        (k, m), (k2, nn_) = a.shape, b.shape
    n = nn_ if n is None else n
    assert k == k2, (a.shape, b.shape, mode)
    tm, tn, tk = _tile(m, tm_cap), _tile(n, tn_cap), _tile(k, tk_cap)
    if slab is not None and slab[2]:
        tm = _tile(math.gcd(m, slab[2]), tm_cap)
    nk = k // tk
    if mode == "tn":
        a_spec = pl.BlockSpec((tk, tm), lambda j, i, kk, *_: (kk, i))
    else:
        a_spec = pl.BlockSpec((tm, tk), lambda j, i, kk, *_: (i, kk))
    if mode == "nt":
        b_spec = pl.BlockSpec((tn, tk), lambda j, i, kk, *_: (j, kk))
    else:
        b_spec = pl.BlockSpec((tk, tn), lambda j, i, kk, *_: (kk, j))
    dims = (_DIMS[mode], ((), ()))
    aliased = into is not None

    n_in = 2 + aliased + (3 if norms is not None else 0)

    def finish(refs, z):
        refs[n_in][...] = z.astype(out_dtype).reshape(refs[n_in].shape)
        if norms is not None:
            x_ref, gp_ref, gn_ref = refs[n_in - 3:n_in]
            r = lax.rsqrt(jnp.mean(z * z, axis=-1, keepdims=True) + EPS)
            xn = x_ref[...] + z * r * gp_ref[...]
            refs[n_in + 1][...] = xn
            r2 = lax.rsqrt(jnp.mean(xn * xn, axis=-1, keepdims=True) + EPS)
            refs[n_in + 2][...] = (xn * r2 * gn_ref[...]).astype(BF16)

    def body(*refs):
        p = lax.dot_general(refs[0][...], refs[1][...], dims, preferred_element_type=F32)
        if nk == 1:
            finish(refs, p)
        else:
            acc = refs[-1]
            kk = pl.program_id(2)

            @pl.when(kk == 0)
            def _():
                acc[...] = p

            @pl.when(kk > 0)
            def _():
                acc[...] += p

            @pl.when(kk == nk - 1)
            def _():
                finish(refs, acc[...])

    if slab is None:
        out_spec = pl.BlockSpec((tm, tn), lambda j, i, kk: (i, j))
        out_shape = jax.ShapeDtypeStruct((m, n), out_dtype)
    else:
        shape3, lead, row0 = slab
        assert row0 % tm == 0 and shape3[2] == n
        out_spec = pl.BlockSpec((1, tm, tn), lambda j, i, kk: (lead, row0 // tm + i, j))
        out_shape = jax.ShapeDtypeStruct(shape3, out_dtype)
    in_specs, args = [a_spec, b_spec], [a, b]
    if aliased:
        in_specs.append(pl.BlockSpec(memory_space=pl.ANY))
        args.append(into)
    if norms is not None:
        assert tn == n and slab is None, "the fused norms need whole rows"
        row = pl.BlockSpec((tm, n), lambda j, i, kk: (i, 0))
        vec = pl.BlockSpec((1, n), lambda j, i, kk: (0, 0))
        in_specs += [row, vec, vec]
        args += list(norms)
        out_spec = [out_spec, row, row]
        out_shape = [out_shape, jax.ShapeDtypeStruct((m, n), F32), jax.ShapeDtypeStruct((m, n), BF16)]
    out, moved = _call(
        body, grid=(n // tn, m // tm, nk), in_specs=in_specs, out_specs=out_spec, out_shape=out_shape,
        scratch_shapes=[pltpu.VMEM((tm, tn), F32)] if nk > 1 else [], dims=("parallel", "parallel", "arbitrary"), name=name,
        args=args, aliases={2: 0} if aliased else None, job=job)
    return out if job is None else (out, moved)


def _matmul_rows(pieces, b, rows, name, job=None, tm=1024, tk=1024):
    k, n = b.shape
    tm = math.gcd(tm, *[a.shape[1] for a in pieces])
    tk = _tile(k, tk)
    nk = k // tk
    spans, r0 = [], 0
    for a in pieces:
        assert a.shape[0] == k and a.shape[1] % tm == 0
        spans.append((r0, a.shape[1] // tm))
        r0 += a.shape[1] // tm
    nr = r0
    np_ = len(pieces)

    def body(*refs):
        b_ref, o_ref, acc = refs[np_], refs[np_ + 1], refs[-1]
        r, kk = pl.program_id(0), pl.program_id(1)
        for p, (first, count) in enumerate(spans):
            @pl.when((r >= first) & (r < first + count))
            def _(p=p):
                part = lax.dot_general(refs[p][...], b_ref[...], (_DIMS["tn"], ((), ())), preferred_element_type=F32)

                @pl.when(kk == 0)
                def _():
                    acc[...] = part

                @pl.when(kk > 0)
                def _():
                    acc[...] += part

        @pl.when(kk == nk - 1)
        def _():
            o_ref[0] = acc[...].astype(BF16)

    in_specs = []
    for first, count in spans:
        in_specs.append(pl.BlockSpec((tk, tm), lambda r, kk, f=first, c=count: (
            jnp.where(r < f, 0, jnp.where(r >= f + c, nk - 1, kk)), jnp.clip(r - f, 0, c - 1))))
    in_specs.append(pl.BlockSpec((tk, n), lambda r, kk: (kk, 0)))
    out, moved = _call(
        body, grid=(nr, nk), in_specs=in_specs, out_specs=pl.BlockSpec((1, tm, n), lambda r, kk: (0, r, 0)),
        out_shape=jax.ShapeDtypeStruct((1, rows, n), BF16), scratch_shapes=[pltpu.VMEM((tm, n), F32)],
        dims=("arbitrary", "arbitrary"), name=name, args=list(pieces) + [b], job=job)
    return out if job is None else (out, moved)


def _matmul_pieces(pieces, addend, name, tk, job=None, tm_cap=512):
    m = pieces[0][0].shape[0]
    n = pieces[0][1].shape[1]
    tm = _tile(m, tm_cap)
    spans, s0 = [], 0
    for a, b, row0 in pieces:
        assert a.shape[1] % tk == 0 and row0 % tk == 0 and b.shape[1] == n and a.shape[0] == m
        spans.append((s0, a.shape[1] // tk, row0 // tk))
        s0 += a.shape[1] // tk
    steps = s0
    np_ = len(pieces)
    groups = []
    for (a, b, _), (first, count, brow) in zip(pieces, spans):
        if groups and groups[-1][0] is b and groups[-1][3] + groups[-1][2] == brow:
            groups[-1][2] += count
        else:
            groups.append([b, first, count, brow])
    b_of = []
    for first, count, _ in spans:
        b_of.append(next(k for k, g in enumerate(groups) if g[1] <= first < g[1] + g[2]))
    ng = len(groups)

    nm = m // tm

    def body(*refs):
        o_ref, acc = refs[-2], refs[-1]
        s, i = pl.program_id(0), pl.program_id(1)
        rows = pl.ds(pl.multiple_of(i * tm, tm), tm)

        @pl.when(s == 0)
        def _():
            acc[rows, :] = refs[np_ + ng][...] if addend is not None else jnp.zeros((tm, n), F32)

        for p, (first, count, _) in enumerate(spans):
            @pl.when((s >= first) & (s < first + count))
            def _(p=p):
                acc[rows, :] += jnp.dot(refs[p][...], refs[np_ + b_of[p]][...], preferred_element_type=F32)

        @pl.when(s == steps - 1)
        def _():
            o_ref[...] = acc[rows, :]

    in_specs, args = [], []
    for (a, _, _), (first, count, _) in zip(pieces, spans):
        in_specs.append(pl.BlockSpec((tm, tk), lambda s, i, f=first, c=count: (
            jnp.where(s < f, 0, jnp.where(s >= f + c, nm - 1, i)), jnp.clip(s - f, 0, c - 1))))
        args.append(a)
    for b, first, count, brow in groups:
        in_specs.append(pl.BlockSpec((tk, n), lambda s, i, f=first, c=count, r=brow: (r + jnp.clip(s - f, 0, c - 1), 0)))
        args.append(b)
    if addend is not None:
        in_specs.append(pl.BlockSpec((tm, n), lambda s, i: (jnp.where(s == 0, i, nm - 1), 0)))
        args.append(addend)
    out, moved = _call(
        body, grid=(steps, nm), in_specs=in_specs,
        out_specs=pl.BlockSpec((tm, n), lambda s, i: (jnp.where(s == steps - 1, i, 0), 0)),
        out_shape=jax.ShapeDtypeStruct((m, n), F32), scratch_shapes=[pltpu.VMEM((m, n), F32)],
        dims=("arbitrary", "arbitrary"), name=name, args=args, job=job)
    return out if job is None else (out, moved)


def _norm_fwd(x, z, g_post, g_next, name):
    t, d = x.shape
    tt = _rows(t, 512)
    row = pl.BlockSpec((tt, d), lambda i: (i, 0))
    vec = pl.BlockSpec((1, d), lambda i: (0, 0))

    def body(*refs):
        if z is None:
            x_ref, gn_ref, h_ref = refs
            xn = x_ref[...]
        else:
            x_ref, z_ref, gp_ref, gn_ref, xo_ref, h_ref = refs
            zz = z_ref[...]
            r = lax.rsqrt(jnp.mean(zz * zz, axis=-1, keepdims=True) + EPS)
            xn = x_ref[...] + zz * r * gp_ref[...]
            xo_ref[...] = xn
        r2 = lax.rsqrt(jnp.mean(xn * xn, axis=-1, keepdims=True) + EPS)
        h_ref[...] = (xn * r2 * gn_ref[...]).astype(BF16)

    if z is None:
        return pl.pallas_call(
            body, grid=(t // tt,), in_specs=[row, vec], out_specs=row,
            out_shape=jax.ShapeDtypeStruct((t, d), BF16), compiler_params=_cparams(("parallel",)), name=name,
        )(x, g_next)
    return pl.pallas_call(
        body, grid=(t // tt,), in_specs=[row, row, vec, vec], out_specs=[row, row],
        out_shape=[jax.ShapeDtypeStruct((t, d), F32), jax.ShapeDtypeStruct((t, d), BF16)],
        compiler_params=_cparams(("parallel",)), name=name,
    )(x, z, g_post, g_next)


def _rms_bwd(dy, x, g):
    r = lax.rsqrt(jnp.mean(x * x, axis=-1, keepdims=True) + EPS)
    n = x * r
    dn = dy * g
    dx = r * (dn - n * jnp.mean(dn * n, axis=-1, keepdims=True))
    return dx, dy * n


def _norm_bwd(dres, pre, post, name):
    t, d = dres.shape
    tt = _rows(t, 512)
    nt = t // tt
    row = pl.BlockSpec((tt, d), lambda i: (i, 0))
    vec = pl.BlockSpec((1, d), lambda i: (0, 0))
    has_pre, has_post = pre is not None, post is not None
    n_in = 1 + (3 if has_pre else 0) + (2 if has_post else 0)
    n_out = has_pre + has_post + has_pre + has_post

    def body(*refs):
        ins, outs, scr = refs[:n_in], refs[n_in:n_in + n_out], refs[n_in + n_out:]
        i = pl.program_id(0)
        dx = ins[0][...]
        pos, opos, spos = 1, 0, 0
        accs = []
        if has_pre:
            dh_ref, xa_ref, ga_ref = ins[pos:pos + 3]
            pos += 3
            dxa, dga_t = _rms_bwd(dh_ref[...], xa_ref[...], ga_ref[...])
            dx = dx + dxa
            outs[opos][...] = dx
            opos += 1
            accs.append((scr[spos], dga_t))
            spos += 1
        if has_post:
            zb_ref, gb_ref = ins[pos:pos + 2]
            dz, dgb_t = _rms_bwd(dx, zb_ref[...], gb_ref[...])
            outs[opos][...] = dz.astype(BF16)
            opos += 1
            accs.append((scr[spos], dgb_t))
            spos += 1
        for (acc, val), out in zip(accs, outs[opos:]):
            part = _sum8(val)

            @pl.when(i == 0)
            def _(acc=acc, part=part):
                acc[...] = part

            @pl.when(i > 0)
            def _(acc=acc, part=part):
                acc[...] += part

            @pl.when(i == nt - 1)
            def _(acc=acc, out=out):
                out[...] = jnp.sum(acc[...], axis=0, keepdims=True)

    in_specs, args = [row], [dres]
    out_specs, out_shape = [], []
    if has_pre:
        in_specs += [row, row, vec]
        args += list(pre)
        out_specs.append(row)
        out_shape.append(jax.ShapeDtypeStruct((t, d), F32))
    if has_post:
        in_specs += [row, vec]
        args += list(post)
        out_specs.append(row)
        out_shape.append(jax.ShapeDtypeStruct((t, d), BF16))
    for _ in range(has_pre + has_post):
        out_specs.append(vec)
        out_shape.append(jax.ShapeDtypeStruct((1, d), F32))
    return pl.pallas_call(
        body, grid=(nt,), in_specs=in_specs, out_specs=out_specs, out_shape=out_shape,
        scratch_shapes=[pltpu.VMEM((SUBLANE, d), F32)] * (has_pre + has_post),
        compiler_params=_cparams(("arbitrary",)), name=name,
    )(*args)


def _loss_grad(y, target, name):
    t, d = y.shape
    tt = _rows(t, 512)
    nt = t // tt
    row = pl.BlockSpec((tt, d), lambda i: (i, 0))
    inv_d = 1.0 / d

    def body(y_ref, t_ref, dy_ref, l_ref):
        i = pl.program_id(0)
        diff = y_ref[...] - t_ref[...]
        dy_ref[...] = diff * inv_d
        s8 = _sum8(diff * diff)
        part = s8[:, 0:LANE]
        for k in range(1, d // LANE):
            part = part + s8[:, k * LANE:(k + 1) * LANE]
        part = part * (0.5 * inv_d)

        @pl.when(i == 0)
        def _():
            l_ref[...] = part

        @pl.when(i > 0)
        def _():
            l_ref[...] += part

    return pl.pallas_call(
        body, grid=(nt,), in_specs=[row, row],
        out_specs=[row, pl.BlockSpec((SUBLANE, LANE), lambda i: (0, 0))],
        out_shape=[jax.ShapeDtypeStruct((t, d), F32), jax.ShapeDtypeStruct((SUBLANE, LANE), F32)],
        compiler_params=_cparams(("arbitrary",)), name=name,
    )(y, target)


def _swiglu_fwd(a, b, name):
    t, f = a.shape
    tt = _rows(t, 256)
    blk = pl.BlockSpec((tt, f), lambda i: (i, 0))

    def body(a_ref, b_ref, m_ref):
        av = a_ref[...].astype(F32)
        m_ref[...] = (av * _sigmoid(av) * b_ref[...].astype(F32)).astype(BF16)

    return pl.pallas_call(
        body, grid=(t // tt,), in_specs=[blk, blk], out_specs=blk,
        out_shape=jax.ShapeDtypeStruct((t, f), BF16), compiler_params=_cparams(("parallel",)), name=name,
    )(a, b)


def _swiglu_bwd(a, b, dm, name):
    t, f = a.shape
    tt = _rows(t, 256)
    blk = pl.BlockSpec((tt, f), lambda i: (i, 0))

    def body(a_ref, b_ref, dm_ref, da_ref, db_ref):
        av = a_ref[...].astype(F32)
        s = _sigmoid(av)
        dv = dm_ref[...].astype(F32)
        da_ref[...] = (dv * b_ref[...].astype(F32) * s * (1.0 + av * (1.0 - s))).astype(BF16)
        db_ref[...] = (dv * av * s).astype(BF16)

    return pl.pallas_call(
        body, grid=(t // tt,), in_specs=[blk, blk, blk], out_specs=[blk, blk],
        out_shape=[jax.ShapeDtypeStruct((t, f), BF16)] * 2, compiler_params=_cparams(("parallel",)), name=name,
    )(a, b, dm)


def _log_sigmoid(x):
    return jnp.minimum(x, 0.0) - jnp.log1p(jnp.exp(-jnp.abs(x)))


def _fox_prep(f_t, b_f, name):
    h, t = f_t.shape

    def body(f_ref, b_ref, c_ref):
        r = lax.broadcasted_iota(jnp.int32, (LANE, LANE), 0)
        c = lax.broadcasted_iota(jnp.int32, (LANE, LANE), 1)
        upper = (r <= c).astype(F32)
        carry = jnp.zeros((h, 1), F32)
        for j in range(t // LANE):
            sl = slice(j * LANE, (j + 1) * LANE)
            lf = _log_sigmoid(f_ref[:, sl] + b_ref[...])
            cs = jnp.dot(lf, upper, precision=lax.Precision.HIGHEST, preferred_element_type=F32) + carry
            c_ref[:, sl] = cs
            carry = cs[:, LANE - 1:LANE]

    return pl.pallas_call(body, out_shape=jax.ShapeDtypeStruct((h, t), F32), compiler_params=_cparams(), name=name)(f_t, b_f)


def _fox_bwd(dc_q, dc_k, f_t, b_f, name):
    h, t = f_t.shape

    def body(dq_ref, dk_ref, f_ref, b_ref, df_ref, db_ref):
        r = lax.broadcasted_iota(jnp.int32, (LANE, LANE), 0)
        c = lax.broadcasted_iota(jnp.int32, (LANE, LANE), 1)
        lower = (r >= c).astype(F32)
        carry = jnp.zeros((h, 1), F32)
        dbsum = jnp.zeros((h, 1), F32)
        for j in reversed(range(t // LANE)):
            sl = slice(j * LANE, (j + 1) * LANE)
            dc = dq_ref[:, sl] - dk_ref[:, sl]
            dl = jnp.dot(dc, lower, precision=lax.Precision.HIGHEST, preferred_element_type=F32) + carry
            carry = dl[:, 0:1]
            df = dl * _sigmoid(-(f_ref[:, sl] + b_ref[...]))
            df_ref[:, sl] = df
            dbsum = dbsum + jnp.sum(df, axis=-1, keepdims=True)
        db_ref[...] = dbsum

    return pl.pallas_call(
        body, out_shape=[jax.ShapeDtypeStruct((h, t), F32), jax.ShapeDtypeStruct((h, 1), F32)],
        compiler_params=_cparams(), name=name,
    )(dc_q, dc_k, f_t, b_f)


ATTN_FWD = (1024, 512)
ATTN_BWD = (512, 512)


def _attn_tiles(t, tiles):
    return _tile(t, tiles[0]), _tile(t, tiles[1])


def _attn_fwd(proj, c_t, d, name, job=None):
    t = proj.shape[0]
    h = d // LANE
    bq, bk = _attn_tiles(t, ATTN_FWD)
    nq, nk, rr = t // bq, t // bk, bq // bk
    qc, kc, vc = COL_Q * h, COL_K * h, COL_VA * h
    qscale = LANE ** -0.5 * LOG2E

    def body(q_ref, k_ref, v_ref, cc_ref, cr_ref, o_ref, lse_ref, kb, vt, ckb, acc):
        i = pl.program_id(1)

        @pl.when(i == 0)
        def _():
            kb[...] = k_ref[...].astype(BF16)
            ckb[...] = jnp.broadcast_to(cc_ref[0] * LOG2E, (t, bq))
            for jn in range(nk):
                vt[jn] = v_ref[jn * bk:(jn + 1) * bk, :].astype(F32).T.astype(BF16)

        q = (q_ref[...].astype(F32) * qscale).astype(BF16)
        cq = cr_ref[0, 0] * LOG2E
        acc[...] = jnp.zeros((LANE, bq), F32)

        def block(j, diag, m_old, l_old):
            off = 0 if diag is None else diag * bk
            w = bq - off
            rows = pl.ds(pl.multiple_of(j * bk, bk), bk)
            s = _nt_dot(kb[rows, :], q[off:, :]) - ckb[rows, off:]
            if diag is not None:
                kk = lax.broadcasted_iota(jnp.int32, (bk, w), 0)
                qq = lax.broadcasted_iota(jnp.int32, (bk, w), 1)
                s = jnp.where(qq >= kk, s, NEG)
            cqs, m_part, l_part = cq[:, off:], m_old[:, off:], l_old[:, off:]
            m_new = jnp.maximum(m_part, jnp.max(s, axis=0, keepdims=True) + cqs)
            p = jnp.exp2(s + (cqs - m_new))
            alpha = jnp.exp2(m_part - m_new)
            l_new = alpha * l_part + jnp.sum(p, axis=0, keepdims=True)
            acc[:, off:] = alpha * acc[:, off:] + jnp.dot(vt[j], p.astype(BF16), preferred_element_type=F32)
            if off:
                m_new = jnp.concatenate([m_old[:, :off], m_new], axis=1)
                l_new = jnp.concatenate([l_old[:, :off], l_new], axis=1)
            return m_new, l_new

        m, l = lax.fori_loop(0, i * rr, lambda j, c: block(j, None, *c),
                             (jnp.full((1, bq), NEG, F32), jnp.zeros((1, bq), F32)))
        for jj in range(rr):
            m, l = block(i * rr + jj, jj, m, l)
        o_ref[...] = (acc[...] / l).T
        lse_ref[0, 0] = m + jnp.log2(l)

    rowq = pl.BlockSpec((1, 1, 1, bq), lambda hh, i: (hh, i, 0, 0))
    outs, moved = _call(
        body, grid=(h, nq),
        in_specs=[
            pl.BlockSpec((bq, LANE), lambda hh, i: (i, qc + hh)),
            pl.BlockSpec((t, LANE), lambda hh, i: (0, kc + hh)),
            pl.BlockSpec((t, LANE), lambda hh, i: (0, vc + hh)),
            pl.BlockSpec((1, t, 1), lambda hh, i: (hh, 0, 0)),
            rowq,
        ],
        out_specs=[pl.BlockSpec((bq, LANE), lambda hh, i: (i, hh)), rowq],
        out_shape=[jax.ShapeDtypeStruct((t, d), F32), jax.ShapeDtypeStruct((h, nq, 1, bq), F32)],
        scratch_shapes=[pltpu.VMEM((t, LANE), BF16), pltpu.VMEM((nk, LANE, bk), BF16), pltpu.VMEM((t, bq), F32),
                        pltpu.VMEM((LANE, bq), F32)],
        dims=("arbitrary", "arbitrary"), name=name,
        args=[proj, proj, proj, c_t.reshape(h, t, 1), c_t.reshape(h, nq, 1, bq)], job=job)
    outs = [outs[0], outs[1].reshape(h, t)]
    return outs if job is None else (outs, moved)


def _attn_bwd(proj, do, o, lse, c_t, d, name, job=None):
    t = proj.shape[0]
    h = d // LANE
    bq, bk = _attn_tiles(t, ATTN_BWD)
    nq, nk, rr = t // bq, t // bk, bq // bk
    qc, kc, vc = COL_Q * h, COL_K * h, COL_VA * h
    scale = LANE ** -0.5

    def body(q_ref, k_ref, v_ref, do_ref, o_ref, lse_ref, cc_ref, cr_ref, dq_ref, dk_ref, dv_ref, dcq_ref, dck_ref,
             kb, kt, vb, ckb, dk_acc, dv_acc, dck_acc, dqt_acc):
        i = pl.program_id(1)

        @pl.when(i == 0)
        def _():
            kb[...] = k_ref[...].astype(BF16)
            vb[...] = v_ref[...].astype(BF16)
            ckb[...] = jnp.broadcast_to(cc_ref[0] * LOG2E, (t, bq))
            for jn in range(nk):
                kt[jn] = k_ref[jn * bk:(jn + 1) * bk, :].astype(F32).T.astype(BF16)
            dk_acc[...] = jnp.zeros((t, LANE), F32)
            dv_acc[...] = jnp.zeros((t, LANE), F32)
            dck_acc[...] = jnp.zeros((t, LANE), F32)

        q = (q_ref[...].astype(F32) * (scale * LOG2E)).astype(BF16)
        dof = do_ref[...]
        dob = dof.astype(BF16)
        delta = jnp.sum((dof * o_ref[...]).T, axis=0, keepdims=True)
        rowb = cr_ref[0, 0] * LOG2E - lse_ref[0, 0]
        dqt_acc[...] = jnp.zeros((LANE, bq), F32)

        def block(j, diag, dcq):
            rows = pl.ds(pl.multiple_of(j * bk, bk), bk)
            p = jnp.exp2(_nt_dot(kb[rows, :], q) - ckb[rows, :] + rowb)
            if diag is not None:
                kk = lax.broadcasted_iota(jnp.int32, (bk, bq), 0)
                qq = lax.broadcasted_iota(jnp.int32, (bk, bq), 1)
                p = jnp.where(qq >= kk + diag * bk, p, 0.0)
            dv_acc[rows, :] += jnp.dot(p.astype(BF16), dob, preferred_element_type=F32)
            ds = p * (_nt_dot(vb[rows, :], dob) - delta)
            dsb = ds.astype(BF16)
            dk_acc[rows, :] += jnp.dot(dsb, q, preferred_element_type=F32)
            dqt_acc[...] += jnp.dot(kt[j], dsb, preferred_element_type=F32)
            part = ds[:, 0:LANE]
            for k in range(1, bq // LANE):
                part = part + ds[:, k * LANE:(k + 1) * LANE]
            dck_acc[rows, :] += part
            return dcq + jnp.sum(ds, axis=0, keepdims=True)

        dcq = lax.fori_loop(0, i * rr, lambda j, c: block(j, None, c), jnp.zeros((1, bq), F32))
        for jj in range(rr):
            dcq = block(i * rr + jj, jj, dcq)
        dq_ref[...] = (dqt_acc[...] * scale).T.astype(BF16)
        dcq_ref[0, 0] = dcq

        @pl.when(i == nq - 1)
        def _():
            dk_ref[...] = (dk_acc[...] * LN2).astype(BF16)
            dv_ref[...] = dv_acc[...].astype(BF16)
            dck_ref[0] = jnp.sum(dck_acc[...], axis=-1, keepdims=True)

    rowq = pl.BlockSpec((1, 1, 1, bq), lambda hh, i: (hh, i, 0, 0))
    blk = pl.BlockSpec((bq, LANE), lambda hh, i: (i, hh))
    whole = pl.BlockSpec((t, LANE), lambda hh, i: (0, hh))
    colk = pl.BlockSpec((1, t, 1), lambda hh, i: (hh, 0, 0))
    outs, moved = _call(
        body, grid=(h, nq),
        in_specs=[
            pl.BlockSpec((bq, LANE), lambda hh, i: (i, qc + hh)),
            pl.BlockSpec((t, LANE), lambda hh, i: (0, kc + hh)),
            pl.BlockSpec((t, LANE), lambda hh, i: (0, vc + hh)),
            blk, blk, rowq, colk, rowq,
        ],
        out_specs=[blk, whole, whole, rowq, colk],
        out_shape=[jax.ShapeDtypeStruct((t, d), BF16), jax.ShapeDtypeStruct((t, d), BF16), jax.ShapeDtypeStruct((t, d), BF16),
                   jax.ShapeDtypeStruct((h, nq, 1, bq), F32), jax.ShapeDtypeStruct((h, t, 1), F32)],
        scratch_shapes=[pltpu.VMEM((t, LANE), BF16), pltpu.VMEM((nk, LANE, bk), BF16), pltpu.VMEM((t, LANE), BF16),
                        pltpu.VMEM((t, bq), F32), pltpu.VMEM((t, LANE), F32), pltpu.VMEM((t, LANE), F32),
                        pltpu.VMEM((t, LANE), F32), pltpu.VMEM((LANE, bq), F32)],
        dims=("arbitrary", "arbitrary"), name=name,
        args=[proj, proj, proj, do, o, lse.reshape(h, nq, 1, bq), c_t.reshape(h, t, 1), c_t.reshape(h, nq, 1, bq)], job=job)
    outs = list(outs[:3]) + [outs[3].reshape(h, t), outs[4].reshape(h, t)]
    return outs if job is None else (outs, moved)


def _sgu_forward(u_ref, v_ref, gv_ref, wm_ref, bs_ref, mix_sc, groups):
    gu, dgu = _gelu_and_grad(u_ref[...].astype(F32))
    gvv, dgv = _gelu_and_grad(v_ref[...].astype(F32))
    mu = jnp.mean(gvv, axis=-1, keepdims=True)
    xc = gvv - mu
    r = lax.rsqrt(jnp.mean(xc * xc, axis=-1, keepdims=True) + EPS)
    nhat = xc * r
    vn = (nhat * gv_ref[...]).astype(BF16)
    for g in range(groups):
        sl = slice(g * LANE, (g + 1) * LANE)
        mix_sc[:, sl] = jnp.dot(wm_ref[g], vn[:, sl], preferred_element_type=F32) + bs_ref[g]
    return gu, dgu, dgv, nhat, r, vn, mix_sc[...]


def _mix_fwd(proj, o, wm, bs, g_v, d, name):
    t = proj.shape[0]
    groups = d // LANE

    def body(u_ref, v_ref, ga_ref, gb_ref, o_ref, wm_ref, bs_ref, gv_ref, out_ref, mix_sc):
        gu, _, _, _, _, _, mixed = _sgu_forward(u_ref, v_ref, gv_ref, wm_ref, bs_ref, mix_sc, groups)
        out_ref[...] = (_sigmoid(ga_ref[...].astype(F32)) * (gu * mixed) + _sigmoid(gb_ref[...].astype(F32)) * o_ref[...]).astype(BF16)

    def colblk(k):
        return pl.BlockSpec((LANE, d), lambda i, k=k: (i, k))

    full3 = pl.BlockSpec((groups, LANE, LANE), lambda i: (0, 0, 0))
    return pl.pallas_call(
        body, grid=(t // LANE,),
        in_specs=[colblk(COL_U), colblk(COL_V), colblk(COL_GA), colblk(COL_GB), colblk(0), full3,
                  pl.BlockSpec((groups, LANE, 1), lambda i: (0, 0, 0)), pl.BlockSpec((1, d), lambda i: (0, 0))],
        out_specs=colblk(0),
        out_shape=jax.ShapeDtypeStruct((t, d), BF16),
        scratch_shapes=[pltpu.VMEM((LANE, d), F32)],
        compiler_params=_cparams(("parallel",)), name=name,
    )(proj, proj, proj, proj, o, wm, bs, g_v)


def _mix_bwd(dmerged, proj, o, wm, wm_t, bs, g_v, d, name, job=None):
    t = proj.shape[0]
    groups = d // LANE
    nt = t // LANE

    def body(dm_ref, u_ref, v_ref, ga_ref, gb_ref, o_ref, wm_ref, wmt_ref, bs_ref, gv_ref,
             duv_ref, dg_ref, do_ref, dws_ref, dbs_ref, dgv_ref, mix_sc, dvn_sc, gv_acc):
        i = pl.program_id(0)

        @pl.when(i == 0)
        def _():
            dws_ref[...] = jnp.zeros_like(dws_ref)
            dbs_ref[...] = jnp.zeros_like(dbs_ref)
            gv_acc[...] = jnp.zeros_like(gv_acc)

        gu, dgu, dgv, nhat, r, vn, mixed = _sgu_forward(u_ref, v_ref, gv_ref, wm_ref, bs_ref, mix_sc, groups)
        dm = dm_ref[...]
        sa = _sigmoid(ga_ref[...].astype(F32))
        sb = _sigmoid(gb_ref[...].astype(F32))
        ov = o_ref[...]
        y_a = gu * mixed
        dg_ref[:, 0:d] = (dm * y_a * sa * (1.0 - sa)).astype(BF16)
        dg_ref[:, d:2 * d] = (dm * ov * sb * (1.0 - sb)).astype(BF16)
        do_ref[...] = dm * sb
        dy_a = dm * sa
        duv_ref[:, 0:d] = (dy_a * mixed * dgu).astype(BF16)
        dmixed = dy_a * gu
        dmixed_b = dmixed.astype(BF16)
        for g in range(groups):
            sl = slice(g * LANE, (g + 1) * LANE)
            dvn_sc[:, sl] = jnp.dot(wmt_ref[g], dmixed_b[:, sl], preferred_element_type=F32)
            dws_ref[g] += _nt_dot(dmixed_b[:, sl], vn[:, sl])
            dbs_ref[g] += jnp.sum(dmixed[:, sl], axis=-1, keepdims=True)
        dvn = dvn_sc[...]
        gv_acc[...] += _sum8(dvn * nhat)
        dn = dvn * gv_ref[...]
        dgelu = r * (dn - jnp.mean(dn, axis=-1, keepdims=True) - nhat * jnp.mean(dn * nhat, axis=-1, keepdims=True))
        duv_ref[:, d:2 * d] = (dgelu * dgv).astype(BF16)

        @pl.when(i == nt - 1)
        def _():
            dgv_ref[...] = jnp.sum(gv_acc[...], axis=0, keepdims=True)
            rr = lax.broadcasted_iota(jnp.int32, (LANE, LANE), 0)
            cl = lax.broadcasted_iota(jnp.int32, (LANE, LANE), 1)
            for g in range(groups):
                dws_ref[g] = jnp.where(rr >= cl, dws_ref[g], 0.0)

    def colblk(k):
        return pl.BlockSpec((LANE, d), lambda i, k=k: (i, k))

    full3 = pl.BlockSpec((groups, LANE, LANE), lambda i: (0, 0, 0))
    col3 = pl.BlockSpec((groups, LANE, 1), lambda i: (0, 0, 0))
    vec = pl.BlockSpec((1, d), lambda i: (0, 0))
    two = pl.BlockSpec((LANE, 2 * d), lambda i: (i, 0))
    outs, moved = _call(
        body, grid=(nt,),
        in_specs=[colblk(0), colblk(COL_U), colblk(COL_V), colblk(COL_GA), colblk(COL_GB), colblk(0), full3, full3, col3, vec],
        out_specs=[two, two, colblk(0), full3, col3, vec],
        out_shape=[jax.ShapeDtypeStruct((t, 2 * d), BF16), jax.ShapeDtypeStruct((t, 2 * d), BF16), jax.ShapeDtypeStruct((t, d), F32),
                   jax.ShapeDtypeStruct((groups, LANE, LANE), F32), jax.ShapeDtypeStruct((groups, LANE, 1), F32),
                   jax.ShapeDtypeStruct((1, d), F32)],
        scratch_shapes=[pltpu.VMEM((LANE, d), F32), pltpu.VMEM((LANE, d), F32), pltpu.VMEM((SUBLANE, d), F32)],
        dims=("arbitrary",), name=name, args=[dmerged, proj, proj, proj, proj, o, wm, wm_t, bs, g_v], job=job)
    return outs if job is None else (outs, moved)


def _adam_math(w, g, m, v):
    nm = ADAM_B1 * m + (1.0 - ADAM_B1) * g
    nv = ADAM_B2 * v + (1.0 - ADAM_B2) * (g * g)
    delta = -ADAM_LR * ((nm * ADAM_C1) / (jnp.sqrt(nv * ADAM_C2) + ADAM_EPS) + ADAM_WD * w)
    return delta, nm, nv


def _adamw(w, g, m, v, name):
    r, c = w.shape
    cap = max(SUBLANE, (2 * 1024 * 1024) // (4 * c) // SUBLANE * SUBLANE)
    tr = _rows(r, cap)

    def body(w_ref, g_ref, m_ref, v_ref, d_ref, nm_ref, nv_ref):
        d_ref[...], nm_ref[...], nv_ref[...] = _adam_math(w_ref[...], g_ref[...], m_ref[...], v_ref[...])

    blk = pl.BlockSpec((tr, c), lambda i: (i, 0))
    return pl.pallas_call(
        body, grid=(r // tr,), in_specs=[blk] * 4, out_specs=[blk] * 3,
        out_shape=[jax.ShapeDtypeStruct((r, c), F32)] * 3, compiler_params=_cparams(("parallel",)), name=name,
    )(w, g, m, v)


def _adamw_layers(w, g0, g1, m, v, name):
    _, r, c = w.shape
    cap = max(SUBLANE, (1024 * 1024) // (4 * c) // SUBLANE * SUBLANE)
    tr = _rows(r, cap)

    def body(w_ref, g0_ref, g1_ref, m_ref, v_ref, g_ref, d_ref, nm_ref, nv_ref):
        gg = jnp.where(pl.program_id(0) == 0, g0_ref[...], g1_ref[...])
        g_ref[0] = gg
        d_ref[0], nm_ref[0], nv_ref[0] = _adam_math(w_ref[0], gg, m_ref[0], v_ref[0])

    lay = pl.BlockSpec((1, tr, c), lambda l, i: (l, i, 0))

    def gspec(l0):
        return pl.BlockSpec((tr, c), lambda l, i: (jnp.where(l == l0, i, 0), 0))

    return pl.pallas_call(
        body, grid=(2, r // tr), in_specs=[lay, gspec(0), gspec(1), lay, lay], out_specs=[lay] * 4,
        out_shape=[jax.ShapeDtypeStruct((2, r, c), F32)] * 4, compiler_params=_cparams(("arbitrary", "arbitrary")), name=name,
    )(w, g0, g1, m, v)


def _adamw_interleaved(w, g0, g1, m, v, name):
    r, _, c = w.shape
    tr = 128

    def body(w_ref, g0_ref, g1_ref, m_ref, v_ref, g_ref, d_ref, nm_ref, nv_ref):
        for l, gl in enumerate((g0_ref, g1_ref)):
            gg = gl[...]
            g_ref[:, l, :] = gg
            d_ref[:, l, :], nm_ref[:, l, :], nv_ref[:, l, :] = _adam_math(w_ref[:, l, :], gg, m_ref[:, l, :], v_ref[:, l, :])

    lay = pl.BlockSpec((tr, 2, c), lambda i: (i, 0, 0))
    flat = pl.BlockSpec((tr, c), lambda i: (i, 0))
    return pl.pallas_call(
        body, grid=(pl.cdiv(r, tr),), in_specs=[lay, flat, flat, lay, lay], out_specs=[lay] * 4,
        out_shape=[jax.ShapeDtypeStruct((r, 2, c), F32)] * 4, compiler_params=_cparams(("parallel",)), name=name,
    )(w, g0, g1, m, v)


def _add_half(p4, recv, c_idx, name):
    _, r, c = p4.shape
    hw = c // 2
    tr = 256 if r % 256 == 0 else r

    def body(c_ref, a_ref, b_ref, o_ref):
        o_ref[...] = (a_ref[...].astype(F32) + b_ref[...].astype(F32)).astype(BF16)

    return pl.pallas_call(
        body,
        grid_spec=pltpu.PrefetchScalarGridSpec(
            num_scalar_prefetch=1, grid=(N_CHIPS, pl.cdiv(r, tr)),
            in_specs=[pl.BlockSpec((1, tr, hw), lambda s, i, cr: (s, i, cr[0])), pl.BlockSpec((1, tr, hw), lambda s, i, cr: (s, i, 0))],
            out_specs=pl.BlockSpec((1, tr, hw), lambda s, i, cr: (s, i, 0)),
        ),
        out_shape=jax.ShapeDtypeStruct((N_CHIPS, r, hw), BF16), compiler_params=_cparams(("parallel", "parallel")), name=name,
    )(c_idx, p4, recv)


def _sum_slots(x, own, sel, name, out_cols=None):
    s, r, c = x.shape
    tr = 128 if r % 128 == 0 else r

    def body(sel_ref, x_ref, own_ref, o_ref):
        mine = own_ref[0].astype(F32)
        acc = jnp.zeros((tr, c), F32)
        for k in range(s):
            acc = acc + jnp.where(sel_ref[0] == k, mine, x_ref[k].astype(F32))
        o_ref[...] = acc

    return pl.pallas_call(
        body,
        grid_spec=pltpu.PrefetchScalarGridSpec(
            num_scalar_prefetch=1, grid=(pl.cdiv(r, tr),),
            in_specs=[pl.BlockSpec((s, tr, c), lambda i, sr: (0, i, 0)), pl.BlockSpec((1, tr, c), lambda i, sr: (sr[1], i, 0))],
            out_specs=pl.BlockSpec((tr, c), lambda i, sr: (i, sr[2])),
        ),
        out_shape=jax.ShapeDtypeStruct((r, out_cols or c), F32), compiler_params=_cparams(("parallel",)), name=name,
    )(sel, x, own)


def _half_cols(width, hc):
    hw = width // 2
    assert hw % LANE == 0
    return pl.ds(pl.multiple_of(hc * hw, LANE), hw)


def _remote(src, dst, ssem, rsem, k, to):
    return pltpu.make_async_remote_copy(src_ref=src, dst_ref=dst, send_sem=ssem.at[k], recv_sem=rsem.at[k], device_id=to,
                                        device_id_type=MESH)


def _gather_job(bufs, mid_at=0.5):
    def part(o, a, slot, hc):
        return o[a].at[slot, :, _half_cols(bufs[a].shape[2], hc)]

    def first(ins, o, fresh, ssem, rsem):
        x, y, c, chips = _place()
        for a in range(len(bufs)):
            mine = part(o, a, 2 * x + y, c)
            for j, chip in enumerate(chips):
                _remote(mine, mine, ssem, rsem, 6 * a + j, (chip[0], chip[1], c)).start()

    def mid(ins, o, fresh, ssem, rsem):
        x, y, c, chips = _place()
        for a in range(len(bufs)):
            for j, chip in enumerate(chips):
                got = part(o, a, 2 * chip[0] + chip[1], c)
                _remote(got, got, ssem, rsem, 6 * a + j, (x, y, c)).wait_recv()
                _remote(got, got, ssem, rsem, 6 * a + 3 + j, (x, y, 1 - c)).start()

    def last(ins, o, fresh, ssem, rsem):
        x, y, c, chips = _place()
        for a in range(len(bufs)):
            for j, chip in enumerate(chips):
                got = part(o, a, 2 * chip[0] + chip[1], 1 - c)
                _remote(got, got, ssem, rsem, 6 * a + 3 + j, (x, y, c)).wait_recv()
        for a in range(len(bufs)):
            mine = part(o, a, 2 * x + y, c)
            for j, chip in enumerate(chips):
                _remote(mine, mine, ssem, rsem, 6 * a + j, (x, y, c)).wait_send()
                passed = part(o, a, 2 * chip[0] + chip[1], c)
                _remote(passed, passed, ssem, rsem, 6 * a + 3 + j, (x, y, c)).wait_send()

    return _Job([], bufs, [], 6 * len(bufs), first, mid, last, mid_at)


def _swap_job(p4s):
    def pairs(ins, fresh, c):
        return [(a, s, ins[a].at[s, :, _half_cols(p4s[a].shape[2], 1 - c)], fresh[a].at[s])
                for a in range(len(p4s)) for s in range(N_CHIPS)]

    def first(ins, inout, fresh, ssem, rsem):
        x, y, c, _ = _place()
        for a, s, src, dst in pairs(ins, fresh, c):
            _remote(src, dst, ssem, rsem, N_CHIPS * a + s, (x, y, 1 - c)).start()

    def last(ins, inout, fresh, ssem, rsem):
        x, y, c, _ = _place()
        for a, s, src, dst in pairs(ins, fresh, c):
            _remote(src, dst, ssem, rsem, N_CHIPS * a + s, (x, y, 1 - c)).wait()

    fresh = [jax.ShapeDtypeStruct(p.shape[:2] + (p.shape[2] // 2,), p.dtype) for p in p4s]
    return _Job(p4s, [], fresh, N_CHIPS * len(p4s), first, None, last)


def _scatter_job(parts):
    def first(ins, inout, fresh, ssem, rsem):
        x, y, c, chips = _place()
        for a in range(len(parts)):
            for j, chip in enumerate(chips):
                _remote(ins[a].at[2 * chip[0] + chip[1]], fresh[a].at[2 * x + y], ssem, rsem, 3 * a + j, (chip[0], chip[1], c)).start()

    def last(ins, inout, fresh, ssem, rsem):
        x, y, c, chips = _place()
        for a in range(len(parts)):
            for j, chip in enumerate(chips):
                slot = 2 * chip[0] + chip[1]
                _remote(ins[a].at[slot], fresh[a].at[slot], ssem, rsem, 3 * a + j, (x, y, c)).wait()

    return _Job(parts, [], [jax.ShapeDtypeStruct(p.shape, p.dtype) for p in parts], 3 * len(parts), first, None, last)


def _share_job(gs):
    def halves(o, a, c):
        width = gs[a].shape[1]
        return o[a].at[:, _half_cols(width, c)], o[a].at[:, _half_cols(width, 1 - c)]

    def first(ins, o, fresh, ssem, rsem):
        x, y, c, _ = _place()
        for a in range(len(gs)):
            mine, _ = halves(o, a, c)
            _remote(mine, mine, ssem, rsem, a, (x, y, 1 - c)).start()

    def last(ins, o, fresh, ssem, rsem):
        x, y, c, _ = _place()
        for a in range(len(gs)):
            mine, theirs = halves(o, a, c)
            _remote(mine, theirs, ssem, rsem, a, (x, y, 1 - c)).wait()

    return _Job([], gs, [], len(gs), first, None, last)


def _gather_all_job(buf):
    def peers():
        x, y, c, _ = _place()
        flips = [(fx, fy, fc) for fx in (0, 1) for fy in (0, 1) for fc in (0, 1)][1:]
        return (x, y, c), [((1 - x) if fx else x, (1 - y) if fy else y, (1 - c) if fc else c) for fx, fy, fc in flips]

    def first(ins, inout, fresh, ssem, rsem):
        (x, y, c), others = peers()
        for k, peer in enumerate(others):
            _remote(ins[0], fresh[0].at[4 * x + 2 * y + c], ssem, rsem, k, peer).start()

    def last(ins, inout, fresh, ssem, rsem):
        me, others = peers()
        for k, peer in enumerate(others):
            _remote(ins[0], fresh[0].at[4 * peer[0] + 2 * peer[1] + peer[2]], ssem, rsem, k, me).wait()

    return _Job([buf], [], [jax.ShapeDtypeStruct((N_DEV,) + buf.shape, buf.dtype)], N_DEV - 1, first, None, last)


class _SemView:
    def __init__(self, sems, off):
        self.sems, self.off = sems, off

    @property
    def at(self):
        return self

    def __getitem__(self, k):
        return self.sems.at[k + self.off]


def _join(jobs):
    spans, pos = [], [0, 0, 0, 0]
    for j in jobs:
        nxt = [pos[0] + len(j.ins), pos[1] + len(j.inout), pos[2] + len(j.fresh), pos[3] + j.nsem]
        spans.append((pos, nxt))
        pos = nxt

    def hook(which):
        fns = [getattr(j, which) for j in jobs]
        if all(f is None for f in fns):
            return None

        def run(ins, inout, fresh, ssem, rsem):
            for fn, (lo, hi) in zip(fns, spans):
                if fn is not None:
                    fn(ins[lo[0]:hi[0]], inout[lo[1]:hi[1]], fresh[lo[2]:hi[2]], _SemView(ssem, lo[3]), _SemView(rsem, lo[3]))

        return run

    mids = [j.mid_at for j in jobs if j.mid is not None]
    joined = _Job([a for j in jobs for a in j.ins], [a for j in jobs for a in j.inout], [a for j in jobs for a in j.fresh],
                  pos[3], hook("first"), hook("mid"), hook("last"), max(mids) if mids else 0.5)
    n_io = pos[1]

    def split(moved):
        return [list(moved[lo[1]:hi[1]]) + list(moved[n_io + lo[2]:n_io + hi[2]]) for lo, hi in spans]

    return joined, split


def _carrying(stages, call):
    stages = [s for s in stages if s is not None]
    if not stages:
        return call(None)
    job, split = _join([s[0] for s in stages])
    out, moved = call(job)
    for (_, done), part in zip(stages, split(moved)):
        done(part)
    return out


def _layer_forward(x, h, w_in_t, rest, sm, g_next, d, stages=None):
    stages = stages or {}
    proj = _carrying([stages.get("proj")], lambda job: _matmul(h, w_in_t, "nt", BF16, "proj_fwd", n=7 * d, tn_cap=1792, job=job))
    f_t = _matmul(w_in_t[7 * d:], h, "nt", F32, "forget_fwd", tn_cap=1024)
    c_t = _fox_prep(f_t, sm["b_f"], "fox_prep")
    o, lse = _carrying([stages.get("attn")], lambda job: _attn_fwd(proj, c_t, d, "attn_fwd", job=job))
    wts = rest()
    merged = _mix_fwd(proj, o, sm["wm"], sm["bs"], sm["g_v"], d, "mix_fwd")
    z, x1, h2 = _matmul(merged, wts["w_out"], "nn", F32, "out_fwd", norms=(x, sm["g_post"], sm["g_fpre"]))
    a = _carrying([stages.get("gate")], lambda job: _matmul(h2, wts["w_g_t"], "nt", BF16, "gate_fwd", tn_cap=1408, job=job))
    b = _carrying([stages.get("up")], lambda job: _matmul(h2, wts["w_u_t"], "nt", BF16, "up_fwd", tn_cap=1408, job=job))
    mm = _swiglu_fwd(a, b, "swiglu_fwd")
    z2, x_out, h_out = _carrying([stages.get("down")], lambda job: _matmul(
        mm, wts["w_d"], "nn", F32, "down_fwd", norms=(x1, sm["g_fpost"], g_next), job=job))
    return dict(x=x, h=h, proj=proj, f_t=f_t, c_t=c_t, o=o, lse=lse, merged=merged, z=z, x1=x1,
                h2=h2, a=a, b=b, mm=mm, z2=z2, x_out=x_out, h_out=h_out)


class _GradExchange:
    def __init__(self, pay, keys, c_idx, chip):
        self.keys = list(keys)
        self.p4 = [pay[k].reshape(N_CHIPS, pay[k].shape[1] // N_CHIPS, pay[k].shape[2]) for k in self.keys]
        self.c_idx = c_idx
        self.sel = jnp.stack([chip, chip, c_idx[0]]).astype(jnp.int32)
        self.done = 0

    def _after_swap(self, landed):
        self.parts = [_add_half(p, r, self.c_idx, "add_sibling") for p, r in zip(self.p4, landed)]
        self.done = 1

    def _after_scatter(self, landed):
        self.g = [_sum_slots(got, sent, self.sel, "sum_chips", out_cols=p.shape[2])
                  for got, sent, p in zip(landed, self.parts, self.p4)]
        self.done = 2

    def _after_share(self, moved):
        self.g = list(moved)
        self.done = 3

    def stage(self):
        if self.done == 0:
            return _swap_job(self.p4), self._after_swap
        if self.done == 1:
            return _scatter_job(self.parts), self._after_scatter
        if self.done == 2:
            return _share_job(self.g), self._after_share
        return None

    def run(self):
        for name in ("swap_grads", "scatter_grads", "share_grads")[self.done:]:
            job, done = self.stage()
            done(_run_job(job, name))

    def grads(self):
        return dict(zip(self.keys, self.g))


EARLY_KEYS = ("w_d", "w_g", "w_u", "w_out")


def _layer_backward(dz2, dx2, sv, wts, sm, d, c_idx, chip, carried=(), split_own=False, small_stage=None):
    t = dx2.shape[0]
    heads = d // LANE
    ff = wts["w_d"].shape[0]
    in_w = 7 * d + heads
    g, pay = {}, {}
    carried = list(carried)

    def payload(key, a, b, rows, row0, name, extra=()):
        def call(job):
            return _matmul(a, b, "tn", BF16, name, slab=((1, rows, d), 0, row0), into=pay.get(key), job=job, tm_cap=1408,
                           tn_cap=1024, tk_cap=1024)
        pay[key] = _carrying(list(extra), call)

    def nxt(*exchanges):
        return [ex.stage() for ex in exchanges]

    dm = _carrying(nxt(*carried), lambda job: _matmul(dz2, wts["w_d"], "nt", BF16, "down_bwd_x", tn_cap=1408, tk_cap=1024, job=job))
    payload("w_d", sv["mm"], dz2, ff, 0, "down_bwd_w")
    da, db = _swiglu_bwd(sv["a"], sv["b"], dm, "swiglu_bwd")
    dh2 = _matmul_pieces([(da, wts["w_g_t"], 0), (db, wts["w_u_t"], 0)], None, "gu_bwd_x", tk=_tile(ff, 1408))
    payload("w_g", da, sv["h2"], ff, 0, "gate_bwd_w")
    payload("w_u", db, sv["h2"], ff, 0, "up_bwd_w")
    dx1, dz, g["g_fpre"], g["g_post"] = _norm_bwd(dx2, (dh2, sv["x1"], sm["g_fpre"]), (sv["z"], sm["g_post"]), "norm_bwd_mid")
    dmerged = _matmul(dz, wts["w_out"], "nt", F32, "out_bwd_x", tk_cap=1024)
    payload("w_out", sv["merged"], dz, d, 0, "out_bwd_w")
    early = [_GradExchange(pay, EARLY_KEYS, c_idx, chip)] if split_own else []
    d_uv, d_g, do, g["w_s"], g["b_s"], g["g_v"] = _carrying(nxt(*early), lambda job: _mix_bwd(
        dmerged, sv["proj"], sv["o"], sm["wm"], sm["wm_t"], sm["bs"], sm["g_v"], d, "mix_bwd", job=job))
    extra = [small_stage(g)] if small_stage is not None else []
    attn_args = (sv["proj"], do, sv["o"], sv["lse"], sv["c_t"], d)
    dq, dk, dv, dc_q, dc_k = _carrying(nxt(*carried) + extra, lambda job: _attn_bwd(*attn_args, "attn_bwd", job=job))
    df_t, g["b_f"] = _fox_bwd(dc_q, dc_k, sv["f_t"], sm["b_f"], "fox_bwd")
    df_b = df_t.astype(BF16)
    pieces = [(d_uv, COL_U), (dq, COL_Q), (dk, COL_K), (dv, COL_VA), (d_g, COL_GA)]
    pay["w_in"] = _carrying(nxt(*carried, *early), lambda job: _matmul_rows([p for p, _ in pieces], sv["h"], in_w,
                                                                           "proj_bwd_w", job=job))
    w_f_rows = _carrying(nxt(*early), lambda job: _matmul(df_b, sv["h"], "nn", BF16, "forget_bwd_w", tk_cap=1024, job=job))
    pay["w_in"] = lax.dynamic_update_slice(pay["w_in"], w_f_rows[None], (0, 7 * d, 0))
    late = _GradExchange(pay, [k for k in ("w_in",) + EARLY_KEYS if not (split_own and k in EARLY_KEYS)], c_idx, chip)
    mine = [late] if split_own else []
    dh_f = _carrying(nxt(*mine), lambda job: _matmul(df_b, wts["w_in_t"][7 * d:], "tn", F32, "forget_bwd_x", job=job))
    ops = [(p, wts["w_in_t"], col * d) for p, col in pieces]
    dh = _carrying(nxt(*mine), lambda job: _matmul_pieces(ops, dh_f, "proj_bwd_x", job=job, tk=_tile(d, 1024)))
    return dh, dx1, g, early + [late]


def _small_pack(parts):
    flat = jnp.concatenate([p.reshape(-1) for p in parts])
    n = flat.shape[0]
    pad = (-n) % (LANE * LANE)
    return jnp.pad(flat, (0, pad)).reshape(-1, LANE)


def kernel(x, mix_pre_g, w_in, b_forget, sgu_norm_g, w_spatial, b_spatial, w_out, mix_post_g, ffn_pre_g, w_gate, w_up, w_down, ffn_post_g, loss_target, m_mix_pre_g, m_w_in, m_b_forget, m_sgu_norm_g, m_w_spatial, m_b_spatial, m_w_out, m_mix_post_g, m_ffn_pre_g, m_w_gate, m_w_up, m_w_down, m_ffn_post_g, v_mix_pre_g, v_w_in, v_b_forget, v_sgu_norm_g, v_w_spatial, v_b_spatial, v_w_out, v_mix_post_g, v_ffn_pre_g, v_w_gate, v_w_up, v_w_down, v_ffn_post_g):
    depth, d = mix_pre_g.shape
    assert depth == 2, "core c of a chip owns layer c"
    heads = d // LANE
    t = x.shape[1]
    ff = w_down.shape[1] * N_CHIPS
    in_w = w_in.shape[2] * N_CHIPS
    assert in_w == 7 * d + heads
    xs = x.reshape(t, d)
    target = loss_target.reshape(t, d)
    c_idx = lax.axis_index("c").astype(jnp.int32).reshape(1)
    chip = 2 * lax.axis_index("x") + lax.axis_index("y")
    dev = 2 * chip + lax.axis_index("c")

    def in_view(w):
        return jnp.transpose(w, (2, 0, 1))

    def gu_view(w):
        return jnp.transpose(w, (0, 2, 1))

    own = [jnp.transpose(in_view(w_in).astype(BF16), (1, 0, 2)), w_out.astype(BF16), gu_view(w_gate).astype(BF16),
           gu_view(w_up).astype(BF16), w_down.astype(BF16)]
    bufs = [[lax.dynamic_update_slice(lax.empty((N_CHIPS,) + o.shape[1:], BF16), o[l][None], (chip, 0, 0)) for o in own]
            for l in range(depth)]
    first_in = _run_job(_gather_job([bufs[0][0]]), "gather_first")[0]

    def weights(g_in, g_out, g_g, g_u, g_d):
        return dict(w_in_t=g_in.reshape(in_w, d), w_out=g_out.reshape(d, d), w_g_t=g_g.reshape(ff, d),
                    w_u_t=g_u.reshape(ff, d), w_d=g_d.reshape(ff, d))

    tril = jnp.tril(jnp.ones((LANE, LANE), bool))
    smalls = []
    for l in range(depth):
        wm = jnp.where(tril[None], w_spatial[l], 0.0).astype(BF16)
        smalls.append(dict(
            b_f=b_forget[l].reshape(heads, 1), wm=wm, wm_t=jnp.swapaxes(wm, 1, 2), bs=b_spatial[l].reshape(heads, LANE, 1),
            g_v=sgu_norm_g[l].reshape(1, d), g_pre=mix_pre_g[l].reshape(1, d), g_post=mix_post_g[l].reshape(1, d),
            g_fpre=ffn_pre_g[l].reshape(1, d), g_fpost=ffn_post_g[l].reshape(1, d)))

    wts, later = [], {}

    def keep(key):
        def done(moved):
            later[key] = list(moved)
        return done

    def rest_first():
        wts.append(weights(first_in, *later["rest0"]))
        return wts[0]

    stages = dict(proj=(_gather_job(bufs[0][1:], mid_at=1.0), keep("rest0")),
                  attn=(_gather_job(bufs[1][0:2], mid_at=0.7), keep("in_out1")),
                  gate=(_gather_job(bufs[1][2:3], mid_at=1.0), keep("g1")), up=(_gather_job(bufs[1][3:4], mid_at=1.0), keep("u1")),
                  down=(_gather_job(bufs[1][4:5], mid_at=1.0), keep("d1")))
    h = _norm_fwd(xs, None, None, smalls[0]["g_pre"], "norm_first")
    g_after = [smalls[min(l + 1, depth - 1)]["g_pre"] for l in range(depth)]
    saved = [_layer_forward(xs, h, first_in.reshape(in_w, d), rest_first, smalls[0], g_after[0], d, stages)]
    wts.append(weights(*later["in_out1"], later["g1"][0], later["u1"][0], later["d1"][0]))
    for l in range(1, depth):
        saved.append(_layer_forward(saved[l - 1]["x_out"], saved[l - 1]["h_out"], wts[l]["w_in_t"], lambda l=l: wts[l], smalls[l],
                                    g_after[l], d))
    dy, loss_part = _loss_grad(saved[-1]["x_out"], target, "loss")
    loss = lax.psum(jnp.sum(loss_part), ("x", "y", "c"))

    small_shapes = dict(g_pre=(d,), b_f=(heads,), g_v=(d,), w_s=w_spatial.shape[1:], b_s=b_spatial.shape[1:], g_post=(d,),
                        g_fpre=(d,), g_fpost=(d,))
    late_entries = [(0, "g_pre"), (0, "b_f")]
    early_entries = [(l, n) for l in reversed(range(depth)) for n in small_shapes if (l, n) not in late_entries]
    dev_sel = jnp.stack([dev, jnp.zeros_like(dev), jnp.zeros_like(dev)]).astype(jnp.int32)
    small_sum = {}

    def small_exchange(entries, values):
        packed = _small_pack([values[e].reshape(-1) for e in entries])

        def done(moved):
            total = _sum_slots(moved[0], packed[None], dev_sel, "sum_small").reshape(-1)
            off = 0
            for e in entries:
                n = math.prod(small_shapes[e[1]])
                small_sum[e] = total[off:off + n].reshape(small_shapes[e[1]])
                off += n

        return _gather_all_job(packed), done

    grads = [None] * depth
    exchanges = [None] * depth
    dx2 = dy
    dz2, g_fpost = _norm_bwd(dx2, None, (saved[depth - 1]["z2"], smalls[depth - 1]["g_fpost"]), "norm_bwd_top")
    for l in reversed(range(depth)):
        last = l == 0

        def small_stage(g, l=l, g_fpost=g_fpost):
            known = {(k, n): grads[k][n] for k in range(l + 1, depth) for n in small_shapes}
            known.update({(l, n): g[n] for n in g})
            known[(l, "g_fpost")] = g_fpost
            return small_exchange(early_entries, known)

        carried = [ex for k in range(l + 1, depth) for ex in exchanges[k]]
        dh, dx1, g, exchanges[l] = _layer_backward(dz2, dx2, saved[l], wts[l], smalls[l], d, c_idx, chip, carried=carried,
                                                    split_own=last, small_stage=small_stage if last else None)
        g["g_fpost"] = g_fpost
        if l > 0:
            dx2, dz2, g["g_pre"], g_fpost = _norm_bwd(dx1, (dh, saved[l]["x"], smalls[l]["g_pre"]),
                                                       (saved[l - 1]["z2"], smalls[l - 1]["g_fpost"]), "norm_bwd_between")
        else:
            grad_x, g["g_pre"] = _norm_bwd(dx1, (dh, saved[l]["x"], smalls[l]["g_pre"]), None, "norm_bwd_bottom")
        grads[l] = g
    job, done = small_exchange(late_entries, {(0, n): grads[0][n] for n in ("g_pre", "b_f")})
    done(_run_job(job, "gather_small"))
    big = [{} for _ in range(depth)]
    for l in range(depth):
        for ex in exchanges[l]:
            ex.run()
            big[l].update(ex.grads())
    small_grads = {n: jnp.stack([small_sum[(l, n)] for l in range(depth)]) for n in small_shapes}

    def adam_small(w, g, m, v):
        shp = w.shape
        if w.ndim >= 3 and shp[-1] >= LANE:
            two = (math.prod(shp[:-1]), shp[-1])
        else:
            two = (1, math.prod(shp)) if math.prod(shp) < LANE else (math.prod(shp) // LANE, LANE)
        outs = _adamw(w.reshape(two), g.reshape(two), m.reshape(two), v.reshape(two), "adamw")
        return [g] + [o.reshape(shp) for o in outs]

    def adam_in(w, m, v):
        outs = _adamw_interleaved(in_view(w), big[0]["w_in"], big[1]["w_in"], in_view(m), in_view(v), "adamw_in")
        return [jnp.transpose(o, (1, 2, 0)) for o in outs]

    def adam_gu(k, w, m, v):
        outs = _adamw_layers(gu_view(w), big[0][k], big[1][k], gu_view(m), gu_view(v), "adamw_layers")
        return [jnp.transpose(o, (0, 2, 1)) for o in outs]

    def adam_rows(k, w, m, v):
        return _adamw_layers(w, big[0][k], big[1][k], m, v, "adamw_layers")

    results = [
        adam_small(mix_pre_g, small_grads["g_pre"], m_mix_pre_g, v_mix_pre_g),
        adam_in(w_in, m_w_in, v_w_in),
        adam_small(b_forget, small_grads["b_f"], m_b_forget, v_b_forget),
        adam_small(sgu_norm_g, small_grads["g_v"], m_sgu_norm_g, v_sgu_norm_g),
        adam_small(w_spatial, small_grads["w_s"], m_w_spatial, v_w_spatial),
        adam_small(b_spatial, small_grads["b_s"], m_b_spatial, v_b_spatial),
        adam_rows("w_out", w_out, m_w_out, v_w_out),
        adam_small(mix_post_g, small_grads["g_post"], m_mix_post_g, v_mix_post_g),
        adam_small(ffn_pre_g, small_grads["g_fpre"], m_ffn_pre_g, v_ffn_pre_g),
        adam_gu("w_g", w_gate, m_w_gate, v_w_gate),
        adam_gu("w_u", w_up, m_w_up, v_w_up),
        adam_rows("w_d", w_down, m_w_down, v_w_down),
        adam_small(ffn_post_g, small_grads["g_fpost"], m_ffn_post_g, v_ffn_post_g),
    ]
    gs, deltas, new_ms, new_vs = zip(*results)
    return (loss, grad_x.reshape(x.shape), *gs, *deltas, *new_ms, *new_vs)
```

```python
import functools
import math

import jax
import jax.numpy as jnp
from jax import lax
from jax.experimental import pallas as pl
from jax.experimental.pallas import tpu as pltpu

F32 = jnp.float32
BF16 = jnp.bfloat16

EPS = 1e-6
LANE = 128
SUBLANE = 8
N_CHIPS = 4
N_DEV = 8
VMEM_LIMIT = 48 * 1024 * 1024
MESH = pl.DeviceIdType.MESH

ADAM_LR = 0.001
ADAM_B1 = 0.9
ADAM_B2 = 0.999
ADAM_EPS = 1e-08
ADAM_WD = 0.01
ADAM_STEP = 10
ADAM_C1 = 1.0 / (1.0 - ADAM_B1 ** ADAM_STEP)
ADAM_C2 = 1.0 / (1.0 - ADAM_B2 ** ADAM_STEP)

GELU_K = math.sqrt(2.0 / math.pi)
GELU_A = 0.044715
NEG = -1e30
LOG2E = 1.4426950408889634
LN2 = 0.6931471805599453

COL_U, COL_V, COL_Q, COL_K, COL_VA, COL_GA, COL_GB, COL_F = range(8)


def _cparams(sem=None):
    return pltpu.CompilerParams(dimension_semantics=sem, vmem_limit_bytes=VMEM_LIMIT)


def _tile(n, cap):
    best = None
    for t in range(LANE, min(n, cap) + 1, LANE):
        if n % t == 0:
            best = t
    return best if best is not None else n


def _rows(n, cap):
    best = None
    for t in range(SUBLANE, min(n, cap) + 1, SUBLANE):
        if n % t == 0:
            best = t
    return best if best is not None else n


def _gelu_and_grad(x):
    x2 = x * x
    t = jnp.tanh(GELU_K * (x + GELU_A * x2 * x))
    g = 0.5 * x * (1.0 + t)
    dg = 0.5 * (1.0 + t) + 0.5 * x * (1.0 - t * t) * (GELU_K * (1.0 + 3.0 * GELU_A * x2))
    return g, dg


def _sigmoid(x):
    return 1.0 / (1.0 + jnp.exp(-x))


def _sum8(v):
    n, d = v.shape
    return v.reshape(n // SUBLANE, SUBLANE, d).sum(axis=0)


def _nt_dot(a, b):
    return lax.dot_general(a, b, (((1,), (1,)), ((), ())), preferred_element_type=F32)


_HBM = pl.BlockSpec(memory_space=pl.ANY)


def _place():
    x, y, c = lax.axis_index("x"), lax.axis_index("y"), lax.axis_index("c")
    chips = [(1 - x, y), (x, 1 - y), (1 - x, 1 - y)]
    return x, y, c, chips


class _Job:
    def __init__(self, ins, inout, fresh, nsem, first, mid, last, mid_at=0.5):
        self.ins, self.inout, self.fresh, self.nsem = list(ins), list(inout), list(fresh), nsem
        self.first, self.mid, self.last, self.mid_at = first, mid, last, mid_at


def _call(body, *, grid, in_specs, out_specs, out_shape, scratch_shapes, dims, name, args, aliases=None, job=None):
    single = not isinstance(out_shape, (list, tuple))
    out_specs = [out_specs] if single else list(out_specs)
    out_shape = [out_shape] if single else list(out_shape)
    aliases = dict(aliases or {})
    if job is None:
        outs = pl.pallas_call(body, grid=grid, in_specs=in_specs, out_specs=out_specs, out_shape=out_shape,
                              scratch_shapes=scratch_shapes, input_output_aliases=aliases, compiler_params=_cparams(dims),
                              name=name)(*args)
        return (outs[0] if single else outs), []
    n_in, n_out, n_scr = len(args), len(out_shape), len(scratch_shapes)
    n_ji, n_jio, n_jf = len(job.ins), len(job.inout), len(job.fresh)
    total = math.prod(grid)

    def wrapped(*refs):
        host_in = refs[:n_in]
        pos = n_in
        j_ins = refs[pos:pos + n_ji]
        pos += n_ji + n_jio
        host_out = refs[pos:pos + n_out]
        pos += n_out
        j_inout = refs[pos:pos + n_jio]
        pos += n_jio
        j_fresh = refs[pos:pos + n_jf]
        pos += n_jf
        host_scr = refs[pos:pos + n_scr]
        ssem, rsem = refs[pos + n_scr:]
        flat = 0
        for ax, size in enumerate(grid):
            flat = flat * size + pl.program_id(ax)

        def hook(fn, at):
            if fn is not None:
                @pl.when(flat == at)
                def _():
                    fn(j_ins, j_inout, j_fresh, ssem, rsem)

        hook(job.first, 0)
        body(*host_in, *host_out, *host_scr)
        hook(job.mid, min(int(total * job.mid_at), total - 1))
        hook(job.last, total - 1)

    for k in range(n_jio):
        aliases[n_in + n_ji + k] = n_out + k
    outs = pl.pallas_call(
        wrapped, grid=grid,
        in_specs=list(in_specs) + [_HBM] * (n_ji + n_jio),
        out_specs=out_specs + [_HBM] * (n_jio + n_jf),
        out_shape=out_shape + [jax.ShapeDtypeStruct(b.shape, b.dtype) for b in job.inout] + list(job.fresh),
        scratch_shapes=list(scratch_shapes) + [pltpu.SemaphoreType.DMA((job.nsem,)), pltpu.SemaphoreType.DMA((job.nsem,))],
        input_output_aliases=aliases, compiler_params=_cparams(tuple("arbitrary" for _ in grid)), name=name,
    )(*args, *job.ins, *job.inout)
    host = outs[:n_out]
    return (host[0] if single else host), outs[n_out:]


def _run_job(job, name):
    n_ji, n_jio, n_jf = len(job.ins), len(job.inout), len(job.fresh)

    def body(*refs):
        j_ins = refs[:n_ji]
        pos = n_ji + n_jio
        j_inout = refs[pos:pos + n_jio]
        j_fresh = refs[pos + n_jio:pos + n_jio + n_jf]
        ssem, rsem = refs[pos + n_jio + n_jf:]
        for fn in (job.first, job.mid, job.last):
            if fn is not None:
                fn(j_ins, j_inout, j_fresh, ssem, rsem)

    return pl.pallas_call(
        body, in_specs=[_HBM] * (n_ji + n_jio), out_specs=[_HBM] * (n_jio + n_jf),
        out_shape=[jax.ShapeDtypeStruct(b.shape, b.dtype) for b in job.inout] + list(job.fresh),
        scratch_shapes=[pltpu.SemaphoreType.DMA((job.nsem,)), pltpu.SemaphoreType.DMA((job.nsem,))],
        input_output_aliases={n_ji + k: k for k in range(n_jio)}, name=name,
    )(*job.ins, *job.inout)


_DIMS ={"nn": ((1,), (0,)), "nt": ((1,), (1,)), "tn": ((0,), (0,))}


def _matmul(a, b, mode, out_dtype, name, n=None, slab=None, into=None, job=None, norms=None, silu_of=None, tm_cap=512,
            tn_cap=2048, tk_cap=1408):
    if mode == "nn":
        (m, k), (k2, nn_) = a.shape, b.shape
    elif mode == "nt":
        (m, k), (nn_, k2) = a.shape, b.shape
    else:
        (k, m), (k2, nn_) = a.shape, b.shape
    n = nn_ if n is None else n
    assert k == k2, (a.shape, b.shape, mode)
    tm, tn, tk = _tile(m, tm_cap), _tile(n, tn_cap), _tile(k, tk_cap)
    if slab is not None and slab[2]:
        tm = _tile(math.gcd(m, slab[2]), tm_cap)
    nk = k // tk
    if mode == "tn":
        a_spec = pl.BlockSpec((tk, tm), lambda j, i, kk, *_: (kk, i))
    else:
        a_spec = pl.BlockSpec((tm, tk), lambda j, i, kk, *_: (i, kk))
    if mode == "nt":
        b_spec = pl.BlockSpec((tn, tk), lambda j, i, kk, *_: (j, kk))
    else:
        b_spec = pl.BlockSpec((tk, tn), lambda j, i, kk, *_: (kk, j))
    dims = (_DIMS[mode], ((), ()))
    aliased = into is not None

    n_a = 1 if silu_of is None else 2
    n_in = n_a + 1 + aliased + (3 if norms is not None else 0)
    n_main = 1 + (2 if norms is not None else 0)

    def finish(refs, z):
        refs[n_in][...] = z.astype(out_dtype).reshape(refs[n_in].shape)
        if norms is not None:
            x_ref, gp_ref, gn_ref = refs[n_in - 3:n_in]
            r = lax.rsqrt(jnp.mean(z * z, axis=-1, keepdims=True) + EPS)
            xn = x_ref[...] + z * r * gp_ref[...]
            refs[n_in + 1][...] = xn
            r2 = lax.rsqrt(jnp.mean(xn * xn, axis=-1, keepdims=True) + EPS)
            refs[n_in + 2][...] = (xn * r2 * gn_ref[...]).astype(BF16)

    def body(*refs):
        lhs = refs[0][...]
        if silu_of is not None:
            gv = refs[1][...].astype(F32)
            lhs = (gv * _sigmoid(gv) * lhs.astype(F32)).astype(BF16)
            refs[n_in + n_main][...] = lhs
        p = lax.dot_general(lhs, refs[n_a][...], dims, preferred_element_type=F32)
        if nk == 1:
            finish(refs, p)
        else:
            acc = refs[-1]
            kk = pl.program_id(2)

            @pl.when(kk == 0)
            def _():
                acc[...] = p

            @pl.when(kk > 0)
            def _():
                acc[...] += p

            @pl.when(kk == nk - 1)
            def _():
                finish(refs, acc[...])

    if slab is None:
        out_spec = pl.BlockSpec((tm, tn), lambda j, i, kk: (i, j))
        out_shape = jax.ShapeDtypeStruct((m, n), out_dtype)
    else:
        shape3, lead, row0 = slab
        assert row0 % tm == 0 and shape3[2] == n
        out_spec = pl.BlockSpec((1, tm, tn), lambda j, i, kk: (lead, row0 // tm + i, j))
        out_shape = jax.ShapeDtypeStruct(shape3, out_dtype)
    in_specs, args = [a_spec, b_spec], [a, b]
    if silu_of is not None:
        assert mode == "nn" and tn == n and slab is None, "the left operand's blocks are written once each"
        in_specs, args = [a_spec, a_spec, b_spec], [a, silu_of, b]
    if aliased:
        in_specs.append(pl.BlockSpec(memory_space=pl.ANY))
        args.append(into)
    if norms is not None:
        assert tn == n and slab is None, "the fused norms need whole rows"
        row = pl.BlockSpec((tm, n), lambda j, i, kk: (i, 0))
        vec = pl.BlockSpec((1, n), lambda j, i, kk: (0, 0))
        in_specs += [row, vec, vec]
        args += list(norms)
        out_spec = [out_spec, row, row]
        out_shape = [out_shape, jax.ShapeDtypeStruct((m, n), F32), jax.ShapeDtypeStruct((m, n), BF16)]
    if silu_of is not None:
        out_spec = (out_spec if isinstance(out_spec, list) else [out_spec]) + [pl.BlockSpec((tm, tk), lambda j, i, kk: (i, kk))]
        out_shape = (out_shape if isinstance(out_shape, list) else [out_shape]) + [jax.ShapeDtypeStruct((m, k), BF16)]
    out, moved = _call(
        body, grid=(n // tn, m // tm, nk), in_specs=in_specs, out_specs=out_spec, out_shape=out_shape,
        scratch_shapes=[pltpu.VMEM((tm, tn), F32)] if nk > 1 else [], dims=("parallel", "parallel", "arbitrary"), name=name,
        args=args, aliases={n_a + 1: 0} if aliased else None, job=job)
    return out if job is None else (out, moved)


def _matmul_rows(pieces, b, rows, name, job=None, tm=1024, tk=1024):
    k, n = b.shape
    tm = math.gcd(tm, *[a.shape[1] for a in pieces])
    tk = _tile(k, tk)
    nk = k // tk
    spans, r0 = [], 0
    for a in pieces:
        assert a.shape[0] == k and a.shape[1] % tm == 0
        spans.append((r0, a.shape[1] // tm))
        r0 += a.shape[1] // tm
    nr = r0
    np_ = len(pieces)

    def body(*refs):
        b_ref, o_ref, acc = refs[np_], refs[np_ + 1], refs[-1]
        r, kk = pl.program_id(0), pl.program_id(1)
        for p, (first, count) in enumerate(spans):
            @pl.when((r >= first) & (r < first + count))
            def _(p=p):
                part = lax.dot_general(refs[p][...], b_ref[...], (_DIMS["tn"], ((), ())), preferred_element_type=F32)

                @pl.when(kk == 0)
                def _():
                    acc[...] = part

                @pl.when(kk > 0)
                def _():
                    acc[...] += part

        @pl.when(kk == nk - 1)
        def _():
            o_ref[0] = acc[...].astype(BF16)

    in_specs = []
    for first, count in spans:
        in_specs.append(pl.BlockSpec((tk, tm), lambda r, kk, f=first, c=count: (
            jnp.where(r < f, 0, jnp.where(r >= f + c, nk - 1, kk)), jnp.clip(r - f, 0, c - 1))))
    in_specs.append(pl.BlockSpec((tk, n), lambda r, kk: (kk, 0)))
    out, moved = _call(
        body, grid=(nr, nk), in_specs=in_specs, out_specs=pl.BlockSpec((1, tm, n), lambda r, kk: (0, r, 0)),
        out_shape=jax.ShapeDtypeStruct((1, rows, n), BF16), scratch_shapes=[pltpu.VMEM((tm, n), F32)],
        dims=("arbitrary", "arbitrary"), name=name, args=list(pieces) + [b], job=job)
    return out if job is None else (out, moved)


def _matmul_pieces(pieces, addend, name, tk, job=None, tm_cap=512):
    m = pieces[0][0].shape[0]
    n = pieces[0][1].shape[1]
    tm = _tile(m, tm_cap)
    spans, s0 = [], 0
    for a, b, row0 in pieces:
        assert a.shape[1] % tk == 0 and row0 % tk == 0 and b.shape[1] == n and a.shape[0] == m
        spans.append((s0, a.shape[1] // tk, row0 // tk))
        s0 += a.shape[1] // tk
    steps = s0
    np_ = len(pieces)
    groups = []
    for (a, b, _), (first, count, brow) in zip(pieces, spans):
        if groups and groups[-1][0] is b and groups[-1][3] + groups[-1][2] == brow:
            groups[-1][2] += count
        else:
            groups.append([b, first, count, brow])
    b_of = []
    for first, count, _ in spans:
        b_of.append(next(k for k, g in enumerate(groups) if g[1] <= first < g[1] + g[2]))
    ng = len(groups)

    nm = m // tm

    def body(*refs):
        o_ref, acc = refs[-2], refs[-1]
        s, i = pl.program_id(0), pl.program_id(1)
        rows = pl.ds(pl.multiple_of(i * tm, tm), tm)

        @pl.when(s == 0)
        def _():
            acc[rows, :] = refs[np_ + ng][...] if addend is not None else jnp.zeros((tm, n), F32)

        for p, (first, count, _) in enumerate(spans):
            @pl.when((s >= first) & (s < first + count))
            def _(p=p):
                acc[rows, :] += jnp.dot(refs[p][...], refs[np_ + b_of[p]][...], preferred_element_type=F32)

        @pl.when(s == steps - 1)
        def _():
            o_ref[...] = acc[rows, :]

    in_specs, args = [], []
    for (a, _, _), (first, count, _) in zip(pieces, spans):
        in_specs.append(pl.BlockSpec((tm, tk), lambda s, i, f=first, c=count: (
            jnp.where(s < f, 0, jnp.where(s >= f + c, nm - 1, i)), jnp.clip(s - f, 0, c - 1))))
        args.append(a)
    for b, first, count, brow in groups:
        in_specs.append(pl.BlockSpec((tk, n), lambda s, i, f=first, c=count, r=brow: (r + jnp.clip(s - f, 0, c - 1), 0)))
        args.append(b)
    if addend is not None:
        in_specs.append(pl.BlockSpec((tm, n), lambda s, i: (jnp.where(s == 0, i, nm - 1), 0)))
        args.append(addend)
    out, moved = _call(
        body, grid=(steps, nm), in_specs=in_specs,
        out_specs=pl.BlockSpec((tm, n), lambda s, i: (jnp.where(s == steps - 1, i, 0), 0)),
        out_shape=jax.ShapeDtypeStruct((m, n), F32), scratch_shapes=[pltpu.VMEM((m, n), F32)],
        dims=("arbitrary", "arbitrary"), name=name, args=args, job=job)
    return out if job is None else (out, moved)


def _norm_fwd(x, z, g_post, g_next, name):
    t, d = x.shape
    tt = _rows(t, 512)
    row = pl.BlockSpec((tt, d), lambda i: (i, 0))
    vec = pl.BlockSpec((1, d), lambda i: (0, 0))

    def body(*refs):
        if z is None:
            x_ref, gn_ref, h_ref = refs
            xn = x_ref[...]
        else:
            x_ref, z_ref, gp_ref, gn_ref, xo_ref, h_ref = refs
            zz = z_ref[...]
            r = lax.rsqrt(jnp.mean(zz * zz, axis=-1, keepdims=True) + EPS)
            xn = x_ref[...] + zz * r * gp_ref[...]
            xo_ref[...] = xn
        r2 = lax.rsqrt(jnp.mean(xn * xn, axis=-1, keepdims=True) + EPS)
        h_ref[...] = (xn * r2 * gn_ref[...]).astype(BF16)

    if z is None:
        return pl.pallas_call(
            body, grid=(t // tt,), in_specs=[row, vec], out_specs=row,
            out_shape=jax.ShapeDtypeStruct((t, d), BF16), compiler_params=_cparams(("parallel",)), name=name,
        )(x, g_next)
    return pl.pallas_call(
        body, grid=(t // tt,), in_specs=[row, row, vec, vec], out_specs=[row, row],
        out_shape=[jax.ShapeDtypeStruct((t, d), F32), jax.ShapeDtypeStruct((t, d), BF16)],
        compiler_params=_cparams(("parallel",)), name=name,
    )(x, z, g_post, g_next)


def _rms_bwd(dy, x, g):
    r = lax.rsqrt(jnp.mean(x * x, axis=-1, keepdims=True) + EPS)
    n = x * r
    dn = dy * g
    dx = r * (dn - n * jnp.mean(dn * n, axis=-1, keepdims=True))
    return dx, dy * n


def _norm_bwd(dres, pre, post, name):
    t, d = dres.shape
    tt = _rows(t, 512)
    nt = t // tt
    row = pl.BlockSpec((tt, d), lambda i: (i, 0))
    vec = pl.BlockSpec((1, d), lambda i: (0, 0))
    has_pre, has_post = pre is not None, post is not None
    n_in = 1 + (3 if has_pre else 0) + (2 if has_post else 0)
    n_out = has_pre + has_post + has_pre + has_post

    def body(*refs):
        ins, outs, scr = refs[:n_in], refs[n_in:n_in + n_out], refs[n_in + n_out:]
        i = pl.program_id(0)
        dx = ins[0][...]
        pos, opos, spos = 1, 0, 0
        accs = []
        if has_pre:
            dh_ref, xa_ref, ga_ref = ins[pos:pos + 3]
            pos += 3
            dxa, dga_t = _rms_bwd(dh_ref[...], xa_ref[...], ga_ref[...])
            dx = dx + dxa
            outs[opos][...] = dx
            opos += 1
            accs.append((scr[spos], dga_t))
            spos += 1
        if has_post:
            zb_ref, gb_ref = ins[pos:pos + 2]
            dz, dgb_t = _rms_bwd(dx, zb_ref[...], gb_ref[...])
            outs[opos][...] = dz.astype(BF16)
            opos += 1
            accs.append((scr[spos], dgb_t))
            spos += 1
        for (acc, val), out in zip(accs, outs[opos:]):
            part = _sum8(val)

            @pl.when(i == 0)
            def _(acc=acc, part=part):
                acc[...] = part

            @pl.when(i > 0)
            def _(acc=acc, part=part):
                acc[...] += part

            @pl.when(i == nt - 1)
            def _(acc=acc, out=out):
                out[...] = jnp.sum(acc[...], axis=0, keepdims=True)

    in_specs, args = [row], [dres]
    out_specs, out_shape = [], []
    if has_pre:
        in_specs += [row, row, vec]
        args += list(pre)
        out_specs.append(row)
        out_shape.append(jax.ShapeDtypeStruct((t, d), F32))
    if has_post:
        in_specs += [row, vec]
        args += list(post)
        out_specs.append(row)
        out_shape.append(jax.ShapeDtypeStruct((t, d), BF16))
    for _ in range(has_pre + has_post):
        out_specs.append(vec)
        out_shape.append(jax.ShapeDtypeStruct((1, d), F32))
    return pl.pallas_call(
        body, grid=(nt,), in_specs=in_specs, out_specs=out_specs, out_shape=out_shape,
        scratch_shapes=[pltpu.VMEM((SUBLANE, d), F32)] * (has_pre + has_post),
        compiler_params=_cparams(("arbitrary",)), name=name,
    )(*args)


def _loss_grad(y, target, name):
    t, d = y.shape
    tt = _rows(t, 512)
    nt = t // tt
    row = pl.BlockSpec((tt, d), lambda i: (i, 0))
    inv_d = 1.0 / d

    def body(y_ref, t_ref, dy_ref, l_ref):
        i = pl.program_id(0)
        diff = y_ref[...] - t_ref[...]
        dy_ref[...] = diff * inv_d
        s8 = _sum8(diff * diff)
        part = s8[:, 0:LANE]
        for k in range(1, d // LANE):
            part = part + s8[:, k * LANE:(k + 1) * LANE]
        part = part * (0.5 * inv_d)

        @pl.when(i == 0)
        def _():
            l_ref[...] = part

        @pl.when(i > 0)
        def _():
            l_ref[...] += part

    return pl.pallas_call(
        body, grid=(nt,), in_specs=[row, row],
        out_specs=[row, pl.BlockSpec((SUBLANE, LANE), lambda i: (0, 0))],
        out_shape=[jax.ShapeDtypeStruct((t, d), F32), jax.ShapeDtypeStruct((SUBLANE, LANE), F32)],
        compiler_params=_cparams(("arbitrary",)), name=name,
    )(y, target)


def _swiglu_bwd(a, b, dm, name):
    t, f = a.shape
    tt = _rows(t, 256)
    blk = pl.BlockSpec((tt, f), lambda i: (i, 0))

    def body(a_ref, b_ref, dm_ref, da_ref, db_ref):
        av = a_ref[...].astype(F32)
        s = _sigmoid(av)
        dv = dm_ref[...].astype(F32)
        da_ref[...] = (dv * b_ref[...].astype(F32) * s * (1.0 + av * (1.0 - s))).astype(BF16)
        db_ref[...] = (dv * av * s).astype(BF16)

    return pl.pallas_call(
        body, grid=(t // tt,), in_specs=[blk, blk, blk], out_specs=[blk, blk],
        out_shape=[jax.ShapeDtypeStruct((t, f), BF16)] * 2, compiler_params=_cparams(("parallel",)), name=name,
    )(a, b, dm)


def _log_sigmoid(x):
    return jnp.minimum(x, 0.0) - jnp.log1p(jnp.exp(-jnp.abs(x)))


def _fox_prep(f_t, b_f, name):
    h, t = f_t.shape

    def body(f_ref, b_ref, c_ref):
        r = lax.broadcasted_iota(jnp.int32, (LANE, LANE), 0)
        c = lax.broadcasted_iota(jnp.int32, (LANE, LANE), 1)
        upper = (r <= c).astype(F32)
        carry = jnp.zeros((h, 1), F32)
        for j in range(t // LANE):
            sl = slice(j * LANE, (j + 1) * LANE)
            lf = _log_sigmoid(f_ref[:, sl] + b_ref[...])
            cs = jnp.dot(lf, upper, precision=lax.Precision.HIGHEST, preferred_element_type=F32) + carry
            c_ref[:, sl] = cs
            carry = cs[:, LANE - 1:LANE]

    return pl.pallas_call(body, out_shape=jax.ShapeDtypeStruct((h, t), F32), compiler_params=_cparams(), name=name)(f_t, b_f)


def _fox_bwd(dc_q, dc_k, f_t, b_f, name):
    h, t = f_t.shape

    def body(dq_ref, dk_ref, f_ref, b_ref, df_ref, db_ref):
        r = lax.broadcasted_iota(jnp.int32, (LANE, LANE), 0)
        c = lax.broadcasted_iota(jnp.int32, (LANE, LANE), 1)
        lower = (r >= c).astype(F32)
        carry = jnp.zeros((h, 1), F32)
        dbsum = jnp.zeros((h, 1), F32)
        for j in reversed(range(t // LANE)):
            sl = slice(j * LANE, (j + 1) * LANE)
            dc = dq_ref[:, sl] - dk_ref[:, sl]
            dl = jnp.dot(dc, lower, precision=lax.Precision.HIGHEST, preferred_element_type=F32) + carry
            carry = dl[:, 0:1]
            df = dl * _sigmoid(-(f_ref[:, sl] + b_ref[...]))
            df_ref[:, sl] = df
            dbsum = dbsum + jnp.sum(df, axis=-1, keepdims=True)
        db_ref[...] = dbsum

    return pl.pallas_call(
        body, out_shape=[jax.ShapeDtypeStruct((h, t), F32), jax.ShapeDtypeStruct((h, 1), F32)],
        compiler_params=_cparams(), name=name,
    )(dc_q, dc_k, f_t, b_f)


ATTN_FWD = (1024, 512)
ATTN_BWD = (512, 512)


def _attn_tiles(t, tiles):
    return _tile(t, tiles[0]), _tile(t, tiles[1])


def _attn_fwd(proj, c_t, d, name, job=None):
    t = proj.shape[0]
    h = d // LANE
    bq, bk = _attn_tiles(t, ATTN_FWD)
    nq, nk, rr = t // bq, t // bk, bq // bk
    qc, kc, vc = COL_Q * h, COL_K * h, COL_VA * h
    qscale = LANE ** -0.5 * LOG2E

    def body(q_ref, k_ref, v_ref, cc_ref, cr_ref, o_ref, lse_ref, kb, vt, ckb, acc):
        i = pl.program_id(1)

        @pl.when(i == 0)
        def _():
            kb[...] = k_ref[...].astype(BF16)
            ckb[...] = jnp.broadcast_to(cc_ref[0] * LOG2E, (t, bq))
            for jn in range(nk):
                vt[jn] = v_ref[jn * bk:(jn + 1) * bk, :].astype(F32).T.astype(BF16)

        q = (q_ref[...].astype(F32) * qscale).astype(BF16)
        cq = cr_ref[0, 0] * LOG2E
        acc[...] = jnp.zeros((LANE, bq), F32)

        def block(j, diag, m_old, l_old):
            off = 0 if diag is None else diag * bk
            w = bq - off
            rows = pl.ds(pl.multiple_of(j * bk, bk), bk)
            s = _nt_dot(kb[rows, :], q[off:, :]) - ckb[rows, off:]
            if diag is not None:
                kk = lax.broadcasted_iota(jnp.int32, (bk, w), 0)
                qq = lax.broadcasted_iota(jnp.int32, (bk, w), 1)
                s = jnp.where(qq >= kk, s, NEG)
            cqs, m_part, l_part = cq[:, off:], m_old[:, off:], l_old[:, off:]
            m_new = jnp.maximum(m_part, jnp.max(s, axis=0, keepdims=True) + cqs)
            p = jnp.exp2(s + (cqs - m_new))
            alpha = jnp.exp2(m_part - m_new)
            l_new = alpha * l_part + jnp.sum(p, axis=0, keepdims=True)
            acc[:, off:] = alpha * acc[:, off:] + jnp.dot(vt[j], p.astype(BF16), preferred_element_type=F32)
            if off:
                m_new = jnp.concatenate([m_old[:, :off], m_new], axis=1)
                l_new = jnp.concatenate([l_old[:, :off], l_new], axis=1)
            return m_new, l_new

        m, l = lax.fori_loop(0, i * rr, lambda j, c: block(j, None, *c),
                             (jnp.full((1, bq), NEG, F32), jnp.zeros((1, bq), F32)))
        for jj in range(rr):
            m, l = block(i * rr + jj, jj, m, l)
        o_ref[...] = (acc[...] / l).T
        lse_ref[0, 0] = m + jnp.log2(l)

    rowq = pl.BlockSpec((1, 1, 1, bq), lambda hh, i: (hh, i, 0, 0))
    outs, moved = _call(
        body, grid=(h, nq),
        in_specs=[
            pl.BlockSpec((bq, LANE), lambda hh, i: (i, qc + hh)),
            pl.BlockSpec((t, LANE), lambda hh, i: (0, kc + hh)),
            pl.BlockSpec((t, LANE), lambda hh, i: (0, vc + hh)),
            pl.BlockSpec((1, t, 1), lambda hh, i: (hh, 0, 0)),
            rowq,
        ],
        out_specs=[pl.BlockSpec((bq, LANE), lambda hh, i: (i, hh)), rowq],
        out_shape=[jax.ShapeDtypeStruct((t, d), F32), jax.ShapeDtypeStruct((h, nq, 1, bq), F32)],
        scratch_shapes=[pltpu.VMEM((t, LANE), BF16), pltpu.VMEM((nk, LANE, bk), BF16), pltpu.VMEM((t, bq), F32),
                        pltpu.VMEM((LANE, bq), F32)],
        dims=("arbitrary", "arbitrary"), name=name,
        args=[proj, proj, proj, c_t.reshape(h, t, 1), c_t.reshape(h, nq, 1, bq)], job=job)
    outs = [outs[0], outs[1].reshape(h, t)]
    return outs if job is None else (outs, moved)


def _attn_bwd(proj, do, o, lse, c_t, d, name, job=None):
    t = proj.shape[0]
    h = d // LANE
    bq, bk = _attn_tiles(t, ATTN_BWD)
    nq, nk, rr = t // bq, t // bk, bq // bk
    qc, kc, vc = COL_Q * h, COL_K * h, COL_VA * h
    scale = LANE ** -0.5

    def body(q_ref, k_ref, v_ref, do_ref, o_ref, lse_ref, cc_ref, cr_ref, dq_ref, dk_ref, dv_ref, dcq_ref, dck_ref,
             kb, kt, vb, ckb, dk_acc, dv_acc, dck_acc, dqt_acc):
        i = pl.program_id(1)

        @pl.when(i == 0)
        def _():
            kb[...] = k_ref[...].astype(BF16)
            vb[...] = v_ref[...].astype(BF16)
            ckb[...] = jnp.broadcast_to(cc_ref[0] * LOG2E, (t, bq))
            for jn in range(nk):
                kt[jn] = k_ref[jn * bk:(jn + 1) * bk, :].astype(F32).T.astype(BF16)
            dk_acc[...] = jnp.zeros((t, LANE), F32)
            dv_acc[...] = jnp.zeros((t, LANE), F32)
            dck_acc[...] = jnp.zeros((t, LANE), F32)

        q = (q_ref[...].astype(F32) * (scale * LOG2E)).astype(BF16)
        dof = do_ref[...]
        dob = dof.astype(BF16)
        delta = jnp.sum((dof * o_ref[...]).T, axis=0, keepdims=True)
        rowb = cr_ref[0, 0] * LOG2E - lse_ref[0, 0]
        dqt_acc[...] = jnp.zeros((LANE, bq), F32)

        def block(j, diag, dcq):
            rows = pl.ds(pl.multiple_of(j * bk, bk), bk)
            p = jnp.exp2(_nt_dot(kb[rows, :], q) - ckb[rows, :] + rowb)
            if diag is not None:
                kk = lax.broadcasted_iota(jnp.int32, (bk, bq), 0)
                qq = lax.broadcasted_iota(jnp.int32, (bk, bq), 1)
                p = jnp.where(qq >= kk + diag * bk, p, 0.0)
            dv_acc[rows, :] += jnp.dot(p.astype(BF16), dob, preferred_element_type=F32)
            ds = p * (_nt_dot(vb[rows, :], dob) - delta)
            dsb = ds.astype(BF16)
            dk_acc[rows, :] += jnp.dot(dsb, q, preferred_element_type=F32)
            dqt_acc[...] += jnp.dot(kt[j], dsb, preferred_element_type=F32)
            part = ds[:, 0:LANE]
            for k in range(1, bq // LANE):
                part = part + ds[:, k * LANE:(k + 1) * LANE]
            dck_acc[rows, :] += part
            return dcq + jnp.sum(ds, axis=0, keepdims=True)

        dcq = lax.fori_loop(0, i * rr, lambda j, c: block(j, None, c), jnp.zeros((1, bq), F32))
        for jj in range(rr):
            dcq = block(i * rr + jj, jj, dcq)
        dq_ref[...] = (dqt_acc[...] * scale).T.astype(BF16)
        dcq_ref[0, 0] = dcq

        @pl.when(i == nq - 1)
        def _():
            dk_ref[...] = (dk_acc[...] * LN2).astype(BF16)
            dv_ref[...] = dv_acc[...].astype(BF16)
            dck_ref[0] = jnp.sum(dck_acc[...], axis=-1, keepdims=True)

    rowq = pl.BlockSpec((1, 1, 1, bq), lambda hh, i: (hh, i, 0, 0))
    blk = pl.BlockSpec((bq, LANE), lambda hh, i: (i, hh))
    whole = pl.BlockSpec((t, LANE), lambda hh, i: (0, hh))
    colk = pl.BlockSpec((1, t, 1), lambda hh, i: (hh, 0, 0))
    outs, moved = _call(
        body, grid=(h, nq),
        in_specs=[
            pl.BlockSpec((bq, LANE), lambda hh, i: (i, qc + hh)),
            pl.BlockSpec((t, LANE), lambda hh, i: (0, kc + hh)),
            pl.BlockSpec((t, LANE), lambda hh, i: (0, vc + hh)),
            blk, blk, rowq, colk, rowq,
        ],
        out_specs=[blk, whole, whole, rowq, colk],
        out_shape=[jax.ShapeDtypeStruct((t, d), BF16), jax.ShapeDtypeStruct((t, d), BF16), jax.ShapeDtypeStruct((t, d), BF16),
                   jax.ShapeDtypeStruct((h, nq, 1, bq), F32), jax.ShapeDtypeStruct((h, t, 1), F32)],
        scratch_shapes=[pltpu.VMEM((t, LANE), BF16), pltpu.VMEM((nk, LANE, bk), BF16), pltpu.VMEM((t, LANE), BF16),
                        pltpu.VMEM((t, bq), F32), pltpu.VMEM((t, LANE), F32), pltpu.VMEM((t, LANE), F32),
                        pltpu.VMEM((t, LANE), F32), pltpu.VMEM((LANE, bq), F32)],
        dims=("arbitrary", "arbitrary"), name=name,
        args=[proj, proj, proj, do, o, lse.reshape(h, nq, 1, bq), c_t.reshape(h, t, 1), c_t.reshape(h, nq, 1, bq)], job=job)
    outs = list(outs[:3]) + [outs[3].reshape(h, t), outs[4].reshape(h, t)]
    return outs if job is None else (outs, moved)


def _sgu_forward(u_ref, v_ref, gv_ref, wm_ref, bs_ref, mix_sc, groups):
    gu, dgu = _gelu_and_grad(u_ref[...].astype(F32))
    gvv, dgv = _gelu_and_grad(v_ref[...].astype(F32))
    mu = jnp.mean(gvv, axis=-1, keepdims=True)
    xc = gvv - mu
    r = lax.rsqrt(jnp.mean(xc * xc, axis=-1, keepdims=True) + EPS)
    nhat = xc * r
    vn = (nhat * gv_ref[...]).astype(BF16)
    for g in range(groups):
        sl = slice(g * LANE, (g + 1) * LANE)
        mix_sc[:, sl] = jnp.dot(wm_ref[g], vn[:, sl], preferred_element_type=F32) + bs_ref[g]
    return gu, dgu, dgv, nhat, r, vn, mix_sc[...]


def _mix_fwd(proj, o, wm, bs, g_v, d, name):
    t = proj.shape[0]
    groups = d // LANE

    def body(u_ref, v_ref, ga_ref, gb_ref, o_ref, wm_ref, bs_ref, gv_ref, out_ref, mix_sc):
        gu, _, _, _, _, _, mixed = _sgu_forward(u_ref, v_ref, gv_ref, wm_ref, bs_ref, mix_sc, groups)
        out_ref[...] = (_sigmoid(ga_ref[...].astype(F32)) * (gu * mixed) + _sigmoid(gb_ref[...].astype(F32)) * o_ref[...]).astype(BF16)

    def colblk(k):
        return pl.BlockSpec((LANE, d), lambda i, k=k: (i, k))

    full3 = pl.BlockSpec((groups, LANE, LANE), lambda i: (0, 0, 0))
    return pl.pallas_call(
        body, grid=(t // LANE,),
        in_specs=[colblk(COL_U), colblk(COL_V), colblk(COL_GA), colblk(COL_GB), colblk(0), full3,
                  pl.BlockSpec((groups, LANE, 1), lambda i: (0, 0, 0)), pl.BlockSpec((1, d), lambda i: (0, 0))],
        out_specs=colblk(0),
        out_shape=jax.ShapeDtypeStruct((t, d), BF16),
        scratch_shapes=[pltpu.VMEM((LANE, d), F32)],
        compiler_params=_cparams(("parallel",)), name=name,
    )(proj, proj, proj, proj, o, wm, bs, g_v)


def _mix_bwd(dmerged, proj, o, wm, wm_t, bs, g_v, d, name, job=None):
    t = proj.shape[0]
    groups = d // LANE
    nt = t // LANE

    def body(dm_ref, u_ref, v_ref, ga_ref, gb_ref, o_ref, wm_ref, wmt_ref, bs_ref, gv_ref,
             duv_ref, dg_ref, do_ref, dws_ref, dbs_ref, dgv_ref, mix_sc, dvn_sc, gv_acc):
        i = pl.program_id(0)

        @pl.when(i == 0)
        def _():
            dws_ref[...] = jnp.zeros_like(dws_ref)
            dbs_ref[...] = jnp.zeros_like(dbs_ref)
            gv_acc[...] = jnp.zeros_like(gv_acc)

        gu, dgu, dgv, nhat, r, vn, mixed = _sgu_forward(u_ref, v_ref, gv_ref, wm_ref, bs_ref, mix_sc, groups)
        dm = dm_ref[...]
        sa = _sigmoid(ga_ref[...].astype(F32))
        sb = _sigmoid(gb_ref[...].astype(F32))
        ov = o_ref[...]
        y_a = gu * mixed
        dg_ref[:, 0:d] = (dm * y_a * sa * (1.0 - sa)).astype(BF16)
        dg_ref[:, d:2 * d] = (dm * ov * sb * (1.0 - sb)).astype(BF16)
        do_ref[...] = dm * sb
        dy_a = dm * sa
        duv_ref[:, 0:d] = (dy_a * mixed * dgu).astype(BF16)
        dmixed = dy_a * gu
        dmixed_b = dmixed.astype(BF16)
        for g in range(groups):
            sl = slice(g * LANE, (g + 1) * LANE)
            dvn_sc[:, sl] = jnp.dot(wmt_ref[g], dmixed_b[:, sl], preferred_element_type=F32)
            dws_ref[g] += _nt_dot(dmixed_b[:, sl], vn[:, sl])
            dbs_ref[g] += jnp.sum(dmixed[:, sl], axis=-1, keepdims=True)
        dvn = dvn_sc[...]
        gv_acc[...] += _sum8(dvn * nhat)
        dn = dvn * gv_ref[...]
        dgelu = r * (dn - jnp.mean(dn, axis=-1, keepdims=True) - nhat * jnp.mean(dn * nhat, axis=-1, keepdims=True))
        duv_ref[:, d:2 * d] = (dgelu * dgv).astype(BF16)

        @pl.when(i == nt - 1)
        def _():
            dgv_ref[...] = jnp.sum(gv_acc[...], axis=0, keepdims=True)
            rr = lax.broadcasted_iota(jnp.int32, (LANE, LANE), 0)
            cl = lax.broadcasted_iota(jnp.int32, (LANE, LANE), 1)
            for g in range(groups):
                dws_ref[g] = jnp.where(rr >= cl, dws_ref[g], 0.0)

    def colblk(k):
        return pl.BlockSpec((LANE, d), lambda i, k=k: (i, k))

    full3 = pl.BlockSpec((groups, LANE, LANE), lambda i: (0, 0, 0))
    col3 = pl.BlockSpec((groups, LANE, 1), lambda i: (0, 0, 0))
    vec = pl.BlockSpec((1, d), lambda i: (0, 0))
    two = pl.BlockSpec((LANE, 2 * d), lambda i: (i, 0))
    outs, moved = _call(
        body, grid=(nt,),
        in_specs=[colblk(0), colblk(COL_U), colblk(COL_V), colblk(COL_GA), colblk(COL_GB), colblk(0), full3, full3, col3, vec],
        out_specs=[two, two, colblk(0), full3, col3, vec],
        out_shape=[jax.ShapeDtypeStruct((t, 2 * d), BF16), jax.ShapeDtypeStruct((t, 2 * d), BF16), jax.ShapeDtypeStruct((t, d), F32),
                   jax.ShapeDtypeStruct((groups, LANE, LANE), F32), jax.ShapeDtypeStruct((groups, LANE, 1), F32),
                   jax.ShapeDtypeStruct((1, d), F32)],
        scratch_shapes=[pltpu.VMEM((LANE, d), F32), pltpu.VMEM((LANE, d), F32), pltpu.VMEM((SUBLANE, d), F32)],
        dims=("arbitrary",), name=name, args=[dmerged, proj, proj, proj, proj, o, wm, wm_t, bs, g_v], job=job)
    return outs if job is None else (outs, moved)


def _adam_math(w, g, m, v):
    nm = ADAM_B1 * m + (1.0 - ADAM_B1) * g
    nv = ADAM_B2 * v + (1.0 - ADAM_B2) * (g * g)
    delta = -ADAM_LR * ((nm * ADAM_C1) / (jnp.sqrt(nv * ADAM_C2) + ADAM_EPS) + ADAM_WD * w)
    return delta, nm, nv


def _adamw(w, g, m, v, name):
    r, c = w.shape
    cap = max(SUBLANE, (2 * 1024 * 1024) // (4 * c) // SUBLANE * SUBLANE)
    tr = _rows(r, cap)

    def body(w_ref, g_ref, m_ref, v_ref, d_ref, nm_ref, nv_ref):
        d_ref[...], nm_ref[...], nv_ref[...] = _adam_math(w_ref[...], g_ref[...], m_ref[...], v_ref[...])

    blk = pl.BlockSpec((tr, c), lambda i: (i, 0))
    return pl.pallas_call(
        body, grid=(r // tr,), in_specs=[blk] * 4, out_specs=[blk] * 3,
        out_shape=[jax.ShapeDtypeStruct((r, c), F32)] * 3, compiler_params=_cparams(("parallel",)), name=name,
    )(w, g, m, v)


def _adamw_layers(w, g0, g1, m, v, name):
    _, r, c = w.shape
    cap = max(SUBLANE, (1024 * 1024) // (4 * c) // SUBLANE * SUBLANE)
    tr = _rows(r, cap)

    def body(w_ref, g0_ref, g1_ref, m_ref, v_ref, g_ref, d_ref, nm_ref, nv_ref):
        gg = jnp.where(pl.program_id(0) == 0, g0_ref[...], g1_ref[...])
        g_ref[0] = gg
        d_ref[0], nm_ref[0], nv_ref[0] = _adam_math(w_ref[0], gg, m_ref[0], v_ref[0])

    lay = pl.BlockSpec((1, tr, c), lambda l, i: (l, i, 0))

    def gspec(l0):
        return pl.BlockSpec((tr, c), lambda l, i: (jnp.where(l == l0, i, 0), 0))

    return pl.pallas_call(
        body, grid=(2, r // tr), in_specs=[lay, gspec(0), gspec(1), lay, lay], out_specs=[lay] * 4,
        out_shape=[jax.ShapeDtypeStruct((2, r, c), F32)] * 4, compiler_params=_cparams(("arbitrary", "arbitrary")), name=name,
    )(w, g0, g1, m, v)


def _adamw_interleaved(w, g0, g1, m, v, name):
    r, _, c = w.shape
    tr = 128

    def body(w_ref, g0_ref, g1_ref, m_ref, v_ref, g_ref, d_ref, nm_ref, nv_ref):
        for l, gl in enumerate((g0_ref, g1_ref)):
            gg = gl[...]
            g_ref[:, l, :] = gg
            d_ref[:, l, :], nm_ref[:, l, :], nv_ref[:, l, :] = _adam_math(w_ref[:, l, :], gg, m_ref[:, l, :], v_ref[:, l, :])

    lay = pl.BlockSpec((tr, 2, c), lambda i: (i, 0, 0))
    flat = pl.BlockSpec((tr, c), lambda i: (i, 0))
    return pl.pallas_call(
        body, grid=(pl.cdiv(r, tr),), in_specs=[lay, flat, flat, lay, lay], out_specs=[lay] * 4,
        out_shape=[jax.ShapeDtypeStruct((r, 2, c), F32)] * 4, compiler_params=_cparams(("parallel",)), name=name,
    )(w, g0, g1, m, v)


def _add_half(p4, recv, c_idx, name):
    _, r, c = p4.shape
    hw = c // 2
    tr = 256 if r % 256 == 0 else r

    def body(c_ref, a_ref, b_ref, o_ref):
        o_ref[...] = (a_ref[...].astype(F32) + b_ref[...].astype(F32)).astype(BF16)

    return pl.pallas_call(
        body,
        grid_spec=pltpu.PrefetchScalarGridSpec(
            num_scalar_prefetch=1, grid=(N_CHIPS, pl.cdiv(r, tr)),
            in_specs=[pl.BlockSpec((1, tr, hw), lambda s, i, cr: (s, i, cr[0])), pl.BlockSpec((1, tr, hw), lambda s, i, cr: (s, i, 0))],
            out_specs=pl.BlockSpec((1, tr, hw), lambda s, i, cr: (s, i, 0)),
        ),
        out_shape=jax.ShapeDtypeStruct((N_CHIPS, r, hw), BF16), compiler_params=_cparams(("parallel", "parallel")), name=name,
    )(c_idx, p4, recv)


def _sum_slots(x, own, sel, name, out_cols=None):
    s, r, c = x.shape
    tr = 128 if r % 128 == 0 else r

    def body(sel_ref, x_ref, own_ref, o_ref):
        mine = own_ref[0].astype(F32)
        acc = jnp.zeros((tr, c), F32)
        for k in range(s):
            acc = acc + jnp.where(sel_ref[0] == k, mine, x_ref[k].astype(F32))
        o_ref[...] = acc

    return pl.pallas_call(
        body,
        grid_spec=pltpu.PrefetchScalarGridSpec(
            num_scalar_prefetch=1, grid=(pl.cdiv(r, tr),),
            in_specs=[pl.BlockSpec((s, tr, c), lambda i, sr: (0, i, 0)), pl.BlockSpec((1, tr, c), lambda i, sr: (sr[1], i, 0))],
            out_specs=pl.BlockSpec((tr, c), lambda i, sr: (i, sr[2])),
        ),
        out_shape=jax.ShapeDtypeStruct((r, out_cols or c), F32), compiler_params=_cparams(("parallel",)), name=name,
    )(sel, x, own)


def _half_cols(width, hc):
    hw = width // 2
    assert hw % LANE == 0
    return pl.ds(pl.multiple_of(hc * hw, LANE), hw)


def _remote(src, dst, ssem, rsem, k, to):
    return pltpu.make_async_remote_copy(src_ref=src, dst_ref=dst, send_sem=ssem.at[k], recv_sem=rsem.at[k], device_id=to,
                                        device_id_type=MESH)


def _gather_job(bufs, mid_at=0.5):
    def part(o, a, slot, hc):
        return o[a].at[slot, :, _half_cols(bufs[a].shape[2], hc)]

    def first(ins, o, fresh, ssem, rsem):
        x, y, c, chips = _place()
        for a in range(len(bufs)):
            mine = part(o, a, 2 * x + y, c)
            for j, chip in enumerate(chips):
                _remote(mine, mine, ssem, rsem, 6 * a + j, (chip[0], chip[1], c)).start()

    def mid(ins, o, fresh, ssem, rsem):
        x, y, c, chips = _place()
        for a in range(len(bufs)):
            for j, chip in enumerate(chips):
                got = part(o, a, 2 * chip[0] + chip[1], c)
                _remote(got, got, ssem, rsem, 6 * a + j, (x, y, c)).wait_recv()
                _remote(got, got, ssem, rsem, 6 * a + 3 + j, (x, y, 1 - c)).start()

    def last(ins, o, fresh, ssem, rsem):
        x, y, c, chips = _place()
        for a in range(len(bufs)):
            for j, chip in enumerate(chips):
                got = part(o, a, 2 * chip[0] + chip[1], 1 - c)
                _remote(got, got, ssem, rsem, 6 * a + 3 + j, (x, y, c)).wait_recv()
        for a in range(len(bufs)):
            mine = part(o, a, 2 * x + y, c)
            for j, chip in enumerate(chips):
                _remote(mine, mine, ssem, rsem, 6 * a + j, (x, y, c)).wait_send()
                passed = part(o, a, 2 * chip[0] + chip[1], c)
                _remote(passed, passed, ssem, rsem, 6 * a + 3 + j, (x, y, c)).wait_send()

    return _Job([], bufs, [], 6 * len(bufs), first, mid, last, mid_at)


def _swap_job(p4s):
    def pairs(ins, fresh, c):
        return [(a, s, ins[a].at[s, :, _half_cols(p4s[a].shape[2], 1 - c)], fresh[a].at[s])
                for a in range(len(p4s)) for s in range(N_CHIPS)]

    def first(ins, inout, fresh, ssem, rsem):
        x, y, c, _ = _place()
        for a, s, src, dst in pairs(ins, fresh, c):
            _remote(src, dst, ssem, rsem, N_CHIPS * a + s, (x, y, 1 - c)).start()

    def last(ins, inout, fresh, ssem, rsem):
        x, y, c, _ = _place()
        for a, s, src, dst in pairs(ins, fresh, c):
            _remote(src, dst, ssem, rsem, N_CHIPS * a + s, (x, y, 1 - c)).wait()

    fresh = [jax.ShapeDtypeStruct(p.shape[:2] + (p.shape[2] // 2,), p.dtype) for p in p4s]
    return _Job(p4s, [], fresh, N_CHIPS * len(p4s), first, None, last)


def _scatter_job(parts):
    def first(ins, inout, fresh, ssem, rsem):
        x, y, c, chips = _place()
        for a in range(len(parts)):
            for j, chip in enumerate(chips):
                _remote(ins[a].at[2 * chip[0] + chip[1]], fresh[a].at[2 * x + y], ssem, rsem, 3 * a + j, (chip[0], chip[1], c)).start()

    def last(ins, inout, fresh, ssem, rsem):
        x, y, c, chips = _place()
        for a in range(len(parts)):
            for j, chip in enumerate(chips):
                slot = 2 * chip[0] + chip[1]
                _remote(ins[a].at[slot], fresh[a].at[slot], ssem, rsem, 3 * a + j, (x, y, c)).wait()

    return _Job(parts, [], [jax.ShapeDtypeStruct(p.shape, p.dtype) for p in parts], 3 * len(parts), first, None, last)


def _share_job(gs):
    def halves(o, a, c):
        width = gs[a].shape[1]
        return o[a].at[:, _half_cols(width, c)], o[a].at[:, _half_cols(width, 1 - c)]

    def first(ins, o, fresh, ssem, rsem):
        x, y, c, _ = _place()
        for a in range(len(gs)):
            mine, _ = halves(o, a, c)
            _remote(mine, mine, ssem, rsem, a, (x, y, 1 - c)).start()

    def last(ins, o, fresh, ssem, rsem):
        x, y, c, _ = _place()
        for a in range(len(gs)):
            mine, theirs = halves(o, a, c)
            _remote(mine, theirs, ssem, rsem, a, (x, y, 1 - c)).wait()

    return _Job([], gs, [], len(gs), first, None, last)


def _gather_all_job(buf):
    def peers():
        x, y, c, _ = _place()
        flips = [(fx, fy, fc) for fx in (0, 1) for fy in (0, 1) for fc in (0, 1)][1:]
        return (x, y, c), [((1 - x) if fx else x, (1 - y) if fy else y, (1 - c) if fc else c) for fx, fy, fc in flips]

    def first(ins, inout, fresh, ssem, rsem):
        (x, y, c), others = peers()
        for k, peer in enumerate(others):
            _remote(ins[0], fresh[0].at[4 * x + 2 * y + c], ssem, rsem, k, peer).start()

    def last(ins, inout, fresh, ssem, rsem):
        me, others = peers()
        for k, peer in enumerate(others):
            _remote(ins[0], fresh[0].at[4 * peer[0] + 2 * peer[1] + peer[2]], ssem, rsem, k, me).wait()

    return _Job([buf], [], [jax.ShapeDtypeStruct((N_DEV,) + buf.shape, buf.dtype)], N_DEV - 1, first, None, last)


class _SemView:
    def __init__(self, sems, off):
        self.sems, self.off = sems, off

    @property
    def at(self):
        return self

    def __getitem__(self, k):
        return self.sems.at[k + self.off]


def _join(jobs):
    spans, pos = [], [0, 0, 0, 0]
    for j in jobs:
        nxt = [pos[0] + len(j.ins), pos[1] + len(j.inout), pos[2] + len(j.fresh), pos[3] + j.nsem]
        spans.append((pos, nxt))
        pos = nxt

    def hook(which):
        fns = [getattr(j, which) for j in jobs]
        if all(f is None for f in fns):
            return None

        def run(ins, inout, fresh, ssem, rsem):
            for fn, (lo, hi) in zip(fns, spans):
                if fn is not None:
                    fn(ins[lo[0]:hi[0]], inout[lo[1]:hi[1]], fresh[lo[2]:hi[2]], _SemView(ssem, lo[3]), _SemView(rsem, lo[3]))

        return run

    mids = [j.mid_at for j in jobs if j.mid is not None]
    joined = _Job([a for j in jobs for a in j.ins], [a for j in jobs for a in j.inout], [a for j in jobs for a in j.fresh],
                  pos[3], hook("first"), hook("mid"), hook("last"), max(mids) if mids else 0.5)
    n_io = pos[1]

    def split(moved):
        return [list(moved[lo[1]:hi[1]]) + list(moved[n_io + lo[2]:n_io + hi[2]]) for lo, hi in spans]

    return joined, split


def _carrying(stages, call):
    stages = [s for s in stages if s is not None]
    if not stages:
        return call(None)
    job, split = _join([s[0] for s in stages])
    out, moved = call(job)
    for (_, done), part in zip(stages, split(moved)):
        done(part)
    return out


def _layer_forward(x, h, w_in_t, rest, sm, g_next, d, stages=None):
    stages = stages or {}
    proj = _carrying([stages.get("proj")], lambda job: _matmul(h, w_in_t, "nt", BF16, "proj_fwd", n=7 * d, tn_cap=1792, job=job))
    f_t = _matmul(w_in_t[7 * d:], h, "nt", F32, "forget_fwd", tn_cap=1024)
    c_t = _fox_prep(f_t, sm["b_f"], "fox_prep")
    o, lse = _carrying([stages.get("attn")], lambda job: _attn_fwd(proj, c_t, d, "attn_fwd", job=job))
    wts = rest()
    merged = _mix_fwd(proj, o, sm["wm"], sm["bs"], sm["g_v"], d, "mix_fwd")
    z, x1, h2 = _matmul(merged, wts["w_out"], "nn", F32, "out_fwd", norms=(x, sm["g_post"], sm["g_fpre"]))
    a = _carrying([stages.get("gate")], lambda job: _matmul(h2, wts["w_g_t"], "nt", BF16, "gate_fwd", tn_cap=1408, job=job))
    b = _carrying([stages.get("up")], lambda job: _matmul(h2, wts["w_u_t"], "nt", BF16, "up_fwd", tn_cap=1408, job=job))
    z2, x_out, h_out, mm = _carrying([stages.get("down")], lambda job: _matmul(
        b, wts["w_d"], "nn", F32, "down_fwd", norms=(x1, sm["g_fpost"], g_next), silu_of=a, job=job))
    return dict(x=x, h=h, proj=proj, f_t=f_t, c_t=c_t, o=o, lse=lse, merged=merged, z=z, x1=x1,
                h2=h2, a=a, b=b, mm=mm, z2=z2, x_out=x_out, h_out=h_out)


class _GradExchange:
    def __init__(self, pay, keys, c_idx, chip):
        self.keys = list(keys)
        self.p4 = [pay[k].reshape(N_CHIPS, pay[k].shape[1] // N_CHIPS, pay[k].shape[2]) for k in self.keys]
        self.c_idx = c_idx
        self.sel = jnp.stack([chip, chip, c_idx[0]]).astype(jnp.int32)
        self.done = 0

    def _after_swap(self, landed):
        self.parts = [_add_half(p, r, self.c_idx, "add_sibling") for p, r in zip(self.p4, landed)]
        self.done = 1

    def _after_scatter(self, landed):
        self.g = [_sum_slots(got, sent, self.sel, "sum_chips", out_cols=p.shape[2])
                  for got, sent, p in zip(landed, self.parts, self.p4)]
        self.done = 2

    def _after_share(self, moved):
        self.g = list(moved)
        self.done = 3

    def stage(self):
        if self.done == 0:
            return _swap_job(self.p4), self._after_swap
        if self.done == 1:
            return _scatter_job(self.parts), self._after_scatter
        if self.done == 2:
            return _share_job(self.g), self._after_share
        return None

    def run(self):
        for name in ("swap_grads", "scatter_grads", "share_grads")[self.done:]:
            job, done = self.stage()
            done(_run_job(job, name))

    def grads(self):
        return dict(zip(self.keys, self.g))


EARLY_KEYS = ("w_d", "w_g", "w_u", "w_out")


def _layer_backward(dz2, dx2, sv, wts, sm, d, c_idx, chip, carried=(), split_own=False, small_stage=None):
    t = dx2.shape[0]
    heads = d // LANE
    ff = wts["w_d"].shape[0]
    in_w = 7 * d + heads
    g, pay = {}, {}
    carried = list(carried)

    def payload(key, a, b, rows, row0, name, extra=()):
        def call(job):
            return _matmul(a, b, "tn", BF16, name, slab=((1, rows, d), 0, row0), into=pay.get(key), job=job, tm_cap=1408,
                           tn_cap=1024, tk_cap=1024)
        pay[key] = _carrying(list(extra), call)

    def nxt(*exchanges):
        return [ex.stage() for ex in exchanges]

    dm = _carrying(nxt(*carried), lambda job: _matmul(dz2, wts["w_d"], "nt", BF16, "down_bwd_x", tn_cap=1408, tk_cap=1024, job=job))
    payload("w_d", sv["mm"], dz2, ff, 0, "down_bwd_w")
    da, db = _swiglu_bwd(sv["a"], sv["b"], dm, "swiglu_bwd")
    dh2 = _matmul_pieces([(da, wts["w_g_t"], 0), (db, wts["w_u_t"], 0)], None, "gu_bwd_x", tk=_tile(ff, 1408))
    payload("w_g", da, sv["h2"], ff, 0, "gate_bwd_w")
    payload("w_u", db, sv["h2"], ff, 0, "up_bwd_w")
    dx1, dz, g["g_fpre"], g["g_post"] = _norm_bwd(dx2, (dh2, sv["x1"], sm["g_fpre"]), (sv["z"], sm["g_post"]), "norm_bwd_mid")
    dmerged = _matmul(dz, wts["w_out"], "nt", F32, "out_bwd_x", tk_cap=1024)
    payload("w_out", sv["merged"], dz, d, 0, "out_bwd_w")
    early = [_GradExchange(pay, EARLY_KEYS, c_idx, chip)] if split_own else []
    d_uv, d_g, do, g["w_s"], g["b_s"], g["g_v"] = _carrying(nxt(*early), lambda job: _mix_bwd(
        dmerged, sv["proj"], sv["o"], sm["wm"], sm["wm_t"], sm["bs"], sm["g_v"], d, "mix_bwd", job=job))
    extra = [small_stage(g)] if small_stage is not None else []
    attn_args = (sv["proj"], do, sv["o"], sv["lse"], sv["c_t"], d)
    dq, dk, dv, dc_q, dc_k = _carrying(nxt(*carried) + extra, lambda job: _attn_bwd(*attn_args, "attn_bwd", job=job))
    df_t, g["b_f"] = _fox_bwd(dc_q, dc_k, sv["f_t"], sm["b_f"], "fox_bwd")
    df_b = df_t.astype(BF16)
    pieces = [(d_uv, COL_U), (dq, COL_Q), (dk, COL_K), (dv, COL_VA), (d_g, COL_GA)]
    pay["w_in"] = _carrying(nxt(*carried, *early), lambda job: _matmul_rows([p for p, _ in pieces], sv["h"], in_w,
                                                                           "proj_bwd_w", job=job))
    w_f_rows = _carrying(nxt(*early), lambda job: _matmul(df_b, sv["h"], "nn", BF16, "forget_bwd_w", tk_cap=1024, job=job))
    pay["w_in"] = lax.dynamic_update_slice(pay["w_in"], w_f_rows[None], (0, 7 * d, 0))
    late = _GradExchange(pay, [k for k in ("w_in",) + EARLY_KEYS if not (split_own and k in EARLY_KEYS)], c_idx, chip)
    mine = [late] if split_own else []
    dh_f = _carrying(nxt(*mine), lambda job: _matmul(df_b, wts["w_in_t"][7 * d:], "tn", F32, "forget_bwd_x", job=job))
    ops = [(p, wts["w_in_t"], col * d) for p, col in pieces]
    dh = _carrying(nxt(*mine), lambda job: _matmul_pieces(ops, dh_f, "proj_bwd_x", job=job, tk=_tile(d, 1024)))
    return dh, dx1, g, early + [late]


def _small_pack(parts):
    flat = jnp.concatenate([p.reshape(-1) for p in parts])
    n = flat.shape[0]
    pad = (-n) % (LANE * LANE)
    return jnp.pad(flat, (0, pad)).reshape(-1, LANE)


def kernel(x, mix_pre_g, w_in, b_forget, sgu_norm_g, w_spatial, b_spatial, w_out, mix_post_g, ffn_pre_g, w_gate, w_up, w_down, ffn_post_g, loss_target, m_mix_pre_g, m_w_in, m_b_forget, m_sgu_norm_g, m_w_spatial, m_b_spatial, m_w_out, m_mix_post_g, m_ffn_pre_g, m_w_gate, m_w_up, m_w_down, m_ffn_post_g, v_mix_pre_g, v_w_in, v_b_forget, v_sgu_norm_g, v_w_spatial, v_b_spatial, v_w_out, v_mix_post_g, v_ffn_pre_g, v_w_gate, v_w_up, v_w_down, v_ffn_post_g):
    depth, d = mix_pre_g.shape
    assert depth == 2, "core c of a chip owns layer c"
    heads = d // LANE
    t = x.shape[1]
    ff = w_down.shape[1] * N_CHIPS
    in_w = w_in.shape[2] * N_CHIPS
    assert in_w == 7 * d + heads
    xs = x.reshape(t, d)
    target = loss_target.reshape(t, d)
    c_idx = lax.axis_index("c").astype(jnp.int32).reshape(1)
    chip = 2 * lax.axis_index("x") + lax.axis_index("y")
    dev = 2 * chip + lax.axis_index("c")

    def in_view(w):
        return jnp.transpose(w, (2, 0, 1))

    def gu_view(w):
        return jnp.transpose(w, (0, 2, 1))

    own = [jnp.transpose(in_view(w_in).astype(BF16), (1, 0, 2)), w_out.astype(BF16), gu_view(w_gate).astype(BF16),
           gu_view(w_up).astype(BF16), w_down.astype(BF16)]
    bufs = [[lax.dynamic_update_slice(lax.empty((N_CHIPS,) + o.shape[1:], BF16), o[l][None], (chip, 0, 0)) for o in own]
            for l in range(depth)]
    first_in = _run_job(_gather_job([bufs[0][0]]), "gather_first")[0]

    def weights(g_in, g_out, g_g, g_u, g_d):
        return dict(w_in_t=g_in.reshape(in_w, d), w_out=g_out.reshape(d, d), w_g_t=g_g.reshape(ff, d),
                    w_u_t=g_u.reshape(ff, d), w_d=g_d.reshape(ff, d))

    tril = jnp.tril(jnp.ones((LANE, LANE), bool))
    smalls = []
    for l in range(depth):
        wm = jnp.where(tril[None], w_spatial[l], 0.0).astype(BF16)
        smalls.append(dict(
            b_f=b_forget[l].reshape(heads, 1), wm=wm, wm_t=jnp.swapaxes(wm, 1, 2), bs=b_spatial[l].reshape(heads, LANE, 1),
            g_v=sgu_norm_g[l].reshape(1, d), g_pre=mix_pre_g[l].reshape(1, d), g_post=mix_post_g[l].reshape(1, d),
            g_fpre=ffn_pre_g[l].reshape(1, d), g_fpost=ffn_post_g[l].reshape(1, d)))

    wts, later = [], {}

    def keep(key):
        def done(moved):
            later[key] = list(moved)
        return done

    def rest_first():
        wts.append(weights(first_in, *later["rest0"]))
        return wts[0]

    stages = dict(proj=(_gather_job(bufs[0][1:], mid_at=1.0), keep("rest0")),
                  attn=(_gather_job(bufs[1][0:2], mid_at=0.7), keep("in_out1")),
                  gate=(_gather_job(bufs[1][2:3], mid_at=1.0), keep("g1")), up=(_gather_job(bufs[1][3:4], mid_at=1.0), keep("u1")),
                  down=(_gather_job(bufs[1][4:5], mid_at=1.0), keep("d1")))
    h = _norm_fwd(xs, None, None, smalls[0]["g_pre"], "norm_first")
    g_after = [smalls[min(l + 1, depth - 1)]["g_pre"] for l in range(depth)]
    saved = [_layer_forward(xs, h, first_in.reshape(in_w, d), rest_first, smalls[0], g_after[0], d, stages)]
    wts.append(weights(*later["in_out1"], later["g1"][0], later["u1"][0], later["d1"][0]))
    for l in range(1, depth):
        saved.append(_layer_forward(saved[l - 1]["x_out"], saved[l - 1]["h_out"], wts[l]["w_in_t"], lambda l=l: wts[l], smalls[l],
                                    g_after[l], d))
    dy, loss_part = _loss_grad(saved[-1]["x_out"], target, "loss")
    loss = lax.psum(jnp.sum(loss_part), ("x", "y", "c"))

    small_shapes = dict(g_pre=(d,), b_f=(heads,), g_v=(d,), w_s=w_spatial.shape[1:], b_s=b_spatial.shape[1:], g_post=(d,),
                        g_fpre=(d,), g_fpost=(d,))
    late_entries = [(0, "g_pre"), (0, "b_f")]
    early_entries = [(l, n) for l in reversed(range(depth)) for n in small_shapes if (l, n) not in late_entries]
    dev_sel = jnp.stack([dev, jnp.zeros_like(dev), jnp.zeros_like(dev)]).astype(jnp.int32)
    small_sum = {}

    def small_exchange(entries, values):
        packed = _small_pack([values[e].reshape(-1) for e in entries])

        def done(moved):
            total = _sum_slots(moved[0], packed[None], dev_sel, "sum_small").reshape(-1)
            off = 0
            for e in entries:
                n = math.prod(small_shapes[e[1]])
                small_sum[e] = total[off:off + n].reshape(small_shapes[e[1]])
                off += n

        return _gather_all_job(packed), done

    grads = [None] * depth
    exchanges = [None] * depth
    dx2 = dy
    dz2, g_fpost = _norm_bwd(dx2, None, (saved[depth - 1]["z2"], smalls[depth - 1]["g_fpost"]), "norm_bwd_top")
    for l in reversed(range(depth)):
        last = l == 0

        def small_stage(g, l=l, g_fpost=g_fpost):
            known = {(k, n): grads[k][n] for k in range(l + 1, depth) for n in small_shapes}
            known.update({(l, n): g[n] for n in g})
            known[(l, "g_fpost")] = g_fpost
            return small_exchange(early_entries, known)

        carried = [ex for k in range(l + 1, depth) for ex in exchanges[k]]
        dh, dx1, g, exchanges[l] = _layer_backward(dz2, dx2, saved[l], wts[l], smalls[l], d, c_idx, chip, carried=carried,
                                                    split_own=last, small_stage=small_stage if last else None)
        g["g_fpost"] = g_fpost
        if l > 0:
            dx2, dz2, g["g_pre"], g_fpost = _norm_bwd(dx1, (dh, saved[l]["x"], smalls[l]["g_pre"]),
                                                       (saved[l - 1]["z2"], smalls[l - 1]["g_fpost"]), "norm_bwd_between")
        else:
            grad_x, g["g_pre"] = _norm_bwd(dx1, (dh, saved[l]["x"], smalls[l]["g_pre"]), None, "norm_bwd_bottom")
        grads[l] = g
    job, done = small_exchange(late_entries, {(0, n): grads[0][n] for n in ("g_pre", "b_f")})
    done(_run_job(job, "gather_small"))
    big = [{} for _ in range(depth)]
    for l in range(depth):
        for ex in exchanges[l]:
            ex.run()
            big[l].update(ex.grads())
    small_grads = {n: jnp.stack([small_sum[(l, n)] for l in range(depth)]) for n in small_shapes}

    def adam_small(w, g, m, v):
        shp = w.shape
        if w.ndim >= 3 and shp[-1] >= LANE:
            two = (math.prod(shp[:-1]), shp[-1])
        else:
            two = (1, math.prod(shp)) if math.prod(shp) < LANE else (math.prod(shp) // LANE, LANE)
        outs = _adamw(w.reshape(two), g.reshape(two), m.reshape(two), v.reshape(two), "adamw")
        return [g] + [o.reshape(shp) for o in outs]

    def adam_in(w, m, v):
        outs = _adamw_interleaved(in_view(w), big[0]["w_in"], big[1]["w_in"], in_view(m), in_view(v), "adamw_in")
        return [jnp.transpose(o, (1, 2, 0)) for o in outs]

    def adam_gu(k, w, m, v):
        outs = _adamw_layers(gu_view(w), big[0][k], big[1][k], gu_view(m), gu_view(v), "adamw_layers")
        return [jnp.transpose(o, (0, 2, 1)) for o in outs]

    def adam_rows(k, w, m, v):
        return _adamw_layers(w, big[0][k], big[1][k], m, v, "adamw_layers")

    results = [
        adam_small(mix_pre_g, small_grads["g_pre"], m_mix_pre_g, v_mix_pre_g),
        adam_in(w_in, m_w_in, v_w_in),
        adam_small(b_forget, small_grads["b_f"], m_b_forget, v_b_forget),
        adam_small(sgu_norm_g, small_grads["g_v"], m_sgu_norm_g, v_sgu_norm_g),
        adam_small(w_spatial, small_grads["w_s"], m_w_spatial, v_w_spatial),
        adam_small(b_spatial, small_grads["b_s"], m_b_spatial, v_b_spatial),
        adam_rows("w_out", w_out, m_w_out, v_w_out),
        adam_small(mix_post_g, small_grads["g_post"], m_mix_post_g, v_mix_post_g),
        adam_small(ffn_pre_g, small_grads["g_fpre"], m_ffn_pre_g, v_ffn_pre_g),
        adam_gu("w_g", w_gate, m_w_gate, v_w_gate),
        adam_gu("w_u", w_up, m_w_up, v_w_up),
        adam_rows("w_d", w_down, m_w_down, v_w_down),
        adam_small(ffn_post_g, small_grads["g_fpost"], m_ffn_post_g, v_ffn_post_g),
    ]
    gs, deltas, new_ms, new_vs = zip(*results)
    return (loss, grad_x.reshape(x.shape), *gs, *deltas, *new_ms, *new_vs)
```

```python
import functools
import math

import jax
import jax.numpy as jnp
from jax import lax
from jax.experimental import pallas as pl
from jax.experimental.pallas import tpu as pltpu

F32 = jnp.float32
BF16 = jnp.bfloat16

EPS = 1e-6
LANE = 128
SUBLANE = 8
N_CHIPS = 4
N_DEV = 8
VMEM_LIMIT = 48 * 1024 * 1024
MESH = pl.DeviceIdType.MESH

ADAM_LR = 0.001
ADAM_B1 = 0.9
ADAM_B2 = 0.999
ADAM_EPS = 1e-08
ADAM_WD = 0.01
ADAM_STEP = 10
ADAM_C1 = 1.0 / (1.0 - ADAM_B1 ** ADAM_STEP)
ADAM_C2 = 1.0 / (1.0 - ADAM_B2 ** ADAM_STEP)

GELU_K = math.sqrt(2.0 / math.pi)
GELU_A = 0.044715
NEG = -1e30
LOG2E = 1.4426950408889634
LN2 = 0.6931471805599453

COL_U, COL_V, COL_Q, COL_K, COL_VA, COL_GA, COL_GB, COL_F = range(8)


def _cparams(sem=None):
    return pltpu.CompilerParams(dimension_semantics=sem, vmem_limit_bytes=VMEM_LIMIT)


def _tile(n, cap):
    best = None
    for t in range(LANE, min(n, cap) + 1, LANE):
        if n % t == 0:
            best = t
    return best if best is not None else n


def _rows(n, cap):
    best = None
    for t in range(SUBLANE, min(n, cap) + 1, SUBLANE):
        if n % t == 0:
            best = t
    return best if best is not None else n


def _gelu_and_grad(x):
    x2 = x * x
    t = jnp.tanh(GELU_K * (x + GELU_A * x2 * x))
    g = 0.5 * x * (1.0 + t)
    dg = 0.5 * (1.0 + t) + 0.5 * x * (1.0 - t * t) * (GELU_K * (1.0 + 3.0 * GELU_A * x2))
    return g, dg


def _sigmoid(x):
    return 0.5 + 0.5 * jnp.tanh(0.5 * x)


def _sigmoid_small(x):
    return 1.0 / (1.0 + jnp.exp(-x))


def _sum8(v):
    n, d = v.shape
    return v.reshape(n // SUBLANE, SUBLANE, d).sum(axis=0)


def _nt_dot(a, b):
    return lax.dot_general(a, b, (((1,), (1,)), ((), ())), preferred_element_type=F32)


_HBM = pl.BlockSpec(memory_space=pl.ANY)


def _place():
    x, y, c = lax.axis_index("x"), lax.axis_index("y"), lax.axis_index("c")
    chips = [(1 - x, y), (x, 1 - y), (1 - x, 1 - y)]
    return x, y, c, chips


class _Job:
    def __init__(self, ins, inout, fresh, nsem, first, mid, last, mid_at=0.5):
        self.ins, self.inout, self.fresh, self.nsem = list(ins), list(inout), list(fresh), nsem
        self.first, self.mid, self.last, self.mid_at = first, mid, last, mid_at


def _call(body, *, grid, in_specs, out_specs, out_shape, scratch_shapes, dims, name, args, aliases=None, job=None):
    single = not isinstance(out_shape, (list, tuple))
    out_specs = [out_specs] if single else list(out_specs)
    out_shape = [out_shape] if single else list(out_shape)
    aliases = dict(aliases or {})
    if job is None:
        outs = pl.pallas_call(body, grid=grid, in_specs=in_specs, out_specs=out_specs, out_shape=out_shape,
                              scratch_shapes=scratch_shapes, input_output_aliases=aliases, compiler_params=_cparams(dims),
                              name=name)(*args)
        return (outs[0] if single else outs), []
    n_in, n_out, n_scr = len(args), len(out_shape), len(scratch_shapes)
    n_ji, n_jio, n_jf = len(job.ins), len(job.inout), len(job.fresh)
    total = math.prod(grid)

    def wrapped(*refs):
        host_in = refs[:n_in]
        pos = n_in
        j_ins = refs[pos:pos + n_ji]
        pos += n_ji + n_jio
        host_out = refs[pos:pos + n_out]
        pos += n_out
        j_inout = refs[pos:pos + n_jio]
        pos += n_jio
        j_fresh = refs[pos:pos + n_jf]
        pos += n_jf
        host_scr = refs[pos:pos + n_scr]
        ssem, rsem = refs[pos + n_scr:]
        flat = 0
        for ax, size in enumerate(grid):
            flat = flat * size + pl.program_id(ax)

        def hook(fn, at):
            if fn is not None:
                @pl.when(flat == at)
                def _():
                    fn(j_ins, j_inout, j_fresh, ssem, rsem)

        hook(job.first, 0)
        body(*host_in, *host_out, *host_scr)
        hook(job.mid, min(int(total * job.mid_at), total - 1))
        hook(job.last, total - 1)

    for k in range(n_jio):
        aliases[n_in + n_ji + k] = n_out + k
    outs = pl.pallas_call(
        wrapped, grid=grid,
        in_specs=list(in_specs) + [_HBM] * (n_ji + n_jio),
        out_specs=out_specs + [_HBM] * (n_jio + n_jf),
        out_shape=out_shape + [jax.ShapeDtypeStruct(b.shape, b.dtype) for b in job.inout] + list(job.fresh),
        scratch_shapes=list(scratch_shapes) + [pltpu.SemaphoreType.DMA((job.nsem,)), pltpu.SemaphoreType.DMA((job.nsem,))],
        input_output_aliases=aliases, compiler_params=_cparams(tuple("arbitrary" for _ in grid)), name=name,
    )(*args, *job.ins, *job.inout)
    host = outs[:n_out]
    return (host[0] if single else host), outs[n_out:]


def _run_job(job, name):
    n_ji, n_jio, n_jf = len(job.ins), len(job.inout), len(job.fresh)

    def body(*refs):
        j_ins = refs[:n_ji]
        pos = n_ji + n_jio
        j_inout = refs[pos:pos + n_jio]
        j_fresh = refs[pos + n_jio:pos + n_jio + n_jf]
        ssem, rsem = refs[pos + n_jio + n_jf:]
        for fn in (job.first, job.mid, job.last):
            if fn is not None:
                fn(j_ins, j_inout, j_fresh, ssem, rsem)

    return pl.pallas_call(
        body, in_specs=[_HBM] * (n_ji + n_jio), out_specs=[_HBM] * (n_jio + n_jf),
        out_shape=[jax.ShapeDtypeStruct(b.shape, b.dtype) for b in job.inout] + list(job.fresh),
        scratch_shapes=[pltpu.SemaphoreType.DMA((job.nsem,)), pltpu.SemaphoreType.DMA((job.nsem,))],
        input_output_aliases={n_ji + k: k for k in range(n_jio)}, name=name,
    )(*job.ins, *job.inout)


_DIMS ={"nn": ((1,), (0,)), "nt": ((1,), (1,)), "tn": ((0,), (0,))}


def _matmul(a, b, mode, out_dtype, name, n=None, slab=None, into=None, job=None, norms=None, silu_of=None, tm_cap=512,
            tn_cap=2048, tk_cap=1408):
    if mode == "nn":
        (m, k), (k2, nn_) = a.shape, b.shape
    elif mode == "nt":
        (m, k), (nn_, k2) = a.shape, b.shape
    else:
        (k, m), (k2, nn_) = a.shape, b.shape
    n = nn_ if n is None else n
    assert k == k2, (a.shape, b.shape, mode)
    tm, tn, tk = _tile(m, tm_cap), _tile(n, tn_cap), _tile(k, tk_cap)
    if slab is not None and slab[2]:
        tm = _tile(math.gcd(m, slab[2]), tm_cap)
    nk = k // tk
    if mode == "tn":
        a_spec = pl.BlockSpec((tk, tm), lambda j, i, kk, *_: (kk, i))
    else:
        a_spec = pl.BlockSpec((tm, tk), lambda j, i, kk, *_: (i, kk))
    if mode == "nt":
        b_spec = pl.BlockSpec((tn, tk), lambda j, i, kk, *_: (j, kk))
    else:
        b_spec = pl.BlockSpec((tk, tn), lambda j, i, kk, *_: (kk, j))
    dims = (_DIMS[mode], ((), ()))
    aliased = into is not None

    n_a = 1 if silu_of is None else 2
    n_in = n_a + 1 + aliased + (3 if norms is not None else 0)
    n_main = 1 + (2 if norms is not None else 0)

    def finish(refs, z):
        refs[n_in][...] = z.astype(out_dtype).reshape(refs[n_in].shape)
        if norms is not None:
            x_ref, gp_ref, gn_ref = refs[n_in - 3:n_in]
            r = lax.rsqrt(jnp.mean(z * z, axis=-1, keepdims=True) + EPS)
            xn = x_ref[...] + z * r * gp_ref[...]
            refs[n_in + 1][...] = xn
            r2 = lax.rsqrt(jnp.mean(xn * xn, axis=-1, keepdims=True) + EPS)
            refs[n_in + 2][...] = (xn * r2 * gn_ref[...]).astype(BF16)

    def body(*refs):
        lhs = refs[0][...]
        if silu_of is not None:
            gv = refs[1][...].astype(F32)
            lhs = (gv * _sigmoid(gv) * lhs.astype(F32)).astype(BF16)
            refs[n_in + n_main][...] = lhs
        p = lax.dot_general(lhs, refs[n_a][...], dims, preferred_element_type=F32)
        if nk == 1:
            finish(refs, p)
        else:
            acc = refs[-1]
            kk = pl.program_id(2)

            @pl.when(kk == 0)
            def _():
                acc[...] = p

            @pl.when(kk > 0)
            def _():
                acc[...] += p

            @pl.when(kk == nk - 1)
            def _():
                finish(refs, acc[...])

    if slab is None:
        out_spec = pl.BlockSpec((tm, tn), lambda j, i, kk: (i, j))
        out_shape = jax.ShapeDtypeStruct((m, n), out_dtype)
    else:
        shape3, lead, row0 = slab
        assert row0 % tm == 0 and shape3[2] == n
        out_spec = pl.BlockSpec((1, tm, tn), lambda j, i, kk: (lead, row0 // tm + i, j))
        out_shape = jax.ShapeDtypeStruct(shape3, out_dtype)
    in_specs, args = [a_spec, b_spec], [a, b]
    if silu_of is not None:
        assert mode == "nn" and tn == n and slab is None, "the left operand's blocks are written once each"
        in_specs, args = [a_spec, a_spec, b_spec], [a, silu_of, b]
    if aliased:
        in_specs.append(pl.BlockSpec(memory_space=pl.ANY))
        args.append(into)
    if norms is not None:
        assert tn == n and slab is None, "the fused norms need whole rows"
        row = pl.BlockSpec((tm, n), lambda j, i, kk: (i, 0))
        vec = pl.BlockSpec((1, n), lambda j, i, kk: (0, 0))
        in_specs += [row, vec, vec]
        args += list(norms)
        out_spec = [out_spec, row, row]
        out_shape = [out_shape, jax.ShapeDtypeStruct((m, n), F32), jax.ShapeDtypeStruct((m, n), BF16)]
    if silu_of is not None:
        out_spec = (out_spec if isinstance(out_spec, list) else [out_spec]) + [pl.BlockSpec((tm, tk), lambda j, i, kk: (i, kk))]
        out_shape = (out_shape if isinstance(out_shape, list) else [out_shape]) + [jax.ShapeDtypeStruct((m, k), BF16)]
    out, moved = _call(
        body, grid=(n // tn, m // tm, nk), in_specs=in_specs, out_specs=out_spec, out_shape=out_shape,
        scratch_shapes=[pltpu.VMEM((tm, tn), F32)] if nk > 1 else [], dims=("parallel", "parallel", "arbitrary"), name=name,
        args=args, aliases={n_a + 1: 0} if aliased else None, job=job)
    return out if job is None else (out, moved)


def _matmul_rows(pieces, b, rows, name, job=None, tm=1024, tk=1024):
    k, n = b.shape
    tm = math.gcd(tm, *[a.shape[1] for a in pieces])
    tk = _tile(k, tk)
    nk = k // tk
    spans, r0 = [], 0
    for a in pieces:
        assert a.shape[0] == k and a.shape[1] % tm == 0
        spans.append((r0, a.shape[1] // tm))
        r0 += a.shape[1] // tm
    nr = r0
    np_ = len(pieces)

    def body(*refs):
        b_ref, o_ref, acc = refs[np_], refs[np_ + 1], refs[-1]
        r, kk = pl.program_id(0), pl.program_id(1)
        for p, (first, count) in enumerate(spans):
            @pl.when((r >= first) & (r < first + count))
            def _(p=p):
                part = lax.dot_general(refs[p][...], b_ref[...], (_DIMS["tn"], ((), ())), preferred_element_type=F32)

                @pl.when(kk == 0)
                def _():
                    acc[...] = part

                @pl.when(kk > 0)
                def _():
                    acc[...] += part

        @pl.when(kk == nk - 1)
        def _():
            o_ref[0] = acc[...].astype(BF16)

    in_specs = []
    for first, count in spans:
        in_specs.append(pl.BlockSpec((tk, tm), lambda r, kk, f=first, c=count: (
            jnp.where(r < f, 0, jnp.where(r >= f + c, nk - 1, kk)), jnp.clip(r - f, 0, c - 1))))
    in_specs.append(pl.BlockSpec((tk, n), lambda r, kk: (kk, 0)))
    out, moved = _call(
        body, grid=(nr, nk), in_specs=in_specs, out_specs=pl.BlockSpec((1, tm, n), lambda r, kk: (0, r, 0)),
        out_shape=jax.ShapeDtypeStruct((1, rows, n), BF16), scratch_shapes=[pltpu.VMEM((tm, n), F32)],
        dims=("arbitrary", "arbitrary"), name=name, args=list(pieces) + [b], job=job)
    return out if job is None else (out, moved)


def _matmul_pieces(pieces, addend, name, tk, job=None, tm_cap=512):
    m = pieces[0][0].shape[0]
    n = pieces[0][1].shape[1]
    tm = _tile(m, tm_cap)
    spans, s0 = [], 0
    for a, b, row0 in pieces:
        assert a.shape[1] % tk == 0 and row0 % tk == 0 and b.shape[1] == n and a.shape[0] == m
        spans.append((s0, a.shape[1] // tk, row0 // tk))
        s0 += a.shape[1] // tk
    steps = s0
    np_ = len(pieces)
    groups = []
    for (a, b, _), (first, count, brow) in zip(pieces, spans):
        if groups and groups[-1][0] is b and groups[-1][3] + groups[-1][2] == brow:
            groups[-1][2] += count
        else:
            groups.append([b, first, count, brow])
    b_of = []
    for first, count, _ in spans:
        b_of.append(next(k for k, g in enumerate(groups) if g[1] <= first < g[1] + g[2]))
    ng = len(groups)

    nm = m // tm

    def body(*refs):
        o_ref, acc = refs[-2], refs[-1]
        s, i = pl.program_id(0), pl.program_id(1)
        rows = pl.ds(pl.multiple_of(i * tm, tm), tm)

        @pl.when(s == 0)
        def _():
            acc[rows, :] = refs[np_ + ng][...] if addend is not None else jnp.zeros((tm, n), F32)

        for p, (first, count, _) in enumerate(spans):
            @pl.when((s >= first) & (s < first + count))
            def _(p=p):
                acc[rows, :] += jnp.dot(refs[p][...], refs[np_ + b_of[p]][...], preferred_element_type=F32)

        @pl.when(s == steps - 1)
        def _():
            o_ref[...] = acc[rows, :]

    in_specs, args = [], []
    for (a, _, _), (first, count, _) in zip(pieces, spans):
        in_specs.append(pl.BlockSpec((tm, tk), lambda s, i, f=first, c=count: (
            jnp.where(s < f, 0, jnp.where(s >= f + c, nm - 1, i)), jnp.clip(s - f, 0, c - 1))))
        args.append(a)
    for b, first, count, brow in groups:
        in_specs.append(pl.BlockSpec((tk, n), lambda s, i, f=first, c=count, r=brow: (r + jnp.clip(s - f, 0, c - 1), 0)))
        args.append(b)
    if addend is not None:
        in_specs.append(pl.BlockSpec((tm, n), lambda s, i: (jnp.where(s == 0, i, nm - 1), 0)))
        args.append(addend)
    out, moved = _call(
        body, grid=(steps, nm), in_specs=in_specs,
        out_specs=pl.BlockSpec((tm, n), lambda s, i: (jnp.where(s == steps - 1, i, 0), 0)),
        out_shape=jax.ShapeDtypeStruct((m, n), F32), scratch_shapes=[pltpu.VMEM((m, n), F32)],
        dims=("arbitrary", "arbitrary"), name=name, args=args, job=job)
    return out if job is None else (out, moved)


def _norm_fwd(x, z, g_post, g_next, name):
    t, d = x.shape
    tt = _rows(t, 512)
    row = pl.BlockSpec((tt, d), lambda i: (i, 0))
    vec = pl.BlockSpec((1, d), lambda i: (0, 0))

    def body(*refs):
        if z is None:
            x_ref, gn_ref, h_ref = refs
            xn = x_ref[...]
        else:
            x_ref, z_ref, gp_ref, gn_ref, xo_ref, h_ref = refs
            zz = z_ref[...]
            r = lax.rsqrt(jnp.mean(zz * zz, axis=-1, keepdims=True) + EPS)
            xn = x_ref[...] + zz * r * gp_ref[...]
            xo_ref[...] = xn
        r2 = lax.rsqrt(jnp.mean(xn * xn, axis=-1, keepdims=True) + EPS)
        h_ref[...] = (xn * r2 * gn_ref[...]).astype(BF16)

    if z is None:
        return pl.pallas_call(
            body, grid=(t // tt,), in_specs=[row, vec], out_specs=row,
            out_shape=jax.ShapeDtypeStruct((t, d), BF16), compiler_params=_cparams(("parallel",)), name=name,
        )(x, g_next)
    return pl.pallas_call(
        body, grid=(t // tt,), in_specs=[row, row, vec, vec], out_specs=[row, row],
        out_shape=[jax.ShapeDtypeStruct((t, d), F32), jax.ShapeDtypeStruct((t, d), BF16)],
        compiler_params=_cparams(("parallel",)), name=name,
    )(x, z, g_post, g_next)


def _rms_bwd(dy, x, g):
    r = lax.rsqrt(jnp.mean(x * x, axis=-1, keepdims=True) + EPS)
    n = x * r
    dn = dy * g
    dx = r * (dn - n * jnp.mean(dn * n, axis=-1, keepdims=True))
    return dx, dy * n


def _norm_bwd(dres, pre, post, name):
    t, d = dres.shape
    tt = _rows(t, 512)
    nt = t // tt
    row = pl.BlockSpec((tt, d), lambda i: (i, 0))
    vec = pl.BlockSpec((1, d), lambda i: (0, 0))
    has_pre, has_post = pre is not None, post is not None
    n_in = 1 + (3 if has_pre else 0) + (2 if has_post else 0)
    n_out = has_pre + has_post + has_pre + has_post

    def body(*refs):
        ins, outs, scr = refs[:n_in], refs[n_in:n_in + n_out], refs[n_in + n_out:]
        i = pl.program_id(0)
        dx = ins[0][...]
        pos, opos, spos = 1, 0, 0
        accs = []
        if has_pre:
            dh_ref, xa_ref, ga_ref = ins[pos:pos + 3]
            pos += 3
            dxa, dga_t = _rms_bwd(dh_ref[...], xa_ref[...], ga_ref[...])
            dx = dx + dxa
            outs[opos][...] = dx
            opos += 1
            accs.append((scr[spos], dga_t))
            spos += 1
        if has_post:
            zb_ref, gb_ref = ins[pos:pos + 2]
            dz, dgb_t = _rms_bwd(dx, zb_ref[...], gb_ref[...])
            outs[opos][...] = dz.astype(BF16)
            opos += 1
            accs.append((scr[spos], dgb_t))
            spos += 1
        for (acc, val), out in zip(accs, outs[opos:]):
            part = _sum8(val)

            @pl.when(i == 0)
            def _(acc=acc, part=part):
                acc[...] = part

            @pl.when(i > 0)
            def _(acc=acc, part=part):
                acc[...] += part

            @pl.when(i == nt - 1)
            def _(acc=acc, out=out):
                out[...] = jnp.sum(acc[...], axis=0, keepdims=True)

    in_specs, args = [row], [dres]
    out_specs, out_shape = [], []
    if has_pre:
        in_specs += [row, row, vec]
        args += list(pre)
        out_specs.append(row)
        out_shape.append(jax.ShapeDtypeStruct((t, d), F32))
    if has_post:
        in_specs += [row, vec]
        args += list(post)
        out_specs.append(row)
        out_shape.append(jax.ShapeDtypeStruct((t, d), BF16))
    for _ in range(has_pre + has_post):
        out_specs.append(vec)
        out_shape.append(jax.ShapeDtypeStruct((1, d), F32))
    return pl.pallas_call(
        body, grid=(nt,), in_specs=in_specs, out_specs=out_specs, out_shape=out_shape,
        scratch_shapes=[pltpu.VMEM((SUBLANE, d), F32)] * (has_pre + has_post),
        compiler_params=_cparams(("arbitrary",)), name=name,
    )(*args)


def _loss_grad(y, target, name):
    t, d = y.shape
    tt = _rows(t, 512)
    nt = t // tt
    row = pl.BlockSpec((tt, d), lambda i: (i, 0))
    inv_d = 1.0 / d

    def body(y_ref, t_ref, dy_ref, l_ref):
        i = pl.program_id(0)
        diff = y_ref[...] - t_ref[...]
        dy_ref[...] = diff * inv_d
        s8 = _sum8(diff * diff)
        part = s8[:, 0:LANE]
        for k in range(1, d // LANE):
            part = part + s8[:, k * LANE:(k + 1) * LANE]
        part = part * (0.5 * inv_d)

        @pl.when(i == 0)
        def _():
            l_ref[...] = part

        @pl.when(i > 0)
        def _():
            l_ref[...] += part

    return pl.pallas_call(
        body, grid=(nt,), in_specs=[row, row],
        out_specs=[row, pl.BlockSpec((SUBLANE, LANE), lambda i: (0, 0))],
        out_shape=[jax.ShapeDtypeStruct((t, d), F32), jax.ShapeDtypeStruct((SUBLANE, LANE), F32)],
        compiler_params=_cparams(("arbitrary",)), name=name,
    )(y, target)


def _swiglu_bwd(a, b, dm, name):
    t, f = a.shape
    tt = _rows(t, 256)
    blk = pl.BlockSpec((tt, f), lambda i: (i, 0))

    def body(a_ref, b_ref, dm_ref, da_ref, db_ref):
        av = a_ref[...].astype(F32)
        s = _sigmoid(av)
        dv = dm_ref[...].astype(F32)
        da_ref[...] = (dv * b_ref[...].astype(F32) * s * (1.0 + av * (1.0 - s))).astype(BF16)
        db_ref[...] = (dv * av * s).astype(BF16)

    return pl.pallas_call(
        body, grid=(t // tt,), in_specs=[blk, blk, blk], out_specs=[blk, blk],
        out_shape=[jax.ShapeDtypeStruct((t, f), BF16)] * 2, compiler_params=_cparams(("parallel",)), name=name,
    )(a, b, dm)


def _log_sigmoid(x):
    return jnp.minimum(x, 0.0) - jnp.log1p(jnp.exp(-jnp.abs(x)))


def _fox_prep(f_t, b_f, name):
    h, t = f_t.shape

    def body(f_ref, b_ref, c_ref):
        r = lax.broadcasted_iota(jnp.int32, (LANE, LANE), 0)
        c = lax.broadcasted_iota(jnp.int32, (LANE, LANE), 1)
        upper = (r <= c).astype(F32)
        carry = jnp.zeros((h, 1), F32)
        for j in range(t // LANE):
            sl = slice(j * LANE, (j + 1) * LANE)
            lf = _log_sigmoid(f_ref[:, sl] + b_ref[...])
            cs = jnp.dot(lf, upper, precision=lax.Precision.HIGHEST, preferred_element_type=F32) + carry
            c_ref[:, sl] = cs
            carry = cs[:, LANE - 1:LANE]

    return pl.pallas_call(body, out_shape=jax.ShapeDtypeStruct((h, t), F32), compiler_params=_cparams(), name=name)(f_t, b_f)


def _fox_bwd(dc_q, dc_k, f_t, b_f, name):
    h, t = f_t.shape

    def body(dq_ref, dk_ref, f_ref, b_ref, df_ref, db_ref):
        r = lax.broadcasted_iota(jnp.int32, (LANE, LANE), 0)
        c = lax.broadcasted_iota(jnp.int32, (LANE, LANE), 1)
        lower = (r >= c).astype(F32)
        carry = jnp.zeros((h, 1), F32)
        dbsum = jnp.zeros((h, 1), F32)
        for j in reversed(range(t // LANE)):
            sl = slice(j * LANE, (j + 1) * LANE)
            dc = dq_ref[:, sl] - dk_ref[:, sl]
            dl = jnp.dot(dc, lower, precision=lax.Precision.HIGHEST, preferred_element_type=F32) + carry
            carry = dl[:, 0:1]
            df = dl * _sigmoid_small(-(f_ref[:, sl] + b_ref[...]))
            df_ref[:, sl] = df
            dbsum = dbsum + jnp.sum(df, axis=-1, keepdims=True)
        db_ref[...] = dbsum

    return pl.pallas_call(
        body, out_shape=[jax.ShapeDtypeStruct((h, t), F32), jax.ShapeDtypeStruct((h, 1), F32)],
        compiler_params=_cparams(), name=name,
    )(dc_q, dc_k, f_t, b_f)


ATTN_FWD = (1024, 512)
ATTN_BWD = (512, 512)


def _attn_tiles(t, tiles):
    return _tile(t, tiles[0]), _tile(t, tiles[1])


def _attn_fwd(proj, c_t, d, name, job=None):
    t = proj.shape[0]
    h = d // LANE
    bq, bk = _attn_tiles(t, ATTN_FWD)
    nq, nk, rr = t // bq, t // bk, bq // bk
    qc, kc, vc = COL_Q * h, COL_K * h, COL_VA * h
    qscale = LANE ** -0.5 * LOG2E

    def body(q_ref, k_ref, v_ref, cc_ref, cr_ref, o_ref, lse_ref, kb, vt, ckb, acc):
        i = pl.program_id(1)

        @pl.when(i == 0)
        def _():
            kb[...] = k_ref[...].astype(BF16)
            ckb[...] = jnp.broadcast_to(cc_ref[0] * LOG2E, (t, bq))
            for jn in range(nk):
                vt[jn] = v_ref[jn * bk:(jn + 1) * bk, :].astype(F32).T.astype(BF16)

        q = (q_ref[...].astype(F32) * qscale).astype(BF16)
        cq = cr_ref[0, 0] * LOG2E
        acc[...] = jnp.zeros((LANE, bq), F32)

        def block(j, diag, m_old, l_old):
            off = 0 if diag is None else diag * bk
            w = bq - off
            rows = pl.ds(pl.multiple_of(j * bk, bk), bk)
            s = _nt_dot(kb[rows, :], q[off:, :]) - ckb[rows, off:]
            if diag is not None:
                kk = lax.broadcasted_iota(jnp.int32, (bk, w), 0)
                qq = lax.broadcasted_iota(jnp.int32, (bk, w), 1)
                s = jnp.where(qq >= kk, s, NEG)
            cqs, m_part, l_part = cq[:, off:], m_old[:, off:], l_old[:, off:]
            m_new = jnp.maximum(m_part, jnp.max(s, axis=0, keepdims=True) + cqs)
            p = jnp.exp2(s + (cqs - m_new))
            alpha = jnp.exp2(m_part - m_new)
            l_new = alpha * l_part + jnp.sum(p, axis=0, keepdims=True)
            acc[:, off:] = alpha * acc[:, off:] + jnp.dot(vt[j], p.astype(BF16), preferred_element_type=F32)
            if off:
                m_new = jnp.concatenate([m_old[:, :off], m_new], axis=1)
                l_new = jnp.concatenate([l_old[:, :off], l_new], axis=1)
            return m_new, l_new

        m, l = lax.fori_loop(0, i * rr, lambda j, c: block(j, None, *c),
                             (jnp.full((1, bq), NEG, F32), jnp.zeros((1, bq), F32)))
        for jj in range(rr):
            m, l = block(i * rr + jj, jj, m, l)
        o_ref[...] = (acc[...] / l).T
        lse_ref[0, 0] = m + jnp.log2(l)

    rowq = pl.BlockSpec((1, 1, 1, bq), lambda hh, i: (hh, i, 0, 0))
    outs, moved = _call(
        body, grid=(h, nq),
        in_specs=[
            pl.BlockSpec((bq, LANE), lambda hh, i: (i, qc + hh)),
            pl.BlockSpec((t, LANE), lambda hh, i: (0, kc + hh)),
            pl.BlockSpec((t, LANE), lambda hh, i: (0, vc + hh)),
            pl.BlockSpec((1, t, 1), lambda hh, i: (hh, 0, 0)),
            rowq,
        ],
        out_specs=[pl.BlockSpec((bq, LANE), lambda hh, i: (i, hh)), rowq],
        out_shape=[jax.ShapeDtypeStruct((t, d), F32), jax.ShapeDtypeStruct((h, nq, 1, bq), F32)],
        scratch_shapes=[pltpu.VMEM((t, LANE), BF16), pltpu.VMEM((nk, LANE, bk), BF16), pltpu.VMEM((t, bq), F32),
                        pltpu.VMEM((LANE, bq), F32)],
        dims=("arbitrary", "arbitrary"), name=name,
        args=[proj, proj, proj, c_t.reshape(h, t, 1), c_t.reshape(h, nq, 1, bq)], job=job)
    outs = [outs[0], outs[1].reshape(h, t)]
    return outs if job is None else (outs, moved)


def _attn_bwd(proj, do, o, lse, c_t, d, name, job=None):
    t = proj.shape[0]
    h = d // LANE
    bq, bk = _attn_tiles(t, ATTN_BWD)
    nq, nk, rr = t // bq, t // bk, bq // bk
    qc, kc, vc = COL_Q * h, COL_K * h, COL_VA * h
    scale = LANE ** -0.5

    def body(q_ref, k_ref, v_ref, do_ref, o_ref, lse_ref, cc_ref, cr_ref, dq_ref, dk_ref, dv_ref, dcq_ref, dck_ref,
             kb, kt, vb, ckb, dk_acc, dv_acc, dck_acc, dqt_acc):
        i = pl.program_id(1)

        @pl.when(i == 0)
        def _():
            kb[...] = k_ref[...].astype(BF16)
            vb[...] = v_ref[...].astype(BF16)
            ckb[...] = jnp.broadcast_to(cc_ref[0] * LOG2E, (t, bq))
            for jn in range(nk):
                kt[jn] = k_ref[jn * bk:(jn + 1) * bk, :].astype(F32).T.astype(BF16)
            dk_acc[...] = jnp.zeros((t, LANE), F32)
            dv_acc[...] = jnp.zeros((t, LANE), F32)
            dck_acc[...] = jnp.zeros((t, LANE), F32)

        q = (q_ref[...].astype(F32) * (scale * LOG2E)).astype(BF16)
        dof = do_ref[...]
        dob = dof.astype(BF16)
        delta = jnp.sum((dof * o_ref[...]).T, axis=0, keepdims=True)
        rowb = cr_ref[0, 0] * LOG2E - lse_ref[0, 0]
        dqt_acc[...] = jnp.zeros((LANE, bq), F32)

        def block(j, diag, dcq):
            rows = pl.ds(pl.multiple_of(j * bk, bk), bk)
            p = jnp.exp2(_nt_dot(kb[rows, :], q) - ckb[rows, :] + rowb)
            if diag is not None:
                kk = lax.broadcasted_iota(jnp.int32, (bk, bq), 0)
                qq = lax.broadcasted_iota(jnp.int32, (bk, bq), 1)
                p = jnp.where(qq >= kk + diag * bk, p, 0.0)
            dv_acc[rows, :] += jnp.dot(p.astype(BF16), dob, preferred_element_type=F32)
            ds = p * (_nt_dot(vb[rows, :], dob) - delta)
            dsb = ds.astype(BF16)
            dk_acc[rows, :] += jnp.dot(dsb, q, preferred_element_type=F32)
            dqt_acc[...] += jnp.dot(kt[j], dsb, preferred_element_type=F32)
            part = ds[:, 0:LANE]
            for k in range(1, bq // LANE):
                part = part + ds[:, k * LANE:(k + 1) * LANE]
            dck_acc[rows, :] += part
            return dcq + jnp.sum(ds, axis=0, keepdims=True)

        dcq = lax.fori_loop(0, i * rr, lambda j, c: block(j, None, c), jnp.zeros((1, bq), F32))
        for jj in range(rr):
            dcq = block(i * rr + jj, jj, dcq)
        dq_ref[...] = (dqt_acc[...] * scale).T.astype(BF16)
        dcq_ref[0, 0] = dcq

        @pl.when(i == nq - 1)
        def _():
            dk_ref[...] = (dk_acc[...] * LN2).astype(BF16)
            dv_ref[...] = dv_acc[...].astype(BF16)
            dck_ref[0] = jnp.sum(dck_acc[...], axis=-1, keepdims=True)

    rowq = pl.BlockSpec((1, 1, 1, bq), lambda hh, i: (hh, i, 0, 0))
    blk = pl.BlockSpec((bq, LANE), lambda hh, i: (i, hh))
    whole = pl.BlockSpec((t, LANE), lambda hh, i: (0, hh))
    colk = pl.BlockSpec((1, t, 1), lambda hh, i: (hh, 0, 0))
    outs, moved = _call(
        body, grid=(h, nq),
        in_specs=[
            pl.BlockSpec((bq, LANE), lambda hh, i: (i, qc + hh)),
            pl.BlockSpec((t, LANE), lambda hh, i: (0, kc + hh)),
            pl.BlockSpec((t, LANE), lambda hh, i: (0, vc + hh)),
            blk, blk, rowq, colk, rowq,
        ],
        out_specs=[blk, whole, whole, rowq, colk],
        out_shape=[jax.ShapeDtypeStruct((t, d), BF16), jax.ShapeDtypeStruct((t, d), BF16), jax.ShapeDtypeStruct((t, d), BF16),
                   jax.ShapeDtypeStruct((h, nq, 1, bq), F32), jax.ShapeDtypeStruct((h, t, 1), F32)],
        scratch_shapes=[pltpu.VMEM((t, LANE), BF16), pltpu.VMEM((nk, LANE, bk), BF16), pltpu.VMEM((t, LANE), BF16),
                        pltpu.VMEM((t, bq), F32), pltpu.VMEM((t, LANE), F32), pltpu.VMEM((t, LANE), F32),
                        pltpu.VMEM((t, LANE), F32), pltpu.VMEM((LANE, bq), F32)],
        dims=("arbitrary", "arbitrary"), name=name,
        args=[proj, proj, proj, do, o, lse.reshape(h, nq, 1, bq), c_t.reshape(h, t, 1), c_t.reshape(h, nq, 1, bq)], job=job)
    outs = list(outs[:3]) + [outs[3].reshape(h, t), outs[4].reshape(h, t)]
    return outs if job is None else (outs, moved)


def _sgu_forward(u_ref, v_ref, gv_ref, wm_ref, bs_ref, mix_sc, groups):
    gu, dgu = _gelu_and_grad(u_ref[...].astype(F32))
    gvv, dgv = _gelu_and_grad(v_ref[...].astype(F32))
    mu = jnp.mean(gvv, axis=-1, keepdims=True)
    xc = gvv - mu
    r = lax.rsqrt(jnp.mean(xc * xc, axis=-1, keepdims=True) + EPS)
    nhat = xc * r
    vn = (nhat * gv_ref[...]).astype(BF16)
    for g in range(groups):
        sl = slice(g * LANE, (g + 1) * LANE)
        mix_sc[:, sl] = jnp.dot(wm_ref[g], vn[:, sl], preferred_element_type=F32) + bs_ref[g]
    return gu, dgu, dgv, nhat, r, vn, mix_sc[...]


def _mix_fwd(proj, o, wm, bs, g_v, d, name):
    t = proj.shape[0]
    groups = d // LANE

    def body(u_ref, v_ref, ga_ref, gb_ref, o_ref, wm_ref, bs_ref, gv_ref, out_ref, mix_sc):
        gu, _, _, _, _, _, mixed = _sgu_forward(u_ref, v_ref, gv_ref, wm_ref, bs_ref, mix_sc, groups)
        out_ref[...] = (_sigmoid(ga_ref[...].astype(F32)) * (gu * mixed) + _sigmoid(gb_ref[...].astype(F32)) * o_ref[...]).astype(BF16)

    def colblk(k):
        return pl.BlockSpec((LANE, d), lambda i, k=k: (i, k))

    full3 = pl.BlockSpec((groups, LANE, LANE), lambda i: (0, 0, 0))
    return pl.pallas_call(
        body, grid=(t // LANE,),
        in_specs=[colblk(COL_U), colblk(COL_V), colblk(COL_GA), colblk(COL_GB), colblk(0), full3,
                  pl.BlockSpec((groups, LANE, 1), lambda i: (0, 0, 0)), pl.BlockSpec((1, d), lambda i: (0, 0))],
        out_specs=colblk(0),
        out_shape=jax.ShapeDtypeStruct((t, d), BF16),
        scratch_shapes=[pltpu.VMEM((LANE, d), F32)],
        compiler_params=_cparams(("parallel",)), name=name,
    )(proj, proj, proj, proj, o, wm, bs, g_v)


def _mix_bwd(dmerged, proj, o, wm, wm_t, bs, g_v, d, name, job=None):
    t = proj.shape[0]
    groups = d // LANE
    nt = t // LANE

    def body(dm_ref, u_ref, v_ref, ga_ref, gb_ref, o_ref, wm_ref, wmt_ref, bs_ref, gv_ref,
             duv_ref, dg_ref, do_ref, dws_ref, dbs_ref, dgv_ref, mix_sc, dvn_sc, gv_acc):
        i = pl.program_id(0)

        @pl.when(i == 0)
        def _():
            dws_ref[...] = jnp.zeros_like(dws_ref)
            dbs_ref[...] = jnp.zeros_like(dbs_ref)
            gv_acc[...] = jnp.zeros_like(gv_acc)

        gu, dgu, dgv, nhat, r, vn, mixed = _sgu_forward(u_ref, v_ref, gv_ref, wm_ref, bs_ref, mix_sc, groups)
        dm = dm_ref[...]
        sa = _sigmoid(ga_ref[...].astype(F32))
        sb = _sigmoid(gb_ref[...].astype(F32))
        ov = o_ref[...]
        y_a = gu * mixed
        dg_ref[:, 0:d] = (dm * y_a * sa * (1.0 - sa)).astype(BF16)
        dg_ref[:, d:2 * d] = (dm * ov * sb * (1.0 - sb)).astype(BF16)
        do_ref[...] = dm * sb
        dy_a = dm * sa
        duv_ref[:, 0:d] = (dy_a * mixed * dgu).astype(BF16)
        dmixed = dy_a * gu
        dmixed_b = dmixed.astype(BF16)
        for g in range(groups):
            sl = slice(g * LANE, (g + 1) * LANE)
            dvn_sc[:, sl] = jnp.dot(wmt_ref[g], dmixed_b[:, sl], preferred_element_type=F32)
            dws_ref[g] += _nt_dot(dmixed_b[:, sl], vn[:, sl])
            dbs_ref[g] += jnp.sum(dmixed[:, sl], axis=-1, keepdims=True)
        dvn = dvn_sc[...]
        gv_acc[...] += _sum8(dvn * nhat)
        dn = dvn * gv_ref[...]
        dgelu = r * (dn - jnp.mean(dn, axis=-1, keepdims=True) - nhat * jnp.mean(dn * nhat, axis=-1, keepdims=True))
        duv_ref[:, d:2 * d] = (dgelu * dgv).astype(BF16)

        @pl.when(i == nt - 1)
        def _():
            dgv_ref[...] = jnp.sum(gv_acc[...], axis=0, keepdims=True)
            rr = lax.broadcasted_iota(jnp.int32, (LANE, LANE), 0)
            cl = lax.broadcasted_iota(jnp.int32, (LANE, LANE), 1)
            for g in range(groups):
                dws_ref[g] = jnp.where(rr >= cl, dws_ref[g], 0.0)

    def colblk(k):
        return pl.BlockSpec((LANE, d), lambda i, k=k: (i, k))

    full3 = pl.BlockSpec((groups, LANE, LANE), lambda i: (0, 0, 0))
    col3 = pl.BlockSpec((groups, LANE, 1), lambda i: (0, 0, 0))
    vec = pl.BlockSpec((1, d), lambda i: (0, 0))
    two = pl.BlockSpec((LANE, 2 * d), lambda i: (i, 0))
    outs, moved = _call(
        body, grid=(nt,),
        in_specs=[colblk(0), colblk(COL_U), colblk(COL_V), colblk(COL_GA), colblk(COL_GB), colblk(0), full3, full3, col3, vec],
        out_specs=[two, two, colblk(0), full3, col3, vec],
        out_shape=[jax.ShapeDtypeStruct((t, 2 * d), BF16), jax.ShapeDtypeStruct((t, 2 * d), BF16), jax.ShapeDtypeStruct((t, d), F32),
                   jax.ShapeDtypeStruct((groups, LANE, LANE), F32), jax.ShapeDtypeStruct((groups, LANE, 1), F32),
                   jax.ShapeDtypeStruct((1, d), F32)],
        scratch_shapes=[pltpu.VMEM((LANE, d), F32), pltpu.VMEM((LANE, d), F32), pltpu.VMEM((SUBLANE, d), F32)],
        dims=("arbitrary",), name=name, args=[dmerged, proj, proj, proj, proj, o, wm, wm_t, bs, g_v], job=job)
    return outs if job is None else (outs, moved)


def _adam_math(w, g, m, v):
    nm = ADAM_B1 * m + (1.0 - ADAM_B1) * g
    nv = ADAM_B2 * v + (1.0 - ADAM_B2) * (g * g)
    delta = -ADAM_LR * ((nm * ADAM_C1) / (jnp.sqrt(nv * ADAM_C2) + ADAM_EPS) + ADAM_WD * w)
    return delta, nm, nv


def _adamw(w, g, m, v, name):
    r, c = w.shape
    cap = max(SUBLANE, (2 * 1024 * 1024) // (4 * c) // SUBLANE * SUBLANE)
    tr = _rows(r, cap)

    def body(w_ref, g_ref, m_ref, v_ref, d_ref, nm_ref, nv_ref):
        d_ref[...], nm_ref[...], nv_ref[...] = _adam_math(w_ref[...], g_ref[...], m_ref[...], v_ref[...])

    blk = pl.BlockSpec((tr, c), lambda i: (i, 0))
    return pl.pallas_call(
        body, grid=(r // tr,), in_specs=[blk] * 4, out_specs=[blk] * 3,
        out_shape=[jax.ShapeDtypeStruct((r, c), F32)] * 3, compiler_params=_cparams(("parallel",)), name=name,
    )(w, g, m, v)


def _adamw_layers(w, g0, g1, m, v, name):
    _, r, c = w.shape
    cap = max(SUBLANE, (1024 * 1024) // (4 * c) // SUBLANE * SUBLANE)
    tr = _rows(r, cap)

    def body(w_ref, g0_ref, g1_ref, m_ref, v_ref, g_ref, d_ref, nm_ref, nv_ref):
        gg = jnp.where(pl.program_id(0) == 0, g0_ref[...], g1_ref[...])
        g_ref[0] = gg
        d_ref[0], nm_ref[0], nv_ref[0] = _adam_math(w_ref[0], gg, m_ref[0], v_ref[0])

    lay = pl.BlockSpec((1, tr, c), lambda l, i: (l, i, 0))

    def gspec(l0):
        return pl.BlockSpec((tr, c), lambda l, i: (jnp.where(l == l0, i, 0), 0))

    return pl.pallas_call(
        body, grid=(2, r // tr), in_specs=[lay, gspec(0), gspec(1), lay, lay], out_specs=[lay] * 4,
        out_shape=[jax.ShapeDtypeStruct((2, r, c), F32)] * 4, compiler_params=_cparams(("arbitrary", "arbitrary")), name=name,
    )(w, g0, g1, m, v)


def _adamw_interleaved(w, g0, g1, m, v, name):
    r, _, c = w.shape
    tr = 128

    def body(w_ref, g0_ref, g1_ref, m_ref, v_ref, g_ref, d_ref, nm_ref, nv_ref):
        for l, gl in enumerate((g0_ref, g1_ref)):
            gg = gl[...]
            g_ref[:, l, :] = gg
            d_ref[:, l, :], nm_ref[:, l, :], nv_ref[:, l, :] = _adam_math(w_ref[:, l, :], gg, m_ref[:, l, :], v_ref[:, l, :])

    lay = pl.BlockSpec((tr, 2, c), lambda i: (i, 0, 0))
    flat = pl.BlockSpec((tr, c), lambda i: (i, 0))
    return pl.pallas_call(
        body, grid=(pl.cdiv(r, tr),), in_specs=[lay, flat, flat, lay, lay], out_specs=[lay] * 4,
        out_shape=[jax.ShapeDtypeStruct((r, 2, c), F32)] * 4, compiler_params=_cparams(("parallel",)), name=name,
    )(w, g0, g1, m, v)


def _add_half(p4, recv, c_idx, name):
    _, r, c = p4.shape
    hw = c // 2
    tr = 256 if r % 256 == 0 else r

    def body(c_ref, a_ref, b_ref, o_ref):
        o_ref[...] = (a_ref[...].astype(F32) + b_ref[...].astype(F32)).astype(BF16)

    return pl.pallas_call(
        body,
        grid_spec=pltpu.PrefetchScalarGridSpec(
            num_scalar_prefetch=1, grid=(N_CHIPS, pl.cdiv(r, tr)),
            in_specs=[pl.BlockSpec((1, tr, hw), lambda s, i, cr: (s, i, cr[0])), pl.BlockSpec((1, tr, hw), lambda s, i, cr: (s, i, 0))],
            out_specs=pl.BlockSpec((1, tr, hw), lambda s, i, cr: (s, i, 0)),
        ),
        out_shape=jax.ShapeDtypeStruct((N_CHIPS, r, hw), BF16), compiler_params=_cparams(("parallel", "parallel")), name=name,
    )(c_idx, p4, recv)


def _sum_slots(x, own, sel, name, out_cols=None):
    s, r, c = x.shape
    tr = 128 if r % 128 == 0 else r

    def body(sel_ref, x_ref, own_ref, o_ref):
        mine = own_ref[0].astype(F32)
        acc = jnp.zeros((tr, c), F32)
        for k in range(s):
            acc = acc + jnp.where(sel_ref[0] == k, mine, x_ref[k].astype(F32))
        o_ref[...] = acc

    return pl.pallas_call(
        body,
        grid_spec=pltpu.PrefetchScalarGridSpec(
            num_scalar_prefetch=1, grid=(pl.cdiv(r, tr),),
            in_specs=[pl.BlockSpec((s, tr, c), lambda i, sr: (0, i, 0)), pl.BlockSpec((1, tr, c), lambda i, sr: (sr[1], i, 0))],
            out_specs=pl.BlockSpec((tr, c), lambda i, sr: (i, sr[2])),
        ),
        out_shape=jax.ShapeDtypeStruct((r, out_cols or c), F32), compiler_params=_cparams(("parallel",)), name=name,
    )(sel, x, own)


def _half_cols(width, hc):
    hw = width // 2
    assert hw % LANE == 0
    return pl.ds(pl.multiple_of(hc * hw, LANE), hw)


def _remote(src, dst, ssem, rsem, k, to):
    return pltpu.make_async_remote_copy(src_ref=src, dst_ref=dst, send_sem=ssem.at[k], recv_sem=rsem.at[k], device_id=to,
                                        device_id_type=MESH)


def _gather_job(bufs, mid_at=0.5):
    def part(o, a, slot, hc):
        return o[a].at[slot, :, _half_cols(bufs[a].shape[2], hc)]

    def first(ins, o, fresh, ssem, rsem):
        x, y, c, chips = _place()
        for a in range(len(bufs)):
            mine = part(o, a, 2 * x + y, c)
            for j, chip in enumerate(chips):
                _remote(mine, mine, ssem, rsem, 6 * a + j, (chip[0], chip[1], c)).start()

    def mid(ins, o, fresh, ssem, rsem):
        x, y, c, chips = _place()
        for a in range(len(bufs)):
            for j, chip in enumerate(chips):
                got = part(o, a, 2 * chip[0] + chip[1], c)
                _remote(got, got, ssem, rsem, 6 * a + j, (x, y, c)).wait_recv()
                _remote(got, got, ssem, rsem, 6 * a + 3 + j, (x, y, 1 - c)).start()

    def last(ins, o, fresh, ssem, rsem):
        x, y, c, chips = _place()
        for a in range(len(bufs)):
            for j, chip in enumerate(chips):
                got = part(o, a, 2 * chip[0] + chip[1], 1 - c)
                _remote(got, got, ssem, rsem, 6 * a + 3 + j, (x, y, c)).wait_recv()
        for a in range(len(bufs)):
            mine = part(o, a, 2 * x + y, c)
            for j, chip in enumerate(chips):
                _remote(mine, mine, ssem, rsem, 6 * a + j, (x, y, c)).wait_send()
                passed = part(o, a, 2 * chip[0] + chip[1], c)
                _remote(passed, passed, ssem, rsem, 6 * a + 3 + j, (x, y, c)).wait_send()

    return _Job([], bufs, [], 6 * len(bufs), first, mid, last, mid_at)


def _swap_job(p4s):
    def pairs(ins, fresh, c):
        return [(a, s, ins[a].at[s, :, _half_cols(p4s[a].shape[2], 1 - c)], fresh[a].at[s])
                for a in range(len(p4s)) for s in range(N_CHIPS)]

    def first(ins, inout, fresh, ssem, rsem):
        x, y, c, _ = _place()
        for a, s, src, dst in pairs(ins, fresh, c):
            _remote(src, dst, ssem, rsem, N_CHIPS * a + s, (x, y, 1 - c)).start()

    def last(ins, inout, fresh, ssem, rsem):
        x, y, c, _ = _place()
        for a, s, src, dst in pairs(ins, fresh, c):
            _remote(src, dst, ssem, rsem, N_CHIPS * a + s, (x, y, 1 - c)).wait()

    fresh = [jax.ShapeDtypeStruct(p.shape[:2] + (p.shape[2] // 2,), p.dtype) for p in p4s]
    return _Job(p4s, [], fresh, N_CHIPS * len(p4s), first, None, last)


def _scatter_job(parts):
    def first(ins, inout, fresh, ssem, rsem):
        x, y, c, chips = _place()
        for a in range(len(parts)):
            for j, chip in enumerate(chips):
                _remote(ins[a].at[2 * chip[0] + chip[1]], fresh[a].at[2 * x + y], ssem, rsem, 3 * a + j, (chip[0], chip[1], c)).start()

    def last(ins, inout, fresh, ssem, rsem):
        x, y, c, chips = _place()
        for a in range(len(parts)):
            for j, chip in enumerate(chips):
                slot = 2 * chip[0] + chip[1]
                _remote(ins[a].at[slot], fresh[a].at[slot], ssem, rsem, 3 * a + j, (x, y, c)).wait()

    return _Job(parts, [], [jax.ShapeDtypeStruct(p.shape, p.dtype) for p in parts], 3 * len(parts), first, None, last)


def _share_job(gs):
    def halves(o, a, c):
        width = gs[a].shape[1]
        return o[a].at[:, _half_cols(width, c)], o[a].at[:, _half_cols(width, 1 - c)]

    def first(ins, o, fresh, ssem, rsem):
        x, y, c, _ = _place()
        for a in range(len(gs)):
            mine, _ = halves(o, a, c)
            _remote(mine, mine, ssem, rsem, a, (x, y, 1 - c)).start()

    def last(ins, o, fresh, ssem, rsem):
        x, y, c, _ = _place()
        for a in range(len(gs)):
            mine, theirs = halves(o, a, c)
            _remote(mine, theirs, ssem, rsem, a, (x, y, 1 - c)).wait()

    return _Job([], gs, [], len(gs), first, None, last)


def _gather_all_job(buf):
    def peers():
        x, y, c, _ = _place()
        flips = [(fx, fy, fc) for fx in (0, 1) for fy in (0, 1) for fc in (0, 1)][1:]
        return (x, y, c), [((1 - x) if fx else x, (1 - y) if fy else y, (1 - c) if fc else c) for fx, fy, fc in flips]

    def first(ins, inout, fresh, ssem, rsem):
        (x, y, c), others = peers()
        for k, peer in enumerate(others):
            _remote(ins[0], fresh[0].at[4 * x + 2 * y + c], ssem, rsem, k, peer).start()

    def last(ins, inout, fresh, ssem, rsem):
        me, others = peers()
        for k, peer in enumerate(others):
            _remote(ins[0], fresh[0].at[4 * peer[0] + 2 * peer[1] + peer[2]], ssem, rsem, k, me).wait()

    return _Job([buf], [], [jax.ShapeDtypeStruct((N_DEV,) + buf.shape, buf.dtype)], N_DEV - 1, first, None, last)


class _SemView:
    def __init__(self, sems, off):
        self.sems, self.off = sems, off

    @property
    def at(self):
        return self

    def __getitem__(self, k):
        return self.sems.at[k + self.off]


def _join(jobs):
    spans, pos = [], [0, 0, 0, 0]
    for j in jobs:
        nxt = [pos[0] + len(j.ins), pos[1] + len(j.inout), pos[2] + len(j.fresh), pos[3] + j.nsem]
        spans.append((pos, nxt))
        pos = nxt

    def hook(which):
        fns = [getattr(j, which) for j in jobs]
        if all(f is None for f in fns):
            return None

        def run(ins, inout, fresh, ssem, rsem):
            for fn, (lo, hi) in zip(fns, spans):
                if fn is not None:
                    fn(ins[lo[0]:hi[0]], inout[lo[1]:hi[1]], fresh[lo[2]:hi[2]], _SemView(ssem, lo[3]), _SemView(rsem, lo[3]))

        return run

    mids = [j.mid_at for j in jobs if j.mid is not None]
    joined = _Job([a for j in jobs for a in j.ins], [a for j in jobs for a in j.inout], [a for j in jobs for a in j.fresh],
                  pos[3], hook("first"), hook("mid"), hook("last"), max(mids) if mids else 0.5)
    n_io = pos[1]

    def split(moved):
        return [list(moved[lo[1]:hi[1]]) + list(moved[n_io + lo[2]:n_io + hi[2]]) for lo, hi in spans]

    return joined, split


def _carrying(stages, call):
    stages = [s for s in stages if s is not None]
    if not stages:
        return call(None)
    job, split = _join([s[0] for s in stages])
    out, moved = call(job)
    for (_, done), part in zip(stages, split(moved)):
        done(part)
    return out


def _layer_forward(x, h, w_in_t, rest, sm, g_next, d, stages=None):
    stages = stages or {}
    proj = _carrying([stages.get("proj")], lambda job: _matmul(h, w_in_t, "nt", BF16, "proj_fwd", n=7 * d, tn_cap=1792, job=job))
    f_t = _matmul(w_in_t[7 * d:], h, "nt", F32, "forget_fwd", tn_cap=1024)
    c_t = _fox_prep(f_t, sm["b_f"], "fox_prep")
    o, lse = _carrying([stages.get("attn")], lambda job: _attn_fwd(proj, c_t, d, "attn_fwd", job=job))
    wts = rest()
    merged = _mix_fwd(proj, o, sm["wm"], sm["bs"], sm["g_v"], d, "mix_fwd")
    z, x1, h2 = _matmul(merged, wts["w_out"], "nn", F32, "out_fwd", norms=(x, sm["g_post"], sm["g_fpre"]))
    a = _carrying([stages.get("gate")], lambda job: _matmul(h2, wts["w_g_t"], "nt", BF16, "gate_fwd", tn_cap=1408, job=job))
    b = _carrying([stages.get("up")], lambda job: _matmul(h2, wts["w_u_t"], "nt", BF16, "up_fwd", tn_cap=1408, job=job))
    z2, x_out, h_out, mm = _carrying([stages.get("down")], lambda job: _matmul(
        b, wts["w_d"], "nn", F32, "down_fwd", norms=(x1, sm["g_fpost"], g_next), silu_of=a, job=job))
    return dict(x=x, h=h, proj=proj, f_t=f_t, c_t=c_t, o=o, lse=lse, merged=merged, z=z, x1=x1,
                h2=h2, a=a, b=b, mm=mm, z2=z2, x_out=x_out, h_out=h_out)


class _GradExchange:
    def __init__(self, pay, keys, c_idx, chip):
        self.keys = list(keys)
        self.p4 = [pay[k].reshape(N_CHIPS, pay[k].shape[1] // N_CHIPS, pay[k].shape[2]) for k in self.keys]
        self.c_idx = c_idx
        self.sel = jnp.stack([chip, chip, c_idx[0]]).astype(jnp.int32)
        self.done = 0

    def _after_swap(self, landed):
        self.parts = [_add_half(p, r, self.c_idx, "add_sibling") for p, r in zip(self.p4, landed)]
        self.done = 1

    def _after_scatter(self, landed):
        self.g = [_sum_slots(got, sent, self.sel, "sum_chips", out_cols=p.shape[2])
                  for got, sent, p in zip(landed, self.parts, self.p4)]
        self.done = 2

    def _after_share(self, moved):
        self.g = list(moved)
        self.done = 3

    def stage(self):
        if self.done == 0:
            return _swap_job(self.p4), self._after_swap
        if self.done == 1:
            return _scatter_job(self.parts), self._after_scatter
        if self.done == 2:
            return _share_job(self.g), self._after_share
        return None

    def run(self):
        for name in ("swap_grads", "scatter_grads", "share_grads")[self.done:]:
            job, done = self.stage()
            done(_run_job(job, name))

    def grads(self):
        return dict(zip(self.keys, self.g))


EARLY_KEYS = ("w_d", "w_g", "w_u", "w_out")


def _layer_backward(dz2, dx2, sv, wts, sm, d, c_idx, chip, carried=(), split_own=False, small_stage=None):
    t = dx2.shape[0]
    heads = d // LANE
    ff = wts["w_d"].shape[0]
    in_w = 7 * d + heads
    g, pay = {}, {}
    carried = list(carried)

    def payload(key, a, b, rows, row0, name, extra=()):
        def call(job):
            return _matmul(a, b, "tn", BF16, name, slab=((1, rows, d), 0, row0), into=pay.get(key), job=job, tm_cap=1408,
                           tn_cap=1024, tk_cap=1024)
        pay[key] = _carrying(list(extra), call)

    def nxt(*exchanges):
        return [ex.stage() for ex in exchanges]

    dm = _carrying(nxt(*carried), lambda job: _matmul(dz2, wts["w_d"], "nt", BF16, "down_bwd_x", tn_cap=1408, tk_cap=1024, job=job))
    payload("w_d", sv["mm"], dz2, ff, 0, "down_bwd_w")
    da, db = _swiglu_bwd(sv["a"], sv["b"], dm, "swiglu_bwd")
    dh2 = _matmul_pieces([(da, wts["w_g_t"], 0), (db, wts["w_u_t"], 0)], None, "gu_bwd_x", tk=_tile(ff, 1408))
    payload("w_g", da, sv["h2"], ff, 0, "gate_bwd_w")
    payload("w_u", db, sv["h2"], ff, 0, "up_bwd_w")
    dx1, dz, g["g_fpre"], g["g_post"] = _norm_bwd(dx2, (dh2, sv["x1"], sm["g_fpre"]), (sv["z"], sm["g_post"]), "norm_bwd_mid")
    dmerged = _matmul(dz, wts["w_out"], "nt", F32, "out_bwd_x", tk_cap=1024)
    payload("w_out", sv["merged"], dz, d, 0, "out_bwd_w")
    early = [_GradExchange(pay, EARLY_KEYS, c_idx, chip)] if split_own else []
    d_uv, d_g, do, g["w_s"], g["b_s"], g["g_v"] = _carrying(nxt(*early), lambda job: _mix_bwd(
        dmerged, sv["proj"], sv["o"], sm["wm"], sm["wm_t"], sm["bs"], sm["g_v"], d, "mix_bwd", job=job))
    extra = [small_stage(g)] if small_stage is not None else []
    attn_args = (sv["proj"], do, sv["o"], sv["lse"], sv["c_t"], d)
    dq, dk, dv, dc_q, dc_k = _carrying(nxt(*carried) + extra, lambda job: _attn_bwd(*attn_args, "attn_bwd", job=job))
    df_t, g["b_f"] = _fox_bwd(dc_q, dc_k, sv["f_t"], sm["b_f"], "fox_bwd")
    df_b = df_t.astype(BF16)
    pieces = [(d_uv, COL_U), (dq, COL_Q), (dk, COL_K), (dv, COL_VA), (d_g, COL_GA)]
    pay["w_in"] = _carrying(nxt(*carried, *early), lambda job: _matmul_rows([p for p, _ in pieces], sv["h"], in_w,
                                                                           "proj_bwd_w", job=job))
    w_f_rows = _carrying(nxt(*early), lambda job: _matmul(df_b, sv["h"], "nn", BF16, "forget_bwd_w", tk_cap=1024, job=job))
    pay["w_in"] = lax.dynamic_update_slice(pay["w_in"], w_f_rows[None], (0, 7 * d, 0))
    late = _GradExchange(pay, [k for k in ("w_in",) + EARLY_KEYS if not (split_own and k in EARLY_KEYS)], c_idx, chip)
    mine = [late] if split_own else []
    dh_f = _carrying(nxt(*mine), lambda job: _matmul(df_b, wts["w_in_t"][7 * d:], "tn", F32, "forget_bwd_x", job=job))
    ops = [(p, wts["w_in_t"], col * d) for p, col in pieces]
    dh = _carrying(nxt(*mine), lambda job: _matmul_pieces(ops, dh_f, "proj_bwd_x", job=job, tk=_tile(d, 1024)))
    return dh, dx1, g, early + [late]


def _small_pack(parts):
    flat = jnp.concatenate([p.reshape(-1) for p in parts])
    n = flat.shape[0]
    pad = (-n) % (LANE * LANE)
    return jnp.pad(flat, (0, pad)).reshape(-1, LANE)


def kernel(x, mix_pre_g, w_in, b_forget, sgu_norm_g, w_spatial, b_spatial, w_out, mix_post_g, ffn_pre_g, w_gate, w_up, w_down, ffn_post_g, loss_target, m_mix_pre_g, m_w_in, m_b_forget, m_sgu_norm_g, m_w_spatial, m_b_spatial, m_w_out, m_mix_post_g, m_ffn_pre_g, m_w_gate, m_w_up, m_w_down, m_ffn_post_g, v_mix_pre_g, v_w_in, v_b_forget, v_sgu_norm_g, v_w_spatial, v_b_spatial, v_w_out, v_mix_post_g, v_ffn_pre_g, v_w_gate, v_w_up, v_w_down, v_ffn_post_g):
    depth, d = mix_pre_g.shape
    assert depth == 2, "the AdamW kernels and the exchange schedule are written for two blocks"
    heads = d // LANE
    t = x.shape[1]
    ff = w_down.shape[1] * N_CHIPS
    in_w = w_in.shape[2] * N_CHIPS
    assert in_w == 7 * d + heads
    xs = x.reshape(t, d)
    target = loss_target.reshape(t, d)
    c_idx = lax.axis_index("c").astype(jnp.int32).reshape(1)
    chip = 2 * lax.axis_index("x") + lax.axis_index("y")
    dev = 2 * chip + lax.axis_index("c")

    def in_view(w):
        return jnp.transpose(w, (2, 0, 1))

    def gu_view(w):
        return jnp.transpose(w, (0, 2, 1))

    own = [jnp.transpose(in_view(w_in).astype(BF16), (1, 0, 2)), w_out.astype(BF16), gu_view(w_gate).astype(BF16),
           gu_view(w_up).astype(BF16), w_down.astype(BF16)]
    bufs = [[lax.dynamic_update_slice(lax.empty((N_CHIPS,) + o.shape[1:], BF16), o[l][None], (chip, 0, 0)) for o in own]
            for l in range(depth)]
    first_in = _run_job(_gather_job([bufs[0][0]]), "gather_first")[0]

    def weights(g_in, g_out, g_g, g_u, g_d):
        return dict(w_in_t=g_in.reshape(in_w, d), w_out=g_out.reshape(d, d), w_g_t=g_g.reshape(ff, d),
                    w_u_t=g_u.reshape(ff, d), w_d=g_d.reshape(ff, d))

    tril = jnp.tril(jnp.ones((LANE, LANE), bool))
    smalls = []
    for l in range(depth):
        wm = jnp.where(tril[None], w_spatial[l], 0.0).astype(BF16)
        smalls.append(dict(
            b_f=b_forget[l].reshape(heads, 1), wm=wm, wm_t=jnp.swapaxes(wm, 1, 2), bs=b_spatial[l].reshape(heads, LANE, 1),
            g_v=sgu_norm_g[l].reshape(1, d), g_pre=mix_pre_g[l].reshape(1, d), g_post=mix_post_g[l].reshape(1, d),
            g_fpre=ffn_pre_g[l].reshape(1, d), g_fpost=ffn_post_g[l].reshape(1, d)))

    wts, later = [], {}

    def keep(key):
        def done(moved):
            later[key] = list(moved)
        return done

    def rest_first():
        wts.append(weights(first_in, *later["rest0"]))
        return wts[0]

    stages = dict(proj=(_gather_job(bufs[0][1:], mid_at=1.0), keep("rest0")),
                  attn=(_gather_job(bufs[1][0:2], mid_at=0.7), keep("in_out1")),
                  gate=(_gather_job(bufs[1][2:3], mid_at=1.0), keep("g1")), up=(_gather_job(bufs[1][3:4], mid_at=1.0), keep("u1")),
                  down=(_gather_job(bufs[1][4:5], mid_at=1.0), keep("d1")))
    h = _norm_fwd(xs, None, None, smalls[0]["g_pre"], "norm_first")
    g_after = [smalls[min(l + 1, depth - 1)]["g_pre"] for l in range(depth)]
    saved = [_layer_forward(xs, h, first_in.reshape(in_w, d), rest_first, smalls[0], g_after[0], d, stages)]
    wts.append(weights(*later["in_out1"], later["g1"][0], later["u1"][0], later["d1"][0]))
    for l in range(1, depth):
        saved.append(_layer_forward(saved[l - 1]["x_out"], saved[l - 1]["h_out"], wts[l]["w_in_t"], lambda l=l: wts[l], smalls[l],
                                    g_after[l], d))
    dy, loss_part = _loss_grad(saved[-1]["x_out"], target, "loss")
    loss = lax.psum(jnp.sum(loss_part), ("x", "y", "c"))

    small_shapes = dict(g_pre=(d,), b_f=(heads,), g_v=(d,), w_s=w_spatial.shape[1:], b_s=b_spatial.shape[1:], g_post=(d,),
                        g_fpre=(d,), g_fpost=(d,))
    late_entries = [(0, "g_pre"), (0, "b_f")]
    early_entries = [(l, n) for l in reversed(range(depth)) for n in small_shapes if (l, n) not in late_entries]
    dev_sel = jnp.stack([dev, jnp.zeros_like(dev), jnp.zeros_like(dev)]).astype(jnp.int32)
    small_sum = {}

    def small_exchange(entries, values):
        packed = _small_pack([values[e].reshape(-1) for e in entries])

        def done(moved):
            total = _sum_slots(moved[0], packed[None], dev_sel, "sum_small").reshape(-1)
            off = 0
            for e in entries:
                n = math.prod(small_shapes[e[1]])
                small_sum[e] = total[off:off + n].reshape(small_shapes[e[1]])
                off += n

        return _gather_all_job(packed), done

    grads = [None] * depth
    exchanges = [None] * depth
    dx2 = dy
    dz2, g_fpost = _norm_bwd(dx2, None, (saved[depth - 1]["z2"], smalls[depth - 1]["g_fpost"]), "norm_bwd_top")
    for l in reversed(range(depth)):
        last = l == 0

        def small_stage(g, l=l, g_fpost=g_fpost):
            known = {(k, n): grads[k][n] for k in range(l + 1, depth) for n in small_shapes}
            known.update({(l, n): g[n] for n in g})
            known[(l, "g_fpost")] = g_fpost
            return small_exchange(early_entries, known)

        carried = [ex for k in range(l + 1, depth) for ex in exchanges[k]]
        dh, dx1, g, exchanges[l] = _layer_backward(dz2, dx2, saved[l], wts[l], smalls[l], d, c_idx, chip, carried=carried,
                                                    split_own=last, small_stage=small_stage if last else None)
        g["g_fpost"] = g_fpost
        if l > 0:
            dx2, dz2, g["g_pre"], g_fpost = _norm_bwd(dx1, (dh, saved[l]["x"], smalls[l]["g_pre"]),
                                                       (saved[l - 1]["z2"], smalls[l - 1]["g_fpost"]), "norm_bwd_between")
        else:
            grad_x, g["g_pre"] = _norm_bwd(dx1, (dh, saved[l]["x"], smalls[l]["g_pre"]), None, "norm_bwd_bottom")
        grads[l] = g
    job, done = small_exchange(late_entries, {(0, n): grads[0][n] for n in ("g_pre", "b_f")})
    done(_run_job(job, "gather_small"))
    big = [{} for _ in range(depth)]
    for l in range(depth):
        for ex in exchanges[l]:
            ex.run()
            big[l].update(ex.grads())
    small_grads = {n: jnp.stack([small_sum[(l, n)] for l in range(depth)]) for n in small_shapes}

    def adam_small(w, g, m, v):
        shp = w.shape
        if w.ndim >= 3 and shp[-1] >= LANE:
            two = (math.prod(shp[:-1]), shp[-1])
        else:
            two = (1, math.prod(shp)) if math.prod(shp) < LANE else (math.prod(shp) // LANE, LANE)
        outs = _adamw(w.reshape(two), g.reshape(two), m.reshape(two), v.reshape(two), "adamw")
        return [g] + [o.reshape(shp) for o in outs]

    def adam_in(w, m, v):
        outs = _adamw_interleaved(in_view(w), big[0]["w_in"], big[1]["w_in"], in_view(m), in_view(v), "adamw_in")
        return [jnp.transpose(o, (1, 2, 0)) for o in outs]

    def adam_gu(k, w, m, v):
        outs = _adamw_layers(gu_view(w), big[0][k], big[1][k], gu_view(m), gu_view(v), "adamw_layers")
        return [jnp.transpose(o, (0, 2, 1)) for o in outs]

    def adam_rows(k, w, m, v):
        return _adamw_layers(w, big[0][k], big[1][k], m, v, "adamw_layers")

    results = [
        adam_small(mix_pre_g, small_grads["g_pre"], m_mix_pre_g, v_mix_pre_g),
        adam_in(w_in, m_w_in, v_w_in),
        adam_small(b_forget, small_grads["b_f"], m_b_forget, v_b_forget),
        adam_small(sgu_norm_g, small_grads["g_v"], m_sgu_norm_g, v_sgu_norm_g),
        adam_small(w_spatial, small_grads["w_s"], m_w_spatial, v_w_spatial),
        adam_small(b_spatial, small_grads["b_s"], m_b_spatial, v_b_spatial),
        adam_rows("w_out", w_out, m_w_out, v_w_out),
        adam_small(mix_post_g, small_grads["g_post"], m_mix_post_g, v_mix_post_g),
        adam_small(ffn_pre_g, small_grads["g_fpre"], m_ffn_pre_g, v_ffn_pre_g),
        adam_gu("w_g", w_gate, m_w_gate, v_w_gate),
        adam_gu("w_u", w_up, m_w_up, v_w_up),
        adam_rows("w_d", w_down, m_w_down, v_w_down),
        adam_small(ffn_post_g, small_grads["g_fpost"], m_ffn_post_g, v_ffn_post_g),
    ]
    gs, deltas, new_ms, new_vs = zip(*results)
    return (loss, grad_x.reshape(x.shape), *gs, *deltas, *new_ms, *new_vs)
```

```python
import functools
import math

import jax
import jax.numpy as jnp
from jax import lax
from jax.experimental import pallas as pl
from jax.experimental.pallas import tpu as pltpu

F32 = jnp.float32
BF16 = jnp.bfloat16

EPS = 1e-6
LANE = 128
SUBLANE = 8
N_CHIPS = 4
N_DEV = 8
VMEM_LIMIT = 48 * 1024 * 1024
MESH = pl.DeviceIdType.MESH

ADAM_LR = 0.001
ADAM_B1 = 0.9
ADAM_B2 = 0.999
ADAM_EPS = 1e-08
ADAM_WD = 0.01
ADAM_STEP = 10
ADAM_C1 = 1.0 / (1.0 - ADAM_B1 ** ADAM_STEP)
ADAM_C2 = 1.0 / (1.0 - ADAM_B2 ** ADAM_STEP)

GELU_K = math.sqrt(2.0 / math.pi)
GELU_A = 0.044715
NEG = -1e30
LOG2E = 1.4426950408889634
LN2 = 0.6931471805599453

COL_U, COL_V, COL_Q, COL_K, COL_VA, COL_GA, COL_GB, COL_F = range(8)


def _cparams(sem=None):
    return pltpu.CompilerParams(dimension_semantics=sem, vmem_limit_bytes=VMEM_LIMIT)


def _tile(n, cap):
    best = None
    for t in range(LANE, min(n, cap) + 1, LANE):
        if n % t == 0:
            best = t
    return best if best is not None else n


def _rows(n, cap):
    best = None
    for t in range(SUBLANE, min(n, cap) + 1, SUBLANE):
        if n % t == 0:
            best = t
    return best if best is not None else n


def _gelu_and_grad(x):
    x2 = x * x
    t = jnp.tanh(GELU_K * (x + GELU_A * x2 * x))
    g = 0.5 * x * (1.0 + t)
    dg = 0.5 * (1.0 + t) + 0.5 * x * (1.0 - t * t) * (GELU_K * (1.0 + 3.0 * GELU_A * x2))
    return g, dg


def _sigmoid(x):
    return 1.0 / (1.0 + jnp.exp(-x))


def _sum8(v):
    n, d = v.shape
    return v.reshape(n // SUBLANE, SUBLANE, d).sum(axis=0)


def _nt_dot(a, b):
    return lax.dot_general(a, b, (((1,), (1,)), ((), ())), preferred_element_type=F32)


_HBM = pl.BlockSpec(memory_space=pl.ANY)


def _place():
    x, y, c = lax.axis_index("x"), lax.axis_index("y"), lax.axis_index("c")
    chips = [(1 - x, y), (x, 1 - y), (1 - x, 1 - y)]
    return x, y, c, chips


class _Job:
    def __init__(self, ins, inout, fresh, nsem, first, mid, last, mid_at=0.5):
        self.ins, self.inout, self.fresh, self.nsem = list(ins), list(inout), list(fresh), nsem
        self.first, self.mid, self.last, self.mid_at = first, mid, last, mid_at


def _call(body, *, grid, in_specs, out_specs, out_shape, scratch_shapes, dims, name, args, aliases=None, job=None):
    single = not isinstance(out_shape, (list, tuple))
    out_specs = [out_specs] if single else list(out_specs)
    out_shape = [out_shape] if single else list(out_shape)
    aliases = dict(aliases or {})
    if job is None:
        outs = pl.pallas_call(body, grid=grid, in_specs=in_specs, out_specs=out_specs, out_shape=out_shape,
                              scratch_shapes=scratch_shapes, input_output_aliases=aliases, compiler_params=_cparams(dims),
                              name=name)(*args)
        return (outs[0] if single else outs), []
    n_in, n_out, n_scr = len(args), len(out_shape), len(scratch_shapes)
    n_ji, n_jio, n_jf = len(job.ins), len(job.inout), len(job.fresh)
    total = math.prod(grid)

    def wrapped(*refs):
        host_in = refs[:n_in]
        pos = n_in
        j_ins = refs[pos:pos + n_ji]
        pos += n_ji + n_jio
        host_out = refs[pos:pos + n_out]
        pos += n_out
        j_inout = refs[pos:pos + n_jio]
        pos += n_jio
        j_fresh = refs[pos:pos + n_jf]
        pos += n_jf
        host_scr = refs[pos:pos + n_scr]
        ssem, rsem = refs[pos + n_scr:]
        flat = 0
        for ax, size in enumerate(grid):
            flat = flat * size + pl.program_id(ax)

        def hook(fn, at):
            if fn is not None:
                @pl.when(flat == at)
                def _():
                    fn(j_ins, j_inout, j_fresh, ssem, rsem)

        hook(job.first, 0)
        body(*host_in, *host_out, *host_scr)
        hook(job.mid, min(int(total * job.mid_at), total - 1))
        hook(job.last, total - 1)

    for k in range(n_jio):
        aliases[n_in + n_ji + k] = n_out + k
    outs = pl.pallas_call(
        wrapped, grid=grid,
        in_specs=list(in_specs) + [_HBM] * (n_ji + n_jio),
        out_specs=out_specs + [_HBM] * (n_jio + n_jf),
        out_shape=out_shape + [jax.ShapeDtypeStruct(b.shape, b.dtype) for b in job.inout] + list(job.fresh),
        scratch_shapes=list(scratch_shapes) + [pltpu.SemaphoreType.DMA((job.nsem,)), pltpu.SemaphoreType.DMA((job.nsem,))],
        input_output_aliases=aliases, compiler_params=_cparams(tuple("arbitrary" for _ in grid)), name=name,
    )(*args, *job.ins, *job.inout)
    host = outs[:n_out]
    return (host[0] if single else host), outs[n_out:]


def _run_job(job, name):
    n_ji, n_jio, n_jf = len(job.ins), len(job.inout), len(job.fresh)

    def body(*refs):
        j_ins = refs[:n_ji]
        pos = n_ji + n_jio
        j_inout = refs[pos:pos + n_jio]
        j_fresh = refs[pos + n_jio:pos + n_jio + n_jf]
        ssem, rsem = refs[pos + n_jio + n_jf:]
        for fn in (job.first, job.mid, job.last):
            if fn is not None:
                fn(j_ins, j_inout, j_fresh, ssem, rsem)

    return pl.pallas_call(
        body, in_specs=[_HBM] * (n_ji + n_jio), out_specs=[_HBM] * (n_jio + n_jf),
        out_shape=[jax.ShapeDtypeStruct(b.shape, b.dtype) for b in job.inout] + list(job.fresh),
        scratch_shapes=[pltpu.SemaphoreType.DMA((job.nsem,)), pltpu.SemaphoreType.DMA((job.nsem,))],
        input_output_aliases={n_ji + k: k for k in range(n_jio)}, name=name,
    )(*job.ins, *job.inout)


_DIMS ={"nn": ((1,), (0,)), "nt": ((1,), (1,)), "tn": ((0,), (0,))}


def _matmul(a, b, mode, out_dtype, name, n=None, slab=None, into=None, job=None, norms=None, silu_of=None, tm_cap=512,
            tn_cap=2048, tk_cap=1408):
    if mode == "nn":
        (m, k), (k2, nn_) = a.shape, b.shape
    elif mode == "nt":
        (m, k), (nn_, k2) = a.shape, b.shape
    else:
        (k, m), (k2, nn_) = a.shape, b.shape
    n = nn_ if n is None else n
    assert k == k2, (a.shape, b.shape, mode)
    tm, tn, tk = _tile(m, tm_cap), _tile(n, tn_cap), _tile(k, tk_cap)
    if slab is not None and slab[2]:
        tm = _tile(math.gcd(m, slab[2]), tm_cap)
    nk = k // tk
    if mode == "tn":
        a_spec = pl.BlockSpec((tk, tm), lambda j, i, kk, *_: (kk, i))
    else:
        a_spec = pl.BlockSpec((tm, tk), lambda j, i, kk, *_: (i, kk))
    if mode == "nt":
        b_spec = pl.BlockSpec((tn, tk), lambda j, i, kk, *_: (j, kk))
    else:
        b_spec = pl.BlockSpec((tk, tn), lambda j, i, kk, *_: (kk, j))
    dims = (_DIMS[mode], ((), ()))
    aliased = into is not None

    n_a = 1 if silu_of is None else 2
    n_in = n_a + 1 + aliased + (3 if norms is not None else 0)
    n_main = 1 + (2 if norms is not None else 0)

    def finish(refs, z):
        refs[n_in][...] = z.astype(out_dtype).reshape(refs[n_in].shape)
        if norms is not None:
            x_ref, gp_ref, gn_ref = refs[n_in - 3:n_in]
            r = lax.rsqrt(jnp.mean(z * z, axis=-1, keepdims=True) + EPS)
            xn = x_ref[...] + z * r * gp_ref[...]
            refs[n_in + 1][...] = xn
            r2 = lax.rsqrt(jnp.mean(xn * xn, axis=-1, keepdims=True) + EPS)
            refs[n_in + 2][...] = (xn * r2 * gn_ref[...]).astype(BF16)

    def body(*refs):
        lhs = refs[0][...]
        if silu_of is not None:
            gv = refs[1][...].astype(F32)
            lhs = (gv * _sigmoid(gv) * lhs.astype(F32)).astype(BF16)
            refs[n_in + n_main][...] = lhs
        p = lax.dot_general(lhs, refs[n_a][...], dims, preferred_element_type=F32)
        if nk == 1:
            finish(refs, p)
        else:
            acc = refs[-1]
            kk = pl.program_id(2)

            @pl.when(kk == 0)
            def _():
                acc[...] = p

            @pl.when(kk > 0)
            def _():
                acc[...] += p

            @pl.when(kk == nk - 1)
            def _():
                finish(refs, acc[...])

    if slab is None:
        out_spec = pl.BlockSpec((tm, tn), lambda j, i, kk: (i, j))
        out_shape = jax.ShapeDtypeStruct((m, n), out_dtype)
    else:
        shape3, lead, row0 = slab
        assert row0 % tm == 0 and shape3[2] == n
        out_spec = pl.BlockSpec((1, tm, tn), lambda j, i, kk: (lead, row0 // tm + i, j))
        out_shape = jax.ShapeDtypeStruct(shape3, out_dtype)
    in_specs, args = [a_spec, b_spec], [a, b]
    if silu_of is not None:
        assert mode == "nn" and tn == n and slab is None, "the left operand's blocks are written once each"
        in_specs, args = [a_spec, a_spec, b_spec], [a, silu_of, b]
    if aliased:
        in_specs.append(pl.BlockSpec(memory_space=pl.ANY))
        args.append(into)
    if norms is not None:
        assert tn == n and slab is None, "the fused norms need whole rows"
        row = pl.BlockSpec((tm, n), lambda j, i, kk: (i, 0))
        vec = pl.BlockSpec((1, n), lambda j, i, kk: (0, 0))
        in_specs += [row, vec, vec]
        args += list(norms)
        out_spec = [out_spec, row, row]
        out_shape = [out_shape, jax.ShapeDtypeStruct((m, n), F32), jax.ShapeDtypeStruct((m, n), BF16)]
    if silu_of is not None:
        out_spec = (out_spec if isinstance(out_spec, list) else [out_spec]) + [pl.BlockSpec((tm, tk), lambda j, i, kk: (i, kk))]
        out_shape = (out_shape if isinstance(out_shape, list) else [out_shape]) + [jax.ShapeDtypeStruct((m, k), BF16)]
    out, moved = _call(
        body, grid=(n // tn, m // tm, nk), in_specs=in_specs, out_specs=out_spec, out_shape=out_shape,
        scratch_shapes=[pltpu.VMEM((tm, tn), F32)] if nk > 1 else [], dims=("parallel", "parallel", "arbitrary"), name=name,
        args=args, aliases={n_a + 1: 0} if aliased else None, job=job)
    return out if job is None else (out, moved)


def _matmul_rows(pieces, b, rows, name, job=None, tm=1024, tk=1024):
    k, n = b.shape
    tm = math.gcd(tm, *[a.shape[1] for a in pieces])
    tk = _tile(k, tk)
    nk = k // tk
    spans, r0 = [], 0
    for a in pieces:
        assert a.shape[0] == k and a.shape[1] % tm == 0
        spans.append((r0, a.shape[1] // tm))
        r0 += a.shape[1] // tm
    nr = r0
    np_ = len(pieces)

    def body(*refs):
        b_ref, o_ref, acc = refs[np_], refs[np_ + 1], refs[-1]
        r, kk = pl.program_id(0), pl.program_id(1)
        for p, (first, count) in enumerate(spans):
            @pl.when((r >= first) & (r < first + count))
            def _(p=p):
                part = lax.dot_general(refs[p][...], b_ref[...], (_DIMS["tn"], ((), ())), preferred_element_type=F32)

                @pl.when(kk == 0)
                def _():
                    acc[...] = part

                @pl.when(kk > 0)
                def _():
                    acc[...] += part

        @pl.when(kk == nk - 1)
        def _():
            o_ref[0] = acc[...].astype(BF16)

    in_specs = []
    for first, count in spans:
        in_specs.append(pl.BlockSpec((tk, tm), lambda r, kk, f=first, c=count: (
            jnp.where(r < f, 0, jnp.where(r >= f + c, nk - 1, kk)), jnp.clip(r - f, 0, c - 1))))
    in_specs.append(pl.BlockSpec((tk, n), lambda r, kk: (kk, 0)))
    out, moved = _call(
        body, grid=(nr, nk), in_specs=in_specs, out_specs=pl.BlockSpec((1, tm, n), lambda r, kk: (0, r, 0)),
        out_shape=jax.ShapeDtypeStruct((1, rows, n), BF16), scratch_shapes=[pltpu.VMEM((tm, n), F32)],
        dims=("arbitrary", "arbitrary"), name=name, args=list(pieces) + [b], job=job)
    return out if job is None else (out, moved)


def _matmul_pieces(pieces, addend, name, tk, job=None, tm_cap=512):
    m = pieces[0][0].shape[0]
    n = pieces[0][1].shape[1]
    tm = _tile(m, tm_cap)
    spans, s0 = [], 0
    for a, b, row0 in pieces:
        assert a.shape[1] % tk == 0 and row0 % tk == 0 and b.shape[1] == n and a.shape[0] == m
        spans.append((s0, a.shape[1] // tk, row0 // tk))
        s0 += a.shape[1] // tk
    steps = s0
    np_ = len(pieces)
    groups = []
    for (a, b, _), (first, count, brow) in zip(pieces, spans):
        if groups and groups[-1][0] is b and groups[-1][3] + groups[-1][2] == brow:
            groups[-1][2] += count
        else:
            groups.append([b, first, count, brow])
    b_of = []
    for first, count, _ in spans:
        b_of.append(next(k for k, g in enumerate(groups) if g[1] <= first < g[1] + g[2]))
    ng = len(groups)

    nm = m // tm

    def body(*refs):
        o_ref, acc = refs[-2], refs[-1]
        s, i = pl.program_id(0), pl.program_id(1)
        rows = pl.ds(pl.multiple_of(i * tm, tm), tm)

        @pl.when(s == 0)
        def _():
            acc[rows, :] = refs[np_ + ng][...] if addend is not None else jnp.zeros((tm, n), F32)

        for p, (first, count, _) in enumerate(spans):
            @pl.when((s >= first) & (s < first + count))
            def _(p=p):
                acc[rows, :] += jnp.dot(refs[p][...], refs[np_ + b_of[p]][...], preferred_element_type=F32)

        @pl.when(s == steps - 1)
        def _():
            o_ref[...] = acc[rows, :]

    in_specs, args = [], []
    for (a, _, _), (first, count, _) in zip(pieces, spans):
        in_specs.append(pl.BlockSpec((tm, tk), lambda s, i, f=first, c=count: (
            jnp.where(s < f, 0, jnp.where(s >= f + c, nm - 1, i)), jnp.clip(s - f, 0, c - 1))))
        args.append(a)
    for b, first, count, brow in groups:
        in_specs.append(pl.BlockSpec((tk, n), lambda s, i, f=first, c=count, r=brow: (r + jnp.clip(s - f, 0, c - 1), 0)))
        args.append(b)
    if addend is not None:
        in_specs.append(pl.BlockSpec((tm, n), lambda s, i: (jnp.where(s == 0, i, nm - 1), 0)))
        args.append(addend)
    out, moved = _call(
        body, grid=(steps, nm), in_specs=in_specs,
        out_specs=pl.BlockSpec((tm, n), lambda s, i: (jnp.where(s == steps - 1, i, 0), 0)),
        out_shape=jax.ShapeDtypeStruct((m, n), F32), scratch_shapes=[pltpu.VMEM((m, n), F32)],
        dims=("arbitrary", "arbitrary"), name=name, args=args, job=job)
    return out if job is None else (out, moved)


def _norm_fwd(x, z, g_post, g_next, name):
    t, d = x.shape
    tt = _rows(t, 512)
    row = pl.BlockSpec((tt, d), lambda i: (i, 0))
    vec = pl.BlockSpec((1, d), lambda i: (0, 0))

    def body(*refs):
        if z is None:
            x_ref, gn_ref, h_ref = refs
            xn = x_ref[...]
        else:
            x_ref, z_ref, gp_ref, gn_ref, xo_ref, h_ref = refs
            zz = z_ref[...]
            r = lax.rsqrt(jnp.mean(zz * zz, axis=-1, keepdims=True) + EPS)
            xn = x_ref[...] + zz * r * gp_ref[...]
            xo_ref[...] = xn
        r2 = lax.rsqrt(jnp.mean(xn * xn, axis=-1, keepdims=True) + EPS)
        h_ref[...] = (xn * r2 * gn_ref[...]).astype(BF16)

    if z is None:
        return pl.pallas_call(
            body, grid=(t // tt,), in_specs=[row, vec], out_specs=row,
            out_shape=jax.ShapeDtypeStruct((t, d), BF16), compiler_params=_cparams(("parallel",)), name=name,
        )(x, g_next)
    return pl.pallas_call(
        body, grid=(t // tt,), in_specs=[row, row, vec, vec], out_specs=[row, row],
        out_shape=[jax.ShapeDtypeStruct((t, d), F32), jax.ShapeDtypeStruct((t, d), BF16)],
        compiler_params=_cparams(("parallel",)), name=name,
    )(x, z, g_post, g_next)


def _rms_bwd(dy, x, g):
    r = lax.rsqrt(jnp.mean(x * x, axis=-1, keepdims=True) + EPS)
    n = x * r
    dn = dy * g
    dx = r * (dn - n * jnp.mean(dn * n, axis=-1, keepdims=True))
    return dx, dy * n


def _norm_bwd(dres, pre, post, name):
    t, d = dres.shape
    tt = _rows(t, 512)
    nt = t // tt
    row = pl.BlockSpec((tt, d), lambda i: (i, 0))
    vec = pl.BlockSpec((1, d), lambda i: (0, 0))
    has_pre, has_post = pre is not None, post is not None
    n_in = 1 + (3 if has_pre else 0) + (2 if has_post else 0)
    n_out = has_pre + has_post + has_pre + has_post

    def body(*refs):
        ins, outs, scr = refs[:n_in], refs[n_in:n_in + n_out], refs[n_in + n_out:]
        i = pl.program_id(0)
        dx = ins[0][...]
        pos, opos, spos = 1, 0, 0
        accs = []
        if has_pre:
            dh_ref, xa_ref, ga_ref = ins[pos:pos + 3]
            pos += 3
            dxa, dga_t = _rms_bwd(dh_ref[...], xa_ref[...], ga_ref[...])
            dx = dx + dxa
            outs[opos][...] = dx
            opos += 1
            accs.append((scr[spos], dga_t))
            spos += 1
        if has_post:
            zb_ref, gb_ref = ins[pos:pos + 2]
            dz, dgb_t = _rms_bwd(dx, zb_ref[...], gb_ref[...])
            outs[opos][...] = dz.astype(BF16)
            opos += 1
            accs.append((scr[spos], dgb_t))
            spos += 1
        for (acc, val), out in zip(accs, outs[opos:]):
            part = _sum8(val)

            @pl.when(i == 0)
            def _(acc=acc, part=part):
                acc[...] = part

            @pl.when(i > 0)
            def _(acc=acc, part=part):
                acc[...] += part

            @pl.when(i == nt - 1)
            def _(acc=acc, out=out):
                out[...] = jnp.sum(acc[...], axis=0, keepdims=True)

    in_specs, args = [row], [dres]
    out_specs, out_shape = [], []
    if has_pre:
        in_specs += [row, row, vec]
        args += list(pre)
        out_specs.append(row)
        out_shape.append(jax.ShapeDtypeStruct((t, d), F32))
    if has_post:
        in_specs += [row, vec]
        args += list(post)
        out_specs.append(row)
        out_shape.append(jax.ShapeDtypeStruct((t, d), BF16))
    for _ in range(has_pre + has_post):
        out_specs.append(vec)
        out_shape.append(jax.ShapeDtypeStruct((1, d), F32))
    return pl.pallas_call(
        body, grid=(nt,), in_specs=in_specs, out_specs=out_specs, out_shape=out_shape,
        scratch_shapes=[pltpu.VMEM((SUBLANE, d), F32)] * (has_pre + has_post),
        compiler_params=_cparams(("arbitrary",)), name=name,
    )(*args)


def _loss_grad(y, target, name):
    t, d = y.shape
    tt = _rows(t, 512)
    nt = t // tt
    row = pl.BlockSpec((tt, d), lambda i: (i, 0))
    inv_d = 1.0 / d

    def body(y_ref, t_ref, dy_ref, l_ref):
        i = pl.program_id(0)
        diff = y_ref[...] - t_ref[...]
        dy_ref[...] = diff * inv_d
        s8 = _sum8(diff * diff)
        part = s8[:, 0:LANE]
        for k in range(1, d // LANE):
            part = part + s8[:, k * LANE:(k + 1) * LANE]
        part = part * (0.5 * inv_d)

        @pl.when(i == 0)
        def _():
            l_ref[...] = part

        @pl.when(i > 0)
        def _():
            l_ref[...] += part

    return pl.pallas_call(
        body, grid=(nt,), in_specs=[row, row],
        out_specs=[row, pl.BlockSpec((SUBLANE, LANE), lambda i: (0, 0))],
        out_shape=[jax.ShapeDtypeStruct((t, d), F32), jax.ShapeDtypeStruct((SUBLANE, LANE), F32)],
        compiler_params=_cparams(("arbitrary",)), name=name,
    )(y, target)


def _swiglu_bwd(a, b, dm, name):
    t, f = a.shape
    tt = _rows(t, 256)
    blk = pl.BlockSpec((tt, f), lambda i: (i, 0))

    def body(a_ref, b_ref, dm_ref, da_ref, db_ref):
        av = a_ref[...].astype(F32)
        s = _sigmoid(av)
        dv = dm_ref[...].astype(F32)
        da_ref[...] = (dv * b_ref[...].astype(F32) * s * (1.0 + av * (1.0 - s))).astype(BF16)
        db_ref[...] = (dv * av * s).astype(BF16)

    return pl.pallas_call(
        body, grid=(t // tt,), in_specs=[blk, blk, blk], out_specs=[blk, blk],
        out_shape=[jax.ShapeDtypeStruct((t, f), BF16)] * 2, compiler_params=_cparams(("parallel",)), name=name,
    )(a, b, dm)


def _log_sigmoid(x):
    return jnp.minimum(x, 0.0) - jnp.log1p(jnp.exp(-jnp.abs(x)))


def _fox_prep(f_t, b_f, name):
    h, t = f_t.shape

    def body(f_ref, b_ref, c_ref):
        r = lax.broadcasted_iota(jnp.int32, (LANE, LANE), 0)
        c = lax.broadcasted_iota(jnp.int32, (LANE, LANE), 1)
        upper = (r <= c).astype(F32)
        carry = jnp.zeros((h, 1), F32)
        for j in range(t // LANE):
            sl = slice(j * LANE, (j + 1) * LANE)
            lf = _log_sigmoid(f_ref[:, sl] + b_ref[...])
            cs = jnp.dot(lf, upper, precision=lax.Precision.HIGHEST, preferred_element_type=F32) + carry
            c_ref[:, sl] = cs
            carry = cs[:, LANE - 1:LANE]

    return pl.pallas_call(body, out_shape=jax.ShapeDtypeStruct((h, t), F32), compiler_params=_cparams(), name=name)(f_t, b_f)


def _fox_bwd(dc_q, dc_k, f_t, b_f, name):
    h, t = f_t.shape

    def body(dq_ref, dk_ref, f_ref, b_ref, df_ref, db_ref):
        r = lax.broadcasted_iota(jnp.int32, (LANE, LANE), 0)
        c = lax.broadcasted_iota(jnp.int32, (LANE, LANE), 1)
        lower = (r >= c).astype(F32)
        carry = jnp.zeros((h, 1), F32)
        dbsum = jnp.zeros((h, 1), F32)
        for j in reversed(range(t // LANE)):
            sl = slice(j * LANE, (j + 1) * LANE)
            dc = dq_ref[:, sl] - dk_ref[:, sl]
            dl = jnp.dot(dc, lower, precision=lax.Precision.HIGHEST, preferred_element_type=F32) + carry
            carry = dl[:, 0:1]
            df = dl * _sigmoid(-(f_ref[:, sl] + b_ref[...]))
            df_ref[:, sl] = df
            dbsum = dbsum + jnp.sum(df, axis=-1, keepdims=True)
        db_ref[...] = dbsum

    return pl.pallas_call(
        body, out_shape=[jax.ShapeDtypeStruct((h, t), F32), jax.ShapeDtypeStruct((h, 1), F32)],
        compiler_params=_cparams(), name=name,
    )(dc_q, dc_k, f_t, b_f)


ATTN_FWD = (1024, 512)
ATTN_BWD = (512, 512)


def _attn_tiles(t, tiles):
    return _tile(t, tiles[0]), _tile(t, tiles[1])


def _attn_fwd(proj, c_t, d, name, job=None):
    t = proj.shape[0]
    h = d // LANE
    bq, bk = _attn_tiles(t, ATTN_FWD)
    nq, nk, rr = t // bq, t // bk, bq // bk
    qc, kc, vc = COL_Q * h, COL_K * h, COL_VA * h
    qscale = LANE ** -0.5 * LOG2E

    def body(q_ref, k_ref, v_ref, cc_ref, cr_ref, o_ref, lse_ref, kb, vt, ckb, acc):
        i = pl.program_id(1)

        @pl.when(i == 0)
        def _():
            kb[...] = k_ref[...].astype(BF16)
            ckb[...] = jnp.broadcast_to(cc_ref[0] * LOG2E, (t, bq))
            for jn in range(nk):
                vt[jn] = v_ref[jn * bk:(jn + 1) * bk, :].astype(F32).T.astype(BF16)

        q = (q_ref[...].astype(F32) * qscale).astype(BF16)
        cq = cr_ref[0, 0] * LOG2E
        acc[...] = jnp.zeros((LANE, bq), F32)

        def block(j, diag, m_old, l_old):
            off = 0 if diag is None else diag * bk
            w = bq - off
            rows = pl.ds(pl.multiple_of(j * bk, bk), bk)
            s = _nt_dot(kb[rows, :], q[off:, :]) - ckb[rows, off:]
            if diag is not None:
                kk = lax.broadcasted_iota(jnp.int32, (bk, w), 0)
                qq = lax.broadcasted_iota(jnp.int32, (bk, w), 1)
                s = jnp.where(qq >= kk, s, NEG)
            cqs, m_part, l_part = cq[:, off:], m_old[:, off:], l_old[:, off:]
            m_new = jnp.maximum(m_part, jnp.max(s, axis=0, keepdims=True) + cqs)
            p = jnp.exp2(s + (cqs - m_new))
            alpha = jnp.exp2(m_part - m_new)
            l_new = alpha * l_part + jnp.sum(p, axis=0, keepdims=True)
            acc[:, off:] = alpha * acc[:, off:] + jnp.dot(vt[j], p.astype(BF16), preferred_element_type=F32)
            if off:
                m_new = jnp.concatenate([m_old[:, :off], m_new], axis=1)
                l_new = jnp.concatenate([l_old[:, :off], l_new], axis=1)
            return m_new, l_new

        m, l = lax.fori_loop(0, i * rr, lambda j, c: block(j, None, *c),
                             (jnp.full((1, bq), NEG, F32), jnp.zeros((1, bq), F32)))
        for jj in range(rr):
            m, l = block(i * rr + jj, jj, m, l)
        o_ref[...] = (acc[...] / l).T
        lse_ref[0, 0] = m + jnp.log2(l)

    rowq = pl.BlockSpec((1, 1, 1, bq), lambda hh, i: (hh, i, 0, 0))
    outs, moved = _call(
        body, grid=(h, nq),
        in_specs=[
            pl.BlockSpec((bq, LANE), lambda hh, i: (i, qc + hh)),
            pl.BlockSpec((t, LANE), lambda hh, i: (0, kc + hh)),
            pl.BlockSpec((t, LANE), lambda hh, i: (0, vc + hh)),
            pl.BlockSpec((1, t, 1), lambda hh, i: (hh, 0, 0)),
            rowq,
        ],
        out_specs=[pl.BlockSpec((bq, LANE), lambda hh, i: (i, hh)), rowq],
        out_shape=[jax.ShapeDtypeStruct((t, d), F32), jax.ShapeDtypeStruct((h, nq, 1, bq), F32)],
        scratch_shapes=[pltpu.VMEM((t, LANE), BF16), pltpu.VMEM((nk, LANE, bk), BF16), pltpu.VMEM((t, bq), F32),
                        pltpu.VMEM((LANE, bq), F32)],
        dims=("arbitrary", "arbitrary"), name=name,
        args=[proj, proj, proj, c_t.reshape(h, t, 1), c_t.reshape(h, nq, 1, bq)], job=job)
    outs = [outs[0], outs[1].reshape(h, t)]
    return outs if job is None else (outs, moved)


def _attn_bwd(proj, do, o, lse, c_t, d, name, job=None):
    t = proj.shape[0]
    h = d // LANE
    bq, bk = _attn_tiles(t, ATTN_BWD)
    nq, nk, rr = t // bq, t // bk, bq // bk
    qc, kc, vc = COL_Q * h, COL_K * h, COL_VA * h
    scale = LANE ** -0.5

    def body(q_ref, k_ref, v_ref, do_ref, o_ref, lse_ref, cc_ref, cr_ref, dq_ref, dk_ref, dv_ref, dcq_ref, dck_ref,
             kb, kt, vb, ckb, dk_acc, dv_acc, dck_acc, dqt_acc):
        i = pl.program_id(1)

        @pl.when(i == 0)
        def _():
            kb[...] = k_ref[...].astype(BF16)
            vb[...] = v_ref[...].astype(BF16)
            ckb[...] = jnp.broadcast_to(cc_ref[0] * LOG2E, (t, bq))
            for jn in range(nk):
                kt[jn] = k_ref[jn * bk:(jn + 1) * bk, :].astype(F32).T.astype(BF16)
            dk_acc[...] = jnp.zeros((t, LANE), F32)
            dv_acc[...] = jnp.zeros((t, LANE), F32)
            dck_acc[...] = jnp.zeros((t, LANE), F32)

        q = (q_ref[...].astype(F32) * (scale * LOG2E)).astype(BF16)
        dof = do_ref[...]
        dob = dof.astype(BF16)
        delta = jnp.sum((dof * o_ref[...]).T, axis=0, keepdims=True)
        rowb = cr_ref[0, 0] * LOG2E - lse_ref[0, 0]
        dqt_acc[...] = jnp.zeros((LANE, bq), F32)

        def block(j, diag, dcq):
            rows = pl.ds(pl.multiple_of(j * bk, bk), bk)
            p = jnp.exp2(_nt_dot(kb[rows, :], q) - ckb[rows, :] + rowb)
            if diag is not None:
                kk = lax.broadcasted_iota(jnp.int32, (bk, bq), 0)
                qq = lax.broadcasted_iota(jnp.int32, (bk, bq), 1)
                p = jnp.where(qq >= kk + diag * bk, p, 0.0)
            dv_acc[rows, :] += jnp.dot(p.astype(BF16), dob, preferred_element_type=F32)
            ds = p * (_nt_dot(vb[rows, :], dob) - delta)
            dsb = ds.astype(BF16)
            dk_acc[rows, :] += jnp.dot(dsb, q, preferred_element_type=F32)
            dqt_acc[...] += jnp.dot(kt[j], dsb, preferred_element_type=F32)
            part = ds[:, 0:LANE]
            for k in range(1, bq // LANE):
                part = part + ds[:, k * LANE:(k + 1) * LANE]
            dck_acc[rows, :] += part
            return dcq + jnp.sum(ds, axis=0, keepdims=True)

        dcq = lax.fori_loop(0, i * rr, lambda j, c: block(j, None, c), jnp.zeros((1, bq), F32))
        for jj in range(rr):
            dcq = block(i * rr + jj, jj, dcq)
        dq_ref[...] = (dqt_acc[...] * scale).T.astype(BF16)
        dcq_ref[0, 0] = dcq

        @pl.when(i == nq - 1)
        def _():
            dk_ref[...] = (dk_acc[...] * LN2).astype(BF16)
            dv_ref[...] = dv_acc[...].astype(BF16)
            dck_ref[0] = jnp.sum(dck_acc[...], axis=-1, keepdims=True)

    rowq = pl.BlockSpec((1, 1, 1, bq), lambda hh, i: (hh, i, 0, 0))
    blk = pl.BlockSpec((bq, LANE), lambda hh, i: (i, hh))
    whole = pl.BlockSpec((t, LANE), lambda hh, i: (0, hh))
    colk = pl.BlockSpec((1, t, 1), lambda hh, i: (hh, 0, 0))
    outs, moved = _call(
        body, grid=(h, nq),
        in_specs=[
            pl.BlockSpec((bq, LANE), lambda hh, i: (i, qc + hh)),
            pl.BlockSpec((t, LANE), lambda hh, i: (0, kc + hh)),
            pl.BlockSpec((t, LANE), lambda hh, i: (0, vc + hh)),
            blk, blk, rowq, colk, rowq,
        ],
        out_specs=[blk, whole, whole, rowq, colk],
        out_shape=[jax.ShapeDtypeStruct((t, d), BF16), jax.ShapeDtypeStruct((t, d), BF16), jax.ShapeDtypeStruct((t, d), BF16),
                   jax.ShapeDtypeStruct((h, nq, 1, bq), F32), jax.ShapeDtypeStruct((h, t, 1), F32)],
        scratch_shapes=[pltpu.VMEM((t, LANE), BF16), pltpu.VMEM((nk, LANE, bk), BF16), pltpu.VMEM((t, LANE), BF16),
                        pltpu.VMEM((t, bq), F32), pltpu.VMEM((t, LANE), F32), pltpu.VMEM((t, LANE), F32),
                        pltpu.VMEM((t, LANE), F32), pltpu.VMEM((LANE, bq), F32)],
        dims=("arbitrary", "arbitrary"), name=name,
        args=[proj, proj, proj, do, o, lse.reshape(h, nq, 1, bq), c_t.reshape(h, t, 1), c_t.reshape(h, nq, 1, bq)], job=job)
    outs = list(outs[:3]) + [outs[3].reshape(h, t), outs[4].reshape(h, t)]
    return outs if job is None else (outs, moved)


def _sgu_forward(u_ref, v_ref, gv_ref, wm_ref, bs_ref, mix_sc, groups):
    gu, dgu = _gelu_and_grad(u_ref[...].astype(F32))
    gvv, dgv = _gelu_and_grad(v_ref[...].astype(F32))
    mu = jnp.mean(gvv, axis=-1, keepdims=True)
    xc = gvv - mu
    r = lax.rsqrt(jnp.mean(xc * xc, axis=-1, keepdims=True) + EPS)
    nhat = xc * r
    vn = (nhat * gv_ref[...]).astype(BF16)
    for g in range(groups):
        sl = slice(g * LANE, (g + 1) * LANE)
        mix_sc[:, sl] = jnp.dot(wm_ref[g], vn[:, sl], preferred_element_type=F32) + bs_ref[g]
    return gu, dgu, dgv, nhat, r, vn, mix_sc[...]


def _mix_fwd(proj, o, wm, bs, g_v, d, name):
    t = proj.shape[0]
    groups = d // LANE

    def body(u_ref, v_ref, ga_ref, gb_ref, o_ref, wm_ref, bs_ref, gv_ref, out_ref, mix_sc):
        gu, _, _, _, _, _, mixed = _sgu_forward(u_ref, v_ref, gv_ref, wm_ref, bs_ref, mix_sc, groups)
        out_ref[...] = (_sigmoid(ga_ref[...].astype(F32)) * (gu * mixed) + _sigmoid(gb_ref[...].astype(F32)) * o_ref[...]).astype(BF16)

    def colblk(k):
        return pl.BlockSpec((LANE, d), lambda i, k=k: (i, k))

    full3 = pl.BlockSpec((groups, LANE, LANE), lambda i: (0, 0, 0))
    return pl.pallas_call(
        body, grid=(t // LANE,),
        in_specs=[colblk(COL_U), colblk(COL_V), colblk(COL_GA), colblk(COL_GB), colblk(0), full3,
                  pl.BlockSpec((groups, LANE, 1), lambda i: (0, 0, 0)), pl.BlockSpec((1, d), lambda i: (0, 0))],
        out_specs=colblk(0),
        out_shape=jax.ShapeDtypeStruct((t, d), BF16),
        scratch_shapes=[pltpu.VMEM((LANE, d), F32)],
        compiler_params=_cparams(("parallel",)), name=name,
    )(proj, proj, proj, proj, o, wm, bs, g_v)


def _mix_bwd(dmerged, proj, o, wm, wm_t, bs, g_v, d, name, job=None):
    t = proj.shape[0]
    groups = d // LANE
    nt = t // LANE

    def body(dm_ref, u_ref, v_ref, ga_ref, gb_ref, o_ref, wm_ref, wmt_ref, bs_ref, gv_ref,
             duv_ref, dg_ref, do_ref, dws_ref, dbs_ref, dgv_ref, mix_sc, dvn_sc, gv_acc):
        i = pl.program_id(0)

        @pl.when(i == 0)
        def _():
            dws_ref[...] = jnp.zeros_like(dws_ref)
            dbs_ref[...] = jnp.zeros_like(dbs_ref)
            gv_acc[...] = jnp.zeros_like(gv_acc)

        gu, dgu, dgv, nhat, r, vn, mixed = _sgu_forward(u_ref, v_ref, gv_ref, wm_ref, bs_ref, mix_sc, groups)
        dm = dm_ref[...]
        sa = _sigmoid(ga_ref[...].astype(F32))
        sb = _sigmoid(gb_ref[...].astype(F32))
        ov = o_ref[...]
        y_a = gu * mixed
        dg_ref[:, 0:d] = (dm * y_a * sa * (1.0 - sa)).astype(BF16)
        dg_ref[:, d:2 * d] = (dm * ov * sb * (1.0 - sb)).astype(BF16)
        do_ref[...] = dm * sb
        dy_a = dm * sa
        duv_ref[:, 0:d] = (dy_a * mixed * dgu).astype(BF16)
        dmixed = dy_a * gu
        dmixed_b = dmixed.astype(BF16)
        for g in range(groups):
            sl = slice(g * LANE, (g + 1) * LANE)
            dvn_sc[:, sl] = jnp.dot(wmt_ref[g], dmixed_b[:, sl], preferred_element_type=F32)
            dws_ref[g] += _nt_dot(dmixed_b[:, sl], vn[:, sl])
            dbs_ref[g] += jnp.sum(dmixed[:, sl], axis=-1, keepdims=True)
        dvn = dvn_sc[...]
        gv_acc[...] += _sum8(dvn * nhat)
        dn = dvn * gv_ref[...]
        dgelu = r * (dn - jnp.mean(dn, axis=-1, keepdims=True) - nhat * jnp.mean(dn * nhat, axis=-1, keepdims=True))
        duv_ref[:, d:2 * d] = (dgelu * dgv).astype(BF16)

        @pl.when(i == nt - 1)
        def _():
            dgv_ref[...] = jnp.sum(gv_acc[...], axis=0, keepdims=True)
            rr = lax.broadcasted_iota(jnp.int32, (LANE, LANE), 0)
            cl = lax.broadcasted_iota(jnp.int32, (LANE, LANE), 1)
            for g in range(groups):
                dws_ref[g] = jnp.where(rr >= cl, dws_ref[g], 0.0)

    def colblk(k):
        return pl.BlockSpec((LANE, d), lambda i, k=k: (i, k))

    full3 = pl.BlockSpec((groups, LANE, LANE), lambda i: (0, 0, 0))
    col3 = pl.BlockSpec((groups, LANE, 1), lambda i: (0, 0, 0))
    vec = pl.BlockSpec((1, d), lambda i: (0, 0))
    two = pl.BlockSpec((LANE, 2 * d), lambda i: (i, 0))
    outs, moved = _call(
        body, grid=(nt,),
        in_specs=[colblk(0), colblk(COL_U), colblk(COL_V), colblk(COL_GA), colblk(COL_GB), colblk(0), full3, full3, col3, vec],
        out_specs=[two, two, colblk(0), full3, col3, vec],
        out_shape=[jax.ShapeDtypeStruct((t, 2 * d), BF16), jax.ShapeDtypeStruct((t, 2 * d), BF16), jax.ShapeDtypeStruct((t, d), F32),
                   jax.ShapeDtypeStruct((groups, LANE, LANE), F32), jax.ShapeDtypeStruct((groups, LANE, 1), F32),
                   jax.ShapeDtypeStruct((1, d), F32)],
        scratch_shapes=[pltpu.VMEM((LANE, d), F32), pltpu.VMEM((LANE, d), F32), pltpu.VMEM((SUBLANE, d), F32)],
        dims=("arbitrary",), name=name, args=[dmerged, proj, proj, proj, proj, o, wm, wm_t, bs, g_v], job=job)
    return outs if job is None else (outs, moved)


def _adam_math(w, g, m, v):
    nm = ADAM_B1 * m + (1.0 - ADAM_B1) * g
    nv = ADAM_B2 * v + (1.0 - ADAM_B2) * (g * g)
    delta = -ADAM_LR * ((nm * ADAM_C1) / (jnp.sqrt(nv * ADAM_C2) + ADAM_EPS) + ADAM_WD * w)
    return delta, nm, nv


def _adamw(w, g, m, v, name):
    r, c = w.shape
    cap = max(SUBLANE, (2 * 1024 * 1024) // (4 * c) // SUBLANE * SUBLANE)
    tr = _rows(r, cap)

    def body(w_ref, g_ref, m_ref, v_ref, d_ref, nm_ref, nv_ref):
        d_ref[...], nm_ref[...], nv_ref[...] = _adam_math(w_ref[...], g_ref[...], m_ref[...], v_ref[...])

    blk = pl.BlockSpec((tr, c), lambda i: (i, 0))
    return pl.pallas_call(
        body, grid=(r // tr,), in_specs=[blk] * 4, out_specs=[blk] * 3,
        out_shape=[jax.ShapeDtypeStruct((r, c), F32)] * 3, compiler_params=_cparams(("parallel",)), name=name,
    )(w, g, m, v)


def _adamw_layers(w, g0, g1, m, v, name):
    _, r, c = w.shape
    cap = max(SUBLANE, (1024 * 1024) // (4 * c) // SUBLANE * SUBLANE)
    tr = _rows(r, cap)

    def body(w_ref, g0_ref, g1_ref, m_ref, v_ref, g_ref, d_ref, nm_ref, nv_ref):
        gg = jnp.where(pl.program_id(0) == 0, g0_ref[...], g1_ref[...])
        g_ref[0] = gg
        d_ref[0], nm_ref[0], nv_ref[0] = _adam_math(w_ref[0], gg, m_ref[0], v_ref[0])

    lay = pl.BlockSpec((1, tr, c), lambda l, i: (l, i, 0))

    def gspec(l0):
        return pl.BlockSpec((tr, c), lambda l, i: (jnp.where(l == l0, i, 0), 0))

    return pl.pallas_call(
        body, grid=(2, r // tr), in_specs=[lay, gspec(0), gspec(1), lay, lay], out_specs=[lay] * 4,
        out_shape=[jax.ShapeDtypeStruct((2, r, c), F32)] * 4, compiler_params=_cparams(("arbitrary", "arbitrary")), name=name,
    )(w, g0, g1, m, v)


def _adamw_interleaved(w, g0, g1, m, v, name):
    r, _, c = w.shape
    tr = 128

    def body(w_ref, g0_ref, g1_ref, m_ref, v_ref, g_ref, d_ref, nm_ref, nv_ref):
        for l, gl in enumerate((g0_ref, g1_ref)):
            gg = gl[...]
            g_ref[:, l, :] = gg
            d_ref[:, l, :], nm_ref[:, l, :], nv_ref[:, l, :] = _adam_math(w_ref[:, l, :], gg, m_ref[:, l, :], v_ref[:, l, :])

    lay = pl.BlockSpec((tr, 2, c), lambda i: (i, 0, 0))
    flat = pl.BlockSpec((tr, c), lambda i: (i, 0))
    return pl.pallas_call(
        body, grid=(pl.cdiv(r, tr),), in_specs=[lay, flat, flat, lay, lay], out_specs=[lay] * 4,
        out_shape=[jax.ShapeDtypeStruct((r, 2, c), F32)] * 4, compiler_params=_cparams(("parallel",)), name=name,
    )(w, g0, g1, m, v)


def _add_half(p4, recv, c_idx, name):
    _, r, c = p4.shape
    hw = c // 2
    tr = 256 if r % 256 == 0 else r

    def body(c_ref, a_ref, b_ref, o_ref):
        o_ref[...] = (a_ref[...].astype(F32) + b_ref[...].astype(F32)).astype(BF16)

    return pl.pallas_call(
        body,
        grid_spec=pltpu.PrefetchScalarGridSpec(
            num_scalar_prefetch=1, grid=(N_CHIPS, pl.cdiv(r, tr)),
            in_specs=[pl.BlockSpec((1, tr, hw), lambda s, i, cr: (s, i, cr[0])), pl.BlockSpec((1, tr, hw), lambda s, i, cr: (s, i, 0))],
            out_specs=pl.BlockSpec((1, tr, hw), lambda s, i, cr: (s, i, 0)),
        ),
        out_shape=jax.ShapeDtypeStruct((N_CHIPS, r, hw), BF16), compiler_params=_cparams(("parallel", "parallel")), name=name,
    )(c_idx, p4, recv)


def _sum_slots(x, own, sel, name, out_cols=None):
    s, r, c = x.shape
    tr = 128 if r % 128 == 0 else r

    def body(sel_ref, x_ref, own_ref, o_ref):
        mine = own_ref[0].astype(F32)
        acc = jnp.zeros((tr, c), F32)
        for k in range(s):
            acc = acc + jnp.where(sel_ref[0] == k, mine, x_ref[k].astype(F32))
        o_ref[...] = acc

    return pl.pallas_call(
        body,
        grid_spec=pltpu.PrefetchScalarGridSpec(
            num_scalar_prefetch=1, grid=(pl.cdiv(r, tr),),
            in_specs=[pl.BlockSpec((s, tr, c), lambda i, sr: (0, i, 0)), pl.BlockSpec((1, tr, c), lambda i, sr: (sr[1], i, 0))],
            out_specs=pl.BlockSpec((tr, c), lambda i, sr: (i, sr[2])),
        ),
        out_shape=jax.ShapeDtypeStruct((r, out_cols or c), F32), compiler_params=_cparams(("parallel",)), name=name,
    )(sel, x, own)


def _half_cols(width, hc):
    hw = width // 2
    assert hw % LANE == 0
    return pl.ds(pl.multiple_of(hc * hw, LANE), hw)


def _remote(src, dst, ssem, rsem, k, to):
    return pltpu.make_async_remote_copy(src_ref=src, dst_ref=dst, send_sem=ssem.at[k], recv_sem=rsem.at[k], device_id=to,
                                        device_id_type=MESH)


def _gather_job(bufs, mid_at=0.5):
    def part(o, a, slot, hc):
        return o[a].at[slot, :, _half_cols(bufs[a].shape[2], hc)]

    def first(ins, o, fresh, ssem, rsem):
        x, y, c, chips = _place()
        for a in range(len(bufs)):
            mine = part(o, a, 2 * x + y, c)
            for j, chip in enumerate(chips):
                _remote(mine, mine, ssem, rsem, 6 * a + j, (chip[0], chip[1], c)).start()

    def mid(ins, o, fresh, ssem, rsem):
        x, y, c, chips = _place()
        for a in range(len(bufs)):
            for j, chip in enumerate(chips):
                got = part(o, a, 2 * chip[0] + chip[1], c)
                _remote(got, got, ssem, rsem, 6 * a + j, (x, y, c)).wait_recv()
                _remote(got, got, ssem, rsem, 6 * a + 3 + j, (x, y, 1 - c)).start()

    def last(ins, o, fresh, ssem, rsem):
        x, y, c, chips = _place()
        for a in range(len(bufs)):
            for j, chip in enumerate(chips):
                got = part(o, a, 2 * chip[0] + chip[1], 1 - c)
                _remote(got, got, ssem, rsem, 6 * a + 3 + j, (x, y, c)).wait_recv()
        for a in range(len(bufs)):
            mine = part(o, a, 2 * x + y, c)
            for j, chip in enumerate(chips):
                _remote(mine, mine, ssem, rsem, 6 * a + j, (x, y, c)).wait_send()
                passed = part(o, a, 2 * chip[0] + chip[1], c)
                _remote(passed, passed, ssem, rsem, 6 * a + 3 + j, (x, y, c)).wait_send()

    return _Job([], bufs, [], 6 * len(bufs), first, mid, last, mid_at)


def _swap_job(p4s):
    def pairs(ins, fresh, c):
        return [(a, s, ins[a].at[s, :, _half_cols(p4s[a].shape[2], 1 - c)], fresh[a].at[s])
                for a in range(len(p4s)) for s in range(N_CHIPS)]

    def first(ins, inout, fresh, ssem, rsem):
        x, y, c, _ = _place()
        for a, s, src, dst in pairs(ins, fresh, c):
            _remote(src, dst, ssem, rsem, N_CHIPS * a + s, (x, y, 1 - c)).start()

    def last(ins, inout, fresh, ssem, rsem):
        x, y, c, _ = _place()
        for a, s, src, dst in pairs(ins, fresh, c):
            _remote(src, dst, ssem, rsem, N_CHIPS * a + s, (x, y, 1 - c)).wait()

    fresh = [jax.ShapeDtypeStruct(p.shape[:2] + (p.shape[2] // 2,), p.dtype) for p in p4s]
    return _Job(p4s, [], fresh, N_CHIPS * len(p4s), first, None, last)


def _scatter_job(parts):
    def first(ins, inout, fresh, ssem, rsem):
        x, y, c, chips = _place()
        for a in range(len(parts)):
            for j, chip in enumerate(chips):
                _remote(ins[a].at[2 * chip[0] + chip[1]], fresh[a].at[2 * x + y], ssem, rsem, 3 * a + j, (chip[0], chip[1], c)).start()

    def last(ins, inout, fresh, ssem, rsem):
        x, y, c, chips = _place()
        for a in range(len(parts)):
            for j, chip in enumerate(chips):
                slot = 2 * chip[0] + chip[1]
                _remote(ins[a].at[slot], fresh[a].at[slot], ssem, rsem, 3 * a + j, (x, y, c)).wait()

    return _Job(parts, [], [jax.ShapeDtypeStruct(p.shape, p.dtype) for p in parts], 3 * len(parts), first, None, last)


def _share_job(gs):
    def halves(o, a, c):
        width = gs[a].shape[1]
        return o[a].at[:, _half_cols(width, c)], o[a].at[:, _half_cols(width, 1 - c)]

    def first(ins, o, fresh, ssem, rsem):
        x, y, c, _ = _place()
        for a in range(len(gs)):
            mine, _ = halves(o, a, c)
            _remote(mine, mine, ssem, rsem, a, (x, y, 1 - c)).start()

    def last(ins, o, fresh, ssem, rsem):
        x, y, c, _ = _place()
        for a in range(len(gs)):
            mine, theirs = halves(o, a, c)
            _remote(mine, theirs, ssem, rsem, a, (x, y, 1 - c)).wait()

    return _Job([], gs, [], len(gs), first, None, last)


def _gather_all_job(buf):
    def peers():
        x, y, c, _ = _place()
        flips = [(fx, fy, fc) for fx in (0, 1) for fy in (0, 1) for fc in (0, 1)][1:]
        return (x, y, c), [((1 - x) if fx else x, (1 - y) if fy else y, (1 - c) if fc else c) for fx, fy, fc in flips]

    def first(ins, inout, fresh, ssem, rsem):
        (x, y, c), others = peers()
        for k, peer in enumerate(others):
            _remote(ins[0], fresh[0].at[4 * x + 2 * y + c], ssem, rsem, k, peer).start()

    def last(ins, inout, fresh, ssem, rsem):
        me, others = peers()
        for k, peer in enumerate(others):
            _remote(ins[0], fresh[0].at[4 * peer[0] + 2 * peer[1] + peer[2]], ssem, rsem, k, me).wait()

    return _Job([buf], [], [jax.ShapeDtypeStruct((N_DEV,) + buf.shape, buf.dtype)], N_DEV - 1, first, None, last)


class _SemView:
    def __init__(self, sems, off):
        self.sems, self.off = sems, off

    @property
    def at(self):
        return self

    def __getitem__(self, k):
        return self.sems.at[k + self.off]


def _join(jobs):
    spans, pos = [], [0, 0, 0, 0]
    for j in jobs:
        nxt = [pos[0] + len(j.ins), pos[1] + len(j.inout), pos[2] + len(j.fresh), pos[3] + j.nsem]
        spans.append((pos, nxt))
        pos = nxt

    def hook(which):
        fns = [getattr(j, which) for j in jobs]
        if all(f is None for f in fns):
            return None

        def run(ins, inout, fresh, ssem, rsem):
            for fn, (lo, hi) in zip(fns, spans):
                if fn is not None:
                    fn(ins[lo[0]:hi[0]], inout[lo[1]:hi[1]], fresh[lo[2]:hi[2]], _SemView(ssem, lo[3]), _SemView(rsem, lo[3]))

        return run

    mids = [j.mid_at for j in jobs if j.mid is not None]
    joined = _Job([a for j in jobs for a in j.ins], [a for j in jobs for a in j.inout], [a for j in jobs for a in j.fresh],
                  pos[3], hook("first"), hook("mid"), hook("last"), max(mids) if mids else 0.5)
    n_io = pos[1]

    def split(moved):
        return [list(moved[lo[1]:hi[1]]) + list(moved[n_io + lo[2]:n_io + hi[2]]) for lo, hi in spans]

    return joined, split


def _carrying(stages, call):
    stages = [s for s in stages if s is not None]
    if not stages:
        return call(None)
    job, split = _join([s[0] for s in stages])
    out, moved = call(job)
    for (_, done), part in zip(stages, split(moved)):
        done(part)
    return out


def _layer_forward(x, h, w_in_t, rest, sm, g_next, d, stages=None):
    stages = stages or {}
    proj = _carrying([stages.get("proj")], lambda job: _matmul(h, w_in_t, "nt", BF16, "proj_fwd", n=7 * d, tn_cap=1792, job=job))
    f_t = _matmul(w_in_t[7 * d:], h, "nt", F32, "forget_fwd", tn_cap=1024)
    c_t = _fox_prep(f_t, sm["b_f"], "fox_prep")
    o, lse = _carrying([stages.get("attn")], lambda job: _attn_fwd(proj, c_t, d, "attn_fwd", job=job))
    wts = rest()
    merged = _mix_fwd(proj, o, sm["wm"], sm["bs"], sm["g_v"], d, "mix_fwd")
    z, x1, h2 = _matmul(merged, wts["w_out"], "nn", F32, "out_fwd", norms=(x, sm["g_post"], sm["g_fpre"]))
    a = _carrying([stages.get("gate")], lambda job: _matmul(h2, wts["w_g_t"], "nt", BF16, "gate_fwd", tn_cap=1408, job=job))
    b = _carrying([stages.get("up")], lambda job: _matmul(h2, wts["w_u_t"], "nt", BF16, "up_fwd", tn_cap=1408, job=job))
    z2, x_out, h_out, mm = _carrying([stages.get("down")], lambda job: _matmul(
        b, wts["w_d"], "nn", F32, "down_fwd", norms=(x1, sm["g_fpost"], g_next), silu_of=a, job=job))
    return dict(x=x, h=h, proj=proj, f_t=f_t, c_t=c_t, o=o, lse=lse, merged=merged, z=z, x1=x1,
                h2=h2, a=a, b=b, mm=mm, z2=z2, x_out=x_out, h_out=h_out)


class _GradExchange:
    def __init__(self, pay, keys, c_idx, chip):
        self.keys = list(keys)
        self.p4 = [pay[k].reshape(N_CHIPS, pay[k].shape[1] // N_CHIPS, pay[k].shape[2]) for k in self.keys]
        self.c_idx = c_idx
        self.sel = jnp.stack([chip, chip, c_idx[0]]).astype(jnp.int32)
        self.done = 0

    def _after_swap(self, landed):
        self.parts = [_add_half(p, r, self.c_idx, "add_sibling") for p, r in zip(self.p4, landed)]
        self.done = 1

    def _after_scatter(self, landed):
        self.g = [_sum_slots(got, sent, self.sel, "sum_chips", out_cols=p.shape[2])
                  for got, sent, p in zip(landed, self.parts, self.p4)]
        self.done = 2

    def _after_share(self, moved):
        self.g = list(moved)
        self.done = 3

    def stage(self):
        if self.done == 0:
            return _swap_job(self.p4), self._after_swap
        if self.done == 1:
            return _scatter_job(self.parts), self._after_scatter
        if self.done == 2:
            return _share_job(self.g), self._after_share
        return None

    def run(self):
        for name in ("swap_grads", "scatter_grads", "share_grads")[self.done:]:
            job, done = self.stage()
            done(_run_job(job, name))

    def grads(self):
        return dict(zip(self.keys, self.g))


EARLY_KEYS = ("w_d", "w_g", "w_u", "w_out")


def _layer_backward(dz2, dx2, sv, wts, sm, d, c_idx, chip, carried=(), split_own=False, small_stage=None):
    t = dx2.shape[0]
    heads = d // LANE
    ff = wts["w_d"].shape[0]
    in_w = 7 * d + heads
    g, pay = {}, {}
    carried = list(carried)

    def payload(key, a, b, rows, row0, name, extra=()):
        def call(job):
            return _matmul(a, b, "tn", BF16, name, slab=((1, rows, d), 0, row0), into=pay.get(key), job=job, tm_cap=1408,
                           tn_cap=1024, tk_cap=1024)
        pay[key] = _carrying(list(extra), call)

    def nxt(*exchanges):
        return [ex.stage() for ex in exchanges]

    dm = _carrying(nxt(*carried), lambda job: _matmul(dz2, wts["w_d"], "nt", BF16, "down_bwd_x", tn_cap=1408, tk_cap=1024, job=job))
    payload("w_d", sv["mm"], dz2, ff, 0, "down_bwd_w")
    da, db = _swiglu_bwd(sv["a"], sv["b"], dm, "swiglu_bwd")
    dh2 = _matmul_pieces([(da, wts["w_g_t"], 0), (db, wts["w_u_t"], 0)], None, "gu_bwd_x", tk=_tile(ff, 1408))
    payload("w_g", da, sv["h2"], ff, 0, "gate_bwd_w")
    payload("w_u", db, sv["h2"], ff, 0, "up_bwd_w")
    dx1, dz, g["g_fpre"], g["g_post"] = _norm_bwd(dx2, (dh2, sv["x1"], sm["g_fpre"]), (sv["z"], sm["g_post"]), "norm_bwd_mid")
    dmerged = _matmul(dz, wts["w_out"], "nt", F32, "out_bwd_x", tk_cap=1024)
    payload("w_out", sv["merged"], dz, d, 0, "out_bwd_w")
    early = [_GradExchange(pay, EARLY_KEYS, c_idx, chip)] if split_own else []
    d_uv, d_g, do, g["w_s"], g["b_s"], g["g_v"] = _carrying(nxt(*early), lambda job: _mix_bwd(
        dmerged, sv["proj"], sv["o"], sm["wm"], sm["wm_t"], sm["bs"], sm["g_v"], d, "mix_bwd", job=job))
    extra = [small_stage(g)] if small_stage is not None else []
    attn_args = (sv["proj"], do, sv["o"], sv["lse"], sv["c_t"], d)
    dq, dk, dv, dc_q, dc_k = _carrying(nxt(*carried) + extra, lambda job: _attn_bwd(*attn_args, "attn_bwd", job=job))
    df_t, g["b_f"] = _fox_bwd(dc_q, dc_k, sv["f_t"], sm["b_f"], "fox_bwd")
    df_b = df_t.astype(BF16)
    pieces = [(d_uv, COL_U), (dq, COL_Q), (dk, COL_K), (dv, COL_VA), (d_g, COL_GA)]
    pay["w_in"] = _carrying(nxt(*carried, *early), lambda job: _matmul_rows([p for p, _ in pieces], sv["h"], in_w,
                                                                           "proj_bwd_w", job=job))
    w_f_rows = _carrying(nxt(*early), lambda job: _matmul(df_b, sv["h"], "nn", BF16, "forget_bwd_w", tk_cap=1024, job=job))
    pay["w_in"] = lax.dynamic_update_slice(pay["w_in"], w_f_rows[None], (0, 7 * d, 0))
    late = _GradExchange(pay, [k for k in ("w_in",) + EARLY_KEYS if not (split_own and k in EARLY_KEYS)], c_idx, chip)
    mine = [late] if split_own else []
    dh_f = _carrying(nxt(*mine), lambda job: _matmul(df_b, wts["w_in_t"][7 * d:], "tn", F32, "forget_bwd_x", job=job))
    ops = [(p, wts["w_in_t"], col * d) for p, col in pieces]
    dh = _carrying(nxt(*mine), lambda job: _matmul_pieces(ops, dh_f, "proj_bwd_x", job=job, tk=_tile(d, 1024)))
    return dh, dx1, g, early + [late]


def _small_pack(parts):
    flat = jnp.concatenate([p.reshape(-1) for p in parts])
    n = flat.shape[0]
    pad = (-n) % (LANE * LANE)
    return jnp.pad(flat, (0, pad)).reshape(-1, LANE)


def kernel(x, mix_pre_g, w_in, b_forget, sgu_norm_g, w_spatial, b_spatial, w_out, mix_post_g, ffn_pre_g, w_gate, w_up, w_down, ffn_post_g, loss_target, m_mix_pre_g, m_w_in, m_b_forget, m_sgu_norm_g, m_w_spatial, m_b_spatial, m_w_out, m_mix_post_g, m_ffn_pre_g, m_w_gate, m_w_up, m_w_down, m_ffn_post_g, v_mix_pre_g, v_w_in, v_b_forget, v_sgu_norm_g, v_w_spatial, v_b_spatial, v_w_out, v_mix_post_g, v_ffn_pre_g, v_w_gate, v_w_up, v_w_down, v_ffn_post_g):
    depth, d = mix_pre_g.shape
    assert depth == 2, "the AdamW kernels and the exchange schedule are written for two blocks"
    heads = d // LANE
    t = x.shape[1]
    ff = w_down.shape[1] * N_CHIPS
    in_w = w_in.shape[2] * N_CHIPS
    assert in_w == 7 * d + heads
    xs = x.reshape(t, d)
    target = loss_target.reshape(t, d)
    c_idx = lax.axis_index("c").astype(jnp.int32).reshape(1)
    chip = 2 * lax.axis_index("x") + lax.axis_index("y")
    dev = 2 * chip + lax.axis_index("c")

    def in_view(w):
        return jnp.transpose(w, (2, 0, 1))

    def gu_view(w):
        return jnp.transpose(w, (0, 2, 1))

    own = [jnp.transpose(in_view(w_in).astype(BF16), (1, 0, 2)), w_out.astype(BF16), gu_view(w_gate).astype(BF16),
           gu_view(w_up).astype(BF16), w_down.astype(BF16)]
    bufs = [[lax.dynamic_update_slice(lax.empty((N_CHIPS,) + o.shape[1:], BF16), o[l][None], (chip, 0, 0)) for o in own]
            for l in range(depth)]
    first_in = _run_job(_gather_job([bufs[0][0]]), "gather_first")[0]

    def weights(g_in, g_out, g_g, g_u, g_d):
        return dict(w_in_t=g_in.reshape(in_w, d), w_out=g_out.reshape(d, d), w_g_t=g_g.reshape(ff, d),
                    w_u_t=g_u.reshape(ff, d), w_d=g_d.reshape(ff, d))

    tril = jnp.tril(jnp.ones((LANE, LANE), bool))
    smalls = []
    for l in range(depth):
        wm = jnp.where(tril[None], w_spatial[l], 0.0).astype(BF16)
        smalls.append(dict(
            b_f=b_forget[l].reshape(heads, 1), wm=wm, wm_t=jnp.swapaxes(wm, 1, 2), bs=b_spatial[l].reshape(heads, LANE, 1),
            g_v=sgu_norm_g[l].reshape(1, d), g_pre=mix_pre_g[l].reshape(1, d), g_post=mix_post_g[l].reshape(1, d),
            g_fpre=ffn_pre_g[l].reshape(1, d), g_fpost=ffn_post_g[l].reshape(1, d)))

    wts, later = [], {}

    def keep(key):
        def done(moved):
            later[key] = list(moved)
        return done

    def rest_first():
        wts.append(weights(first_in, later["out0"][0], *later["attn"][:3]))
        return wts[0]

    stages = dict(proj=(_gather_job(bufs[0][1:2], mid_at=1.0), keep("out0")),
                  attn=(_gather_job(bufs[0][2:5] + bufs[1][0:2], mid_at=0.7), keep("attn")),
                  gate=(_gather_job(bufs[1][2:3], mid_at=1.0), keep("g1")), up=(_gather_job(bufs[1][3:4], mid_at=1.0), keep("u1")),
                  down=(_gather_job(bufs[1][4:5], mid_at=1.0), keep("d1")))
    h = _norm_fwd(xs, None, None, smalls[0]["g_pre"], "norm_first")
    g_after = [smalls[min(l + 1, depth - 1)]["g_pre"] for l in range(depth)]
    saved = [_layer_forward(xs, h, first_in.reshape(in_w, d), rest_first, smalls[0], g_after[0], d, stages)]
    wts.append(weights(*later["attn"][3:5], later["g1"][0], later["u1"][0], later["d1"][0]))
    for l in range(1, depth):
        saved.append(_layer_forward(saved[l - 1]["x_out"], saved[l - 1]["h_out"], wts[l]["w_in_t"], lambda l=l: wts[l], smalls[l],
                                    g_after[l], d))
    dy, loss_part = _loss_grad(saved[-1]["x_out"], target, "loss")
    loss = lax.psum(jnp.sum(loss_part), ("x", "y", "c"))

    small_shapes = dict(g_pre=(d,), b_f=(heads,), g_v=(d,), w_s=w_spatial.shape[1:], b_s=b_spatial.shape[1:], g_post=(d,),
                        g_fpre=(d,), g_fpost=(d,))
    late_entries = [(0, "g_pre"), (0, "b_f")]
    early_entries = [(l, n) for l in reversed(range(depth)) for n in small_shapes if (l, n) not in late_entries]
    dev_sel = jnp.stack([dev, jnp.zeros_like(dev), jnp.zeros_like(dev)]).astype(jnp.int32)
    small_sum = {}

    def small_exchange(entries, values):
        packed = _small_pack([values[e].reshape(-1) for e in entries])

        def done(moved):
            total = _sum_slots(moved[0], packed[None], dev_sel, "sum_small").reshape(-1)
            off = 0
            for e in entries:
                n = math.prod(small_shapes[e[1]])
                small_sum[e] = total[off:off + n].reshape(small_shapes[e[1]])
                off += n

        return _gather_all_job(packed), done

    grads = [None] * depth
    exchanges = [None] * depth
    dx2 = dy
    dz2, g_fpost = _norm_bwd(dx2, None, (saved[depth - 1]["z2"], smalls[depth - 1]["g_fpost"]), "norm_bwd_top")
    for l in reversed(range(depth)):
        last = l == 0

        def small_stage(g, l=l, g_fpost=g_fpost):
            known = {(k, n): grads[k][n] for k in range(l + 1, depth) for n in small_shapes}
            known.update({(l, n): g[n] for n in g})
            known[(l, "g_fpost")] = g_fpost
            return small_exchange(early_entries, known)

        carried = [ex for k in range(l + 1, depth) for ex in exchanges[k]]
        dh, dx1, g, exchanges[l] = _layer_backward(dz2, dx2, saved[l], wts[l], smalls[l], d, c_idx, chip, carried=carried,
                                                    split_own=last, small_stage=small_stage if last else None)
        g["g_fpost"] = g_fpost
        if l > 0:
            dx2, dz2, g["g_pre"], g_fpost = _norm_bwd(dx1, (dh, saved[l]["x"], smalls[l]["g_pre"]),
                                                       (saved[l - 1]["z2"], smalls[l - 1]["g_fpost"]), "norm_bwd_between")
        else:
            grad_x, g["g_pre"] = _norm_bwd(dx1, (dh, saved[l]["x"], smalls[l]["g_pre"]), None, "norm_bwd_bottom")
        grads[l] = g
    job, done = small_exchange(late_entries, {(0, n): grads[0][n] for n in ("g_pre", "b_f")})
    done(_run_job(job, "gather_small"))
    big = [{} for _ in range(depth)]
    for l in range(depth):
        for ex in exchanges[l]:
            ex.run()
            big[l].update(ex.grads())
    small_grads = {n: jnp.stack([small_sum[(l, n)] for l in range(depth)]) for n in small_shapes}

    def adam_small(w, g, m, v):
        shp = w.shape
        if w.ndim >= 3 and shp[-1] >= LANE:
            two = (math.prod(shp[:-1]), shp[-1])
        else:
            two = (1, math.prod(shp)) if math.prod(shp) < LANE else (math.prod(shp) // LANE, LANE)
        outs = _adamw(w.reshape(two), g.reshape(two), m.reshape(two), v.reshape(two), "adamw")
        return [g] + [o.reshape(shp) for o in outs]

    def adam_in(w, m, v):
        outs = _adamw_interleaved(in_view(w), big[0]["w_in"], big[1]["w_in"], in_view(m), in_view(v), "adamw_in")
        return [jnp.transpose(o, (1, 2, 0)) for o in outs]

    def adam_gu(k, w, m, v):
        outs = _adamw_layers(gu_view(w), big[0][k], big[1][k], gu_view(m), gu_view(v), "adamw_layers")
        return [jnp.transpose(o, (0, 2, 1)) for o in outs]

    def adam_rows(k, w, m, v):
        return _adamw_layers(w, big[0][k], big[1][k], m, v, "adamw_layers")

    results = [
        adam_small(mix_pre_g, small_grads["g_pre"], m_mix_pre_g, v_mix_pre_g),
        adam_in(w_in, m_w_in, v_w_in),
        adam_small(b_forget, small_grads["b_f"], m_b_forget, v_b_forget),
        adam_small(sgu_norm_g, small_grads["g_v"], m_sgu_norm_g, v_sgu_norm_g),
        adam_small(w_spatial, small_grads["w_s"], m_w_spatial, v_w_spatial),
        adam_small(b_spatial, small_grads["b_s"], m_b_spatial, v_b_spatial),
        adam_rows("w_out", w_out, m_w_out, v_w_out),
        adam_small(mix_post_g, small_grads["g_post"], m_mix_post_g, v_mix_post_g),
        adam_small(ffn_pre_g, small_grads["g_fpre"], m_ffn_pre_g, v_ffn_pre_g),
        adam_gu("w_g", w_gate, m_w_gate, v_w_gate),
        adam_gu("w_u", w_up, m_w_up, v_w_up),
        adam_rows("w_d", w_down, m_w_down, v_w_down),
        adam_small(ffn_post_g, small_grads["g_fpost"], m_ffn_post_g, v_ffn_post_g),
    ]
    gs, deltas, new_ms, new_vs = zip(*results)
    return (loss, grad_x.reshape(x.shape), *gs, *deltas, *new_ms, *new_vs)
```

```python
import functools
import math

import jax
import jax.numpy as jnp
from jax import lax
from jax.experimental import pallas as pl
from jax.experimental.pallas import tpu as pltpu

F32 = jnp.float32
BF16 = jnp.bfloat16

EPS = 1e-6
LANE = 128
SUBLANE = 8
N_CHIPS = 4
N_DEV = 8
VMEM_LIMIT = 48 * 1024 * 1024
MESH = pl.DeviceIdType.MESH

ADAM_LR = 0.001
ADAM_B1 = 0.9
ADAM_B2 = 0.999
ADAM_EPS = 1e-08
ADAM_WD = 0.01
ADAM_STEP = 10
ADAM_C1 = 1.0 / (1.0 - ADAM_B1 ** ADAM_STEP)
ADAM_C2 = 1.0 / (1.0 - ADAM_B2 ** ADAM_STEP)

GELU_K = math.sqrt(2.0 / math.pi)
GELU_A = 0.044715
NEG = -1e30
LOG2E = 1.4426950408889634
LN2 = 0.6931471805599453

COL_U, COL_V, COL_Q, COL_K, COL_VA, COL_GA, COL_GB, COL_F = range(8)


def _cparams(sem=None):
    return pltpu.CompilerParams(dimension_semantics=sem, vmem_limit_bytes=VMEM_LIMIT)


def _tile(n, cap):
    best = None
    for t in range(LANE, min(n, cap) + 1, LANE):
        if n % t == 0:
            best = t
    return best if best is not None else n


def _rows(n, cap):
    best = None
    for t in range(SUBLANE, min(n, cap) + 1, SUBLANE):
        if n % t == 0:
            best = t
    return best if best is not None else n


def _gelu_and_grad(x):
    x2 = x * x
    t = jnp.tanh(GELU_K * (x + GELU_A * x2 * x))
    g = 0.5 * x * (1.0 + t)
    dg = 0.5 * (1.0 + t) + 0.5 * x * (1.0 - t * t) * (GELU_K * (1.0 + 3.0 * GELU_A * x2))
    return g, dg


def _sigmoid(x):
    return 1.0 / (1.0 + jnp.exp(-x))


def _sum8(v):
    n, d = v.shape
    return v.reshape(n // SUBLANE, SUBLANE, d).sum(axis=0)


def _nt_dot(a, b):
    return lax.dot_general(a, b, (((1,), (1,)), ((), ())), preferred_element_type=F32)


_HBM = pl.BlockSpec(memory_space=pl.ANY)


def _place():
    x, y, c = lax.axis_index("x"), lax.axis_index("y"), lax.axis_index("c")
    chips = [(1 - x, y), (x, 1 - y), (1 - x, 1 - y)]
    return x, y, c, chips


class _Job:
    def __init__(self, ins, inout, fresh, nsem, first, mid, last, mid_at=0.5):
        self.ins, self.inout, self.fresh, self.nsem = list(ins), list(inout), list(fresh), nsem
        self.first, self.mid, self.last, self.mid_at = first, mid, last, mid_at


def _call(body, *, grid, in_specs, out_specs, out_shape, scratch_shapes, dims, name, args, aliases=None, job=None):
    single = not isinstance(out_shape, (list, tuple))
    out_specs = [out_specs] if single else list(out_specs)
    out_shape = [out_shape] if single else list(out_shape)
    aliases = dict(aliases or {})
    if job is None:
        outs = pl.pallas_call(body, grid=grid, in_specs=in_specs, out_specs=out_specs, out_shape=out_shape,
                              scratch_shapes=scratch_shapes, input_output_aliases=aliases, compiler_params=_cparams(dims),
                              name=name)(*args)
        return (outs[0] if single else outs), []
    n_in, n_out, n_scr = len(args), len(out_shape), len(scratch_shapes)
    n_ji, n_jio, n_jf = len(job.ins), len(job.inout), len(job.fresh)
    total = math.prod(grid)

    def wrapped(*refs):
        host_in = refs[:n_in]
        pos = n_in
        j_ins = refs[pos:pos + n_ji]
        pos += n_ji + n_jio
        host_out = refs[pos:pos + n_out]
        pos += n_out
        j_inout = refs[pos:pos + n_jio]
        pos += n_jio
        j_fresh = refs[pos:pos + n_jf]
        pos += n_jf
        host_scr = refs[pos:pos + n_scr]
        ssem, rsem = refs[pos + n_scr:]
        flat = 0
        for ax, size in enumerate(grid):
            flat = flat * size + pl.program_id(ax)

        def hook(fn, at):
            if fn is not None:
                @pl.when(flat == at)
                def _():
                    fn(j_ins, j_inout, j_fresh, ssem, rsem)

        hook(job.first, 0)
        body(*host_in, *host_out, *host_scr)
        hook(job.mid, min(int(total * job.mid_at), total - 1))
        hook(job.last, total - 1)

    for k in range(n_jio):
        aliases[n_in + n_ji + k] = n_out + k
    outs = pl.pallas_call(
        wrapped, grid=grid,
        in_specs=list(in_specs) + [_HBM] * (n_ji + n_jio),
        out_specs=out_specs + [_HBM] * (n_jio + n_jf),
        out_shape=out_shape + [jax.ShapeDtypeStruct(b.shape, b.dtype) for b in job.inout] + list(job.fresh),
        scratch_shapes=list(scratch_shapes) + [pltpu.SemaphoreType.DMA((job.nsem,)), pltpu.SemaphoreType.DMA((job.nsem,))],
        input_output_aliases=aliases, compiler_params=_cparams(tuple("arbitrary" for _ in grid)), name=name,
    )(*args, *job.ins, *job.inout)
    host = outs[:n_out]
    return (host[0] if single else host), outs[n_out:]


def _run_job(job, name):
    n_ji, n_jio, n_jf = len(job.ins), len(job.inout), len(job.fresh)

    def body(*refs):
        j_ins = refs[:n_ji]
        pos = n_ji + n_jio
        j_inout = refs[pos:pos + n_jio]
        j_fresh = refs[pos + n_jio:pos + n_jio + n_jf]
        ssem, rsem = refs[pos + n_jio + n_jf:]
        for fn in (job.first, job.mid, job.last):
            if fn is not None:
                fn(j_ins, j_inout, j_fresh, ssem, rsem)

    return pl.pallas_call(
        body, in_specs=[_HBM] * (n_ji + n_jio), out_specs=[_HBM] * (n_jio + n_jf),
        out_shape=[jax.ShapeDtypeStruct(b.shape, b.dtype) for b in job.inout] + list(job.fresh),
        scratch_shapes=[pltpu.SemaphoreType.DMA((job.nsem,)), pltpu.SemaphoreType.DMA((job.nsem,))],
        input_output_aliases={n_ji + k: k for k in range(n_jio)}, name=name,
    )(*job.ins, *job.inout)


_DIMS ={"nn": ((1,), (0,)), "nt": ((1,), (1,)), "tn": ((0,), (0,))}


def _matmul(a, b, mode, out_dtype, name, n=None, slab=None, into=None, job=None, norms=None, silu_of=None, tm_cap=512,
            tn_cap=2048, tk_cap=1408):
    if mode == "nn":
        (m, k), (k2, nn_) = a.shape, b.shape
    elif mode == "nt":
        (m, k), (nn_, k2) = a.shape, b.shape
    else:
        (k, m), (k2, nn_) = a.shape, b.shape
    n = nn_ if n is None else n
    assert k == k2, (a.shape, b.shape, mode)
    tm, tn, tk = _tile(m, tm_cap), _tile(n, tn_cap), _tile(k, tk_cap)
    if slab is not None and slab[2]:
        tm = _tile(math.gcd(m, slab[2]), tm_cap)
    nk = k // tk
    if mode == "tn":
        a_spec = pl.BlockSpec((tk, tm), lambda j, i, kk, *_: (kk, i))
    else:
        a_spec = pl.BlockSpec((tm, tk), lambda j, i, kk, *_: (i, kk))
    if mode == "nt":
        b_spec = pl.BlockSpec((tn, tk), lambda j, i, kk, *_: (j, kk))
    else:
        b_spec = pl.BlockSpec((tk, tn), lambda j, i, kk, *_: (kk, j))
    dims = (_DIMS[mode], ((), ()))
    aliased = into is not None

    n_a = 1 if silu_of is None else 2
    n_in = n_a + 1 + aliased + (3 if norms is not None else 0)
    n_main = 1 + (2 if norms is not None else 0)

    def finish(refs, z):
        refs[n_in][...] = z.astype(out_dtype).reshape(refs[n_in].shape)
        if norms is not None:
            x_ref, gp_ref, gn_ref = refs[n_in - 3:n_in]
            r = lax.rsqrt(jnp.mean(z * z, axis=-1, keepdims=True) + EPS)
            xn = x_ref[...] + z * r * gp_ref[...]
            refs[n_in + 1][...] = xn
            r2 = lax.rsqrt(jnp.mean(xn * xn, axis=-1, keepdims=True) + EPS)
            refs[n_in + 2][...] = (xn * r2 * gn_ref[...]).astype(BF16)

    def body(*refs):
        lhs = refs[0][...]
        if silu_of is not None:
            gv = refs[1][...].astype(F32)
            lhs = (gv * _sigmoid(gv) * lhs.astype(F32)).astype(BF16)
            refs[n_in + n_main][...] = lhs
        p = lax.dot_general(lhs, refs[n_a][...], dims, preferred_element_type=F32)
        if nk == 1:
            finish(refs, p)
        else:
            acc = refs[-1]
            kk = pl.program_id(2)

            @pl.when(kk == 0)
            def _():
                acc[...] = p

            @pl.when(kk > 0)
            def _():
                acc[...] += p

            @pl.when(kk == nk - 1)
            def _():
                finish(refs, acc[...])

    if slab is None:
        out_spec = pl.BlockSpec((tm, tn), lambda j, i, kk: (i, j))
        out_shape = jax.ShapeDtypeStruct((m, n), out_dtype)
    else:
        shape3, lead, row0 = slab
        assert row0 % tm == 0 and shape3[2] == n
        out_spec = pl.BlockSpec((1, tm, tn), lambda j, i, kk: (lead, row0 // tm + i, j))
        out_shape = jax.ShapeDtypeStruct(shape3, out_dtype)
    in_specs, args = [a_spec, b_spec], [a, b]
    if silu_of is not None:
        assert mode == "nn" and tn == n and slab is None, "the left operand's blocks are written once each"
        in_specs, args = [a_spec, a_spec, b_spec], [a, silu_of, b]
    if aliased:
        in_specs.append(pl.BlockSpec(memory_space=pl.ANY))
        args.append(into)
    if norms is not None:
        assert tn == n and slab is None, "the fused norms need whole rows"
        row = pl.BlockSpec((tm, n), lambda j, i, kk: (i, 0))
        vec = pl.BlockSpec((1, n), lambda j, i, kk: (0, 0))
        in_specs += [row, vec, vec]
        args += list(norms)
        out_spec = [out_spec, row, row]
        out_shape = [out_shape, jax.ShapeDtypeStruct((m, n), F32), jax.ShapeDtypeStruct((m, n), BF16)]
    if silu_of is not None:
        out_spec = (out_spec if isinstance(out_spec, list) else [out_spec]) + [pl.BlockSpec((tm, tk), lambda j, i, kk: (i, kk))]
        out_shape = (out_shape if isinstance(out_shape, list) else [out_shape]) + [jax.ShapeDtypeStruct((m, k), BF16)]
    out, moved = _call(
        body, grid=(n // tn, m // tm, nk), in_specs=in_specs, out_specs=out_spec, out_shape=out_shape,
        scratch_shapes=[pltpu.VMEM((tm, tn), F32)] if nk > 1 else [], dims=("parallel", "parallel", "arbitrary"), name=name,
        args=args, aliases={n_a + 1: 0} if aliased else None, job=job)
    return out if job is None else (out, moved)


def _matmul_rows(pieces, b, rows, name, job=None, tm=1024, tk=1024):
    k, n = b.shape
    tm = math.gcd(tm, *[a.shape[1] for a in pieces])
    tk = _tile(k, tk)
    nk = k // tk
    spans, r0 = [], 0
    for a in pieces:
        assert a.shape[0] == k and a.shape[1] % tm == 0
        spans.append((r0, a.shape[1] // tm))
        r0 += a.shape[1] // tm
    nr = r0
    np_ = len(pieces)

    def body(*refs):
        b_ref, o_ref, acc = refs[np_], refs[np_ + 1], refs[-1]
        r, kk = pl.program_id(0), pl.program_id(1)
        for p, (first, count) in enumerate(spans):
            @pl.when((r >= first) & (r < first + count))
            def _(p=p):
                part = lax.dot_general(refs[p][...], b_ref[...], (_DIMS["tn"], ((), ())), preferred_element_type=F32)

                @pl.when(kk == 0)
                def _():
                    acc[...] = part

                @pl.when(kk > 0)
                def _():
                    acc[...] += part

        @pl.when(kk == nk - 1)
        def _():
            o_ref[0] = acc[...].astype(BF16)

    in_specs = []
    for first, count in spans:
        in_specs.append(pl.BlockSpec((tk, tm), lambda r, kk, f=first, c=count: (
            jnp.where(r < f, 0, jnp.where(r >= f + c, nk - 1, kk)), jnp.clip(r - f, 0, c - 1))))
    in_specs.append(pl.BlockSpec((tk, n), lambda r, kk: (kk, 0)))
    out, moved = _call(
        body, grid=(nr, nk), in_specs=in_specs, out_specs=pl.BlockSpec((1, tm, n), lambda r, kk: (0, r, 0)),
        out_shape=jax.ShapeDtypeStruct((1, rows, n), BF16), scratch_shapes=[pltpu.VMEM((tm, n), F32)],
        dims=("arbitrary", "arbitrary"), name=name, args=list(pieces) + [b], job=job)
    return out if job is None else (out, moved)


def _matmul_pieces(pieces, addend, name, tk, job=None, tm_cap=512):
    m = pieces[0][0].shape[0]
    n = pieces[0][1].shape[1]
    tm = _tile(m, tm_cap)
    spans, s0 = [], 0
    for a, b, row0 in pieces:
        assert a.shape[1] % tk == 0 and row0 % tk == 0 and b.shape[1] == n and a.shape[0] == m
        spans.append((s0, a.shape[1] // tk, row0 // tk))
        s0 += a.shape[1] // tk
    steps = s0
    np_ = len(pieces)
    groups = []
    for (a, b, _), (first, count, brow) in zip(pieces, spans):
        if groups and groups[-1][0] is b and groups[-1][3] + groups[-1][2] == brow:
            groups[-1][2] += count
        else:
            groups.append([b, first, count, brow])
    b_of = []
    for first, count, _ in spans:
        b_of.append(next(k for k, g in enumerate(groups) if g[1] <= first < g[1] + g[2]))
    ng = len(groups)

    nm = m // tm

    def body(*refs):
        o_ref, acc = refs[-2], refs[-1]
        s, i = pl.program_id(0), pl.program_id(1)
        rows = pl.ds(pl.multiple_of(i * tm, tm), tm)

        @pl.when(s == 0)
        def _():
            acc[rows, :] = refs[np_ + ng][...] if addend is not None else jnp.zeros((tm, n), F32)

        for p, (first, count, _) in enumerate(spans):
            @pl.when((s >= first) & (s < first + count))
            def _(p=p):
                acc[rows, :] += jnp.dot(refs[p][...], refs[np_ + b_of[p]][...], preferred_element_type=F32)

        @pl.when(s == steps - 1)
        def _():
            o_ref[...] = acc[rows, :]

    in_specs, args = [], []
    for (a, _, _), (first, count, _) in zip(pieces, spans):
        in_specs.append(pl.BlockSpec((tm, tk), lambda s, i, f=first, c=count: (
            jnp.where(s < f, 0, jnp.where(s >= f + c, nm - 1, i)), jnp.clip(s - f, 0, c - 1))))
        args.append(a)
    for b, first, count, brow in groups:
        in_specs.append(pl.BlockSpec((tk, n), lambda s, i, f=first, c=count, r=brow: (r + jnp.clip(s - f, 0, c - 1), 0)))
        args.append(b)
    if addend is not None:
        in_specs.append(pl.BlockSpec((tm, n), lambda s, i: (jnp.where(s == 0, i, nm - 1), 0)))
        args.append(addend)
    out, moved = _call(
        body, grid=(steps, nm), in_specs=in_specs,
        out_specs=pl.BlockSpec((tm, n), lambda s, i: (jnp.where(s == steps - 1, i, 0), 0)),
        out_shape=jax.ShapeDtypeStruct((m, n), F32), scratch_shapes=[pltpu.VMEM((m, n), F32)],
        dims=("arbitrary", "arbitrary"), name=name, args=args, job=job)
    return out if job is None else (out, moved)


def _norm_fwd(x, z, g_post, g_next, name):
    t, d = x.shape
    tt = _rows(t, 512)
    row = pl.BlockSpec((tt, d), lambda i: (i, 0))
    vec = pl.BlockSpec((1, d), lambda i: (0, 0))

    def body(*refs):
        if z is None:
            x_ref, gn_ref, h_ref = refs
            xn = x_ref[...]
        else:
            x_ref, z_ref, gp_ref, gn_ref, xo_ref, h_ref = refs
            zz = z_ref[...]
            r = lax.rsqrt(jnp.mean(zz * zz, axis=-1, keepdims=True) + EPS)
            xn = x_ref[...] + zz * r * gp_ref[...]
            xo_ref[...] = xn
        r2 = lax.rsqrt(jnp.mean(xn * xn, axis=-1, keepdims=True) + EPS)
        h_ref[...] = (xn * r2 * gn_ref[...]).astype(BF16)

    if z is None:
        return pl.pallas_call(
            body, grid=(t // tt,), in_specs=[row, vec], out_specs=row,
            out_shape=jax.ShapeDtypeStruct((t, d), BF16), compiler_params=_cparams(("parallel",)), name=name,
        )(x, g_next)
    return pl.pallas_call(
        body, grid=(t // tt,), in_specs=[row, row, vec, vec], out_specs=[row, row],
        out_shape=[jax.ShapeDtypeStruct((t, d), F32), jax.ShapeDtypeStruct((t, d), BF16)],
        compiler_params=_cparams(("parallel",)), name=name,
    )(x, z, g_post, g_next)


def _rms_bwd(dy, x, g):
    r = lax.rsqrt(jnp.mean(x * x, axis=-1, keepdims=True) + EPS)
    n = x * r
    dn = dy * g
    dx = r * (dn - n * jnp.mean(dn * n, axis=-1, keepdims=True))
    return dx, dy * n


def _norm_bwd(dres, pre, post, name):
    t, d = dres.shape
    tt = _rows(t, 512)
    nt = t // tt
    row = pl.BlockSpec((tt, d), lambda i: (i, 0))
    vec = pl.BlockSpec((1, d), lambda i: (0, 0))
    has_pre, has_post = pre is not None, post is not None
    n_in = 1 + (3 if has_pre else 0) + (2 if has_post else 0)
    n_out = has_pre + has_post + has_pre + has_post

    def body(*refs):
        ins, outs, scr = refs[:n_in], refs[n_in:n_in + n_out], refs[n_in + n_out:]
        i = pl.program_id(0)
        dx = ins[0][...]
        pos, opos, spos = 1, 0, 0
        accs = []
        if has_pre:
            dh_ref, xa_ref, ga_ref = ins[pos:pos + 3]
            pos += 3
            dxa, dga_t = _rms_bwd(dh_ref[...], xa_ref[...], ga_ref[...])
            dx = dx + dxa
            outs[opos][...] = dx
            opos += 1
            accs.append((scr[spos], dga_t))
            spos += 1
        if has_post:
            zb_ref, gb_ref = ins[pos:pos + 2]
            dz, dgb_t = _rms_bwd(dx, zb_ref[...], gb_ref[...])
            outs[opos][...] = dz.astype(BF16)
            opos += 1
            accs.append((scr[spos], dgb_t))
            spos += 1
        for (acc, val), out in zip(accs, outs[opos:]):
            part = _sum8(val)

            @pl.when(i == 0)
            def _(acc=acc, part=part):
                acc[...] = part

            @pl.when(i > 0)
            def _(acc=acc, part=part):
                acc[...] += part

            @pl.when(i == nt - 1)
            def _(acc=acc, out=out):
                out[...] = jnp.sum(acc[...], axis=0, keepdims=True)

    in_specs, args = [row], [dres]
    out_specs, out_shape = [], []
    if has_pre:
        in_specs += [row, row, vec]
        args += list(pre)
        out_specs.append(row)
        out_shape.append(jax.ShapeDtypeStruct((t, d), F32))
    if has_post:
        in_specs += [row, vec]
        args += list(post)
        out_specs.append(row)
        out_shape.append(jax.ShapeDtypeStruct((t, d), BF16))
    for _ in range(has_pre + has_post):
        out_specs.append(vec)
        out_shape.append(jax.ShapeDtypeStruct((1, d), F32))
    return pl.pallas_call(
        body, grid=(nt,), in_specs=in_specs, out_specs=out_specs, out_shape=out_shape,
        scratch_shapes=[pltpu.VMEM((SUBLANE, d), F32)] * (has_pre + has_post),
        compiler_params=_cparams(("arbitrary",)), name=name,
    )(*args)


def _loss_grad(y, target, name):
    t, d = y.shape
    tt = _rows(t, 512)
    nt = t // tt
    row = pl.BlockSpec((tt, d), lambda i: (i, 0))
    inv_d = 1.0 / d

    def body(y_ref, t_ref, dy_ref, l_ref):
        i = pl.program_id(0)
        diff = y_ref[...] - t_ref[...]
        dy_ref[...] = diff * inv_d
        s8 = _sum8(diff * diff)
        part = s8[:, 0:LANE]
        for k in range(1, d // LANE):
            part = part + s8[:, k * LANE:(k + 1) * LANE]
        part = part * (0.5 * inv_d)

        @pl.when(i == 0)
        def _():
            l_ref[...] = part

        @pl.when(i > 0)
        def _():
            l_ref[...] += part

    return pl.pallas_call(
        body, grid=(nt,), in_specs=[row, row],
        out_specs=[row, pl.BlockSpec((SUBLANE, LANE), lambda i: (0, 0))],
        out_shape=[jax.ShapeDtypeStruct((t, d), F32), jax.ShapeDtypeStruct((SUBLANE, LANE), F32)],
        compiler_params=_cparams(("arbitrary",)), name=name,
    )(y, target)


def _swiglu_bwd(a, b, dm, name):
    t, f = a.shape
    tt = _rows(t, 256)
    blk = pl.BlockSpec((tt, f), lambda i: (i, 0))

    def body(a_ref, b_ref, dm_ref, da_ref, db_ref):
        av = a_ref[...].astype(F32)
        s = _sigmoid(av)
        dv = dm_ref[...].astype(F32)
        da_ref[...] = (dv * b_ref[...].astype(F32) * s * (1.0 + av * (1.0 - s))).astype(BF16)
        db_ref[...] = (dv * av * s).astype(BF16)

    return pl.pallas_call(
        body, grid=(t // tt,), in_specs=[blk, blk, blk], out_specs=[blk, blk],
        out_shape=[jax.ShapeDtypeStruct((t, f), BF16)] * 2, compiler_params=_cparams(("parallel",)), name=name,
    )(a, b, dm)


def _log_sigmoid(x):
    return jnp.minimum(x, 0.0) - jnp.log1p(jnp.exp(-jnp.abs(x)))


def _fox_prep(f_t, b_f, name):
    h, t = f_t.shape

    def body(f_ref, b_ref, c_ref):
        r = lax.broadcasted_iota(jnp.int32, (LANE, LANE), 0)
        c = lax.broadcasted_iota(jnp.int32, (LANE, LANE), 1)
        upper = (r <= c).astype(F32)
        carry = jnp.zeros((h, 1), F32)
        for j in range(t // LANE):
            sl = slice(j * LANE, (j + 1) * LANE)
            lf = _log_sigmoid(f_ref[:, sl] + b_ref[...])
            cs = jnp.dot(lf, upper, precision=lax.Precision.HIGHEST, preferred_element_type=F32) + carry
            c_ref[:, sl] = cs
            carry = cs[:, LANE - 1:LANE]

    return pl.pallas_call(body, out_shape=jax.ShapeDtypeStruct((h, t), F32), compiler_params=_cparams(), name=name)(f_t, b_f)


def _fox_bwd(dc_q, dc_k, f_t, b_f, name):
    h, t = f_t.shape

    def body(dq_ref, dk_ref, f_ref, b_ref, df_ref, db_ref):
        r = lax.broadcasted_iota(jnp.int32, (LANE, LANE), 0)
        c = lax.broadcasted_iota(jnp.int32, (LANE, LANE), 1)
        lower = (r >= c).astype(F32)
        carry = jnp.zeros((h, 1), F32)
        dbsum = jnp.zeros((h, 1), F32)
        for j in reversed(range(t // LANE)):
            sl = slice(j * LANE, (j + 1) * LANE)
            dc = dq_ref[:, sl] - dk_ref[:, sl]
            dl = jnp.dot(dc, lower, precision=lax.Precision.HIGHEST, preferred_element_type=F32) + carry
            carry = dl[:, 0:1]
            df = dl * _sigmoid(-(f_ref[:, sl] + b_ref[...]))
            df_ref[:, sl] = df
            dbsum = dbsum + jnp.sum(df, axis=-1, keepdims=True)
        db_ref[...] = dbsum

    return pl.pallas_call(
        body, out_shape=[jax.ShapeDtypeStruct((h, t), F32), jax.ShapeDtypeStruct((h, 1), F32)],
        compiler_params=_cparams(), name=name,
    )(dc_q, dc_k, f_t, b_f)


ATTN_FWD = (1024, 512)
ATTN_BWD = (512, 512)


def _attn_tiles(t, tiles):
    return _tile(t, tiles[0]), _tile(t, tiles[1])


def _attn_fwd(proj, c_t, d, name, job=None):
    t = proj.shape[0]
    h = d // LANE
    bq, bk = _attn_tiles(t, ATTN_FWD)
    nq, nk, rr = t // bq, t // bk, bq // bk
    qc, kc, vc = COL_Q * h, COL_K * h, COL_VA * h
    qscale = LANE ** -0.5 * LOG2E

    def body(q_ref, k_ref, v_ref, cc_ref, cr_ref, o_ref, lse_ref, kb, vt, ckb, acc):
        i = pl.program_id(1)

        @pl.when(i == 0)
        def _():
            kb[...] = k_ref[...].astype(BF16)
            ckb[...] = jnp.broadcast_to(cc_ref[0] * LOG2E, (t, bq))
            for jn in range(nk):
                vt[jn] = v_ref[jn * bk:(jn + 1) * bk, :].astype(F32).T.astype(BF16)

        q = (q_ref[...].astype(F32) * qscale).astype(BF16)
        cq = cr_ref[0, 0] * LOG2E
        acc[...] = jnp.zeros((LANE, bq), F32)

        def block(j, diag, m_old, l_old):
            off = 0 if diag is None else diag * bk
            w = bq - off
            rows = pl.ds(pl.multiple_of(j * bk, bk), bk)
            s = _nt_dot(kb[rows, :], q[off:, :]) - ckb[rows, off:]
            if diag is not None:
                kk = lax.broadcasted_iota(jnp.int32, (bk, w), 0)
                qq = lax.broadcasted_iota(jnp.int32, (bk, w), 1)
                s = jnp.where(qq >= kk, s, NEG)
            cqs, m_part, l_part = cq[:, off:], m_old[:, off:], l_old[:, off:]
            m_new = jnp.maximum(m_part, jnp.max(s, axis=0, keepdims=True) + cqs)
            p = jnp.exp2(s + (cqs - m_new))
            alpha = jnp.exp2(m_part - m_new)
            l_new = alpha * l_part + jnp.sum(p, axis=0, keepdims=True)
            acc[:, off:] = alpha * acc[:, off:] + jnp.dot(vt[j], p.astype(BF16), preferred_element_type=F32)
            if off:
                m_new = jnp.concatenate([m_old[:, :off], m_new], axis=1)
                l_new = jnp.concatenate([l_old[:, :off], l_new], axis=1)
            return m_new, l_new

        m, l = lax.fori_loop(0, i * rr, lambda j, c: block(j, None, *c),
                             (jnp.full((1, bq), NEG, F32), jnp.zeros((1, bq), F32)))
        for jj in range(rr):
            m, l = block(i * rr + jj, jj, m, l)
        o_ref[...] = (acc[...] / l).T
        lse_ref[0, 0] = m + jnp.log2(l)

    rowq = pl.BlockSpec((1, 1, 1, bq), lambda hh, i: (hh, i, 0, 0))
    outs, moved = _call(
        body, grid=(h, nq),
        in_specs=[
            pl.BlockSpec((bq, LANE), lambda hh, i: (i, qc + hh)),
            pl.BlockSpec((t, LANE), lambda hh, i: (0, kc + hh)),
            pl.BlockSpec((t, LANE), lambda hh, i: (0, vc + hh)),
            pl.BlockSpec((1, t, 1), lambda hh, i: (hh, 0, 0)),
            rowq,
        ],
        out_specs=[pl.BlockSpec((bq, LANE), lambda hh, i: (i, hh)), rowq],
        out_shape=[jax.ShapeDtypeStruct((t, d), F32), jax.ShapeDtypeStruct((h, nq, 1, bq), F32)],
        scratch_shapes=[pltpu.VMEM((t, LANE), BF16), pltpu.VMEM((nk, LANE, bk), BF16), pltpu.VMEM((t, bq), F32),
                        pltpu.VMEM((LANE, bq), F32)],
        dims=("arbitrary", "arbitrary"), name=name,
        args=[proj, proj, proj, c_t.reshape(h, t, 1), c_t.reshape(h, nq, 1, bq)], job=job)
    outs = [outs[0], outs[1].reshape(h, t)]
    return outs if job is None else (outs, moved)


def _attn_bwd(proj, do, o, lse, c_t, d, name, job=None):
    t = proj.shape[0]
    h = d // LANE
    bq, bk = _attn_tiles(t, ATTN_BWD)
    nq, nk, rr = t // bq, t // bk, bq // bk
    qc, kc, vc = COL_Q * h, COL_K * h, COL_VA * h
    scale = LANE ** -0.5

    def body(q_ref, k_ref, v_ref, do_ref, o_ref, lse_ref, cc_ref, cr_ref, dq_ref, dk_ref, dv_ref, dcq_ref, dck_ref,
             kb, kt, vb, ckb, dk_acc, dv_acc, dck_acc, dqt_acc):
        i = pl.program_id(1)

        @pl.when(i == 0)
        def _():
            kb[...] = k_ref[...].astype(BF16)
            vb[...] = v_ref[...].astype(BF16)
            ckb[...] = jnp.broadcast_to(cc_ref[0] * LOG2E, (t, bq))
            for jn in range(nk):
                kt[jn] = k_ref[jn * bk:(jn + 1) * bk, :].astype(F32).T.astype(BF16)
            dk_acc[...] = jnp.zeros((t, LANE), F32)
            dv_acc[...] = jnp.zeros((t, LANE), F32)
            dck_acc[...] = jnp.zeros((t, LANE), F32)

        q = (q_ref[...].astype(F32) * (scale * LOG2E)).astype(BF16)
        dof = do_ref[...]
        dob = dof.astype(BF16)
        delta = jnp.sum((dof * o_ref[...]).T, axis=0, keepdims=True)
        rowb = cr_ref[0, 0] * LOG2E - lse_ref[0, 0]
        dqt_acc[...] = jnp.zeros((LANE, bq), F32)

        def block(j, diag, dcq):
            rows = pl.ds(pl.multiple_of(j * bk, bk), bk)
            p = jnp.exp2(_nt_dot(kb[rows, :], q) - ckb[rows, :] + rowb)
            if diag is not None:
                kk = lax.broadcasted_iota(jnp.int32, (bk, bq), 0)
                qq = lax.broadcasted_iota(jnp.int32, (bk, bq), 1)
                p = jnp.where(qq >= kk + diag * bk, p, 0.0)
            dv_acc[rows, :] += jnp.dot(p.astype(BF16), dob, preferred_element_type=F32)
            ds = p * (_nt_dot(vb[rows, :], dob) - delta)
            dsb = ds.astype(BF16)
            dk_acc[rows, :] += jnp.dot(dsb, q, preferred_element_type=F32)
            dqt_acc[...] += jnp.dot(kt[j], dsb, preferred_element_type=F32)
            part = ds[:, 0:LANE]
            for k in range(1, bq // LANE):
                part = part + ds[:, k * LANE:(k + 1) * LANE]
            dck_acc[rows, :] += part
            return dcq + jnp.sum(ds, axis=0, keepdims=True)

        dcq = lax.fori_loop(0, i * rr, lambda j, c: block(j, None, c), jnp.zeros((1, bq), F32))
        for jj in range(rr):
            dcq = block(i * rr + jj, jj, dcq)
        dq_ref[...] = (dqt_acc[...] * scale).T.astype(BF16)
        dcq_ref[0, 0] = dcq

        @pl.when(i == nq - 1)
        def _():
            dk_ref[...] = (dk_acc[...] * LN2).astype(BF16)
            dv_ref[...] = dv_acc[...].astype(BF16)
            dck_ref[0] = jnp.sum(dck_acc[...], axis=-1, keepdims=True)

    rowq = pl.BlockSpec((1, 1, 1, bq), lambda hh, i: (hh, i, 0, 0))
    blk = pl.BlockSpec((bq, LANE), lambda hh, i: (i, hh))
    whole = pl.BlockSpec((t, LANE), lambda hh, i: (0, hh))
    colk = pl.BlockSpec((1, t, 1), lambda hh, i: (hh, 0, 0))
    outs, moved = _call(
        body, grid=(h, nq),
        in_specs=[
            pl.BlockSpec((bq, LANE), lambda hh, i: (i, qc + hh)),
            pl.BlockSpec((t, LANE), lambda hh, i: (0, kc + hh)),
            pl.BlockSpec((t, LANE), lambda hh, i: (0, vc + hh)),
            blk, blk, rowq, colk, rowq,
        ],
        out_specs=[blk, whole, whole, rowq, colk],
        out_shape=[jax.ShapeDtypeStruct((t, d), BF16), jax.ShapeDtypeStruct((t, d), BF16), jax.ShapeDtypeStruct((t, d), BF16),
                   jax.ShapeDtypeStruct((h, nq, 1, bq), F32), jax.ShapeDtypeStruct((h, t, 1), F32)],
        scratch_shapes=[pltpu.VMEM((t, LANE), BF16), pltpu.VMEM((nk, LANE, bk), BF16), pltpu.VMEM((t, LANE), BF16),
                        pltpu.VMEM((t, bq), F32), pltpu.VMEM((t, LANE), F32), pltpu.VMEM((t, LANE), F32),
                        pltpu.VMEM((t, LANE), F32), pltpu.VMEM((LANE, bq), F32)],
        dims=("arbitrary", "arbitrary"), name=name,
        args=[proj, proj, proj, do, o, lse.reshape(h, nq, 1, bq), c_t.reshape(h, t, 1), c_t.reshape(h, nq, 1, bq)], job=job)
    outs = list(outs[:3]) + [outs[3].reshape(h, t), outs[4].reshape(h, t)]
    return outs if job is None else (outs, moved)


def _sgu_forward(u_ref, v_ref, gv_ref, wm_ref, bs_ref, mix_sc, groups):
    gu, dgu = _gelu_and_grad(u_ref[...].astype(F32))
    gvv, dgv = _gelu_and_grad(v_ref[...].astype(F32))
    mu = jnp.mean(gvv, axis=-1, keepdims=True)
    xc = gvv - mu
    r = lax.rsqrt(jnp.mean(xc * xc, axis=-1, keepdims=True) + EPS)
    nhat = xc * r
    vn = (nhat * gv_ref[...]).astype(BF16)
    for g in range(groups):
        sl = slice(g * LANE, (g + 1) * LANE)
        mix_sc[:, sl] = jnp.dot(wm_ref[g], vn[:, sl], preferred_element_type=F32) + bs_ref[g]
    return gu, dgu, dgv, nhat, r, vn, mix_sc[...]


def _mix_fwd(proj, o, wm, bs, g_v, d, name):
    t = proj.shape[0]
    groups = d // LANE

    def body(u_ref, v_ref, ga_ref, gb_ref, o_ref, wm_ref, bs_ref, gv_ref, out_ref, mix_sc):
        gu, _, _, _, _, _, mixed = _sgu_forward(u_ref, v_ref, gv_ref, wm_ref, bs_ref, mix_sc, groups)
        out_ref[...] = (_sigmoid(ga_ref[...].astype(F32)) * (gu * mixed) + _sigmoid(gb_ref[...].astype(F32)) * o_ref[...]).astype(BF16)

    def colblk(k):
        return pl.BlockSpec((LANE, d), lambda i, k=k: (i, k))

    full3 = pl.BlockSpec((groups, LANE, LANE), lambda i: (0, 0, 0))
    return pl.pallas_call(
        body, grid=(t // LANE,),
        in_specs=[colblk(COL_U), colblk(COL_V), colblk(COL_GA), colblk(COL_GB), colblk(0), full3,
                  pl.BlockSpec((groups, LANE, 1), lambda i: (0, 0, 0)), pl.BlockSpec((1, d), lambda i: (0, 0))],
        out_specs=colblk(0),
        out_shape=jax.ShapeDtypeStruct((t, d), BF16),
        scratch_shapes=[pltpu.VMEM((LANE, d), F32)],
        compiler_params=_cparams(("parallel",)), name=name,
    )(proj, proj, proj, proj, o, wm, bs, g_v)


def _mix_bwd(dmerged, proj, o, wm, wm_t, bs, g_v, d, name, job=None):
    t = proj.shape[0]
    groups = d // LANE
    nt = t // LANE

    def body(dm_ref, u_ref, v_ref, ga_ref, gb_ref, o_ref, wm_ref, wmt_ref, bs_ref, gv_ref,
             duv_ref, dg_ref, do_ref, dws_ref, dbs_ref, dgv_ref, mix_sc, dvn_sc, gv_acc):
        i = pl.program_id(0)

        @pl.when(i == 0)
        def _():
            dws_ref[...] = jnp.zeros_like(dws_ref)
            dbs_ref[...] = jnp.zeros_like(dbs_ref)
            gv_acc[...] = jnp.zeros_like(gv_acc)

        gu, dgu, dgv, nhat, r, vn, mixed = _sgu_forward(u_ref, v_ref, gv_ref, wm_ref, bs_ref, mix_sc, groups)
        dm = dm_ref[...]
        sa = _sigmoid(ga_ref[...].astype(F32))
        sb = _sigmoid(gb_ref[...].astype(F32))
        ov = o_ref[...]
        y_a = gu * mixed
        dg_ref[:, 0:d] = (dm * y_a * sa * (1.0 - sa)).astype(BF16)
        dg_ref[:, d:2 * d] = (dm * ov * sb * (1.0 - sb)).astype(BF16)
        do_ref[...] = dm * sb
        dy_a = dm * sa
        duv_ref[:, 0:d] = (dy_a * mixed * dgu).astype(BF16)
        dmixed = dy_a * gu
        dmixed_b = dmixed.astype(BF16)
        for g in range(groups):
            sl = slice(g * LANE, (g + 1) * LANE)
            dvn_sc[:, sl] = jnp.dot(wmt_ref[g], dmixed_b[:, sl], preferred_element_type=F32)
            dws_ref[g] += _nt_dot(dmixed_b[:, sl], vn[:, sl])
            dbs_ref[g] += jnp.sum(dmixed[:, sl], axis=-1, keepdims=True)
        dvn = dvn_sc[...]
        gv_acc[...] += _sum8(dvn * nhat)
        dn = dvn * gv_ref[...]
        dgelu = r * (dn - jnp.mean(dn, axis=-1, keepdims=True) - nhat * jnp.mean(dn * nhat, axis=-1, keepdims=True))
        duv_ref[:, d:2 * d] = (dgelu * dgv).astype(BF16)

        @pl.when(i == nt - 1)
        def _():
            dgv_ref[...] = jnp.sum(gv_acc[...], axis=0, keepdims=True)
            rr = lax.broadcasted_iota(jnp.int32, (LANE, LANE), 0)
            cl = lax.broadcasted_iota(jnp.int32, (LANE, LANE), 1)
            for g in range(groups):
                dws_ref[g] = jnp.where(rr >= cl, dws_ref[g], 0.0)

    def colblk(k):
        return pl.BlockSpec((LANE, d), lambda i, k=k: (i, k))

    full3 = pl.BlockSpec((groups, LANE, LANE), lambda i: (0, 0, 0))
    col3 = pl.BlockSpec((groups, LANE, 1), lambda i: (0, 0, 0))
    vec = pl.BlockSpec((1, d), lambda i: (0, 0))
    two = pl.BlockSpec((LANE, 2 * d), lambda i: (i, 0))
    outs, moved = _call(
        body, grid=(nt,),
        in_specs=[colblk(0), colblk(COL_U), colblk(COL_V), colblk(COL_GA), colblk(COL_GB), colblk(0), full3, full3, col3, vec],
        out_specs=[two, two, colblk(0), full3, col3, vec],
        out_shape=[jax.ShapeDtypeStruct((t, 2 * d), BF16), jax.ShapeDtypeStruct((t, 2 * d), BF16), jax.ShapeDtypeStruct((t, d), F32),
                   jax.ShapeDtypeStruct((groups, LANE, LANE), F32), jax.ShapeDtypeStruct((groups, LANE, 1), F32),
                   jax.ShapeDtypeStruct((1, d), F32)],
        scratch_shapes=[pltpu.VMEM((LANE, d), F32), pltpu.VMEM((LANE, d), F32), pltpu.VMEM((SUBLANE, d), F32)],
        dims=("arbitrary",), name=name, args=[dmerged, proj, proj, proj, proj, o, wm, wm_t, bs, g_v], job=job)
    return outs if job is None else (outs, moved)


def _adam_math(w, g, m, v):
    nm = ADAM_B1 * m + (1.0 - ADAM_B1) * g
    nv = ADAM_B2 * v + (1.0 - ADAM_B2) * (g * g)
    delta = -ADAM_LR * ((nm * ADAM_C1) / (jnp.sqrt(nv * ADAM_C2) + ADAM_EPS) + ADAM_WD * w)
    return delta, nm, nv


def _adamw(w, g, m, v, name):
    r, c = w.shape
    cap = max(SUBLANE, (2 * 1024 * 1024) // (4 * c) // SUBLANE * SUBLANE)
    tr = _rows(r, cap)

    def body(w_ref, g_ref, m_ref, v_ref, d_ref, nm_ref, nv_ref):
        d_ref[...], nm_ref[...], nv_ref[...] = _adam_math(w_ref[...], g_ref[...], m_ref[...], v_ref[...])

    blk = pl.BlockSpec((tr, c), lambda i: (i, 0))
    return pl.pallas_call(
        body, grid=(r // tr,), in_specs=[blk] * 4, out_specs=[blk] * 3,
        out_shape=[jax.ShapeDtypeStruct((r, c), F32)] * 3, compiler_params=_cparams(("parallel",)), name=name,
    )(w, g, m, v)


def _adamw_layers(w, g0, g1, m, v, name):
    _, r, c = w.shape
    cap = max(SUBLANE, (1024 * 1024) // (4 * c) // SUBLANE * SUBLANE)
    tr = _rows(r, cap)

    def body(w_ref, g0_ref, g1_ref, m_ref, v_ref, g_ref, d_ref, nm_ref, nv_ref):
        gg = jnp.where(pl.program_id(0) == 0, g0_ref[...], g1_ref[...])
        g_ref[0] = gg
        d_ref[0], nm_ref[0], nv_ref[0] = _adam_math(w_ref[0], gg, m_ref[0], v_ref[0])

    lay = pl.BlockSpec((1, tr, c), lambda l, i: (l, i, 0))

    def gspec(l0):
        return pl.BlockSpec((tr, c), lambda l, i: (jnp.where(l == l0, i, 0), 0))

    return pl.pallas_call(
        body, grid=(2, r // tr), in_specs=[lay, gspec(0), gspec(1), lay, lay], out_specs=[lay] * 4,
        out_shape=[jax.ShapeDtypeStruct((2, r, c), F32)] * 4, compiler_params=_cparams(("arbitrary", "arbitrary")), name=name,
    )(w, g0, g1, m, v)


def _adamw_interleaved(w, g0, g1, m, v, name):
    r, _, c = w.shape
    tr = 128

    def body(w_ref, g0_ref, g1_ref, m_ref, v_ref, g_ref, d_ref, nm_ref, nv_ref):
        for l, gl in enumerate((g0_ref, g1_ref)):
            gg = gl[...]
            g_ref[:, l, :] = gg
            d_ref[:, l, :], nm_ref[:, l, :], nv_ref[:, l, :] = _adam_math(w_ref[:, l, :], gg, m_ref[:, l, :], v_ref[:, l, :])

    lay = pl.BlockSpec((tr, 2, c), lambda i: (i, 0, 0))
    flat = pl.BlockSpec((tr, c), lambda i: (i, 0))
    return pl.pallas_call(
        body, grid=(pl.cdiv(r, tr),), in_specs=[lay, flat, flat, lay, lay], out_specs=[lay] * 4,
        out_shape=[jax.ShapeDtypeStruct((r, 2, c), F32)] * 4, compiler_params=_cparams(("parallel",)), name=name,
    )(w, g0, g1, m, v)


def _add_half(p4, recv, c_idx, name):
    _, r, c = p4.shape
    hw = c // 2
    tr = 256 if r % 256 == 0 else r

    def body(c_ref, a_ref, b_ref, o_ref):
        o_ref[...] = (a_ref[...].astype(F32) + b_ref[...].astype(F32)).astype(BF16)

    return pl.pallas_call(
        body,
        grid_spec=pltpu.PrefetchScalarGridSpec(
            num_scalar_prefetch=1, grid=(N_CHIPS, pl.cdiv(r, tr)),
            in_specs=[pl.BlockSpec((1, tr, hw), lambda s, i, cr: (s, i, cr[0])), pl.BlockSpec((1, tr, hw), lambda s, i, cr: (s, i, 0))],
            out_specs=pl.BlockSpec((1, tr, hw), lambda s, i, cr: (s, i, 0)),
        ),
        out_shape=jax.ShapeDtypeStruct((N_CHIPS, r, hw), BF16), compiler_params=_cparams(("parallel", "parallel")), name=name,
    )(c_idx, p4, recv)


def _sum_slots(x, own, sel, name, out_cols=None):
    s, r, c = x.shape
    tr = 128 if r % 128 == 0 else r

    def body(sel_ref, x_ref, own_ref, o_ref):
        mine = own_ref[0].astype(F32)
        acc = jnp.zeros((tr, c), F32)
        for k in range(s):
            acc = acc + jnp.where(sel_ref[0] == k, mine, x_ref[k].astype(F32))
        o_ref[...] = acc

    return pl.pallas_call(
        body,
        grid_spec=pltpu.PrefetchScalarGridSpec(
            num_scalar_prefetch=1, grid=(pl.cdiv(r, tr),),
            in_specs=[pl.BlockSpec((s, tr, c), lambda i, sr: (0, i, 0)), pl.BlockSpec((1, tr, c), lambda i, sr: (sr[1], i, 0))],
            out_specs=pl.BlockSpec((tr, c), lambda i, sr: (i, sr[2])),
        ),
        out_shape=jax.ShapeDtypeStruct((r, out_cols or c), F32), compiler_params=_cparams(("parallel",)), name=name,
    )(sel, x, own)


def _half_cols(width, hc):
    hw = width // 2
    assert hw % LANE == 0
    return pl.ds(pl.multiple_of(hc * hw, LANE), hw)


def _remote(src, dst, ssem, rsem, k, to):
    return pltpu.make_async_remote_copy(src_ref=src, dst_ref=dst, send_sem=ssem.at[k], recv_sem=rsem.at[k], device_id=to,
                                        device_id_type=MESH)


def _gather_job(bufs, mid_at=0.5):
    def part(o, a, slot, hc):
        return o[a].at[slot, :, _half_cols(bufs[a].shape[2], hc)]

    def first(ins, o, fresh, ssem, rsem):
        x, y, c, chips = _place()
        for a in range(len(bufs)):
            mine = part(o, a, 2 * x + y, c)
            for j, chip in enumerate(chips):
                _remote(mine, mine, ssem, rsem, 6 * a + j, (chip[0], chip[1], c)).start()

    def mid(ins, o, fresh, ssem, rsem):
        x, y, c, chips = _place()
        for a in range(len(bufs)):
            for j, chip in enumerate(chips):
                got = part(o, a, 2 * chip[0] + chip[1], c)
                _remote(got, got, ssem, rsem, 6 * a + j, (x, y, c)).wait_recv()
                _remote(got, got, ssem, rsem, 6 * a + 3 + j, (x, y, 1 - c)).start()

    def last(ins, o, fresh, ssem, rsem):
        x, y, c, chips = _place()
        for a in range(len(bufs)):
            for j, chip in enumerate(chips):
                got = part(o, a, 2 * chip[0] + chip[1], 1 - c)
                _remote(got, got, ssem, rsem, 6 * a + 3 + j, (x, y, c)).wait_recv()
        for a in range(len(bufs)):
            mine = part(o, a, 2 * x + y, c)
            for j, chip in enumerate(chips):
                _remote(mine, mine, ssem, rsem, 6 * a + j, (x, y, c)).wait_send()
                passed = part(o, a, 2 * chip[0] + chip[1], c)
                _remote(passed, passed, ssem, rsem, 6 * a + 3 + j, (x, y, c)).wait_send()

    return _Job([], bufs, [], 6 * len(bufs), first, mid, last, mid_at)


def _swap_job(p4s):
    def pairs(ins, fresh, c):
        return [(a, s, ins[a].at[s, :, _half_cols(p4s[a].shape[2], 1 - c)], fresh[a].at[s])
                for a in range(len(p4s)) for s in range(N_CHIPS)]

    def first(ins, inout, fresh, ssem, rsem):
        x, y, c, _ = _place()
        for a, s, src, dst in pairs(ins, fresh, c):
            _remote(src, dst, ssem, rsem, N_CHIPS * a + s, (x, y, 1 - c)).start()

    def last(ins, inout, fresh, ssem, rsem):
        x, y, c, _ = _place()
        for a, s, src, dst in pairs(ins, fresh, c):
            _remote(src, dst, ssem, rsem, N_CHIPS * a + s, (x, y, 1 - c)).wait()

    fresh = [jax.ShapeDtypeStruct(p.shape[:2] + (p.shape[2] // 2,), p.dtype) for p in p4s]
    return _Job(p4s, [], fresh, N_CHIPS * len(p4s), first, None, last)


def _scatter_job(parts):
    def first(ins, inout, fresh, ssem, rsem):
        x, y, c, chips = _place()
        for a in range(len(parts)):
            for j, chip in enumerate(chips):
                _remote(ins[a].at[2 * chip[0] + chip[1]], fresh[a].at[2 * x + y], ssem, rsem, 3 * a + j, (chip[0], chip[1], c)).start()

    def last(ins, inout, fresh, ssem, rsem):
        x, y, c, chips = _place()
        for a in range(len(parts)):
            for j, chip in enumerate(chips):
                slot = 2 * chip[0] + chip[1]
                _remote(ins[a].at[slot], fresh[a].at[slot], ssem, rsem, 3 * a + j, (x, y, c)).wait()

    return _Job(parts, [], [jax.ShapeDtypeStruct(p.shape, p.dtype) for p in parts], 3 * len(parts), first, None, last)


def _share_job(gs):
    def halves(o, a, c):
        width = gs[a].shape[1]
        return o[a].at[:, _half_cols(width, c)], o[a].at[:, _half_cols(width, 1 - c)]

    def first(ins, o, fresh, ssem, rsem):
        x, y, c, _ = _place()
        for a in range(len(gs)):
            mine, _ = halves(o, a, c)
            _remote(mine, mine, ssem, rsem, a, (x, y, 1 - c)).start()

    def last(ins, o, fresh, ssem, rsem):
        x, y, c, _ = _place()
        for a in range(len(gs)):
            mine, theirs = halves(o, a, c)
            _remote(mine, theirs, ssem, rsem, a, (x, y, 1 - c)).wait()

    return _Job([], gs, [], len(gs), first, None, last)


def _gather_all_job(buf):
    def peers():
        x, y, c, _ = _place()
        flips = [(fx, fy, fc) for fx in (0, 1) for fy in (0, 1) for fc in (0, 1)][1:]
        return (x, y, c), [((1 - x) if fx else x, (1 - y) if fy else y, (1 - c) if fc else c) for fx, fy, fc in flips]

    def first(ins, inout, fresh, ssem, rsem):
        (x, y, c), others = peers()
        for k, peer in enumerate(others):
            _remote(ins[0], fresh[0].at[4 * x + 2 * y + c], ssem, rsem, k, peer).start()

    def last(ins, inout, fresh, ssem, rsem):
        me, others = peers()
        for k, peer in enumerate(others):
            _remote(ins[0], fresh[0].at[4 * peer[0] + 2 * peer[1] + peer[2]], ssem, rsem, k, me).wait()

    return _Job([buf], [], [jax.ShapeDtypeStruct((N_DEV,) + buf.shape, buf.dtype)], N_DEV - 1, first, None, last)


class _SemView:
    def __init__(self, sems, off):
        self.sems, self.off = sems, off

    @property
    def at(self):
        return self

    def __getitem__(self, k):
        return self.sems.at[k + self.off]


def _join(jobs):
    spans, pos = [], [0, 0, 0, 0]
    for j in jobs:
        nxt = [pos[0] + len(j.ins), pos[1] + len(j.inout), pos[2] + len(j.fresh), pos[3] + j.nsem]
        spans.append((pos, nxt))
        pos = nxt

    def hook(which):
        fns = [getattr(j, which) for j in jobs]
        if all(f is None for f in fns):
            return None

        def run(ins, inout, fresh, ssem, rsem):
            for fn, (lo, hi) in zip(fns, spans):
                if fn is not None:
                    fn(ins[lo[0]:hi[0]], inout[lo[1]:hi[1]], fresh[lo[2]:hi[2]], _SemView(ssem, lo[3]), _SemView(rsem, lo[3]))

        return run

    mids = [j.mid_at for j in jobs if j.mid is not None]
    joined = _Job([a for j in jobs for a in j.ins], [a for j in jobs for a in j.inout], [a for j in jobs for a in j.fresh],
                  pos[3], hook("first"), hook("mid"), hook("last"), max(mids) if mids else 0.5)
    n_io = pos[1]

    def split(moved):
        return [list(moved[lo[1]:hi[1]]) + list(moved[n_io + lo[2]:n_io + hi[2]]) for lo, hi in spans]

    return joined, split


def _carrying(stages, call):
    stages = [s for s in stages if s is not None]
    if not stages:
        return call(None)
    job, split = _join([s[0] for s in stages])
    out, moved = call(job)
    for (_, done), part in zip(stages, split(moved)):
        done(part)
    return out


def _layer_forward(x, h, w_in_t, rest, sm, g_next, d, stages=None):
    stages = stages or {}
    proj = _carrying([stages.get("proj")], lambda job: _matmul(h, w_in_t, "nt", BF16, "proj_fwd", n=7 * d, tn_cap=1792, job=job))
    f_t = _matmul(w_in_t[7 * d:], h, "nt", F32, "forget_fwd", tn_cap=1024)
    c_t = _fox_prep(f_t, sm["b_f"], "fox_prep")
    o, lse = _carrying([stages.get("attn")], lambda job: _attn_fwd(proj, c_t, d, "attn_fwd", job=job))
    wts = rest()
    merged = _mix_fwd(proj, o, sm["wm"], sm["bs"], sm["g_v"], d, "mix_fwd")
    z, x1, h2 = _matmul(merged, wts["w_out"], "nn", F32, "out_fwd", norms=(x, sm["g_post"], sm["g_fpre"]))
    a = _carrying([stages.get("gate")], lambda job: _matmul(h2, wts["w_g_t"], "nt", BF16, "gate_fwd", tn_cap=1408, job=job))
    b = _carrying([stages.get("up")], lambda job: _matmul(h2, wts["w_u_t"], "nt", BF16, "up_fwd", tn_cap=1408, job=job))
    z2, x_out, h_out, mm = _carrying([stages.get("down")], lambda job: _matmul(
        b, wts["w_d"], "nn", F32, "down_fwd", norms=(x1, sm["g_fpost"], g_next), silu_of=a, job=job))
    return dict(x=x, h=h, proj=proj, f_t=f_t, c_t=c_t, o=o, lse=lse, merged=merged, z=z, x1=x1,
                h2=h2, a=a, b=b, mm=mm, z2=z2, x_out=x_out, h_out=h_out)


class _GradExchange:
    def __init__(self, pay, keys, c_idx, chip):
        self.keys = list(keys)
        self.p4 = [pay[k].reshape(N_CHIPS, pay[k].shape[1] // N_CHIPS, pay[k].shape[2]) for k in self.keys]
        self.c_idx = c_idx
        self.sel = jnp.stack([chip, chip, c_idx[0]]).astype(jnp.int32)
        self.done = 0

    def _after_swap(self, landed):
        self.parts = [_add_half(p, r, self.c_idx, "add_sibling") for p, r in zip(self.p4, landed)]
        self.done = 1

    def _after_scatter(self, landed):
        self.g = [_sum_slots(got, sent, self.sel, "sum_chips", out_cols=p.shape[2])
                  for got, sent, p in zip(landed, self.parts, self.p4)]
        self.done = 2

    def _after_share(self, moved):
        self.g = list(moved)
        self.done = 3

    def stage(self):
        if self.done == 0:
            return _swap_job(self.p4), self._after_swap
        if self.done == 1:
            return _scatter_job(self.parts), self._after_scatter
        if self.done == 2:
            return _share_job(self.g), self._after_share
        return None

    def run(self):
        for name in ("swap_grads", "scatter_grads", "share_grads")[self.done:]:
            job, done = self.stage()
            done(_run_job(job, name))

    def grads(self):
        return dict(zip(self.keys, self.g))


EARLY_KEYS = ("w_d", "w_g", "w_u", "w_out")


def _layer_backward(dz2, dx2, sv, wts, sm, d, c_idx, chip, carried=(), split_own=False, small_stage=None):
    t = dx2.shape[0]
    heads = d // LANE
    ff = wts["w_d"].shape[0]
    in_w = 7 * d + heads
    g, pay = {}, {}
    carried = list(carried)

    def payload(key, a, b, rows, row0, name, extra=()):
        def call(job):
            return _matmul(a, b, "tn", BF16, name, slab=((1, rows, d), 0, row0), into=pay.get(key), job=job, tm_cap=1408,
                           tn_cap=1024, tk_cap=1024)
        pay[key] = _carrying(list(extra), call)

    def nxt(*exchanges):
        return [ex.stage() for ex in exchanges]

    dm = _carrying(nxt(*carried), lambda job: _matmul(dz2, wts["w_d"], "nt", BF16, "down_bwd_x", tn_cap=1408, tk_cap=1024, job=job))
    payload("w_d", sv["mm"], dz2, ff, 0, "down_bwd_w")
    da, db = _swiglu_bwd(sv["a"], sv["b"], dm, "swiglu_bwd")
    dh2 = _matmul_pieces([(da, wts["w_g_t"], 0), (db, wts["w_u_t"], 0)], None, "gu_bwd_x", tk=_tile(ff, 1408))
    payload("w_g", da, sv["h2"], ff, 0, "gate_bwd_w")
    payload("w_u", db, sv["h2"], ff, 0, "up_bwd_w")
    dx1, dz, g["g_fpre"], g["g_post"] = _norm_bwd(dx2, (dh2, sv["x1"], sm["g_fpre"]), (sv["z"], sm["g_post"]), "norm_bwd_mid")
    dmerged = _matmul(dz, wts["w_out"], "nt", F32, "out_bwd_x", tk_cap=1024)
    payload("w_out", sv["merged"], dz, d, 0, "out_bwd_w")
    early = [_GradExchange(pay, EARLY_KEYS, c_idx, chip)] if split_own else []
    d_uv, d_g, do, g["w_s"], g["b_s"], g["g_v"] = _carrying(nxt(*early), lambda job: _mix_bwd(
        dmerged, sv["proj"], sv["o"], sm["wm"], sm["wm_t"], sm["bs"], sm["g_v"], d, "mix_bwd", job=job))
    extra = [small_stage(g)] if small_stage is not None else []
    attn_args = (sv["proj"], do, sv["o"], sv["lse"], sv["c_t"], d)
    dq, dk, dv, dc_q, dc_k = _carrying(nxt(*carried) + extra, lambda job: _attn_bwd(*attn_args, "attn_bwd", job=job))
    df_t, g["b_f"] = _fox_bwd(dc_q, dc_k, sv["f_t"], sm["b_f"], "fox_bwd")
    df_b = df_t.astype(BF16)
    pieces = [(d_uv, COL_U), (dq, COL_Q), (dk, COL_K), (dv, COL_VA), (d_g, COL_GA)]
    pay["w_in"] = _carrying(nxt(*carried, *early), lambda job: _matmul_rows([p for p, _ in pieces], sv["h"], in_w,
                                                                           "proj_bwd_w", job=job))
    w_f_rows = _carrying(nxt(*early), lambda job: _matmul(df_b, sv["h"], "nn", BF16, "forget_bwd_w", tk_cap=1024, job=job))
    pay["w_in"] = lax.dynamic_update_slice(pay["w_in"], w_f_rows[None], (0, 7 * d, 0))
    late = _GradExchange(pay, [k for k in ("w_in",) + EARLY_KEYS if not (split_own and k in EARLY_KEYS)], c_idx, chip)
    mine = [late] if split_own else []
    dh_f = _carrying(nxt(*mine), lambda job: _matmul(df_b, wts["w_in_t"][7 * d:], "tn", F32, "forget_bwd_x", job=job))
    ops = [(p, wts["w_in_t"], col * d) for p, col in pieces]
    dh = _carrying(nxt(*mine), lambda job: _matmul_pieces(ops, dh_f, "proj_bwd_x", job=job, tk=_tile(d, 1024)))
    return dh, dx1, g, early + [late]


def _small_pack(parts):
    flat = jnp.concatenate([p.reshape(-1) for p in parts])
    n = flat.shape[0]
    pad = (-n) % (LANE * LANE)
    return jnp.pad(flat, (0, pad)).reshape(-1, LANE)


def kernel(x, mix_pre_g, w_in, b_forget, sgu_norm_g, w_spatial, b_spatial, w_out, mix_post_g, ffn_pre_g, w_gate, w_up, w_down, ffn_post_g, loss_target, m_mix_pre_g, m_w_in, m_b_forget, m_sgu_norm_g, m_w_spatial, m_b_spatial, m_w_out, m_mix_post_g, m_ffn_pre_g, m_w_gate, m_w_up, m_w_down, m_ffn_post_g, v_mix_pre_g, v_w_in, v_b_forget, v_sgu_norm_g, v_w_spatial, v_b_spatial, v_w_out, v_mix_post_g, v_ffn_pre_g, v_w_gate, v_w_up, v_w_down, v_ffn_post_g):
    depth, d = mix_pre_g.shape
    assert depth == 2, "the AdamW kernels and the exchange schedule are written for two blocks"
    heads = d // LANE
    t = x.shape[1]
    ff = w_down.shape[1] * N_CHIPS
    in_w = w_in.shape[2] * N_CHIPS
    assert in_w == 7 * d + heads
    xs = x.reshape(t, d)
    target = loss_target.reshape(t, d)
    c_idx = lax.axis_index("c").astype(jnp.int32).reshape(1)
    chip = 2 * lax.axis_index("x") + lax.axis_index("y")
    dev = 2 * chip + lax.axis_index("c")

    def in_view(w):
        return jnp.transpose(w, (2, 0, 1))

    def gu_view(w):
        return jnp.transpose(w, (0, 2, 1))

    own = [jnp.transpose(in_view(w_in).astype(BF16), (1, 0, 2)), w_out.astype(BF16), gu_view(w_gate).astype(BF16),
           gu_view(w_up).astype(BF16), w_down.astype(BF16)]
    bufs = [[lax.dynamic_update_slice(lax.empty((N_CHIPS,) + o.shape[1:], BF16), o[l][None], (chip, 0, 0)) for o in own]
            for l in range(depth)]
    first_in = _run_job(_gather_job([bufs[0][0]]), "gather_first")[0]

    def weights(g_in, g_out, g_g, g_u, g_d):
        return dict(w_in_t=g_in.reshape(in_w, d), w_out=g_out.reshape(d, d), w_g_t=g_g.reshape(ff, d),
                    w_u_t=g_u.reshape(ff, d), w_d=g_d.reshape(ff, d))

    tril = jnp.tril(jnp.ones((LANE, LANE), bool))
    smalls = []
    for l in range(depth):
        wm = jnp.where(tril[None], w_spatial[l], 0.0).astype(BF16)
        smalls.append(dict(
            b_f=b_forget[l].reshape(heads, 1), wm=wm, wm_t=jnp.swapaxes(wm, 1, 2), bs=b_spatial[l].reshape(heads, LANE, 1),
            g_v=sgu_norm_g[l].reshape(1, d), g_pre=mix_pre_g[l].reshape(1, d), g_post=mix_post_g[l].reshape(1, d),
            g_fpre=ffn_pre_g[l].reshape(1, d), g_fpost=ffn_post_g[l].reshape(1, d)))

    wts, later = [], {}

    def keep(key):
        def done(moved):
            later[key] = list(moved)
        return done

    def rest_first():
        wts.append(weights(first_in, *later["rest0"], later["attn"][0]))
        return wts[0]

    stages = dict(proj=(_gather_job(bufs[0][1:4], mid_at=1.0), keep("rest0")),
                  attn=(_gather_job(bufs[0][4:5] + bufs[1][0:2], mid_at=0.7), keep("attn")),
                  gate=(_gather_job(bufs[1][2:3], mid_at=1.0), keep("g1")), up=(_gather_job(bufs[1][3:4], mid_at=1.0), keep("u1")),
                  down=(_gather_job(bufs[1][4:5], mid_at=1.0), keep("d1")))
    h = _norm_fwd(xs, None, None, smalls[0]["g_pre"], "norm_first")
    g_after = [smalls[min(l + 1, depth - 1)]["g_pre"] for l in range(depth)]
    saved = [_layer_forward(xs, h, first_in.reshape(in_w, d), rest_first, smalls[0], g_after[0], d, stages)]
    wts.append(weights(*later["attn"][1:3], later["g1"][0], later["u1"][0], later["d1"][0]))
    for l in range(1, depth):
        saved.append(_layer_forward(saved[l - 1]["x_out"], saved[l - 1]["h_out"], wts[l]["w_in_t"], lambda l=l: wts[l], smalls[l],
                                    g_after[l], d))
    dy, loss_part = _loss_grad(saved[-1]["x_out"], target, "loss")
    loss = lax.psum(jnp.sum(loss_part), ("x", "y", "c"))

    small_shapes = dict(g_pre=(d,), b_f=(heads,), g_v=(d,), w_s=w_spatial.shape[1:], b_s=b_spatial.shape[1:], g_post=(d,),
                        g_fpre=(d,), g_fpost=(d,))
    late_entries = [(0, "g_pre"), (0, "b_f")]
    early_entries = [(l, n) for l in reversed(range(depth)) for n in small_shapes if (l, n) not in late_entries]
    dev_sel = jnp.stack([dev, jnp.zeros_like(dev), jnp.zeros_like(dev)]).astype(jnp.int32)
    small_sum = {}

    def small_exchange(entries, values):
        packed = _small_pack([values[e].reshape(-1) for e in entries])

        def done(moved):
            total = _sum_slots(moved[0], packed[None], dev_sel, "sum_small").reshape(-1)
            off = 0
            for e in entries:
                n = math.prod(small_shapes[e[1]])
                small_sum[e] = total[off:off + n].reshape(small_shapes[e[1]])
                off += n

        return _gather_all_job(packed), done

    grads = [None] * depth
    exchanges = [None] * depth
    dx2 = dy
    dz2, g_fpost = _norm_bwd(dx2, None, (saved[depth - 1]["z2"], smalls[depth - 1]["g_fpost"]), "norm_bwd_top")
    for l in reversed(range(depth)):
        last = l == 0

        def small_stage(g, l=l, g_fpost=g_fpost):
            known = {(k, n): grads[k][n] for k in range(l + 1, depth) for n in small_shapes}
            known.update({(l, n): g[n] for n in g})
            known[(l, "g_fpost")] = g_fpost
            return small_exchange(early_entries, known)

        carried = [ex for k in range(l + 1, depth) for ex in exchanges[k]]
        dh, dx1, g, exchanges[l] = _layer_backward(dz2, dx2, saved[l], wts[l], smalls[l], d, c_idx, chip, carried=carried,
                                                    split_own=last, small_stage=small_stage if last else None)
        g["g_fpost"] = g_fpost
        if l > 0:
            dx2, dz2, g["g_pre"], g_fpost = _norm_bwd(dx1, (dh, saved[l]["x"], smalls[l]["g_pre"]),
                                                       (saved[l - 1]["z2"], smalls[l - 1]["g_fpost"]), "norm_bwd_between")
        else:
            grad_x, g["g_pre"] = _norm_bwd(dx1, (dh, saved[l]["x"], smalls[l]["g_pre"]), None, "norm_bwd_bottom")
        grads[l] = g
    job, done = small_exchange(late_entries, {(0, n): grads[0][n] for n in ("g_pre", "b_f")})
    done(_run_job(job, "gather_small"))
    big = [{} for _ in range(depth)]
    for l in range(depth):
        for ex in exchanges[l]:
            ex.run()
            big[l].update(ex.grads())
    small_grads = {n: jnp.stack([small_sum[(l, n)] for l in range(depth)]) for n in small_shapes}

    def adam_small(w, g, m, v):
        shp = w.shape
        if w.ndim >= 3 and shp[-1] >= LANE:
            two = (math.prod(shp[:-1]), shp[-1])
        else:
            two = (1, math.prod(shp)) if math.prod(shp) < LANE else (math.prod(shp) // LANE, LANE)
        outs = _adamw(w.reshape(two), g.reshape(two), m.reshape(two), v.reshape(two), "adamw")
        return [g] + [o.reshape(shp) for o in outs]

    def adam_in(w, m, v):
        outs = _adamw_interleaved(in_view(w), big[0]["w_in"], big[1]["w_in"], in_view(m), in_view(v), "adamw_in")
        return [jnp.transpose(o, (1, 2, 0)) for o in outs]

    def adam_gu(k, w, m, v):
        outs = _adamw_layers(gu_view(w), big[0][k], big[1][k], gu_view(m), gu_view(v), "adamw_layers")
        return [jnp.transpose(o, (0, 2, 1)) for o in outs]

    def adam_rows(k, w, m, v):
        return _adamw_layers(w, big[0][k], big[1][k], m, v, "adamw_layers")

    results = [
        adam_small(mix_pre_g, small_grads["g_pre"], m_mix_pre_g, v_mix_pre_g),
        adam_in(w_in, m_w_in, v_w_in),
        adam_small(b_forget, small_grads["b_f"], m_b_forget, v_b_forget),
        adam_small(sgu_norm_g, small_grads["g_v"], m_sgu_norm_g, v_sgu_norm_g),
        adam_small(w_spatial, small_grads["w_s"], m_w_spatial, v_w_spatial),
        adam_small(b_spatial, small_grads["b_s"], m_b_spatial, v_b_spatial),
        adam_rows("w_out", w_out, m_w_out, v_w_out),
        adam_small(mix_post_g, small_grads["g_post"], m_mix_post_g, v_mix_post_g),
        adam_small(ffn_pre_g, small_grads["g_fpre"], m_ffn_pre_g, v_ffn_pre_g),
        adam_gu("w_g", w_gate, m_w_gate, v_w_gate),
        adam_gu("w_u", w_up, m_w_up, v_w_up),
        adam_rows("w_d", w_down, m_w_down, v_w_down),
        adam_small(ffn_post_g, small_grads["g_fpost"], m_ffn_post_g, v_ffn_post_g),
    ]
    gs, deltas, new_ms, new_vs = zip(*results)
    return (loss, grad_x.reshape(x.shape), *gs, *deltas, *new_ms, *new_vs)
```

```python
import functools
import math

import jax
import jax.numpy as jnp
from jax import lax
from jax.experimental import pallas as pl
from jax.experimental.pallas import tpu as pltpu

F32 = jnp.float32
BF16 = jnp.bfloat16

EPS = 1e-6
LANE = 128
SUBLANE = 8
N_CHIPS = 4
N_DEV = 8
VMEM_LIMIT = 48 * 1024 * 1024
MESH = pl.DeviceIdType.MESH

ADAM_LR = 0.001
ADAM_B1 = 0.9
ADAM_B2 = 0.999
ADAM_EPS = 1e-08
ADAM_WD = 0.01
ADAM_STEP = 10
ADAM_C1 = 1.0 / (1.0 - ADAM_B1 ** ADAM_STEP)
ADAM_C2 = 1.0 / (1.0 - ADAM_B2 ** ADAM_STEP)

GELU_K = math.sqrt(2.0 / math.pi)
GELU_A = 0.044715
NEG = -1e30
LOG2E = 1.4426950408889634
LN2 = 0.6931471805599453

COL_U, COL_V, COL_Q, COL_K, COL_VA, COL_GA, COL_GB, COL_F = range(8)


def _cparams(sem=None):
    return pltpu.CompilerParams(dimension_semantics=sem, vmem_limit_bytes=VMEM_LIMIT)


def _tile(n, cap):
    best = None
    for t in range(LANE, min(n, cap) + 1, LANE):
        if n % t == 0:
            best = t
    return best if best is not None else n


def _rows(n, cap):
    best = None
    for t in range(SUBLANE, min(n, cap) + 1, SUBLANE):
        if n % t == 0:
            best = t
    return best if best is not None else n


def _gelu_and_grad(x):
    x2 = x * x
    t = jnp.tanh(GELU_K * (x + GELU_A * x2 * x))
    g = 0.5 * x * (1.0 + t)
    dg = 0.5 * (1.0 + t) + 0.5 * x * (1.0 - t * t) * (GELU_K * (1.0 + 3.0 * GELU_A * x2))
    return g, dg


def _sigmoid(x):
    return 1.0 / (1.0 + jnp.exp(-x))


def _sum8(v):
    n, d = v.shape
    return v.reshape(n // SUBLANE, SUBLANE, d).sum(axis=0)


def _nt_dot(a, b):
    return lax.dot_general(a, b, (((1,), (1,)), ((), ())), preferred_element_type=F32)


_HBM = pl.BlockSpec(memory_space=pl.ANY)


def _place():
    x, y, c = lax.axis_index("x"), lax.axis_index("y"), lax.axis_index("c")
    chips = [(1 - x, y), (x, 1 - y), (1 - x, 1 - y)]
    return x, y, c, chips


class _Job:
    def __init__(self, ins, inout, fresh, nsem, first, mid, last, mid_at=0.5):
        self.ins, self.inout, self.fresh, self.nsem = list(ins), list(inout), list(fresh), nsem
        self.first, self.mid, self.last, self.mid_at = first, mid, last, mid_at


def _call(body, *, grid, in_specs, out_specs, out_shape, scratch_shapes, dims, name, args, aliases=None, job=None):
    single = not isinstance(out_shape, (list, tuple))
    out_specs = [out_specs] if single else list(out_specs)
    out_shape = [out_shape] if single else list(out_shape)
    aliases = dict(aliases or {})
    if job is None:
        outs = pl.pallas_call(body, grid=grid, in_specs=in_specs, out_specs=out_specs, out_shape=out_shape,
                              scratch_shapes=scratch_shapes, input_output_aliases=aliases, compiler_params=_cparams(dims),
                              name=name)(*args)
        return (outs[0] if single else outs), []
    n_in, n_out, n_scr = len(args), len(out_shape), len(scratch_shapes)
    n_ji, n_jio, n_jf = len(job.ins), len(job.inout), len(job.fresh)
    total = math.prod(grid)

    def wrapped(*refs):
        host_in = refs[:n_in]
        pos = n_in
        j_ins = refs[pos:pos + n_ji]
        pos += n_ji + n_jio
        host_out = refs[pos:pos + n_out]
        pos += n_out
        j_inout = refs[pos:pos + n_jio]
        pos += n_jio
        j_fresh = refs[pos:pos + n_jf]
        pos += n_jf
        host_scr = refs[pos:pos + n_scr]
        ssem, rsem = refs[pos + n_scr:]
        flat = 0
        for ax, size in enumerate(grid):
            flat = flat * size + pl.program_id(ax)

        def hook(fn, at):
            if fn is not None:
                @pl.when(flat == at)
                def _():
                    fn(j_ins, j_inout, j_fresh, ssem, rsem)

        hook(job.first, 0)
        body(*host_in, *host_out, *host_scr)
        hook(job.mid, min(int(total * job.mid_at), total - 1))
        hook(job.last, total - 1)

    for k in range(n_jio):
        aliases[n_in + n_ji + k] = n_out + k
    outs = pl.pallas_call(
        wrapped, grid=grid,
        in_specs=list(in_specs) + [_HBM] * (n_ji + n_jio),
        out_specs=out_specs + [_HBM] * (n_jio + n_jf),
        out_shape=out_shape + [jax.ShapeDtypeStruct(b.shape, b.dtype) for b in job.inout] + list(job.fresh),
        scratch_shapes=list(scratch_shapes) + [pltpu.SemaphoreType.DMA((job.nsem,)), pltpu.SemaphoreType.DMA((job.nsem,))],
        input_output_aliases=aliases, compiler_params=_cparams(tuple("arbitrary" for _ in grid)), name=name,
    )(*args, *job.ins, *job.inout)
    host = outs[:n_out]
    return (host[0] if single else host), outs[n_out:]


def _run_job(job, name):
    n_ji, n_jio, n_jf = len(job.ins), len(job.inout), len(job.fresh)

    def body(*refs):
        j_ins = refs[:n_ji]
        pos = n_ji + n_jio
        j_inout = refs[pos:pos + n_jio]
        j_fresh = refs[pos + n_jio:pos + n_jio + n_jf]
        ssem, rsem = refs[pos + n_jio + n_jf:]
        for fn in (job.first, job.mid, job.last):
            if fn is not None:
                fn(j_ins, j_inout, j_fresh, ssem, rsem)

    return pl.pallas_call(
        body, in_specs=[_HBM] * (n_ji + n_jio), out_specs=[_HBM] * (n_jio + n_jf),
        out_shape=[jax.ShapeDtypeStruct(b.shape, b.dtype) for b in job.inout] + list(job.fresh),
        scratch_shapes=[pltpu.SemaphoreType.DMA((job.nsem,)), pltpu.SemaphoreType.DMA((job.nsem,))],
        input_output_aliases={n_ji + k: k for k in range(n_jio)}, name=name,
    )(*job.ins, *job.inout)


_DIMS ={"nn": ((1,), (0,)), "nt": ((1,), (1,)), "tn": ((0,), (0,))}


def _matmul(a, b, mode, out_dtype, name, n=None, slab=None, into=None, job=None, norms=None, silu_of=None, tm_cap=512,
            tn_cap=2048, tk_cap=1408):
    if mode == "nn":
        (m, k), (k2, nn_) = a.shape, b.shape
    elif mode == "nt":
        (m, k), (nn_, k2) = a.shape, b.shape
    else:
        (k, m), (k2, nn_) = a.shape, b.shape
    n = nn_ if n is None else n
    assert k == k2, (a.shape, b.shape, mode)
    tm, tn, tk = _tile(m, tm_cap), _tile(n, tn_cap), _tile(k, tk_cap)
    if slab is not None and slab[2]:
        tm = _tile(math.gcd(m, slab[2]), tm_cap)
    nk = k // tk
    if mode == "tn":
        a_spec = pl.BlockSpec((tk, tm), lambda j, i, kk, *_: (kk, i))
    else:
        a_spec = pl.BlockSpec((tm, tk), lambda j, i, kk, *_: (i, kk))
    if mode == "nt":
        b_spec = pl.BlockSpec((tn, tk), lambda j, i, kk, *_: (j, kk))
    else:
        b_spec = pl.BlockSpec((tk, tn), lambda j, i, kk, *_: (kk, j))
    dims = (_DIMS[mode], ((), ()))
    aliased = into is not None

    n_a = 1 if silu_of is None else 2
    n_in = n_a + 1 + aliased + (3 if norms is not None else 0)
    n_main = 1 + (2 if norms is not None else 0)

    def finish(refs, z):
        refs[n_in][...] = z.astype(out_dtype).reshape(refs[n_in].shape)
        if norms is not None:
            x_ref, gp_ref, gn_ref = refs[n_in - 3:n_in]
            r = lax.rsqrt(jnp.mean(z * z, axis=-1, keepdims=True) + EPS)
            xn = x_ref[...] + z * r * gp_ref[...]
            refs[n_in + 1][...] = xn
            r2 = lax.rsqrt(jnp.mean(xn * xn, axis=-1, keepdims=True) + EPS)
            refs[n_in + 2][...] = (xn * r2 * gn_ref[...]).astype(BF16)

    def body(*refs):
        lhs = refs[0][...]
        if silu_of is not None:
            gv = refs[1][...].astype(F32)
            lhs = (gv * _sigmoid(gv) * lhs.astype(F32)).astype(BF16)
            refs[n_in + n_main][...] = lhs
        p = lax.dot_general(lhs, refs[n_a][...], dims, preferred_element_type=F32)
        if nk == 1:
            finish(refs, p)
        else:
            acc = refs[-1]
            kk = pl.program_id(2)

            @pl.when(kk == 0)
            def _():
                acc[...] = p

            @pl.when(kk > 0)
            def _():
                acc[...] += p

            @pl.when(kk == nk - 1)
            def _():
                finish(refs, acc[...])

    if slab is None:
        out_spec = pl.BlockSpec((tm, tn), lambda j, i, kk: (i, j))
        out_shape = jax.ShapeDtypeStruct((m, n), out_dtype)
    else:
        shape3, lead, row0 = slab
        assert row0 % tm == 0 and shape3[2] == n
        out_spec = pl.BlockSpec((1, tm, tn), lambda j, i, kk: (lead, row0 // tm + i, j))
        out_shape = jax.ShapeDtypeStruct(shape3, out_dtype)
    in_specs, args = [a_spec, b_spec], [a, b]
    if silu_of is not None:
        assert mode == "nn" and tn == n and slab is None, "the left operand's blocks are written once each"
        in_specs, args = [a_spec, a_spec, b_spec], [a, silu_of, b]
    if aliased:
        in_specs.append(pl.BlockSpec(memory_space=pl.ANY))
        args.append(into)
    if norms is not None:
        assert tn == n and slab is None, "the fused norms need whole rows"
        row = pl.BlockSpec((tm, n), lambda j, i, kk: (i, 0))
        vec = pl.BlockSpec((1, n), lambda j, i, kk: (0, 0))
        in_specs += [row, vec, vec]
        args += list(norms)
        out_spec = [out_spec, row, row]
        out_shape = [out_shape, jax.ShapeDtypeStruct((m, n), F32), jax.ShapeDtypeStruct((m, n), BF16)]
    if silu_of is not None:
        out_spec = (out_spec if isinstance(out_spec, list) else [out_spec]) + [pl.BlockSpec((tm, tk), lambda j, i, kk: (i, kk))]
        out_shape = (out_shape if isinstance(out_shape, list) else [out_shape]) + [jax.ShapeDtypeStruct((m, k), BF16)]
    out, moved = _call(
        body, grid=(n // tn, m // tm, nk), in_specs=in_specs, out_specs=out_spec, out_shape=out_shape,
        scratch_shapes=[pltpu.VMEM((tm, tn), F32)] if nk > 1 else [], dims=("parallel", "parallel", "arbitrary"), name=name,
        args=args, aliases={n_a + 1: 0} if aliased else None, job=job)
    return out if job is None else (out, moved)


def _matmul_rows(pieces, b, rows, name, job=None, tm=1024, tk=1024):
    k, n = b.shape
    tm = math.gcd(tm, *[a.shape[1] for a in pieces])
    tk = _tile(k, tk)
    nk = k // tk
    spans, r0 = [], 0
    for a in pieces:
        assert a.shape[0] == k and a.shape[1] % tm == 0
        spans.append((r0, a.shape[1] // tm))
        r0 += a.shape[1] // tm
    nr = r0
    np_ = len(pieces)

    def body(*refs):
        b_ref, o_ref, acc = refs[np_], refs[np_ + 1], refs[-1]
        r, kk = pl.program_id(0), pl.program_id(1)
        for p, (first, count) in enumerate(spans):
            @pl.when((r >= first) & (r < first + count))
            def _(p=p):
                part = lax.dot_general(refs[p][...], b_ref[...], (_DIMS["tn"], ((), ())), preferred_element_type=F32)

                @pl.when(kk == 0)
                def _():
                    acc[...] = part

                @pl.when(kk > 0)
                def _():
                    acc[...] += part

        @pl.when(kk == nk - 1)
        def _():
            o_ref[0] = acc[...].astype(BF16)

    in_specs = []
    for first, count in spans:
        in_specs.append(pl.BlockSpec((tk, tm), lambda r, kk, f=first, c=count: (
            jnp.where(r < f, 0, jnp.where(r >= f + c, nk - 1, kk)), jnp.clip(r - f, 0, c - 1))))
    in_specs.append(pl.BlockSpec((tk, n), lambda r, kk: (kk, 0)))
    out, moved = _call(
        body, grid=(nr, nk), in_specs=in_specs, out_specs=pl.BlockSpec((1, tm, n), lambda r, kk: (0, r, 0)),
        out_shape=jax.ShapeDtypeStruct((1, rows, n), BF16), scratch_shapes=[pltpu.VMEM((tm, n), F32)],
        dims=("arbitrary", "arbitrary"), name=name, args=list(pieces) + [b], job=job)
    return out if job is None else (out, moved)


def _matmul_pieces(pieces, addend, name, tk, job=None, tm_cap=512):
    m = pieces[0][0].shape[0]
    n = pieces[0][1].shape[1]
    tm = _tile(m, tm_cap)
    spans, s0 = [], 0
    for a, b, row0 in pieces:
        assert a.shape[1] % tk == 0 and row0 % tk == 0 and b.shape[1] == n and a.shape[0] == m
        spans.append((s0, a.shape[1] // tk, row0 // tk))
        s0 += a.shape[1] // tk
    steps = s0
    np_ = len(pieces)
    groups = []
    for (a, b, _), (first, count, brow) in zip(pieces, spans):
        if groups and groups[-1][0] is b and groups[-1][3] + groups[-1][2] == brow:
            groups[-1][2] += count
        else:
            groups.append([b, first, count, brow])
    b_of = []
    for first, count, _ in spans:
        b_of.append(next(k for k, g in enumerate(groups) if g[1] <= first < g[1] + g[2]))
    ng = len(groups)

    nm = m // tm

    def body(*refs):
        o_ref, acc = refs[-2], refs[-1]
        s, i = pl.program_id(0), pl.program_id(1)
        rows = pl.ds(pl.multiple_of(i * tm, tm), tm)

        @pl.when(s == 0)
        def _():
            acc[rows, :] = refs[np_ + ng][...] if addend is not None else jnp.zeros((tm, n), F32)

        for p, (first, count, _) in enumerate(spans):
            @pl.when((s >= first) & (s < first + count))
            def _(p=p):
                acc[rows, :] += jnp.dot(refs[p][...], refs[np_ + b_of[p]][...], preferred_element_type=F32)

        @pl.when(s == steps - 1)
        def _():
            o_ref[...] = acc[rows, :]

    in_specs, args = [], []
    for (a, _, _), (first, count, _) in zip(pieces, spans):
        in_specs.append(pl.BlockSpec((tm, tk), lambda s, i, f=first, c=count: (
            jnp.where(s < f, 0, jnp.where(s >= f + c, nm - 1, i)), jnp.clip(s - f, 0, c - 1))))
        args.append(a)
    for b, first, count, brow in groups:
        in_specs.append(pl.BlockSpec((tk, n), lambda s, i, f=first, c=count, r=brow: (r + jnp.clip(s - f, 0, c - 1), 0)))
        args.append(b)
    if addend is not None:
        in_specs.append(pl.BlockSpec((tm, n), lambda s, i: (jnp.where(s == 0, i, nm - 1), 0)))
        args.append(addend)
    out, moved = _call(
        body, grid=(steps, nm), in_specs=in_specs,
        out_specs=pl.BlockSpec((tm, n), lambda s, i: (jnp.where(s == steps - 1, i, 0), 0)),
        out_shape=jax.ShapeDtypeStruct((m, n), F32), scratch_shapes=[pltpu.VMEM((m, n), F32)],
        dims=("arbitrary", "arbitrary"), name=name, args=args, job=job)
    return out if job is None else (out, moved)


def _norm_fwd(x, z, g_post, g_next, name):
    t, d = x.shape
    tt = _rows(t, 512)
    row = pl.BlockSpec((tt, d), lambda i: (i, 0))
    vec = pl.BlockSpec((1, d), lambda i: (0, 0))

    def body(*refs):
        if z is None:
            x_ref, gn_ref, h_ref = refs
            xn = x_ref[...]
        else:
            x_ref, z_ref, gp_ref, gn_ref, xo_ref, h_ref = refs
            zz = z_ref[...]
            r = lax.rsqrt(jnp.mean(zz * zz, axis=-1, keepdims=True) + EPS)
            xn = x_ref[...] + zz * r * gp_ref[...]
            xo_ref[...] = xn
        r2 = lax.rsqrt(jnp.mean(xn * xn, axis=-1, keepdims=True) + EPS)
        h_ref[...] = (xn * r2 * gn_ref[...]).astype(BF16)

    if z is None:
        return pl.pallas_call(
            body, grid=(t // tt,), in_specs=[row, vec], out_specs=row,
            out_shape=jax.ShapeDtypeStruct((t, d), BF16), compiler_params=_cparams(("parallel",)), name=name,
        )(x, g_next)
    return pl.pallas_call(
        body, grid=(t // tt,), in_specs=[row, row, vec, vec], out_specs=[row, row],
        out_shape=[jax.ShapeDtypeStruct((t, d), F32), jax.ShapeDtypeStruct((t, d), BF16)],
        compiler_params=_cparams(("parallel",)), name=name,
    )(x, z, g_post, g_next)


def _rms_bwd(dy, x, g):
    r = lax.rsqrt(jnp.mean(x * x, axis=-1, keepdims=True) + EPS)
    n = x * r
    dn = dy * g
    dx = r * (dn - n * jnp.mean(dn * n, axis=-1, keepdims=True))
    return dx, dy * n


def _norm_bwd(dres, pre, post, name):
    t, d = dres.shape
    tt = _rows(t, 512)
    nt = t // tt
    row = pl.BlockSpec((tt, d), lambda i: (i, 0))
    vec = pl.BlockSpec((1, d), lambda i: (0, 0))
    has_pre, has_post = pre is not None, post is not None
    n_in = 1 + (3 if has_pre else 0) + (2 if has_post else 0)
    n_out = has_pre + has_post + has_pre + has_post

    def body(*refs):
        ins, outs, scr = refs[:n_in], refs[n_in:n_in + n_out], refs[n_in + n_out:]
        i = pl.program_id(0)
        dx = ins[0][...]
        pos, opos, spos = 1, 0, 0
        accs = []
        if has_pre:
            dh_ref, xa_ref, ga_ref = ins[pos:pos + 3]
            pos += 3
            dxa, dga_t = _rms_bwd(dh_ref[...], xa_ref[...], ga_ref[...])
            dx = dx + dxa
            outs[opos][...] = dx
            opos += 1
            accs.append((scr[spos], dga_t))
            spos += 1
        if has_post:
            zb_ref, gb_ref = ins[pos:pos + 2]
            dz, dgb_t = _rms_bwd(dx, zb_ref[...], gb_ref[...])
            outs[opos][...] = dz.astype(BF16)
            opos += 1
            accs.append((scr[spos], dgb_t))
            spos += 1
        for (acc, val), out in zip(accs, outs[opos:]):
            part = _sum8(val)

            @pl.when(i == 0)
            def _(acc=acc, part=part):
                acc[...] = part

            @pl.when(i > 0)
            def _(acc=acc, part=part):
                acc[...] += part

            @pl.when(i == nt - 1)
            def _(acc=acc, out=out):
                out[...] = jnp.sum(acc[...], axis=0, keepdims=True)

    in_specs, args = [row], [dres]
    out_specs, out_shape = [], []
    if has_pre:
        in_specs += [row, row, vec]
        args += list(pre)
        out_specs.append(row)
        out_shape.append(jax.ShapeDtypeStruct((t, d), F32))
    if has_post:
        in_specs += [row, vec]
        args += list(post)
        out_specs.append(row)
        out_shape.append(jax.ShapeDtypeStruct((t, d), BF16))
    for _ in range(has_pre + has_post):
        out_specs.append(vec)
        out_shape.append(jax.ShapeDtypeStruct((1, d), F32))
    return pl.pallas_call(
        body, grid=(nt,), in_specs=in_specs, out_specs=out_specs, out_shape=out_shape,
        scratch_shapes=[pltpu.VMEM((SUBLANE, d), F32)] * (has_pre + has_post),
        compiler_params=_cparams(("arbitrary",)), name=name,
    )(*args)


def _loss_grad(y, target, name):
    t, d = y.shape
    tt = _rows(t, 512)
    nt = t // tt
    row = pl.BlockSpec((tt, d), lambda i: (i, 0))
    inv_d = 1.0 / d

    def body(y_ref, t_ref, dy_ref, l_ref):
        i = pl.program_id(0)
        diff = y_ref[...] - t_ref[...]
        dy_ref[...] = diff * inv_d
        s8 = _sum8(diff * diff)
        part = s8[:, 0:LANE]
        for k in range(1, d // LANE):
            part = part + s8[:, k * LANE:(k + 1) * LANE]
        part = part * (0.5 * inv_d)

        @pl.when(i == 0)
        def _():
            l_ref[...] = part

        @pl.when(i > 0)
        def _():
            l_ref[...] += part

    return pl.pallas_call(
        body, grid=(nt,), in_specs=[row, row],
        out_specs=[row, pl.BlockSpec((SUBLANE, LANE), lambda i: (0, 0))],
        out_shape=[jax.ShapeDtypeStruct((t, d), F32), jax.ShapeDtypeStruct((SUBLANE, LANE), F32)],
        compiler_params=_cparams(("arbitrary",)), name=name,
    )(y, target)


def _swiglu_bwd(a, b, dm, name):
    t, f = a.shape
    tt = _rows(t, 256)
    blk = pl.BlockSpec((tt, f), lambda i: (i, 0))

    def body(a_ref, b_ref, dm_ref, da_ref, db_ref):
        av = a_ref[...].astype(F32)
        s = _sigmoid(av)
        dv = dm_ref[...].astype(F32)
        da_ref[...] = (dv * b_ref[...].astype(F32) * s * (1.0 + av * (1.0 - s))).astype(BF16)
        db_ref[...] = (dv * av * s).astype(BF16)

    return pl.pallas_call(
        body, grid=(t // tt,), in_specs=[blk, blk, blk], out_specs=[blk, blk],
        out_shape=[jax.ShapeDtypeStruct((t, f), BF16)] * 2, compiler_params=_cparams(("parallel",)), name=name,
    )(a, b, dm)


def _log_sigmoid(x):
    return jnp.minimum(x, 0.0) - jnp.log1p(jnp.exp(-jnp.abs(x)))


def _fox_prep(f_t, b_f, name):
    h, t = f_t.shape

    def body(f_ref, b_ref, c_ref):
        r = lax.broadcasted_iota(jnp.int32, (LANE, LANE), 0)
        c = lax.broadcasted_iota(jnp.int32, (LANE, LANE), 1)
        upper = (r <= c).astype(F32)
        carry = jnp.zeros((h, 1), F32)
        for j in range(t // LANE):
            sl = slice(j * LANE, (j + 1) * LANE)
            lf = _log_sigmoid(f_ref[:, sl] + b_ref[...])
            cs = jnp.dot(lf, upper, precision=lax.Precision.HIGHEST, preferred_element_type=F32) + carry
            c_ref[:, sl] = cs
            carry = cs[:, LANE - 1:LANE]

    return pl.pallas_call(body, out_shape=jax.ShapeDtypeStruct((h, t), F32), compiler_params=_cparams(), name=name)(f_t, b_f)


def _fox_bwd(dc_q, dc_k, f_t, b_f, name):
    h, t = f_t.shape

    def body(dq_ref, dk_ref, f_ref, b_ref, df_ref, db_ref):
        r = lax.broadcasted_iota(jnp.int32, (LANE, LANE), 0)
        c = lax.broadcasted_iota(jnp.int32, (LANE, LANE), 1)
        lower = (r >= c).astype(F32)
        carry = jnp.zeros((h, 1), F32)
        dbsum = jnp.zeros((h, 1), F32)
        for j in reversed(range(t // LANE)):
            sl = slice(j * LANE, (j + 1) * LANE)
            dc = dq_ref[:, sl] - dk_ref[:, sl]
            dl = jnp.dot(dc, lower, precision=lax.Precision.HIGHEST, preferred_element_type=F32) + carry
            carry = dl[:, 0:1]
            df = dl * _sigmoid(-(f_ref[:, sl] + b_ref[...]))
            df_ref[:, sl] = df
            dbsum = dbsum + jnp.sum(df, axis=-1, keepdims=True)
        db_ref[...] = dbsum

    return pl.pallas_call(
        body, out_shape=[jax.ShapeDtypeStruct((h, t), F32), jax.ShapeDtypeStruct((h, 1), F32)],
        compiler_params=_cparams(), name=name,
    )(dc_q, dc_k, f_t, b_f)


ATTN_FWD = (1024, 512)
ATTN_BWD = (512, 512)


def _attn_tiles(t, tiles):
    return _tile(t, tiles[0]), _tile(t, tiles[1])


def _attn_fwd(proj, c_t, d, name, job=None):
    t = proj.shape[0]
    h = d // LANE
    bq, bk = _attn_tiles(t, ATTN_FWD)
    nq, nk, rr = t // bq, t // bk, bq // bk
    qc, kc, vc = COL_Q * h, COL_K * h, COL_VA * h
    qscale = LANE ** -0.5 * LOG2E

    def body(q_ref, k_ref, v_ref, cc_ref, cr_ref, o_ref, lse_ref, kb, vt, ckb, acc):
        i = pl.program_id(1)

        @pl.when(i == 0)
        def _():
            kb[...] = k_ref[...].astype(BF16)
            ckb[...] = jnp.broadcast_to(cc_ref[0] * LOG2E, (t, bq))
            for jn in range(nk):
                vt[jn] = v_ref[jn * bk:(jn + 1) * bk, :].astype(F32).T.astype(BF16)

        q = (q_ref[...].astype(F32) * qscale).astype(BF16)
        cq = cr_ref[0, 0] * LOG2E
        acc[...] = jnp.zeros((LANE, bq), F32)

        def block(j, diag, m_old, l_old):
            off = 0 if diag is None else diag * bk
            w = bq - off
            rows = pl.ds(pl.multiple_of(j * bk, bk), bk)
            s = _nt_dot(kb[rows, :], q[off:, :]) - ckb[rows, off:]
            if diag is not None:
                kk = lax.broadcasted_iota(jnp.int32, (bk, w), 0)
                qq = lax.broadcasted_iota(jnp.int32, (bk, w), 1)
                s = jnp.where(qq >= kk, s, NEG)
            cqs, m_part, l_part = cq[:, off:], m_old[:, off:], l_old[:, off:]
            m_new = jnp.maximum(m_part, jnp.max(s, axis=0, keepdims=True) + cqs)
            p = jnp.exp2(s + (cqs - m_new))
            alpha = jnp.exp2(m_part - m_new)
            l_new = alpha * l_part + jnp.sum(p, axis=0, keepdims=True)
            acc[:, off:] = alpha * acc[:, off:] + jnp.dot(vt[j], p.astype(BF16), preferred_element_type=F32)
            if off:
                m_new = jnp.concatenate([m_old[:, :off], m_new], axis=1)
                l_new = jnp.concatenate([l_old[:, :off], l_new], axis=1)
            return m_new, l_new

        m, l = lax.fori_loop(0, i * rr, lambda j, c: block(j, None, *c),
                             (jnp.full((1, bq), NEG, F32), jnp.zeros((1, bq), F32)))
        for jj in range(rr):
            m, l = block(i * rr + jj, jj, m, l)
        o_ref[...] = (acc[...] / l).T
        lse_ref[0, 0] = m + jnp.log2(l)

    rowq = pl.BlockSpec((1, 1, 1, bq), lambda hh, i: (hh, i, 0, 0))
    outs, moved = _call(
        body, grid=(h, nq),
        in_specs=[
            pl.BlockSpec((bq, LANE), lambda hh, i: (i, qc + hh)),
            pl.BlockSpec((t, LANE), lambda hh, i: (0, kc + hh)),
            pl.BlockSpec((t, LANE), lambda hh, i: (0, vc + hh)),
            pl.BlockSpec((1, t, 1), lambda hh, i: (hh, 0, 0)),
            rowq,
        ],
        out_specs=[pl.BlockSpec((bq, LANE), lambda hh, i: (i, hh)), rowq],
        out_shape=[jax.ShapeDtypeStruct((t, d), F32), jax.ShapeDtypeStruct((h, nq, 1, bq), F32)],
        scratch_shapes=[pltpu.VMEM((t, LANE), BF16), pltpu.VMEM((nk, LANE, bk), BF16), pltpu.VMEM((t, bq), F32),
                        pltpu.VMEM((LANE, bq), F32)],
        dims=("arbitrary", "arbitrary"), name=name,
        args=[proj, proj, proj, c_t.reshape(h, t, 1), c_t.reshape(h, nq, 1, bq)], job=job)
    outs = [outs[0], outs[1].reshape(h, t)]
    return outs if job is None else (outs, moved)


def _attn_bwd(proj, do, o, lse, c_t, d, name, job=None):
    t = proj.shape[0]
    h = d // LANE
    bq, bk = _attn_tiles(t, ATTN_BWD)
    nq, nk, rr = t // bq, t // bk, bq // bk
    qc, kc, vc = COL_Q * h, COL_K * h, COL_VA * h
    scale = LANE ** -0.5

    def body(q_ref, k_ref, v_ref, do_ref, o_ref, lse_ref, cc_ref, cr_ref, dq_ref, dk_ref, dv_ref, dcq_ref, dck_ref,
             kb, kt, vb, ckb, dk_acc, dv_acc, dck_acc, dqt_acc):
        i = pl.program_id(1)

        @pl.when(i == 0)
        def _():
            kb[...] = k_ref[...].astype(BF16)
            vb[...] = v_ref[...].astype(BF16)
            ckb[...] = jnp.broadcast_to(cc_ref[0] * LOG2E, (t, bq))
            for jn in range(nk):
                kt[jn] = k_ref[jn * bk:(jn + 1) * bk, :].astype(F32).T.astype(BF16)
            dk_acc[...] = jnp.zeros((t, LANE), F32)
            dv_acc[...] = jnp.zeros((t, LANE), F32)
            dck_acc[...] = jnp.zeros((t, LANE), F32)

        q = (q_ref[...].astype(F32) * (scale * LOG2E)).astype(BF16)
        dof = do_ref[...]
        dob = dof.astype(BF16)
        delta = jnp.sum((dof * o_ref[...]).T, axis=0, keepdims=True)
        rowb = cr_ref[0, 0] * LOG2E - lse_ref[0, 0]
        dqt_acc[...] = jnp.zeros((LANE, bq), F32)

        def block(j, diag, dcq):
            rows = pl.ds(pl.multiple_of(j * bk, bk), bk)
            p = jnp.exp2(_nt_dot(kb[rows, :], q) - ckb[rows, :] + rowb)
            if diag is not None:
                kk = lax.broadcasted_iota(jnp.int32, (bk, bq), 0)
                qq = lax.broadcasted_iota(jnp.int32, (bk, bq), 1)
                p = jnp.where(qq >= kk + diag * bk, p, 0.0)
            dv_acc[rows, :] += jnp.dot(p.astype(BF16), dob, preferred_element_type=F32)
            ds = p * (_nt_dot(vb[rows, :], dob) - delta)
            dsb = ds.astype(BF16)
            dk_acc[rows, :] += jnp.dot(dsb, q, preferred_element_type=F32)
            dqt_acc[...] += jnp.dot(kt[j], dsb, preferred_element_type=F32)
            part = ds[:, 0:LANE]
            for k in range(1, bq // LANE):
                part = part + ds[:, k * LANE:(k + 1) * LANE]
            dck_acc[rows, :] += part
            return dcq + jnp.sum(ds, axis=0, keepdims=True)

        dcq = lax.fori_loop(0, i * rr, lambda j, c: block(j, None, c), jnp.zeros((1, bq), F32))
        for jj in range(rr):
            dcq = block(i * rr + jj, jj, dcq)
        dq_ref[...] = (dqt_acc[...] * scale).T.astype(BF16)
        dcq_ref[0, 0] = dcq

        @pl.when(i == nq - 1)
        def _():
            dk_ref[...] = (dk_acc[...] * LN2).astype(BF16)
            dv_ref[...] = dv_acc[...].astype(BF16)
            dck_ref[0] = jnp.sum(dck_acc[...], axis=-1, keepdims=True)

    rowq = pl.BlockSpec((1, 1, 1, bq), lambda hh, i: (hh, i, 0, 0))
    blk = pl.BlockSpec((bq, LANE), lambda hh, i: (i, hh))
    whole = pl.BlockSpec((t, LANE), lambda hh, i: (0, hh))
    colk = pl.BlockSpec((1, t, 1), lambda hh, i: (hh, 0, 0))
    outs, moved = _call(
        body, grid=(h, nq),
        in_specs=[
            pl.BlockSpec((bq, LANE), lambda hh, i: (i, qc + hh)),
            pl.BlockSpec((t, LANE), lambda hh, i: (0, kc + hh)),
            pl.BlockSpec((t, LANE), lambda hh, i: (0, vc + hh)),
            blk, blk, rowq, colk, rowq,
        ],
        out_specs=[blk, whole, whole, rowq, colk],
        out_shape=[jax.ShapeDtypeStruct((t, d), BF16), jax.ShapeDtypeStruct((t, d), BF16), jax.ShapeDtypeStruct((t, d), BF16),
                   jax.ShapeDtypeStruct((h, nq, 1, bq), F32), jax.ShapeDtypeStruct((h, t, 1), F32)],
        scratch_shapes=[pltpu.VMEM((t, LANE), BF16), pltpu.VMEM((nk, LANE, bk), BF16), pltpu.VMEM((t, LANE), BF16),
                        pltpu.VMEM((t, bq), F32), pltpu.VMEM((t, LANE), F32), pltpu.VMEM((t, LANE), F32),
                        pltpu.VMEM((t, LANE), F32), pltpu.VMEM((LANE, bq), F32)],
        dims=("arbitrary", "arbitrary"), name=name,
        args=[proj, proj, proj, do, o, lse.reshape(h, nq, 1, bq), c_t.reshape(h, t, 1), c_t.reshape(h, nq, 1, bq)], job=job)
    outs = list(outs[:3]) + [outs[3].reshape(h, t), outs[4].reshape(h, t)]
    return outs if job is None else (outs, moved)


def _sgu_forward(u_ref, v_ref, gv_ref, wm_ref, bs_ref, mix_sc, groups):
    gu, dgu = _gelu_and_grad(u_ref[...].astype(F32))
    gvv, dgv = _gelu_and_grad(v_ref[...].astype(F32))
    mu = jnp.mean(gvv, axis=-1, keepdims=True)
    xc = gvv - mu
    r = lax.rsqrt(jnp.mean(xc * xc, axis=-1, keepdims=True) + EPS)
    nhat = xc * r
    vn = (nhat * gv_ref[...]).astype(BF16)
    for g in range(groups):
        sl = slice(g * LANE, (g + 1) * LANE)
        mix_sc[:, sl] = jnp.dot(wm_ref[g], vn[:, sl], preferred_element_type=F32) + bs_ref[g]
    return gu, dgu, dgv, nhat, r, vn, mix_sc[...]


def _mix_fwd(proj, o, wm, bs, g_v, d, name):
    t = proj.shape[0]
    groups = d // LANE

    def body(u_ref, v_ref, ga_ref, gb_ref, o_ref, wm_ref, bs_ref, gv_ref, out_ref, mix_sc):
        gu, _, _, _, _, _, mixed = _sgu_forward(u_ref, v_ref, gv_ref, wm_ref, bs_ref, mix_sc, groups)
        out_ref[...] = (_sigmoid(ga_ref[...].astype(F32)) * (gu * mixed) + _sigmoid(gb_ref[...].astype(F32)) * o_ref[...]).astype(BF16)

    def colblk(k):
        return pl.BlockSpec((LANE, d), lambda i, k=k: (i, k))

    full3 = pl.BlockSpec((groups, LANE, LANE), lambda i: (0, 0, 0))
    return pl.pallas_call(
        body, grid=(t // LANE,),
        in_specs=[colblk(COL_U), colblk(COL_V), colblk(COL_GA), colblk(COL_GB), colblk(0), full3,
                  pl.BlockSpec((groups, LANE, 1), lambda i: (0, 0, 0)), pl.BlockSpec((1, d), lambda i: (0, 0))],
        out_specs=colblk(0),
        out_shape=jax.ShapeDtypeStruct((t, d), BF16),
        scratch_shapes=[pltpu.VMEM((LANE, d), F32)],
        compiler_params=_cparams(("parallel",)), name=name,
    )(proj, proj, proj, proj, o, wm, bs, g_v)


def _mix_bwd(dmerged, proj, o, wm, wm_t, bs, g_v, d, name, job=None):
    t = proj.shape[0]
    groups = d // LANE
    nt = t // LANE

    def body(dm_ref, u_ref, v_ref, ga_ref, gb_ref, o_ref, wm_ref, wmt_ref, bs_ref, gv_ref,
             duv_ref, dg_ref, do_ref, dws_ref, dbs_ref, dgv_ref, mix_sc, dvn_sc, gv_acc):
        i = pl.program_id(0)

        @pl.when(i == 0)
        def _():
            dws_ref[...] = jnp.zeros_like(dws_ref)
            dbs_ref[...] = jnp.zeros_like(dbs_ref)
            gv_acc[...] = jnp.zeros_like(gv_acc)

        gu, dgu, dgv, nhat, r, vn, mixed = _sgu_forward(u_ref, v_ref, gv_ref, wm_ref, bs_ref, mix_sc, groups)
        dm = dm_ref[...]
        sa = _sigmoid(ga_ref[...].astype(F32))
        sb = _sigmoid(gb_ref[...].astype(F32))
        ov = o_ref[...]
        y_a = gu * mixed
        dg_ref[:, 0:d] = (dm * y_a * sa * (1.0 - sa)).astype(BF16)
        dg_ref[:, d:2 * d] = (dm * ov * sb * (1.0 - sb)).astype(BF16)
        do_ref[...] = dm * sb
        dy_a = dm * sa
        duv_ref[:, 0:d] = (dy_a * mixed * dgu).astype(BF16)
        dmixed = dy_a * gu
        dmixed_b = dmixed.astype(BF16)
        for g in range(groups):
            sl = slice(g * LANE, (g + 1) * LANE)
            dvn_sc[:, sl] = jnp.dot(wmt_ref[g], dmixed_b[:, sl], preferred_element_type=F32)
            dws_ref[g] += _nt_dot(dmixed_b[:, sl], vn[:, sl])
            dbs_ref[g] += jnp.sum(dmixed[:, sl], axis=-1, keepdims=True)
        dvn = dvn_sc[...]
        gv_acc[...] += _sum8(dvn * nhat)
        dn = dvn * gv_ref[...]
        dgelu = r * (dn - jnp.mean(dn, axis=-1, keepdims=True) - nhat * jnp.mean(dn * nhat, axis=-1, keepdims=True))
        duv_ref[:, d:2 * d] = (dgelu * dgv).astype(BF16)

        @pl.when(i == nt - 1)
        def _():
            dgv_ref[...] = jnp.sum(gv_acc[...], axis=0, keepdims=True)
            rr = lax.broadcasted_iota(jnp.int32, (LANE, LANE), 0)
            cl = lax.broadcasted_iota(jnp.int32, (LANE, LANE), 1)
            for g in range(groups):
                dws_ref[g] = jnp.where(rr >= cl, dws_ref[g], 0.0)

    def colblk(k):
        return pl.BlockSpec((LANE, d), lambda i, k=k: (i, k))

    full3 = pl.BlockSpec((groups, LANE, LANE), lambda i: (0, 0, 0))
    col3 = pl.BlockSpec((groups, LANE, 1), lambda i: (0, 0, 0))
    vec = pl.BlockSpec((1, d), lambda i: (0, 0))
    two = pl.BlockSpec((LANE, 2 * d), lambda i: (i, 0))
    outs, moved = _call(
        body, grid=(nt,),
        in_specs=[colblk(0), colblk(COL_U), colblk(COL_V), colblk(COL_GA), colblk(COL_GB), colblk(0), full3, full3, col3, vec],
        out_specs=[two, two, colblk(0), full3, col3, vec],
        out_shape=[jax.ShapeDtypeStruct((t, 2 * d), BF16), jax.ShapeDtypeStruct((t, 2 * d), BF16), jax.ShapeDtypeStruct((t, d), F32),
                   jax.ShapeDtypeStruct((groups, LANE, LANE), F32), jax.ShapeDtypeStruct((groups, LANE, 1), F32),
                   jax.ShapeDtypeStruct((1, d), F32)],
        scratch_shapes=[pltpu.VMEM((LANE, d), F32), pltpu.VMEM((LANE, d), F32), pltpu.VMEM((SUBLANE, d), F32)],
        dims=("arbitrary",), name=name, args=[dmerged, proj, proj, proj, proj, o, wm, wm_t, bs, g_v], job=job)
    return outs if job is None else (outs, moved)


def _adam_math(w, g, m, v):
    nm = ADAM_B1 * m + (1.0 - ADAM_B1) * g
    nv = ADAM_B2 * v + (1.0 - ADAM_B2) * (g * g)
    delta = -ADAM_LR * ((nm * ADAM_C1) / (jnp.sqrt(nv * ADAM_C2) + ADAM_EPS) + ADAM_WD * w)
    return delta, nm, nv


def _adamw(w, g, m, v, name):
    r, c = w.shape
    cap = max(SUBLANE, (2 * 1024 * 1024) // (4 * c) // SUBLANE * SUBLANE)
    tr = _rows(r, cap)

    def body(w_ref, g_ref, m_ref, v_ref, d_ref, nm_ref, nv_ref):
        d_ref[...], nm_ref[...], nv_ref[...] = _adam_math(w_ref[...], g_ref[...], m_ref[...], v_ref[...])

    blk = pl.BlockSpec((tr, c), lambda i: (i, 0))
    return pl.pallas_call(
        body, grid=(r // tr,), in_specs=[blk] * 4, out_specs=[blk] * 3,
        out_shape=[jax.ShapeDtypeStruct((r, c), F32)] * 3, compiler_params=_cparams(("parallel",)), name=name,
    )(w, g, m, v)


def _adamw_layers(w, g0, g1, m, v, name):
    _, r, c = w.shape
    cap = max(SUBLANE, (1024 * 1024) // (4 * c) // SUBLANE * SUBLANE)
    tr = _rows(r, cap)

    def body(w_ref, g0_ref, g1_ref, m_ref, v_ref, g_ref, d_ref, nm_ref, nv_ref):
        gg = jnp.where(pl.program_id(0) == 0, g0_ref[...], g1_ref[...])
        g_ref[0] = gg
        d_ref[0], nm_ref[0], nv_ref[0] = _adam_math(w_ref[0], gg, m_ref[0], v_ref[0])

    lay = pl.BlockSpec((1, tr, c), lambda l, i: (l, i, 0))

    def gspec(l0):
        return pl.BlockSpec((tr, c), lambda l, i: (jnp.where(l == l0, i, 0), 0))

    return pl.pallas_call(
        body, grid=(2, r // tr), in_specs=[lay, gspec(0), gspec(1), lay, lay], out_specs=[lay] * 4,
        out_shape=[jax.ShapeDtypeStruct((2, r, c), F32)] * 4, compiler_params=_cparams(("arbitrary", "arbitrary")), name=name,
    )(w, g0, g1, m, v)


def _adamw_interleaved(w, g0, g1, m, v, name):
    r, _, c = w.shape
    tr = 128

    def body(w_ref, g0_ref, g1_ref, m_ref, v_ref, g_ref, d_ref, nm_ref, nv_ref):
        for l, gl in enumerate((g0_ref, g1_ref)):
            gg = gl[...]
            g_ref[:, l, :] = gg
            d_ref[:, l, :], nm_ref[:, l, :], nv_ref[:, l, :] = _adam_math(w_ref[:, l, :], gg, m_ref[:, l, :], v_ref[:, l, :])

    lay = pl.BlockSpec((tr, 2, c), lambda i: (i, 0, 0))
    flat = pl.BlockSpec((tr, c), lambda i: (i, 0))
    return pl.pallas_call(
        body, grid=(pl.cdiv(r, tr),), in_specs=[lay, flat, flat, lay, lay], out_specs=[lay] * 4,
        out_shape=[jax.ShapeDtypeStruct((r, 2, c), F32)] * 4, compiler_params=_cparams(("parallel",)), name=name,
    )(w, g0, g1, m, v)


def _add_half(p4, recv, c_idx, name):
    _, r, c = p4.shape
    hw = c // 2
    tr = 256 if r % 256 == 0 else r

    def body(c_ref, a_ref, b_ref, o_ref):
        o_ref[...] = (a_ref[...].astype(F32) + b_ref[...].astype(F32)).astype(BF16)

    return pl.pallas_call(
        body,
        grid_spec=pltpu.PrefetchScalarGridSpec(
            num_scalar_prefetch=1, grid=(N_CHIPS, pl.cdiv(r, tr)),
            in_specs=[pl.BlockSpec((1, tr, hw), lambda s, i, cr: (s, i, cr[0])), pl.BlockSpec((1, tr, hw), lambda s, i, cr: (s, i, 0))],
            out_specs=pl.BlockSpec((1, tr, hw), lambda s, i, cr: (s, i, 0)),
        ),
        out_shape=jax.ShapeDtypeStruct((N_CHIPS, r, hw), BF16), compiler_params=_cparams(("parallel", "parallel")), name=name,
    )(c_idx, p4, recv)


def _sum_slots(x, own, sel, name, out_cols=None):
    s, r, c = x.shape
    tr = 128 if r % 128 == 0 else r

    def body(sel_ref, x_ref, own_ref, o_ref):
        mine = own_ref[0].astype(F32)
        acc = jnp.zeros((tr, c), F32)
        for k in range(s):
            acc = acc + jnp.where(sel_ref[0] == k, mine, x_ref[k].astype(F32))
        o_ref[...] = acc

    return pl.pallas_call(
        body,
        grid_spec=pltpu.PrefetchScalarGridSpec(
            num_scalar_prefetch=1, grid=(pl.cdiv(r, tr),),
            in_specs=[pl.BlockSpec((s, tr, c), lambda i, sr: (0, i, 0)), pl.BlockSpec((1, tr, c), lambda i, sr: (sr[1], i, 0))],
            out_specs=pl.BlockSpec((tr, c), lambda i, sr: (i, sr[2])),
        ),
        out_shape=jax.ShapeDtypeStruct((r, out_cols or c), F32), compiler_params=_cparams(("parallel",)), name=name,
    )(sel, x, own)


def _half_cols(width, hc):
    hw = width // 2
    assert hw % LANE == 0
    return pl.ds(pl.multiple_of(hc * hw, LANE), hw)


def _remote(src, dst, ssem, rsem, k, to):
    return pltpu.make_async_remote_copy(src_ref=src, dst_ref=dst, send_sem=ssem.at[k], recv_sem=rsem.at[k], device_id=to,
                                        device_id_type=MESH)


def _gather_job(bufs, mid_at=0.5):
    def part(o, a, slot, hc):
        return o[a].at[slot, :, _half_cols(bufs[a].shape[2], hc)]

    def first(ins, o, fresh, ssem, rsem):
        x, y, c, chips = _place()
        for a in range(len(bufs)):
            mine = part(o, a, 2 * x + y, c)
            for j, chip in enumerate(chips):
                _remote(mine, mine, ssem, rsem, 6 * a + j, (chip[0], chip[1], c)).start()

    def mid(ins, o, fresh, ssem, rsem):
        x, y, c, chips = _place()
        for a in range(len(bufs)):
            for j, chip in enumerate(chips):
                got = part(o, a, 2 * chip[0] + chip[1], c)
                _remote(got, got, ssem, rsem, 6 * a + j, (x, y, c)).wait_recv()
                _remote(got, got, ssem, rsem, 6 * a + 3 + j, (x, y, 1 - c)).start()

    def last(ins, o, fresh, ssem, rsem):
        x, y, c, chips = _place()
        for a in range(len(bufs)):
            for j, chip in enumerate(chips):
                got = part(o, a, 2 * chip[0] + chip[1], 1 - c)
                _remote(got, got, ssem, rsem, 6 * a + 3 + j, (x, y, c)).wait_recv()
        for a in range(len(bufs)):
            mine = part(o, a, 2 * x + y, c)
            for j, chip in enumerate(chips):
                _remote(mine, mine, ssem, rsem, 6 * a + j, (x, y, c)).wait_send()
                passed = part(o, a, 2 * chip[0] + chip[1], c)
                _remote(passed, passed, ssem, rsem, 6 * a + 3 + j, (x, y, c)).wait_send()

    return _Job([], bufs, [], 6 * len(bufs), first, mid, last, mid_at)


def _swap_job(p4s):
    def pairs(ins, fresh, c):
        return [(a, s, ins[a].at[s, :, _half_cols(p4s[a].shape[2], 1 - c)], fresh[a].at[s])
                for a in range(len(p4s)) for s in range(N_CHIPS)]

    def first(ins, inout, fresh, ssem, rsem):
        x, y, c, _ = _place()
        for a, s, src, dst in pairs(ins, fresh, c):
            _remote(src, dst, ssem, rsem, N_CHIPS * a + s, (x, y, 1 - c)).start()

    def last(ins, inout, fresh, ssem, rsem):
        x, y, c, _ = _place()
        for a, s, src, dst in pairs(ins, fresh, c):
            _remote(src, dst, ssem, rsem, N_CHIPS * a + s, (x, y, 1 - c)).wait()

    fresh = [jax.ShapeDtypeStruct(p.shape[:2] + (p.shape[2] // 2,), p.dtype) for p in p4s]
    return _Job(p4s, [], fresh, N_CHIPS * len(p4s), first, None, last)


def _scatter_job(parts):
    def first(ins, inout, fresh, ssem, rsem):
        x, y, c, chips = _place()
        for a in range(len(parts)):
            for j, chip in enumerate(chips):
                _remote(ins[a].at[2 * chip[0] + chip[1]], fresh[a].at[2 * x + y], ssem, rsem, 3 * a + j, (chip[0], chip[1], c)).start()

    def last(ins, inout, fresh, ssem, rsem):
        x, y, c, chips = _place()
        for a in range(len(parts)):
            for j, chip in enumerate(chips):
                slot = 2 * chip[0] + chip[1]
                _remote(ins[a].at[slot], fresh[a].at[slot], ssem, rsem, 3 * a + j, (x, y, c)).wait()

    return _Job(parts, [], [jax.ShapeDtypeStruct(p.shape, p.dtype) for p in parts], 3 * len(parts), first, None, last)


def _share_job(gs):
    def halves(o, a, c):
        width = gs[a].shape[1]
        return o[a].at[:, _half_cols(width, c)], o[a].at[:, _half_cols(width, 1 - c)]

    def first(ins, o, fresh, ssem, rsem):
        x, y, c, _ = _place()
        for a in range(len(gs)):
            mine, _ = halves(o, a, c)
            _remote(mine, mine, ssem, rsem, a, (x, y, 1 - c)).start()

    def last(ins, o, fresh, ssem, rsem):
        x, y, c, _ = _place()
        for a in range(len(gs)):
            mine, theirs = halves(o, a, c)
            _remote(mine, theirs, ssem, rsem, a, (x, y, 1 - c)).wait()

    return _Job([], gs, [], len(gs), first, None, last)


def _gather_all_job(buf):
    def peers():
        x, y, c, _ = _place()
        flips = [(fx, fy, fc) for fx in (0, 1) for fy in (0, 1) for fc in (0, 1)][1:]
        return (x, y, c), [((1 - x) if fx else x, (1 - y) if fy else y, (1 - c) if fc else c) for fx, fy, fc in flips]

    def first(ins, inout, fresh, ssem, rsem):
        (x, y, c), others = peers()
        for k, peer in enumerate(others):
            _remote(ins[0], fresh[0].at[4 * x + 2 * y + c], ssem, rsem, k, peer).start()

    def last(ins, inout, fresh, ssem, rsem):
        me, others = peers()
        for k, peer in enumerate(others):
            _remote(ins[0], fresh[0].at[4 * peer[0] + 2 * peer[1] + peer[2]], ssem, rsem, k, me).wait()

    return _Job([buf], [], [jax.ShapeDtypeStruct((N_DEV,) + buf.shape, buf.dtype)], N_DEV - 1, first, None, last)


class _SemView:
    def __init__(self, sems, off):
        self.sems, self.off = sems, off

    @property
    def at(self):
        return self

    def __getitem__(self, k):
        return self.sems.at[k + self.off]


def _join(jobs):
    spans, pos = [], [0, 0, 0, 0]
    for j in jobs:
        nxt = [pos[0] + len(j.ins), pos[1] + len(j.inout), pos[2] + len(j.fresh), pos[3] + j.nsem]
        spans.append((pos, nxt))
        pos = nxt

    def hook(which):
        fns = [getattr(j, which) for j in jobs]
        if all(f is None for f in fns):
            return None

        def run(ins, inout, fresh, ssem, rsem):
            for fn, (lo, hi) in zip(fns, spans):
                if fn is not None:
                    fn(ins[lo[0]:hi[0]], inout[lo[1]:hi[1]], fresh[lo[2]:hi[2]], _SemView(ssem, lo[3]), _SemView(rsem, lo[3]))

        return run

    mids = [j.mid_at for j in jobs if j.mid is not None]
    joined = _Job([a for j in jobs for a in j.ins], [a for j in jobs for a in j.inout], [a for j in jobs for a in j.fresh],
                  pos[3], hook("first"), hook("mid"), hook("last"), max(mids) if mids else 0.5)
    n_io = pos[1]

    def split(moved):
        return [list(moved[lo[1]:hi[1]]) + list(moved[n_io + lo[2]:n_io + hi[2]]) for lo, hi in spans]

    return joined, split


def _carrying(stages, call):
    stages = [s for s in stages if s is not None]
    if not stages:
        return call(None)
    job, split = _join([s[0] for s in stages])
    out, moved = call(job)
    for (_, done), part in zip(stages, split(moved)):
        done(part)
    return out


def _layer_forward(x, h, w_in_t, rest, sm, g_next, d, stages=None):
    stages = stages or {}
    proj = _carrying([stages.get("proj")], lambda job: _matmul(h, w_in_t, "nt", BF16, "proj_fwd", n=7 * d, tn_cap=1792, job=job))
    f_t = _matmul(w_in_t[7 * d:], h, "nt", F32, "forget_fwd", tn_cap=1024)
    c_t = _fox_prep(f_t, sm["b_f"], "fox_prep")
    o, lse = _carrying([stages.get("attn")], lambda job: _attn_fwd(proj, c_t, d, "attn_fwd", job=job))
    wts = rest()
    merged = _mix_fwd(proj, o, sm["wm"], sm["bs"], sm["g_v"], d, "mix_fwd")
    z, x1, h2 = _matmul(merged, wts["w_out"], "nn", F32, "out_fwd", norms=(x, sm["g_post"], sm["g_fpre"]))
    a = _carrying([stages.get("gate")], lambda job: _matmul(h2, wts["w_g_t"], "nt", BF16, "gate_fwd", tn_cap=1408, job=job))
    b = _carrying([stages.get("up")], lambda job: _matmul(h2, wts["w_u_t"], "nt", BF16, "up_fwd", tn_cap=1408, job=job))
    z2, x_out, h_out, mm = _carrying([stages.get("down")], lambda job: _matmul(
        b, wts["w_d"], "nn", F32, "down_fwd", norms=(x1, sm["g_fpost"], g_next), silu_of=a, job=job))
    return dict(x=x, h=h, proj=proj, f_t=f_t, c_t=c_t, o=o, lse=lse, merged=merged, z=z, x1=x1,
                h2=h2, a=a, b=b, mm=mm, z2=z2, x_out=x_out, h_out=h_out)


class _GradExchange:
    def __init__(self, pay, keys, c_idx, chip):
        self.keys = list(keys)
        self.p4 = [pay[k].reshape(N_CHIPS, pay[k].shape[1] // N_CHIPS, pay[k].shape[2]) for k in self.keys]
        self.c_idx = c_idx
        self.sel = jnp.stack([chip, chip, c_idx[0]]).astype(jnp.int32)
        self.done = 0

    def _after_swap(self, landed):
        self.parts = [_add_half(p, r, self.c_idx, "add_sibling") for p, r in zip(self.p4, landed)]
        self.done = 1

    def _after_scatter(self, landed):
        self.g = [_sum_slots(got, sent, self.sel, "sum_chips", out_cols=p.shape[2])
                  for got, sent, p in zip(landed, self.parts, self.p4)]
        self.done = 2

    def _after_share(self, moved):
        self.g = list(moved)
        self.done = 3

    def stage(self):
        if self.done == 0:
            return _swap_job(self.p4), self._after_swap
        if self.done == 1:
            return _scatter_job(self.parts), self._after_scatter
        if self.done == 2:
            return _share_job(self.g), self._after_share
        return None

    def run(self):
        for name in ("swap_grads", "scatter_grads", "share_grads")[self.done:]:
            job, done = self.stage()
            done(_run_job(job, name))

    def grads(self):
        return dict(zip(self.keys, self.g))


EARLY_KEYS = ("w_d", "w_g", "w_u", "w_out")


def _layer_backward(dz2, dx2, sv, wts, sm, d, c_idx, chip, carried=(), split_own=False, small_stage=None):
    t = dx2.shape[0]
    heads = d // LANE
    ff = wts["w_d"].shape[0]
    in_w = 7 * d + heads
    g, pay = {}, {}
    carried = list(carried)

    def payload(key, a, b, rows, row0, name, extra=()):
        def call(job):
            return _matmul(a, b, "tn", BF16, name, slab=((1, rows, d), 0, row0), into=pay.get(key), job=job, tm_cap=1408,
                           tn_cap=1024, tk_cap=1024)
        pay[key] = _carrying(list(extra), call)

    def nxt(*exchanges):
        return [ex.stage() for ex in exchanges]

    dm = _carrying(nxt(*carried), lambda job: _matmul(dz2, wts["w_d"], "nt", BF16, "down_bwd_x", tn_cap=1408, tk_cap=1024, job=job))
    payload("w_d", sv["mm"], dz2, ff, 0, "down_bwd_w")
    da, db = _swiglu_bwd(sv["a"], sv["b"], dm, "swiglu_bwd")
    dh2 = _matmul_pieces([(da, wts["w_g_t"], 0), (db, wts["w_u_t"], 0)], None, "gu_bwd_x", tk=_tile(ff, 1408))
    payload("w_g", da, sv["h2"], ff, 0, "gate_bwd_w")
    payload("w_u", db, sv["h2"], ff, 0, "up_bwd_w")
    dx1, dz, g["g_fpre"], g["g_post"] = _norm_bwd(dx2, (dh2, sv["x1"], sm["g_fpre"]), (sv["z"], sm["g_post"]), "norm_bwd_mid")
    dmerged = _matmul(dz, wts["w_out"], "nt", F32, "out_bwd_x", tk_cap=1024)
    payload("w_out", sv["merged"], dz, d, 0, "out_bwd_w")
    early = [_GradExchange(pay, EARLY_KEYS, c_idx, chip)] if split_own else []
    d_uv, d_g, do, g["w_s"], g["b_s"], g["g_v"] = _carrying(nxt(*early), lambda job: _mix_bwd(
        dmerged, sv["proj"], sv["o"], sm["wm"], sm["wm_t"], sm["bs"], sm["g_v"], d, "mix_bwd", job=job))
    extra = [small_stage(g)] if small_stage is not None else []
    attn_args = (sv["proj"], do, sv["o"], sv["lse"], sv["c_t"], d)
    dq, dk, dv, dc_q, dc_k = _carrying(nxt(*carried) + extra, lambda job: _attn_bwd(*attn_args, "attn_bwd", job=job))
    df_t, g["b_f"] = _fox_bwd(dc_q, dc_k, sv["f_t"], sm["b_f"], "fox_bwd")
    df_b = df_t.astype(BF16)
    pieces = [(d_uv, COL_U), (dq, COL_Q), (dk, COL_K), (dv, COL_VA), (d_g, COL_GA)]
    pay["w_in"] = _carrying(nxt(*carried, *early), lambda job: _matmul_rows([p for p, _ in pieces], sv["h"], in_w,
                                                                           "proj_bwd_w", job=job))
    w_f_rows = _carrying(nxt(*early), lambda job: _matmul(df_b, sv["h"], "nn", BF16, "forget_bwd_w", tk_cap=1024, job=job))
    pay["w_in"] = lax.dynamic_update_slice(pay["w_in"], w_f_rows[None], (0, 7 * d, 0))
    late = _GradExchange(pay, [k for k in ("w_in",) + EARLY_KEYS if not (split_own and k in EARLY_KEYS)], c_idx, chip)
    mine = [late] if split_own else []
    dh_f = _carrying(nxt(*mine), lambda job: _matmul(df_b, wts["w_in_t"][7 * d:], "tn", F32, "forget_bwd_x", job=job))
    ops = [(p, wts["w_in_t"], col * d) for p, col in pieces]
    dh = _carrying(nxt(*mine), lambda job: _matmul_pieces(ops, dh_f, "proj_bwd_x", job=job, tk=_tile(d, 1024)))
    return dh, dx1, g, early + [late]


def _small_pack(parts):
    flat = jnp.concatenate([p.reshape(-1) for p in parts])
    n = flat.shape[0]
    pad = (-n) % (LANE * LANE)
    return jnp.pad(flat, (0, pad)).reshape(-1, LANE)


def kernel(x, mix_pre_g, w_in, b_forget, sgu_norm_g, w_spatial, b_spatial, w_out, mix_post_g, ffn_pre_g, w_gate, w_up, w_down, ffn_post_g, loss_target, m_mix_pre_g, m_w_in, m_b_forget, m_sgu_norm_g, m_w_spatial, m_b_spatial, m_w_out, m_mix_post_g, m_ffn_pre_g, m_w_gate, m_w_up, m_w_down, m_ffn_post_g, v_mix_pre_g, v_w_in, v_b_forget, v_sgu_norm_g, v_w_spatial, v_b_spatial, v_w_out, v_mix_post_g, v_ffn_pre_g, v_w_gate, v_w_up, v_w_down, v_ffn_post_g):
    depth, d = mix_pre_g.shape
    assert depth == 2, "the AdamW kernels and the exchange schedule are written for two blocks"
    heads = d // LANE
    t = x.shape[1]
    ff = w_down.shape[1] * N_CHIPS
    in_w = w_in.shape[2] * N_CHIPS
    assert in_w == 7 * d + heads
    xs = x.reshape(t, d)
    target = loss_target.reshape(t, d)
    c_idx = lax.axis_index("c").astype(jnp.int32).reshape(1)
    chip = 2 * lax.axis_index("x") + lax.axis_index("y")
    dev = 2 * chip + lax.axis_index("c")

    def in_view(w):
        return jnp.transpose(w, (2, 0, 1))

    def gu_view(w):
        return jnp.transpose(w, (0, 2, 1))

    own = [jnp.transpose(in_view(w_in).astype(BF16), (1, 0, 2)), w_out.astype(BF16), gu_view(w_gate).astype(BF16),
           gu_view(w_up).astype(BF16), w_down.astype(BF16)]
    bufs = [[lax.dynamic_update_slice(lax.empty((N_CHIPS,) + o.shape[1:], BF16), o[l][None], (chip, 0, 0)) for o in own]
            for l in range(depth)]
    first_in = _run_job(_gather_job([bufs[0][0]]), "gather_first")[0]

    def weights(g_in, g_out, g_g, g_u, g_d):
        return dict(w_in_t=g_in.reshape(in_w, d), w_out=g_out.reshape(d, d), w_g_t=g_g.reshape(ff, d),
                    w_u_t=g_u.reshape(ff, d), w_d=g_d.reshape(ff, d))

    tril = jnp.tril(jnp.ones((LANE, LANE), bool))
    smalls = []
    for l in range(depth):
        wm = jnp.where(tril[None], w_spatial[l], 0.0).astype(BF16)
        smalls.append(dict(
            b_f=b_forget[l].reshape(heads, 1), wm=wm, wm_t=jnp.swapaxes(wm, 1, 2), bs=b_spatial[l].reshape(heads, LANE, 1),
            g_v=sgu_norm_g[l].reshape(1, d), g_pre=mix_pre_g[l].reshape(1, d), g_post=mix_post_g[l].reshape(1, d),
            g_fpre=ffn_pre_g[l].reshape(1, d), g_fpost=ffn_post_g[l].reshape(1, d)))

    wts, later = [], {}

    def keep(key):
        def done(moved):
            later[key] = list(moved)
        return done

    def rest_first():
        wts.append(weights(first_in, *later["rest0"], later["attn"][0]))
        return wts[0]

    stages = dict(proj=(_gather_job(bufs[0][1:4], mid_at=1.0), keep("rest0")),
                  attn=(_gather_job(bufs[0][4:5] + bufs[1][0:3], mid_at=0.7), keep("attn")),
                  up=(_gather_job(bufs[1][3:4], mid_at=1.0), keep("u1")), down=(_gather_job(bufs[1][4:5], mid_at=1.0), keep("d1")))
    h = _norm_fwd(xs, None, None, smalls[0]["g_pre"], "norm_first")
    g_after = [smalls[min(l + 1, depth - 1)]["g_pre"] for l in range(depth)]
    saved = [_layer_forward(xs, h, first_in.reshape(in_w, d), rest_first, smalls[0], g_after[0], d, stages)]
    wts.append(weights(*later["attn"][1:4], later["u1"][0], later["d1"][0]))
    for l in range(1, depth):
        saved.append(_layer_forward(saved[l - 1]["x_out"], saved[l - 1]["h_out"], wts[l]["w_in_t"], lambda l=l: wts[l], smalls[l],
                                    g_after[l], d))
    dy, loss_part = _loss_grad(saved[-1]["x_out"], target, "loss")
    loss = lax.psum(jnp.sum(loss_part), ("x", "y", "c"))

    small_shapes = dict(g_pre=(d,), b_f=(heads,), g_v=(d,), w_s=w_spatial.shape[1:], b_s=b_spatial.shape[1:], g_post=(d,),
                        g_fpre=(d,), g_fpost=(d,))
    late_entries = [(0, "g_pre"), (0, "b_f")]
    early_entries = [(l, n) for l in reversed(range(depth)) for n in small_shapes if (l, n) not in late_entries]
    dev_sel = jnp.stack([dev, jnp.zeros_like(dev), jnp.zeros_like(dev)]).astype(jnp.int32)
    small_sum = {}

    def small_exchange(entries, values):
        packed = _small_pack([values[e].reshape(-1) for e in entries])

        def done(moved):
            total = _sum_slots(moved[0], packed[None], dev_sel, "sum_small").reshape(-1)
            off = 0
            for e in entries:
                n = math.prod(small_shapes[e[1]])
                small_sum[e] = total[off:off + n].reshape(small_shapes[e[1]])
                off += n

        return _gather_all_job(packed), done

    grads = [None] * depth
    exchanges = [None] * depth
    dx2 = dy
    dz2, g_fpost = _norm_bwd(dx2, None, (saved[depth - 1]["z2"], smalls[depth - 1]["g_fpost"]), "norm_bwd_top")
    for l in reversed(range(depth)):
        last = l == 0

        def small_stage(g, l=l, g_fpost=g_fpost):
            known = {(k, n): grads[k][n] for k in range(l + 1, depth) for n in small_shapes}
            known.update({(l, n): g[n] for n in g})
            known[(l, "g_fpost")] = g_fpost
            return small_exchange(early_entries, known)

        carried = [ex for k in range(l + 1, depth) for ex in exchanges[k]]
        dh, dx1, g, exchanges[l] = _layer_backward(dz2, dx2, saved[l], wts[l], smalls[l], d, c_idx, chip, carried=carried,
                                                    split_own=last, small_stage=small_stage if last else None)
        g["g_fpost"] = g_fpost
        if l > 0:
            dx2, dz2, g["g_pre"], g_fpost = _norm_bwd(dx1, (dh, saved[l]["x"], smalls[l]["g_pre"]),
                                                       (saved[l - 1]["z2"], smalls[l - 1]["g_fpost"]), "norm_bwd_between")
        else:
            grad_x, g["g_pre"] = _norm_bwd(dx1, (dh, saved[l]["x"], smalls[l]["g_pre"]), None, "norm_bwd_bottom")
        grads[l] = g
    job, done = small_exchange(late_entries, {(0, n): grads[0][n] for n in ("g_pre", "b_f")})
    done(_run_job(job, "gather_small"))
    big = [{} for _ in range(depth)]
    for l in range(depth):
        for ex in exchanges[l]:
            ex.run()
            big[l].update(ex.grads())
    small_grads = {n: jnp.stack([small_sum[(l, n)] for l in range(depth)]) for n in small_shapes}

    def adam_small(w, g, m, v):
        shp = w.shape
        if w.ndim >= 3 and shp[-1] >= LANE:
            two = (math.prod(shp[:-1]), shp[-1])
        else:
            two = (1, math.prod(shp)) if math.prod(shp) < LANE else (math.prod(shp) // LANE, LANE)
        outs = _adamw(w.reshape(two), g.reshape(two), m.reshape(two), v.reshape(two), "adamw")
        return [g] + [o.reshape(shp) for o in outs]

    def adam_in(w, m, v):
        outs = _adamw_interleaved(in_view(w), big[0]["w_in"], big[1]["w_in"], in_view(m), in_view(v), "adamw_in")
        return [jnp.transpose(o, (1, 2, 0)) for o in outs]

    def adam_gu(k, w, m, v):
        outs = _adamw_layers(gu_view(w), big[0][k], big[1][k], gu_view(m), gu_view(v), "adamw_layers")
        return [jnp.transpose(o, (0, 2, 1)) for o in outs]

    def adam_rows(k, w, m, v):
        return _adamw_layers(w, big[0][k], big[1][k], m, v, "adamw_layers")

    results = [
        adam_small(mix_pre_g, small_grads["g_pre"], m_mix_pre_g, v_mix_pre_g),
        adam_in(w_in, m_w_in, v_w_in),
        adam_small(b_forget, small_grads["b_f"], m_b_forget, v_b_forget),
        adam_small(sgu_norm_g, small_grads["g_v"], m_sgu_norm_g, v_sgu_norm_g),
        adam_small(w_spatial, small_grads["w_s"], m_w_spatial, v_w_spatial),
        adam_small(b_spatial, small_grads["b_s"], m_b_spatial, v_b_spatial),
        adam_rows("w_out", w_out, m_w_out, v_w_out),
        adam_small(mix_post_g, small_grads["g_post"], m_mix_post_g, v_mix_post_g),
        adam_small(ffn_pre_g, small_grads["g_fpre"], m_ffn_pre_g, v_ffn_pre_g),
        adam_gu("w_g", w_gate, m_w_gate, v_w_gate),
        adam_gu("w_u", w_up, m_w_up, v_w_up),
        adam_rows("w_d", w_down, m_w_down, v_w_down),
        adam_small(ffn_post_g, small_grads["g_fpost"], m_ffn_post_g, v_ffn_post_g),
    ]
    gs, deltas, new_ms, new_vs = zip(*results)
    return (loss, grad_x.reshape(x.shape), *gs, *deltas, *new_ms, *new_vs)
```

```python
import functools
import math

import jax
import jax.numpy as jnp
from jax import lax
from jax.experimental import pallas as pl
from jax.experimental.pallas import tpu as pltpu

F32 = jnp.float32
BF16 = jnp.bfloat16

EPS = 1e-6
LANE = 128
SUBLANE = 8
N_CHIPS = 4
N_DEV = 8
VMEM_LIMIT = 48 * 1024 * 1024
MESH = pl.DeviceIdType.MESH

ADAM_LR = 0.001
ADAM_B1 = 0.9
ADAM_B2 = 0.999
ADAM_EPS = 1e-08
ADAM_WD = 0.01
ADAM_STEP = 10
ADAM_C1 = 1.0 / (1.0 - ADAM_B1 ** ADAM_STEP)
ADAM_C2 = 1.0 / (1.0 - ADAM_B2 ** ADAM_STEP)

GELU_K = math.sqrt(2.0 / math.pi)
GELU_A = 0.044715
NEG = -1e30
LOG2E = 1.4426950408889634
LN2 = 0.6931471805599453

COL_U, COL_V, COL_Q, COL_K, COL_VA, COL_GA, COL_GB, COL_F = range(8)


def _cparams(sem=None):
    return pltpu.CompilerParams(dimension_semantics=sem, vmem_limit_bytes=VMEM_LIMIT)


def _tile(n, cap):
    best = None
    for t in range(LANE, min(n, cap) + 1, LANE):
        if n % t == 0:
            best = t
    return best if best is not None else n


def _rows(n, cap):
    best = None
    for t in range(SUBLANE, min(n, cap) + 1, SUBLANE):
        if n % t == 0:
            best = t
    return best if best is not None else n


def _gelu_and_grad(x):
    x2 = x * x
    t = jnp.tanh(GELU_K * (x + GELU_A * x2 * x))
    g = 0.5 * x * (1.0 + t)
    dg = 0.5 * (1.0 + t) + 0.5 * x * (1.0 - t * t) * (GELU_K * (1.0 + 3.0 * GELU_A * x2))
    return g, dg


def _sigmoid(x):
    return 1.0 / (1.0 + jnp.exp(-x))


def _sum8(v):
    n, d = v.shape
    return v.reshape(n // SUBLANE, SUBLANE, d).sum(axis=0)


def _nt_dot(a, b):
    return lax.dot_general(a, b, (((1,), (1,)), ((), ())), preferred_element_type=F32)


_HBM = pl.BlockSpec(memory_space=pl.ANY)


def _place():
    x, y, c = lax.axis_index("x"), lax.axis_index("y"), lax.axis_index("c")
    chips = [(1 - x, y), (x, 1 - y), (1 - x, 1 - y)]
    return x, y, c, chips


class _Job:
    def __init__(self, ins, inout, fresh, nsem, first, mid, last, mid_at=0.5):
        self.ins, self.inout, self.fresh, self.nsem = list(ins), list(inout), list(fresh), nsem
        self.first, self.mid, self.last, self.mid_at = first, mid, last, mid_at


def _call(body, *, grid, in_specs, out_specs, out_shape, scratch_shapes, dims, name, args, aliases=None, job=None):
    single = not isinstance(out_shape, (list, tuple))
    out_specs = [out_specs] if single else list(out_specs)
    out_shape = [out_shape] if single else list(out_shape)
    aliases = dict(aliases or {})
    if job is None:
        outs = pl.pallas_call(body, grid=grid, in_specs=in_specs, out_specs=out_specs, out_shape=out_shape,
                              scratch_shapes=scratch_shapes, input_output_aliases=aliases, compiler_params=_cparams(dims),
                              name=name)(*args)
        return (outs[0] if single else outs), []
    n_in, n_out, n_scr = len(args), len(out_shape), len(scratch_shapes)
    n_ji, n_jio, n_jf = len(job.ins), len(job.inout), len(job.fresh)
    total = math.prod(grid)

    def wrapped(*refs):
        host_in = refs[:n_in]
        pos = n_in
        j_ins = refs[pos:pos + n_ji]
        pos += n_ji + n_jio
        host_out = refs[pos:pos + n_out]
        pos += n_out
        j_inout = refs[pos:pos + n_jio]
        pos += n_jio
        j_fresh = refs[pos:pos + n_jf]
        pos += n_jf
        host_scr = refs[pos:pos + n_scr]
        ssem, rsem = refs[pos + n_scr:]
        flat = 0
        for ax, size in enumerate(grid):
            flat = flat * size + pl.program_id(ax)

        def hook(fn, at):
            if fn is not None:
                @pl.when(flat == at)
                def _():
                    fn(j_ins, j_inout, j_fresh, ssem, rsem)

        hook(job.first, 0)
        body(*host_in, *host_out, *host_scr)
        hook(job.mid, min(int(total * job.mid_at), total - 1))
        hook(job.last, total - 1)

    for k in range(n_jio):
        aliases[n_in + n_ji + k] = n_out + k
    outs = pl.pallas_call(
        wrapped, grid=grid,
        in_specs=list(in_specs) + [_HBM] * (n_ji + n_jio),
        out_specs=out_specs + [_HBM] * (n_jio + n_jf),
        out_shape=out_shape + [jax.ShapeDtypeStruct(b.shape, b.dtype) for b in job.inout] + list(job.fresh),
        scratch_shapes=list(scratch_shapes) + [pltpu.SemaphoreType.DMA((job.nsem,)), pltpu.SemaphoreType.DMA((job.nsem,))],
        input_output_aliases=aliases, compiler_params=_cparams(tuple("arbitrary" for _ in grid)), name=name,
    )(*args, *job.ins, *job.inout)
    host = outs[:n_out]
    return (host[0] if single else host), outs[n_out:]


def _run_job(job, name):
    n_ji, n_jio, n_jf = len(job.ins), len(job.inout), len(job.fresh)

    def body(*refs):
        j_ins = refs[:n_ji]
        pos = n_ji + n_jio
        j_inout = refs[pos:pos + n_jio]
        j_fresh = refs[pos + n_jio:pos + n_jio + n_jf]
        ssem, rsem = refs[pos + n_jio + n_jf:]
        for fn in (job.first, job.mid, job.last):
            if fn is not None:
                fn(j_ins, j_inout, j_fresh, ssem, rsem)

    return pl.pallas_call(
        body, in_specs=[_HBM] * (n_ji + n_jio), out_specs=[_HBM] * (n_jio + n_jf),
        out_shape=[jax.ShapeDtypeStruct(b.shape, b.dtype) for b in job.inout] + list(job.fresh),
        scratch_shapes=[pltpu.SemaphoreType.DMA((job.nsem,)), pltpu.SemaphoreType.DMA((job.nsem,))],
        input_output_aliases={n_ji + k: k for k in range(n_jio)}, name=name,
    )(*job.ins, *job.inout)


_DIMS ={"nn": ((1,), (0,)), "nt": ((1,), (1,)), "tn": ((0,), (0,))}


def _matmul(a, b, mode, out_dtype, name, n=None, slab=None, into=None, job=None, norms=None, silu_of=None, tm_cap=512,
            tn_cap=2048, tk_cap=1408):
    if mode == "nn":
        (m, k), (k2, nn_) = a.shape, b.shape
    elif mode == "nt":
        (m, k), (nn_, k2) = a.shape, b.shape
    else:
        (k, m), (k2, nn_) = a.shape, b.shape
    n = nn_ if n is None else n
    assert k == k2, (a.shape, b.shape, mode)
    tm, tn, tk = _tile(m, tm_cap), _tile(n, tn_cap), _tile(k, tk_cap)
    if slab is not None and slab[2]:
        tm = _tile(math.gcd(m, slab[2]), tm_cap)
    nk = k // tk
    if mode == "tn":
        a_spec = pl.BlockSpec((tk, tm), lambda j, i, kk, *_: (kk, i))
    else:
        a_spec = pl.BlockSpec((tm, tk), lambda j, i, kk, *_: (i, kk))
    if mode == "nt":
        b_spec = pl.BlockSpec((tn, tk), lambda j, i, kk, *_: (j, kk))
    else:
        b_spec = pl.BlockSpec((tk, tn), lambda j, i, kk, *_: (kk, j))
    dims = (_DIMS[mode], ((), ()))
    aliased = into is not None

    n_a = 1 if silu_of is None else 2
    n_in = n_a + 1 + aliased + (3 if norms is not None else 0)
    n_main = 1 + (2 if norms is not None else 0)

    def finish(refs, z):
        refs[n_in][...] = z.astype(out_dtype).reshape(refs[n_in].shape)
        if norms is not None:
            x_ref, gp_ref, gn_ref = refs[n_in - 3:n_in]
            r = lax.rsqrt(jnp.mean(z * z, axis=-1, keepdims=True) + EPS)
            xn = x_ref[...] + z * r * gp_ref[...]
            refs[n_in + 1][...] = xn
            r2 = lax.rsqrt(jnp.mean(xn * xn, axis=-1, keepdims=True) + EPS)
            refs[n_in + 2][...] = (xn * r2 * gn_ref[...]).astype(BF16)

    def body(*refs):
        lhs = refs[0][...]
        if silu_of is not None:
            gv = refs[1][...].astype(F32)
            lhs = (gv * _sigmoid(gv) * lhs.astype(F32)).astype(BF16)
            refs[n_in + n_main][...] = lhs
        p = lax.dot_general(lhs, refs[n_a][...], dims, preferred_element_type=F32)
        if nk == 1:
            finish(refs, p)
        else:
            acc = refs[-1]
            kk = pl.program_id(2)

            @pl.when(kk == 0)
            def _():
                acc[...] = p

            @pl.when(kk > 0)
            def _():
                acc[...] += p

            @pl.when(kk == nk - 1)
            def _():
                finish(refs, acc[...])

    if slab is None:
        out_spec = pl.BlockSpec((tm, tn), lambda j, i, kk: (i, j))
        out_shape = jax.ShapeDtypeStruct((m, n), out_dtype)
    else:
        shape3, lead, row0 = slab
        assert row0 % tm == 0 and shape3[2] == n
        out_spec = pl.BlockSpec((1, tm, tn), lambda j, i, kk: (lead, row0 // tm + i, j))
        out_shape = jax.ShapeDtypeStruct(shape3, out_dtype)
    in_specs, args = [a_spec, b_spec], [a, b]
    if silu_of is not None:
        assert mode == "nn" and tn == n and slab is None, "the left operand's blocks are written once each"
        in_specs, args = [a_spec, a_spec, b_spec], [a, silu_of, b]
    if aliased:
        in_specs.append(pl.BlockSpec(memory_space=pl.ANY))
        args.append(into)
    if norms is not None:
        assert tn == n and slab is None, "the fused norms need whole rows"
        row = pl.BlockSpec((tm, n), lambda j, i, kk: (i, 0))
        vec = pl.BlockSpec((1, n), lambda j, i, kk: (0, 0))
        in_specs += [row, vec, vec]
        args += list(norms)
        out_spec = [out_spec, row, row]
        out_shape = [out_shape, jax.ShapeDtypeStruct((m, n), F32), jax.ShapeDtypeStruct((m, n), BF16)]
    if silu_of is not None:
        out_spec = (out_spec if isinstance(out_spec, list) else [out_spec]) + [pl.BlockSpec((tm, tk), lambda j, i, kk: (i, kk))]
        out_shape = (out_shape if isinstance(out_shape, list) else [out_shape]) + [jax.ShapeDtypeStruct((m, k), BF16)]
    out, moved = _call(
        body, grid=(n // tn, m // tm, nk), in_specs=in_specs, out_specs=out_spec, out_shape=out_shape,
        scratch_shapes=[pltpu.VMEM((tm, tn), F32)] if nk > 1 else [], dims=("parallel", "parallel", "arbitrary"), name=name,
        args=args, aliases={n_a + 1: 0} if aliased else None, job=job)
    return out if job is None else (out, moved)


def _matmul_rows(pieces, b, rows, name, job=None, tm=1024, tk=1024):
    k, n = b.shape
    tm = math.gcd(tm, *[a.shape[1] for a in pieces])
    tk = _tile(k, tk)
    nk = k // tk
    spans, r0 = [], 0
    for a in pieces:
        assert a.shape[0] == k and a.shape[1] % tm == 0
        spans.append((r0, a.shape[1] // tm))
        r0 += a.shape[1] // tm
    nr = r0
    np_ = len(pieces)

    def body(*refs):
        b_ref, o_ref, acc = refs[np_], refs[np_ + 1], refs[-1]
        r, kk = pl.program_id(0), pl.program_id(1)
        for p, (first, count) in enumerate(spans):
            @pl.when((r >= first) & (r < first + count))
            def _(p=p):
                part = lax.dot_general(refs[p][...], b_ref[...], (_DIMS["tn"], ((), ())), preferred_element_type=F32)

                @pl.when(kk == 0)
                def _():
                    acc[...] = part

                @pl.when(kk > 0)
                def _():
                    acc[...] += part

        @pl.when(kk == nk - 1)
        def _():
            o_ref[0] = acc[...].astype(BF16)

    in_specs = []
    for first, count in spans:
        in_specs.append(pl.BlockSpec((tk, tm), lambda r, kk, f=first, c=count: (
            jnp.where(r < f, 0, jnp.where(r >= f + c, nk - 1, kk)), jnp.clip(r - f, 0, c - 1))))
    in_specs.append(pl.BlockSpec((tk, n), lambda r, kk: (kk, 0)))
    out, moved = _call(
        body, grid=(nr, nk), in_specs=in_specs, out_specs=pl.BlockSpec((1, tm, n), lambda r, kk: (0, r, 0)),
        out_shape=jax.ShapeDtypeStruct((1, rows, n), BF16), scratch_shapes=[pltpu.VMEM((tm, n), F32)],
        dims=("arbitrary", "arbitrary"), name=name, args=list(pieces) + [b], job=job)
    return out if job is None else (out, moved)


def _matmul_pieces(pieces, addend, name, tk, job=None, tm_cap=512):
    m = pieces[0][0].shape[0]
    n = pieces[0][1].shape[1]
    tm = _tile(m, tm_cap)
    spans, s0 = [], 0
    for a, b, row0 in pieces:
        assert a.shape[1] % tk == 0 and row0 % tk == 0 and b.shape[1] == n and a.shape[0] == m
        spans.append((s0, a.shape[1] // tk, row0 // tk))
        s0 += a.shape[1] // tk
    steps = s0
    np_ = len(pieces)
    groups = []
    for (a, b, _), (first, count, brow) in zip(pieces, spans):
        if groups and groups[-1][0] is b and groups[-1][3] + groups[-1][2] == brow:
            groups[-1][2] += count
        else:
            groups.append([b, first, count, brow])
    b_of = []
    for first, count, _ in spans:
        b_of.append(next(k for k, g in enumerate(groups) if g[1] <= first < g[1] + g[2]))
    ng = len(groups)

    nm = m // tm

    def body(*refs):
        o_ref, acc = refs[-2], refs[-1]
        s, i = pl.program_id(0), pl.program_id(1)
        rows = pl.ds(pl.multiple_of(i * tm, tm), tm)

        @pl.when(s == 0)
        def _():
            acc[rows, :] = refs[np_ + ng][...] if addend is not None else jnp.zeros((tm, n), F32)

        for p, (first, count, _) in enumerate(spans):
            @pl.when((s >= first) & (s < first + count))
            def _(p=p):
                acc[rows, :] += jnp.dot(refs[p][...], refs[np_ + b_of[p]][...], preferred_element_type=F32)

        @pl.when(s == steps - 1)
        def _():
            o_ref[...] = acc[rows, :]

    in_specs, args = [], []
    for (a, _, _), (first, count, _) in zip(pieces, spans):
        in_specs.append(pl.BlockSpec((tm, tk), lambda s, i, f=first, c=count: (
            jnp.where(s < f, 0, jnp.where(s >= f + c, nm - 1, i)), jnp.clip(s - f, 0, c - 1))))
        args.append(a)
    for b, first, count, brow in groups:
        in_specs.append(pl.BlockSpec((tk, n), lambda s, i, f=first, c=count, r=brow: (r + jnp.clip(s - f, 0, c - 1), 0)))
        args.append(b)
    if addend is not None:
        in_specs.append(pl.BlockSpec((tm, n), lambda s, i: (jnp.where(s == 0, i, nm - 1), 0)))
        args.append(addend)
    out, moved = _call(
        body, grid=(steps, nm), in_specs=in_specs,
        out_specs=pl.BlockSpec((tm, n), lambda s, i: (jnp.where(s == steps - 1, i, 0), 0)),
        out_shape=jax.ShapeDtypeStruct((m, n), F32), scratch_shapes=[pltpu.VMEM((m, n), F32)],
        dims=("arbitrary", "arbitrary"), name=name, args=args, job=job)
    return out if job is None else (out, moved)


def _norm_fwd(x, z, g_post, g_next, name):
    t, d = x.shape
    tt = _rows(t, 512)
    row = pl.BlockSpec((tt, d), lambda i: (i, 0))
    vec = pl.BlockSpec((1, d), lambda i: (0, 0))

    def body(*refs):
        if z is None:
            x_ref, gn_ref, h_ref = refs
            xn = x_ref[...]
        else:
            x_ref, z_ref, gp_ref, gn_ref, xo_ref, h_ref = refs
            zz = z_ref[...]
            r = lax.rsqrt(jnp.mean(zz * zz, axis=-1, keepdims=True) + EPS)
            xn = x_ref[...] + zz * r * gp_ref[...]
            xo_ref[...] = xn
        r2 = lax.rsqrt(jnp.mean(xn * xn, axis=-1, keepdims=True) + EPS)
        h_ref[...] = (xn * r2 * gn_ref[...]).astype(BF16)

    if z is None:
        return pl.pallas_call(
            body, grid=(t // tt,), in_specs=[row, vec], out_specs=row,
            out_shape=jax.ShapeDtypeStruct((t, d), BF16), compiler_params=_cparams(("parallel",)), name=name,
        )(x, g_next)
    return pl.pallas_call(
        body, grid=(t // tt,), in_specs=[row, row, vec, vec], out_specs=[row, row],
        out_shape=[jax.ShapeDtypeStruct((t, d), F32), jax.ShapeDtypeStruct((t, d), BF16)],
        compiler_params=_cparams(("parallel",)), name=name,
    )(x, z, g_post, g_next)


def _rms_bwd(dy, x, g):
    r = lax.rsqrt(jnp.mean(x * x, axis=-1, keepdims=True) + EPS)
    n = x * r
    dn = dy * g
    dx = r * (dn - n * jnp.mean(dn * n, axis=-1, keepdims=True))
    return dx, dy * n


def _norm_bwd(dres, pre, post, name):
    t, d = dres.shape
    tt = _rows(t, 512)
    nt = t // tt
    row = pl.BlockSpec((tt, d), lambda i: (i, 0))
    vec = pl.BlockSpec((1, d), lambda i: (0, 0))
    has_pre, has_post = pre is not None, post is not None
    n_in = 1 + (3 if has_pre else 0) + (2 if has_post else 0)
    n_out = has_pre + has_post + has_pre + has_post

    def body(*refs):
        ins, outs, scr = refs[:n_in], refs[n_in:n_in + n_out], refs[n_in + n_out:]
        i = pl.program_id(0)
        dx = ins[0][...]
        pos, opos, spos = 1, 0, 0
        accs = []
        if has_pre:
            dh_ref, xa_ref, ga_ref = ins[pos:pos + 3]
            pos += 3
            dxa, dga_t = _rms_bwd(dh_ref[...], xa_ref[...], ga_ref[...])
            dx = dx + dxa
            outs[opos][...] = dx
            opos += 1
            accs.append((scr[spos], dga_t))
            spos += 1
        if has_post:
            zb_ref, gb_ref = ins[pos:pos + 2]
            dz, dgb_t = _rms_bwd(dx, zb_ref[...], gb_ref[...])
            outs[opos][...] = dz.astype(BF16)
            opos += 1
            accs.append((scr[spos], dgb_t))
            spos += 1
        for (acc, val), out in zip(accs, outs[opos:]):
            part = _sum8(val)

            @pl.when(i == 0)
            def _(acc=acc, part=part):
                acc[...] = part

            @pl.when(i > 0)
            def _(acc=acc, part=part):
                acc[...] += part

            @pl.when(i == nt - 1)
            def _(acc=acc, out=out):
                out[...] = jnp.sum(acc[...], axis=0, keepdims=True)

    in_specs, args = [row], [dres]
    out_specs, out_shape = [], []
    if has_pre:
        in_specs += [row, row, vec]
        args += list(pre)
        out_specs.append(row)
        out_shape.append(jax.ShapeDtypeStruct((t, d), F32))
    if has_post:
        in_specs += [row, vec]
        args += list(post)
        out_specs.append(row)
        out_shape.append(jax.ShapeDtypeStruct((t, d), BF16))
    for _ in range(has_pre + has_post):
        out_specs.append(vec)
        out_shape.append(jax.ShapeDtypeStruct((1, d), F32))
    return pl.pallas_call(
        body, grid=(nt,), in_specs=in_specs, out_specs=out_specs, out_shape=out_shape,
        scratch_shapes=[pltpu.VMEM((SUBLANE, d), F32)] * (has_pre + has_post),
        compiler_params=_cparams(("arbitrary",)), name=name,
    )(*args)


def _loss_grad(y, target, name):
    t, d = y.shape
    tt = _rows(t, 512)
    nt = t // tt
    row = pl.BlockSpec((tt, d), lambda i: (i, 0))
    inv_d = 1.0 / d

    def body(y_ref, t_ref, dy_ref, l_ref):
        i = pl.program_id(0)
        diff = y_ref[...] - t_ref[...]
        dy_ref[...] = diff * inv_d
        s8 = _sum8(diff * diff)
        part = s8[:, 0:LANE]
        for k in range(1, d // LANE):
            part = part + s8[:, k * LANE:(k + 1) * LANE]
        part = part * (0.5 * inv_d)

        @pl.when(i == 0)
        def _():
            l_ref[...] = part

        @pl.when(i > 0)
        def _():
            l_ref[...] += part

    return pl.pallas_call(
        body, grid=(nt,), in_specs=[row, row],
        out_specs=[row, pl.BlockSpec((SUBLANE, LANE), lambda i: (0, 0))],
        out_shape=[jax.ShapeDtypeStruct((t, d), F32), jax.ShapeDtypeStruct((SUBLANE, LANE), F32)],
        compiler_params=_cparams(("arbitrary",)), name=name,
    )(y, target)


def _swiglu_bwd(a, b, dm, name):
    t, f = a.shape
    tt = _rows(t, 256)
    blk = pl.BlockSpec((tt, f), lambda i: (i, 0))

    def body(a_ref, b_ref, dm_ref, da_ref, db_ref):
        av = a_ref[...].astype(F32)
        s = _sigmoid(av)
        dv = dm_ref[...].astype(F32)
        da_ref[...] = (dv * b_ref[...].astype(F32) * s * (1.0 + av * (1.0 - s))).astype(BF16)
        db_ref[...] = (dv * av * s).astype(BF16)

    return pl.pallas_call(
        body, grid=(t // tt,), in_specs=[blk, blk, blk], out_specs=[blk, blk],
        out_shape=[jax.ShapeDtypeStruct((t, f), BF16)] * 2, compiler_params=_cparams(("parallel",)), name=name,
    )(a, b, dm)


def _log_sigmoid(x):
    return jnp.minimum(x, 0.0) - jnp.log1p(jnp.exp(-jnp.abs(x)))


def _fox_prep(f_t, b_f, name):
    h, t = f_t.shape

    def body(f_ref, b_ref, c_ref):
        r = lax.broadcasted_iota(jnp.int32, (LANE, LANE), 0)
        c = lax.broadcasted_iota(jnp.int32, (LANE, LANE), 1)
        upper = (r <= c).astype(F32)
        carry = jnp.zeros((h, 1), F32)
        for j in range(t // LANE):
            sl = slice(j * LANE, (j + 1) * LANE)
            lf = _log_sigmoid(f_ref[:, sl] + b_ref[...])
            cs = jnp.dot(lf, upper, precision=lax.Precision.HIGHEST, preferred_element_type=F32) + carry
            c_ref[:, sl] = cs
            carry = cs[:, LANE - 1:LANE]

    return pl.pallas_call(body, out_shape=jax.ShapeDtypeStruct((h, t), F32), compiler_params=_cparams(), name=name)(f_t, b_f)


def _fox_bwd(dc_q, dc_k, f_t, b_f, name):
    h, t = f_t.shape

    def body(dq_ref, dk_ref, f_ref, b_ref, df_ref, db_ref):
        r = lax.broadcasted_iota(jnp.int32, (LANE, LANE), 0)
        c = lax.broadcasted_iota(jnp.int32, (LANE, LANE), 1)
        lower = (r >= c).astype(F32)
        carry = jnp.zeros((h, 1), F32)
        dbsum = jnp.zeros((h, 1), F32)
        for j in reversed(range(t // LANE)):
            sl = slice(j * LANE, (j + 1) * LANE)
            dc = dq_ref[:, sl] - dk_ref[:, sl]
            dl = jnp.dot(dc, lower, precision=lax.Precision.HIGHEST, preferred_element_type=F32) + carry
            carry = dl[:, 0:1]
            df = dl * _sigmoid(-(f_ref[:, sl] + b_ref[...]))
            df_ref[:, sl] = df
            dbsum = dbsum + jnp.sum(df, axis=-1, keepdims=True)
        db_ref[...] = dbsum

    return pl.pallas_call(
        body, out_shape=[jax.ShapeDtypeStruct((h, t), F32), jax.ShapeDtypeStruct((h, 1), F32)],
        compiler_params=_cparams(), name=name,
    )(dc_q, dc_k, f_t, b_f)


ATTN_FWD = (1024, 512)
ATTN_BWD = (512, 512)


def _attn_tiles(t, tiles):
    return _tile(t, tiles[0]), _tile(t, tiles[1])


def _attn_fwd(proj, c_t, d, name, job=None):
    t = proj.shape[0]
    h = d // LANE
    bq, bk = _attn_tiles(t, ATTN_FWD)
    nq, nk, rr = t // bq, t // bk, bq // bk
    qc, kc, vc = COL_Q * h, COL_K * h, COL_VA * h
    qscale = LANE ** -0.5 * LOG2E

    def body(q_ref, k_ref, v_ref, cc_ref, cr_ref, o_ref, lse_ref, kb, vt, ckb, acc):
        i = pl.program_id(1)

        @pl.when(i == 0)
        def _():
            kb[...] = k_ref[...].astype(BF16)
            ckb[...] = jnp.broadcast_to(cc_ref[0] * LOG2E, (t, bq))
            for jn in range(nk):
                vt[jn] = v_ref[jn * bk:(jn + 1) * bk, :].astype(F32).T.astype(BF16)

        q = (q_ref[...].astype(F32) * qscale).astype(BF16)
        cq = cr_ref[0, 0] * LOG2E
        acc[...] = jnp.zeros((LANE, bq), F32)

        def block(j, diag, m_old, l_old):
            off = 0 if diag is None else diag * bk
            w = bq - off
            rows = pl.ds(pl.multiple_of(j * bk, bk), bk)
            s = _nt_dot(kb[rows, :], q[off:, :]) - ckb[rows, off:]
            if diag is not None:
                kk = lax.broadcasted_iota(jnp.int32, (bk, w), 0)
                qq = lax.broadcasted_iota(jnp.int32, (bk, w), 1)
                s = jnp.where(qq >= kk, s, NEG)
            cqs, m_part, l_part = cq[:, off:], m_old[:, off:], l_old[:, off:]
            m_new, l_new, alpha, p = [], [], [], []
            for c0 in range(0, w, LANE):
                sl = slice(c0, c0 + LANE)
                m_s = jnp.maximum(m_part[:, sl], jnp.max(s[:, sl], axis=0, keepdims=True) + cqs[:, sl])
                p_s = jnp.exp2(s[:, sl] + (cqs[:, sl] - m_s))
                a_s = jnp.exp2(m_part[:, sl] - m_s)
                l_new.append(a_s * l_part[:, sl] + jnp.sum(p_s, axis=0, keepdims=True))
                m_new.append(m_s)
                alpha.append(a_s)
                p.append(p_s.astype(BF16))
            m_new, l_new, alpha = (jnp.concatenate(v, axis=1) for v in (m_new, l_new, alpha))
            acc[:, off:] = alpha * acc[:, off:] + jnp.dot(vt[j], jnp.concatenate(p, axis=1), preferred_element_type=F32)
            if off:
                m_new = jnp.concatenate([m_old[:, :off], m_new], axis=1)
                l_new = jnp.concatenate([l_old[:, :off], l_new], axis=1)
            return m_new, l_new

        m, l = lax.fori_loop(0, i * rr, lambda j, c: block(j, None, *c),
                             (jnp.full((1, bq), NEG, F32), jnp.zeros((1, bq), F32)))
        for jj in range(rr):
            m, l = block(i * rr + jj, jj, m, l)
        o_ref[...] = (acc[...] / l).T
        lse_ref[0, 0] = m + jnp.log2(l)

    rowq = pl.BlockSpec((1, 1, 1, bq), lambda hh, i: (hh, i, 0, 0))
    outs, moved = _call(
        body, grid=(h, nq),
        in_specs=[
            pl.BlockSpec((bq, LANE), lambda hh, i: (i, qc + hh)),
            pl.BlockSpec((t, LANE), lambda hh, i: (0, kc + hh)),
            pl.BlockSpec((t, LANE), lambda hh, i: (0, vc + hh)),
            pl.BlockSpec((1, t, 1), lambda hh, i: (hh, 0, 0)),
            rowq,
        ],
        out_specs=[pl.BlockSpec((bq, LANE), lambda hh, i: (i, hh)), rowq],
        out_shape=[jax.ShapeDtypeStruct((t, d), F32), jax.ShapeDtypeStruct((h, nq, 1, bq), F32)],
        scratch_shapes=[pltpu.VMEM((t, LANE), BF16), pltpu.VMEM((nk, LANE, bk), BF16), pltpu.VMEM((t, bq), F32),
                        pltpu.VMEM((LANE, bq), F32)],
        dims=("arbitrary", "arbitrary"), name=name,
        args=[proj, proj, proj, c_t.reshape(h, t, 1), c_t.reshape(h, nq, 1, bq)], job=job)
    outs = [outs[0], outs[1].reshape(h, t)]
    return outs if job is None else (outs, moved)


def _attn_bwd(proj, do, o, lse, c_t, d, name, job=None):
    t = proj.shape[0]
    h = d // LANE
    bq, bk = _attn_tiles(t, ATTN_BWD)
    nq, nk, rr = t // bq, t // bk, bq // bk
    qc, kc, vc = COL_Q * h, COL_K * h, COL_VA * h
    scale = LANE ** -0.5

    def body(q_ref, k_ref, v_ref, do_ref, o_ref, lse_ref, cc_ref, cr_ref, dq_ref, dk_ref, dv_ref, dcq_ref, dck_ref,
             kb, kt, vb, ckb, dk_acc, dv_acc, dck_acc, dqt_acc):
        i = pl.program_id(1)

        @pl.when(i == 0)
        def _():
            kb[...] = k_ref[...].astype(BF16)
            vb[...] = v_ref[...].astype(BF16)
            ckb[...] = jnp.broadcast_to(cc_ref[0] * LOG2E, (t, bq))
            for jn in range(nk):
                kt[jn] = k_ref[jn * bk:(jn + 1) * bk, :].astype(F32).T.astype(BF16)
            dk_acc[...] = jnp.zeros((t, LANE), F32)
            dv_acc[...] = jnp.zeros((t, LANE), F32)
            dck_acc[...] = jnp.zeros((t, LANE), F32)

        q = (q_ref[...].astype(F32) * (scale * LOG2E)).astype(BF16)
        dof = do_ref[...]
        dob = dof.astype(BF16)
        delta = jnp.sum((dof * o_ref[...]).T, axis=0, keepdims=True)
        rowb = cr_ref[0, 0] * LOG2E - lse_ref[0, 0]
        dqt_acc[...] = jnp.zeros((LANE, bq), F32)

        def block(j, diag, dcq):
            rows = pl.ds(pl.multiple_of(j * bk, bk), bk)
            p = jnp.exp2(_nt_dot(kb[rows, :], q) - ckb[rows, :] + rowb)
            if diag is not None:
                kk = lax.broadcasted_iota(jnp.int32, (bk, bq), 0)
                qq = lax.broadcasted_iota(jnp.int32, (bk, bq), 1)
                p = jnp.where(qq >= kk + diag * bk, p, 0.0)
            dv_acc[rows, :] += jnp.dot(p.astype(BF16), dob, preferred_element_type=F32)
            ds = p * (_nt_dot(vb[rows, :], dob) - delta)
            dsb = ds.astype(BF16)
            dk_acc[rows, :] += jnp.dot(dsb, q, preferred_element_type=F32)
            dqt_acc[...] += jnp.dot(kt[j], dsb, preferred_element_type=F32)
            part = ds[:, 0:LANE]
            for k in range(1, bq // LANE):
                part = part + ds[:, k * LANE:(k + 1) * LANE]
            dck_acc[rows, :] += part
            return dcq + jnp.sum(ds, axis=0, keepdims=True)

        dcq = lax.fori_loop(0, i * rr, lambda j, c: block(j, None, c), jnp.zeros((1, bq), F32))
        for jj in range(rr):
            dcq = block(i * rr + jj, jj, dcq)
        dq_ref[...] = (dqt_acc[...] * scale).T.astype(BF16)
        dcq_ref[0, 0] = dcq

        @pl.when(i == nq - 1)
        def _():
            dk_ref[...] = (dk_acc[...] * LN2).astype(BF16)
            dv_ref[...] = dv_acc[...].astype(BF16)
            dck_ref[0] = jnp.sum(dck_acc[...], axis=-1, keepdims=True)

    rowq = pl.BlockSpec((1, 1, 1, bq), lambda hh, i: (hh, i, 0, 0))
    blk = pl.BlockSpec((bq, LANE), lambda hh, i: (i, hh))
    whole = pl.BlockSpec((t, LANE), lambda hh, i: (0, hh))
    colk = pl.BlockSpec((1, t, 1), lambda hh, i: (hh, 0, 0))
    outs, moved = _call(
        body, grid=(h, nq),
        in_specs=[
            pl.BlockSpec((bq, LANE), lambda hh, i: (i, qc + hh)),
            pl.BlockSpec((t, LANE), lambda hh, i: (0, kc + hh)),
            pl.BlockSpec((t, LANE), lambda hh, i: (0, vc + hh)),
            blk, blk, rowq, colk, rowq,
        ],
        out_specs=[blk, whole, whole, rowq, colk],
        out_shape=[jax.ShapeDtypeStruct((t, d), BF16), jax.ShapeDtypeStruct((t, d), BF16), jax.ShapeDtypeStruct((t, d), BF16),
                   jax.ShapeDtypeStruct((h, nq, 1, bq), F32), jax.ShapeDtypeStruct((h, t, 1), F32)],
        scratch_shapes=[pltpu.VMEM((t, LANE), BF16), pltpu.VMEM((nk, LANE, bk), BF16), pltpu.VMEM((t, LANE), BF16),
                        pltpu.VMEM((t, bq), F32), pltpu.VMEM((t, LANE), F32), pltpu.VMEM((t, LANE), F32),
                        pltpu.VMEM((t, LANE), F32), pltpu.VMEM((LANE, bq), F32)],
        dims=("arbitrary", "arbitrary"), name=name,
        args=[proj, proj, proj, do, o, lse.reshape(h, nq, 1, bq), c_t.reshape(h, t, 1), c_t.reshape(h, nq, 1, bq)], job=job)
    outs = list(outs[:3]) + [outs[3].reshape(h, t), outs[4].reshape(h, t)]
    return outs if job is None else (outs, moved)


def _sgu_forward(u_ref, v_ref, gv_ref, wm_ref, bs_ref, mix_sc, groups):
    gu, dgu = _gelu_and_grad(u_ref[...].astype(F32))
    gvv, dgv = _gelu_and_grad(v_ref[...].astype(F32))
    mu = jnp.mean(gvv, axis=-1, keepdims=True)
    xc = gvv - mu
    r = lax.rsqrt(jnp.mean(xc * xc, axis=-1, keepdims=True) + EPS)
    nhat = xc * r
    vn = (nhat * gv_ref[...]).astype(BF16)
    for g in range(groups):
        sl = slice(g * LANE, (g + 1) * LANE)
        mix_sc[:, sl] = jnp.dot(wm_ref[g], vn[:, sl], preferred_element_type=F32) + bs_ref[g]
    return gu, dgu, dgv, nhat, r, vn, mix_sc[...]


def _mix_fwd(proj, o, wm, bs, g_v, d, name):
    t = proj.shape[0]
    groups = d // LANE

    def body(u_ref, v_ref, ga_ref, gb_ref, o_ref, wm_ref, bs_ref, gv_ref, out_ref, mix_sc):
        gu, _, _, _, _, _, mixed = _sgu_forward(u_ref, v_ref, gv_ref, wm_ref, bs_ref, mix_sc, groups)
        out_ref[...] = (_sigmoid(ga_ref[...].astype(F32)) * (gu * mixed) + _sigmoid(gb_ref[...].astype(F32)) * o_ref[...]).astype(BF16)

    def colblk(k):
        return pl.BlockSpec((LANE, d), lambda i, k=k: (i, k))

    full3 = pl.BlockSpec((groups, LANE, LANE), lambda i: (0, 0, 0))
    return pl.pallas_call(
        body, grid=(t // LANE,),
        in_specs=[colblk(COL_U), colblk(COL_V), colblk(COL_GA), colblk(COL_GB), colblk(0), full3,
                  pl.BlockSpec((groups, LANE, 1), lambda i: (0, 0, 0)), pl.BlockSpec((1, d), lambda i: (0, 0))],
        out_specs=colblk(0),
        out_shape=jax.ShapeDtypeStruct((t, d), BF16),
        scratch_shapes=[pltpu.VMEM((LANE, d), F32)],
        compiler_params=_cparams(("parallel",)), name=name,
    )(proj, proj, proj, proj, o, wm, bs, g_v)


def _mix_bwd(dmerged, proj, o, wm, wm_t, bs, g_v, d, name, job=None):
    t = proj.shape[0]
    groups = d // LANE
    nt = t // LANE

    def body(dm_ref, u_ref, v_ref, ga_ref, gb_ref, o_ref, wm_ref, wmt_ref, bs_ref, gv_ref,
             duv_ref, dg_ref, do_ref, dws_ref, dbs_ref, dgv_ref, mix_sc, dvn_sc, gv_acc):
        i = pl.program_id(0)

        @pl.when(i == 0)
        def _():
            dws_ref[...] = jnp.zeros_like(dws_ref)
            dbs_ref[...] = jnp.zeros_like(dbs_ref)
            gv_acc[...] = jnp.zeros_like(gv_acc)

        gu, dgu, dgv, nhat, r, vn, mixed = _sgu_forward(u_ref, v_ref, gv_ref, wm_ref, bs_ref, mix_sc, groups)
        dm = dm_ref[...]
        sa = _sigmoid(ga_ref[...].astype(F32))
        sb = _sigmoid(gb_ref[...].astype(F32))
        ov = o_ref[...]
        y_a = gu * mixed
        dg_ref[:, 0:d] = (dm * y_a * sa * (1.0 - sa)).astype(BF16)
        dg_ref[:, d:2 * d] = (dm * ov * sb * (1.0 - sb)).astype(BF16)
        do_ref[...] = dm * sb
        dy_a = dm * sa
        duv_ref[:, 0:d] = (dy_a * mixed * dgu).astype(BF16)
        dmixed = dy_a * gu
        dmixed_b = dmixed.astype(BF16)
        for g in range(groups):
            sl = slice(g * LANE, (g + 1) * LANE)
            dvn_sc[:, sl] = jnp.dot(wmt_ref[g], dmixed_b[:, sl], preferred_element_type=F32)
            dws_ref[g] += _nt_dot(dmixed_b[:, sl], vn[:, sl])
            dbs_ref[g] += jnp.sum(dmixed[:, sl], axis=-1, keepdims=True)
        dvn = dvn_sc[...]
        gv_acc[...] += _sum8(dvn * nhat)
        dn = dvn * gv_ref[...]
        dgelu = r * (dn - jnp.mean(dn, axis=-1, keepdims=True) - nhat * jnp.mean(dn * nhat, axis=-1, keepdims=True))
        duv_ref[:, d:2 * d] = (dgelu * dgv).astype(BF16)

        @pl.when(i == nt - 1)
        def _():
            dgv_ref[...] = jnp.sum(gv_acc[...], axis=0, keepdims=True)
            rr = lax.broadcasted_iota(jnp.int32, (LANE, LANE), 0)
            cl = lax.broadcasted_iota(jnp.int32, (LANE, LANE), 1)
            for g in range(groups):
                dws_ref[g] = jnp.where(rr >= cl, dws_ref[g], 0.0)

    def colblk(k):
        return pl.BlockSpec((LANE, d), lambda i, k=k: (i, k))

    full3 = pl.BlockSpec((groups, LANE, LANE), lambda i: (0, 0, 0))
    col3 = pl.BlockSpec((groups, LANE, 1), lambda i: (0, 0, 0))
    vec = pl.BlockSpec((1, d), lambda i: (0, 0))
    two = pl.BlockSpec((LANE, 2 * d), lambda i: (i, 0))
    outs, moved = _call(
        body, grid=(nt,),
        in_specs=[colblk(0), colblk(COL_U), colblk(COL_V), colblk(COL_GA), colblk(COL_GB), colblk(0), full3, full3, col3, vec],
        out_specs=[two, two, colblk(0), full3, col3, vec],
        out_shape=[jax.ShapeDtypeStruct((t, 2 * d), BF16), jax.ShapeDtypeStruct((t, 2 * d), BF16), jax.ShapeDtypeStruct((t, d), F32),
                   jax.ShapeDtypeStruct((groups, LANE, LANE), F32), jax.ShapeDtypeStruct((groups, LANE, 1), F32),
                   jax.ShapeDtypeStruct((1, d), F32)],
        scratch_shapes=[pltpu.VMEM((LANE, d), F32), pltpu.VMEM((LANE, d), F32), pltpu.VMEM((SUBLANE, d), F32)],
        dims=("arbitrary",), name=name, args=[dmerged, proj, proj, proj, proj, o, wm, wm_t, bs, g_v], job=job)
    return outs if job is None else (outs, moved)


def _adam_math(w, g, m, v):
    nm = ADAM_B1 * m + (1.0 - ADAM_B1) * g
    nv = ADAM_B2 * v + (1.0 - ADAM_B2) * (g * g)
    delta = -ADAM_LR * ((nm * ADAM_C1) / (jnp.sqrt(nv * ADAM_C2) + ADAM_EPS) + ADAM_WD * w)
    return delta, nm, nv


def _adamw(w, g, m, v, name):
    r, c = w.shape
    cap = max(SUBLANE, (2 * 1024 * 1024) // (4 * c) // SUBLANE * SUBLANE)
    tr = _rows(r, cap)

    def body(w_ref, g_ref, m_ref, v_ref, d_ref, nm_ref, nv_ref):
        d_ref[...], nm_ref[...], nv_ref[...] = _adam_math(w_ref[...], g_ref[...], m_ref[...], v_ref[...])

    blk = pl.BlockSpec((tr, c), lambda i: (i, 0))
    return pl.pallas_call(
        body, grid=(r // tr,), in_specs=[blk] * 4, out_specs=[blk] * 3,
        out_shape=[jax.ShapeDtypeStruct((r, c), F32)] * 3, compiler_params=_cparams(("parallel",)), name=name,
    )(w, g, m, v)


def _adamw_layers(w, g0, g1, m, v, name):
    _, r, c = w.shape
    cap = max(SUBLANE, (1024 * 1024) // (4 * c) // SUBLANE * SUBLANE)
    tr = _rows(r, cap)

    def body(w_ref, g0_ref, g1_ref, m_ref, v_ref, g_ref, d_ref, nm_ref, nv_ref):
        gg = jnp.where(pl.program_id(0) == 0, g0_ref[...], g1_ref[...])
        g_ref[0] = gg
        d_ref[0], nm_ref[0], nv_ref[0] = _adam_math(w_ref[0], gg, m_ref[0], v_ref[0])

    lay = pl.BlockSpec((1, tr, c), lambda l, i: (l, i, 0))

    def gspec(l0):
        return pl.BlockSpec((tr, c), lambda l, i: (jnp.where(l == l0, i, 0), 0))

    return pl.pallas_call(
        body, grid=(2, r // tr), in_specs=[lay, gspec(0), gspec(1), lay, lay], out_specs=[lay] * 4,
        out_shape=[jax.ShapeDtypeStruct((2, r, c), F32)] * 4, compiler_params=_cparams(("arbitrary", "arbitrary")), name=name,
    )(w, g0, g1, m, v)


def _adamw_interleaved(w, g0, g1, m, v, name):
    r, _, c = w.shape
    tr = 128

    def body(w_ref, g0_ref, g1_ref, m_ref, v_ref, g_ref, d_ref, nm_ref, nv_ref):
        for l, gl in enumerate((g0_ref, g1_ref)):
            gg = gl[...]
            g_ref[:, l, :] = gg
            d_ref[:, l, :], nm_ref[:, l, :], nv_ref[:, l, :] = _adam_math(w_ref[:, l, :], gg, m_ref[:, l, :], v_ref[:, l, :])

    lay = pl.BlockSpec((tr, 2, c), lambda i: (i, 0, 0))
    flat = pl.BlockSpec((tr, c), lambda i: (i, 0))
    return pl.pallas_call(
        body, grid=(pl.cdiv(r, tr),), in_specs=[lay, flat, flat, lay, lay], out_specs=[lay] * 4,
        out_shape=[jax.ShapeDtypeStruct((r, 2, c), F32)] * 4, compiler_params=_cparams(("parallel",)), name=name,
    )(w, g0, g1, m, v)


def _add_half(p4, recv, c_idx, name):
    _, r, c = p4.shape
    hw = c // 2
    tr = 256 if r % 256 == 0 else r

    def body(c_ref, a_ref, b_ref, o_ref):
        o_ref[...] = (a_ref[...].astype(F32) + b_ref[...].astype(F32)).astype(BF16)

    return pl.pallas_call(
        body,
        grid_spec=pltpu.PrefetchScalarGridSpec(
            num_scalar_prefetch=1, grid=(N_CHIPS, pl.cdiv(r, tr)),
            in_specs=[pl.BlockSpec((1, tr, hw), lambda s, i, cr: (s, i, cr[0])), pl.BlockSpec((1, tr, hw), lambda s, i, cr: (s, i, 0))],
            out_specs=pl.BlockSpec((1, tr, hw), lambda s, i, cr: (s, i, 0)),
        ),
        out_shape=jax.ShapeDtypeStruct((N_CHIPS, r, hw), BF16), compiler_params=_cparams(("parallel", "parallel")), name=name,
    )(c_idx, p4, recv)


def _sum_slots(x, own, sel, name, out_cols=None):
    s, r, c = x.shape
    tr = 128 if r % 128 == 0 else r

    def body(sel_ref, x_ref, own_ref, o_ref):
        mine = own_ref[0].astype(F32)
        acc = jnp.zeros((tr, c), F32)
        for k in range(s):
            acc = acc + jnp.where(sel_ref[0] == k, mine, x_ref[k].astype(F32))
        o_ref[...] = acc

    return pl.pallas_call(
        body,
        grid_spec=pltpu.PrefetchScalarGridSpec(
            num_scalar_prefetch=1, grid=(pl.cdiv(r, tr),),
            in_specs=[pl.BlockSpec((s, tr, c), lambda i, sr: (0, i, 0)), pl.BlockSpec((1, tr, c), lambda i, sr: (sr[1], i, 0))],
            out_specs=pl.BlockSpec((tr, c), lambda i, sr: (i, sr[2])),
        ),
        out_shape=jax.ShapeDtypeStruct((r, out_cols or c), F32), compiler_params=_cparams(("parallel",)), name=name,
    )(sel, x, own)


def _half_cols(width, hc):
    hw = width // 2
    assert hw % LANE == 0
    return pl.ds(pl.multiple_of(hc * hw, LANE), hw)


def _remote(src, dst, ssem, rsem, k, to):
    return pltpu.make_async_remote_copy(src_ref=src, dst_ref=dst, send_sem=ssem.at[k], recv_sem=rsem.at[k], device_id=to,
                                        device_id_type=MESH)


def _gather_job(bufs, mid_at=0.5):
    def part(o, a, slot, hc):
        return o[a].at[slot, :, _half_cols(bufs[a].shape[2], hc)]

    def first(ins, o, fresh, ssem, rsem):
        x, y, c, chips = _place()
        for a in range(len(bufs)):
            mine = part(o, a, 2 * x + y, c)
            for j, chip in enumerate(chips):
                _remote(mine, mine, ssem, rsem, 6 * a + j, (chip[0], chip[1], c)).start()

    def mid(ins, o, fresh, ssem, rsem):
        x, y, c, chips = _place()
        for a in range(len(bufs)):
            for j, chip in enumerate(chips):
                got = part(o, a, 2 * chip[0] + chip[1], c)
                _remote(got, got, ssem, rsem, 6 * a + j, (x, y, c)).wait_recv()
                _remote(got, got, ssem, rsem, 6 * a + 3 + j, (x, y, 1 - c)).start()

    def last(ins, o, fresh, ssem, rsem):
        x, y, c, chips = _place()
        for a in range(len(bufs)):
            for j, chip in enumerate(chips):
                got = part(o, a, 2 * chip[0] + chip[1], 1 - c)
                _remote(got, got, ssem, rsem, 6 * a + 3 + j, (x, y, c)).wait_recv()
        for a in range(len(bufs)):
            mine = part(o, a, 2 * x + y, c)
            for j, chip in enumerate(chips):
                _remote(mine, mine, ssem, rsem, 6 * a + j, (x, y, c)).wait_send()
                passed = part(o, a, 2 * chip[0] + chip[1], c)
                _remote(passed, passed, ssem, rsem, 6 * a + 3 + j, (x, y, c)).wait_send()

    return _Job([], bufs, [], 6 * len(bufs), first, mid, last, mid_at)


def _swap_job(p4s):
    def pairs(ins, fresh, c):
        return [(a, s, ins[a].at[s, :, _half_cols(p4s[a].shape[2], 1 - c)], fresh[a].at[s])
                for a in range(len(p4s)) for s in range(N_CHIPS)]

    def first(ins, inout, fresh, ssem, rsem):
        x, y, c, _ = _place()
        for a, s, src, dst in pairs(ins, fresh, c):
            _remote(src, dst, ssem, rsem, N_CHIPS * a + s, (x, y, 1 - c)).start()

    def last(ins, inout, fresh, ssem, rsem):
        x, y, c, _ = _place()
        for a, s, src, dst in pairs(ins, fresh, c):
            _remote(src, dst, ssem, rsem, N_CHIPS * a + s, (x, y, 1 - c)).wait()

    fresh = [jax.ShapeDtypeStruct(p.shape[:2] + (p.shape[2] // 2,), p.dtype) for p in p4s]
    return _Job(p4s, [], fresh, N_CHIPS * len(p4s), first, None, last)


def _scatter_job(parts):
    def first(ins, inout, fresh, ssem, rsem):
        x, y, c, chips = _place()
        for a in range(len(parts)):
            for j, chip in enumerate(chips):
                _remote(ins[a].at[2 * chip[0] + chip[1]], fresh[a].at[2 * x + y], ssem, rsem, 3 * a + j, (chip[0], chip[1], c)).start()

    def last(ins, inout, fresh, ssem, rsem):
        x, y, c, chips = _place()
        for a in range(len(parts)):
            for j, chip in enumerate(chips):
                slot = 2 * chip[0] + chip[1]
                _remote(ins[a].at[slot], fresh[a].at[slot], ssem, rsem, 3 * a + j, (x, y, c)).wait()

    return _Job(parts, [], [jax.ShapeDtypeStruct(p.shape, p.dtype) for p in parts], 3 * len(parts), first, None, last)


def _share_job(gs):
    def halves(o, a, c):
        width = gs[a].shape[1]
        return o[a].at[:, _half_cols(width, c)], o[a].at[:, _half_cols(width, 1 - c)]

    def first(ins, o, fresh, ssem, rsem):
        x, y, c, _ = _place()
        for a in range(len(gs)):
            mine, _ = halves(o, a, c)
            _remote(mine, mine, ssem, rsem, a, (x, y, 1 - c)).start()

    def last(ins, o, fresh, ssem, rsem):
        x, y, c, _ = _place()
        for a in range(len(gs)):
            mine, theirs = halves(o, a, c)
            _remote(mine, theirs, ssem, rsem, a, (x, y, 1 - c)).wait()

    return _Job([], gs, [], len(gs), first, None, last)


def _gather_all_job(buf):
    def peers():
        x, y, c, _ = _place()
        flips = [(fx, fy, fc) for fx in (0, 1) for fy in (0, 1) for fc in (0, 1)][1:]
        return (x, y, c), [((1 - x) if fx else x, (1 - y) if fy else y, (1 - c) if fc else c) for fx, fy, fc in flips]

    def first(ins, inout, fresh, ssem, rsem):
        (x, y, c), others = peers()
        for k, peer in enumerate(others):
            _remote(ins[0], fresh[0].at[4 * x + 2 * y + c], ssem, rsem, k, peer).start()

    def last(ins, inout, fresh, ssem, rsem):
        me, others = peers()
        for k, peer in enumerate(others):
            _remote(ins[0], fresh[0].at[4 * peer[0] + 2 * peer[1] + peer[2]], ssem, rsem, k, me).wait()

    return _Job([buf], [], [jax.ShapeDtypeStruct((N_DEV,) + buf.shape, buf.dtype)], N_DEV - 1, first, None, last)


class _SemView:
    def __init__(self, sems, off):
        self.sems, self.off = sems, off

    @property
    def at(self):
        return self

    def __getitem__(self, k):
        return self.sems.at[k + self.off]


def _join(jobs):
    spans, pos = [], [0, 0, 0, 0]
    for j in jobs:
        nxt = [pos[0] + len(j.ins), pos[1] + len(j.inout), pos[2] + len(j.fresh), pos[3] + j.nsem]
        spans.append((pos, nxt))
        pos = nxt

    def hook(which):
        fns = [getattr(j, which) for j in jobs]
        if all(f is None for f in fns):
            return None

        def run(ins, inout, fresh, ssem, rsem):
            for fn, (lo, hi) in zip(fns, spans):
                if fn is not None:
                    fn(ins[lo[0]:hi[0]], inout[lo[1]:hi[1]], fresh[lo[2]:hi[2]], _SemView(ssem, lo[3]), _SemView(rsem, lo[3]))

        return run

    mids = [j.mid_at for j in jobs if j.mid is not None]
    joined = _Job([a for j in jobs for a in j.ins], [a for j in jobs for a in j.inout], [a for j in jobs for a in j.fresh],
                  pos[3], hook("first"), hook("mid"), hook("last"), max(mids) if mids else 0.5)
    n_io = pos[1]

    def split(moved):
        return [list(moved[lo[1]:hi[1]]) + list(moved[n_io + lo[2]:n_io + hi[2]]) for lo, hi in spans]

    return joined, split


def _carrying(stages, call):
    stages = [s for s in stages if s is not None]
    if not stages:
        return call(None)
    job, split = _join([s[0] for s in stages])
    out, moved = call(job)
    for (_, done), part in zip(stages, split(moved)):
        done(part)
    return out


def _layer_forward(x, h, w_in_t, rest, sm, g_next, d, stages=None):
    stages = stages or {}
    proj = _carrying([stages.get("proj")], lambda job: _matmul(h, w_in_t, "nt", BF16, "proj_fwd", n=7 * d, tn_cap=1792, job=job))
    f_t = _matmul(w_in_t[7 * d:], h, "nt", F32, "forget_fwd", tn_cap=1024)
    c_t = _fox_prep(f_t, sm["b_f"], "fox_prep")
    o, lse = _carrying([stages.get("attn")], lambda job: _attn_fwd(proj, c_t, d, "attn_fwd", job=job))
    wts = rest()
    merged = _mix_fwd(proj, o, sm["wm"], sm["bs"], sm["g_v"], d, "mix_fwd")
    z, x1, h2 = _matmul(merged, wts["w_out"], "nn", F32, "out_fwd", norms=(x, sm["g_post"], sm["g_fpre"]))
    a = _carrying([stages.get("gate")], lambda job: _matmul(h2, wts["w_g_t"], "nt", BF16, "gate_fwd", tn_cap=1408, job=job))
    b = _carrying([stages.get("up")], lambda job: _matmul(h2, wts["w_u_t"], "nt", BF16, "up_fwd", tn_cap=1408, job=job))
    z2, x_out, h_out, mm = _carrying([stages.get("down")], lambda job: _matmul(
        b, wts["w_d"], "nn", F32, "down_fwd", norms=(x1, sm["g_fpost"], g_next), silu_of=a, job=job))
    return dict(x=x, h=h, proj=proj, f_t=f_t, c_t=c_t, o=o, lse=lse, merged=merged, z=z, x1=x1,
                h2=h2, a=a, b=b, mm=mm, z2=z2, x_out=x_out, h_out=h_out)


class _GradExchange:
    def __init__(self, pay, keys, c_idx, chip):
        self.keys = list(keys)
        self.p4 = [pay[k].reshape(N_CHIPS, pay[k].shape[1] // N_CHIPS, pay[k].shape[2]) for k in self.keys]
        self.c_idx = c_idx
        self.sel = jnp.stack([chip, chip, c_idx[0]]).astype(jnp.int32)
        self.done = 0

    def _after_swap(self, landed):
        self.parts = [_add_half(p, r, self.c_idx, "add_sibling") for p, r in zip(self.p4, landed)]
        self.done = 1

    def _after_scatter(self, landed):
        self.g = [_sum_slots(got, sent, self.sel, "sum_chips", out_cols=p.shape[2])
                  for got, sent, p in zip(landed, self.parts, self.p4)]
        self.done = 2

    def _after_share(self, moved):
        self.g = list(moved)
        self.done = 3

    def stage(self):
        if self.done == 0:
            return _swap_job(self.p4), self._after_swap
        if self.done == 1:
            return _scatter_job(self.parts), self._after_scatter
        if self.done == 2:
            return _share_job(self.g), self._after_share
        return None

    def run(self):
        for name in ("swap_grads", "scatter_grads", "share_grads")[self.done:]:
            job, done = self.stage()
            done(_run_job(job, name))

    def grads(self):
        return dict(zip(self.keys, self.g))


EARLY_KEYS = ("w_d", "w_g", "w_u", "w_out")


def _layer_backward(dz2, dx2, sv, wts, sm, d, c_idx, chip, carried=(), split_own=False, small_stage=None):
    t = dx2.shape[0]
    heads = d // LANE
    ff = wts["w_d"].shape[0]
    in_w = 7 * d + heads
    g, pay = {}, {}
    carried = list(carried)

    def payload(key, a, b, rows, row0, name, extra=()):
        def call(job):
            return _matmul(a, b, "tn", BF16, name, slab=((1, rows, d), 0, row0), into=pay.get(key), job=job, tm_cap=1408,
                           tn_cap=1024, tk_cap=1024)
        pay[key] = _carrying(list(extra), call)

    def nxt(*exchanges):
        return [ex.stage() for ex in exchanges]

    dm = _carrying(nxt(*carried), lambda job: _matmul(dz2, wts["w_d"], "nt", BF16, "down_bwd_x", tn_cap=1408, tk_cap=1024, job=job))
    payload("w_d", sv["mm"], dz2, ff, 0, "down_bwd_w")
    da, db = _swiglu_bwd(sv["a"], sv["b"], dm, "swiglu_bwd")
    dh2 = _matmul_pieces([(da, wts["w_g_t"], 0), (db, wts["w_u_t"], 0)], None, "gu_bwd_x", tk=_tile(ff, 1408))
    payload("w_g", da, sv["h2"], ff, 0, "gate_bwd_w")
    payload("w_u", db, sv["h2"], ff, 0, "up_bwd_w")
    dx1, dz, g["g_fpre"], g["g_post"] = _norm_bwd(dx2, (dh2, sv["x1"], sm["g_fpre"]), (sv["z"], sm["g_post"]), "norm_bwd_mid")
    dmerged = _matmul(dz, wts["w_out"], "nt", F32, "out_bwd_x", tk_cap=1024)
    payload("w_out", sv["merged"], dz, d, 0, "out_bwd_w")
    early = [_GradExchange(pay, EARLY_KEYS, c_idx, chip)] if split_own else []
    d_uv, d_g, do, g["w_s"], g["b_s"], g["g_v"] = _carrying(nxt(*early), lambda job: _mix_bwd(
        dmerged, sv["proj"], sv["o"], sm["wm"], sm["wm_t"], sm["bs"], sm["g_v"], d, "mix_bwd", job=job))
    extra = [small_stage(g)] if small_stage is not None else []
    attn_args = (sv["proj"], do, sv["o"], sv["lse"], sv["c_t"], d)
    dq, dk, dv, dc_q, dc_k = _carrying(nxt(*carried) + extra, lambda job: _attn_bwd(*attn_args, "attn_bwd", job=job))
    df_t, g["b_f"] = _fox_bwd(dc_q, dc_k, sv["f_t"], sm["b_f"], "fox_bwd")
    df_b = df_t.astype(BF16)
    pieces = [(d_uv, COL_U), (dq, COL_Q), (dk, COL_K), (dv, COL_VA), (d_g, COL_GA)]
    pay["w_in"] = _carrying(nxt(*carried, *early), lambda job: _matmul_rows([p for p, _ in pieces], sv["h"], in_w,
                                                                           "proj_bwd_w", job=job))
    w_f_rows = _carrying(nxt(*early), lambda job: _matmul(df_b, sv["h"], "nn", BF16, "forget_bwd_w", tk_cap=1024, job=job))
    pay["w_in"] = lax.dynamic_update_slice(pay["w_in"], w_f_rows[None], (0, 7 * d, 0))
    late = _GradExchange(pay, [k for k in ("w_in",) + EARLY_KEYS if not (split_own and k in EARLY_KEYS)], c_idx, chip)
    mine = [late] if split_own else []
    dh_f = _carrying(nxt(*mine), lambda job: _matmul(df_b, wts["w_in_t"][7 * d:], "tn", F32, "forget_bwd_x", job=job))
    ops = [(p, wts["w_in_t"], col * d) for p, col in pieces]
    dh = _carrying(nxt(*mine), lambda job: _matmul_pieces(ops, dh_f, "proj_bwd_x", job=job, tk=_tile(d, 1024)))
    return dh, dx1, g, early + [late]


def _small_pack(parts):
    flat = jnp.concatenate([p.reshape(-1) for p in parts])
    n = flat.shape[0]
    pad = (-n) % (LANE * LANE)
    return jnp.pad(flat, (0, pad)).reshape(-1, LANE)


def kernel(x, mix_pre_g, w_in, b_forget, sgu_norm_g, w_spatial, b_spatial, w_out, mix_post_g, ffn_pre_g, w_gate, w_up, w_down, ffn_post_g, loss_target, m_mix_pre_g, m_w_in, m_b_forget, m_sgu_norm_g, m_w_spatial, m_b_spatial, m_w_out, m_mix_post_g, m_ffn_pre_g, m_w_gate, m_w_up, m_w_down, m_ffn_post_g, v_mix_pre_g, v_w_in, v_b_forget, v_sgu_norm_g, v_w_spatial, v_b_spatial, v_w_out, v_mix_post_g, v_ffn_pre_g, v_w_gate, v_w_up, v_w_down, v_ffn_post_g):
    depth, d = mix_pre_g.shape
    assert depth == 2, "the AdamW kernels and the exchange schedule are written for two blocks"
    heads = d // LANE
    t = x.shape[1]
    ff = w_down.shape[1] * N_CHIPS
    in_w = w_in.shape[2] * N_CHIPS
    assert in_w == 7 * d + heads
    xs = x.reshape(t, d)
    target = loss_target.reshape(t, d)
    c_idx = lax.axis_index("c").astype(jnp.int32).reshape(1)
    chip = 2 * lax.axis_index("x") + lax.axis_index("y")
    dev = 2 * chip + lax.axis_index("c")

    def in_view(w):
        return jnp.transpose(w, (2, 0, 1))

    def gu_view(w):
        return jnp.transpose(w, (0, 2, 1))

    own = [jnp.transpose(in_view(w_in).astype(BF16), (1, 0, 2)), w_out.astype(BF16), gu_view(w_gate).astype(BF16),
           gu_view(w_up).astype(BF16), w_down.astype(BF16)]
    bufs = [[lax.dynamic_update_slice(lax.empty((N_CHIPS,) + o.shape[1:], BF16), o[l][None], (chip, 0, 0)) for o in own]
            for l in range(depth)]
    first_in = _run_job(_gather_job([bufs[0][0]]), "gather_first")[0]

    def weights(g_in, g_out, g_g, g_u, g_d):
        return dict(w_in_t=g_in.reshape(in_w, d), w_out=g_out.reshape(d, d), w_g_t=g_g.reshape(ff, d),
                    w_u_t=g_u.reshape(ff, d), w_d=g_d.reshape(ff, d))

    tril = jnp.tril(jnp.ones((LANE, LANE), bool))
    smalls = []
    for l in range(depth):
        wm = jnp.where(tril[None], w_spatial[l], 0.0).astype(BF16)
        smalls.append(dict(
            b_f=b_forget[l].reshape(heads, 1), wm=wm, wm_t=jnp.swapaxes(wm, 1, 2), bs=b_spatial[l].reshape(heads, LANE, 1),
            g_v=sgu_norm_g[l].reshape(1, d), g_pre=mix_pre_g[l].reshape(1, d), g_post=mix_post_g[l].reshape(1, d),
            g_fpre=ffn_pre_g[l].reshape(1, d), g_fpost=ffn_post_g[l].reshape(1, d)))

    wts, later = [], {}

    def keep(key):
        def done(moved):
            later[key] = list(moved)
        return done

    def rest_first():
        wts.append(weights(first_in, *later["rest0"], later["attn"][0]))
        return wts[0]

    stages = dict(proj=(_gather_job(bufs[0][1:4], mid_at=1.0), keep("rest0")),
                  attn=(_gather_job(bufs[0][4:5] + bufs[1][0:2], mid_at=0.7), keep("attn")),
                  gate=(_gather_job(bufs[1][2:3], mid_at=1.0), keep("g1")), up=(_gather_job(bufs[1][3:4], mid_at=1.0), keep("u1")),
                  down=(_gather_job(bufs[1][4:5], mid_at=1.0), keep("d1")))
    h = _norm_fwd(xs, None, None, smalls[0]["g_pre"], "norm_first")
    g_after = [smalls[min(l + 1, depth - 1)]["g_pre"] for l in range(depth)]
    saved = [_layer_forward(xs, h, first_in.reshape(in_w, d), rest_first, smalls[0], g_after[0], d, stages)]
    wts.append(weights(*later["attn"][1:3], later["g1"][0], later["u1"][0], later["d1"][0]))
    for l in range(1, depth):
        saved.append(_layer_forward(saved[l - 1]["x_out"], saved[l - 1]["h_out"], wts[l]["w_in_t"], lambda l=l: wts[l], smalls[l],
                                    g_after[l], d))
    dy, loss_part = _loss_grad(saved[-1]["x_out"], target, "loss")
    loss = lax.psum(jnp.sum(loss_part), ("x", "y", "c"))

    small_shapes = dict(g_pre=(d,), b_f=(heads,), g_v=(d,), w_s=w_spatial.shape[1:], b_s=b_spatial.shape[1:], g_post=(d,),
                        g_fpre=(d,), g_fpost=(d,))
    late_entries = [(0, "g_pre"), (0, "b_f")]
    early_entries = [(l, n) for l in reversed(range(depth)) for n in small_shapes if (l, n) not in late_entries]
    dev_sel = jnp.stack([dev, jnp.zeros_like(dev), jnp.zeros_like(dev)]).astype(jnp.int32)
    small_sum = {}

    def small_exchange(entries, values):
        packed = _small_pack([values[e].reshape(-1) for e in entries])

        def done(moved):
            total = _sum_slots(moved[0], packed[None], dev_sel, "sum_small").reshape(-1)
            off = 0
            for e in entries:
                n = math.prod(small_shapes[e[1]])
                small_sum[e] = total[off:off + n].reshape(small_shapes[e[1]])
                off += n

        return _gather_all_job(packed), done

    grads = [None] * depth
    exchanges = [None] * depth
    dx2 = dy
    dz2, g_fpost = _norm_bwd(dx2, None, (saved[depth - 1]["z2"], smalls[depth - 1]["g_fpost"]), "norm_bwd_top")
    for l in reversed(range(depth)):
        last = l == 0

        def small_stage(g, l=l, g_fpost=g_fpost):
            known = {(k, n): grads[k][n] for k in range(l + 1, depth) for n in small_shapes}
            known.update({(l, n): g[n] for n in g})
            known[(l, "g_fpost")] = g_fpost
            return small_exchange(early_entries, known)

        carried = [ex for k in range(l + 1, depth) for ex in exchanges[k]]
        dh, dx1, g, exchanges[l] = _layer_backward(dz2, dx2, saved[l], wts[l], smalls[l], d, c_idx, chip, carried=carried,
                                                    split_own=last, small_stage=small_stage if last else None)
        g["g_fpost"] = g_fpost
        if l > 0:
            dx2, dz2, g["g_pre"], g_fpost = _norm_bwd(dx1, (dh, saved[l]["x"], smalls[l]["g_pre"]),
                                                       (saved[l - 1]["z2"], smalls[l - 1]["g_fpost"]), "norm_bwd_between")
        else:
            grad_x, g["g_pre"] = _norm_bwd(dx1, (dh, saved[l]["x"], smalls[l]["g_pre"]), None, "norm_bwd_bottom")
        grads[l] = g
    job, done = small_exchange(late_entries, {(0, n): grads[0][n] for n in ("g_pre", "b_f")})
    done(_run_job(job, "gather_small"))
    big = [{} for _ in range(depth)]
    for l in range(depth):
        for ex in exchanges[l]:
            ex.run()
            big[l].update(ex.grads())
    small_grads = {n: jnp.stack([small_sum[(l, n)] for l in range(depth)]) for n in small_shapes}

    def adam_small(w, g, m, v):
        shp = w.shape
        if w.ndim >= 3 and shp[-1] >= LANE:
            two = (math.prod(shp[:-1]), shp[-1])
        else:
            two = (1, math.prod(shp)) if math.prod(shp) < LANE else (math.prod(shp) // LANE, LANE)
        outs = _adamw(w.reshape(two), g.reshape(two), m.reshape(two), v.reshape(two), "adamw")
        return [g] + [o.reshape(shp) for o in outs]

    def adam_in(w, m, v):
        outs = _adamw_interleaved(in_view(w), big[0]["w_in"], big[1]["w_in"], in_view(m), in_view(v), "adamw_in")
        return [jnp.transpose(o, (1, 2, 0)) for o in outs]

    def adam_gu(k, w, m, v):
        outs = _adamw_layers(gu_view(w), big[0][k], big[1][k], gu_view(m), gu_view(v), "adamw_layers")
        return [jnp.transpose(o, (0, 2, 1)) for o in outs]

    def adam_rows(k, w, m, v):
        return _adamw_layers(w, big[0][k], big[1][k], m, v, "adamw_layers")

    results = [
        adam_small(mix_pre_g, small_grads["g_pre"], m_mix_pre_g, v_mix_pre_g),
        adam_in(w_in, m_w_in, v_w_in),
        adam_small(b_forget, small_grads["b_f"], m_b_forget, v_b_forget),
        adam_small(sgu_norm_g, small_grads["g_v"], m_sgu_norm_g, v_sgu_norm_g),
        adam_small(w_spatial, small_grads["w_s"], m_w_spatial, v_w_spatial),
        adam_small(b_spatial, small_grads["b_s"], m_b_spatial, v_b_spatial),
        adam_rows("w_out", w_out, m_w_out, v_w_out),
        adam_small(mix_post_g, small_grads["g_post"], m_mix_post_g, v_mix_post_g),
        adam_small(ffn_pre_g, small_grads["g_fpre"], m_ffn_pre_g, v_ffn_pre_g),
        adam_gu("w_g", w_gate, m_w_gate, v_w_gate),
        adam_gu("w_u", w_up, m_w_up, v_w_up),
        adam_rows("w_d", w_down, m_w_down, v_w_down),
        adam_small(ffn_post_g, small_grads["g_fpost"], m_ffn_post_g, v_ffn_post_g),
    ]
    gs, deltas, new_ms, new_vs = zip(*results)
    return (loss, grad_x.reshape(x.shape), *gs, *deltas, *new_ms, *new_vs)
```
